```python
import math
import jax
import jax.numpy as jnp
from jax import lax
import numpy as np

D_MODEL = 4096
BATCH = 4
SEQ = 2048
DEPTH = 1

CTX_LEN = 256
GRID_W = 64
EPS = 1e-6

S5_WIDTH = D_MODEL // 2
S5_GROUP = 16
S5_GROUPS = S5_WIDTH // S5_GROUP
S5_STATE = 64

SSD_WIDTH = D_MODEL
SSD_HEAD_DIM = 64
SSD_HEADS = SSD_WIDTH // SSD_HEAD_DIM
SSD_GROUPS = 8
SSD_HPG = SSD_HEADS // SSD_GROUPS
SSD_STATE = 128
SSD_CONV = 5
SSD_CHUNK = 128
SSD_CONV_DIM = SSD_WIDTH + 2 * SSD_GROUPS * SSD_STATE
SSD_IN = SSD_WIDTH + SSD_CONV_DIM + 2 * SSD_HEADS

IN_COLS = S5_WIDTH + SSD_IN + 2 * D_MODEL

MOE_GROUPS = 4
MOE_PER_GROUP = 8
MOE_EXPERTS = MOE_GROUPS * MOE_PER_GROUP
MOE_TOP_K = 2
MOE_HIDDEN = D_MODEL // 4
MOE_BLOCK = 256

kernel_name = 'hybrid_s5_ssd_hmoe_dit_block'


def _rms(t):
    tf = t.astype(jnp.float32)
    return tf * lax.rsqrt(jnp.mean(tf * tf, axis=-1, keepdims=True) + EPS)


def rmsnorm(t, g):
    return (_rms(t) * g.astype(jnp.float32)).astype(t.dtype)


def _to_col_major(t, rows):
    b, n, ch = t.shape
    return t.reshape(b, rows, GRID_W, ch).swapaxes(1, 2).reshape(b, n, ch)


def _to_row_major(t, rows):
    b, n, ch = t.shape
    return t.reshape(b, GRID_W, rows, ch).swapaxes(1, 2).reshape(b, n, ch)


def _dwconv(t, w, bias):
    out = lax.conv_general_dilated(
        t, w[:, None, :], window_strides=(1,), padding=[(SSD_CONV // 2, SSD_CONV // 2)],
        dimension_numbers=('NWC', 'WIO', 'NWC'), feature_group_count=t.shape[-1])
    return out + bias


def _linear_recurrence(e1, e2):
    a1, b1 = e1
    a2, b2 = e2
    return a1 * a2, a2 * b1 + b2


def _s5_scan(u, lam_re, lam_im, log_dt, b_re, b_im, h0, reverse):
    lam = lax.complex(lam_re.astype(jnp.float32), lam_im.astype(jnp.float32))
    lam_bar = jnp.exp(lam * jnp.exp(log_dt.astype(jnp.float32))[:, None])
    b_bar = ((lam_bar - 1.0) / lam)[..., None] * lax.complex(
        b_re.astype(jnp.float32), b_im.astype(jnp.float32))
    bu = jnp.einsum('gps,blgs->blgp', b_bar, u.astype(jnp.complex64))
    first, last = (-1, 0) if reverse else (0, -1)
    bu = bu.at[:, first].add(lam_bar * h0)
    a = jnp.broadcast_to(lam_bar, (1, u.shape[1]) + lam_bar.shape)
    _, h = lax.associative_scan(_linear_recurrence, (a, bu), reverse=reverse, axis=1)
    return h, h[:, last]


def s5_mixer(u, lp, init):
    bsz, n = u.shape[:2]
    uf = u.astype(jnp.float32).reshape(bsz, n, S5_GROUPS, S5_GROUP)
    y = lp['s5_d'].astype(jnp.float32).reshape(S5_GROUPS, S5_GROUP) * uf
    finals = []
    for d, reverse in enumerate((False, True)):
        h, h_fin = _s5_scan(uf, lp['s5_lam_re'][d], lp['s5_lam_im'][d], lp['s5_log_dt'][d],
                            lp['s5_b_re'][d], lp['s5_b_im'][d], init[d], reverse)
        cmat = lax.complex(lp['s5_c_re'][d].astype(jnp.float32), lp['s5_c_im'][d].astype(jnp.float32))
        y = y + jnp.einsum('gsp,blgp->blgs', cmat, h).real
        finals.append(h_fin)
    return y.reshape(bsz, n, S5_WIDTH).astype(u.dtype), (finals[0], finals[1])


def _ssd_chunked(xs, dt, a, bm, cm, h0):
    bsz, n = xs.shape[:2]
    nc = n // SSD_CHUNK
    xs = xs.reshape(bsz, nc, SSD_CHUNK, SSD_GROUPS, SSD_HPG, SSD_HEAD_DIM)
    dt = dt.reshape(bsz, nc, SSD_CHUNK, SSD_GROUPS, SSD_HPG)
    bm = bm.reshape(bsz, nc, SSD_CHUNK, SSD_GROUPS, SSD_STATE)
    cm = cm.reshape(bsz, nc, SSD_CHUNK, SSD_GROUPS, SSD_STATE)
    acum = jnp.cumsum(jnp.moveaxis(dt * a, 2, -1), axis=-1)
    xdt = xs * dt[..., None]
    lower = jnp.tril(jnp.ones((SSD_CHUNK, SSD_CHUNK), dtype=bool))
    seg = jnp.exp(jnp.where(lower, acum[..., :, None] - acum[..., None, :], -jnp.inf))
    cb = jnp.einsum('bcign,bcjgn->bcgij', cm, bm)
    y_diag = jnp.einsum('bcgrij,bcjgrp->bcigrp', cb[:, :, :, None] * seg, xdt)
    to_end = jnp.moveaxis(jnp.exp(acum[..., -1:] - acum), -1, 2)
    states = jnp.einsum('bcjgn,bcjgrp->bcgrpn', bm, xdt * to_end[..., None])
    chunk_decay = jnp.exp(acum[..., -1])

    def step(h, inp):
        s, d = inp
        return h * d[..., None, None] + s, h

    h_last, h_in = lax.scan(step, h0, (jnp.moveaxis(states, 1, 0), jnp.moveaxis(chunk_decay, 1, 0)))
    h_in = jnp.moveaxis(h_in, 0, 1)
    y_off = jnp.einsum('bcign,bcgrpn->bcigrp', cm, h_in) * jnp.moveaxis(jnp.exp(acum), -1, 2)[..., None]
    return (y_diag + y_off).reshape(bsz, n, SSD_GROUPS, SSD_HPG, SSD_HEAD_DIM), h_last


def ssd_mixer(u, lp, init):
    bsz, n = u.shape[:2]
    f32 = jnp.float32
    z = u[..., :SSD_WIDTH]
    xbc = jax.nn.silu(_dwconv(u[..., SSD_WIDTH:SSD_WIDTH + SSD_CONV_DIM], lp['ssd_conv_w'], lp['ssd_conv_b']))
    dt_raw = u[..., SSD_WIDTH + SSD_CONV_DIM:].astype(f32).reshape(bsz, n, 2, SSD_GROUPS, SSD_HPG)
    gn = SSD_GROUPS * SSD_STATE
    xs = xbc[..., :SSD_WIDTH].astype(f32).reshape(bsz, n, SSD_GROUPS, SSD_HPG, SSD_HEAD_DIM)
    bm = xbc[..., SSD_WIDTH:SSD_WIDTH + gn].astype(f32).reshape(bsz, n, SSD_GROUPS, SSD_STATE)
    cm = xbc[..., SSD_WIDTH + gn:].astype(f32).reshape(bsz, n, SSD_GROUPS, SSD_STATE)
    dt = jax.nn.softplus(dt_raw + lp['ssd_dt_bias'].astype(f32).reshape(2, SSD_GROUPS, SSD_HPG))
    a = -jnp.exp(lp['ssd_a_log'].astype(f32)).reshape(2, SSD_GROUPS, SSD_HPG)
    flip = lambda t: jnp.flip(t, axis=1)
    y_f, h_f = _ssd_chunked(xs, dt[:, :, 0], a[0], bm, cm, init[0])
    y_b, h_b = _ssd_chunked(flip(xs), flip(dt[:, :, 1]), a[1], flip(bm), flip(cm), init[1])
    y = lp['ssd_d'].astype(f32).reshape(SSD_GROUPS, SSD_HPG)[..., None] * xs + y_f + flip(y_b)
    y = y.reshape(bsz, n, SSD_WIDTH) * jax.nn.silu(z.astype(f32))
    y = _rms(y.reshape(bsz, n, SSD_GROUPS, -1)).reshape(bsz, n, SSD_WIDTH) * lp['ssd_norm_g'].astype(f32)
    return y.astype(u.dtype), (h_f, h_b)


def token_mixers(hn, lp, s5_init, ssd_init, grid_rows):
    proj = hn @ lp['w_in']
    o1 = S5_WIDTH
    o2 = o1 + SSD_IN
    u_a = proj[..., :o1]
    u_b = proj[..., o1:o2]
    gate_a = proj[..., o2:o2 + D_MODEL]
    gate_b = proj[..., o2 + D_MODEL:]
    y_a, s5_fin = s5_mixer(u_a, lp, s5_init)
    if grid_rows is not None:
        u_b = _to_col_major(u_b, grid_rows)
    y_b, ssd_fin = ssd_mixer(u_b, lp, ssd_init)
    if grid_rows is not None:
        y_b = _to_row_major(y_b, grid_rows)
    g = jax.nn.gelu(y_a)
    br_a = (g @ lp['s5_w_val']) * jax.nn.sigmoid(g @ lp['s5_w_gate'])
    br_b = y_b @ lp['ssd_w_out']
    merged = jax.nn.sigmoid(gate_a) * br_a + jax.nn.sigmoid(gate_b) * br_b
    return merged @ lp['w_o'], s5_fin, ssd_fin


def hier_moe(h, w_group, b_group, w_expert, b_expert, w_gate, w_up, w_down):
    m, d_model = h.shape
    f32 = jnp.float32
    hf = h.astype(f32)
    g_prob = jax.nn.softmax(hf @ w_group.astype(f32) + b_group.astype(f32), axis=-1)
    g_p, g_idx = lax.top_k(g_prob, 1)
    e_logits = (hf @ w_expert.astype(f32) + b_expert.astype(f32)).reshape(m, MOE_GROUPS, MOE_PER_GROUP)
    e_logits = jnp.take_along_axis(e_logits, g_idx[:, :, None], axis=1)[:, 0]
    e_top, e_idx = lax.top_k(e_logits, MOE_TOP_K)
    weights = g_p * jax.nn.softmax(e_top, axis=-1)
    experts = g_idx * MOE_PER_GROUP + e_idx
    n_assign = m * MOE_TOP_K
    flat_e = experts.reshape(-1)
    order = jnp.argsort(flat_e)
    sorted_e = flat_e[order]
    counts = jnp.zeros((MOE_EXPERTS,), jnp.int32).at[flat_e].add(1)
    padded = (counts + MOE_BLOCK - 1) // MOE_BLOCK * MOE_BLOCK
    pad_end = jnp.cumsum(padded)
    pad_start = pad_end - padded
    start = jnp.cumsum(counts) - counts
    dest = pad_start[sorted_e] + jnp.arange(n_assign, dtype=jnp.int32) - start[sorted_e]
    n_blocks = -(-(n_assign + MOE_EXPERTS * (MOE_BLOCK - 1)) // MOE_BLOCK)
    tok_of_sorted = (order // MOE_TOP_K).astype(jnp.int32)
    slot_tok = jnp.full((n_blocks * MOE_BLOCK,), m, jnp.int32).at[dest].set(tok_of_sorted)
    block_e = jnp.minimum(
        jnp.searchsorted(pad_end, jnp.arange(n_blocks, dtype=jnp.int32) * MOE_BLOCK, side='right'),
        MOE_EXPERTS - 1)
    h_pad = jnp.concatenate([h, jnp.zeros((1, d_model), h.dtype)], axis=0)

    def expert_block(args):
        tok, e = args
        xb = h_pad[tok]
        return (jax.nn.silu(xb @ w_gate[e]) * (xb @ w_up[e])) @ w_down[e]

    y_slots = lax.map(expert_block, (slot_tok.reshape(n_blocks, MOE_BLOCK), block_e)).reshape(-1, d_model)
    w_sorted = weights.reshape(-1)[order].astype(y_slots.dtype)
    return jax.ops.segment_sum(y_slots[dest] * w_sorted[:, None], tok_of_sorted, num_segments=m)


def setup_inputs(seed: int = 0) -> dict:
    key = jax.random.key(seed)
    ks = iter(jax.random.split(key, 40))
    f32 = jnp.float32

    def nrm(shape, scale):
        return scale * jax.random.normal(next(ks), shape, f32)

    def unif(shape, lo, hi):
        return jax.random.uniform(next(ks), shape, f32, minval=lo, maxval=hi)

    L, D = DEPTH, D_MODEL
    dt_lo, dt_hi = math.log(1e-3), math.log(1e-1)
    ssd_dt = jnp.exp(unif((L, 2, SSD_HEADS), dt_lo, dt_hi))
    return {
        'x': nrm((BATCH, SEQ, D), 1.0),
        'c': nrm((BATCH, D), 1.0),
        'ctx': nrm((BATCH, CTX_LEN, D), 1.0),
        'c_ctx': nrm((D,), 1.0),
        'w_mod': nrm((L, D, 6 * D), 0.5 * D ** -0.5),
        'b_mod': nrm((L, 6 * D), 0.01),
        'norm1_g': 1.0 + nrm((L, D), 0.02),
        'w_in': nrm((L, D, IN_COLS), D ** -0.5),
        's5_lam_re': -0.5 + nrm((L, 2, S5_GROUPS, S5_STATE), 0.01),
        's5_lam_im': math.pi * jnp.arange(S5_STATE, dtype=f32) + nrm((L, 2, S5_GROUPS, S5_STATE), 0.01),
        's5_log_dt': unif((L, 2, S5_GROUPS), dt_lo, dt_hi),
        's5_b_re': nrm((L, 2, S5_GROUPS, S5_STATE, S5_GROUP), (2 * S5_GROUP) ** -0.5),
        's5_b_im': nrm((L, 2, S5_GROUPS, S5_STATE, S5_GROUP), (2 * S5_GROUP) ** -0.5),
        's5_c_re': nrm((L, 2, S5_GROUPS, S5_GROUP, S5_STATE), (2 * S5_STATE) ** -0.5),
        's5_c_im': nrm((L, 2, S5_GROUPS, S5_GROUP, S5_STATE), (2 * S5_STATE) ** -0.5),
        's5_d': nrm((L, S5_WIDTH), 1.0),
        's5_w_val': nrm((L, S5_WIDTH, D), S5_WIDTH ** -0.5),
        's5_w_gate': nrm((L, S5_WIDTH, D), S5_WIDTH ** -0.5),
        'ssd_conv_w': nrm((L, SSD_CONV, SSD_CONV_DIM), SSD_CONV ** -0.5),
        'ssd_conv_b': nrm((L, SSD_CONV_DIM), 0.01),
        'ssd_a_log': jnp.log(unif((L, 2, SSD_HEADS), 1.0, 16.0)),
        'ssd_dt_bias': ssd_dt + jnp.log(-jnp.expm1(-ssd_dt)),
        'ssd_d': 1.0 + nrm((L, SSD_HEADS), 0.1),
        'ssd_norm_g': 1.0 + nrm((L, SSD_WIDTH), 0.02),
        'ssd_w_out': nrm((L, SSD_WIDTH, D), SSD_WIDTH ** -0.5),
        'w_o': nrm((L, D, D), D ** -0.5),
        'norm2_g': 1.0 + nrm((L, D), 0.02),
        'moe_w_group': nrm((L, D, MOE_GROUPS), D ** -0.5),
        'moe_b_group': nrm((L, MOE_GROUPS), 0.01),
        'moe_w_expert': nrm((L, D, MOE_EXPERTS), D ** -0.5),
        'moe_b_expert': nrm((L, MOE_EXPERTS), 0.01),
        'moe_w_gate': nrm((L, MOE_EXPERTS, D, MOE_HIDDEN), D ** -0.5),
        'moe_w_up': nrm((L, MOE_EXPERTS, D, MOE_HIDDEN), D ** -0.5),
        'moe_w_down': nrm((L, MOE_EXPERTS, MOE_HIDDEN, D), MOE_HIDDEN ** -0.5),
        'final_g': 1.0 + nrm((D,), 0.02),
    }


def reference(x, c, ctx, c_ctx, w_mod, b_mod, norm1_g, w_in,
              s5_lam_re, s5_lam_im, s5_log_dt, s5_b_re, s5_b_im, s5_c_re, s5_c_im, s5_d,
              s5_w_val, s5_w_gate,
              ssd_conv_w, ssd_conv_b, ssd_a_log, ssd_dt_bias, ssd_d, ssd_norm_g, ssd_w_out,
              w_o, norm2_g, moe_w_group, moe_b_group, moe_w_expert, moe_b_expert,
              moe_w_gate, moe_w_up, moe_w_down, final_g):
    bsz, n_lat, _ = x.shape
    rows = n_lat // GRID_W
    for l in range(DEPTH):
        lp = {
            'w_in': w_in[l], 'w_o': w_o[l],
            's5_lam_re': s5_lam_re[l], 's5_lam_im': s5_lam_im[l], 's5_log_dt': s5_log_dt[l],
            's5_b_re': s5_b_re[l], 's5_b_im': s5_b_im[l], 's5_c_re': s5_c_re[l], 's5_c_im': s5_c_im[l],
            's5_d': s5_d[l], 's5_w_val': s5_w_val[l], 's5_w_gate': s5_w_gate[l],
            'ssd_conv_w': ssd_conv_w[l], 'ssd_conv_b': ssd_conv_b[l], 'ssd_a_log': ssd_a_log[l],
            'ssd_dt_bias': ssd_dt_bias[l], 'ssd_d': ssd_d[l], 'ssd_norm_g': ssd_norm_g[l],
            'ssd_w_out': ssd_w_out[l],
        }
        moe_args = (moe_w_group[l], moe_b_group[l], moe_w_expert[l], moe_b_expert[l],
                    moe_w_gate[l], moe_w_up[l], moe_w_down[l])
        mod_x = (jax.nn.silu(c) @ w_mod[l] + b_mod[l])[:, None, :]
        mod_c = jax.nn.silu(c_ctx) @ w_mod[l] + b_mod[l]
        sh1, sc1, g1, sh2, sc2, g2 = jnp.split(mod_x, 6, axis=-1)
        csh1, csc1, cg1, csh2, csc2, cg2 = jnp.split(mod_c, 6, axis=-1)
        s5_zero = jnp.zeros((bsz, S5_GROUPS, S5_STATE), jnp.complex64)
        ssd_zero = jnp.zeros((bsz, SSD_GROUPS, SSD_HPG, SSD_HEAD_DIM, SSD_STATE), jnp.float32)
        hc = rmsnorm(ctx, norm1_g[l]) * (1 + csc1) + csh1
        mix_c, s5_ctx, ssd_ctx = token_mixers(hc, lp, (s5_zero, s5_zero), (ssd_zero, ssd_zero), None)
        hx = rmsnorm(x, norm1_g[l]) * (1 + sc1) + sh1
        mix_x, _, _ = token_mixers(hx, lp, s5_ctx, ssd_ctx, rows)
        x = x + g1 * mix_x
        hx = rmsnorm(x, norm2_g[l]) * (1 + sc2) + sh2
        x = x + g2 * hier_moe(hx.reshape(-1, D_MODEL), *moe_args).reshape(x.shape)
        if l < DEPTH - 1:
            ctx = ctx + cg1 * mix_c
            hc = rmsnorm(ctx, norm2_g[l]) * (1 + csc2) + csh2
            ctx = ctx + cg2 * hier_moe(hc.reshape(-1, D_MODEL), *moe_args).reshape(ctx.shape)
    return rmsnorm(x, final_g)
```

```python
import functools
import math

import jax
import jax.numpy as jnp
from jax import lax
from jax.experimental import pallas as pl
from jax.experimental.pallas import tpu as pltpu

F32 = jnp.float32
BF16 = jnp.bfloat16

GRID_W = 64
EPS = 1e-6
LANE = 128
SSD_HEAD_DIM = 64
SSD_STATE = 128
SSD_CHUNK = 128
SSD_CONV = 5
MOE_TOP_K = 2
MOE_BLOCK = 256
S5_CHUNK = 128


def _cp(sem, mb):
    return pltpu.CompilerParams(dimension_semantics=sem, vmem_limit_bytes=mb * 1024 * 1024)


def _sigmoid(x):
    return 1.0 / (1.0 + jnp.exp(-x))


def _silu(x):
    return x * _sigmoid(x)


def _mod_kernel(c_ref, w_ref, b_ref, o_ref):
    s = _silu(c_ref[...])
    o_ref[...] = jnp.dot(s.astype(BF16), w_ref[...].astype(BF16),
                         preferred_element_type=F32) + b_ref[...]


def _mod_call(cc, w, b):
    r, d = cc.shape
    n = w.shape[1]
    tn = min(512, n)
    return pl.pallas_call(
        _mod_kernel, grid=(n // tn,),
        in_specs=[pl.BlockSpec((r, d), lambda j: (0, 0)),
                  pl.BlockSpec((d, tn), lambda j: (0, j)),
                  pl.BlockSpec((1, tn), lambda j: (0, j))],
        out_specs=pl.BlockSpec((r, tn), lambda j: (0, j)),
        out_shape=jax.ShapeDtypeStruct((r, n), F32),
        compiler_params=_cp(("parallel",), 40), name="mod")(cc, w, b.reshape(1, n))


def _rmsmod(x, g, sc, sh):
    ms = jnp.mean(x * x, axis=-1, keepdims=True)
    return (x * lax.rsqrt(ms + EPS) * g) * (1.0 + sc) + sh


def _norm_lat_kernel(x_ref, g_ref, sc_ref, sh_ref, orm_ref, ocm_ref, *, rb, d):
    for r in range(rb):
        y = _rmsmod(x_ref[r], g_ref[...], sc_ref[...], sh_ref[...]).astype(BF16)
        orm_ref[r] = y
        ocm_ref[:, r * d:(r + 1) * d] = y


def _norm_lat_call(x, g, mods, isc, ish):
    b, l, d = x.shape
    rows = l // GRID_W
    rb = min(4, rows)
    xv = x.reshape(b, rows, GRID_W, d)
    orm, ocm = pl.pallas_call(
        functools.partial(_norm_lat_kernel, rb=rb, d=d), grid=(b, rows // rb),
        in_specs=[pl.BlockSpec((None, rb, GRID_W, d), lambda i, r: (i, r, 0, 0)),
                  pl.BlockSpec((1, d), lambda i, r: (0, 0)),
                  pl.BlockSpec((None, None, 1, d), lambda i, r: (i, isc, 0, 0)),
                  pl.BlockSpec((None, None, 1, d), lambda i, r: (i, ish, 0, 0))],
        out_specs=[pl.BlockSpec((None, rb, GRID_W, d), lambda i, r: (i, r, 0, 0)),
                   pl.BlockSpec((None, GRID_W, rb * d), lambda i, r: (i, 0, r))],
        out_shape=[jax.ShapeDtypeStruct((b, rows, GRID_W, d), BF16),
                   jax.ShapeDtypeStruct((b, GRID_W, rows * d), BF16)],
        compiler_params=_cp(("parallel", "parallel"), 48), name="norm1_lat")(xv, g.reshape(1, d), mods, mods)
    return orm.reshape(b, l, d), ocm.reshape(b, l, d)


def _norm_kernel(x_ref, g_ref, sc_ref, sh_ref, o_ref):
    o_ref[...] = _rmsmod(x_ref[...], g_ref[...], sc_ref[...], sh_ref[...]).astype(o_ref.dtype)


def _norm_ctx_call(x, g, mods, row, isc, ish):
    b, l, d = x.shape
    tr = min(256, l)
    return pl.pallas_call(
        _norm_kernel, grid=(b, l // tr),
        in_specs=[pl.BlockSpec((None, tr, d), lambda i, r: (i, r, 0)),
                  pl.BlockSpec((1, d), lambda i, r: (0, 0)),
                  pl.BlockSpec((None, None, 1, d), lambda i, r: (row, isc, 0, 0)),
                  pl.BlockSpec((None, None, 1, d), lambda i, r: (row, ish, 0, 0))],
        out_specs=pl.BlockSpec((None, tr, d), lambda i, r: (i, r, 0)),
        out_shape=jax.ShapeDtypeStruct((b, l, d), BF16),
        compiler_params=_cp(("parallel", "parallel"), 40), name="norm1_ctx")(x, g.reshape(1, d), mods, mods)


def _norm_router_kernel(x_ref, g_ref, sc_ref, sh_ref, wr_ref, o_ref, lg_ref):
    y = _rmsmod(x_ref[...], g_ref[...], sc_ref[...], sh_ref[...])
    o_ref[...] = y.astype(BF16)
    lg_ref[...] = jnp.dot(y, wr_ref[...], precision=lax.Precision.HIGHEST, preferred_element_type=F32)


def _norm_router_call(x, g, mods, isc, ish, wr):
    b, l, d = x.shape
    tr = min(256, l)
    nr = wr.shape[1]
    return pl.pallas_call(
        _norm_router_kernel, grid=(b, l // tr),
        in_specs=[pl.BlockSpec((None, tr, d), lambda i, r: (i, r, 0)),
                  pl.BlockSpec((1, d), lambda i, r: (0, 0)),
                  pl.BlockSpec((None, None, 1, d), lambda i, r: (i, isc, 0, 0)),
                  pl.BlockSpec((None, None, 1, d), lambda i, r: (i, ish, 0, 0)),
                  pl.BlockSpec((d, nr), lambda i, r: (0, 0))],
        out_specs=[pl.BlockSpec((None, tr, d), lambda i, r: (i, r, 0)),
                   pl.BlockSpec((None, tr, nr), lambda i, r: (i, r, 0))],
        out_shape=[jax.ShapeDtypeStruct((b, l, d), BF16), jax.ShapeDtypeStruct((b, l, nr), F32)],
        compiler_params=_cp(("parallel", "parallel"), 40), name="norm2_router")(x, g.reshape(1, d), mods, mods, wr)


def _mm_kernel(a_ref, b_ref, o_ref):
    o_ref[...] = jnp.dot(a_ref[...], b_ref[...], preferred_element_type=F32).astype(o_ref.dtype)


def _mm_tiles(m, n):
    tm = min(1024, m)
    tn = min(512, n)
    return tm, tn


def _mm_call(a, b, out_dtype, name):
    m, k = a.shape
    n = b.shape[1]
    tm, tn = _mm_tiles(m, n)
    return pl.pallas_call(
        _mm_kernel, grid=(m // tm, n // tn),
        in_specs=[pl.BlockSpec((tm, k), lambda i, j: (i, 0)),
                  pl.BlockSpec((k, tn), lambda i, j: (0, j))],
        out_specs=pl.BlockSpec((tm, tn), lambda i, j: (i, j)),
        out_shape=jax.ShapeDtypeStruct((m, n), out_dtype),
        compiler_params=_cp(("parallel", "parallel"), 48), name=name)(a, b)


def _glu_kernel(a_ref, wv_ref, wg_ref, gate_ref, o_ref):
    a = a_ref[...]
    val = jnp.dot(a, wv_ref[...], preferred_element_type=F32)
    gl = jnp.dot(a, wg_ref[...], preferred_element_type=F32)
    o_ref[...] = (_sigmoid(gate_ref[...].astype(F32)) * (val * _sigmoid(gl))).astype(o_ref.dtype)


def _glu_call(a, wv, wg, gates):
    m, k = a.shape
    n = wv.shape[1]
    tm, tn = _mm_tiles(m, n)
    return pl.pallas_call(
        _glu_kernel, grid=(m // tm, n // tn),
        in_specs=[pl.BlockSpec((tm, k), lambda i, j: (i, 0)),
                  pl.BlockSpec((k, tn), lambda i, j: (0, j)),
                  pl.BlockSpec((k, tn), lambda i, j: (0, j)),
                  pl.BlockSpec((tm, tn), lambda i, j: (i, j))],
        out_specs=pl.BlockSpec((tm, tn), lambda i, j: (i, j)),
        out_shape=jax.ShapeDtypeStruct((m, n), BF16),
        compiler_params=_cp(("parallel", "parallel"), 48), name="s5_glu")(a, wv, wg, gates)


def _merge_kernel(a_ref, w_ref, gate_ref, pa_ref, o_ref):
    br = jnp.dot(a_ref[...], w_ref[...], preferred_element_type=F32)
    o_ref[...] = (pa_ref[...].astype(F32) + _sigmoid(gate_ref[...].astype(F32)) * br).astype(o_ref.dtype)


def _merge_call(a, w, gates, part_a):
    m, k = a.shape
    n = w.shape[1]
    tm, tn = _mm_tiles(m, n)
    off = n // tn
    return pl.pallas_call(
        _merge_kernel, grid=(m // tm, n // tn),
        in_specs=[pl.BlockSpec((tm, k), lambda i, j: (i, 0)),
                  pl.BlockSpec((k, tn), lambda i, j: (0, j)),
                  pl.BlockSpec((tm, tn), lambda i, j: (i, j + off)),
                  pl.BlockSpec((tm, tn), lambda i, j: (i, j))],
        out_specs=pl.BlockSpec((tm, tn), lambda i, j: (i, j)),
        out_shape=jax.ShapeDtypeStruct((m, n), BF16),
        compiler_params=_cp(("parallel", "parallel"), 48), name="ssd_out_merge")(a, w, gates, part_a)


def _resid_kernel(a_ref, w_ref, x_ref, g_ref, o_ref):
    mix = jnp.dot(a_ref[...], w_ref[...], preferred_element_type=F32)
    o_ref[...] = x_ref[...] + g_ref[...] * mix


def _resid_call(a, w, x, mods, ig):
    b, l, d = x.shape
    k = a.shape[1]
    tm, tn = _mm_tiles(l, d)
    nt = l // tm
    return pl.pallas_call(
        _resid_kernel, grid=(b * nt, d // tn),
        in_specs=[pl.BlockSpec((tm, k), lambda i, j: (i, 0)),
                  pl.BlockSpec((k, tn), lambda i, j: (0, j)),
                  pl.BlockSpec((None, tm, tn), lambda i, j: (i // nt, i % nt, j)),
                  pl.BlockSpec((None, None, 1, tn), lambda i, j: (i // nt, ig, 0, j))],
        out_specs=pl.BlockSpec((None, tm, tn), lambda i, j: (i // nt, i % nt, j)),
        out_shape=jax.ShapeDtypeStruct((b, l, d), F32),
        compiler_params=_cp(("parallel", "parallel"), 48), name="w_o_resid")(a, w, x, mods)


def _s5_kernel(u_ref, b_ref, c_ref, lre_ref, lim_ref, y_ref, bu_ref, hr_ref, hi_ref, *, nb, t, p8, nskip):
    c = pl.program_id(1)
    q = 2 * nb
    nk = p8 // LANE

    @pl.when(c == 0)
    def _():
        hr_ref[...] = jnp.zeros_like(hr_ref)
        hi_ref[...] = jnp.zeros_like(hi_ref)

    for d in range(2):
        u = u_ref[d * nb:(d + 1) * nb].reshape(nb * t, LANE)
        bu = jnp.dot(u, b_ref[d], preferred_element_type=F32)
        for k in range(2 * nk):
            bu_ref[k, d * nb * t:(d + 1) * nb * t, :] = bu[:, k * LANE:(k + 1) * LANE]
    ar = [lre_ref[:, k * LANE:(k + 1) * LANE] for k in range(nk)]
    ai = [lim_ref[:, k * LANE:(k + 1) * LANE] for k in range(nk)]

    def step(s, carry):
        rows = pl.ds(s, q, stride=t)
        out = []
        for k in range(nk):
            hr, hi = carry[k]
            nr = ar[k] * hr - ai[k] * hi + bu_ref[k, rows, :]
            ni = ar[k] * hi + ai[k] * hr + bu_ref[nk + k, rows, :]
            bu_ref[k, rows, :] = nr
            bu_ref[nk + k, rows, :] = ni
            out.append((nr, ni))
        return tuple(out)

    init = tuple((hr_ref[:, k * LANE:(k + 1) * LANE], hi_ref[:, k * LANE:(k + 1) * LANE]) for k in range(nk))
    fin = lax.fori_loop(0, t, step, init, unroll=8)
    for k in range(nk):
        hr_ref[:, k * LANE:(k + 1) * LANE] = fin[k][0]
        hi_ref[:, k * LANE:(k + 1) * LANE] = fin[k][1]

    @pl.when(c >= nskip)
    def _():
        for d in range(2):
            h = jnp.concatenate([bu_ref[k, d * nb * t:(d + 1) * nb * t, :].astype(BF16) for k in range(2 * nk)],
                                axis=1)
            y = jnp.dot(h, c_ref[d], preferred_element_type=F32)
            y_ref[d * nb:(d + 1) * nb] = y.reshape(nb, t, LANE).astype(y_ref.dtype)


def _s5_call(u_all, bmat, cmat, lre, lim, nb, l_ctx):
    nj, q, lt, _ = u_all.shape
    t = S5_CHUNK
    p8 = lre.shape[-1]
    nskip = l_ctx // t
    nc = lt // t
    kern = functools.partial(_s5_kernel, nb=nb, t=t, p8=p8, nskip=nskip)
    return pl.pallas_call(
        kern, grid=(nj, nc),
        in_specs=[pl.BlockSpec((None, q, t, LANE), lambda j, c: (j, 0, c, 0)),
                  pl.BlockSpec((None, 2, LANE, 2 * p8), lambda j, c: (j, 0, 0, 0)),
                  pl.BlockSpec((None, 2, 2 * p8, LANE), lambda j, c: (j, 0, 0, 0)),
                  pl.BlockSpec((None, q, p8), lambda j, c: (j, 0, 0)),
                  pl.BlockSpec((None, q, p8), lambda j, c: (j, 0, 0))],
        out_specs=pl.BlockSpec((None, q, t, LANE), lambda j, c: (j, 0, jnp.maximum(c - nskip, 0), 0)),
        out_shape=jax.ShapeDtypeStruct((nj, q, lt - l_ctx, LANE), F32),
        scratch_shapes=[pltpu.VMEM((2 * p8 // LANE, q * t, LANE), F32),
                        pltpu.VMEM((q, p8), F32), pltpu.VMEM((q, p8), F32)],
        compiler_params=_cp(("parallel", "arbitrary"), 40), name="s5_scan")(u_all, bmat, cmat, lre, lim)


def _s5_params(lam_re, lam_im, log_dt, b_re, b_im, c_re, c_im, nb):
    _, g, p = lam_re.shape
    s = b_re.shape[-1]
    gpb = LANE // s
    nj = g // gpb
    lam = lax.complex(lam_re.astype(F32), lam_im.astype(F32))
    lam_bar = jnp.exp(lam * jnp.exp(log_dt.astype(F32))[..., None])
    b_bar = ((lam_bar - 1.0) / lam)[..., None] * lax.complex(b_re.astype(F32), b_im.astype(F32))
    eye = jnp.eye(gpb, dtype=F32)

    def bmat_of(bpart):
        bb = bpart.reshape(2, nj, gpb, p, s)
        m = jnp.einsum('dnkps,kl->dnkslp', bb, eye)
        return m.reshape(2, nj, gpb * s, gpb * p).transpose(1, 0, 2, 3)

    bmat = jnp.concatenate([bmat_of(b_bar.real), bmat_of(b_bar.imag)], axis=-1).astype(BF16)

    def cmat_of(cpart):
        cc = cpart.astype(F32).reshape(2, nj, gpb, s, p)
        m = jnp.einsum('dnksp,kl->dnkpls', cc, eye)
        return m.reshape(2, nj, gpb * p, gpb * s).transpose(1, 0, 2, 3)

    cmat = jnp.concatenate([cmat_of(c_re), -cmat_of(c_im)], axis=-2).astype(BF16)

    def lam_of(part):
        v = part.reshape(2, nj, gpb * p).transpose(1, 0, 2)
        return jnp.repeat(v, nb, axis=1)

    return bmat, cmat, lam_of(lam_bar.real), lam_of(lam_bar.imag)


def _conv_kernel(x_ref, w_ref, b_ref, o_ref, *, l):
    x = x_ref[...].astype(F32)
    rows = lax.broadcasted_iota(jnp.int32, x.shape, 0)
    half = SSD_CONV // 2
    acc = x * w_ref[half:half + 1, :] + b_ref[...]
    for k in range(SSD_CONV):
        if k == half:
            continue
        off = k - half
        xs = pltpu.roll(x, shift=(-off) % l, axis=0)
        valid = jnp.logical_and(rows + off >= 0, rows + off < l)
        acc = acc + jnp.where(valid, xs, 0.0) * w_ref[k:k + 1, :]
    o_ref[...] = _silu(acc).astype(o_ref.dtype)


def _conv_call(proj, w, bias, col0):
    b, l, _ = proj.shape
    c = w.shape[1]
    tc = 256
    off = col0 // tc
    return pl.pallas_call(
        functools.partial(_conv_kernel, l=l), grid=(b, c // tc),
        in_specs=[pl.BlockSpec((None, l, tc), lambda i, j: (i, 0, j + off)),
                  pl.BlockSpec((SSD_CONV, tc), lambda i, j: (0, j)),
                  pl.BlockSpec((1, tc), lambda i, j: (0, j))],
        out_specs=pl.BlockSpec((None, l, tc), lambda i, j: (i, 0, j)),
        out_shape=jax.ShapeDtypeStruct((b, l, c), BF16),
        compiler_params=_cp(("parallel", "parallel"), 40), name="ssd_conv")(proj, w, bias.reshape(1, c))


def _softplus(x):
    return jnp.maximum(x, 0.0) + jnp.log1p(jnp.exp(-jnp.abs(x)))


def _ssd_dir(xbc_ref, dt_ref, y_ref, h_ref, bias, a_neg, d, *, nh, ng, need_y):
    qn = SSD_CHUNK
    hd = SSD_HEAD_DIM
    w = nh * hd
    gw = w // ng
    gn = ng * SSD_STATE
    ii = lax.broadcasted_iota(jnp.int32, (qn, qn), 0)
    jj = lax.broadcasted_iota(jnp.int32, (qn, qn), 1)
    mask = (jj <= ii) if d == 0 else (jj >= ii)
    lmat = mask.astype(F32)
    dtv = _softplus(dt_ref[...] + bias)
    cum = jnp.dot(lmat, dtv * a_neg, precision=lax.Precision.HIGHEST, preferred_element_type=F32)
    cum_t = cum.T
    edge = qn - 1 if d == 0 else 0
    tot = cum[edge:edge + 1, :]
    to_edge = jnp.exp(tot - cum)
    ecum = jnp.exp(cum)
    decay = jnp.exp(tot)
    lane = lax.broadcasted_iota(jnp.int32, (qn, LANE), 1)
    left = lane < hd
    for g in range(ng):
        bg = xbc_ref[:, w + g * SSD_STATE:w + (g + 1) * SSD_STATE]
        cg = xbc_ref[:, w + gn + g * SSD_STATE:w + gn + (g + 1) * SSD_STATE]
        s_in = h_ref[d, :, g * gw:(g + 1) * gw]
        if need_y:
            cb = lax.dot_general(cg, bg, (((1,), (1,)), ((), ())), preferred_element_type=F32)
            yoff = jnp.dot(cg, s_in.astype(BF16), preferred_element_type=F32)
        xw_parts = []
        dec_parts = []
        for pr in range(gw // LANE):
            c0 = d * nh + (g * gw) // hd + 2 * pr
            col = g * gw + pr * LANE
            xp = xbc_ref[:, col:col + LANE].astype(F32)
            xdt = xp * jnp.where(left, dtv[:, c0:c0 + 1], dtv[:, c0 + 1:c0 + 2])
            xw_parts.append((xdt * jnp.where(left, to_edge[:, c0:c0 + 1], to_edge[:, c0 + 1:c0 + 2])).astype(BF16))
            dec_parts.append(jnp.where(left[0:1], decay[:, c0:c0 + 1], decay[:, c0 + 1:c0 + 2]))
            if need_y:
                seg0 = jnp.where(mask, jnp.exp(cum[:, c0:c0 + 1] - cum_t[c0:c0 + 1, :]), 0.0)
                seg1 = jnp.where(mask, jnp.exp(cum[:, c0 + 1:c0 + 2] - cum_t[c0 + 1:c0 + 2, :]), 0.0)
                m = jnp.concatenate([cb * seg0, cb * seg1], axis=1).astype(BF16)
                r = jnp.concatenate([jnp.where(left, xdt, 0.0), jnp.where(left, 0.0, xdt)], axis=0).astype(BF16)
                yd = jnp.dot(m, r, preferred_element_type=F32)
                ec = jnp.where(left, ecum[:, c0:c0 + 1], ecum[:, c0 + 1:c0 + 2])
                y_ref[:, col:col + LANE] = (yd + yoff[:, pr * LANE:(pr + 1) * LANE] * ec).astype(y_ref.dtype)
        xw = jnp.concatenate(xw_parts, axis=1)
        dec = jnp.concatenate(dec_parts, axis=1)
        s_new = lax.dot_general(bg, xw, (((0,), (0,)), ((), ())), preferred_element_type=F32)
        h_ref[d, :, g * gw:(g + 1) * gw] = s_in * dec + s_new


def _ssd_kernel(*refs, nh, ng, need_y):
    if need_y:
        xf_ref, xb_ref, dtf_ref, dtb_ref, h0_ref, bias_ref, alog_ref, yf_ref, yb_ref, h_ref = refs
    else:
        xf_ref, xb_ref, dtf_ref, dtb_ref, h0_ref, bias_ref, alog_ref, h_ref = refs
        yf_ref = yb_ref = None

    @pl.when(pl.program_id(1) == 0)
    def _():
        h_ref[...] = h0_ref[...]

    bias = bias_ref[...]
    a_neg = -jnp.exp(alog_ref[...])
    _ssd_dir(xf_ref, dtf_ref, yf_ref, h_ref, bias, a_neg, 0, nh=nh, ng=ng, need_y=need_y)
    _ssd_dir(xb_ref, dtb_ref, yb_ref, h_ref, bias, a_neg, 1, nh=nh, ng=ng, need_y=need_y)


def _ssd_call(xbc, dt, h0, bias, alog, nh, ng, need_y):
    b, l, cd = xbc.shape
    w = nh * SSD_HEAD_DIM
    nc = l // SSD_CHUNK
    q = SSD_CHUNK
    fwd = lambda i, s: (i, s, 0)
    bwd = lambda i, s: (i, nc - 1 - s, 0)
    hspec = pl.BlockSpec((None, 2, SSD_STATE, w), lambda i, s: (i, 0, 0, 0))
    in_specs = [pl.BlockSpec((None, q, cd), fwd), pl.BlockSpec((None, q, cd), bwd),
                pl.BlockSpec((None, q, LANE), fwd), pl.BlockSpec((None, q, LANE), bwd),
                hspec,
                pl.BlockSpec((1, LANE), lambda i, s: (0, 0)), pl.BlockSpec((1, LANE), lambda i, s: (0, 0))]
    out_specs = [hspec]
    out_shape = [jax.ShapeDtypeStruct((b, 2, SSD_STATE, w), F32)]
    if need_y:
        out_specs = [pl.BlockSpec((None, q, w), fwd), pl.BlockSpec((None, q, w), bwd)] + out_specs
        out_shape = [jax.ShapeDtypeStruct((b, l, w), BF16)] * 2 + out_shape
    return pl.pallas_call(
        functools.partial(_ssd_kernel, nh=nh, ng=ng, need_y=need_y), grid=(b, nc),
        in_specs=in_specs, out_specs=out_specs, out_shape=out_shape,
        compiler_params=_cp(("parallel", "arbitrary"), 48),
        name="ssd_scan" if need_y else "ssd_scan_ctx")(xbc, xbc, dt, dt, h0, bias, alog)


def _gnorm_kernel(yf_ref, yb_ref, xs_ref, z_ref, d_ref, g_ref, o_ref, *, ng, rows):
    y = d_ref[...] * xs_ref[...].astype(F32) + yf_ref[...].astype(F32) + yb_ref[...].astype(F32)
    y = y * _silu(z_ref[...].astype(F32))
    w = y.shape[1]
    gw = w // ng
    parts = []
    for g in range(ng):
        yg = y[:, g * gw:(g + 1) * gw]
        ms = jnp.mean(yg * yg, axis=-1, keepdims=True)
        parts.append(yg * lax.rsqrt(ms + EPS))
    res = (jnp.concatenate(parts, axis=1) * g_ref[...]).astype(o_ref.dtype)
    for k in range(SSD_CHUNK // rows):
        o_ref[:, k * w:(k + 1) * w] = res[k * rows:(k + 1) * rows, :]


def _gnorm_call(yf, yb, xbc, proj, dvec, gvec, ng):
    b, l, w = yf.shape
    rows = l // GRID_W
    nw = SSD_CHUNK // rows
    blk = lambda i, s: (i, s, 0)
    out = pl.pallas_call(
        functools.partial(_gnorm_kernel, ng=ng, rows=rows), grid=(b, l // SSD_CHUNK),
        in_specs=[pl.BlockSpec((None, SSD_CHUNK, w), blk), pl.BlockSpec((None, SSD_CHUNK, w), blk),
                  pl.BlockSpec((None, SSD_CHUNK, w), blk), pl.BlockSpec((None, SSD_CHUNK, w), blk),
                  pl.BlockSpec((1, w), lambda i, s: (0, 0)), pl.BlockSpec((1, w), lambda i, s: (0, 0))],
        out_specs=pl.BlockSpec((None, rows, nw * w), lambda i, s: (i, 0, s)),
        out_shape=jax.ShapeDtypeStruct((b, rows, GRID_W * w), BF16),
        compiler_params=_cp(("parallel", "parallel"), 40), name="ssd_gnorm")(yf, yb, xbc, proj, dvec, gvec)
    return out.reshape(b * l, w)


def _moe_kernel(be_ref, nu_ref, x_ref, wg_ref, wu_ref, wd_ref, o_ref, acc_ref):
    i = pl.program_id(0)
    h = pl.program_id(1)
    nh = pl.num_programs(1)
    used = i < nu_ref[0]

    @pl.when(used)
    def _():
        x = x_ref[...]
        gt = jnp.dot(x, wg_ref[...].astype(BF16), preferred_element_type=F32)
        up = jnp.dot(x, wu_ref[...].astype(BF16), preferred_element_type=F32)
        act = (_silu(gt) * up).astype(BF16)
        contrib = jnp.dot(act, wd_ref[...].astype(BF16), preferred_element_type=F32)

        @pl.when(h == 0)
        def _():
            acc_ref[...] = contrib

        @pl.when(h > 0)
        def _():
            acc_ref[...] += contrib

    @pl.when(jnp.logical_not(used))
    def _():
        acc_ref[...] = jnp.zeros_like(acc_ref)

    @pl.when(h == nh - 1)
    def _():
        o_ref[...] = acc_ref[...].astype(o_ref.dtype)


def _moe_call(block_e, n_used, xs, w_gate, w_up, w_down):
    n_slots, d = xs.shape
    hid = w_gate.shape[-1]
    th = min(256, hid)
    nblk = n_slots // MOE_BLOCK
    gs = pltpu.PrefetchScalarGridSpec(
        num_scalar_prefetch=2, grid=(nblk, hid // th),
        in_specs=[pl.BlockSpec((MOE_BLOCK, d), lambda i, h, be, nu: (i, 0)),
                  pl.BlockSpec((None, d, th), lambda i, h, be, nu: (be[i], 0, h)),
                  pl.BlockSpec((None, d, th), lambda i, h, be, nu: (be[i], 0, h)),
                  pl.BlockSpec((None, th, d), lambda i, h, be, nu: (be[i], h, 0))],
        out_specs=pl.BlockSpec((MOE_BLOCK, d), lambda i, h, be, nu: (i, 0)),
        scratch_shapes=[pltpu.VMEM((MOE_BLOCK, d), F32)])
    return pl.pallas_call(
        _moe_kernel, grid_spec=gs, out_shape=jax.ShapeDtypeStruct((n_slots, d), BF16),
        compiler_params=_cp(("parallel", "arbitrary"), 56), name="moe_experts")(
            block_e, n_used, xs, w_gate, w_up, w_down)


def _final_kernel(x_ref, ya_ref, yb_ref, w_ref, g2_ref, fg_ref, o_ref):
    wts = w_ref[...]
    moe = ya_ref[...].astype(F32) * wts[:, 0:1] + yb_ref[...].astype(F32) * wts[:, 1:2]
    x = x_ref[...] + g2_ref[...] * moe
    ms = jnp.mean(x * x, axis=-1, keepdims=True)
    o_ref[...] = x * lax.rsqrt(ms + EPS) * fg_ref[...]


def _final_call(x, ya, yb, wts, mods, ig, fg):
    b, l, d = x.shape
    tr = min(256, l)
    nt = l // tr
    row = lambda i, r: (i * nt + r, 0)
    return pl.pallas_call(
        _final_kernel, grid=(b, nt),
        in_specs=[pl.BlockSpec((None, tr, d), lambda i, r: (i, r, 0)),
                  pl.BlockSpec((tr, d), row), pl.BlockSpec((tr, d), row),
                  pl.BlockSpec((tr, MOE_TOP_K), row),
                  pl.BlockSpec((None, None, 1, d), lambda i, r: (i, ig, 0, 0)),
                  pl.BlockSpec((1, d), lambda i, r: (0, 0))],
        out_specs=pl.BlockSpec((None, tr, d), lambda i, r: (i, r, 0)),
        out_shape=jax.ShapeDtypeStruct((b, l, d), F32),
        compiler_params=_cp(("parallel", "parallel"), 40), name="combine_final")(x, ya, yb, wts, mods, fg)


def _route(logits, b_group, b_expert):
    m = logits.shape[0]
    ngr = b_group.shape[0]
    ne = b_expert.shape[0]
    epg = ne // ngr
    g_prob = jax.nn.softmax(logits[:, :ngr] + b_group.astype(F32), axis=-1)
    g_p, g_idx = lax.top_k(g_prob, 1)
    e_logits = (logits[:, ngr:ngr + ne] + b_expert.astype(F32)).reshape(m, ngr, epg)
    e_logits = jnp.take_along_axis(e_logits, g_idx[:, :, None], axis=1)[:, 0]
    e_top, e_idx = lax.top_k(e_logits, MOE_TOP_K)
    weights = g_p * jax.nn.softmax(e_top, axis=-1)
    experts = g_idx * epg + e_idx
    n_assign = m * MOE_TOP_K
    flat_e = experts.reshape(-1)
    order = jnp.argsort(flat_e)
    sorted_e = flat_e[order]
    counts = jnp.zeros((ne,), jnp.int32).at[flat_e].add(1)
    padded = (counts + MOE_BLOCK - 1) // MOE_BLOCK * MOE_BLOCK
    pad_end = jnp.cumsum(padded)
    pad_start = pad_end - padded
    start = jnp.cumsum(counts) - counts
    dest = pad_start[sorted_e] + jnp.arange(n_assign, dtype=jnp.int32) - start[sorted_e]
    n_blocks = -(-(n_assign + ne * (MOE_BLOCK - 1)) // MOE_BLOCK)
    tok_of_sorted = (order // MOE_TOP_K).astype(jnp.int32)
    slot_tok = jnp.full((n_blocks * MOE_BLOCK,), m, jnp.int32).at[dest].set(tok_of_sorted)
    block_e = jnp.minimum(
        jnp.searchsorted(pad_end, jnp.arange(n_blocks, dtype=jnp.int32) * MOE_BLOCK, side='right'),
        ne - 1).astype(jnp.int32)
    n_used = (pad_end[-1] // MOE_BLOCK).astype(jnp.int32).reshape(1)
    pos = jnp.zeros((n_assign,), jnp.int32).at[order].set(dest.astype(jnp.int32)).reshape(m, MOE_TOP_K)
    return weights, slot_tok, block_e, n_used, pos


def kernel(x, c, ctx, c_ctx, w_mod, b_mod, norm1_g, w_in, s5_lam_re, s5_lam_im, s5_log_dt, s5_b_re, s5_b_im, s5_c_re, s5_c_im, s5_d, s5_w_val, s5_w_gate, ssd_conv_w, ssd_conv_b, ssd_a_log, ssd_dt_bias, ssd_d, ssd_norm_g, ssd_w_out, w_o, norm2_g, moe_w_group, moe_b_group, moe_w_expert, moe_b_expert, moe_w_gate, moe_w_up, moe_w_down, final_g):
    depth = w_mod.shape[0]
    assert depth == 1, "single-layer block"
    bsz, n_lat, d = x.shape
    l_ctx = ctx.shape[1]
    w5 = s5_d.shape[1]
    nh = ssd_d.shape[1]
    w = nh * SSD_HEAD_DIM
    conv_dim = ssd_conv_w.shape[2]
    ng = (conv_dim - w) // (2 * SSD_STATE)
    ssd_in = w + conv_dim + 2 * nh
    o1, o2 = w5, w5 + ssd_in
    l = 0

    cc = jnp.concatenate([c, c_ctx[None, :]], axis=0)
    cc = jnp.pad(cc, ((0, (-cc.shape[0]) % 8), (0, 0)))
    mods = _mod_call(cc, w_mod[l], b_mod[l]).reshape(cc.shape[0], 6, 1, d)
    i_sh1, i_sc1, i_g1, i_sh2, i_sc2, i_g2 = range(6)

    w_s5 = w_in[l][:, :o1].astype(BF16)
    w_ssd = w_in[l][:, o1:o1 + w + conv_dim].astype(BF16)
    w_dt = jnp.pad(w_in[l][:, o1 + w + conv_dim:o2], ((0, 0), (0, LANE - 2 * nh))).astype(BF16)
    w_gates = w_in[l][:, o2:].astype(BF16)

    hn_rm, hn_cm = _norm_lat_call(x, norm1_g[l], mods, i_sc1, i_sh1)
    hc = _norm_ctx_call(ctx, norm1_g[l], mods, bsz, i_sc1, i_sh1)
    hn_rm = hn_rm.reshape(bsz * n_lat, d)
    hn_cm = hn_cm.reshape(bsz * n_lat, d)
    hc = hc.reshape(bsz * l_ctx, d)

    u_lat = _mm_call(hn_rm, w_s5, BF16, "in_s5").reshape(bsz, n_lat, w5)
    u_ctx = _mm_call(hc, w_s5, BF16, "in_s5_ctx").reshape(bsz, l_ctx, w5)
    gates = _mm_call(hn_rm, w_gates, BF16, "in_gates")
    p_lat = _mm_call(hn_cm, w_ssd, BF16, "in_ssd").reshape(bsz, n_lat, w + conv_dim)
    p_ctx = _mm_call(hc, w_ssd, BF16, "in_ssd_ctx").reshape(bsz, l_ctx, w + conv_dim)
    dt_lat = _mm_call(hn_cm, w_dt, F32, "in_dt").reshape(bsz, n_lat, LANE)
    dt_ctx = _mm_call(hc, w_dt, F32, "in_dt_ctx").reshape(bsz, l_ctx, LANE)

    nj = w5 // LANE
    seq_f = jnp.concatenate([u_ctx, u_lat], axis=1)
    seq_b = jnp.concatenate([jnp.flip(u_ctx, 1), jnp.flip(u_lat, 1)], axis=1)
    u_all = jnp.concatenate([seq_f, seq_b], axis=0)
    u_all = u_all.reshape(2 * bsz, l_ctx + n_lat, nj, LANE).transpose(2, 0, 1, 3)
    bmat, cmat, lre, lim = _s5_params(s5_lam_re[l], s5_lam_im[l], s5_log_dt[l], s5_b_re[l], s5_b_im[l],
                                      s5_c_re[l], s5_c_im[l], bsz)
    y_all = _s5_call(u_all, bmat, cmat, lre, lim, bsz, l_ctx)
    y_all = y_all.transpose(1, 2, 0, 3).reshape(2 * bsz, n_lat, w5)
    y_a = s5_d[l].astype(F32) * u_lat.astype(F32) + y_all[:bsz] + jnp.flip(y_all[bsz:], 1)
    g_a = jax.nn.gelu(y_a).astype(BF16).reshape(bsz * n_lat, w5)

    xbc_ctx = _conv_call(p_ctx, ssd_conv_w[l], ssd_conv_b[l], w)
    xbc_lat = _conv_call(p_lat, ssd_conv_w[l], ssd_conv_b[l], w)
    pad_h = LANE - 2 * nh
    bias = jnp.pad(ssd_dt_bias[l].astype(F32).reshape(1, 2 * nh), ((0, 0), (0, pad_h)))
    alog = jnp.pad(ssd_a_log[l].astype(F32).reshape(1, 2 * nh), ((0, 0), (0, pad_h)))
    h_zero = jnp.zeros((bsz, 2, SSD_STATE, w), F32)
    (h_ctx,) = _ssd_call(xbc_ctx, dt_ctx, h_zero, bias, alog, nh, ng, False)
    y_f, y_b, _ = _ssd_call(xbc_lat, dt_lat, h_ctx, bias, alog, nh, ng, True)
    d_vec = jnp.repeat(ssd_d[l].astype(F32), SSD_HEAD_DIM).reshape(1, w)
    y_ssd = _gnorm_call(y_f, y_b, xbc_lat, p_lat, d_vec, ssd_norm_g[l].astype(F32).reshape(1, w), ng)

    part_a = _glu_call(g_a, s5_w_val[l].astype(BF16), s5_w_gate[l].astype(BF16), gates)
    merged = _merge_call(y_ssd, ssd_w_out[l].astype(BF16), gates, part_a)
    x1 = _resid_call(merged, w_o[l].astype(BF16), x, mods, i_g1)

    ngr = moe_w_group.shape[-1]
    ne = moe_w_expert.shape[-1]
    wr = jnp.concatenate([moe_w_group[l], moe_w_expert[l]], axis=1).astype(F32)
    wr = jnp.pad(wr, ((0, 0), (0, (-(ngr + ne)) % LANE)))
    hx, logits = _norm_router_call(x1, norm2_g[l], mods, i_sc2, i_sh2, wr)
    m = bsz * n_lat
    hx = hx.reshape(m, d)
    weights, slot_tok, block_e, n_used, pos = _route(logits.reshape(m, -1), moe_b_group[l], moe_b_expert[l])
    h_pad = jnp.concatenate([hx, jnp.zeros((1, d), hx.dtype)], axis=0)
    xs = h_pad[slot_tok]
    y_slots = _moe_call(block_e, n_used, xs, moe_w_gate[l], moe_w_up[l], moe_w_down[l])
    ya = y_slots[pos[:, 0]]
    yb = y_slots[pos[:, 1]]
    return _final_call(x1, ya, yb, weights.astype(F32), mods, i_g2, final_g.reshape(1, d))
```

```python
import functools
import math

import jax
import jax.numpy as jnp
from jax import lax
from jax.experimental import pallas as pl
from jax.experimental.pallas import tpu as pltpu

F32 = jnp.float32
BF16 = jnp.bfloat16

GRID_W = 64
EPS = 1e-6
LANE = 128
SSD_HEAD_DIM = 64
SSD_STATE = 128
SSD_CHUNK = 128
SSD_CONV = 5
MOE_TOP_K = 2
MOE_BLOCK = 256
S5_CHUNK = 128
S5_ROW_PAD = 8


def _cp(sem, mb):
    return pltpu.CompilerParams(dimension_semantics=sem, vmem_limit_bytes=mb * 1024 * 1024)


def _sigmoid(x):
    return 1.0 / (1.0 + jnp.exp(-x))


def _silu(x):
    return x * _sigmoid(x)


def _mod_kernel(c_ref, w_ref, b_ref, o_ref):
    s = _silu(c_ref[...])
    o_ref[...] = jnp.dot(s.astype(BF16), w_ref[...].astype(BF16),
                         preferred_element_type=F32) + b_ref[...]


def _mod_call(cc, w, b):
    r, d = cc.shape
    n = w.shape[1]
    tn = min(512, n)
    return pl.pallas_call(
        _mod_kernel, grid=(n // tn,),
        in_specs=[pl.BlockSpec((r, d), lambda j: (0, 0)),
                  pl.BlockSpec((d, tn), lambda j: (0, j)),
                  pl.BlockSpec((1, tn), lambda j: (0, j))],
        out_specs=pl.BlockSpec((r, tn), lambda j: (0, j)),
        out_shape=jax.ShapeDtypeStruct((r, n), F32),
        compiler_params=_cp(("parallel",), 40), name="mod")(cc, w, b.reshape(1, n))


def _rmsmod(x, g, sc, sh):
    ms = jnp.mean(x * x, axis=-1, keepdims=True)
    return (x * lax.rsqrt(ms + EPS) * g) * (1.0 + sc) + sh


def _norm_lat_kernel(x_ref, g_ref, sc_ref, sh_ref, orm_ref, ocm_ref, *, rb, d):
    for r in range(rb):
        y = _rmsmod(x_ref[r], g_ref[...], sc_ref[...], sh_ref[...]).astype(BF16)
        orm_ref[r] = y
        ocm_ref[:, r * d:(r + 1) * d] = y


def _norm_lat_call(x, g, mods, isc, ish):
    b, l, d = x.shape
    rows = l // GRID_W
    rb = min(4, rows)
    xv = x.reshape(b, rows, GRID_W, d)
    orm, ocm = pl.pallas_call(
        functools.partial(_norm_lat_kernel, rb=rb, d=d), grid=(b, rows // rb),
        in_specs=[pl.BlockSpec((None, rb, GRID_W, d), lambda i, r: (i, r, 0, 0)),
                  pl.BlockSpec((1, d), lambda i, r: (0, 0)),
                  pl.BlockSpec((None, None, 1, d), lambda i, r: (i, isc, 0, 0)),
                  pl.BlockSpec((None, None, 1, d), lambda i, r: (i, ish, 0, 0))],
        out_specs=[pl.BlockSpec((None, rb, GRID_W, d), lambda i, r: (i, r, 0, 0)),
                   pl.BlockSpec((None, GRID_W, rb * d), lambda i, r: (i, 0, r))],
        out_shape=[jax.ShapeDtypeStruct((b, rows, GRID_W, d), BF16),
                   jax.ShapeDtypeStruct((b, GRID_W, rows * d), BF16)],
        compiler_params=_cp(("parallel", "parallel"), 48), name="norm1_lat")(xv, g.reshape(1, d), mods, mods)
    return orm.reshape(b, l, d), ocm.reshape(b, l, d)


def _norm_kernel(x_ref, g_ref, sc_ref, sh_ref, o_ref):
    o_ref[...] = _rmsmod(x_ref[...], g_ref[...], sc_ref[...], sh_ref[...]).astype(o_ref.dtype)


def _norm_ctx_call(x, g, mods, row, isc, ish):
    b, l, d = x.shape
    tr = min(256, l)
    return pl.pallas_call(
        _norm_kernel, grid=(b, l // tr),
        in_specs=[pl.BlockSpec((None, tr, d), lambda i, r: (i, r, 0)),
                  pl.BlockSpec((1, d), lambda i, r: (0, 0)),
                  pl.BlockSpec((None, None, 1, d), lambda i, r: (row, isc, 0, 0)),
                  pl.BlockSpec((None, None, 1, d), lambda i, r: (row, ish, 0, 0))],
        out_specs=pl.BlockSpec((None, tr, d), lambda i, r: (i, r, 0)),
        out_shape=jax.ShapeDtypeStruct((b, l, d), BF16),
        compiler_params=_cp(("parallel", "parallel"), 40), name="norm1_ctx")(x, g.reshape(1, d), mods, mods)


def _norm_router_kernel(x_ref, g_ref, sc_ref, sh_ref, wr_ref, o_ref, lg_ref):
    y = _rmsmod(x_ref[...], g_ref[...], sc_ref[...], sh_ref[...])
    o_ref[...] = y.astype(BF16)
    lg_ref[...] = jnp.dot(y, wr_ref[...], precision=lax.Precision.HIGHEST, preferred_element_type=F32)


def _norm_router_call(x, g, mods, isc, ish, wr):
    b, l, d = x.shape
    tr = min(256, l)
    nr = wr.shape[1]
    return pl.pallas_call(
        _norm_router_kernel, grid=(b, l // tr),
        in_specs=[pl.BlockSpec((None, tr, d), lambda i, r: (i, r, 0)),
                  pl.BlockSpec((1, d), lambda i, r: (0, 0)),
                  pl.BlockSpec((None, None, 1, d), lambda i, r: (i, isc, 0, 0)),
                  pl.BlockSpec((None, None, 1, d), lambda i, r: (i, ish, 0, 0)),
                  pl.BlockSpec((d, nr), lambda i, r: (0, 0))],
        out_specs=[pl.BlockSpec((None, tr, d), lambda i, r: (i, r, 0)),
                   pl.BlockSpec((None, tr, nr), lambda i, r: (i, r, 0))],
        out_shape=[jax.ShapeDtypeStruct((b, l, d), BF16), jax.ShapeDtypeStruct((b, l, nr), F32)],
        compiler_params=_cp(("parallel", "parallel"), 40), name="norm2_router")(x, g.reshape(1, d), mods, mods, wr)


def _mm_kernel(a_ref, b_ref, o_ref):
    o_ref[...] = jnp.dot(a_ref[...], b_ref[...], preferred_element_type=F32).astype(o_ref.dtype)


def _mm_tiles(m, n):
    tm = min(1024, m)
    tn = min(512, n)
    return tm, tn


def _mm_call(a, b, out_dtype, name):
    m, k = a.shape
    n = b.shape[1]
    tm, tn = _mm_tiles(m, n)
    return pl.pallas_call(
        _mm_kernel, grid=(m // tm, n // tn),
        in_specs=[pl.BlockSpec((tm, k), lambda i, j: (i, 0)),
                  pl.BlockSpec((k, tn), lambda i, j: (0, j))],
        out_specs=pl.BlockSpec((tm, tn), lambda i, j: (i, j)),
        out_shape=jax.ShapeDtypeStruct((m, n), out_dtype),
        compiler_params=_cp(("parallel", "parallel"), 48), name=name)(a, b)


def _gelu_tanh(x):
    return x * (0.5 * (1.0 + jnp.tanh(math.sqrt(2.0 / math.pi) * (x + 0.044715 * (x * x * x)))))


def _glu_kernel(u_ref, yf_ref, yb_ref, d_ref, wv_ref, wg_ref, gate_ref, o_ref, a_ref):
    @pl.when(pl.program_id(1) == 0)
    def _():
        y = d_ref[...] * u_ref[...].astype(F32) + yf_ref[...].astype(F32) + yb_ref[...].astype(F32)
        a_ref[...] = _gelu_tanh(y).astype(BF16)

    a = a_ref[...]
    val = jnp.dot(a, wv_ref[...], preferred_element_type=F32)
    gl = jnp.dot(a, wg_ref[...], preferred_element_type=F32)
    o_ref[...] = (_sigmoid(gate_ref[...].astype(F32)) * (val * _sigmoid(gl))).astype(o_ref.dtype)


def _glu_call(u, yf, yb, dvec, wv, wg, gates):
    m, k = u.shape
    n = wv.shape[1]
    tm, tn = _mm_tiles(m, n)
    row = pl.BlockSpec((tm, k), lambda i, j: (i, 0))
    return pl.pallas_call(
        _glu_kernel, grid=(m // tm, n // tn),
        in_specs=[row, row, row,
                  pl.BlockSpec((1, k), lambda i, j: (0, 0)),
                  pl.BlockSpec((k, tn), lambda i, j: (0, j)),
                  pl.BlockSpec((k, tn), lambda i, j: (0, j)),
                  pl.BlockSpec((tm, tn), lambda i, j: (i, j))],
        out_specs=pl.BlockSpec((tm, tn), lambda i, j: (i, j)),
        out_shape=jax.ShapeDtypeStruct((m, n), BF16),
        scratch_shapes=[pltpu.VMEM((tm, k), BF16)],
        compiler_params=_cp(("parallel", "arbitrary"), 56), name="s5_glu")(u, yf, yb, dvec, wv, wg, gates)


def _merge_kernel(a_ref, w_ref, gate_ref, pa_ref, o_ref):
    br = jnp.dot(a_ref[...], w_ref[...], preferred_element_type=F32)
    o_ref[...] = (pa_ref[...].astype(F32) + _sigmoid(gate_ref[...].astype(F32)) * br).astype(o_ref.dtype)


def _merge_call(a, w, gates, part_a):
    m, k = a.shape
    n = w.shape[1]
    tm, tn = _mm_tiles(m, n)
    off = n // tn
    return pl.pallas_call(
        _merge_kernel, grid=(m // tm, n // tn),
        in_specs=[pl.BlockSpec((tm, k), lambda i, j: (i, 0)),
                  pl.BlockSpec((k, tn), lambda i, j: (0, j)),
                  pl.BlockSpec((tm, tn), lambda i, j: (i, j + off)),
                  pl.BlockSpec((tm, tn), lambda i, j: (i, j))],
        out_specs=pl.BlockSpec((tm, tn), lambda i, j: (i, j)),
        out_shape=jax.ShapeDtypeStruct((m, n), BF16),
        compiler_params=_cp(("parallel", "parallel"), 48), name="ssd_out_merge")(a, w, gates, part_a)


def _resid_kernel(a_ref, w_ref, x_ref, g_ref, o_ref):
    mix = jnp.dot(a_ref[...], w_ref[...], preferred_element_type=F32)
    o_ref[...] = x_ref[...] + g_ref[...] * mix


def _resid_call(a, w, x, mods, ig):
    b, l, d = x.shape
    k = a.shape[1]
    tm, tn = _mm_tiles(l, d)
    nt = l // tm
    return pl.pallas_call(
        _resid_kernel, grid=(b * nt, d // tn),
        in_specs=[pl.BlockSpec((tm, k), lambda i, j: (i, 0)),
                  pl.BlockSpec((k, tn), lambda i, j: (0, j)),
                  pl.BlockSpec((None, tm, tn), lambda i, j: (i // nt, i % nt, j)),
                  pl.BlockSpec((None, None, 1, tn), lambda i, j: (i // nt, ig, 0, j))],
        out_specs=pl.BlockSpec((None, tm, tn), lambda i, j: (i // nt, i % nt, j)),
        out_shape=jax.ShapeDtypeStruct((b, l, d), F32),
        compiler_params=_cp(("parallel", "parallel"), 48), name="w_o_resid")(a, w, x, mods)


def _s5_kernel(*refs, nb, t, p8, need_y):
    if need_y:
        uf_ref, ub_ref, b_ref, c_ref, lre_ref, lim_ref, h0_ref, yf_ref, yb_ref, h_ref, bu_ref = refs
    else:
        uf_ref, ub_ref, b_ref, c_ref, lre_ref, lim_ref, h0_ref, h_ref, bu_ref = refs
    q = 2 * nb
    nk = p8 // LANE
    pitch = t + S5_ROW_PAD

    @pl.when(pl.program_id(1) == 0)
    def _():
        h_ref[...] = h0_ref[...]

    rev = (lax.broadcasted_iota(jnp.int32, (t, t), 0) + lax.broadcasted_iota(jnp.int32, (t, t), 1)
           == t - 1).astype(BF16)
    for d in range(2):
        if d == 0:
            u = uf_ref[...].reshape(nb * t, LANE)
        else:
            u = jnp.concatenate([jnp.dot(rev, ub_ref[b], preferred_element_type=F32).astype(BF16)
                                 for b in range(nb)], axis=0)
        bu = jnp.dot(u, b_ref[d], preferred_element_type=F32)
        for b in range(nb):
            r0 = (d * nb + b) * pitch
            for k in range(2 * nk):
                bu_ref[k, r0:r0 + t, :] = bu[b * t:(b + 1) * t, k * LANE:(k + 1) * LANE]
    ar = [lre_ref[:, k * LANE:(k + 1) * LANE] for k in range(nk)]
    ai = [lim_ref[:, k * LANE:(k + 1) * LANE] for k in range(nk)]

    def step(s, carry):
        rows = pl.ds(s, q, stride=pitch)
        out = []
        for k in range(nk):
            hr, hi = carry[k]
            nr = ar[k] * hr - ai[k] * hi + bu_ref[k, rows, :]
            ni = ar[k] * hi + ai[k] * hr + bu_ref[nk + k, rows, :]
            bu_ref[k, rows, :] = nr
            bu_ref[nk + k, rows, :] = ni
            out.append((nr, ni))
        return tuple(out)

    init = tuple((h_ref[0, :, k * LANE:(k + 1) * LANE], h_ref[1, :, k * LANE:(k + 1) * LANE]) for k in range(nk))
    fin = lax.fori_loop(0, t, step, init, unroll=8)
    for k in range(nk):
        h_ref[0, :, k * LANE:(k + 1) * LANE] = fin[k][0]
        h_ref[1, :, k * LANE:(k + 1) * LANE] = fin[k][1]

    if need_y:
        for d in range(2):
            h = jnp.concatenate(
                [jnp.concatenate([bu_ref[k, (d * nb + b) * pitch:(d * nb + b) * pitch + t, :].astype(BF16)
                                  for k in range(2 * nk)], axis=1) for b in range(nb)], axis=0)
            y = jnp.dot(h, c_ref[d], preferred_element_type=F32).astype(BF16)
            for b in range(nb):
                yb = y[b * t:(b + 1) * t]
                if d == 0:
                    yf_ref[b] = yb
                else:
                    yb_ref[b] = jnp.dot(rev, yb, preferred_element_type=F32).astype(BF16)


def _s5_call(u, bmat, cmat, lre, lim, h0, need_y):
    nb, l, w5 = u.shape
    nj = w5 // LANE
    q = 2 * nb
    t = S5_CHUNK
    p8 = lre.shape[-1]
    nc = l // t
    fwd = lambda j, c: (0, c, j)
    bwd = lambda j, c: (0, nc - 1 - c, j)
    hspec = pl.BlockSpec((None, 2, q, p8), lambda j, c: (j, 0, 0, 0))
    in_specs = [pl.BlockSpec((nb, t, LANE), fwd), pl.BlockSpec((nb, t, LANE), bwd),
                pl.BlockSpec((None, 2, LANE, 2 * p8), lambda j, c: (j, 0, 0, 0)),
                pl.BlockSpec((None, 2, 2 * p8, LANE), lambda j, c: (j, 0, 0, 0)),
                pl.BlockSpec((None, q, p8), lambda j, c: (j, 0, 0)),
                pl.BlockSpec((None, q, p8), lambda j, c: (j, 0, 0)),
                hspec]
    out_specs = [hspec]
    out_shape = [jax.ShapeDtypeStruct((nj, 2, q, p8), F32)]
    if need_y:
        out_specs = [pl.BlockSpec((nb, t, LANE), fwd), pl.BlockSpec((nb, t, LANE), bwd)] + out_specs
        out_shape = [jax.ShapeDtypeStruct((nb, l, w5), BF16)] * 2 + out_shape
    kern = functools.partial(_s5_kernel, nb=nb, t=t, p8=p8, need_y=need_y)
    return pl.pallas_call(
        kern, grid=(nj, nc), in_specs=in_specs, out_specs=out_specs, out_shape=out_shape,
        scratch_shapes=[pltpu.VMEM((2 * p8 // LANE, q * (t + S5_ROW_PAD), LANE), F32)],
        compiler_params=_cp(("parallel", "arbitrary"), 40),
        name="s5_scan" if need_y else "s5_scan_ctx")(u, u, bmat, cmat, lre, lim, h0)


def _s5_params(lam_re, lam_im, log_dt, b_re, b_im, c_re, c_im, nb):
    _, g, p = lam_re.shape
    s = b_re.shape[-1]
    gpb = LANE // s
    nj = g // gpb
    lam = lax.complex(lam_re.astype(F32), lam_im.astype(F32))
    lam_bar = jnp.exp(lam * jnp.exp(log_dt.astype(F32))[..., None])
    b_bar = ((lam_bar - 1.0) / lam)[..., None] * lax.complex(b_re.astype(F32), b_im.astype(F32))
    eye = jnp.eye(gpb, dtype=F32)

    def bmat_of(bpart):
        bb = bpart.reshape(2, nj, gpb, p, s)
        m = jnp.einsum('dnkps,kl->dnkslp', bb, eye)
        return m.reshape(2, nj, gpb * s, gpb * p).transpose(1, 0, 2, 3)

    bmat = jnp.concatenate([bmat_of(b_bar.real), bmat_of(b_bar.imag)], axis=-1).astype(BF16)

    def cmat_of(cpart):
        cc = cpart.astype(F32).reshape(2, nj, gpb, s, p)
        m = jnp.einsum('dnksp,kl->dnkpls', cc, eye)
        return m.reshape(2, nj, gpb * p, gpb * s).transpose(1, 0, 2, 3)

    cmat = jnp.concatenate([cmat_of(c_re), -cmat_of(c_im)], axis=-2).astype(BF16)

    def lam_of(part):
        v = part.reshape(2, nj, gpb * p).transpose(1, 0, 2)
        return jnp.repeat(v, nb, axis=1)

    return bmat, cmat, lam_of(lam_bar.real), lam_of(lam_bar.imag)


def _conv_kernel(x_ref, w_ref, b_ref, o_ref, *, l):
    x = x_ref[...].astype(F32)
    rows = lax.broadcasted_iota(jnp.int32, x.shape, 0)
    half = SSD_CONV // 2
    acc = x * w_ref[half:half + 1, :] + b_ref[...]
    for k in range(SSD_CONV):
        if k == half:
            continue
        off = k - half
        xs = pltpu.roll(x, shift=(-off) % l, axis=0)
        valid = jnp.logical_and(rows + off >= 0, rows + off < l)
        acc = acc + jnp.where(valid, xs, 0.0) * w_ref[k:k + 1, :]
    o_ref[...] = _silu(acc).astype(o_ref.dtype)


def _conv_call(proj, w, bias, col0):
    b, l, _ = proj.shape
    c = w.shape[1]
    tc = 256
    off = col0 // tc
    return pl.pallas_call(
        functools.partial(_conv_kernel, l=l), grid=(b, c // tc),
        in_specs=[pl.BlockSpec((None, l, tc), lambda i, j: (i, 0, j + off)),
                  pl.BlockSpec((SSD_CONV, tc), lambda i, j: (0, j)),
                  pl.BlockSpec((1, tc), lambda i, j: (0, j))],
        out_specs=pl.BlockSpec((None, l, tc), lambda i, j: (i, 0, j)),
        out_shape=jax.ShapeDtypeStruct((b, l, c), BF16),
        compiler_params=_cp(("parallel", "parallel"), 40), name="ssd_conv")(proj, w, bias.reshape(1, c))


def _softplus(x):
    return jnp.maximum(x, 0.0) + jnp.log1p(jnp.exp(-jnp.abs(x)))


def _ssd_dir(xbc_ref, dt_ref, y_ref, h_ref, bias, a_neg, d, *, nh, ng, need_y):
    qn = SSD_CHUNK
    hd = SSD_HEAD_DIM
    w = nh * hd
    gw = w // ng
    gn = ng * SSD_STATE
    ii = lax.broadcasted_iota(jnp.int32, (qn, qn), 0)
    jj = lax.broadcasted_iota(jnp.int32, (qn, qn), 1)
    mask = (jj <= ii) if d == 0 else (jj >= ii)
    lmat = mask.astype(F32)
    dtv = _softplus(dt_ref[...] + bias)
    cum = jnp.dot(lmat, dtv * a_neg, precision=lax.Precision.HIGHEST, preferred_element_type=F32)
    cum_t = cum.T
    edge = qn - 1 if d == 0 else 0
    tot = cum[edge:edge + 1, :]
    to_edge = jnp.exp(tot - cum)
    ecum = jnp.exp(cum)
    decay = jnp.exp(tot)
    lane = lax.broadcasted_iota(jnp.int32, (qn, LANE), 1)
    left = lane < hd
    for g in range(ng):
        bg = xbc_ref[:, w + g * SSD_STATE:w + (g + 1) * SSD_STATE]
        cg = xbc_ref[:, w + gn + g * SSD_STATE:w + gn + (g + 1) * SSD_STATE]
        s_in = h_ref[d, :, g * gw:(g + 1) * gw]
        if need_y:
            cb = lax.dot_general(cg, bg, (((1,), (1,)), ((), ())), preferred_element_type=F32)
            yoff = jnp.dot(cg, s_in.astype(BF16), preferred_element_type=F32)
        xw_parts = []
        dec_parts = []
        for pr in range(gw // LANE):
            c0 = d * nh + (g * gw) // hd + 2 * pr
            col = g * gw + pr * LANE
            xp = xbc_ref[:, col:col + LANE].astype(F32)
            xdt = xp * jnp.where(left, dtv[:, c0:c0 + 1], dtv[:, c0 + 1:c0 + 2])
            xw_parts.append((xdt * jnp.where(left, to_edge[:, c0:c0 + 1], to_edge[:, c0 + 1:c0 + 2])).astype(BF16))
            dec_parts.append(jnp.where(left[0:1], decay[:, c0:c0 + 1], decay[:, c0 + 1:c0 + 2]))
            if need_y:
                seg0 = jnp.where(mask, jnp.exp(cum[:, c0:c0 + 1] - cum_t[c0:c0 + 1, :]), 0.0)
                seg1 = jnp.where(mask, jnp.exp(cum[:, c0 + 1:c0 + 2] - cum_t[c0 + 1:c0 + 2, :]), 0.0)
                m = jnp.concatenate([cb * seg0, cb * seg1], axis=1).astype(BF16)
                r = jnp.concatenate([jnp.where(left, xdt, 0.0), jnp.where(left, 0.0, xdt)], axis=0).astype(BF16)
                yd = jnp.dot(m, r, preferred_element_type=F32)
                ec = jnp.where(left, ecum[:, c0:c0 + 1], ecum[:, c0 + 1:c0 + 2])
                y_ref[:, col:col + LANE] = (yd + yoff[:, pr * LANE:(pr + 1) * LANE] * ec).astype(y_ref.dtype)
        xw = jnp.concatenate(xw_parts, axis=1)
        dec = jnp.concatenate(dec_parts, axis=1)
        s_new = lax.dot_general(bg, xw, (((0,), (0,)), ((), ())), preferred_element_type=F32)
        h_ref[d, :, g * gw:(g + 1) * gw] = s_in * dec + s_new


def _ssd_kernel(*refs, nh, ng, need_y):
    if need_y:
        xf_ref, xb_ref, dtf_ref, dtb_ref, h0_ref, bias_ref, alog_ref, yf_ref, yb_ref, h_ref = refs
    else:
        xf_ref, xb_ref, dtf_ref, dtb_ref, h0_ref, bias_ref, alog_ref, h_ref = refs
        yf_ref = yb_ref = None

    @pl.when(pl.program_id(1) == 0)
    def _():
        h_ref[...] = h0_ref[...]

    bias = bias_ref[...]
    a_neg = -jnp.exp(alog_ref[...])
    _ssd_dir(xf_ref, dtf_ref, yf_ref, h_ref, bias, a_neg, 0, nh=nh, ng=ng, need_y=need_y)
    _ssd_dir(xb_ref, dtb_ref, yb_ref, h_ref, bias, a_neg, 1, nh=nh, ng=ng, need_y=need_y)


def _ssd_call(xbc, dt, h0, bias, alog, nh, ng, need_y):
    b, l, cd = xbc.shape
    w = nh * SSD_HEAD_DIM
    nc = l // SSD_CHUNK
    q = SSD_CHUNK
    fwd = lambda i, s: (i, s, 0)
    bwd = lambda i, s: (i, nc - 1 - s, 0)
    hspec = pl.BlockSpec((None, 2, SSD_STATE, w), lambda i, s: (i, 0, 0, 0))
    in_specs = [pl.BlockSpec((None, q, cd), fwd), pl.BlockSpec((None, q, cd), bwd),
                pl.BlockSpec((None, q, LANE), fwd), pl.BlockSpec((None, q, LANE), bwd),
                hspec,
                pl.BlockSpec((1, LANE), lambda i, s: (0, 0)), pl.BlockSpec((1, LANE), lambda i, s: (0, 0))]
    out_specs = [hspec]
    out_shape = [jax.ShapeDtypeStruct((b, 2, SSD_STATE, w), F32)]
    if need_y:
        out_specs = [pl.BlockSpec((None, q, w), fwd), pl.BlockSpec((None, q, w), bwd)] + out_specs
        out_shape = [jax.ShapeDtypeStruct((b, l, w), BF16)] * 2 + out_shape
    return pl.pallas_call(
        functools.partial(_ssd_kernel, nh=nh, ng=ng, need_y=need_y), grid=(b, nc),
        in_specs=in_specs, out_specs=out_specs, out_shape=out_shape,
        compiler_params=_cp(("parallel", "arbitrary"), 48),
        name="ssd_scan" if need_y else "ssd_scan_ctx")(xbc, xbc, dt, dt, h0, bias, alog)


def _gnorm_kernel(yf_ref, yb_ref, xs_ref, z_ref, d_ref, g_ref, o_ref, *, ng, rows):
    y = d_ref[...] * xs_ref[...].astype(F32) + yf_ref[...].astype(F32) + yb_ref[...].astype(F32)
    y = y * _silu(z_ref[...].astype(F32))
    w = y.shape[1]
    gw = w // ng
    parts = []
    for g in range(ng):
        yg = y[:, g * gw:(g + 1) * gw]
        ms = jnp.mean(yg * yg, axis=-1, keepdims=True)
        parts.append(yg * lax.rsqrt(ms + EPS))
    res = (jnp.concatenate(parts, axis=1) * g_ref[...]).astype(o_ref.dtype)
    for k in range(SSD_CHUNK // rows):
        o_ref[:, k * w:(k + 1) * w] = res[k * rows:(k + 1) * rows, :]


def _gnorm_call(yf, yb, xbc, proj, dvec, gvec, ng):
    b, l, w = yf.shape
    rows = l // GRID_W
    nw = SSD_CHUNK // rows
    blk = lambda i, s: (i, s, 0)
    out = pl.pallas_call(
        functools.partial(_gnorm_kernel, ng=ng, rows=rows), grid=(b, l // SSD_CHUNK),
        in_specs=[pl.BlockSpec((None, SSD_CHUNK, w), blk), pl.BlockSpec((None, SSD_CHUNK, w), blk),
                  pl.BlockSpec((None, SSD_CHUNK, w), blk), pl.BlockSpec((None, SSD_CHUNK, w), blk),
                  pl.BlockSpec((1, w), lambda i, s: (0, 0)), pl.BlockSpec((1, w), lambda i, s: (0, 0))],
        out_specs=pl.BlockSpec((None, rows, nw * w), lambda i, s: (i, 0, s)),
        out_shape=jax.ShapeDtypeStruct((b, rows, GRID_W * w), BF16),
        compiler_params=_cp(("parallel", "parallel"), 40), name="ssd_gnorm")(yf, yb, xbc, proj, dvec, gvec)
    return out.reshape(b * l, w)


def _moe_kernel(be_ref, nu_ref, x_ref, wg_ref, wu_ref, wd_ref, o_ref, acc_ref):
    i = pl.program_id(0)
    h = pl.program_id(1)
    nh = pl.num_programs(1)
    used = i < nu_ref[0]

    @pl.when(used)
    def _():
        x = x_ref[...]
        gt = jnp.dot(x, wg_ref[...].astype(BF16), preferred_element_type=F32)
        up = jnp.dot(x, wu_ref[...].astype(BF16), preferred_element_type=F32)
        act = (_silu(gt) * up).astype(BF16)
        contrib = jnp.dot(act, wd_ref[...].astype(BF16), preferred_element_type=F32)

        @pl.when(h == 0)
        def _():
            acc_ref[...] = contrib

        @pl.when(h > 0)
        def _():
            acc_ref[...] += contrib

    @pl.when(jnp.logical_not(used))
    def _():
        acc_ref[...] = jnp.zeros_like(acc_ref)

    @pl.when(h == nh - 1)
    def _():
        o_ref[...] = acc_ref[...].astype(o_ref.dtype)


def _moe_call(block_e, n_used, xs, w_gate, w_up, w_down):
    n_slots, d = xs.shape
    hid = w_gate.shape[-1]
    th = min(256, hid)
    nblk = n_slots // MOE_BLOCK
    gs = pltpu.PrefetchScalarGridSpec(
        num_scalar_prefetch=2, grid=(nblk, hid // th),
        in_specs=[pl.BlockSpec((MOE_BLOCK, d), lambda i, h, be, nu: (i, 0)),
                  pl.BlockSpec((None, d, th), lambda i, h, be, nu: (be[i], 0, h)),
                  pl.BlockSpec((None, d, th), lambda i, h, be, nu: (be[i], 0, h)),
                  pl.BlockSpec((None, th, d), lambda i, h, be, nu: (be[i], h, 0))],
        out_specs=pl.BlockSpec((MOE_BLOCK, d), lambda i, h, be, nu: (i, 0)),
        scratch_shapes=[pltpu.VMEM((MOE_BLOCK, d), F32)])
    return pl.pallas_call(
        _moe_kernel, grid_spec=gs, out_shape=jax.ShapeDtypeStruct((n_slots, d), BF16),
        compiler_params=_cp(("parallel", "arbitrary"), 56), name="moe_experts")(
            block_e, n_used, xs, w_gate, w_up, w_down)


def _final_kernel(x_ref, ya_ref, yb_ref, w_ref, g2_ref, fg_ref, o_ref):
    wts = w_ref[...]
    moe = ya_ref[...].astype(F32) * wts[:, 0:1] + yb_ref[...].astype(F32) * wts[:, 1:2]
    x = x_ref[...] + g2_ref[...] * moe
    ms = jnp.mean(x * x, axis=-1, keepdims=True)
    o_ref[...] = x * lax.rsqrt(ms + EPS) * fg_ref[...]


def _final_call(x, ya, yb, wts, mods, ig, fg):
    b, l, d = x.shape
    tr = min(256, l)
    nt = l // tr
    row = lambda i, r: (i * nt + r, 0)
    return pl.pallas_call(
        _final_kernel, grid=(b, nt),
        in_specs=[pl.BlockSpec((None, tr, d), lambda i, r: (i, r, 0)),
                  pl.BlockSpec((tr, d), row), pl.BlockSpec((tr, d), row),
                  pl.BlockSpec((tr, MOE_TOP_K), row),
                  pl.BlockSpec((None, None, 1, d), lambda i, r: (i, ig, 0, 0)),
                  pl.BlockSpec((1, d), lambda i, r: (0, 0))],
        out_specs=pl.BlockSpec((None, tr, d), lambda i, r: (i, r, 0)),
        out_shape=jax.ShapeDtypeStruct((b, l, d), F32),
        compiler_params=_cp(("parallel", "parallel"), 40), name="combine_final")(x, ya, yb, wts, mods, fg)


def _route(logits, b_group, b_expert):
    m = logits.shape[0]
    ngr = b_group.shape[0]
    ne = b_expert.shape[0]
    epg = ne // ngr
    g_prob = jax.nn.softmax(logits[:, :ngr] + b_group.astype(F32), axis=-1)
    g_p, g_idx = lax.top_k(g_prob, 1)
    e_logits = (logits[:, ngr:ngr + ne] + b_expert.astype(F32)).reshape(m, ngr, epg)
    e_logits = jnp.take_along_axis(e_logits, g_idx[:, :, None], axis=1)[:, 0]
    e_top, e_idx = lax.top_k(e_logits, MOE_TOP_K)
    weights = g_p * jax.nn.softmax(e_top, axis=-1)
    experts = g_idx * epg + e_idx
    n_assign = m * MOE_TOP_K
    flat_e = experts.reshape(-1)
    order = jnp.argsort(flat_e)
    sorted_e = flat_e[order]
    counts = jnp.zeros((ne,), jnp.int32).at[flat_e].add(1)
    padded = (counts + MOE_BLOCK - 1) // MOE_BLOCK * MOE_BLOCK
    pad_end = jnp.cumsum(padded)
    pad_start = pad_end - padded
    start = jnp.cumsum(counts) - counts
    dest = pad_start[sorted_e] + jnp.arange(n_assign, dtype=jnp.int32) - start[sorted_e]
    n_blocks = -(-(n_assign + ne * (MOE_BLOCK - 1)) // MOE_BLOCK)
    tok_of_sorted = (order // MOE_TOP_K).astype(jnp.int32)
    n_slots = n_blocks * MOE_BLOCK
    slot_tok = (jnp.arange(n_slots, dtype=jnp.int32) % m).at[dest].set(tok_of_sorted)
    block_e = jnp.minimum(
        jnp.searchsorted(pad_end, jnp.arange(n_blocks, dtype=jnp.int32) * MOE_BLOCK, side='right'),
        ne - 1).astype(jnp.int32)
    n_used = (pad_end[-1] // MOE_BLOCK).astype(jnp.int32).reshape(1)
    pos = jnp.zeros((n_assign,), jnp.int32).at[order].set(dest.astype(jnp.int32)).reshape(m, MOE_TOP_K)
    return weights, slot_tok, block_e, n_used, pos


def kernel(x, c, ctx, c_ctx, w_mod, b_mod, norm1_g, w_in, s5_lam_re, s5_lam_im, s5_log_dt, s5_b_re, s5_b_im, s5_c_re, s5_c_im, s5_d, s5_w_val, s5_w_gate, ssd_conv_w, ssd_conv_b, ssd_a_log, ssd_dt_bias, ssd_d, ssd_norm_g, ssd_w_out, w_o, norm2_g, moe_w_group, moe_b_group, moe_w_expert, moe_b_expert, moe_w_gate, moe_w_up, moe_w_down, final_g):
    depth = w_mod.shape[0]
    assert depth == 1, "single-layer block"
    bsz, n_lat, d = x.shape
    l_ctx = ctx.shape[1]
    w5 = s5_d.shape[1]
    nh = ssd_d.shape[1]
    w = nh * SSD_HEAD_DIM
    conv_dim = ssd_conv_w.shape[2]
    ng = (conv_dim - w) // (2 * SSD_STATE)
    ssd_in = w + conv_dim + 2 * nh
    o1, o2 = w5, w5 + ssd_in
    l = 0

    cc = jnp.concatenate([c, c_ctx[None, :]], axis=0)
    cc = jnp.pad(cc, ((0, (-cc.shape[0]) % 8), (0, 0)))
    mods = _mod_call(cc, w_mod[l], b_mod[l]).reshape(cc.shape[0], 6, 1, d)
    i_sh1, i_sc1, i_g1, i_sh2, i_sc2, i_g2 = range(6)

    w_s5 = w_in[l][:, :o1].astype(BF16)
    w_ssd = w_in[l][:, o1:o1 + w + conv_dim].astype(BF16)
    w_dt = jnp.pad(w_in[l][:, o1 + w + conv_dim:o2], ((0, 0), (0, LANE - 2 * nh))).astype(BF16)
    w_gates = w_in[l][:, o2:].astype(BF16)

    hn_rm, hn_cm = _norm_lat_call(x, norm1_g[l], mods, i_sc1, i_sh1)
    hc = _norm_ctx_call(ctx, norm1_g[l], mods, bsz, i_sc1, i_sh1)
    hn_rm = hn_rm.reshape(bsz * n_lat, d)
    hn_cm = hn_cm.reshape(bsz * n_lat, d)
    hc = hc.reshape(bsz * l_ctx, d)

    u_lat = _mm_call(hn_rm, w_s5, BF16, "in_s5").reshape(bsz, n_lat, w5)
    u_ctx = _mm_call(hc, w_s5, BF16, "in_s5_ctx").reshape(bsz, l_ctx, w5)
    gates = _mm_call(hn_rm, w_gates, BF16, "in_gates")
    p_lat = _mm_call(hn_cm, w_ssd, BF16, "in_ssd").reshape(bsz, n_lat, w + conv_dim)
    p_ctx = _mm_call(hc, w_ssd, BF16, "in_ssd_ctx").reshape(bsz, l_ctx, w + conv_dim)
    dt_lat = _mm_call(hn_cm, w_dt, F32, "in_dt").reshape(bsz, n_lat, LANE)
    dt_ctx = _mm_call(hc, w_dt, F32, "in_dt_ctx").reshape(bsz, l_ctx, LANE)

    nj = w5 // LANE
    bmat, cmat, lre, lim = _s5_params(s5_lam_re[l], s5_lam_im[l], s5_log_dt[l], s5_b_re[l], s5_b_im[l],
                                      s5_c_re[l], s5_c_im[l], bsz)
    s5_zero = jnp.zeros((nj, 2, 2 * bsz, lre.shape[-1]), F32)
    (s5_ctx,) = _s5_call(u_ctx, bmat, cmat, lre, lim, s5_zero, False)
    ya_f, ya_b, _ = _s5_call(u_lat, bmat, cmat, lre, lim, s5_ctx, True)

    xbc_ctx = _conv_call(p_ctx, ssd_conv_w[l], ssd_conv_b[l], w)
    xbc_lat = _conv_call(p_lat, ssd_conv_w[l], ssd_conv_b[l], w)
    pad_h = LANE - 2 * nh
    bias = jnp.pad(ssd_dt_bias[l].astype(F32).reshape(1, 2 * nh), ((0, 0), (0, pad_h)))
    alog = jnp.pad(ssd_a_log[l].astype(F32).reshape(1, 2 * nh), ((0, 0), (0, pad_h)))
    h_zero = jnp.zeros((bsz, 2, SSD_STATE, w), F32)
    (h_ctx,) = _ssd_call(xbc_ctx, dt_ctx, h_zero, bias, alog, nh, ng, False)
    y_f, y_b, _ = _ssd_call(xbc_lat, dt_lat, h_ctx, bias, alog, nh, ng, True)
    d_vec = jnp.repeat(ssd_d[l].astype(F32), SSD_HEAD_DIM).reshape(1, w)
    y_ssd = _gnorm_call(y_f, y_b, xbc_lat, p_lat, d_vec, ssd_norm_g[l].astype(F32).reshape(1, w), ng)

    m_lat = bsz * n_lat
    part_a = _glu_call(u_lat.reshape(m_lat, w5), ya_f.reshape(m_lat, w5), ya_b.reshape(m_lat, w5),
                       s5_d[l].astype(F32).reshape(1, w5),
                       s5_w_val[l].astype(BF16), s5_w_gate[l].astype(BF16), gates)
    merged = _merge_call(y_ssd, ssd_w_out[l].astype(BF16), gates, part_a)
    x1 = _resid_call(merged, w_o[l].astype(BF16), x, mods, i_g1)

    ngr = moe_w_group.shape[-1]
    ne = moe_w_expert.shape[-1]
    wr = jnp.concatenate([moe_w_group[l], moe_w_expert[l]], axis=1).astype(F32)
    wr = jnp.pad(wr, ((0, 0), (0, (-(ngr + ne)) % LANE)))
    hx, logits = _norm_router_call(x1, norm2_g[l], mods, i_sc2, i_sh2, wr)
    m = bsz * n_lat
    hx = hx.reshape(m, d)
    weights, slot_tok, block_e, n_used, pos = _route(logits.reshape(m, -1), moe_b_group[l], moe_b_expert[l])
    xs = hx[slot_tok]
    y_slots = _moe_call(block_e, n_used, xs, moe_w_gate[l], moe_w_up[l], moe_w_down[l])
    ya = y_slots[pos[:, 0]]
    yb = y_slots[pos[:, 1]]
    return _final_call(x1, ya, yb, weights.astype(F32), mods, i_g2, final_g.reshape(1, d))
```

```python
import functools
import math

import jax
import jax.numpy as jnp
from jax import lax
from jax.experimental import pallas as pl
from jax.experimental.pallas import tpu as pltpu

F32 = jnp.float32
BF16 = jnp.bfloat16

GRID_W = 64
EPS = 1e-6
LANE = 128
SSD_HEAD_DIM = 64
SSD_STATE = 128
SSD_CHUNK = 128
SSD_CONV = 5
MOE_TOP_K = 2
MOE_BLOCK = 256
MOE_SUB = 4
S5_CHUNK = 128
S5_ROW_PAD = 8


def _cp(sem, mb):
    return pltpu.CompilerParams(dimension_semantics=sem, vmem_limit_bytes=mb * 1024 * 1024)


def _sigmoid(x):
    return 1.0 / (1.0 + jnp.exp(-x))


def _silu(x):
    return x * _sigmoid(x)


def _mod_kernel(c_ref, w_ref, b_ref, o_ref):
    s = _silu(c_ref[...])
    o_ref[...] = jnp.dot(s.astype(BF16), w_ref[...].astype(BF16),
                         preferred_element_type=F32) + b_ref[...]


def _mod_call(cc, w, b):
    r, d = cc.shape
    n = w.shape[1]
    tn = min(512, n)
    return pl.pallas_call(
        _mod_kernel, grid=(n // tn,),
        in_specs=[pl.BlockSpec((r, d), lambda j: (0, 0)),
                  pl.BlockSpec((d, tn), lambda j: (0, j)),
                  pl.BlockSpec((1, tn), lambda j: (0, j))],
        out_specs=pl.BlockSpec((r, tn), lambda j: (0, j)),
        out_shape=jax.ShapeDtypeStruct((r, n), F32),
        compiler_params=_cp(("parallel",), 40), name="mod")(cc, w, b.reshape(1, n))


def _rmsmod(x, g, sc, sh):
    ms = jnp.mean(x * x, axis=-1, keepdims=True)
    return (x * lax.rsqrt(ms + EPS) * g) * (1.0 + sc) + sh


def _norm_lat_kernel(x_ref, g_ref, sc_ref, sh_ref, orm_ref, ocm_ref, *, rb, d):
    for r in range(rb):
        y = _rmsmod(x_ref[r], g_ref[...], sc_ref[...], sh_ref[...]).astype(BF16)
        orm_ref[r] = y
        ocm_ref[:, r * d:(r + 1) * d] = y


def _norm_lat_call(x, g, mods, isc, ish):
    b, l, d = x.shape
    rows = l // GRID_W
    rb = min(4, rows)
    xv = x.reshape(b, rows, GRID_W, d)
    orm, ocm = pl.pallas_call(
        functools.partial(_norm_lat_kernel, rb=rb, d=d), grid=(b, rows // rb),
        in_specs=[pl.BlockSpec((None, rb, GRID_W, d), lambda i, r: (i, r, 0, 0)),
                  pl.BlockSpec((1, d), lambda i, r: (0, 0)),
                  pl.BlockSpec((None, None, 1, d), lambda i, r: (i, isc, 0, 0)),
                  pl.BlockSpec((None, None, 1, d), lambda i, r: (i, ish, 0, 0))],
        out_specs=[pl.BlockSpec((None, rb, GRID_W, d), lambda i, r: (i, r, 0, 0)),
                   pl.BlockSpec((None, GRID_W, rb * d), lambda i, r: (i, 0, r))],
        out_shape=[jax.ShapeDtypeStruct((b, rows, GRID_W, d), BF16),
                   jax.ShapeDtypeStruct((b, GRID_W, rows * d), BF16)],
        compiler_params=_cp(("parallel", "parallel"), 48), name="norm1_lat")(xv, g.reshape(1, d), mods, mods)
    return orm.reshape(b, l, d), ocm.reshape(b, l, d)


def _norm_kernel(x_ref, g_ref, sc_ref, sh_ref, o_ref):
    o_ref[...] = _rmsmod(x_ref[...], g_ref[...], sc_ref[...], sh_ref[...]).astype(o_ref.dtype)


def _norm_ctx_call(x, g, mods, row, isc, ish):
    b, l, d = x.shape
    tr = min(256, l)
    return pl.pallas_call(
        _norm_kernel, grid=(b, l // tr),
        in_specs=[pl.BlockSpec((None, tr, d), lambda i, r: (i, r, 0)),
                  pl.BlockSpec((1, d), lambda i, r: (0, 0)),
                  pl.BlockSpec((None, None, 1, d), lambda i, r: (row, isc, 0, 0)),
                  pl.BlockSpec((None, None, 1, d), lambda i, r: (row, ish, 0, 0))],
        out_specs=pl.BlockSpec((None, tr, d), lambda i, r: (i, r, 0)),
        out_shape=jax.ShapeDtypeStruct((b, l, d), BF16),
        compiler_params=_cp(("parallel", "parallel"), 40), name="norm1_ctx")(x, g.reshape(1, d), mods, mods)


def _norm_router_kernel(x_ref, g_ref, sc_ref, sh_ref, wr_ref, o_ref, lg_ref):
    y = _rmsmod(x_ref[...], g_ref[...], sc_ref[...], sh_ref[...])
    o_ref[...] = y.astype(BF16)
    lg_ref[...] = jnp.dot(y, wr_ref[...], precision=lax.Precision.HIGHEST, preferred_element_type=F32)


def _norm_router_call(x, g, mods, isc, ish, wr):
    b, l, d = x.shape
    tr = min(256, l)
    nr = wr.shape[1]
    return pl.pallas_call(
        _norm_router_kernel, grid=(b, l // tr),
        in_specs=[pl.BlockSpec((None, tr, d), lambda i, r: (i, r, 0)),
                  pl.BlockSpec((1, d), lambda i, r: (0, 0)),
                  pl.BlockSpec((None, None, 1, d), lambda i, r: (i, isc, 0, 0)),
                  pl.BlockSpec((None, None, 1, d), lambda i, r: (i, ish, 0, 0)),
                  pl.BlockSpec((d, nr), lambda i, r: (0, 0))],
        out_specs=[pl.BlockSpec((None, tr, d), lambda i, r: (i, r, 0)),
                   pl.BlockSpec((None, tr, nr), lambda i, r: (i, r, 0))],
        out_shape=[jax.ShapeDtypeStruct((b, l, d), BF16), jax.ShapeDtypeStruct((b, l, nr), F32)],
        compiler_params=_cp(("parallel", "parallel"), 40), name="norm2_router")(x, g.reshape(1, d), mods, mods, wr)


def _mm_kernel(a_ref, b_ref, o_ref):
    o_ref[...] = jnp.dot(a_ref[...], b_ref[...], preferred_element_type=F32).astype(o_ref.dtype)


def _mm_tiles(m, n):
    tm = min(1024, m)
    tn = min(512, n)
    return tm, tn


def _mm_call(a, b, out_dtype, name, col0=0, n=None):
    m, k = a.shape
    n = b.shape[1] - col0 if n is None else n
    tm, tn = _mm_tiles(m, n)
    if col0 % tn or n % tn or n % LANE:
        b = b[:, col0:col0 + n]
        pad = (-n) % LANE
        b = jnp.pad(b, ((0, 0), (0, pad)))
        n, col0 = n + pad, 0
        tm, tn = _mm_tiles(m, n)
    off = col0 // tn
    return pl.pallas_call(
        _mm_kernel, grid=(m // tm, n // tn),
        in_specs=[pl.BlockSpec((tm, k), lambda i, j: (i, 0)),
                  pl.BlockSpec((k, tn), lambda i, j: (0, j + off))],
        out_specs=pl.BlockSpec((tm, tn), lambda i, j: (i, j)),
        out_shape=jax.ShapeDtypeStruct((m, n), out_dtype),
        compiler_params=_cp(("parallel", "parallel"), 48), name=name)(a, b)


def _gelu_tanh(x):
    return x * (0.5 * (1.0 + jnp.tanh(math.sqrt(2.0 / math.pi) * (x + 0.044715 * (x * x * x)))))


def _glu_kernel(u_ref, yf_ref, yb_ref, d_ref, wv_ref, wg_ref, gate_ref, o_ref, a_ref):
    @pl.when(pl.program_id(1) == 0)
    def _():
        y = d_ref[...] * u_ref[...].astype(F32) + yf_ref[...].astype(F32) + yb_ref[...].astype(F32)
        a_ref[...] = _gelu_tanh(y).astype(BF16)

    a = a_ref[...]
    val = jnp.dot(a, wv_ref[...], preferred_element_type=F32)
    gl = jnp.dot(a, wg_ref[...], preferred_element_type=F32)
    o_ref[...] = (_sigmoid(gate_ref[...].astype(F32)) * (val * _sigmoid(gl))).astype(o_ref.dtype)


def _glu_call(u, yf, yb, dvec, wv, wg, gates):
    m, k = u.shape
    n = wv.shape[1]
    tm, tn = _mm_tiles(m, n)
    row = pl.BlockSpec((tm, k), lambda i, j: (i, 0))
    return pl.pallas_call(
        _glu_kernel, grid=(m // tm, n // tn),
        in_specs=[row, row, row,
                  pl.BlockSpec((1, k), lambda i, j: (0, 0)),
                  pl.BlockSpec((k, tn), lambda i, j: (0, j)),
                  pl.BlockSpec((k, tn), lambda i, j: (0, j)),
                  pl.BlockSpec((tm, tn), lambda i, j: (i, j))],
        out_specs=pl.BlockSpec((tm, tn), lambda i, j: (i, j)),
        out_shape=jax.ShapeDtypeStruct((m, n), BF16),
        scratch_shapes=[pltpu.VMEM((tm, k), BF16)],
        compiler_params=_cp(("parallel", "arbitrary"), 56), name="s5_glu")(u, yf, yb, dvec, wv, wg, gates)


def _merge_kernel(a_ref, w_ref, gate_ref, pa_ref, o_ref):
    br = jnp.dot(a_ref[...], w_ref[...], preferred_element_type=F32)
    o_ref[...] = (pa_ref[...].astype(F32) + _sigmoid(gate_ref[...].astype(F32)) * br).astype(o_ref.dtype)


def _merge_call(a, w, gates, part_a):
    m, k = a.shape
    n = w.shape[1]
    tm, tn = _mm_tiles(m, n)
    off = n // tn
    return pl.pallas_call(
        _merge_kernel, grid=(m // tm, n // tn),
        in_specs=[pl.BlockSpec((tm, k), lambda i, j: (i, 0)),
                  pl.BlockSpec((k, tn), lambda i, j: (0, j)),
                  pl.BlockSpec((tm, tn), lambda i, j: (i, j + off)),
                  pl.BlockSpec((tm, tn), lambda i, j: (i, j))],
        out_specs=pl.BlockSpec((tm, tn), lambda i, j: (i, j)),
        out_shape=jax.ShapeDtypeStruct((m, n), BF16),
        compiler_params=_cp(("parallel", "parallel"), 48), name="ssd_out_merge")(a, w, gates, part_a)


def _resid_kernel(a_ref, w_ref, x_ref, g_ref, o_ref):
    mix = jnp.dot(a_ref[...], w_ref[...], preferred_element_type=F32)
    o_ref[...] = x_ref[...] + g_ref[...] * mix


def _resid_call(a, w, x, mods, ig):
    b, l, d = x.shape
    k = a.shape[1]
    tm, tn = _mm_tiles(l, d)
    nt = l // tm
    return pl.pallas_call(
        _resid_kernel, grid=(b * nt, d // tn),
        in_specs=[pl.BlockSpec((tm, k), lambda i, j: (i, 0)),
                  pl.BlockSpec((k, tn), lambda i, j: (0, j)),
                  pl.BlockSpec((None, tm, tn), lambda i, j: (i // nt, i % nt, j)),
                  pl.BlockSpec((None, None, 1, tn), lambda i, j: (i // nt, ig, 0, j))],
        out_specs=pl.BlockSpec((None, tm, tn), lambda i, j: (i // nt, i % nt, j)),
        out_shape=jax.ShapeDtypeStruct((b, l, d), F32),
        compiler_params=_cp(("parallel", "parallel"), 48), name="w_o_resid")(a, w, x, mods)


def _s5_kernel(*refs, nb, t, p8, need_y):
    if need_y:
        uf_ref, ub_ref, b_ref, c_ref, lre_ref, lim_ref, h0_ref, yf_ref, yb_ref, h_ref, bu_ref = refs
    else:
        uf_ref, ub_ref, b_ref, c_ref, lre_ref, lim_ref, h0_ref, h_ref, bu_ref = refs
    q = 2 * nb
    nk = p8 // LANE
    pitch = t + S5_ROW_PAD

    @pl.when(pl.program_id(1) == 0)
    def _():
        h_ref[...] = h0_ref[...]

    rev = (lax.broadcasted_iota(jnp.int32, (t, t), 0) + lax.broadcasted_iota(jnp.int32, (t, t), 1)
           == t - 1).astype(BF16)
    for d in range(2):
        if d == 0:
            u = uf_ref[...].reshape(nb * t, LANE)
        else:
            u = jnp.concatenate([jnp.dot(rev, ub_ref[b], preferred_element_type=F32).astype(BF16)
                                 for b in range(nb)], axis=0)
        bu = jnp.dot(u, b_ref[d], preferred_element_type=F32)
        for b in range(nb):
            r0 = (d * nb + b) * pitch
            for k in range(2 * nk):
                bu_ref[k, r0:r0 + t, :] = bu[b * t:(b + 1) * t, k * LANE:(k + 1) * LANE]
    ar = [lre_ref[:, k * LANE:(k + 1) * LANE] for k in range(nk)]
    ai = [lim_ref[:, k * LANE:(k + 1) * LANE] for k in range(nk)]

    def step(s, carry):
        rows = pl.ds(s, q, stride=pitch)
        out = []
        for k in range(nk):
            hr, hi = carry[k]
            nr = ar[k] * hr - ai[k] * hi + bu_ref[k, rows, :]
            ni = ar[k] * hi + ai[k] * hr + bu_ref[nk + k, rows, :]
            bu_ref[k, rows, :] = nr
            bu_ref[nk + k, rows, :] = ni
            out.append((nr, ni))
        return tuple(out)

    init = tuple((h_ref[0, :, k * LANE:(k + 1) * LANE], h_ref[1, :, k * LANE:(k + 1) * LANE]) for k in range(nk))
    fin = lax.fori_loop(0, t, step, init, unroll=8)
    for k in range(nk):
        h_ref[0, :, k * LANE:(k + 1) * LANE] = fin[k][0]
        h_ref[1, :, k * LANE:(k + 1) * LANE] = fin[k][1]

    if need_y:
        for d in range(2):
            h = jnp.concatenate(
                [jnp.concatenate([bu_ref[k, (d * nb + b) * pitch:(d * nb + b) * pitch + t, :].astype(BF16)
                                  for k in range(2 * nk)], axis=1) for b in range(nb)], axis=0)
            y = jnp.dot(h, c_ref[d], preferred_element_type=F32).astype(BF16)
            for b in range(nb):
                yb = y[b * t:(b + 1) * t]
                if d == 0:
                    yf_ref[b] = yb
                else:
                    yb_ref[b] = jnp.dot(rev, yb, preferred_element_type=F32).astype(BF16)


def _s5_call(u, bmat, cmat, lre, lim, h0, need_y):
    nb, l, w5 = u.shape
    nj = w5 // LANE
    q = 2 * nb
    t = S5_CHUNK
    p8 = lre.shape[-1]
    nc = l // t
    fwd = lambda j, c: (0, c, j)
    bwd = lambda j, c: (0, nc - 1 - c, j)
    hspec = pl.BlockSpec((None, 2, q, p8), lambda j, c: (j, 0, 0, 0))
    in_specs = [pl.BlockSpec((nb, t, LANE), fwd), pl.BlockSpec((nb, t, LANE), bwd),
                pl.BlockSpec((None, 2, LANE, 2 * p8), lambda j, c: (j, 0, 0, 0)),
                pl.BlockSpec((None, 2, 2 * p8, LANE), lambda j, c: (j, 0, 0, 0)),
                pl.BlockSpec((None, q, p8), lambda j, c: (j, 0, 0)),
                pl.BlockSpec((None, q, p8), lambda j, c: (j, 0, 0)),
                hspec]
    out_specs = [hspec]
    out_shape = [jax.ShapeDtypeStruct((nj, 2, q, p8), F32)]
    if need_y:
        out_specs = [pl.BlockSpec((nb, t, LANE), fwd), pl.BlockSpec((nb, t, LANE), bwd)] + out_specs
        out_shape = [jax.ShapeDtypeStruct((nb, l, w5), BF16)] * 2 + out_shape
    kern = functools.partial(_s5_kernel, nb=nb, t=t, p8=p8, need_y=need_y)
    return pl.pallas_call(
        kern, grid=(nj, nc), in_specs=in_specs, out_specs=out_specs, out_shape=out_shape,
        scratch_shapes=[pltpu.VMEM((2 * p8 // LANE, q * (t + S5_ROW_PAD), LANE), F32)],
        compiler_params=_cp(("parallel", "arbitrary"), 40),
        name="s5_scan" if need_y else "s5_scan_ctx")(u, u, bmat, cmat, lre, lim, h0)


def _s5_params(lam_re, lam_im, log_dt, b_re, b_im, c_re, c_im, nb):
    _, g, p = lam_re.shape
    s = b_re.shape[-1]
    gpb = LANE // s
    nj = g // gpb
    lam = lax.complex(lam_re.astype(F32), lam_im.astype(F32))
    lam_bar = jnp.exp(lam * jnp.exp(log_dt.astype(F32))[..., None])
    b_bar = ((lam_bar - 1.0) / lam)[..., None] * lax.complex(b_re.astype(F32), b_im.astype(F32))
    eye = jnp.eye(gpb, dtype=F32)

    def bmat_of(bpart):
        bb = bpart.reshape(2, nj, gpb, p, s)
        m = jnp.einsum('dnkps,kl->dnkslp', bb, eye)
        return m.reshape(2, nj, gpb * s, gpb * p).transpose(1, 0, 2, 3)

    bmat = jnp.concatenate([bmat_of(b_bar.real), bmat_of(b_bar.imag)], axis=-1).astype(BF16)

    def cmat_of(cpart):
        cc = cpart.astype(F32).reshape(2, nj, gpb, s, p)
        m = jnp.einsum('dnksp,kl->dnkpls', cc, eye)
        return m.reshape(2, nj, gpb * p, gpb * s).transpose(1, 0, 2, 3)

    cmat = jnp.concatenate([cmat_of(c_re), -cmat_of(c_im)], axis=-2).astype(BF16)

    def lam_of(part):
        v = part.reshape(2, nj, gpb * p).transpose(1, 0, 2)
        return jnp.repeat(v, nb, axis=1)

    return bmat, cmat, lam_of(lam_bar.real), lam_of(lam_bar.imag)


def _conv_kernel(x_ref, w_ref, b_ref, o_ref, *, l):
    x = x_ref[...].astype(F32)
    rows = lax.broadcasted_iota(jnp.int32, x.shape, 0)
    half = SSD_CONV // 2
    acc = x * w_ref[half:half + 1, :] + b_ref[...]
    for k in range(SSD_CONV):
        if k == half:
            continue
        off = k - half
        xs = pltpu.roll(x, shift=(-off) % l, axis=0)
        valid = jnp.logical_and(rows + off >= 0, rows + off < l)
        acc = acc + jnp.where(valid, xs, 0.0) * w_ref[k:k + 1, :]
    o_ref[...] = _silu(acc).astype(o_ref.dtype)


def _conv_call(proj, w, bias, col0):
    b, l, _ = proj.shape
    c = w.shape[1]
    tc = 256
    off = col0 // tc
    return pl.pallas_call(
        functools.partial(_conv_kernel, l=l), grid=(b, c // tc),
        in_specs=[pl.BlockSpec((None, l, tc), lambda i, j: (i, 0, j + off)),
                  pl.BlockSpec((SSD_CONV, tc), lambda i, j: (0, j)),
                  pl.BlockSpec((1, tc), lambda i, j: (0, j))],
        out_specs=pl.BlockSpec((None, l, tc), lambda i, j: (i, 0, j)),
        out_shape=jax.ShapeDtypeStruct((b, l, c), BF16),
        compiler_params=_cp(("parallel", "parallel"), 40), name="ssd_conv")(proj, w, bias.reshape(1, c))


def _softplus(x):
    return jnp.maximum(x, 0.0) + jnp.log1p(jnp.exp(-jnp.abs(x)))


def _ssd_dir(xbc_ref, dt_ref, y_ref, h_ref, bias, a_neg, d, *, nh, ng, need_y):
    qn = SSD_CHUNK
    hd = SSD_HEAD_DIM
    w = nh * hd
    gw = w // ng
    gn = ng * SSD_STATE
    ii = lax.broadcasted_iota(jnp.int32, (qn, qn), 0)
    jj = lax.broadcasted_iota(jnp.int32, (qn, qn), 1)
    mask = (jj <= ii) if d == 0 else (jj >= ii)
    lmat = mask.astype(F32)
    dtv = _softplus(dt_ref[...] + bias)
    cum = jnp.dot(lmat, dtv * a_neg, precision=lax.Precision.HIGHEST, preferred_element_type=F32)
    cum_t = cum.T
    edge = qn - 1 if d == 0 else 0
    tot = cum[edge:edge + 1, :]
    to_edge = jnp.exp(tot - cum)
    ecum = jnp.exp(cum)
    decay = jnp.exp(tot)
    lane = lax.broadcasted_iota(jnp.int32, (qn, LANE), 1)
    left = lane < hd
    for g in range(ng):
        bg = xbc_ref[:, w + g * SSD_STATE:w + (g + 1) * SSD_STATE]
        cg = xbc_ref[:, w + gn + g * SSD_STATE:w + gn + (g + 1) * SSD_STATE]
        s_in = h_ref[d, :, g * gw:(g + 1) * gw]
        if need_y:
            cb = lax.dot_general(cg, bg, (((1,), (1,)), ((), ())), preferred_element_type=F32)
            yoff = jnp.dot(cg, s_in.astype(BF16), preferred_element_type=F32)
        xw_parts = []
        dec_parts = []
        for pr in range(gw // LANE):
            c0 = d * nh + (g * gw) // hd + 2 * pr
            col = g * gw + pr * LANE
            xp = xbc_ref[:, col:col + LANE].astype(F32)
            xdt = xp * jnp.where(left, dtv[:, c0:c0 + 1], dtv[:, c0 + 1:c0 + 2])
            xw_parts.append((xdt * jnp.where(left, to_edge[:, c0:c0 + 1], to_edge[:, c0 + 1:c0 + 2])).astype(BF16))
            dec_parts.append(jnp.where(left[0:1], decay[:, c0:c0 + 1], decay[:, c0 + 1:c0 + 2]))
            if need_y:
                seg0 = jnp.where(mask, jnp.exp(cum[:, c0:c0 + 1] - cum_t[c0:c0 + 1, :]), 0.0)
                seg1 = jnp.where(mask, jnp.exp(cum[:, c0 + 1:c0 + 2] - cum_t[c0 + 1:c0 + 2, :]), 0.0)
                m = jnp.concatenate([cb * seg0, cb * seg1], axis=1).astype(BF16)
                r = jnp.concatenate([jnp.where(left, xdt, 0.0), jnp.where(left, 0.0, xdt)], axis=0).astype(BF16)
                yd = jnp.dot(m, r, preferred_element_type=F32)
                ec = jnp.where(left, ecum[:, c0:c0 + 1], ecum[:, c0 + 1:c0 + 2])
                y_ref[:, col:col + LANE] = (yd + yoff[:, pr * LANE:(pr + 1) * LANE] * ec).astype(y_ref.dtype)
        xw = jnp.concatenate(xw_parts, axis=1)
        dec = jnp.concatenate(dec_parts, axis=1)
        s_new = lax.dot_general(bg, xw, (((0,), (0,)), ((), ())), preferred_element_type=F32)
        h_ref[d, :, g * gw:(g + 1) * gw] = s_in * dec + s_new


def _ssd_kernel(*refs, nh, ng, need_y):
    if need_y:
        xf_ref, xb_ref, dtf_ref, dtb_ref, h0_ref, bias_ref, alog_ref, yf_ref, yb_ref, h_ref = refs
    else:
        xf_ref, xb_ref, dtf_ref, dtb_ref, h0_ref, bias_ref, alog_ref, h_ref = refs
        yf_ref = yb_ref = None

    @pl.when(pl.program_id(1) == 0)
    def _():
        h_ref[...] = h0_ref[...]

    bias = bias_ref[...]
    a_neg = -jnp.exp(alog_ref[...])
    _ssd_dir(xf_ref, dtf_ref, yf_ref, h_ref, bias, a_neg, 0, nh=nh, ng=ng, need_y=need_y)
    _ssd_dir(xb_ref, dtb_ref, yb_ref, h_ref, bias, a_neg, 1, nh=nh, ng=ng, need_y=need_y)


def _ssd_call(xbc, dt, h0, bias, alog, nh, ng, need_y):
    b, l, cd = xbc.shape
    w = nh * SSD_HEAD_DIM
    nc = l // SSD_CHUNK
    q = SSD_CHUNK
    fwd = lambda i, s: (i, s, 0)
    bwd = lambda i, s: (i, nc - 1 - s, 0)
    hspec = pl.BlockSpec((None, 2, SSD_STATE, w), lambda i, s: (i, 0, 0, 0))
    in_specs = [pl.BlockSpec((None, q, cd), fwd), pl.BlockSpec((None, q, cd), bwd),
                pl.BlockSpec((None, q, LANE), fwd), pl.BlockSpec((None, q, LANE), bwd),
                hspec,
                pl.BlockSpec((1, LANE), lambda i, s: (0, 0)), pl.BlockSpec((1, LANE), lambda i, s: (0, 0))]
    out_specs = [hspec]
    out_shape = [jax.ShapeDtypeStruct((b, 2, SSD_STATE, w), F32)]
    if need_y:
        out_specs = [pl.BlockSpec((None, q, w), fwd), pl.BlockSpec((None, q, w), bwd)] + out_specs
        out_shape = [jax.ShapeDtypeStruct((b, l, w), BF16)] * 2 + out_shape
    return pl.pallas_call(
        functools.partial(_ssd_kernel, nh=nh, ng=ng, need_y=need_y), grid=(b, nc),
        in_specs=in_specs, out_specs=out_specs, out_shape=out_shape,
        compiler_params=_cp(("parallel", "arbitrary"), 48),
        name="ssd_scan" if need_y else "ssd_scan_ctx")(xbc, xbc, dt, dt, h0, bias, alog)


def _gnorm_kernel(yf_ref, yb_ref, xs_ref, z_ref, d_ref, g_ref, o_ref, *, ng, rows):
    y = d_ref[...] * xs_ref[...].astype(F32) + yf_ref[...].astype(F32) + yb_ref[...].astype(F32)
    y = y * _silu(z_ref[...].astype(F32))
    w = y.shape[1]
    gw = w // ng
    parts = []
    for g in range(ng):
        yg = y[:, g * gw:(g + 1) * gw]
        ms = jnp.mean(yg * yg, axis=-1, keepdims=True)
        parts.append(yg * lax.rsqrt(ms + EPS))
    res = (jnp.concatenate(parts, axis=1) * g_ref[...]).astype(o_ref.dtype)
    for k in range(SSD_CHUNK // rows):
        o_ref[:, k * w:(k + 1) * w] = res[k * rows:(k + 1) * rows, :]


def _gnorm_call(yf, yb, xbc, proj, dvec, gvec, ng):
    b, l, w = yf.shape
    rows = l // GRID_W
    nw = SSD_CHUNK // rows
    blk = lambda i, s: (i, s, 0)
    out = pl.pallas_call(
        functools.partial(_gnorm_kernel, ng=ng, rows=rows), grid=(b, l // SSD_CHUNK),
        in_specs=[pl.BlockSpec((None, SSD_CHUNK, w), blk), pl.BlockSpec((None, SSD_CHUNK, w), blk),
                  pl.BlockSpec((None, SSD_CHUNK, w), blk), pl.BlockSpec((None, SSD_CHUNK, w), blk),
                  pl.BlockSpec((1, w), lambda i, s: (0, 0)), pl.BlockSpec((1, w), lambda i, s: (0, 0))],
        out_specs=pl.BlockSpec((None, rows, nw * w), lambda i, s: (i, 0, s)),
        out_shape=jax.ShapeDtypeStruct((b, rows, GRID_W * w), BF16),
        compiler_params=_cp(("parallel", "parallel"), 40), name="ssd_gnorm")(yf, yb, xbc, proj, dvec, gvec)
    return out.reshape(b * l, w)


def _moe_kernel(e_ref, b0_ref, nb_ref, so_ref, *refs, nsub, nh):
    xs = refs[:nsub]
    wg_ref, wu_ref, wd_ref, o_ref, act_ref, wdb_ref = refs[nsub:]
    s = pl.program_id(0)
    t = pl.program_id(1)
    nb = nb_ref[s]

    @pl.when(jnp.logical_and(t < nh, nb > 0))
    def _():
        wg = wg_ref[...].astype(BF16)
        wu = wu_ref[...].astype(BF16)
        for k in range(nsub):
            @pl.when(k < nb)
            def _():
                x = xs[k][...]
                gt = jnp.dot(x, wg, preferred_element_type=F32)
                up = jnp.dot(x, wu, preferred_element_type=F32)
                act_ref[k, t] = (_silu(gt) * up).astype(BF16)

    @pl.when(jnp.logical_and(t >= nh, nb > 0))
    def _():
        wdb_ref[...] = wd_ref[...].astype(BF16)
        for k in range(nsub):
            rows = slice(k * MOE_BLOCK, (k + 1) * MOE_BLOCK)

            @pl.when(k < nb)
            def _():
                a = jnp.concatenate([act_ref[k, h] for h in range(nh)], axis=1)
                o_ref[rows, :] = jnp.dot(a, wdb_ref[...], preferred_element_type=F32).astype(o_ref.dtype)

            @pl.when(k >= nb)
            def _():
                o_ref[rows, :] = jnp.zeros((MOE_BLOCK, o_ref.shape[1]), o_ref.dtype)


def _moe_call(sup_e, sup_b0, sup_nb, sup_out, xs, w_gate, w_up, w_down):
    n_slots, d = xs.shape
    hid = w_gate.shape[-1]
    th = min(256, hid)
    tn = min(1024, d)
    nh = hid // th
    nt = d // tn
    ns = sup_e.shape[0]
    rsup = MOE_SUB * MOE_BLOCK

    def x_map(k):
        return lambda s, t, e, b0, nb, so: (b0[s] + jnp.minimum(k, jnp.maximum(nb[s] - 1, 0)), 0)

    def hid_idx(s, t, nb):
        return jnp.where(nb[s] > 0, jnp.minimum(t, nh - 1), nh - 1)

    def col_idx(s, t, nb):
        return jnp.where(nb[s] > 0, jnp.clip(t - nh, 0, nt - 1), nt - 1)

    gs = pltpu.PrefetchScalarGridSpec(
        num_scalar_prefetch=4, grid=(ns, nh + nt),
        in_specs=[pl.BlockSpec((MOE_BLOCK, d), x_map(k)) for k in range(MOE_SUB)] + [
            pl.BlockSpec((None, d, th), lambda s, t, e, b0, nb, so: (e[s], 0, hid_idx(s, t, nb))),
            pl.BlockSpec((None, d, th), lambda s, t, e, b0, nb, so: (e[s], 0, hid_idx(s, t, nb))),
            pl.BlockSpec((None, hid, tn), lambda s, t, e, b0, nb, so: (e[s], 0, col_idx(s, t, nb)))],
        out_specs=pl.BlockSpec((rsup, tn), lambda s, t, e, b0, nb, so: (so[s], col_idx(s, t, nb))),
        scratch_shapes=[pltpu.VMEM((MOE_SUB, nh, MOE_BLOCK, th), BF16), pltpu.VMEM((hid, tn), BF16)])
    return pl.pallas_call(
        functools.partial(_moe_kernel, nsub=MOE_SUB, nh=nh), grid_spec=gs,
        out_shape=jax.ShapeDtypeStruct((ns * rsup, d), BF16),
        compiler_params=_cp(("arbitrary", "arbitrary"), 60), name="moe_experts")(
            sup_e, sup_b0, sup_nb, sup_out, *([xs] * MOE_SUB), w_gate, w_up, w_down)


def _final_kernel(x_ref, ya_ref, yb_ref, w_ref, g2_ref, fg_ref, o_ref):
    wts = w_ref[...]
    moe = ya_ref[...].astype(F32) * wts[:, 0:1] + yb_ref[...].astype(F32) * wts[:, 1:2]
    x = x_ref[...] + g2_ref[...] * moe
    ms = jnp.mean(x * x, axis=-1, keepdims=True)
    o_ref[...] = x * lax.rsqrt(ms + EPS) * fg_ref[...]


def _final_call(x, ya, yb, wts, mods, ig, fg):
    b, l, d = x.shape
    tr = min(256, l)
    nt = l // tr
    row = lambda i, r: (i * nt + r, 0)
    return pl.pallas_call(
        _final_kernel, grid=(b, nt),
        in_specs=[pl.BlockSpec((None, tr, d), lambda i, r: (i, r, 0)),
                  pl.BlockSpec((tr, d), row), pl.BlockSpec((tr, d), row),
                  pl.BlockSpec((tr, MOE_TOP_K), row),
                  pl.BlockSpec((None, None, 1, d), lambda i, r: (i, ig, 0, 0)),
                  pl.BlockSpec((1, d), lambda i, r: (0, 0))],
        out_specs=pl.BlockSpec((None, tr, d), lambda i, r: (i, r, 0)),
        out_shape=jax.ShapeDtypeStruct((b, l, d), F32),
        compiler_params=_cp(("parallel", "parallel"), 40), name="combine_final")(x, ya, yb, wts, mods, fg)


def _route(logits, b_group, b_expert):
    m = logits.shape[0]
    ngr = b_group.shape[0]
    ne = b_expert.shape[0]
    epg = ne // ngr
    g_prob = jax.nn.softmax(logits[:, :ngr] + b_group.astype(F32), axis=-1)
    g_idx = jnp.argmax(g_prob, axis=-1).astype(jnp.int32)[:, None]
    g_p = jnp.take_along_axis(g_prob, g_idx, axis=1)
    e_logits = (logits[:, ngr:ngr + ne] + b_expert.astype(F32)).reshape(m, ngr, epg)
    e_logits = jnp.take_along_axis(e_logits, g_idx[:, :, None], axis=1)[:, 0]
    lane = lax.broadcasted_iota(jnp.int32, e_logits.shape, 1)
    i1 = jnp.argmax(e_logits, axis=-1).astype(jnp.int32)[:, None]
    rest = jnp.where(lane == i1, -jnp.inf, e_logits)
    i2 = jnp.argmax(rest, axis=-1).astype(jnp.int32)[:, None]
    e_idx = jnp.concatenate([i1, i2], axis=1)
    e_top = jnp.take_along_axis(e_logits, e_idx, axis=1)
    weights = g_p * jax.nn.softmax(e_top, axis=-1)
    experts = g_idx * epg + e_idx
    n_assign = m * MOE_TOP_K
    flat_e = experts.reshape(-1)
    order = jnp.argsort(flat_e)
    sorted_e = flat_e[order]
    counts = jnp.zeros((ne,), jnp.int32).at[flat_e].add(1)
    nblk_e = (counts + MOE_BLOCK - 1) // MOE_BLOCK
    padded = nblk_e * MOE_BLOCK
    pad_end = jnp.cumsum(padded)
    pad_start = pad_end - padded
    start = jnp.cumsum(counts) - counts
    local = jnp.arange(n_assign, dtype=jnp.int32) - start[sorted_e]
    dest = pad_start[sorted_e] + local
    n_blocks = -(-(n_assign + ne * (MOE_BLOCK - 1)) // MOE_BLOCK)
    tok_of_sorted = (order // MOE_TOP_K).astype(jnp.int32)
    n_slots = n_blocks * MOE_BLOCK
    slot_tok = (jnp.arange(n_slots, dtype=jnp.int32) % m).at[dest].set(tok_of_sorted)
    rsup = MOE_SUB * MOE_BLOCK
    ns_e = (nblk_e + MOE_SUB - 1) // MOE_SUB
    sup_end = jnp.cumsum(ns_e)
    sup_start = sup_end - ns_e
    n_sup = (n_blocks + (MOE_SUB - 1) * ne) // MOE_SUB
    sidx = jnp.arange(n_sup, dtype=jnp.int32)
    last = sup_end[-1] - 1
    s_eff = jnp.minimum(sidx, last)
    e_s = jnp.minimum(jnp.searchsorted(sup_end, s_eff, side='right'), ne - 1).astype(jnp.int32)
    k_s = s_eff - sup_start[e_s]
    b0_s = pad_start[e_s] // MOE_BLOCK + MOE_SUB * k_s
    nb_s = jnp.clip(nblk_e[e_s] - MOE_SUB * k_s, 0, MOE_SUB)
    used = sidx <= last
    sup_b0 = jnp.where(used, b0_s, b0_s + nb_s - 1).astype(jnp.int32)
    sup_nb = jnp.where(used, nb_s, 0).astype(jnp.int32)
    out_row = (sup_start[sorted_e] + local // rsup) * rsup + local % rsup
    pos = jnp.zeros((n_assign,), jnp.int32).at[order].set(out_row.astype(jnp.int32)).reshape(m, MOE_TOP_K)
    return weights, slot_tok, e_s, sup_b0, sup_nb, s_eff.astype(jnp.int32), pos


def kernel(x, c, ctx, c_ctx, w_mod, b_mod, norm1_g, w_in, s5_lam_re, s5_lam_im, s5_log_dt, s5_b_re, s5_b_im, s5_c_re, s5_c_im, s5_d, s5_w_val, s5_w_gate, ssd_conv_w, ssd_conv_b, ssd_a_log, ssd_dt_bias, ssd_d, ssd_norm_g, ssd_w_out, w_o, norm2_g, moe_w_group, moe_b_group, moe_w_expert, moe_b_expert, moe_w_gate, moe_w_up, moe_w_down, final_g):
    depth = w_mod.shape[0]
    assert depth == 1, "single-layer block"
    bsz, n_lat, d = x.shape
    l_ctx = ctx.shape[1]
    w5 = s5_d.shape[1]
    nh = ssd_d.shape[1]
    w = nh * SSD_HEAD_DIM
    conv_dim = ssd_conv_w.shape[2]
    ng = (conv_dim - w) // (2 * SSD_STATE)
    ssd_in = w + conv_dim + 2 * nh
    o1, o2 = w5, w5 + ssd_in
    l = 0

    cc = jnp.concatenate([c, c_ctx[None, :]], axis=0)
    cc = jnp.pad(cc, ((0, (-cc.shape[0]) % 8), (0, 0)))
    mods = _mod_call(cc, w_mod[l], b_mod[l]).reshape(cc.shape[0], 6, 1, d)
    i_sh1, i_sc1, i_g1, i_sh2, i_sc2, i_g2 = range(6)

    w_in_bf = w_in[l].astype(BF16)
    o_dt = o1 + w + conv_dim

    hn_rm, hn_cm = _norm_lat_call(x, norm1_g[l], mods, i_sc1, i_sh1)
    hc = _norm_ctx_call(ctx, norm1_g[l], mods, bsz, i_sc1, i_sh1)
    hn_rm = hn_rm.reshape(bsz * n_lat, d)
    hn_cm = hn_cm.reshape(bsz * n_lat, d)
    hc = hc.reshape(bsz * l_ctx, d)

    u_lat = _mm_call(hn_rm, w_in_bf, BF16, "in_s5", 0, o1).reshape(bsz, n_lat, w5)
    u_ctx = _mm_call(hc, w_in_bf, BF16, "in_s5_ctx", 0, o1).reshape(bsz, l_ctx, w5)
    gates = _mm_call(hn_rm, w_in_bf, BF16, "in_gates", o2, 2 * d)
    p_lat = _mm_call(hn_cm, w_in_bf, BF16, "in_ssd", o1, w + conv_dim).reshape(bsz, n_lat, w + conv_dim)
    p_ctx = _mm_call(hc, w_in_bf, BF16, "in_ssd_ctx", o1, w + conv_dim).reshape(bsz, l_ctx, w + conv_dim)
    dt_lat = _mm_call(hn_cm, w_in_bf, F32, "in_dt", o_dt, 2 * nh).reshape(bsz, n_lat, LANE)
    dt_ctx = _mm_call(hc, w_in_bf, F32, "in_dt_ctx", o_dt, 2 * nh).reshape(bsz, l_ctx, LANE)

    nj = w5 // LANE
    bmat, cmat, lre, lim = _s5_params(s5_lam_re[l], s5_lam_im[l], s5_log_dt[l], s5_b_re[l], s5_b_im[l],
                                      s5_c_re[l], s5_c_im[l], bsz)
    s5_zero = jnp.zeros((nj, 2, 2 * bsz, lre.shape[-1]), F32)
    (s5_ctx,) = _s5_call(u_ctx, bmat, cmat, lre, lim, s5_zero, False)
    ya_f, ya_b, _ = _s5_call(u_lat, bmat, cmat, lre, lim, s5_ctx, True)

    xbc_ctx = _conv_call(p_ctx, ssd_conv_w[l], ssd_conv_b[l], w)
    xbc_lat = _conv_call(p_lat, ssd_conv_w[l], ssd_conv_b[l], w)
    pad_h = LANE - 2 * nh
    bias = jnp.pad(ssd_dt_bias[l].astype(F32).reshape(1, 2 * nh), ((0, 0), (0, pad_h)))
    alog = jnp.pad(ssd_a_log[l].astype(F32).reshape(1, 2 * nh), ((0, 0), (0, pad_h)))
    h_zero = jnp.zeros((bsz, 2, SSD_STATE, w), F32)
    (h_ctx,) = _ssd_call(xbc_ctx, dt_ctx, h_zero, bias, alog, nh, ng, False)
    y_f, y_b, _ = _ssd_call(xbc_lat, dt_lat, h_ctx, bias, alog, nh, ng, True)
    d_vec = jnp.repeat(ssd_d[l].astype(F32), SSD_HEAD_DIM).reshape(1, w)
    y_ssd = _gnorm_call(y_f, y_b, xbc_lat, p_lat, d_vec, ssd_norm_g[l].astype(F32).reshape(1, w), ng)

    m_lat = bsz * n_lat
    part_a = _glu_call(u_lat.reshape(m_lat, w5), ya_f.reshape(m_lat, w5), ya_b.reshape(m_lat, w5),
                       s5_d[l].astype(F32).reshape(1, w5),
                       s5_w_val[l].astype(BF16), s5_w_gate[l].astype(BF16), gates)
    merged = _merge_call(y_ssd, ssd_w_out[l].astype(BF16), gates, part_a)
    x1 = _resid_call(merged, w_o[l].astype(BF16), x, mods, i_g1)

    ngr = moe_w_group.shape[-1]
    ne = moe_w_expert.shape[-1]
    wr = jnp.concatenate([moe_w_group[l], moe_w_expert[l]], axis=1).astype(F32)
    wr = jnp.pad(wr, ((0, 0), (0, (-(ngr + ne)) % LANE)))
    hx, logits = _norm_router_call(x1, norm2_g[l], mods, i_sc2, i_sh2, wr)
    m = bsz * n_lat
    hx = hx.reshape(m, d)
    weights, slot_tok, sup_e, sup_b0, sup_nb, sup_out, pos = _route(
        logits.reshape(m, -1), moe_b_group[l], moe_b_expert[l])
    xs = hx[slot_tok]
    y_slots = _moe_call(sup_e, sup_b0, sup_nb, sup_out, xs, moe_w_gate[l], moe_w_up[l], moe_w_down[l])
    ya = y_slots[pos[:, 0]]
    yb = y_slots[pos[:, 1]]
    return _final_call(x1, ya, yb, weights.astype(F32), mods, i_g2, final_g.reshape(1, d))
```

```python
import functools
import math

import jax
import jax.numpy as jnp
from jax import lax
from jax.experimental import pallas as pl
from jax.experimental.pallas import tpu as pltpu

F32 = jnp.float32
BF16 = jnp.bfloat16

GRID_W = 64
EPS = 1e-6
LANE = 128
SSD_HEAD_DIM = 64
SSD_STATE = 128
SSD_CHUNK = 128
SSD_CONV = 5
MOE_TOP_K = 2
MOE_BLOCK = 256
MOE_SUB = 4
S5_CHUNK = 128
S5_ROW_PAD = 8


def _cp(sem, mb):
    return pltpu.CompilerParams(dimension_semantics=sem, vmem_limit_bytes=mb * 1024 * 1024)


def _sigmoid(x):
    return 1.0 / (1.0 + jnp.exp(-x))


def _silu(x):
    return x * _sigmoid(x)


def _mod_kernel(c_ref, w_ref, b_ref, o_ref):
    s = _silu(c_ref[...])
    o_ref[...] = jnp.dot(s.astype(BF16), w_ref[...].astype(BF16),
                         preferred_element_type=F32) + b_ref[...]


def _mod_call(cc, w, b):
    r, d = cc.shape
    n = w.shape[1]
    tn = min(512, n)
    return pl.pallas_call(
        _mod_kernel, grid=(n // tn,),
        in_specs=[pl.BlockSpec((r, d), lambda j: (0, 0)),
                  pl.BlockSpec((d, tn), lambda j: (0, j)),
                  pl.BlockSpec((1, tn), lambda j: (0, j))],
        out_specs=pl.BlockSpec((r, tn), lambda j: (0, j)),
        out_shape=jax.ShapeDtypeStruct((r, n), F32),
        compiler_params=_cp(("parallel",), 40), name="mod")(cc, w, b.reshape(1, n))


def _rmsmod(x, g, sc, sh):
    ms = jnp.mean(x * x, axis=-1, keepdims=True)
    return (x * lax.rsqrt(ms + EPS) * g) * (1.0 + sc) + sh


def _norm_lat_kernel(x_ref, g_ref, sc_ref, sh_ref, orm_ref, ocm_ref, *, rb, d):
    for r in range(rb):
        y = _rmsmod(x_ref[r], g_ref[...], sc_ref[...], sh_ref[...]).astype(BF16)
        orm_ref[r] = y
        ocm_ref[:, r * d:(r + 1) * d] = y


def _norm_lat_call(x, g, mods, isc, ish):
    b, l, d = x.shape
    rows = l // GRID_W
    rb = min(4, rows)
    xv = x.reshape(b, rows, GRID_W, d)
    orm, ocm = pl.pallas_call(
        functools.partial(_norm_lat_kernel, rb=rb, d=d), grid=(b, rows // rb),
        in_specs=[pl.BlockSpec((None, rb, GRID_W, d), lambda i, r: (i, r, 0, 0)),
                  pl.BlockSpec((1, d), lambda i, r: (0, 0)),
                  pl.BlockSpec((None, None, 1, d), lambda i, r: (i, isc, 0, 0)),
                  pl.BlockSpec((None, None, 1, d), lambda i, r: (i, ish, 0, 0))],
        out_specs=[pl.BlockSpec((None, rb, GRID_W, d), lambda i, r: (i, r, 0, 0)),
                   pl.BlockSpec((None, GRID_W, rb * d), lambda i, r: (i, 0, r))],
        out_shape=[jax.ShapeDtypeStruct((b, rows, GRID_W, d), BF16),
                   jax.ShapeDtypeStruct((b, GRID_W, rows * d), BF16)],
        compiler_params=_cp(("parallel", "parallel"), 48), name="norm1_lat")(xv, g.reshape(1, d), mods, mods)
    return orm.reshape(b, l, d), ocm.reshape(b, l, d)


def _norm_kernel(x_ref, g_ref, sc_ref, sh_ref, o_ref):
    o_ref[...] = _rmsmod(x_ref[...], g_ref[...], sc_ref[...], sh_ref[...]).astype(o_ref.dtype)


def _norm_ctx_call(x, g, mods, row, isc, ish):
    b, l, d = x.shape
    tr = min(256, l)
    return pl.pallas_call(
        _norm_kernel, grid=(b, l // tr),
        in_specs=[pl.BlockSpec((None, tr, d), lambda i, r: (i, r, 0)),
                  pl.BlockSpec((1, d), lambda i, r: (0, 0)),
                  pl.BlockSpec((None, None, 1, d), lambda i, r: (row, isc, 0, 0)),
                  pl.BlockSpec((None, None, 1, d), lambda i, r: (row, ish, 0, 0))],
        out_specs=pl.BlockSpec((None, tr, d), lambda i, r: (i, r, 0)),
        out_shape=jax.ShapeDtypeStruct((b, l, d), BF16),
        compiler_params=_cp(("parallel", "parallel"), 40), name="norm1_ctx")(x, g.reshape(1, d), mods, mods)


def _norm_router_kernel(x_ref, g_ref, sc_ref, sh_ref, wr_ref, o_ref, lg_ref):
    y = _rmsmod(x_ref[...], g_ref[...], sc_ref[...], sh_ref[...])
    o_ref[...] = y.astype(BF16)
    lg_ref[...] = jnp.dot(y, wr_ref[...], precision=lax.Precision.HIGHEST, preferred_element_type=F32)


def _norm_router_call(x, g, mods, isc, ish, wr):
    b, l, d = x.shape
    tr = min(256, l)
    nr = wr.shape[1]
    return pl.pallas_call(
        _norm_router_kernel, grid=(b, l // tr),
        in_specs=[pl.BlockSpec((None, tr, d), lambda i, r: (i, r, 0)),
                  pl.BlockSpec((1, d), lambda i, r: (0, 0)),
                  pl.BlockSpec((None, None, 1, d), lambda i, r: (i, isc, 0, 0)),
                  pl.BlockSpec((None, None, 1, d), lambda i, r: (i, ish, 0, 0)),
                  pl.BlockSpec((d, nr), lambda i, r: (0, 0))],
        out_specs=[pl.BlockSpec((None, tr, d), lambda i, r: (i, r, 0)),
                   pl.BlockSpec((None, tr, nr), lambda i, r: (i, r, 0))],
        out_shape=[jax.ShapeDtypeStruct((b, l, d), BF16), jax.ShapeDtypeStruct((b, l, nr), F32)],
        compiler_params=_cp(("parallel", "parallel"), 40), name="norm2_router")(x, g.reshape(1, d), mods, mods, wr)


def _mm_kernel(a_ref, b_ref, o_ref):
    o_ref[...] = jnp.dot(a_ref[...], b_ref[...].astype(BF16), preferred_element_type=F32).astype(o_ref.dtype)


def _mm_tiles(m, n):
    tm = min(1024, m)
    tn = min(512, n)
    return tm, tn


def _mm_call(a, b, out_dtype, name, col0=0, n=None):
    m, k = a.shape
    n = b.shape[1] - col0 if n is None else n
    tm, tn = _mm_tiles(m, n)
    if col0 % tn or n % tn or n % LANE:
        b = b[:, col0:col0 + n]
        pad = (-n) % LANE
        b = jnp.pad(b, ((0, 0), (0, pad)))
        n, col0 = n + pad, 0
        tm, tn = _mm_tiles(m, n)
    off = col0 // tn
    return pl.pallas_call(
        _mm_kernel, grid=(m // tm, n // tn),
        in_specs=[pl.BlockSpec((tm, k), lambda i, j: (i, 0)),
                  pl.BlockSpec((k, tn), lambda i, j: (0, j + off))],
        out_specs=pl.BlockSpec((tm, tn), lambda i, j: (i, j)),
        out_shape=jax.ShapeDtypeStruct((m, n), out_dtype),
        compiler_params=_cp(("parallel", "parallel"), 48), name=name)(a, b)


def _gelu_tanh(x):
    return x * (0.5 * (1.0 + jnp.tanh(math.sqrt(2.0 / math.pi) * (x + 0.044715 * (x * x * x)))))


def _glu_kernel(u_ref, yf_ref, yb_ref, d_ref, wv_ref, wg_ref, gate_ref, o_ref, a_ref):
    @pl.when(pl.program_id(1) == 0)
    def _():
        y = d_ref[...] * u_ref[...].astype(F32) + yf_ref[...].astype(F32) + yb_ref[...].astype(F32)
        a_ref[...] = _gelu_tanh(y).astype(BF16)

    a = a_ref[...]
    val = jnp.dot(a, wv_ref[...], preferred_element_type=F32)
    gl = jnp.dot(a, wg_ref[...], preferred_element_type=F32)
    o_ref[...] = (_sigmoid(gate_ref[...].astype(F32)) * (val * _sigmoid(gl))).astype(o_ref.dtype)


def _glu_call(u, yf, yb, dvec, wv, wg, gates):
    m, k = u.shape
    n = wv.shape[1]
    tm, tn = _mm_tiles(m, n)
    row = pl.BlockSpec((tm, k), lambda i, j: (i, 0))
    return pl.pallas_call(
        _glu_kernel, grid=(m // tm, n // tn),
        in_specs=[row, row, row,
                  pl.BlockSpec((1, k), lambda i, j: (0, 0)),
                  pl.BlockSpec((k, tn), lambda i, j: (0, j)),
                  pl.BlockSpec((k, tn), lambda i, j: (0, j)),
                  pl.BlockSpec((tm, tn), lambda i, j: (i, j))],
        out_specs=pl.BlockSpec((tm, tn), lambda i, j: (i, j)),
        out_shape=jax.ShapeDtypeStruct((m, n), BF16),
        scratch_shapes=[pltpu.VMEM((tm, k), BF16)],
        compiler_params=_cp(("parallel", "arbitrary"), 56), name="s5_glu")(u, yf, yb, dvec, wv, wg, gates)


def _merge_kernel(a_ref, w_ref, gate_ref, pa_ref, o_ref):
    br = jnp.dot(a_ref[...], w_ref[...].astype(BF16), preferred_element_type=F32)
    o_ref[...] = (pa_ref[...].astype(F32) + _sigmoid(gate_ref[...].astype(F32)) * br).astype(o_ref.dtype)


def _merge_call(a, w, gates, part_a):
    m, k = a.shape
    n = w.shape[1]
    tm, tn = _mm_tiles(m, n)
    off = n // tn
    return pl.pallas_call(
        _merge_kernel, grid=(m // tm, n // tn),
        in_specs=[pl.BlockSpec((tm, k), lambda i, j: (i, 0)),
                  pl.BlockSpec((k, tn), lambda i, j: (0, j)),
                  pl.BlockSpec((tm, tn), lambda i, j: (i, j + off)),
                  pl.BlockSpec((tm, tn), lambda i, j: (i, j))],
        out_specs=pl.BlockSpec((tm, tn), lambda i, j: (i, j)),
        out_shape=jax.ShapeDtypeStruct((m, n), BF16),
        compiler_params=_cp(("parallel", "parallel"), 48), name="ssd_out_merge")(a, w, gates, part_a)


def _resid_kernel(a_ref, w_ref, x_ref, g_ref, o_ref):
    mix = jnp.dot(a_ref[...], w_ref[...].astype(BF16), preferred_element_type=F32)
    o_ref[...] = x_ref[...] + g_ref[...] * mix


def _resid_call(a, w, x, mods, ig):
    b, l, d = x.shape
    k = a.shape[1]
    tm, tn = _mm_tiles(l, d)
    nt = l // tm
    return pl.pallas_call(
        _resid_kernel, grid=(b * nt, d // tn),
        in_specs=[pl.BlockSpec((tm, k), lambda i, j: (i, 0)),
                  pl.BlockSpec((k, tn), lambda i, j: (0, j)),
                  pl.BlockSpec((None, tm, tn), lambda i, j: (i // nt, i % nt, j)),
                  pl.BlockSpec((None, None, 1, tn), lambda i, j: (i // nt, ig, 0, j))],
        out_specs=pl.BlockSpec((None, tm, tn), lambda i, j: (i // nt, i % nt, j)),
        out_shape=jax.ShapeDtypeStruct((b, l, d), F32),
        compiler_params=_cp(("parallel", "parallel"), 48), name="w_o_resid")(a, w, x, mods)


def _s5_kernel(*refs, nb, t, p8, need_y):
    if need_y:
        uf_ref, ub_ref, b_ref, c_ref, lre_ref, lim_ref, h0_ref, yf_ref, yb_ref, h_ref, bu_ref = refs
    else:
        uf_ref, ub_ref, b_ref, c_ref, lre_ref, lim_ref, h0_ref, h_ref, bu_ref = refs
    q = 2 * nb
    nk = p8 // LANE
    pitch = t + S5_ROW_PAD

    @pl.when(pl.program_id(1) == 0)
    def _():
        h_ref[...] = h0_ref[...]

    rev = (lax.broadcasted_iota(jnp.int32, (t, t), 0) + lax.broadcasted_iota(jnp.int32, (t, t), 1)
           == t - 1).astype(BF16)
    for d in range(2):
        if d == 0:
            u = uf_ref[...].reshape(nb * t, LANE)
        else:
            u = jnp.concatenate([jnp.dot(rev, ub_ref[b], preferred_element_type=F32).astype(BF16)
                                 for b in range(nb)], axis=0)
        bu = jnp.dot(u, b_ref[d], preferred_element_type=F32)
        for b in range(nb):
            r0 = (d * nb + b) * pitch
            for k in range(2 * nk):
                bu_ref[k, r0:r0 + t, :] = bu[b * t:(b + 1) * t, k * LANE:(k + 1) * LANE]
    ar = [lre_ref[:, k * LANE:(k + 1) * LANE] for k in range(nk)]
    ai = [lim_ref[:, k * LANE:(k + 1) * LANE] for k in range(nk)]

    def step(s, carry):
        rows = pl.ds(s, q, stride=pitch)
        out = []
        for k in range(nk):
            hr, hi = carry[k]
            nr = ar[k] * hr - ai[k] * hi + bu_ref[k, rows, :]
            ni = ar[k] * hi + ai[k] * hr + bu_ref[nk + k, rows, :]
            bu_ref[k, rows, :] = nr
            bu_ref[nk + k, rows, :] = ni
            out.append((nr, ni))
        return tuple(out)

    init = tuple((h_ref[0, :, k * LANE:(k + 1) * LANE], h_ref[1, :, k * LANE:(k + 1) * LANE]) for k in range(nk))
    fin = lax.fori_loop(0, t, step, init, unroll=8)
    for k in range(nk):
        h_ref[0, :, k * LANE:(k + 1) * LANE] = fin[k][0]
        h_ref[1, :, k * LANE:(k + 1) * LANE] = fin[k][1]

    if need_y:
        for d in range(2):
            h = jnp.concatenate(
                [jnp.concatenate([bu_ref[k, (d * nb + b) * pitch:(d * nb + b) * pitch + t, :].astype(BF16)
                                  for k in range(2 * nk)], axis=1) for b in range(nb)], axis=0)
            y = jnp.dot(h, c_ref[d], preferred_element_type=F32).astype(BF16)
            for b in range(nb):
                yb = y[b * t:(b + 1) * t]
                if d == 0:
                    yf_ref[b] = yb
                else:
                    yb_ref[b] = jnp.dot(rev, yb, preferred_element_type=F32).astype(BF16)


def _s5_call(u, bmat, cmat, lre, lim, h0, need_y):
    nb, l, w5 = u.shape
    nj = w5 // LANE
    q = 2 * nb
    t = S5_CHUNK
    p8 = lre.shape[-1]
    nc = l // t
    fwd = lambda j, c: (0, c, j)
    bwd = lambda j, c: (0, nc - 1 - c, j)
    hspec = pl.BlockSpec((None, 2, q, p8), lambda j, c: (j, 0, 0, 0))
    in_specs = [pl.BlockSpec((nb, t, LANE), fwd), pl.BlockSpec((nb, t, LANE), bwd),
                pl.BlockSpec((None, 2, LANE, 2 * p8), lambda j, c: (j, 0, 0, 0)),
                pl.BlockSpec((None, 2, 2 * p8, LANE), lambda j, c: (j, 0, 0, 0)),
                pl.BlockSpec((None, q, p8), lambda j, c: (j, 0, 0)),
                pl.BlockSpec((None, q, p8), lambda j, c: (j, 0, 0)),
                hspec]
    out_specs = [hspec]
    out_shape = [jax.ShapeDtypeStruct((nj, 2, q, p8), F32)]
    if need_y:
        out_specs = [pl.BlockSpec((nb, t, LANE), fwd), pl.BlockSpec((nb, t, LANE), bwd)] + out_specs
        out_shape = [jax.ShapeDtypeStruct((nb, l, w5), BF16)] * 2 + out_shape
    kern = functools.partial(_s5_kernel, nb=nb, t=t, p8=p8, need_y=need_y)
    return pl.pallas_call(
        kern, grid=(nj, nc), in_specs=in_specs, out_specs=out_specs, out_shape=out_shape,
        scratch_shapes=[pltpu.VMEM((2 * p8 // LANE, q * (t + S5_ROW_PAD), LANE), F32)],
        compiler_params=_cp(("parallel", "arbitrary"), 40),
        name="s5_scan" if need_y else "s5_scan_ctx")(u, u, bmat, cmat, lre, lim, h0)


def _s5_params(lam_re, lam_im, log_dt, b_re, b_im, c_re, c_im, nb):
    _, g, p = lam_re.shape
    s = b_re.shape[-1]
    gpb = LANE // s
    nj = g // gpb
    lam = lax.complex(lam_re.astype(F32), lam_im.astype(F32))
    lam_bar = jnp.exp(lam * jnp.exp(log_dt.astype(F32))[..., None])
    b_bar = ((lam_bar - 1.0) / lam)[..., None] * lax.complex(b_re.astype(F32), b_im.astype(F32))
    eye = jnp.eye(gpb, dtype=F32)

    def bmat_of(bpart):
        bb = bpart.reshape(2, nj, gpb, p, s)
        m = jnp.einsum('dnkps,kl->dnkslp', bb, eye)
        return m.reshape(2, nj, gpb * s, gpb * p).transpose(1, 0, 2, 3)

    bmat = jnp.concatenate([bmat_of(b_bar.real), bmat_of(b_bar.imag)], axis=-1).astype(BF16)

    def cmat_of(cpart):
        cc = cpart.astype(F32).reshape(2, nj, gpb, s, p)
        m = jnp.einsum('dnksp,kl->dnkpls', cc, eye)
        return m.reshape(2, nj, gpb * p, gpb * s).transpose(1, 0, 2, 3)

    cmat = jnp.concatenate([cmat_of(c_re), -cmat_of(c_im)], axis=-2).astype(BF16)

    def lam_of(part):
        v = part.reshape(2, nj, gpb * p).transpose(1, 0, 2)
        return jnp.repeat(v, nb, axis=1)

    return bmat, cmat, lam_of(lam_bar.real), lam_of(lam_bar.imag)


def _conv_kernel(x_ref, w_ref, b_ref, o_ref, *, l):
    x = x_ref[...].astype(F32)
    rows = lax.broadcasted_iota(jnp.int32, x.shape, 0)
    half = SSD_CONV // 2
    acc = x * w_ref[half:half + 1, :] + b_ref[...]
    for k in range(SSD_CONV):
        if k == half:
            continue
        off = k - half
        xs = pltpu.roll(x, shift=(-off) % l, axis=0)
        valid = jnp.logical_and(rows + off >= 0, rows + off < l)
        acc = acc + jnp.where(valid, xs, 0.0) * w_ref[k:k + 1, :]
    o_ref[...] = _silu(acc).astype(o_ref.dtype)


def _conv_call(proj, w, bias, col0):
    b, l, _ = proj.shape
    c = w.shape[1]
    tc = 256
    off = col0 // tc
    return pl.pallas_call(
        functools.partial(_conv_kernel, l=l), grid=(b, c // tc),
        in_specs=[pl.BlockSpec((None, l, tc), lambda i, j: (i, 0, j + off)),
                  pl.BlockSpec((SSD_CONV, tc), lambda i, j: (0, j)),
                  pl.BlockSpec((1, tc), lambda i, j: (0, j))],
        out_specs=pl.BlockSpec((None, l, tc), lambda i, j: (i, 0, j)),
        out_shape=jax.ShapeDtypeStruct((b, l, c), BF16),
        compiler_params=_cp(("parallel", "parallel"), 40), name="ssd_conv")(proj, w, bias.reshape(1, c))


def _softplus(x):
    return jnp.maximum(x, 0.0) + jnp.log1p(jnp.exp(-jnp.abs(x)))


def _ssd_dir(xbc_ref, dt_ref, y_ref, h_ref, bias, a_neg, d, *, nh, ng, need_y):
    qn = SSD_CHUNK
    hd = SSD_HEAD_DIM
    w = nh * hd
    gw = w // ng
    gn = ng * SSD_STATE
    ii = lax.broadcasted_iota(jnp.int32, (qn, qn), 0)
    jj = lax.broadcasted_iota(jnp.int32, (qn, qn), 1)
    mask = (jj <= ii) if d == 0 else (jj >= ii)
    lmat = mask.astype(F32)
    dtv = _softplus(dt_ref[...] + bias)
    cum = jnp.dot(lmat, dtv * a_neg, precision=lax.Precision.HIGHEST, preferred_element_type=F32)
    cum_t = cum.T
    edge = qn - 1 if d == 0 else 0
    tot = cum[edge:edge + 1, :]
    dt_t = dtv.T
    wt_t = dt_t * jnp.exp(cum_t[:, edge:edge + 1] - cum_t)
    decay = jnp.exp(tot)
    lane = lax.broadcasted_iota(jnp.int32, (qn, LANE), 1)
    left = lane < hd
    zero = jnp.zeros((), BF16)
    for g in range(ng):
        bg = xbc_ref[:, w + g * SSD_STATE:w + (g + 1) * SSD_STATE]
        cg = xbc_ref[:, w + gn + g * SSD_STATE:w + gn + (g + 1) * SSD_STATE]
        bg_t = bg.astype(F32).T
        s_in = h_ref[d, :, g * gw:(g + 1) * gw]
        if need_y:
            cb = lax.dot_general(cg, bg, (((1,), (1,)), ((), ())), preferred_element_type=F32)
            yoff = jnp.dot(cg, s_in.astype(BF16), preferred_element_type=F32)
        for pr in range(gw // LANE):
            c0 = d * nh + (g * gw) // hd + 2 * pr
            col = g * gw + pr * LANE
            xp = xbc_ref[:, col:col + LANE]
            r = jnp.concatenate([jnp.where(left, xp, zero), jnp.where(left, zero, xp)], axis=0)
            tops, bots, cols = [], [], []
            for c in (c0, c0 + 1):
                bots.append((bg_t * wt_t[c:c + 1, :]).astype(BF16))
                if need_y:
                    ccol = jnp.broadcast_to(cum[:, c:c + 1], (qn, qn))
                    seg = jnp.where(mask, jnp.exp(ccol - cum_t[c:c + 1, :]), 0.0)
                    tops.append((cb * seg * dt_t[c:c + 1, :]).astype(BF16))
                    cols.append(ccol)
            dec = jnp.where(left[0:1], decay[:, c0:c0 + 1], decay[:, c0 + 1:c0 + 2])
            s_old = s_in[:, pr * LANE:(pr + 1) * LANE]
            if need_y:
                lhs = jnp.concatenate([jnp.concatenate(tops, axis=1), jnp.concatenate(bots, axis=1)], axis=0)
                out = jnp.dot(lhs, r, preferred_element_type=F32)
                ec = jnp.exp(jnp.where(left, cols[0], cols[1]))
                y_ref[:, col:col + LANE] = (out[:qn] + yoff[:, pr * LANE:(pr + 1) * LANE] * ec).astype(y_ref.dtype)
                s_new = out[qn:]
            else:
                s_new = jnp.dot(jnp.concatenate(bots, axis=1), r, preferred_element_type=F32)
            h_ref[d, :, col:col + LANE] = s_old * dec + s_new


def _ssd_kernel(*refs, nh, ng, need_y):
    if need_y:
        xf_ref, xb_ref, dtf_ref, dtb_ref, h0_ref, bias_ref, alog_ref, yf_ref, yb_ref, h_ref = refs
    else:
        xf_ref, xb_ref, dtf_ref, dtb_ref, h0_ref, bias_ref, alog_ref, h_ref = refs
        yf_ref = yb_ref = None

    @pl.when(pl.program_id(1) == 0)
    def _():
        h_ref[...] = h0_ref[...]

    bias = bias_ref[...]
    a_neg = -jnp.exp(alog_ref[...])
    _ssd_dir(xf_ref, dtf_ref, yf_ref, h_ref, bias, a_neg, 0, nh=nh, ng=ng, need_y=need_y)
    _ssd_dir(xb_ref, dtb_ref, yb_ref, h_ref, bias, a_neg, 1, nh=nh, ng=ng, need_y=need_y)


def _ssd_call(xbc, dt, h0, bias, alog, nh, ng, need_y):
    b, l, cd = xbc.shape
    w = nh * SSD_HEAD_DIM
    nc = l // SSD_CHUNK
    q = SSD_CHUNK
    fwd = lambda i, s: (i, s, 0)
    bwd = lambda i, s: (i, nc - 1 - s, 0)
    hspec = pl.BlockSpec((None, 2, SSD_STATE, w), lambda i, s: (i, 0, 0, 0))
    in_specs = [pl.BlockSpec((None, q, cd), fwd), pl.BlockSpec((None, q, cd), bwd),
                pl.BlockSpec((None, q, LANE), fwd), pl.BlockSpec((None, q, LANE), bwd),
                hspec,
                pl.BlockSpec((1, LANE), lambda i, s: (0, 0)), pl.BlockSpec((1, LANE), lambda i, s: (0, 0))]
    out_specs = [hspec]
    out_shape = [jax.ShapeDtypeStruct((b, 2, SSD_STATE, w), F32)]
    if need_y:
        out_specs = [pl.BlockSpec((None, q, w), fwd), pl.BlockSpec((None, q, w), bwd)] + out_specs
        out_shape = [jax.ShapeDtypeStruct((b, l, w), BF16)] * 2 + out_shape
    return pl.pallas_call(
        functools.partial(_ssd_kernel, nh=nh, ng=ng, need_y=need_y), grid=(b, nc),
        in_specs=in_specs, out_specs=out_specs, out_shape=out_shape,
        compiler_params=_cp(("parallel", "arbitrary"), 48),
        name="ssd_scan" if need_y else "ssd_scan_ctx")(xbc, xbc, dt, dt, h0, bias, alog)


def _gnorm_kernel(yf_ref, yb_ref, xs_ref, z_ref, d_ref, g_ref, o_ref, *, ng, rows):
    y = d_ref[...] * xs_ref[...].astype(F32) + yf_ref[...].astype(F32) + yb_ref[...].astype(F32)
    y = y * _silu(z_ref[...].astype(F32))
    w = y.shape[1]
    gw = w // ng
    parts = []
    for g in range(ng):
        yg = y[:, g * gw:(g + 1) * gw]
        ms = jnp.mean(yg * yg, axis=-1, keepdims=True)
        parts.append(yg * lax.rsqrt(ms + EPS))
    res = (jnp.concatenate(parts, axis=1) * g_ref[...]).astype(o_ref.dtype)
    for k in range(SSD_CHUNK // rows):
        o_ref[:, k * w:(k + 1) * w] = res[k * rows:(k + 1) * rows, :]


def _gnorm_call(yf, yb, xbc, proj, dvec, gvec, ng):
    b, l, w = yf.shape
    rows = l // GRID_W
    nw = SSD_CHUNK // rows
    blk = lambda i, s: (i, s, 0)
    out = pl.pallas_call(
        functools.partial(_gnorm_kernel, ng=ng, rows=rows), grid=(b, l // SSD_CHUNK),
        in_specs=[pl.BlockSpec((None, SSD_CHUNK, w), blk), pl.BlockSpec((None, SSD_CHUNK, w), blk),
                  pl.BlockSpec((None, SSD_CHUNK, w), blk), pl.BlockSpec((None, SSD_CHUNK, w), blk),
                  pl.BlockSpec((1, w), lambda i, s: (0, 0)), pl.BlockSpec((1, w), lambda i, s: (0, 0))],
        out_specs=pl.BlockSpec((None, rows, nw * w), lambda i, s: (i, 0, s)),
        out_shape=jax.ShapeDtypeStruct((b, rows, GRID_W * w), BF16),
        compiler_params=_cp(("parallel", "parallel"), 40), name="ssd_gnorm")(yf, yb, xbc, proj, dvec, gvec)
    return out.reshape(b * l, w)


def _moe_kernel(e_ref, b0_ref, nb_ref, so_ref, *refs, nsub, nh):
    xs = refs[:nsub]
    wg_ref, wu_ref, wd_ref, o_ref, act_ref, wdb_ref = refs[nsub:]
    s = pl.program_id(0)
    t = pl.program_id(1)
    nb = nb_ref[s]

    @pl.when(jnp.logical_and(t < nh, nb > 0))
    def _():
        wg = wg_ref[...].astype(BF16)
        wu = wu_ref[...].astype(BF16)
        for k in range(nsub):
            @pl.when(k < nb)
            def _():
                x = xs[k][...]
                gt = jnp.dot(x, wg, preferred_element_type=F32)
                up = jnp.dot(x, wu, preferred_element_type=F32)
                act_ref[k, t] = (_silu(gt) * up).astype(BF16)

    @pl.when(jnp.logical_and(t >= nh, nb > 0))
    def _():
        wdb_ref[...] = wd_ref[...].astype(BF16)
        for k in range(nsub):
            rows = slice(k * MOE_BLOCK, (k + 1) * MOE_BLOCK)

            @pl.when(k < nb)
            def _():
                a = jnp.concatenate([act_ref[k, h] for h in range(nh)], axis=1)
                o_ref[rows, :] = jnp.dot(a, wdb_ref[...], preferred_element_type=F32).astype(o_ref.dtype)

            @pl.when(k >= nb)
            def _():
                o_ref[rows, :] = jnp.zeros((MOE_BLOCK, o_ref.shape[1]), o_ref.dtype)


def _moe_call(sup_e, sup_b0, sup_nb, sup_out, xs, w_gate, w_up, w_down):
    n_slots, d = xs.shape
    hid = w_gate.shape[-1]
    th = min(256, hid)
    tn = min(1024, d)
    nh = hid // th
    nt = d // tn
    ns = sup_e.shape[0]
    rsup = MOE_SUB * MOE_BLOCK

    def x_map(k):
        return lambda s, t, e, b0, nb, so: (b0[s] + jnp.minimum(k, jnp.maximum(nb[s] - 1, 0)), 0)

    def hid_idx(s, t, nb):
        return jnp.where(nb[s] > 0, jnp.minimum(t, nh - 1), nh - 1)

    def col_idx(s, t, nb):
        return jnp.where(nb[s] > 0, jnp.clip(t - nh, 0, nt - 1), nt - 1)

    gs = pltpu.PrefetchScalarGridSpec(
        num_scalar_prefetch=4, grid=(ns, nh + nt),
        in_specs=[pl.BlockSpec((MOE_BLOCK, d), x_map(k)) for k in range(MOE_SUB)] + [
            pl.BlockSpec((None, d, th), lambda s, t, e, b0, nb, so: (e[s], 0, hid_idx(s, t, nb))),
            pl.BlockSpec((None, d, th), lambda s, t, e, b0, nb, so: (e[s], 0, hid_idx(s, t, nb))),
            pl.BlockSpec((None, hid, tn), lambda s, t, e, b0, nb, so: (e[s], 0, col_idx(s, t, nb)))],
        out_specs=pl.BlockSpec((rsup, tn), lambda s, t, e, b0, nb, so: (so[s], col_idx(s, t, nb))),
        scratch_shapes=[pltpu.VMEM((MOE_SUB, nh, MOE_BLOCK, th), BF16), pltpu.VMEM((hid, tn), BF16)])
    return pl.pallas_call(
        functools.partial(_moe_kernel, nsub=MOE_SUB, nh=nh), grid_spec=gs,
        out_shape=jax.ShapeDtypeStruct((ns * rsup, d), BF16),
        compiler_params=_cp(("arbitrary", "arbitrary"), 60), name="moe_experts")(
            sup_e, sup_b0, sup_nb, sup_out, *([xs] * MOE_SUB), w_gate, w_up, w_down)


def _final_kernel(x_ref, ya_ref, yb_ref, w_ref, g2_ref, fg_ref, o_ref):
    wts = w_ref[...]
    moe = ya_ref[...].astype(F32) * wts[:, 0:1] + yb_ref[...].astype(F32) * wts[:, 1:2]
    x = x_ref[...] + g2_ref[...] * moe
    ms = jnp.mean(x * x, axis=-1, keepdims=True)
    o_ref[...] = x * lax.rsqrt(ms + EPS) * fg_ref[...]


def _final_call(x, ya, yb, wts, mods, ig, fg):
    b, l, d = x.shape
    tr = min(256, l)
    nt = l // tr
    row = lambda i, r: (i * nt + r, 0)
    return pl.pallas_call(
        _final_kernel, grid=(b, nt),
        in_specs=[pl.BlockSpec((None, tr, d), lambda i, r: (i, r, 0)),
                  pl.BlockSpec((tr, d), row), pl.BlockSpec((tr, d), row),
                  pl.BlockSpec((tr, MOE_TOP_K), row),
                  pl.BlockSpec((None, None, 1, d), lambda i, r: (i, ig, 0, 0)),
                  pl.BlockSpec((1, d), lambda i, r: (0, 0))],
        out_specs=pl.BlockSpec((None, tr, d), lambda i, r: (i, r, 0)),
        out_shape=jax.ShapeDtypeStruct((b, l, d), F32),
        compiler_params=_cp(("parallel", "parallel"), 40), name="combine_final")(x, ya, yb, wts, mods, fg)


def _route(logits, b_group, b_expert):
    m = logits.shape[0]
    ngr = b_group.shape[0]
    ne = b_expert.shape[0]
    epg = ne // ngr
    g_prob = jax.nn.softmax(logits[:, :ngr] + b_group.astype(F32), axis=-1)
    g_idx = jnp.argmax(g_prob, axis=-1).astype(jnp.int32)[:, None]
    g_p = jnp.take_along_axis(g_prob, g_idx, axis=1)
    e_logits = (logits[:, ngr:ngr + ne] + b_expert.astype(F32)).reshape(m, ngr, epg)
    e_logits = jnp.take_along_axis(e_logits, g_idx[:, :, None], axis=1)[:, 0]
    lane = lax.broadcasted_iota(jnp.int32, e_logits.shape, 1)
    i1 = jnp.argmax(e_logits, axis=-1).astype(jnp.int32)[:, None]
    rest = jnp.where(lane == i1, -jnp.inf, e_logits)
    i2 = jnp.argmax(rest, axis=-1).astype(jnp.int32)[:, None]
    e_idx = jnp.concatenate([i1, i2], axis=1)
    e_top = jnp.take_along_axis(e_logits, e_idx, axis=1)
    weights = g_p * jax.nn.softmax(e_top, axis=-1)
    experts = g_idx * epg + e_idx
    n_assign = m * MOE_TOP_K
    flat_e = experts.reshape(-1)
    onehot = (flat_e[:, None] == jnp.arange(ne, dtype=flat_e.dtype)[None, :]).astype(jnp.int32)
    csum = jnp.cumsum(onehot, axis=0)
    counts = csum[-1]
    local = jnp.take_along_axis(csum, flat_e[:, None], axis=1)[:, 0] - 1
    nblk_e = (counts + MOE_BLOCK - 1) // MOE_BLOCK
    padded = nblk_e * MOE_BLOCK
    pad_end = jnp.cumsum(padded)
    pad_start = pad_end - padded
    dest = pad_start[flat_e] + local
    n_blocks = -(-(n_assign + ne * (MOE_BLOCK - 1)) // MOE_BLOCK)
    tok = jnp.arange(n_assign, dtype=jnp.int32) // MOE_TOP_K
    n_slots = n_blocks * MOE_BLOCK
    slot_tok = (jnp.arange(n_slots, dtype=jnp.int32) % m).at[dest].set(tok)
    rsup = MOE_SUB * MOE_BLOCK
    ns_e = (nblk_e + MOE_SUB - 1) // MOE_SUB
    sup_end = jnp.cumsum(ns_e)
    sup_start = sup_end - ns_e
    n_sup = (n_blocks + (MOE_SUB - 1) * ne) // MOE_SUB
    sidx = jnp.arange(n_sup, dtype=jnp.int32)
    last = sup_end[-1] - 1
    s_eff = jnp.minimum(sidx, last)
    e_s = jnp.minimum(jnp.searchsorted(sup_end, s_eff, side='right'), ne - 1).astype(jnp.int32)
    k_s = s_eff - sup_start[e_s]
    b0_s = pad_start[e_s] // MOE_BLOCK + MOE_SUB * k_s
    nb_s = jnp.clip(nblk_e[e_s] - MOE_SUB * k_s, 0, MOE_SUB)
    used = sidx <= last
    sup_b0 = jnp.where(used, b0_s, b0_s + nb_s - 1).astype(jnp.int32)
    sup_nb = jnp.where(used, nb_s, 0).astype(jnp.int32)
    pos = ((sup_start[flat_e] + local // rsup) * rsup + local % rsup).astype(jnp.int32).reshape(m, MOE_TOP_K)
    return weights, slot_tok, e_s, sup_b0, sup_nb, s_eff.astype(jnp.int32), pos


def kernel(x, c, ctx, c_ctx, w_mod, b_mod, norm1_g, w_in, s5_lam_re, s5_lam_im, s5_log_dt, s5_b_re, s5_b_im, s5_c_re, s5_c_im, s5_d, s5_w_val, s5_w_gate, ssd_conv_w, ssd_conv_b, ssd_a_log, ssd_dt_bias, ssd_d, ssd_norm_g, ssd_w_out, w_o, norm2_g, moe_w_group, moe_b_group, moe_w_expert, moe_b_expert, moe_w_gate, moe_w_up, moe_w_down, final_g):
    depth = w_mod.shape[0]
    assert depth == 1, "single-layer block"
    bsz, n_lat, d = x.shape
    l_ctx = ctx.shape[1]
    w5 = s5_d.shape[1]
    nh = ssd_d.shape[1]
    w = nh * SSD_HEAD_DIM
    conv_dim = ssd_conv_w.shape[2]
    ng = (conv_dim - w) // (2 * SSD_STATE)
    ssd_in = w + conv_dim + 2 * nh
    o1, o2 = w5, w5 + ssd_in
    l = 0

    cc = jnp.concatenate([c, c_ctx[None, :]], axis=0)
    cc = jnp.pad(cc, ((0, (-cc.shape[0]) % 8), (0, 0)))
    mods = _mod_call(cc, w_mod[l], b_mod[l]).reshape(cc.shape[0], 6, 1, d)
    i_sh1, i_sc1, i_g1, i_sh2, i_sc2, i_g2 = range(6)

    w_in_l = w_in[l]
    w_gates = w_in_l[:, o2:].astype(BF16)
    o_dt = o1 + w + conv_dim

    hn_rm, hn_cm = _norm_lat_call(x, norm1_g[l], mods, i_sc1, i_sh1)
    hc = _norm_ctx_call(ctx, norm1_g[l], mods, bsz, i_sc1, i_sh1)
    hn_rm = hn_rm.reshape(bsz * n_lat, d)
    hn_cm = hn_cm.reshape(bsz * n_lat, d)
    hc = hc.reshape(bsz * l_ctx, d)

    u_lat = _mm_call(hn_rm, w_in_l, BF16, "in_s5", 0, o1).reshape(bsz, n_lat, w5)
    u_ctx = _mm_call(hc, w_in_l, BF16, "in_s5_ctx", 0, o1).reshape(bsz, l_ctx, w5)
    gates = _mm_call(hn_rm, w_gates, BF16, "in_gates")
    p_lat = _mm_call(hn_cm, w_in_l, BF16, "in_ssd", o1, w + conv_dim).reshape(bsz, n_lat, w + conv_dim)
    p_ctx = _mm_call(hc, w_in_l, BF16, "in_ssd_ctx", o1, w + conv_dim).reshape(bsz, l_ctx, w + conv_dim)
    dt_lat = _mm_call(hn_cm, w_in_l, F32, "in_dt", o_dt, 2 * nh).reshape(bsz, n_lat, LANE)
    dt_ctx = _mm_call(hc, w_in_l, F32, "in_dt_ctx", o_dt, 2 * nh).reshape(bsz, l_ctx, LANE)

    nj = w5 // LANE
    bmat, cmat, lre, lim = _s5_params(s5_lam_re[l], s5_lam_im[l], s5_log_dt[l], s5_b_re[l], s5_b_im[l],
                                      s5_c_re[l], s5_c_im[l], bsz)
    s5_zero = jnp.zeros((nj, 2, 2 * bsz, lre.shape[-1]), F32)
    (s5_ctx,) = _s5_call(u_ctx, bmat, cmat, lre, lim, s5_zero, False)
    ya_f, ya_b, _ = _s5_call(u_lat, bmat, cmat, lre, lim, s5_ctx, True)

    xbc_ctx = _conv_call(p_ctx, ssd_conv_w[l], ssd_conv_b[l], w)
    xbc_lat = _conv_call(p_lat, ssd_conv_w[l], ssd_conv_b[l], w)
    pad_h = LANE - 2 * nh
    bias = jnp.pad(ssd_dt_bias[l].astype(F32).reshape(1, 2 * nh), ((0, 0), (0, pad_h)))
    alog = jnp.pad(ssd_a_log[l].astype(F32).reshape(1, 2 * nh), ((0, 0), (0, pad_h)))
    h_zero = jnp.zeros((bsz, 2, SSD_STATE, w), F32)
    (h_ctx,) = _ssd_call(xbc_ctx, dt_ctx, h_zero, bias, alog, nh, ng, False)
    y_f, y_b, _ = _ssd_call(xbc_lat, dt_lat, h_ctx, bias, alog, nh, ng, True)
    d_vec = jnp.repeat(ssd_d[l].astype(F32), SSD_HEAD_DIM).reshape(1, w)
    y_ssd = _gnorm_call(y_f, y_b, xbc_lat, p_lat, d_vec, ssd_norm_g[l].astype(F32).reshape(1, w), ng)

    m_lat = bsz * n_lat
    part_a = _glu_call(u_lat.reshape(m_lat, w5), ya_f.reshape(m_lat, w5), ya_b.reshape(m_lat, w5),
                       s5_d[l].astype(F32).reshape(1, w5),
                       s5_w_val[l].astype(BF16), s5_w_gate[l].astype(BF16), gates)
    merged = _merge_call(y_ssd, ssd_w_out[l], gates, part_a)
    x1 = _resid_call(merged, w_o[l], x, mods, i_g1)

    ngr = moe_w_group.shape[-1]
    ne = moe_w_expert.shape[-1]
    wr = jnp.concatenate([moe_w_group[l], moe_w_expert[l]], axis=1).astype(F32)
    wr = jnp.pad(wr, ((0, 0), (0, (-(ngr + ne)) % LANE)))
    hx, logits = _norm_router_call(x1, norm2_g[l], mods, i_sc2, i_sh2, wr)
    m = bsz * n_lat
    hx = hx.reshape(m, d)
    weights, slot_tok, sup_e, sup_b0, sup_nb, sup_out, pos = _route(
        logits.reshape(m, -1), moe_b_group[l], moe_b_expert[l])
    xs = hx[slot_tok]
    y_slots = _moe_call(sup_e, sup_b0, sup_nb, sup_out, xs, moe_w_gate[l], moe_w_up[l], moe_w_down[l])
    ya = y_slots[pos[:, 0]]
    yb = y_slots[pos[:, 1]]
    return _final_call(x1, ya, yb, weights.astype(F32), mods, i_g2, final_g.reshape(1, d))
```

```python
import functools
import math

import jax
import jax.numpy as jnp
from jax import lax
from jax.experimental import pallas as pl
from jax.experimental.pallas import tpu as pltpu

F32 = jnp.float32
BF16 = jnp.bfloat16

GRID_W = 64
EPS = 1e-6
LANE = 128
SSD_HEAD_DIM = 64
SSD_STATE = 128
SSD_CHUNK = 128
SSD_CONV = 5
MOE_TOP_K = 2
MOE_BLOCK = 256
MOE_SUB = 4
S5_CHUNK = 128
S5_SLICE = 256
S5_ROW_PAD = 8


def _cp(sem, mb):
    return pltpu.CompilerParams(dimension_semantics=sem, vmem_limit_bytes=mb * 1024 * 1024)


def _sigmoid(x):
    return 1.0 / (1.0 + jnp.exp(-x))


def _silu(x):
    return x * _sigmoid(x)


def _mod_kernel(c_ref, w_ref, b_ref, o_ref):
    s = _silu(c_ref[...])
    o_ref[...] = jnp.dot(s.astype(BF16), w_ref[...].astype(BF16),
                         preferred_element_type=F32) + b_ref[...]


def _mod_call(cc, w, b):
    r, d = cc.shape
    n = w.shape[1]
    tn = min(512, n)
    return pl.pallas_call(
        _mod_kernel, grid=(n // tn,),
        in_specs=[pl.BlockSpec((r, d), lambda j: (0, 0)),
                  pl.BlockSpec((d, tn), lambda j: (0, j)),
                  pl.BlockSpec((1, tn), lambda j: (0, j))],
        out_specs=pl.BlockSpec((r, tn), lambda j: (0, j)),
        out_shape=jax.ShapeDtypeStruct((r, n), F32),
        compiler_params=_cp(("parallel",), 40), name="mod")(cc, w, b.reshape(1, n))


def _rmsmod(x, g, sc, sh):
    ms = jnp.mean(x * x, axis=-1, keepdims=True)
    return (x * lax.rsqrt(ms + EPS) * g) * (1.0 + sc) + sh


def _norm_lat_kernel(x_ref, g_ref, sc_ref, sh_ref, orm_ref, ocm_ref, *, rb, d):
    for r in range(rb):
        y = _rmsmod(x_ref[r], g_ref[...], sc_ref[...], sh_ref[...]).astype(BF16)
        orm_ref[r] = y
        ocm_ref[:, r * d:(r + 1) * d] = y


def _norm_lat_call(x, g, mods, isc, ish):
    b, l, d = x.shape
    rows = l // GRID_W
    rb = min(4, rows)
    xv = x.reshape(b, rows, GRID_W, d)
    orm, ocm = pl.pallas_call(
        functools.partial(_norm_lat_kernel, rb=rb, d=d), grid=(b, rows // rb),
        in_specs=[pl.BlockSpec((None, rb, GRID_W, d), lambda i, r: (i, r, 0, 0)),
                  pl.BlockSpec((1, d), lambda i, r: (0, 0)),
                  pl.BlockSpec((None, None, 1, d), lambda i, r: (i, isc, 0, 0)),
                  pl.BlockSpec((None, None, 1, d), lambda i, r: (i, ish, 0, 0))],
        out_specs=[pl.BlockSpec((None, rb, GRID_W, d), lambda i, r: (i, r, 0, 0)),
                   pl.BlockSpec((None, GRID_W, rb * d), lambda i, r: (i, 0, r))],
        out_shape=[jax.ShapeDtypeStruct((b, rows, GRID_W, d), BF16),
                   jax.ShapeDtypeStruct((b, GRID_W, rows * d), BF16)],
        compiler_params=_cp(("parallel", "parallel"), 48), name="norm1_lat")(xv, g.reshape(1, d), mods, mods)
    return orm.reshape(b, l, d), ocm.reshape(b, l, d)


def _norm_kernel(x_ref, g_ref, sc_ref, sh_ref, o_ref):
    o_ref[...] = _rmsmod(x_ref[...], g_ref[...], sc_ref[...], sh_ref[...]).astype(o_ref.dtype)


def _norm_ctx_call(x, g, mods, row, isc, ish):
    b, l, d = x.shape
    tr = min(256, l)
    return pl.pallas_call(
        _norm_kernel, grid=(b, l // tr),
        in_specs=[pl.BlockSpec((None, tr, d), lambda i, r: (i, r, 0)),
                  pl.BlockSpec((1, d), lambda i, r: (0, 0)),
                  pl.BlockSpec((None, None, 1, d), lambda i, r: (row, isc, 0, 0)),
                  pl.BlockSpec((None, None, 1, d), lambda i, r: (row, ish, 0, 0))],
        out_specs=pl.BlockSpec((None, tr, d), lambda i, r: (i, r, 0)),
        out_shape=jax.ShapeDtypeStruct((b, l, d), BF16),
        compiler_params=_cp(("parallel", "parallel"), 40), name="norm1_ctx")(x, g.reshape(1, d), mods, mods)


def _norm_router_kernel(x_ref, g_ref, sc_ref, sh_ref, wr_ref, o_ref, lg_ref):
    y = _rmsmod(x_ref[...], g_ref[...], sc_ref[...], sh_ref[...])
    o_ref[...] = y.astype(BF16)
    lg_ref[...] = jnp.dot(y, wr_ref[...], precision=lax.Precision.HIGHEST, preferred_element_type=F32)


def _norm_router_call(x, g, mods, isc, ish, wr):
    b, l, d = x.shape
    tr = min(256, l)
    nr = wr.shape[1]
    return pl.pallas_call(
        _norm_router_kernel, grid=(b, l // tr),
        in_specs=[pl.BlockSpec((None, tr, d), lambda i, r: (i, r, 0)),
                  pl.BlockSpec((1, d), lambda i, r: (0, 0)),
                  pl.BlockSpec((None, None, 1, d), lambda i, r: (i, isc, 0, 0)),
                  pl.BlockSpec((None, None, 1, d), lambda i, r: (i, ish, 0, 0)),
                  pl.BlockSpec((d, nr), lambda i, r: (0, 0))],
        out_specs=[pl.BlockSpec((None, tr, d), lambda i, r: (i, r, 0)),
                   pl.BlockSpec((None, tr, nr), lambda i, r: (i, r, 0))],
        out_shape=[jax.ShapeDtypeStruct((b, l, d), BF16), jax.ShapeDtypeStruct((b, l, nr), F32)],
        compiler_params=_cp(("parallel", "parallel"), 40), name="norm2_router")(x, g.reshape(1, d), mods, mods, wr)


def _mm_kernel(a_ref, b_ref, o_ref):
    o_ref[...] = jnp.dot(a_ref[...], b_ref[...].astype(BF16), preferred_element_type=F32).astype(o_ref.dtype)


def _mm_tiles(m, n):
    tm = min(1024, m)
    tn = min(512, n)
    return tm, tn


def _mm_call(a, b, out_dtype, name, col0=0, n=None):
    m, k = a.shape
    n = b.shape[1] - col0 if n is None else n
    tm, tn = _mm_tiles(m, n)
    if col0 % tn or n % tn or n % LANE:
        b = b[:, col0:col0 + n]
        pad = (-n) % LANE
        b = jnp.pad(b, ((0, 0), (0, pad)))
        n, col0 = n + pad, 0
        tm, tn = _mm_tiles(m, n)
    off = col0 // tn
    return pl.pallas_call(
        _mm_kernel, grid=(m // tm, n // tn),
        in_specs=[pl.BlockSpec((tm, k), lambda i, j: (i, 0)),
                  pl.BlockSpec((k, tn), lambda i, j: (0, j + off))],
        out_specs=pl.BlockSpec((tm, tn), lambda i, j: (i, j)),
        out_shape=jax.ShapeDtypeStruct((m, n), out_dtype),
        compiler_params=_cp(("parallel", "parallel"), 48), name=name)(a, b)


def _gelu_tanh(x):
    return x * (0.5 * (1.0 + jnp.tanh(math.sqrt(2.0 / math.pi) * (x + 0.044715 * (x * x * x)))))


def _glu_kernel(u_ref, yf_ref, yb_ref, d_ref, wv_ref, wg_ref, gate_ref, o_ref, a_ref):
    @pl.when(pl.program_id(1) == 0)
    def _():
        y = d_ref[...] * u_ref[...].astype(F32) + yf_ref[...].astype(F32) + yb_ref[...].astype(F32)
        a_ref[...] = _gelu_tanh(y).astype(BF16)

    a = a_ref[...]
    val = jnp.dot(a, wv_ref[...], preferred_element_type=F32)
    gl = jnp.dot(a, wg_ref[...], preferred_element_type=F32)
    o_ref[...] = (_sigmoid(gate_ref[...].astype(F32)) * (val * _sigmoid(gl))).astype(o_ref.dtype)


def _glu_call(u, yf, yb, dvec, wv, wg, gates):
    m, k = u.shape
    n = wv.shape[1]
    tm, tn = _mm_tiles(m, n)
    row = pl.BlockSpec((tm, k), lambda i, j: (i, 0))
    return pl.pallas_call(
        _glu_kernel, grid=(m // tm, n // tn),
        in_specs=[row, row, row,
                  pl.BlockSpec((1, k), lambda i, j: (0, 0)),
                  pl.BlockSpec((k, tn), lambda i, j: (0, j)),
                  pl.BlockSpec((k, tn), lambda i, j: (0, j)),
                  pl.BlockSpec((tm, tn), lambda i, j: (i, j))],
        out_specs=pl.BlockSpec((tm, tn), lambda i, j: (i, j)),
        out_shape=jax.ShapeDtypeStruct((m, n), BF16),
        scratch_shapes=[pltpu.VMEM((tm, k), BF16)],
        compiler_params=_cp(("parallel", "arbitrary"), 56), name="s5_glu")(u, yf, yb, dvec, wv, wg, gates)


def _merge_kernel(a_ref, w_ref, gate_ref, pa_ref, o_ref):
    br = jnp.dot(a_ref[...], w_ref[...].astype(BF16), preferred_element_type=F32)
    o_ref[...] = (pa_ref[...].astype(F32) + _sigmoid(gate_ref[...].astype(F32)) * br).astype(o_ref.dtype)


def _merge_call(a, w, gates, part_a):
    m, k = a.shape
    n = w.shape[1]
    tm, tn = _mm_tiles(m, n)
    off = n // tn
    return pl.pallas_call(
        _merge_kernel, grid=(m // tm, n // tn),
        in_specs=[pl.BlockSpec((tm, k), lambda i, j: (i, 0)),
                  pl.BlockSpec((k, tn), lambda i, j: (0, j)),
                  pl.BlockSpec((tm, tn), lambda i, j: (i, j + off)),
                  pl.BlockSpec((tm, tn), lambda i, j: (i, j))],
        out_specs=pl.BlockSpec((tm, tn), lambda i, j: (i, j)),
        out_shape=jax.ShapeDtypeStruct((m, n), BF16),
        compiler_params=_cp(("parallel", "parallel"), 48), name="ssd_out_merge")(a, w, gates, part_a)


def _resid_kernel(a_ref, w_ref, x_ref, g_ref, o_ref):
    mix = jnp.dot(a_ref[...], w_ref[...].astype(BF16), preferred_element_type=F32)
    o_ref[...] = x_ref[...] + g_ref[...] * mix


def _resid_call(a, w, x, mods, ig):
    b, l, d = x.shape
    k = a.shape[1]
    tm, tn = _mm_tiles(l, d)
    nt = l // tm
    return pl.pallas_call(
        _resid_kernel, grid=(b * nt, d // tn),
        in_specs=[pl.BlockSpec((tm, k), lambda i, j: (i, 0)),
                  pl.BlockSpec((k, tn), lambda i, j: (0, j)),
                  pl.BlockSpec((None, tm, tn), lambda i, j: (i // nt, i % nt, j)),
                  pl.BlockSpec((None, None, 1, tn), lambda i, j: (i // nt, ig, 0, j))],
        out_specs=pl.BlockSpec((None, tm, tn), lambda i, j: (i // nt, i % nt, j)),
        out_shape=jax.ShapeDtypeStruct((b, l, d), F32),
        compiler_params=_cp(("parallel", "parallel"), 48), name="w_o_resid")(a, w, x, mods)


def _s5_scan_steps(buf_ref, carry, ar, ai, lo, hi, *, q, nk, pitch, slice_fn):
    for i in range((hi - lo) // 8):
        slice_fn(i)
        for s8 in range(8):
            rows = pl.ds(lo + i * 8 + s8, q, stride=pitch)
            out = []
            for k in range(nk):
                hr, hi_ = carry[k]
                nr = ar[k] * hr - ai[k] * hi_ + buf_ref[k, rows, :]
                ni = ar[k] * hi_ + ai[k] * hr + buf_ref[nk + k, rows, :]
                buf_ref[k, rows, :] = nr
                buf_ref[nk + k, rows, :] = ni
                out.append((nr, ni))
            carry = tuple(out)
    return carry


def _s5_kernel(*refs, nb, t, p8, need_y, nc):
    if need_y:
        (uf0_ref, ub0_ref, ufn_ref, ubn_ref, b_ref, c_ref, lre_ref, lim_ref, h0_ref,
         yf_ref, yb_ref, h_ref, buf0, buf1, buf2, u_ref, yacc_ref) = refs
    else:
        (uf0_ref, ub0_ref, ufn_ref, ubn_ref, b_ref, c_ref, lre_ref, lim_ref, h0_ref,
         h_ref, buf0, buf1, buf2, u_ref) = refs
    c = pl.program_id(1)
    q = 2 * nb
    nk = p8 // LANE
    nsl = 2 * p8 // S5_SLICE
    spl = S5_SLICE // LANE
    pitch = t + S5_ROW_PAD
    bufs = (buf0, buf1, buf2)
    rev = (lax.broadcasted_iota(jnp.int32, (t, t), 0) + lax.broadcasted_iota(jnp.int32, (t, t), 1)
           == t - 1).astype(BF16)

    def stage_u(uf, ub):
        u_ref[0] = uf[...].reshape(nb * t, LANE)
        for b in range(nb):
            u_ref[1, b * t:(b + 1) * t, :] = jnp.dot(rev, ub[b], preferred_element_type=F32).astype(BF16)

    def bu_slice(dst, i):
        d = i // nsl
        sl = i - d * nsl
        bu = jnp.dot(u_ref[d], b_ref[d, sl], preferred_element_type=F32)
        for b in range(nb):
            r0 = (d * nb + b) * pitch
            for kk in range(spl):
                dst[sl * spl + kk, pl.ds(r0, t), :] = bu[b * t:(b + 1) * t, kk * LANE:(kk + 1) * LANE]

    def y_slice(src, i):
        d = i // nsl
        sl = i - d * nsl
        parts = []
        for b in range(nb):
            r0 = (d * nb + b) * pitch
            parts.append(jnp.concatenate(
                [src[sl * spl + kk, pl.ds(r0, t), :].astype(BF16) for kk in range(spl)], axis=1))
        h = jnp.concatenate(parts, axis=0)
        yacc_ref[d] += jnp.dot(h, c_ref[d, sl], preferred_element_type=F32)

    def y_store():
        yf_ref[...] = yacc_ref[0].reshape(nb, t, LANE).astype(BF16)
        for b in range(nb):
            yb_ref[b] = jnp.dot(rev, yacc_ref[1, b * t:(b + 1) * t, :].astype(BF16),
                                preferred_element_type=F32).astype(BF16)

    @pl.when(c == 0)
    def _():
        h_ref[...] = h0_ref[...]
        if need_y:
            buf2[...] = jnp.zeros(buf2.shape, F32)
        stage_u(uf0_ref, ub0_ref)
        for i in range(2 * nsl):
            bu_slice(buf0, i)

    ar = [lre_ref[:, k * LANE:(k + 1) * LANE] for k in range(nk)]
    ai = [lim_ref[:, k * LANE:(k + 1) * LANE] for k in range(nk)]
    half = t // 2
    assert half // 8 == 2 * nsl, "one matmul slice per recurrence-loop iteration"

    def run_chunk(cur, nxt, prv):
        scan = functools.partial(_s5_scan_steps, cur, ar=ar, ai=ai, q=q, nk=nk, pitch=pitch)
        stage_u(ufn_ref, ubn_ref)
        carry = tuple((h_ref[0, :, k * LANE:(k + 1) * LANE], h_ref[1, :, k * LANE:(k + 1) * LANE])
                      for k in range(nk))
        carry = scan(carry, lo=0, hi=half, slice_fn=functools.partial(bu_slice, nxt))
        if need_y:
            yacc_ref[...] = jnp.zeros(yacc_ref.shape, F32)
            carry = scan(carry, lo=half, hi=t, slice_fn=functools.partial(y_slice, prv))
            y_store()
        else:
            carry = scan(carry, lo=half, hi=t, slice_fn=lambda i: None)
        for k in range(nk):
            h_ref[0, :, k * LANE:(k + 1) * LANE] = carry[k][0]
            h_ref[1, :, k * LANE:(k + 1) * LANE] = carry[k][1]

    for r in range(3):
        roles = (bufs[r], bufs[(r + 1) % 3], bufs[(r + 2) % 3])
        pl.when(jnp.logical_and(c < nc, lax.rem(c, 3) == r))(functools.partial(run_chunk, *roles))
    if need_y:
        @pl.when(c == nc)
        def _():
            yacc_ref[...] = jnp.zeros(yacc_ref.shape, F32)
            for i in range(2 * nsl):
                y_slice(bufs[(nc + 2) % 3], i)
            y_store()


def _s5_call(u, bmat, cmat, lre, lim, h0, need_y):
    nb, l, w5 = u.shape
    nj = w5 // LANE
    q = 2 * nb
    t = S5_CHUNK
    p8 = lre.shape[-1]
    nc = l // t
    nsl = 2 * p8 // S5_SLICE
    nxt = lambda c: jnp.minimum(c + 1, nc - 1)
    prv = lambda c: jnp.maximum(c - 1, 0)
    hspec = pl.BlockSpec((None, 2, q, p8), lambda j, c: (j, 0, 0, 0))
    ublk = (nb, t, LANE)
    in_specs = [pl.BlockSpec(ublk, lambda j, c: (0, 0, j)), pl.BlockSpec(ublk, lambda j, c: (0, nc - 1, j)),
                pl.BlockSpec(ublk, lambda j, c: (0, nxt(c), j)),
                pl.BlockSpec(ublk, lambda j, c: (0, nc - 1 - nxt(c), j)),
                pl.BlockSpec((None, 2, nsl, LANE, S5_SLICE), lambda j, c: (j, 0, 0, 0, 0)),
                pl.BlockSpec((None, 2, nsl, S5_SLICE, LANE), lambda j, c: (j, 0, 0, 0, 0)),
                pl.BlockSpec((None, q, p8), lambda j, c: (j, 0, 0)),
                pl.BlockSpec((None, q, p8), lambda j, c: (j, 0, 0)),
                hspec]
    out_specs = [hspec]
    out_shape = [jax.ShapeDtypeStruct((nj, 2, q, p8), F32)]
    scratch = [pltpu.VMEM((2 * p8 // LANE, q * (t + S5_ROW_PAD), LANE), F32) for _ in range(3)]
    scratch = scratch + [pltpu.VMEM((2, nb * t, LANE), BF16)]
    if need_y:
        out_specs = [pl.BlockSpec(ublk, lambda j, c: (0, prv(c), j)),
                     pl.BlockSpec(ublk, lambda j, c: (0, nc - 1 - prv(c), j))] + out_specs
        out_shape = [jax.ShapeDtypeStruct((nb, l, w5), BF16)] * 2 + out_shape
        scratch = scratch + [pltpu.VMEM((2, nb * t, LANE), F32)]
    bsl = bmat.reshape(nj, 2, LANE, nsl, S5_SLICE).transpose(0, 1, 3, 2, 4)
    csl = cmat.reshape(nj, 2, nsl, S5_SLICE, LANE)
    kern = functools.partial(_s5_kernel, nb=nb, t=t, p8=p8, need_y=need_y, nc=nc)
    return pl.pallas_call(
        kern, grid=(nj, nc + 1 if need_y else nc), in_specs=in_specs, out_specs=out_specs, out_shape=out_shape,
        scratch_shapes=scratch, compiler_params=_cp(("parallel", "arbitrary"), 48),
        name="s5_scan" if need_y else "s5_scan_ctx")(u, u, u, u, bsl, csl, lre, lim, h0)


def _s5_params(lam_re, lam_im, log_dt, b_re, b_im, c_re, c_im, nb):
    _, g, p = lam_re.shape
    s = b_re.shape[-1]
    gpb = LANE // s
    nj = g // gpb
    lam = lax.complex(lam_re.astype(F32), lam_im.astype(F32))
    lam_bar = jnp.exp(lam * jnp.exp(log_dt.astype(F32))[..., None])
    b_bar = ((lam_bar - 1.0) / lam)[..., None] * lax.complex(b_re.astype(F32), b_im.astype(F32))
    eye = jnp.eye(gpb, dtype=F32)

    def bmat_of(bpart):
        bb = bpart.reshape(2, nj, gpb, p, s)
        m = jnp.einsum('dnkps,kl->dnkslp', bb, eye)
        return m.reshape(2, nj, gpb * s, gpb * p).transpose(1, 0, 2, 3)

    bmat = jnp.concatenate([bmat_of(b_bar.real), bmat_of(b_bar.imag)], axis=-1).astype(BF16)

    def cmat_of(cpart):
        cc = cpart.astype(F32).reshape(2, nj, gpb, s, p)
        m = jnp.einsum('dnksp,kl->dnkpls', cc, eye)
        return m.reshape(2, nj, gpb * p, gpb * s).transpose(1, 0, 2, 3)

    cmat = jnp.concatenate([cmat_of(c_re), -cmat_of(c_im)], axis=-2).astype(BF16)

    def lam_of(part):
        v = part.reshape(2, nj, gpb * p).transpose(1, 0, 2)
        return jnp.repeat(v, nb, axis=1)

    return bmat, cmat, lam_of(lam_bar.real), lam_of(lam_bar.imag)


def _conv_kernel(x_ref, w_ref, b_ref, o_ref, *, l):
    x = x_ref[...].astype(F32)
    rows = lax.broadcasted_iota(jnp.int32, x.shape, 0)
    half = SSD_CONV // 2
    acc = x * w_ref[half:half + 1, :] + b_ref[...]
    for k in range(SSD_CONV):
        if k == half:
            continue
        off = k - half
        xs = pltpu.roll(x, shift=(-off) % l, axis=0)
        valid = jnp.logical_and(rows + off >= 0, rows + off < l)
        acc = acc + jnp.where(valid, xs, 0.0) * w_ref[k:k + 1, :]
    o_ref[...] = _silu(acc).astype(o_ref.dtype)


def _conv_call(proj, w, bias, col0):
    b, l, _ = proj.shape
    c = w.shape[1]
    tc = 256
    off = col0 // tc
    return pl.pallas_call(
        functools.partial(_conv_kernel, l=l), grid=(b, c // tc),
        in_specs=[pl.BlockSpec((None, l, tc), lambda i, j: (i, 0, j + off)),
                  pl.BlockSpec((SSD_CONV, tc), lambda i, j: (0, j)),
                  pl.BlockSpec((1, tc), lambda i, j: (0, j))],
        out_specs=pl.BlockSpec((None, l, tc), lambda i, j: (i, 0, j)),
        out_shape=jax.ShapeDtypeStruct((b, l, c), BF16),
        compiler_params=_cp(("parallel", "parallel"), 40), name="ssd_conv")(proj, w, bias.reshape(1, c))


def _softplus(x):
    return jnp.maximum(x, 0.0) + jnp.log1p(jnp.exp(-jnp.abs(x)))


def _ssd_dir(xbc_ref, dt_ref, y_ref, h_ref, bias, a_neg, d, *, nh, ng, need_y):
    qn = SSD_CHUNK
    hd = SSD_HEAD_DIM
    w = nh * hd
    gw = w // ng
    gn = ng * SSD_STATE
    ii = lax.broadcasted_iota(jnp.int32, (qn, qn), 0)
    jj = lax.broadcasted_iota(jnp.int32, (qn, qn), 1)
    mask = (jj <= ii) if d == 0 else (jj >= ii)
    lmat = mask.astype(F32)
    dtv = _softplus(dt_ref[...] + bias)
    cum = jnp.dot(lmat, dtv * a_neg, precision=lax.Precision.HIGHEST, preferred_element_type=F32)
    cum_t = cum.T
    edge = qn - 1 if d == 0 else 0
    tot = cum[edge:edge + 1, :]
    dt_t = dtv.T
    wt_t = dt_t * jnp.exp(cum_t[:, edge:edge + 1] - cum_t)
    decay = jnp.exp(tot)
    lane = lax.broadcasted_iota(jnp.int32, (qn, LANE), 1)
    left = lane < hd
    zero = jnp.zeros((), BF16)
    for g in range(ng):
        bg = xbc_ref[:, w + g * SSD_STATE:w + (g + 1) * SSD_STATE]
        cg = xbc_ref[:, w + gn + g * SSD_STATE:w + gn + (g + 1) * SSD_STATE]
        bg_t = bg.astype(F32).T
        s_in = h_ref[d, :, g * gw:(g + 1) * gw]
        if need_y:
            cb = lax.dot_general(cg, bg, (((1,), (1,)), ((), ())), preferred_element_type=F32)
            yoff = jnp.dot(cg, s_in.astype(BF16), preferred_element_type=F32)
        for pr in range(gw // LANE):
            c0 = d * nh + (g * gw) // hd + 2 * pr
            col = g * gw + pr * LANE
            xp = xbc_ref[:, col:col + LANE]
            r = jnp.concatenate([jnp.where(left, xp, zero), jnp.where(left, zero, xp)], axis=0)
            tops, bots, cols = [], [], []
            for c in (c0, c0 + 1):
                bots.append((bg_t * wt_t[c:c + 1, :]).astype(BF16))
                if need_y:
                    ccol = jnp.broadcast_to(cum[:, c:c + 1], (qn, qn))
                    seg = jnp.where(mask, jnp.exp(ccol - cum_t[c:c + 1, :]), 0.0)
                    tops.append((cb * seg * dt_t[c:c + 1, :]).astype(BF16))
                    cols.append(ccol)
            dec = jnp.where(left[0:1], decay[:, c0:c0 + 1], decay[:, c0 + 1:c0 + 2])
            s_old = s_in[:, pr * LANE:(pr + 1) * LANE]
            if need_y:
                lhs = jnp.concatenate([jnp.concatenate(tops, axis=1), jnp.concatenate(bots, axis=1)], axis=0)
                out = jnp.dot(lhs, r, preferred_element_type=F32)
                ec = jnp.exp(jnp.where(left, cols[0], cols[1]))
                y_ref[:, col:col + LANE] = (out[:qn] + yoff[:, pr * LANE:(pr + 1) * LANE] * ec).astype(y_ref.dtype)
                s_new = out[qn:]
            else:
                s_new = jnp.dot(jnp.concatenate(bots, axis=1), r, preferred_element_type=F32)
            h_ref[d, :, col:col + LANE] = s_old * dec + s_new


def _ssd_kernel(*refs, nh, ng, need_y):
    if need_y:
        xf_ref, xb_ref, dtf_ref, dtb_ref, h0_ref, bias_ref, alog_ref, yf_ref, yb_ref, h_ref = refs
    else:
        xf_ref, xb_ref, dtf_ref, dtb_ref, h0_ref, bias_ref, alog_ref, h_ref = refs
        yf_ref = yb_ref = None

    @pl.when(pl.program_id(1) == 0)
    def _():
        h_ref[...] = h0_ref[...]

    bias = bias_ref[...]
    a_neg = -jnp.exp(alog_ref[...])
    _ssd_dir(xf_ref, dtf_ref, yf_ref, h_ref, bias, a_neg, 0, nh=nh, ng=ng, need_y=need_y)
    _ssd_dir(xb_ref, dtb_ref, yb_ref, h_ref, bias, a_neg, 1, nh=nh, ng=ng, need_y=need_y)


def _ssd_call(xbc, dt, h0, bias, alog, nh, ng, need_y):
    b, l, cd = xbc.shape
    w = nh * SSD_HEAD_DIM
    nc = l // SSD_CHUNK
    q = SSD_CHUNK
    fwd = lambda i, s: (i, s, 0)
    bwd = lambda i, s: (i, nc - 1 - s, 0)
    hspec = pl.BlockSpec((None, 2, SSD_STATE, w), lambda i, s: (i, 0, 0, 0))
    in_specs = [pl.BlockSpec((None, q, cd), fwd), pl.BlockSpec((None, q, cd), bwd),
                pl.BlockSpec((None, q, LANE), fwd), pl.BlockSpec((None, q, LANE), bwd),
                hspec,
                pl.BlockSpec((1, LANE), lambda i, s: (0, 0)), pl.BlockSpec((1, LANE), lambda i, s: (0, 0))]
    out_specs = [hspec]
    out_shape = [jax.ShapeDtypeStruct((b, 2, SSD_STATE, w), F32)]
    if need_y:
        out_specs = [pl.BlockSpec((None, q, w), fwd), pl.BlockSpec((None, q, w), bwd)] + out_specs
        out_shape = [jax.ShapeDtypeStruct((b, l, w), BF16)] * 2 + out_shape
    return pl.pallas_call(
        functools.partial(_ssd_kernel, nh=nh, ng=ng, need_y=need_y), grid=(b, nc),
        in_specs=in_specs, out_specs=out_specs, out_shape=out_shape,
        compiler_params=_cp(("parallel", "arbitrary"), 48),
        name="ssd_scan" if need_y else "ssd_scan_ctx")(xbc, xbc, dt, dt, h0, bias, alog)


def _gnorm_kernel(yf_ref, yb_ref, xs_ref, z_ref, d_ref, g_ref, o_ref, *, ng, rows):
    y = d_ref[...] * xs_ref[...].astype(F32) + yf_ref[...].astype(F32) + yb_ref[...].astype(F32)
    y = y * _silu(z_ref[...].astype(F32))
    w = y.shape[1]
    gw = w // ng
    parts = []
    for g in range(ng):
        yg = y[:, g * gw:(g + 1) * gw]
        ms = jnp.mean(yg * yg, axis=-1, keepdims=True)
        parts.append(yg * lax.rsqrt(ms + EPS))
    res = (jnp.concatenate(parts, axis=1) * g_ref[...]).astype(o_ref.dtype)
    for k in range(SSD_CHUNK // rows):
        o_ref[:, k * w:(k + 1) * w] = res[k * rows:(k + 1) * rows, :]


def _gnorm_call(yf, yb, xbc, proj, dvec, gvec, ng):
    b, l, w = yf.shape
    rows = l // GRID_W
    nw = SSD_CHUNK // rows
    blk = lambda i, s: (i, s, 0)
    out = pl.pallas_call(
        functools.partial(_gnorm_kernel, ng=ng, rows=rows), grid=(b, l // SSD_CHUNK),
        in_specs=[pl.BlockSpec((None, SSD_CHUNK, w), blk), pl.BlockSpec((None, SSD_CHUNK, w), blk),
                  pl.BlockSpec((None, SSD_CHUNK, w), blk), pl.BlockSpec((None, SSD_CHUNK, w), blk),
                  pl.BlockSpec((1, w), lambda i, s: (0, 0)), pl.BlockSpec((1, w), lambda i, s: (0, 0))],
        out_specs=pl.BlockSpec((None, rows, nw * w), lambda i, s: (i, 0, s)),
        out_shape=jax.ShapeDtypeStruct((b, rows, GRID_W * w), BF16),
        compiler_params=_cp(("parallel", "parallel"), 40), name="ssd_gnorm")(yf, yb, xbc, proj, dvec, gvec)
    return out.reshape(b * l, w)


def _moe_kernel(e_ref, b0_ref, nb_ref, so_ref, *refs, nsub, nh):
    xs = refs[:nsub]
    wg_ref, wu_ref, wd_ref, o_ref, act_ref, wdb_ref = refs[nsub:]
    s = pl.program_id(0)
    t = pl.program_id(1)
    nb = nb_ref[s]

    @pl.when(jnp.logical_and(t < nh, nb > 0))
    def _():
        wg = wg_ref[...].astype(BF16)
        wu = wu_ref[...].astype(BF16)
        for k in range(nsub):
            @pl.when(k < nb)
            def _():
                x = xs[k][...]
                gt = jnp.dot(x, wg, preferred_element_type=F32)
                up = jnp.dot(x, wu, preferred_element_type=F32)
                act_ref[k, t] = (_silu(gt) * up).astype(BF16)

    @pl.when(jnp.logical_and(t >= nh, nb > 0))
    def _():
        wdb_ref[...] = wd_ref[...].astype(BF16)
        for k in range(nsub):
            rows = slice(k * MOE_BLOCK, (k + 1) * MOE_BLOCK)

            @pl.when(k < nb)
            def _():
                a = jnp.concatenate([act_ref[k, h] for h in range(nh)], axis=1)
                o_ref[rows, :] = jnp.dot(a, wdb_ref[...], preferred_element_type=F32).astype(o_ref.dtype)

            @pl.when(k >= nb)
            def _():
                o_ref[rows, :] = jnp.zeros((MOE_BLOCK, o_ref.shape[1]), o_ref.dtype)

    @pl.when(jnp.logical_and(t >= nh, nb == 0))
    def _():
        o_ref[...] = jnp.zeros(o_ref.shape, o_ref.dtype)


def _moe_call(sup_e, sup_b0, sup_nb, sup_out, xs, w_gate, w_up, w_down):
    n_slots, d = xs.shape
    hid = w_gate.shape[-1]
    th = min(256, hid)
    tn = min(1024, d)
    nh = hid // th
    nt = d // tn
    ns = sup_e.shape[0]
    rsup = MOE_SUB * MOE_BLOCK

    def x_map(k):
        return lambda s, t, e, b0, nb, so: (b0[s] + jnp.minimum(k, jnp.maximum(nb[s] - 1, 0)), 0)

    def hid_idx(s, t, nb):
        return jnp.where(nb[s] > 0, jnp.minimum(t, nh - 1), nh - 1)

    def col_idx(s, t, nb):
        return jnp.where(nb[s] > 0, jnp.clip(t - nh, 0, nt - 1), nt - 1)

    gs = pltpu.PrefetchScalarGridSpec(
        num_scalar_prefetch=4, grid=(ns, nh + nt),
        in_specs=[pl.BlockSpec((MOE_BLOCK, d), x_map(k)) for k in range(MOE_SUB)] + [
            pl.BlockSpec((None, d, th), lambda s, t, e, b0, nb, so: (e[s], 0, hid_idx(s, t, nb))),
            pl.BlockSpec((None, d, th), lambda s, t, e, b0, nb, so: (e[s], 0, hid_idx(s, t, nb))),
            pl.BlockSpec((None, hid, tn), lambda s, t, e, b0, nb, so: (e[s], 0, col_idx(s, t, nb)))],
        out_specs=pl.BlockSpec((rsup, tn), lambda s, t, e, b0, nb, so: (so[s], jnp.clip(t - nh, 0, nt - 1))),
        scratch_shapes=[pltpu.VMEM((MOE_SUB, nh, MOE_BLOCK, th), BF16), pltpu.VMEM((hid, tn), BF16)])
    return pl.pallas_call(
        functools.partial(_moe_kernel, nsub=MOE_SUB, nh=nh), grid_spec=gs,
        out_shape=jax.ShapeDtypeStruct((ns * rsup, d), BF16),
        compiler_params=_cp(("arbitrary", "arbitrary"), 60), name="moe_experts")(
            sup_e, sup_b0, sup_nb, sup_out, *([xs] * MOE_SUB), w_gate, w_up, w_down)


def _final_kernel(x_ref, ya_ref, yb_ref, w_ref, g2_ref, fg_ref, o_ref):
    wts = w_ref[...]
    moe = ya_ref[...].astype(F32) * wts[:, 0:1] + yb_ref[...].astype(F32) * wts[:, 1:2]
    x = x_ref[...] + g2_ref[...] * moe
    ms = jnp.mean(x * x, axis=-1, keepdims=True)
    o_ref[...] = x * lax.rsqrt(ms + EPS) * fg_ref[...]


def _final_call(x, ya, yb, wts, mods, ig, fg):
    b, l, d = x.shape
    tr = min(256, l)
    nt = l // tr
    row = lambda i, r: (i * nt + r, 0)
    return pl.pallas_call(
        _final_kernel, grid=(b, nt),
        in_specs=[pl.BlockSpec((None, tr, d), lambda i, r: (i, r, 0)),
                  pl.BlockSpec((tr, d), row), pl.BlockSpec((tr, d), row),
                  pl.BlockSpec((tr, MOE_TOP_K), row),
                  pl.BlockSpec((None, None, 1, d), lambda i, r: (i, ig, 0, 0)),
                  pl.BlockSpec((1, d), lambda i, r: (0, 0))],
        out_specs=pl.BlockSpec((None, tr, d), lambda i, r: (i, r, 0)),
        out_shape=jax.ShapeDtypeStruct((b, l, d), F32),
        compiler_params=_cp(("parallel", "parallel"), 40), name="combine_final")(x, ya, yb, wts, mods, fg)


def _route(logits, b_group, b_expert):
    m = logits.shape[0]
    ngr = b_group.shape[0]
    ne = b_expert.shape[0]
    epg = ne // ngr
    g_prob = jax.nn.softmax(logits[:, :ngr] + b_group.astype(F32), axis=-1)
    g_idx = jnp.argmax(g_prob, axis=-1).astype(jnp.int32)[:, None]
    g_p = jnp.take_along_axis(g_prob, g_idx, axis=1)
    e_logits = (logits[:, ngr:ngr + ne] + b_expert.astype(F32)).reshape(m, ngr, epg)
    e_logits = jnp.take_along_axis(e_logits, g_idx[:, :, None], axis=1)[:, 0]
    lane = lax.broadcasted_iota(jnp.int32, e_logits.shape, 1)
    i1 = jnp.argmax(e_logits, axis=-1).astype(jnp.int32)[:, None]
    rest = jnp.where(lane == i1, -jnp.inf, e_logits)
    i2 = jnp.argmax(rest, axis=-1).astype(jnp.int32)[:, None]
    e_idx = jnp.concatenate([i1, i2], axis=1)
    e_top = jnp.take_along_axis(e_logits, e_idx, axis=1)
    weights = g_p * jax.nn.softmax(e_top, axis=-1)
    experts = g_idx * epg + e_idx
    n_assign = m * MOE_TOP_K
    flat_e = experts.reshape(-1)
    onehot = (flat_e[:, None] == jnp.arange(ne, dtype=flat_e.dtype)[None, :]).astype(jnp.int32)
    csum = jnp.cumsum(onehot, axis=0)
    counts = csum[-1]
    local = jnp.take_along_axis(csum, flat_e[:, None], axis=1)[:, 0] - 1
    nblk_e = (counts + MOE_BLOCK - 1) // MOE_BLOCK
    padded = nblk_e * MOE_BLOCK
    pad_end = jnp.cumsum(padded)
    pad_start = pad_end - padded
    dest = pad_start[flat_e] + local
    n_blocks = -(-(n_assign + ne * (MOE_BLOCK - 1)) // MOE_BLOCK)
    tok = jnp.arange(n_assign, dtype=jnp.int32) // MOE_TOP_K
    n_slots = n_blocks * MOE_BLOCK
    slot_tok = (jnp.arange(n_slots, dtype=jnp.int32) % m).at[dest].set(tok)
    rsup = MOE_SUB * MOE_BLOCK
    ns_e = (nblk_e + MOE_SUB - 1) // MOE_SUB
    sup_end = jnp.cumsum(ns_e)
    sup_start = sup_end - ns_e
    n_sup = (n_blocks + (MOE_SUB - 1) * ne) // MOE_SUB
    sidx = jnp.arange(n_sup, dtype=jnp.int32)
    last = sup_end[-1] - 1
    s_eff = jnp.minimum(sidx, last)
    e_s = jnp.minimum(jnp.searchsorted(sup_end, s_eff, side='right'), ne - 1).astype(jnp.int32)
    k_s = s_eff - sup_start[e_s]
    b0_s = pad_start[e_s] // MOE_BLOCK + MOE_SUB * k_s
    nb_s = jnp.clip(nblk_e[e_s] - MOE_SUB * k_s, 0, MOE_SUB)
    used = sidx <= last
    sup_b0 = jnp.where(used, b0_s, b0_s + nb_s - 1).astype(jnp.int32)
    sup_nb = jnp.where(used, nb_s, 0).astype(jnp.int32)
    pos = ((sup_start[flat_e] + local // rsup) * rsup + local % rsup).astype(jnp.int32).reshape(m, MOE_TOP_K)
    return weights, slot_tok, e_s, sup_b0, sup_nb, sidx, pos


def kernel(x, c, ctx, c_ctx, w_mod, b_mod, norm1_g, w_in, s5_lam_re, s5_lam_im, s5_log_dt, s5_b_re, s5_b_im, s5_c_re, s5_c_im, s5_d, s5_w_val, s5_w_gate, ssd_conv_w, ssd_conv_b, ssd_a_log, ssd_dt_bias, ssd_d, ssd_norm_g, ssd_w_out, w_o, norm2_g, moe_w_group, moe_b_group, moe_w_expert, moe_b_expert, moe_w_gate, moe_w_up, moe_w_down, final_g):
    depth = w_mod.shape[0]
    assert depth == 1, "single-layer block"
    bsz, n_lat, d = x.shape
    l_ctx = ctx.shape[1]
    w5 = s5_d.shape[1]
    nh = ssd_d.shape[1]
    w = nh * SSD_HEAD_DIM
    conv_dim = ssd_conv_w.shape[2]
    ng = (conv_dim - w) // (2 * SSD_STATE)
    ssd_in = w + conv_dim + 2 * nh
    o1, o2 = w5, w5 + ssd_in
    l = 0

    cc = jnp.concatenate([c, c_ctx[None, :]], axis=0)
    cc = jnp.pad(cc, ((0, (-cc.shape[0]) % 8), (0, 0)))
    mods = _mod_call(cc, w_mod[l], b_mod[l]).reshape(cc.shape[0], 6, 1, d)
    i_sh1, i_sc1, i_g1, i_sh2, i_sc2, i_g2 = range(6)

    w_in_l = w_in[l]
    w_gates = w_in_l[:, o2:].astype(BF16)
    o_dt = o1 + w + conv_dim

    hn_rm, hn_cm = _norm_lat_call(x, norm1_g[l], mods, i_sc1, i_sh1)
    hc = _norm_ctx_call(ctx, norm1_g[l], mods, bsz, i_sc1, i_sh1)
    hn_rm = hn_rm.reshape(bsz * n_lat, d)
    hn_cm = hn_cm.reshape(bsz * n_lat, d)
    hc = hc.reshape(bsz * l_ctx, d)

    u_lat = _mm_call(hn_rm, w_in_l, BF16, "in_s5", 0, o1).reshape(bsz, n_lat, w5)
    u_ctx = _mm_call(hc, w_in_l, BF16, "in_s5_ctx", 0, o1).reshape(bsz, l_ctx, w5)
    gates = _mm_call(hn_rm, w_gates, BF16, "in_gates")
    p_lat = _mm_call(hn_cm, w_in_l, BF16, "in_ssd", o1, w + conv_dim).reshape(bsz, n_lat, w + conv_dim)
    p_ctx = _mm_call(hc, w_in_l, BF16, "in_ssd_ctx", o1, w + conv_dim).reshape(bsz, l_ctx, w + conv_dim)
    dt_lat = _mm_call(hn_cm, w_in_l, F32, "in_dt", o_dt, 2 * nh).reshape(bsz, n_lat, LANE)
    dt_ctx = _mm_call(hc, w_in_l, F32, "in_dt_ctx", o_dt, 2 * nh).reshape(bsz, l_ctx, LANE)

    nj = w5 // LANE
    bmat, cmat, lre, lim = _s5_params(s5_lam_re[l], s5_lam_im[l], s5_log_dt[l], s5_b_re[l], s5_b_im[l],
                                      s5_c_re[l], s5_c_im[l], bsz)
    s5_zero = jnp.zeros((nj, 2, 2 * bsz, lre.shape[-1]), F32)
    (s5_ctx,) = _s5_call(u_ctx, bmat, cmat, lre, lim, s5_zero, False)
    ya_f, ya_b, _ = _s5_call(u_lat, bmat, cmat, lre, lim, s5_ctx, True)

    xbc_ctx = _conv_call(p_ctx, ssd_conv_w[l], ssd_conv_b[l], w)
    xbc_lat = _conv_call(p_lat, ssd_conv_w[l], ssd_conv_b[l], w)
    pad_h = LANE - 2 * nh
    bias = jnp.pad(ssd_dt_bias[l].astype(F32).reshape(1, 2 * nh), ((0, 0), (0, pad_h)))
    alog = jnp.pad(ssd_a_log[l].astype(F32).reshape(1, 2 * nh), ((0, 0), (0, pad_h)))
    h_zero = jnp.zeros((bsz, 2, SSD_STATE, w), F32)
    (h_ctx,) = _ssd_call(xbc_ctx, dt_ctx, h_zero, bias, alog, nh, ng, False)
    y_f, y_b, _ = _ssd_call(xbc_lat, dt_lat, h_ctx, bias, alog, nh, ng, True)
    d_vec = jnp.repeat(ssd_d[l].astype(F32), SSD_HEAD_DIM).reshape(1, w)
    y_ssd = _gnorm_call(y_f, y_b, xbc_lat, p_lat, d_vec, ssd_norm_g[l].astype(F32).reshape(1, w), ng)

    m_lat = bsz * n_lat
    part_a = _glu_call(u_lat.reshape(m_lat, w5), ya_f.reshape(m_lat, w5), ya_b.reshape(m_lat, w5),
                       s5_d[l].astype(F32).reshape(1, w5),
                       s5_w_val[l].astype(BF16), s5_w_gate[l].astype(BF16), gates)
    merged = _merge_call(y_ssd, ssd_w_out[l], gates, part_a)
    x1 = _resid_call(merged, w_o[l], x, mods, i_g1)

    ngr = moe_w_group.shape[-1]
    ne = moe_w_expert.shape[-1]
    wr = jnp.concatenate([moe_w_group[l], moe_w_expert[l]], axis=1).astype(F32)
    wr = jnp.pad(wr, ((0, 0), (0, (-(ngr + ne)) % LANE)))
    hx, logits = _norm_router_call(x1, norm2_g[l], mods, i_sc2, i_sh2, wr)
    m = bsz * n_lat
    hx = hx.reshape(m, d)
    weights, slot_tok, sup_e, sup_b0, sup_nb, sup_out, pos = _route(
        logits.reshape(m, -1), moe_b_group[l], moe_b_expert[l])
    xs = hx[slot_tok]
    y_slots = _moe_call(sup_e, sup_b0, sup_nb, sup_out, xs, moe_w_gate[l], moe_w_up[l], moe_w_down[l])
    ya = y_slots[pos[:, 0]]
    yb = y_slots[pos[:, 1]]
    return _final_call(x1, ya, yb, weights.astype(F32), mods, i_g2, final_g.reshape(1, d))
```

```python
import functools
import math

import jax
import jax.numpy as jnp
from jax import lax
from jax.experimental import pallas as pl
from jax.experimental.pallas import tpu as pltpu

F32 = jnp.float32
BF16 = jnp.bfloat16

GRID_W = 64
EPS = 1e-6
LANE = 128
SSD_HEAD_DIM = 64
SSD_STATE = 128
SSD_CHUNK = 128
SSD_CONV = 5
MOE_TOP_K = 2
MOE_BLOCK = 256
MOE_SUB = 4
XPOSE_TILE = 16
XPOSE_PAD = 8
S5_CHUNK = 128
S5_SLICE = 256
S5_ROW_PAD = 8


def _cp(sem, mb):
    return pltpu.CompilerParams(dimension_semantics=sem, vmem_limit_bytes=mb * 1024 * 1024)


def _sigmoid(x):
    return 1.0 / (1.0 + jnp.exp(-x))


def _silu(x):
    return x * _sigmoid(x)


def _mod_kernel(c_ref, w_ref, b_ref, o_ref):
    s = _silu(c_ref[...])
    o_ref[...] = jnp.dot(s.astype(BF16), w_ref[...].astype(BF16),
                         preferred_element_type=F32) + b_ref[...]


def _mod_call(cc, w, b):
    r, d = cc.shape
    n = w.shape[1]
    tn = min(512, n)
    return pl.pallas_call(
        _mod_kernel, grid=(n // tn,),
        in_specs=[pl.BlockSpec((r, d), lambda j: (0, 0)),
                  pl.BlockSpec((d, tn), lambda j: (0, j)),
                  pl.BlockSpec((1, tn), lambda j: (0, j))],
        out_specs=pl.BlockSpec((r, tn), lambda j: (0, j)),
        out_shape=jax.ShapeDtypeStruct((r, n), F32),
        compiler_params=_cp(("parallel",), 40), name="mod")(cc, w, b.reshape(1, n))


def _rmsmod(x, g, sc, sh):
    ms = jnp.mean(x * x, axis=-1, keepdims=True)
    return (x * lax.rsqrt(ms + EPS) * g) * (1.0 + sc) + sh


def _norm_lat_kernel(x_ref, g_ref, sc_ref, sh_ref, orm_ref, ocm_ref, t_ref, *, tb, d):
    pitch = tb + XPOSE_PAD
    for r in range(tb):
        y = _rmsmod(x_ref[r], g_ref[...], sc_ref[...], sh_ref[...])
        orm_ref[r] = y.astype(BF16)
        for kk in range(d // LANE):
            t_ref[kk, r * pitch:r * pitch + tb, :] = y[:, kk * LANE:(kk + 1) * LANE]
    for wl in range(tb):
        ocm_ref[wl] = jnp.concatenate(
            [t_ref[kk, pl.ds(wl, tb, stride=pitch), :] for kk in range(d // LANE)], axis=1).astype(BF16)


def _norm_lat_call(x, g, mods, isc, ish):
    b, l, d = x.shape
    rows = l // GRID_W
    tb = XPOSE_TILE
    xv = x.reshape(b, rows, GRID_W, d)
    orm, ocm = pl.pallas_call(
        functools.partial(_norm_lat_kernel, tb=tb, d=d), grid=(b, rows // tb, GRID_W // tb),
        in_specs=[pl.BlockSpec((None, tb, tb, d), lambda i, r, c: (i, r, c, 0)),
                  pl.BlockSpec((1, d), lambda i, r, c: (0, 0)),
                  pl.BlockSpec((None, None, 1, d), lambda i, r, c: (i, isc, 0, 0)),
                  pl.BlockSpec((None, None, 1, d), lambda i, r, c: (i, ish, 0, 0))],
        out_specs=[pl.BlockSpec((None, tb, tb, d), lambda i, r, c: (i, r, c, 0)),
                   pl.BlockSpec((None, tb, tb, d), lambda i, r, c: (i, c, r, 0))],
        out_shape=[jax.ShapeDtypeStruct((b, rows, GRID_W, d), BF16),
                   jax.ShapeDtypeStruct((b, GRID_W, rows, d), BF16)],
        scratch_shapes=[pltpu.VMEM((d // LANE, tb * (tb + XPOSE_PAD), LANE), F32)],
        compiler_params=_cp(("parallel", "parallel", "parallel"), 48),
        name="norm1_lat")(xv, g.reshape(1, d), mods, mods)
    return orm.reshape(b, l, d), ocm.reshape(b, l, d)


def _norm_kernel(x_ref, g_ref, sc_ref, sh_ref, o_ref):
    o_ref[...] = _rmsmod(x_ref[...], g_ref[...], sc_ref[...], sh_ref[...]).astype(o_ref.dtype)


def _norm_ctx_call(x, g, mods, row, isc, ish):
    b, l, d = x.shape
    tr = min(256, l)
    return pl.pallas_call(
        _norm_kernel, grid=(b, l // tr),
        in_specs=[pl.BlockSpec((None, tr, d), lambda i, r: (i, r, 0)),
                  pl.BlockSpec((1, d), lambda i, r: (0, 0)),
                  pl.BlockSpec((None, None, 1, d), lambda i, r: (row, isc, 0, 0)),
                  pl.BlockSpec((None, None, 1, d), lambda i, r: (row, ish, 0, 0))],
        out_specs=pl.BlockSpec((None, tr, d), lambda i, r: (i, r, 0)),
        out_shape=jax.ShapeDtypeStruct((b, l, d), BF16),
        compiler_params=_cp(("parallel", "parallel"), 40), name="norm1_ctx")(x, g.reshape(1, d), mods, mods)


def _norm_router_kernel(x_ref, g_ref, sc_ref, sh_ref, wr_ref, o_ref, lg_ref):
    y = _rmsmod(x_ref[...], g_ref[...], sc_ref[...], sh_ref[...])
    o_ref[...] = y.astype(BF16)
    lg_ref[...] = jnp.dot(y, wr_ref[...], precision=lax.Precision.HIGHEST, preferred_element_type=F32)


def _norm_router_call(x, g, mods, isc, ish, wr):
    b, l, d = x.shape
    tr = min(256, l)
    nr = wr.shape[1]
    return pl.pallas_call(
        _norm_router_kernel, grid=(b, l // tr),
        in_specs=[pl.BlockSpec((None, tr, d), lambda i, r: (i, r, 0)),
                  pl.BlockSpec((1, d), lambda i, r: (0, 0)),
                  pl.BlockSpec((None, None, 1, d), lambda i, r: (i, isc, 0, 0)),
                  pl.BlockSpec((None, None, 1, d), lambda i, r: (i, ish, 0, 0)),
                  pl.BlockSpec((d, nr), lambda i, r: (0, 0))],
        out_specs=[pl.BlockSpec((None, tr, d), lambda i, r: (i, r, 0)),
                   pl.BlockSpec((None, tr, nr), lambda i, r: (i, r, 0))],
        out_shape=[jax.ShapeDtypeStruct((b, l, d), BF16), jax.ShapeDtypeStruct((b, l, nr), F32)],
        compiler_params=_cp(("parallel", "parallel"), 40), name="norm2_router")(x, g.reshape(1, d), mods, mods, wr)


def _mm_kernel(a_ref, b_ref, o_ref):
    o_ref[...] = jnp.dot(a_ref[...], b_ref[...].astype(BF16), preferred_element_type=F32).astype(o_ref.dtype)


def _mm_tiles(m, n):
    tm = min(1024, m)
    tn = min(512, n)
    return tm, tn


def _mm_call(a, b, out_dtype, name, col0=0, n=None):
    m, k = a.shape
    n = b.shape[1] - col0 if n is None else n
    tm, tn = _mm_tiles(m, n)
    if col0 % tn or n % tn or n % LANE:
        b = b[:, col0:col0 + n]
        pad = (-n) % LANE
        b = jnp.pad(b, ((0, 0), (0, pad)))
        n, col0 = n + pad, 0
        tm, tn = _mm_tiles(m, n)
    off = col0 // tn
    return pl.pallas_call(
        _mm_kernel, grid=(m // tm, n // tn),
        in_specs=[pl.BlockSpec((tm, k), lambda i, j: (i, 0)),
                  pl.BlockSpec((k, tn), lambda i, j: (0, j + off))],
        out_specs=pl.BlockSpec((tm, tn), lambda i, j: (i, j)),
        out_shape=jax.ShapeDtypeStruct((m, n), out_dtype),
        compiler_params=_cp(("parallel", "parallel"), 48), name=name)(a, b)


def _gelu_tanh(x):
    return x * (0.5 * (1.0 + jnp.tanh(math.sqrt(2.0 / math.pi) * (x + 0.044715 * (x * x * x)))))


def _glu_kernel(u_ref, yf_ref, yb_ref, d_ref, wv_ref, wg_ref, gate_ref, o_ref, a_ref):
    @pl.when(pl.program_id(1) == 0)
    def _():
        y = d_ref[...] * u_ref[...].astype(F32) + yf_ref[...].astype(F32) + yb_ref[...].astype(F32)
        a_ref[...] = _gelu_tanh(y).astype(BF16)

    a = a_ref[...]
    val = jnp.dot(a, wv_ref[...], preferred_element_type=F32)
    gl = jnp.dot(a, wg_ref[...], preferred_element_type=F32)
    o_ref[...] = (_sigmoid(gate_ref[...].astype(F32)) * (val * _sigmoid(gl))).astype(o_ref.dtype)


def _glu_call(u, yf, yb, dvec, wv, wg, gates):
    m, k = u.shape
    n = wv.shape[1]
    tm, tn = _mm_tiles(m, n)
    row = pl.BlockSpec((tm, k), lambda i, j: (i, 0))
    return pl.pallas_call(
        _glu_kernel, grid=(m // tm, n // tn),
        in_specs=[row, row, row,
                  pl.BlockSpec((1, k), lambda i, j: (0, 0)),
                  pl.BlockSpec((k, tn), lambda i, j: (0, j)),
                  pl.BlockSpec((k, tn), lambda i, j: (0, j)),
                  pl.BlockSpec((tm, tn), lambda i, j: (i, j))],
        out_specs=pl.BlockSpec((tm, tn), lambda i, j: (i, j)),
        out_shape=jax.ShapeDtypeStruct((m, n), BF16),
        scratch_shapes=[pltpu.VMEM((tm, k), BF16)],
        compiler_params=_cp(("parallel", "arbitrary"), 56), name="s5_glu")(u, yf, yb, dvec, wv, wg, gates)


def _merge_kernel(a_ref, w_ref, gate_ref, pa_ref, o_ref):
    br = jnp.dot(a_ref[...], w_ref[...].astype(BF16), preferred_element_type=F32)
    o_ref[...] = (pa_ref[...].astype(F32) + _sigmoid(gate_ref[...].astype(F32)) * br).astype(o_ref.dtype)


def _merge_call(a, w, gates, part_a):
    m, k = a.shape
    n = w.shape[1]
    tm, tn = _mm_tiles(m, n)
    off = n // tn
    return pl.pallas_call(
        _merge_kernel, grid=(m // tm, n // tn),
        in_specs=[pl.BlockSpec((tm, k), lambda i, j: (i, 0)),
                  pl.BlockSpec((k, tn), lambda i, j: (0, j)),
                  pl.BlockSpec((tm, tn), lambda i, j: (i, j + off)),
                  pl.BlockSpec((tm, tn), lambda i, j: (i, j))],
        out_specs=pl.BlockSpec((tm, tn), lambda i, j: (i, j)),
        out_shape=jax.ShapeDtypeStruct((m, n), BF16),
        compiler_params=_cp(("parallel", "parallel"), 48), name="ssd_out_merge")(a, w, gates, part_a)


def _resid_kernel(a_ref, w_ref, x_ref, g_ref, o_ref):
    mix = jnp.dot(a_ref[...], w_ref[...].astype(BF16), preferred_element_type=F32)
    o_ref[...] = x_ref[...] + g_ref[...] * mix


def _resid_call(a, w, x, mods, ig):
    b, l, d = x.shape
    k = a.shape[1]
    tm, tn = _mm_tiles(l, d)
    nt = l // tm
    return pl.pallas_call(
        _resid_kernel, grid=(b * nt, d // tn),
        in_specs=[pl.BlockSpec((tm, k), lambda i, j: (i, 0)),
                  pl.BlockSpec((k, tn), lambda i, j: (0, j)),
                  pl.BlockSpec((None, tm, tn), lambda i, j: (i // nt, i % nt, j)),
                  pl.BlockSpec((None, None, 1, tn), lambda i, j: (i // nt, ig, 0, j))],
        out_specs=pl.BlockSpec((None, tm, tn), lambda i, j: (i // nt, i % nt, j)),
        out_shape=jax.ShapeDtypeStruct((b, l, d), F32),
        compiler_params=_cp(("parallel", "parallel"), 48), name="w_o_resid")(a, w, x, mods)


def _s5_scan_steps(buf_ref, carry, ar, ai, lo, hi, *, q, nk, pitch, slice_fn):
    for i in range((hi - lo) // 8):
        slice_fn(i)
        for s8 in range(8):
            rows = pl.ds(lo + i * 8 + s8, q, stride=pitch)
            out = []
            for k in range(nk):
                hr, hi_ = carry[k]
                nr = ar[k] * hr - ai[k] * hi_ + buf_ref[k, rows, :]
                ni = ar[k] * hi_ + ai[k] * hr + buf_ref[nk + k, rows, :]
                buf_ref[k, rows, :] = nr
                buf_ref[nk + k, rows, :] = ni
                out.append((nr, ni))
            carry = tuple(out)
    return carry


def _s5_kernel(*refs, nb, t, p8, need_y, nc):
    if need_y:
        (uf0_ref, ub0_ref, ufn_ref, ubn_ref, b_ref, c_ref, lre_ref, lim_ref, h0_ref,
         yf_ref, yb_ref, h_ref, buf0, buf1, buf2, u_ref, yacc_ref) = refs
    else:
        (uf0_ref, ub0_ref, ufn_ref, ubn_ref, b_ref, c_ref, lre_ref, lim_ref, h0_ref,
         h_ref, buf0, buf1, buf2, u_ref) = refs
    c = pl.program_id(1)
    q = 2 * nb
    nk = p8 // LANE
    nsl = 2 * p8 // S5_SLICE
    spl = S5_SLICE // LANE
    pitch = t + S5_ROW_PAD
    bufs = (buf0, buf1, buf2)
    rev = (lax.broadcasted_iota(jnp.int32, (t, t), 0) + lax.broadcasted_iota(jnp.int32, (t, t), 1)
           == t - 1).astype(BF16)

    def stage_u(uf, ub):
        u_ref[0] = uf[...].reshape(nb * t, LANE)
        for b in range(nb):
            u_ref[1, b * t:(b + 1) * t, :] = jnp.dot(rev, ub[b], preferred_element_type=F32).astype(BF16)

    def bu_slice(dst, i):
        d = i // nsl
        sl = i - d * nsl
        bu = jnp.dot(u_ref[d], b_ref[d, sl], preferred_element_type=F32)
        for b in range(nb):
            r0 = (d * nb + b) * pitch
            for kk in range(spl):
                dst[sl * spl + kk, pl.ds(r0, t), :] = bu[b * t:(b + 1) * t, kk * LANE:(kk + 1) * LANE]

    def y_slice(src, i):
        d = i // nsl
        sl = i - d * nsl
        parts = []
        for b in range(nb):
            r0 = (d * nb + b) * pitch
            parts.append(jnp.concatenate(
                [src[sl * spl + kk, pl.ds(r0, t), :].astype(BF16) for kk in range(spl)], axis=1))
        h = jnp.concatenate(parts, axis=0)
        yacc_ref[d] += jnp.dot(h, c_ref[d, sl], preferred_element_type=F32)

    def y_store():
        yf_ref[...] = yacc_ref[0].reshape(nb, t, LANE).astype(BF16)
        for b in range(nb):
            yb_ref[b] = jnp.dot(rev, yacc_ref[1, b * t:(b + 1) * t, :].astype(BF16),
                                preferred_element_type=F32).astype(BF16)

    @pl.when(c == 0)
    def _():
        h_ref[...] = h0_ref[...]
        if need_y:
            buf2[...] = jnp.zeros(buf2.shape, F32)
        stage_u(uf0_ref, ub0_ref)
        for i in range(2 * nsl):
            bu_slice(buf0, i)

    ar = [lre_ref[:, k * LANE:(k + 1) * LANE] for k in range(nk)]
    ai = [lim_ref[:, k * LANE:(k + 1) * LANE] for k in range(nk)]
    half = t // 2
    assert half // 8 == 2 * nsl, "one matmul slice per recurrence-loop iteration"

    def run_chunk(cur, nxt, prv):
        scan = functools.partial(_s5_scan_steps, cur, ar=ar, ai=ai, q=q, nk=nk, pitch=pitch)
        stage_u(ufn_ref, ubn_ref)
        carry = tuple((h_ref[0, :, k * LANE:(k + 1) * LANE], h_ref[1, :, k * LANE:(k + 1) * LANE])
                      for k in range(nk))
        carry = scan(carry, lo=0, hi=half, slice_fn=functools.partial(bu_slice, nxt))
        if need_y:
            yacc_ref[...] = jnp.zeros(yacc_ref.shape, F32)
            carry = scan(carry, lo=half, hi=t, slice_fn=functools.partial(y_slice, prv))
            y_store()
        else:
            carry = scan(carry, lo=half, hi=t, slice_fn=lambda i: None)
        for k in range(nk):
            h_ref[0, :, k * LANE:(k + 1) * LANE] = carry[k][0]
            h_ref[1, :, k * LANE:(k + 1) * LANE] = carry[k][1]

    for r in range(3):
        roles = (bufs[r], bufs[(r + 1) % 3], bufs[(r + 2) % 3])
        pl.when(jnp.logical_and(c < nc, lax.rem(c, 3) == r))(functools.partial(run_chunk, *roles))
    if need_y:
        @pl.when(c == nc)
        def _():
            yacc_ref[...] = jnp.zeros(yacc_ref.shape, F32)
            for i in range(2 * nsl):
                y_slice(bufs[(nc + 2) % 3], i)
            y_store()


def _s5_call(u, bmat, cmat, lre, lim, h0, need_y):
    nb, l, w5 = u.shape
    nj = w5 // LANE
    q = 2 * nb
    t = S5_CHUNK
    p8 = lre.shape[-1]
    nc = l // t
    nsl = 2 * p8 // S5_SLICE
    nxt = lambda c: jnp.minimum(c + 1, nc - 1)
    prv = lambda c: jnp.maximum(c - 1, 0)
    hspec = pl.BlockSpec((None, 2, q, p8), lambda j, c: (j, 0, 0, 0))
    ublk = (nb, t, LANE)
    in_specs = [pl.BlockSpec(ublk, lambda j, c: (0, 0, j)), pl.BlockSpec(ublk, lambda j, c: (0, nc - 1, j)),
                pl.BlockSpec(ublk, lambda j, c: (0, nxt(c), j)),
                pl.BlockSpec(ublk, lambda j, c: (0, nc - 1 - nxt(c), j)),
                pl.BlockSpec((None, 2, nsl, LANE, S5_SLICE), lambda j, c: (j, 0, 0, 0, 0)),
                pl.BlockSpec((None, 2, nsl, S5_SLICE, LANE), lambda j, c: (j, 0, 0, 0, 0)),
                pl.BlockSpec((None, q, p8), lambda j, c: (j, 0, 0)),
                pl.BlockSpec((None, q, p8), lambda j, c: (j, 0, 0)),
                hspec]
    out_specs = [hspec]
    out_shape = [jax.ShapeDtypeStruct((nj, 2, q, p8), F32)]
    scratch = [pltpu.VMEM((2 * p8 // LANE, q * (t + S5_ROW_PAD), LANE), F32) for _ in range(3)]
    scratch = scratch + [pltpu.VMEM((2, nb * t, LANE), BF16)]
    if need_y:
        out_specs = [pl.BlockSpec(ublk, lambda j, c: (0, prv(c), j)),
                     pl.BlockSpec(ublk, lambda j, c: (0, nc - 1 - prv(c), j))] + out_specs
        out_shape = [jax.ShapeDtypeStruct((nb, l, w5), BF16)] * 2 + out_shape
        scratch = scratch + [pltpu.VMEM((2, nb * t, LANE), F32)]
    bsl = bmat.reshape(nj, 2, LANE, nsl, S5_SLICE).transpose(0, 1, 3, 2, 4)
    csl = cmat.reshape(nj, 2, nsl, S5_SLICE, LANE)
    kern = functools.partial(_s5_kernel, nb=nb, t=t, p8=p8, need_y=need_y, nc=nc)
    return pl.pallas_call(
        kern, grid=(nj, nc + 1 if need_y else nc), in_specs=in_specs, out_specs=out_specs, out_shape=out_shape,
        scratch_shapes=scratch, compiler_params=_cp(("parallel", "arbitrary"), 48),
        name="s5_scan" if need_y else "s5_scan_ctx")(u, u, u, u, bsl, csl, lre, lim, h0)


def _s5_params(lam_re, lam_im, log_dt, b_re, b_im, c_re, c_im, nb):
    _, g, p = lam_re.shape
    s = b_re.shape[-1]
    gpb = LANE // s
    nj = g // gpb
    lam = lax.complex(lam_re.astype(F32), lam_im.astype(F32))
    lam_bar = jnp.exp(lam * jnp.exp(log_dt.astype(F32))[..., None])
    b_bar = ((lam_bar - 1.0) / lam)[..., None] * lax.complex(b_re.astype(F32), b_im.astype(F32))
    eye = jnp.eye(gpb, dtype=F32)

    def bmat_of(bpart):
        bb = bpart.reshape(2, nj, gpb, p, s)
        m = jnp.einsum('dnkps,kl->dnkslp', bb, eye)
        return m.reshape(2, nj, gpb * s, gpb * p).transpose(1, 0, 2, 3)

    bmat = jnp.concatenate([bmat_of(b_bar.real), bmat_of(b_bar.imag)], axis=-1).astype(BF16)

    def cmat_of(cpart):
        cc = cpart.astype(F32).reshape(2, nj, gpb, s, p)
        m = jnp.einsum('dnksp,kl->dnkpls', cc, eye)
        return m.reshape(2, nj, gpb * p, gpb * s).transpose(1, 0, 2, 3)

    cmat = jnp.concatenate([cmat_of(c_re), -cmat_of(c_im)], axis=-2).astype(BF16)

    def lam_of(part):
        v = part.reshape(2, nj, gpb * p).transpose(1, 0, 2)
        return jnp.repeat(v, nb, axis=1)

    return bmat, cmat, lam_of(lam_bar.real), lam_of(lam_bar.imag)


def _conv_kernel(x_ref, w_ref, b_ref, o_ref, *, l):
    x = x_ref[...].astype(F32)
    rows = lax.broadcasted_iota(jnp.int32, x.shape, 0)
    half = SSD_CONV // 2
    acc = x * w_ref[half:half + 1, :] + b_ref[...]
    for k in range(SSD_CONV):
        if k == half:
            continue
        off = k - half
        xs = pltpu.roll(x, shift=(-off) % l, axis=0)
        valid = jnp.logical_and(rows + off >= 0, rows + off < l)
        acc = acc + jnp.where(valid, xs, 0.0) * w_ref[k:k + 1, :]
    o_ref[...] = _silu(acc).astype(o_ref.dtype)


def _conv_call(proj, w, bias, col0):
    b, l, _ = proj.shape
    c = w.shape[1]
    tc = 256
    off = col0 // tc
    return pl.pallas_call(
        functools.partial(_conv_kernel, l=l), grid=(b, c // tc),
        in_specs=[pl.BlockSpec((None, l, tc), lambda i, j: (i, 0, j + off)),
                  pl.BlockSpec((SSD_CONV, tc), lambda i, j: (0, j)),
                  pl.BlockSpec((1, tc), lambda i, j: (0, j))],
        out_specs=pl.BlockSpec((None, l, tc), lambda i, j: (i, 0, j)),
        out_shape=jax.ShapeDtypeStruct((b, l, c), BF16),
        compiler_params=_cp(("parallel", "parallel"), 40), name="ssd_conv")(proj, w, bias.reshape(1, c))


def _softplus(x):
    return jnp.maximum(x, 0.0) + jnp.log1p(jnp.exp(-jnp.abs(x)))


def _ssd_dir(xbc_ref, dt_ref, y_ref, h_ref, bias, a_neg, d, *, nh, ng, need_y):
    qn = SSD_CHUNK
    hd = SSD_HEAD_DIM
    w = nh * hd
    gw = w // ng
    gn = ng * SSD_STATE
    ii = lax.broadcasted_iota(jnp.int32, (qn, qn), 0)
    jj = lax.broadcasted_iota(jnp.int32, (qn, qn), 1)
    mask = (jj <= ii) if d == 0 else (jj >= ii)
    lmat = mask.astype(F32)
    dtv = _softplus(dt_ref[...] + bias)
    cum = jnp.dot(lmat, dtv * a_neg, precision=lax.Precision.HIGHEST, preferred_element_type=F32)
    cum_t = cum.T
    edge = qn - 1 if d == 0 else 0
    tot = cum[edge:edge + 1, :]
    dt_t = dtv.T
    wt_t = dt_t * jnp.exp(cum_t[:, edge:edge + 1] - cum_t)
    decay = jnp.exp(tot)
    lane = lax.broadcasted_iota(jnp.int32, (qn, LANE), 1)
    left = lane < hd
    zero = jnp.zeros((), BF16)
    for g in range(ng):
        bg = xbc_ref[:, w + g * SSD_STATE:w + (g + 1) * SSD_STATE]
        cg = xbc_ref[:, w + gn + g * SSD_STATE:w + gn + (g + 1) * SSD_STATE]
        bg_t = bg.astype(F32).T
        s_in = h_ref[d, :, g * gw:(g + 1) * gw]
        if need_y:
            cb = lax.dot_general(cg, bg, (((1,), (1,)), ((), ())), preferred_element_type=F32)
            yoff = jnp.dot(cg, s_in.astype(BF16), preferred_element_type=F32)
        for pr in range(gw // LANE):
            c0 = d * nh + (g * gw) // hd + 2 * pr
            col = g * gw + pr * LANE
            xp = xbc_ref[:, col:col + LANE]
            r = jnp.concatenate([jnp.where(left, xp, zero), jnp.where(left, zero, xp)], axis=0)
            tops, bots, cols = [], [], []
            for c in (c0, c0 + 1):
                bots.append((bg_t * wt_t[c:c + 1, :]).astype(BF16))
                if need_y:
                    ccol = jnp.broadcast_to(cum[:, c:c + 1], (qn, qn))
                    seg = jnp.where(mask, jnp.exp(ccol - cum_t[c:c + 1, :]), 0.0)
                    tops.append((cb * seg * dt_t[c:c + 1, :]).astype(BF16))
                    cols.append(ccol)
            dec = jnp.where(left[0:1], decay[:, c0:c0 + 1], decay[:, c0 + 1:c0 + 2])
            s_old = s_in[:, pr * LANE:(pr + 1) * LANE]
            if need_y:
                lhs = jnp.concatenate([jnp.concatenate(tops, axis=1), jnp.concatenate(bots, axis=1)], axis=0)
                out = jnp.dot(lhs, r, preferred_element_type=F32)
                ec = jnp.exp(jnp.where(left, cols[0], cols[1]))
                y_ref[:, col:col + LANE] = (out[:qn] + yoff[:, pr * LANE:(pr + 1) * LANE] * ec).astype(y_ref.dtype)
                s_new = out[qn:]
            else:
                s_new = jnp.dot(jnp.concatenate(bots, axis=1), r, preferred_element_type=F32)
            h_ref[d, :, col:col + LANE] = s_old * dec + s_new


def _ssd_kernel(*refs, nh, ng, need_y):
    if need_y:
        xf_ref, xb_ref, dtf_ref, dtb_ref, h0_ref, bias_ref, alog_ref, yf_ref, yb_ref, h_ref = refs
    else:
        xf_ref, xb_ref, dtf_ref, dtb_ref, h0_ref, bias_ref, alog_ref, h_ref = refs
        yf_ref = yb_ref = None

    @pl.when(pl.program_id(1) == 0)
    def _():
        h_ref[...] = h0_ref[...]

    bias = bias_ref[...]
    a_neg = -jnp.exp(alog_ref[...])
    _ssd_dir(xf_ref, dtf_ref, yf_ref, h_ref, bias, a_neg, 0, nh=nh, ng=ng, need_y=need_y)
    _ssd_dir(xb_ref, dtb_ref, yb_ref, h_ref, bias, a_neg, 1, nh=nh, ng=ng, need_y=need_y)


def _ssd_call(xbc, dt, h0, bias, alog, nh, ng, need_y):
    b, l, cd = xbc.shape
    w = nh * SSD_HEAD_DIM
    nc = l // SSD_CHUNK
    q = SSD_CHUNK
    fwd = lambda i, s: (i, s, 0)
    bwd = lambda i, s: (i, nc - 1 - s, 0)
    hspec = pl.BlockSpec((None, 2, SSD_STATE, w), lambda i, s: (i, 0, 0, 0))
    in_specs = [pl.BlockSpec((None, q, cd), fwd), pl.BlockSpec((None, q, cd), bwd),
                pl.BlockSpec((None, q, LANE), fwd), pl.BlockSpec((None, q, LANE), bwd),
                hspec,
                pl.BlockSpec((1, LANE), lambda i, s: (0, 0)), pl.BlockSpec((1, LANE), lambda i, s: (0, 0))]
    out_specs = [hspec]
    out_shape = [jax.ShapeDtypeStruct((b, 2, SSD_STATE, w), F32)]
    if need_y:
        out_specs = [pl.BlockSpec((None, q, w), fwd), pl.BlockSpec((None, q, w), bwd)] + out_specs
        out_shape = [jax.ShapeDtypeStruct((b, l, w), BF16)] * 2 + out_shape
    return pl.pallas_call(
        functools.partial(_ssd_kernel, nh=nh, ng=ng, need_y=need_y), grid=(b, nc),
        in_specs=in_specs, out_specs=out_specs, out_shape=out_shape,
        compiler_params=_cp(("parallel", "arbitrary"), 48),
        name="ssd_scan" if need_y else "ssd_scan_ctx")(xbc, xbc, dt, dt, h0, bias, alog)


def _gnorm_kernel(yf_ref, yb_ref, xs_ref, z_ref, d_ref, g_ref, o_ref, t_ref, *, rows, tb):
    y = d_ref[...] * xs_ref[...].astype(F32) + yf_ref[...].astype(F32) + yb_ref[...].astype(F32)
    y = y * _silu(z_ref[...].astype(F32))
    ms = jnp.mean(y * y, axis=-1, keepdims=True)
    res = y * lax.rsqrt(ms + EPS) * g_ref[...]
    gw = res.shape[1]
    pitch = rows + XPOSE_PAD
    for wl in range(tb):
        for kk in range(gw // LANE):
            t_ref[kk, wl * pitch:wl * pitch + rows, :] = res[wl * rows:(wl + 1) * rows, kk * LANE:(kk + 1) * LANE]
    for r in range(rows):
        o_ref[r] = jnp.concatenate(
            [t_ref[kk, pl.ds(r, tb, stride=pitch), :] for kk in range(gw // LANE)], axis=1).astype(o_ref.dtype)


def _gnorm_call(yf, yb, xbc, proj, dvec, gvec, ng):
    b, l, w = yf.shape
    rows = l // GRID_W
    tb = XPOSE_TILE
    gw = w // ng
    blk = pl.BlockSpec((None, tb * rows, gw), lambda i, s, g: (i, s, g))
    vec = pl.BlockSpec((1, gw), lambda i, s, g: (0, g))
    out = pl.pallas_call(
        functools.partial(_gnorm_kernel, rows=rows, tb=tb), grid=(b, GRID_W // tb, ng),
        in_specs=[blk, blk, blk, blk, vec, vec],
        out_specs=pl.BlockSpec((None, rows, tb, gw), lambda i, s, g: (i, 0, s, g)),
        out_shape=jax.ShapeDtypeStruct((b, rows, GRID_W, w), BF16),
        scratch_shapes=[pltpu.VMEM((gw // LANE, tb * (rows + XPOSE_PAD), LANE), F32)],
        compiler_params=_cp(("parallel", "parallel", "parallel"), 40),
        name="ssd_gnorm")(yf, yb, xbc, proj, dvec, gvec)
    return out.reshape(b * l, w)


def _moe_kernel(e_ref, b0_ref, nb_ref, so_ref, *refs, nsub, nh):
    xs = refs[:nsub]
    wg_ref, wu_ref, wd_ref, o_ref, act_ref, wdb_ref = refs[nsub:]
    s = pl.program_id(0)
    t = pl.program_id(1)
    nb = nb_ref[s]

    @pl.when(jnp.logical_and(t < nh, nb > 0))
    def _():
        wg = wg_ref[...].astype(BF16)
        wu = wu_ref[...].astype(BF16)
        for k in range(nsub):
            @pl.when(k < nb)
            def _():
                x = xs[k][...]
                gt = jnp.dot(x, wg, preferred_element_type=F32)
                up = jnp.dot(x, wu, preferred_element_type=F32)
                act_ref[k, t] = (_silu(gt) * up).astype(BF16)

    @pl.when(jnp.logical_and(t >= nh, nb > 0))
    def _():
        wdb_ref[...] = wd_ref[...].astype(BF16)
        for k in range(nsub):
            rows = slice(k * MOE_BLOCK, (k + 1) * MOE_BLOCK)

            @pl.when(k < nb)
            def _():
                a = jnp.concatenate([act_ref[k, h] for h in range(nh)], axis=1)
                o_ref[rows, :] = jnp.dot(a, wdb_ref[...], preferred_element_type=F32).astype(o_ref.dtype)

            @pl.when(k >= nb)
            def _():
                o_ref[rows, :] = jnp.zeros((MOE_BLOCK, o_ref.shape[1]), o_ref.dtype)

    @pl.when(jnp.logical_and(t >= nh, nb == 0))
    def _():
        o_ref[...] = jnp.zeros(o_ref.shape, o_ref.dtype)


def _moe_call(sup_e, sup_b0, sup_nb, sup_out, xs, w_gate, w_up, w_down):
    n_slots, d = xs.shape
    hid = w_gate.shape[-1]
    th = min(256, hid)
    tn = min(1024, d)
    nh = hid // th
    nt = d // tn
    ns = sup_e.shape[0]
    rsup = MOE_SUB * MOE_BLOCK

    def x_map(k):
        return lambda s, t, e, b0, nb, so: (b0[s] + jnp.minimum(k, jnp.maximum(nb[s] - 1, 0)), 0)

    def hid_idx(s, t, nb):
        return jnp.where(nb[s] > 0, jnp.minimum(t, nh - 1), nh - 1)

    def col_idx(s, t, nb):
        return jnp.where(nb[s] > 0, jnp.clip(t - nh, 0, nt - 1), nt - 1)

    gs = pltpu.PrefetchScalarGridSpec(
        num_scalar_prefetch=4, grid=(ns, nh + nt),
        in_specs=[pl.BlockSpec((MOE_BLOCK, d), x_map(k)) for k in range(MOE_SUB)] + [
            pl.BlockSpec((None, d, th), lambda s, t, e, b0, nb, so: (e[s], 0, hid_idx(s, t, nb))),
            pl.BlockSpec((None, d, th), lambda s, t, e, b0, nb, so: (e[s], 0, hid_idx(s, t, nb))),
            pl.BlockSpec((None, hid, tn), lambda s, t, e, b0, nb, so: (e[s], 0, col_idx(s, t, nb)))],
        out_specs=pl.BlockSpec((rsup, tn), lambda s, t, e, b0, nb, so: (so[s], jnp.clip(t - nh, 0, nt - 1))),
        scratch_shapes=[pltpu.VMEM((MOE_SUB, nh, MOE_BLOCK, th), BF16), pltpu.VMEM((hid, tn), BF16)])
    return pl.pallas_call(
        functools.partial(_moe_kernel, nsub=MOE_SUB, nh=nh), grid_spec=gs,
        out_shape=jax.ShapeDtypeStruct((ns * rsup, d), BF16),
        compiler_params=_cp(("arbitrary", "arbitrary"), 60), name="moe_experts")(
            sup_e, sup_b0, sup_nb, sup_out, *([xs] * MOE_SUB), w_gate, w_up, w_down)


def _final_kernel(x_ref, ya_ref, yb_ref, w_ref, g2_ref, fg_ref, o_ref):
    wts = w_ref[...]
    moe = ya_ref[...].astype(F32) * wts[:, 0:1] + yb_ref[...].astype(F32) * wts[:, 1:2]
    x = x_ref[...] + g2_ref[...] * moe
    ms = jnp.mean(x * x, axis=-1, keepdims=True)
    o_ref[...] = x * lax.rsqrt(ms + EPS) * fg_ref[...]


def _final_call(x, ya, yb, wts, mods, ig, fg):
    b, l, d = x.shape
    tr = min(256, l)
    nt = l // tr
    row = lambda i, r: (i * nt + r, 0)
    return pl.pallas_call(
        _final_kernel, grid=(b, nt),
        in_specs=[pl.BlockSpec((None, tr, d), lambda i, r: (i, r, 0)),
                  pl.BlockSpec((tr, d), row), pl.BlockSpec((tr, d), row),
                  pl.BlockSpec((tr, MOE_TOP_K), row),
                  pl.BlockSpec((None, None, 1, d), lambda i, r: (i, ig, 0, 0)),
                  pl.BlockSpec((1, d), lambda i, r: (0, 0))],
        out_specs=pl.BlockSpec((None, tr, d), lambda i, r: (i, r, 0)),
        out_shape=jax.ShapeDtypeStruct((b, l, d), F32),
        compiler_params=_cp(("parallel", "parallel"), 40), name="combine_final")(x, ya, yb, wts, mods, fg)


def _route(logits, b_group, b_expert):
    m = logits.shape[0]
    ngr = b_group.shape[0]
    ne = b_expert.shape[0]
    epg = ne // ngr
    g_prob = jax.nn.softmax(logits[:, :ngr] + b_group.astype(F32), axis=-1)
    g_idx = jnp.argmax(g_prob, axis=-1).astype(jnp.int32)[:, None]
    g_p = jnp.take_along_axis(g_prob, g_idx, axis=1)
    e_logits = (logits[:, ngr:ngr + ne] + b_expert.astype(F32)).reshape(m, ngr, epg)
    e_logits = jnp.take_along_axis(e_logits, g_idx[:, :, None], axis=1)[:, 0]
    lane = lax.broadcasted_iota(jnp.int32, e_logits.shape, 1)
    i1 = jnp.argmax(e_logits, axis=-1).astype(jnp.int32)[:, None]
    rest = jnp.where(lane == i1, -jnp.inf, e_logits)
    i2 = jnp.argmax(rest, axis=-1).astype(jnp.int32)[:, None]
    e_idx = jnp.concatenate([i1, i2], axis=1)
    e_top = jnp.take_along_axis(e_logits, e_idx, axis=1)
    weights = g_p * jax.nn.softmax(e_top, axis=-1)
    experts = g_idx * epg + e_idx
    n_assign = m * MOE_TOP_K
    flat_e = experts.reshape(-1)
    onehot = (flat_e[:, None] == jnp.arange(ne, dtype=flat_e.dtype)[None, :]).astype(jnp.int32)
    csum = jnp.cumsum(onehot, axis=0)
    counts = csum[-1]
    local = jnp.take_along_axis(csum, flat_e[:, None], axis=1)[:, 0] - 1
    nblk_e = (counts + MOE_BLOCK - 1) // MOE_BLOCK
    padded = nblk_e * MOE_BLOCK
    pad_end = jnp.cumsum(padded)
    pad_start = pad_end - padded
    dest = pad_start[flat_e] + local
    n_blocks = -(-(n_assign + ne * (MOE_BLOCK - 1)) // MOE_BLOCK)
    tok = jnp.arange(n_assign, dtype=jnp.int32) // MOE_TOP_K
    n_slots = n_blocks * MOE_BLOCK
    slot_tok = (jnp.arange(n_slots, dtype=jnp.int32) % m).at[dest].set(tok)
    rsup = MOE_SUB * MOE_BLOCK
    ns_e = (nblk_e + MOE_SUB - 1) // MOE_SUB
    sup_end = jnp.cumsum(ns_e)
    sup_start = sup_end - ns_e
    n_sup = (n_blocks + (MOE_SUB - 1) * ne) // MOE_SUB
    sidx = jnp.arange(n_sup, dtype=jnp.int32)
    last = sup_end[-1] - 1
    s_eff = jnp.minimum(sidx, last)
    e_s = jnp.minimum(jnp.searchsorted(sup_end, s_eff, side='right'), ne - 1).astype(jnp.int32)
    k_s = s_eff - sup_start[e_s]
    b0_s = pad_start[e_s] // MOE_BLOCK + MOE_SUB * k_s
    nb_s = jnp.clip(nblk_e[e_s] - MOE_SUB * k_s, 0, MOE_SUB)
    used = sidx <= last
    sup_b0 = jnp.where(used, b0_s, b0_s + nb_s - 1).astype(jnp.int32)
    sup_nb = jnp.where(used, nb_s, 0).astype(jnp.int32)
    pos = ((sup_start[flat_e] + local // rsup) * rsup + local % rsup).astype(jnp.int32).reshape(m, MOE_TOP_K)
    return weights, slot_tok, e_s, sup_b0, sup_nb, sidx, pos


def kernel(x, c, ctx, c_ctx, w_mod, b_mod, norm1_g, w_in, s5_lam_re, s5_lam_im, s5_log_dt, s5_b_re, s5_b_im, s5_c_re, s5_c_im, s5_d, s5_w_val, s5_w_gate, ssd_conv_w, ssd_conv_b, ssd_a_log, ssd_dt_bias, ssd_d, ssd_norm_g, ssd_w_out, w_o, norm2_g, moe_w_group, moe_b_group, moe_w_expert, moe_b_expert, moe_w_gate, moe_w_up, moe_w_down, final_g):
    depth = w_mod.shape[0]
    assert depth == 1, "single-layer block"
    bsz, n_lat, d = x.shape
    l_ctx = ctx.shape[1]
    w5 = s5_d.shape[1]
    nh = ssd_d.shape[1]
    w = nh * SSD_HEAD_DIM
    conv_dim = ssd_conv_w.shape[2]
    ng = (conv_dim - w) // (2 * SSD_STATE)
    ssd_in = w + conv_dim + 2 * nh
    o1, o2 = w5, w5 + ssd_in
    l = 0

    cc = jnp.concatenate([c, c_ctx[None, :]], axis=0)
    cc = jnp.pad(cc, ((0, (-cc.shape[0]) % 8), (0, 0)))
    mods = _mod_call(cc, w_mod[l], b_mod[l]).reshape(cc.shape[0], 6, 1, d)
    i_sh1, i_sc1, i_g1, i_sh2, i_sc2, i_g2 = range(6)

    w_in_l = w_in[l]
    w_gates = w_in_l[:, o2:].astype(BF16)
    o_dt = o1 + w + conv_dim

    hn_rm, hn_cm = _norm_lat_call(x, norm1_g[l], mods, i_sc1, i_sh1)
    hc = _norm_ctx_call(ctx, norm1_g[l], mods, bsz, i_sc1, i_sh1)
    hn_rm = hn_rm.reshape(bsz * n_lat, d)
    hn_cm = hn_cm.reshape(bsz * n_lat, d)
    hc = hc.reshape(bsz * l_ctx, d)

    u_lat = _mm_call(hn_rm, w_in_l, BF16, "in_s5", 0, o1).reshape(bsz, n_lat, w5)
    u_ctx = _mm_call(hc, w_in_l, BF16, "in_s5_ctx", 0, o1).reshape(bsz, l_ctx, w5)
    gates = _mm_call(hn_rm, w_gates, BF16, "in_gates")
    p_lat = _mm_call(hn_cm, w_in_l, BF16, "in_ssd", o1, w + conv_dim).reshape(bsz, n_lat, w + conv_dim)
    p_ctx = _mm_call(hc, w_in_l, BF16, "in_ssd_ctx", o1, w + conv_dim).reshape(bsz, l_ctx, w + conv_dim)
    dt_lat = _mm_call(hn_cm, w_in_l, F32, "in_dt", o_dt, 2 * nh).reshape(bsz, n_lat, LANE)
    dt_ctx = _mm_call(hc, w_in_l, F32, "in_dt_ctx", o_dt, 2 * nh).reshape(bsz, l_ctx, LANE)

    nj = w5 // LANE
    bmat, cmat, lre, lim = _s5_params(s5_lam_re[l], s5_lam_im[l], s5_log_dt[l], s5_b_re[l], s5_b_im[l],
                                      s5_c_re[l], s5_c_im[l], bsz)
    s5_zero = jnp.zeros((nj, 2, 2 * bsz, lre.shape[-1]), F32)
    (s5_ctx,) = _s5_call(u_ctx, bmat, cmat, lre, lim, s5_zero, False)
    ya_f, ya_b, _ = _s5_call(u_lat, bmat, cmat, lre, lim, s5_ctx, True)

    xbc_ctx = _conv_call(p_ctx, ssd_conv_w[l], ssd_conv_b[l], w)
    xbc_lat = _conv_call(p_lat, ssd_conv_w[l], ssd_conv_b[l], w)
    pad_h = LANE - 2 * nh
    bias = jnp.pad(ssd_dt_bias[l].astype(F32).reshape(1, 2 * nh), ((0, 0), (0, pad_h)))
    alog = jnp.pad(ssd_a_log[l].astype(F32).reshape(1, 2 * nh), ((0, 0), (0, pad_h)))
    h_zero = jnp.zeros((bsz, 2, SSD_STATE, w), F32)
    (h_ctx,) = _ssd_call(xbc_ctx, dt_ctx, h_zero, bias, alog, nh, ng, False)
    y_f, y_b, _ = _ssd_call(xbc_lat, dt_lat, h_ctx, bias, alog, nh, ng, True)
    d_vec = jnp.repeat(ssd_d[l].astype(F32), SSD_HEAD_DIM).reshape(1, w)
    y_ssd = _gnorm_call(y_f, y_b, xbc_lat, p_lat, d_vec, ssd_norm_g[l].astype(F32).reshape(1, w), ng)

    m_lat = bsz * n_lat
    part_a = _glu_call(u_lat.reshape(m_lat, w5), ya_f.reshape(m_lat, w5), ya_b.reshape(m_lat, w5),
                       s5_d[l].astype(F32).reshape(1, w5),
                       s5_w_val[l].astype(BF16), s5_w_gate[l].astype(BF16), gates)
    merged = _merge_call(y_ssd, ssd_w_out[l], gates, part_a)
    x1 = _resid_call(merged, w_o[l], x, mods, i_g1)

    ngr = moe_w_group.shape[-1]
    ne = moe_w_expert.shape[-1]
    wr = jnp.concatenate([moe_w_group[l], moe_w_expert[l]], axis=1).astype(F32)
    wr = jnp.pad(wr, ((0, 0), (0, (-(ngr + ne)) % LANE)))
    hx, logits = _norm_router_call(x1, norm2_g[l], mods, i_sc2, i_sh2, wr)
    m = bsz * n_lat
    hx = hx.reshape(m, d)
    weights, slot_tok, sup_e, sup_b0, sup_nb, sup_out, pos = _route(
        logits.reshape(m, -1), moe_b_group[l], moe_b_expert[l])
    xs = hx[slot_tok]
    y_slots = _moe_call(sup_e, sup_b0, sup_nb, sup_out, xs, moe_w_gate[l], moe_w_up[l], moe_w_down[l])
    ya = y_slots[pos[:, 0]]
    yb = y_slots[pos[:, 1]]
    return _final_call(x1, ya, yb, weights.astype(F32), mods, i_g2, final_g.reshape(1, d))
```

```python
import functools
import math

import jax
import jax.numpy as jnp
from jax import lax
from jax.experimental import pallas as pl
from jax.experimental.pallas import tpu as pltpu

F32 = jnp.float32
BF16 = jnp.bfloat16

GRID_W = 64
EPS = 1e-6
LANE = 128
SSD_HEAD_DIM = 64
SSD_STATE = 128
SSD_CHUNK = 128
SSD_CONV = 5
MOE_TOP_K = 2
MOE_BLOCK = 256
MOE_SUB = 4
XPOSE_TILE = 16
XPOSE_PAD = 8
S5_CHUNK = 128
S5_SLICE = 256
S5_ROW_PAD = 8


def _cp(sem, mb):
    return pltpu.CompilerParams(dimension_semantics=sem, vmem_limit_bytes=mb * 1024 * 1024)


def _sigmoid(x):
    return 1.0 / (1.0 + jnp.exp(-x))


def _silu(x):
    return x * _sigmoid(x)


def _mod_kernel(c_ref, w_ref, b_ref, o_ref):
    s = _silu(c_ref[...])
    o_ref[...] = jnp.dot(s.astype(BF16), w_ref[...].astype(BF16),
                         preferred_element_type=F32) + b_ref[...]


def _mod_call(cc, w, b):
    r, d = cc.shape
    n = w.shape[1]
    tn = min(512, n)
    return pl.pallas_call(
        _mod_kernel, grid=(n // tn,),
        in_specs=[pl.BlockSpec((r, d), lambda j: (0, 0)),
                  pl.BlockSpec((d, tn), lambda j: (0, j)),
                  pl.BlockSpec((1, tn), lambda j: (0, j))],
        out_specs=pl.BlockSpec((r, tn), lambda j: (0, j)),
        out_shape=jax.ShapeDtypeStruct((r, n), F32),
        compiler_params=_cp(("parallel",), 40), name="mod")(cc, w, b.reshape(1, n))


def _rmsmod(x, g, sc, sh):
    ms = jnp.mean(x * x, axis=-1, keepdims=True)
    return (x * lax.rsqrt(ms + EPS) * g) * (1.0 + sc) + sh


def _norm_lat_kernel(x_ref, g_ref, sc_ref, sh_ref, orm_ref, ocm_ref, t_ref, *, tb, d):
    pitch = tb + XPOSE_PAD
    for r in range(tb):
        y = _rmsmod(x_ref[r], g_ref[...], sc_ref[...], sh_ref[...])
        orm_ref[r] = y.astype(BF16)
        for kk in range(d // LANE):
            t_ref[kk, r * pitch:r * pitch + tb, :] = y[:, kk * LANE:(kk + 1) * LANE]
    for wl in range(tb):
        ocm_ref[wl] = jnp.concatenate(
            [t_ref[kk, pl.ds(wl, tb, stride=pitch), :] for kk in range(d // LANE)], axis=1).astype(BF16)


def _norm_lat_call(x, g, mods, isc, ish):
    b, l, d = x.shape
    rows = l // GRID_W
    tb = XPOSE_TILE
    xv = x.reshape(b, rows, GRID_W, d)
    orm, ocm = pl.pallas_call(
        functools.partial(_norm_lat_kernel, tb=tb, d=d), grid=(b, rows // tb, GRID_W // tb),
        in_specs=[pl.BlockSpec((None, tb, tb, d), lambda i, r, c: (i, r, c, 0)),
                  pl.BlockSpec((1, d), lambda i, r, c: (0, 0)),
                  pl.BlockSpec((None, None, 1, d), lambda i, r, c: (i, isc, 0, 0)),
                  pl.BlockSpec((None, None, 1, d), lambda i, r, c: (i, ish, 0, 0))],
        out_specs=[pl.BlockSpec((None, tb, tb, d), lambda i, r, c: (i, r, c, 0)),
                   pl.BlockSpec((None, tb, tb, d), lambda i, r, c: (i, c, r, 0))],
        out_shape=[jax.ShapeDtypeStruct((b, rows, GRID_W, d), BF16),
                   jax.ShapeDtypeStruct((b, GRID_W, rows, d), BF16)],
        scratch_shapes=[pltpu.VMEM((d // LANE, tb * (tb + XPOSE_PAD), LANE), F32)],
        compiler_params=_cp(("parallel", "parallel", "parallel"), 48),
        name="norm1_lat")(xv, g.reshape(1, d), mods, mods)
    return orm.reshape(b, l, d), ocm.reshape(b, l, d)


def _norm_kernel(x_ref, g_ref, sc_ref, sh_ref, o_ref):
    o_ref[...] = _rmsmod(x_ref[...], g_ref[...], sc_ref[...], sh_ref[...]).astype(o_ref.dtype)


def _norm_ctx_call(x, g, mods, row, isc, ish):
    b, l, d = x.shape
    tr = min(256, l)
    return pl.pallas_call(
        _norm_kernel, grid=(b, l // tr),
        in_specs=[pl.BlockSpec((None, tr, d), lambda i, r: (i, r, 0)),
                  pl.BlockSpec((1, d), lambda i, r: (0, 0)),
                  pl.BlockSpec((None, None, 1, d), lambda i, r: (row, isc, 0, 0)),
                  pl.BlockSpec((None, None, 1, d), lambda i, r: (row, ish, 0, 0))],
        out_specs=pl.BlockSpec((None, tr, d), lambda i, r: (i, r, 0)),
        out_shape=jax.ShapeDtypeStruct((b, l, d), BF16),
        compiler_params=_cp(("parallel", "parallel"), 40), name="norm1_ctx")(x, g.reshape(1, d), mods, mods)


def _norm_router_kernel(x_ref, g_ref, sc_ref, sh_ref, wr_ref, o_ref, lg_ref):
    y = _rmsmod(x_ref[...], g_ref[...], sc_ref[...], sh_ref[...])
    o_ref[...] = y.astype(BF16)
    lg_ref[...] = jnp.dot(y, wr_ref[...], precision=lax.Precision.HIGHEST, preferred_element_type=F32)


def _norm_router_call(x, g, mods, isc, ish, wr):
    b, l, d = x.shape
    tr = min(256, l)
    nr = wr.shape[1]
    return pl.pallas_call(
        _norm_router_kernel, grid=(b, l // tr),
        in_specs=[pl.BlockSpec((None, tr, d), lambda i, r: (i, r, 0)),
                  pl.BlockSpec((1, d), lambda i, r: (0, 0)),
                  pl.BlockSpec((None, None, 1, d), lambda i, r: (i, isc, 0, 0)),
                  pl.BlockSpec((None, None, 1, d), lambda i, r: (i, ish, 0, 0)),
                  pl.BlockSpec((d, nr), lambda i, r: (0, 0))],
        out_specs=[pl.BlockSpec((None, tr, d), lambda i, r: (i, r, 0)),
                   pl.BlockSpec((None, tr, nr), lambda i, r: (i, r, 0))],
        out_shape=[jax.ShapeDtypeStruct((b, l, d), BF16), jax.ShapeDtypeStruct((b, l, nr), F32)],
        compiler_params=_cp(("parallel", "parallel"), 40), name="norm2_router")(x, g.reshape(1, d), mods, mods, wr)


def _mm_kernel(a_ref, b_ref, o_ref):
    o_ref[...] = jnp.dot(a_ref[...], b_ref[...].astype(BF16), preferred_element_type=F32).astype(o_ref.dtype)


def _mm_tiles(m, n):
    tm = min(1024, m)
    tn = min(512, n)
    return tm, tn


def _mm_call(a, b, out_dtype, name, col0=0, n=None):
    m, k = a.shape
    n = b.shape[1] - col0 if n is None else n
    tm, tn = _mm_tiles(m, n)
    if col0 % LANE or n % tn or n % LANE:
        b = b[:, col0:col0 + n]
        pad = (-n) % LANE
        b = jnp.pad(b, ((0, 0), (0, pad)))
        n, col0 = n + pad, 0
        tm, tn = _mm_tiles(m, n)
    return pl.pallas_call(
        _mm_kernel, grid=(m // tm, n // tn),
        in_specs=[pl.BlockSpec((tm, k), lambda i, j: (i, 0)),
                  pl.BlockSpec((pl.Element(k), pl.Element(tn)), lambda i, j: (0, (col0 // LANE + j * (tn // LANE)) * LANE))],
        out_specs=pl.BlockSpec((tm, tn), lambda i, j: (i, j)),
        out_shape=jax.ShapeDtypeStruct((m, n), out_dtype),
        compiler_params=_cp(("parallel", "parallel"), 48), name=name)(a, b)


def _gelu_tanh(x):
    return x * (0.5 * (1.0 + jnp.tanh(math.sqrt(2.0 / math.pi) * (x + 0.044715 * (x * x * x)))))


def _glu_kernel(u_ref, yf_ref, yb_ref, d_ref, wv_ref, wg_ref, gate_ref, o_ref, a_ref):
    @pl.when(pl.program_id(1) == 0)
    def _():
        y = d_ref[...] * u_ref[...].astype(F32) + yf_ref[...].astype(F32) + yb_ref[...].astype(F32)
        a_ref[...] = _gelu_tanh(y).astype(BF16)

    a = a_ref[...]
    val = jnp.dot(a, wv_ref[...], preferred_element_type=F32)
    gl = jnp.dot(a, wg_ref[...], preferred_element_type=F32)
    o_ref[...] = (_sigmoid(gate_ref[...].astype(F32)) * (val * _sigmoid(gl))).astype(o_ref.dtype)


def _glu_call(u, yf, yb, dvec, wv, wg, gates):
    m, k = u.shape
    n = wv.shape[1]
    tm, tn = _mm_tiles(m, n)
    row = pl.BlockSpec((tm, k), lambda i, j: (i, 0))
    return pl.pallas_call(
        _glu_kernel, grid=(m // tm, n // tn),
        in_specs=[row, row, row,
                  pl.BlockSpec((1, k), lambda i, j: (0, 0)),
                  pl.BlockSpec((k, tn), lambda i, j: (0, j)),
                  pl.BlockSpec((k, tn), lambda i, j: (0, j)),
                  pl.BlockSpec((tm, tn), lambda i, j: (i, j))],
        out_specs=pl.BlockSpec((tm, tn), lambda i, j: (i, j)),
        out_shape=jax.ShapeDtypeStruct((m, n), BF16),
        scratch_shapes=[pltpu.VMEM((tm, k), BF16)],
        compiler_params=_cp(("parallel", "arbitrary"), 56), name="s5_glu")(u, yf, yb, dvec, wv, wg, gates)


def _merge_kernel(a_ref, w_ref, gate_ref, pa_ref, o_ref):
    br = jnp.dot(a_ref[...], w_ref[...].astype(BF16), preferred_element_type=F32)
    o_ref[...] = (pa_ref[...].astype(F32) + _sigmoid(gate_ref[...].astype(F32)) * br).astype(o_ref.dtype)


def _merge_call(a, w, gates, part_a):
    m, k = a.shape
    n = w.shape[1]
    tm, tn = _mm_tiles(m, n)
    off = n // tn
    return pl.pallas_call(
        _merge_kernel, grid=(m // tm, n // tn),
        in_specs=[pl.BlockSpec((tm, k), lambda i, j: (i, 0)),
                  pl.BlockSpec((k, tn), lambda i, j: (0, j)),
                  pl.BlockSpec((tm, tn), lambda i, j: (i, j + off)),
                  pl.BlockSpec((tm, tn), lambda i, j: (i, j))],
        out_specs=pl.BlockSpec((tm, tn), lambda i, j: (i, j)),
        out_shape=jax.ShapeDtypeStruct((m, n), BF16),
        compiler_params=_cp(("parallel", "parallel"), 48), name="ssd_out_merge")(a, w, gates, part_a)


def _resid_kernel(a_ref, w_ref, x_ref, g_ref, o_ref):
    mix = jnp.dot(a_ref[...], w_ref[...].astype(BF16), preferred_element_type=F32)
    o_ref[...] = x_ref[...] + g_ref[...] * mix


def _resid_call(a, w, x, mods, ig):
    b, l, d = x.shape
    k = a.shape[1]
    tm, tn = _mm_tiles(l, d)
    nt = l // tm
    return pl.pallas_call(
        _resid_kernel, grid=(b * nt, d // tn),
        in_specs=[pl.BlockSpec((tm, k), lambda i, j: (i, 0)),
                  pl.BlockSpec((k, tn), lambda i, j: (0, j)),
                  pl.BlockSpec((None, tm, tn), lambda i, j: (i // nt, i % nt, j)),
                  pl.BlockSpec((None, None, 1, tn), lambda i, j: (i // nt, ig, 0, j))],
        out_specs=pl.BlockSpec((None, tm, tn), lambda i, j: (i // nt, i % nt, j)),
        out_shape=jax.ShapeDtypeStruct((b, l, d), F32),
        compiler_params=_cp(("parallel", "parallel"), 48), name="w_o_resid")(a, w, x, mods)


def _s5_scan_steps(buf_ref, carry, ar, ai, lo, hi, *, q, nk, pitch, slice_fn):
    for i in range((hi - lo) // 8):
        slice_fn(i)
        for s8 in range(8):
            rows = pl.ds(lo + i * 8 + s8, q, stride=pitch)
            out = []
            for k in range(nk):
                hr, hi_ = carry[k]
                nr = ar[k] * hr - ai[k] * hi_ + buf_ref[k, rows, :]
                ni = ar[k] * hi_ + ai[k] * hr + buf_ref[nk + k, rows, :]
                buf_ref[k, rows, :] = nr
                buf_ref[nk + k, rows, :] = ni
                out.append((nr, ni))
            carry = tuple(out)
    return carry


def _s5_kernel(*refs, nb, t, p8, need_y, nc):
    if need_y:
        (uf0_ref, ub0_ref, ufn_ref, ubn_ref, b_ref, c_ref, lre_ref, lim_ref, h0_ref,
         yf_ref, yb_ref, h_ref, buf0, buf1, buf2, u_ref, yacc_ref) = refs
    else:
        (uf0_ref, ub0_ref, ufn_ref, ubn_ref, b_ref, c_ref, lre_ref, lim_ref, h0_ref,
         h_ref, buf0, buf1, buf2, u_ref) = refs
    c = pl.program_id(1)
    q = 2 * nb
    nk = p8 // LANE
    nsl = 2 * p8 // S5_SLICE
    spl = S5_SLICE // LANE
    pitch = t + S5_ROW_PAD
    bufs = (buf0, buf1, buf2)
    rev = (lax.broadcasted_iota(jnp.int32, (t, t), 0) + lax.broadcasted_iota(jnp.int32, (t, t), 1)
           == t - 1).astype(BF16)

    def stage_u(uf, ub):
        u_ref[0] = uf[...].reshape(nb * t, LANE)
        for b in range(nb):
            u_ref[1, b * t:(b + 1) * t, :] = jnp.dot(rev, ub[b], preferred_element_type=F32).astype(BF16)

    def bu_slice(dst, i):
        d = i // nsl
        sl = i - d * nsl
        bu = jnp.dot(u_ref[d], b_ref[d, sl], preferred_element_type=F32)
        for b in range(nb):
            r0 = (d * nb + b) * pitch
            for kk in range(spl):
                dst[sl * spl + kk, pl.ds(r0, t), :] = bu[b * t:(b + 1) * t, kk * LANE:(kk + 1) * LANE]

    def y_slice(src, i):
        d = i // nsl
        sl = i - d * nsl
        parts = []
        for b in range(nb):
            r0 = (d * nb + b) * pitch
            parts.append(jnp.concatenate(
                [src[sl * spl + kk, pl.ds(r0, t), :].astype(BF16) for kk in range(spl)], axis=1))
        h = jnp.concatenate(parts, axis=0)
        yacc_ref[d] += jnp.dot(h, c_ref[d, sl], preferred_element_type=F32)

    def y_store():
        yf_ref[...] = yacc_ref[0].reshape(nb, t, LANE).astype(BF16)
        for b in range(nb):
            yb_ref[b] = jnp.dot(rev, yacc_ref[1, b * t:(b + 1) * t, :].astype(BF16),
                                preferred_element_type=F32).astype(BF16)

    @pl.when(c == 0)
    def _():
        h_ref[...] = h0_ref[...]
        if need_y:
            buf2[...] = jnp.zeros(buf2.shape, F32)
        stage_u(uf0_ref, ub0_ref)
        for i in range(2 * nsl):
            bu_slice(buf0, i)

    ar = [lre_ref[:, k * LANE:(k + 1) * LANE] for k in range(nk)]
    ai = [lim_ref[:, k * LANE:(k + 1) * LANE] for k in range(nk)]
    half = t // 2
    assert half // 8 == 2 * nsl, "one matmul slice per recurrence-loop iteration"

    def run_chunk(cur, nxt, prv):
        scan = functools.partial(_s5_scan_steps, cur, ar=ar, ai=ai, q=q, nk=nk, pitch=pitch)
        stage_u(ufn_ref, ubn_ref)
        carry = tuple((h_ref[0, :, k * LANE:(k + 1) * LANE], h_ref[1, :, k * LANE:(k + 1) * LANE])
                      for k in range(nk))
        carry = scan(carry, lo=0, hi=half, slice_fn=functools.partial(bu_slice, nxt))
        if need_y:
            yacc_ref[...] = jnp.zeros(yacc_ref.shape, F32)
            carry = scan(carry, lo=half, hi=t, slice_fn=functools.partial(y_slice, prv))
            y_store()
        else:
            carry = scan(carry, lo=half, hi=t, slice_fn=lambda i: None)
        for k in range(nk):
            h_ref[0, :, k * LANE:(k + 1) * LANE] = carry[k][0]
            h_ref[1, :, k * LANE:(k + 1) * LANE] = carry[k][1]

    for r in range(3):
        roles = (bufs[r], bufs[(r + 1) % 3], bufs[(r + 2) % 3])
        pl.when(jnp.logical_and(c < nc, lax.rem(c, 3) == r))(functools.partial(run_chunk, *roles))
    if need_y:
        @pl.when(c == nc)
        def _():
            yacc_ref[...] = jnp.zeros(yacc_ref.shape, F32)
            for i in range(2 * nsl):
                y_slice(bufs[(nc + 2) % 3], i)
            y_store()


def _s5_call(u, bmat, cmat, lre, lim, h0, need_y):
    nb, l, w5 = u.shape
    nj = w5 // LANE
    q = 2 * nb
    t = S5_CHUNK
    p8 = lre.shape[-1]
    nc = l // t
    nsl = 2 * p8 // S5_SLICE
    nxt = lambda c: jnp.minimum(c + 1, nc - 1)
    prv = lambda c: jnp.maximum(c - 1, 0)
    hspec = pl.BlockSpec((None, 2, q, p8), lambda j, c: (j, 0, 0, 0))
    ublk = (nb, t, LANE)
    in_specs = [pl.BlockSpec(ublk, lambda j, c: (0, 0, j)), pl.BlockSpec(ublk, lambda j, c: (0, nc - 1, j)),
                pl.BlockSpec(ublk, lambda j, c: (0, nxt(c), j)),
                pl.BlockSpec(ublk, lambda j, c: (0, nc - 1 - nxt(c), j)),
                pl.BlockSpec((None, 2, nsl, LANE, S5_SLICE), lambda j, c: (j, 0, 0, 0, 0)),
                pl.BlockSpec((None, 2, nsl, S5_SLICE, LANE), lambda j, c: (j, 0, 0, 0, 0)),
                pl.BlockSpec((None, q, p8), lambda j, c: (j, 0, 0)),
                pl.BlockSpec((None, q, p8), lambda j, c: (j, 0, 0)),
                hspec]
    out_specs = [hspec]
    out_shape = [jax.ShapeDtypeStruct((nj, 2, q, p8), F32)]
    scratch = [pltpu.VMEM((2 * p8 // LANE, q * (t + S5_ROW_PAD), LANE), F32) for _ in range(3)]
    scratch = scratch + [pltpu.VMEM((2, nb * t, LANE), BF16)]
    if need_y:
        out_specs = [pl.BlockSpec(ublk, lambda j, c: (0, prv(c), j)),
                     pl.BlockSpec(ublk, lambda j, c: (0, nc - 1 - prv(c), j))] + out_specs
        out_shape = [jax.ShapeDtypeStruct((nb, l, w5), BF16)] * 2 + out_shape
        scratch = scratch + [pltpu.VMEM((2, nb * t, LANE), F32)]
    bsl = bmat.reshape(nj, 2, LANE, nsl, S5_SLICE).transpose(0, 1, 3, 2, 4)
    csl = cmat.reshape(nj, 2, nsl, S5_SLICE, LANE)
    kern = functools.partial(_s5_kernel, nb=nb, t=t, p8=p8, need_y=need_y, nc=nc)
    return pl.pallas_call(
        kern, grid=(nj, nc + 1 if need_y else nc), in_specs=in_specs, out_specs=out_specs, out_shape=out_shape,
        scratch_shapes=scratch, compiler_params=_cp(("parallel", "arbitrary"), 48),
        name="s5_scan" if need_y else "s5_scan_ctx")(u, u, u, u, bsl, csl, lre, lim, h0)


def _s5_params(lam_re, lam_im, log_dt, b_re, b_im, c_re, c_im, nb):
    _, g, p = lam_re.shape
    s = b_re.shape[-1]
    gpb = LANE // s
    nj = g // gpb
    lam = lax.complex(lam_re.astype(F32), lam_im.astype(F32))
    lam_bar = jnp.exp(lam * jnp.exp(log_dt.astype(F32))[..., None])
    b_bar = ((lam_bar - 1.0) / lam)[..., None] * lax.complex(b_re.astype(F32), b_im.astype(F32))
    eye = jnp.eye(gpb, dtype=F32)

    def bmat_of(bpart):
        bb = bpart.reshape(2, nj, gpb, p, s)
        m = jnp.einsum('dnkps,kl->dnkslp', bb, eye)
        return m.reshape(2, nj, gpb * s, gpb * p).transpose(1, 0, 2, 3)

    bmat = jnp.concatenate([bmat_of(b_bar.real), bmat_of(b_bar.imag)], axis=-1).astype(BF16)

    def cmat_of(cpart):
        cc = cpart.astype(F32).reshape(2, nj, gpb, s, p)
        m = jnp.einsum('dnksp,kl->dnkpls', cc, eye)
        return m.reshape(2, nj, gpb * p, gpb * s).transpose(1, 0, 2, 3)

    cmat = jnp.concatenate([cmat_of(c_re), -cmat_of(c_im)], axis=-2).astype(BF16)

    def lam_of(part):
        v = part.reshape(2, nj, gpb * p).transpose(1, 0, 2)
        return jnp.repeat(v, nb, axis=1)

    return bmat, cmat, lam_of(lam_bar.real), lam_of(lam_bar.imag)


def _conv_kernel(x_ref, w_ref, b_ref, o_ref, *, l):
    x = x_ref[...].astype(F32)
    rows = lax.broadcasted_iota(jnp.int32, x.shape, 0)
    half = SSD_CONV // 2
    acc = x * w_ref[half:half + 1, :] + b_ref[...]
    for k in range(SSD_CONV):
        if k == half:
            continue
        off = k - half
        xs = pltpu.roll(x, shift=(-off) % l, axis=0)
        valid = jnp.logical_and(rows + off >= 0, rows + off < l)
        acc = acc + jnp.where(valid, xs, 0.0) * w_ref[k:k + 1, :]
    o_ref[...] = _silu(acc).astype(o_ref.dtype)


def _conv_call(proj, w, bias, col0):
    b, l, _ = proj.shape
    c = w.shape[1]
    tc = 256
    off = col0 // tc
    return pl.pallas_call(
        functools.partial(_conv_kernel, l=l), grid=(b, c // tc),
        in_specs=[pl.BlockSpec((None, l, tc), lambda i, j: (i, 0, j + off)),
                  pl.BlockSpec((SSD_CONV, tc), lambda i, j: (0, j)),
                  pl.BlockSpec((1, tc), lambda i, j: (0, j))],
        out_specs=pl.BlockSpec((None, l, tc), lambda i, j: (i, 0, j)),
        out_shape=jax.ShapeDtypeStruct((b, l, c), BF16),
        compiler_params=_cp(("parallel", "parallel"), 40), name="ssd_conv")(proj, w, bias.reshape(1, c))


def _softplus(x):
    return jnp.maximum(x, 0.0) + jnp.log1p(jnp.exp(-jnp.abs(x)))


def _ssd_dir(xbc_ref, dt_ref, y_ref, h_ref, bias, a_neg, d, *, nh, ng, need_y):
    qn = SSD_CHUNK
    hd = SSD_HEAD_DIM
    w = nh * hd
    gw = w // ng
    gn = ng * SSD_STATE
    ii = lax.broadcasted_iota(jnp.int32, (qn, qn), 0)
    jj = lax.broadcasted_iota(jnp.int32, (qn, qn), 1)
    mask = (jj <= ii) if d == 0 else (jj >= ii)
    lmat = mask.astype(F32)
    dtv = _softplus(dt_ref[...] + bias)
    cum = jnp.dot(lmat, dtv * a_neg, precision=lax.Precision.HIGHEST, preferred_element_type=F32)
    cum_t = cum.T
    edge = qn - 1 if d == 0 else 0
    tot = cum[edge:edge + 1, :]
    dt_t = dtv.T
    wt_t = dt_t * jnp.exp(cum_t[:, edge:edge + 1] - cum_t)
    decay = jnp.exp(tot)
    lane = lax.broadcasted_iota(jnp.int32, (qn, LANE), 1)
    left = lane < hd
    zero = jnp.zeros((), BF16)
    for g in range(ng):
        bg = xbc_ref[:, w + g * SSD_STATE:w + (g + 1) * SSD_STATE]
        cg = xbc_ref[:, w + gn + g * SSD_STATE:w + gn + (g + 1) * SSD_STATE]
        bg_t = bg.astype(F32).T
        s_in = h_ref[d, :, g * gw:(g + 1) * gw]
        if need_y:
            cb = lax.dot_general(cg, bg, (((1,), (1,)), ((), ())), preferred_element_type=F32)
            yoff = jnp.dot(cg, s_in.astype(BF16), preferred_element_type=F32)
        for pr in range(gw // LANE):
            c0 = d * nh + (g * gw) // hd + 2 * pr
            col = g * gw + pr * LANE
            xp = xbc_ref[:, col:col + LANE]
            r = jnp.concatenate([jnp.where(left, xp, zero), jnp.where(left, zero, xp)], axis=0)
            tops, bots, cols = [], [], []
            for c in (c0, c0 + 1):
                bots.append((bg_t * wt_t[c:c + 1, :]).astype(BF16))
                if need_y:
                    ccol = jnp.broadcast_to(cum[:, c:c + 1], (qn, qn))
                    seg = jnp.where(mask, jnp.exp(ccol - cum_t[c:c + 1, :]), 0.0)
                    tops.append((cb * seg * dt_t[c:c + 1, :]).astype(BF16))
                    cols.append(ccol)
            dec = jnp.where(left[0:1], decay[:, c0:c0 + 1], decay[:, c0 + 1:c0 + 2])
            s_old = s_in[:, pr * LANE:(pr + 1) * LANE]
            if need_y:
                lhs = jnp.concatenate([jnp.concatenate(tops, axis=1), jnp.concatenate(bots, axis=1)], axis=0)
                out = jnp.dot(lhs, r, preferred_element_type=F32)
                ec = jnp.exp(jnp.where(left, cols[0], cols[1]))
                y_ref[:, col:col + LANE] = (out[:qn] + yoff[:, pr * LANE:(pr + 1) * LANE] * ec).astype(y_ref.dtype)
                s_new = out[qn:]
            else:
                s_new = jnp.dot(jnp.concatenate(bots, axis=1), r, preferred_element_type=F32)
            h_ref[d, :, col:col + LANE] = s_old * dec + s_new


def _ssd_kernel(*refs, nh, ng, need_y):
    if need_y:
        xf_ref, xb_ref, dtf_ref, dtb_ref, h0_ref, bias_ref, alog_ref, yf_ref, yb_ref, h_ref = refs
    else:
        xf_ref, xb_ref, dtf_ref, dtb_ref, h0_ref, bias_ref, alog_ref, h_ref = refs
        yf_ref = yb_ref = None

    @pl.when(pl.program_id(1) == 0)
    def _():
        h_ref[...] = h0_ref[...]

    bias = bias_ref[...]
    a_neg = -jnp.exp(alog_ref[...])
    _ssd_dir(xf_ref, dtf_ref, yf_ref, h_ref, bias, a_neg, 0, nh=nh, ng=ng, need_y=need_y)
    _ssd_dir(xb_ref, dtb_ref, yb_ref, h_ref, bias, a_neg, 1, nh=nh, ng=ng, need_y=need_y)


def _ssd_call(xbc, dt, h0, bias, alog, nh, ng, need_y):
    b, l, cd = xbc.shape
    w = nh * SSD_HEAD_DIM
    nc = l // SSD_CHUNK
    q = SSD_CHUNK
    fwd = lambda i, s: (i, s, 0)
    bwd = lambda i, s: (i, nc - 1 - s, 0)
    hspec = pl.BlockSpec((None, 2, SSD_STATE, w), lambda i, s: (i, 0, 0, 0))
    in_specs = [pl.BlockSpec((None, q, cd), fwd), pl.BlockSpec((None, q, cd), bwd),
                pl.BlockSpec((None, q, LANE), fwd), pl.BlockSpec((None, q, LANE), bwd),
                hspec,
                pl.BlockSpec((1, LANE), lambda i, s: (0, 0)), pl.BlockSpec((1, LANE), lambda i, s: (0, 0))]
    out_specs = [hspec]
    out_shape = [jax.ShapeDtypeStruct((b, 2, SSD_STATE, w), F32)]
    if need_y:
        out_specs = [pl.BlockSpec((None, q, w), fwd), pl.BlockSpec((None, q, w), bwd)] + out_specs
        out_shape = [jax.ShapeDtypeStruct((b, l, w), BF16)] * 2 + out_shape
    return pl.pallas_call(
        functools.partial(_ssd_kernel, nh=nh, ng=ng, need_y=need_y), grid=(b, nc),
        in_specs=in_specs, out_specs=out_specs, out_shape=out_shape,
        compiler_params=_cp(("parallel", "arbitrary"), 48),
        name="ssd_scan" if need_y else "ssd_scan_ctx")(xbc, xbc, dt, dt, h0, bias, alog)


def _gnorm_kernel(yf_ref, yb_ref, xs_ref, z_ref, d_ref, g_ref, o_ref, t_ref, *, rows, tb):
    y = d_ref[...] * xs_ref[...].astype(F32) + yf_ref[...].astype(F32) + yb_ref[...].astype(F32)
    y = y * _silu(z_ref[...].astype(F32))
    ms = jnp.mean(y * y, axis=-1, keepdims=True)
    res = y * lax.rsqrt(ms + EPS) * g_ref[...]
    gw = res.shape[1]
    pitch = rows + XPOSE_PAD
    for wl in range(tb):
        for kk in range(gw // LANE):
            t_ref[kk, wl * pitch:wl * pitch + rows, :] = res[wl * rows:(wl + 1) * rows, kk * LANE:(kk + 1) * LANE]
    for r in range(rows):
        o_ref[r] = jnp.concatenate(
            [t_ref[kk, pl.ds(r, tb, stride=pitch), :] for kk in range(gw // LANE)], axis=1).astype(o_ref.dtype)


def _gnorm_call(yf, yb, xbc, proj, dvec, gvec, ng):
    b, l, w = yf.shape
    rows = l // GRID_W
    tb = XPOSE_TILE
    gw = w // ng
    blk = pl.BlockSpec((None, tb * rows, gw), lambda i, s, g: (i, s, g))
    vec = pl.BlockSpec((1, gw), lambda i, s, g: (0, g))
    out = pl.pallas_call(
        functools.partial(_gnorm_kernel, rows=rows, tb=tb), grid=(b, GRID_W // tb, ng),
        in_specs=[blk, blk, blk, blk, vec, vec],
        out_specs=pl.BlockSpec((None, rows, tb, gw), lambda i, s, g: (i, 0, s, g)),
        out_shape=jax.ShapeDtypeStruct((b, rows, GRID_W, w), BF16),
        scratch_shapes=[pltpu.VMEM((gw // LANE, tb * (rows + XPOSE_PAD), LANE), F32)],
        compiler_params=_cp(("parallel", "parallel", "parallel"), 40),
        name="ssd_gnorm")(yf, yb, xbc, proj, dvec, gvec)
    return out.reshape(b * l, w)


def _moe_kernel(e_ref, b0_ref, nb_ref, so_ref, *refs, nsub, nh):
    xs = refs[:nsub]
    wg_ref, wu_ref, wd_ref, o_ref, act_ref, wdb_ref = refs[nsub:]
    s = pl.program_id(0)
    t = pl.program_id(1)
    nb = nb_ref[s]

    @pl.when(jnp.logical_and(t < nh, nb > 0))
    def _():
        wg = wg_ref[...].astype(BF16)
        wu = wu_ref[...].astype(BF16)
        for k in range(nsub):
            @pl.when(k < nb)
            def _():
                x = xs[k][...]
                gt = jnp.dot(x, wg, preferred_element_type=F32)
                up = jnp.dot(x, wu, preferred_element_type=F32)
                act_ref[k, t] = (_silu(gt) * up).astype(BF16)

    @pl.when(jnp.logical_and(t >= nh, nb > 0))
    def _():
        wdb_ref[...] = wd_ref[...].astype(BF16)
        for k in range(nsub):
            rows = slice(k * MOE_BLOCK, (k + 1) * MOE_BLOCK)

            @pl.when(k < nb)
            def _():
                a = jnp.concatenate([act_ref[k, h] for h in range(nh)], axis=1)
                o_ref[rows, :] = jnp.dot(a, wdb_ref[...], preferred_element_type=F32).astype(o_ref.dtype)

            @pl.when(k >= nb)
            def _():
                o_ref[rows, :] = jnp.zeros((MOE_BLOCK, o_ref.shape[1]), o_ref.dtype)

    @pl.when(jnp.logical_and(t >= nh, nb == 0))
    def _():
        o_ref[...] = jnp.zeros(o_ref.shape, o_ref.dtype)


def _moe_call(sup_e, sup_b0, sup_nb, sup_out, xs, w_gate, w_up, w_down):
    n_slots, d = xs.shape
    hid = w_gate.shape[-1]
    th = min(256, hid)
    tn = min(1024, d)
    nh = hid // th
    nt = d // tn
    ns = sup_e.shape[0]
    rsup = MOE_SUB * MOE_BLOCK

    def x_map(k):
        def index(s, t, e, b0, nb, so):
            sx = jnp.where(t < nh, s, jnp.minimum(s + 1, ns - 1))
            return (b0[sx] + jnp.minimum(k, jnp.maximum(nb[sx] - 1, 0)), 0)
        return index

    def hid_idx(s, t, nb):
        return jnp.where(nb[s] > 0, jnp.minimum(t, nh - 1), nh - 1)

    def col_idx(s, t, nb):
        return jnp.where(nb[s] > 0, jnp.clip(t - nh, 0, nt - 1), nt - 1)

    gs = pltpu.PrefetchScalarGridSpec(
        num_scalar_prefetch=4, grid=(ns, nh + nt),
        in_specs=[pl.BlockSpec((MOE_BLOCK, d), x_map(k)) for k in range(MOE_SUB)] + [
            pl.BlockSpec((None, d, th), lambda s, t, e, b0, nb, so: (e[s], 0, hid_idx(s, t, nb))),
            pl.BlockSpec((None, d, th), lambda s, t, e, b0, nb, so: (e[s], 0, hid_idx(s, t, nb))),
            pl.BlockSpec((None, hid, tn), lambda s, t, e, b0, nb, so: (e[s], 0, col_idx(s, t, nb)))],
        out_specs=pl.BlockSpec((rsup, tn), lambda s, t, e, b0, nb, so: (so[s], jnp.clip(t - nh, 0, nt - 1))),
        scratch_shapes=[pltpu.VMEM((MOE_SUB, nh, MOE_BLOCK, th), BF16), pltpu.VMEM((hid, tn), BF16)])
    return pl.pallas_call(
        functools.partial(_moe_kernel, nsub=MOE_SUB, nh=nh), grid_spec=gs,
        out_shape=jax.ShapeDtypeStruct((ns * rsup, d), BF16),
        compiler_params=_cp(("arbitrary", "arbitrary"), 60), name="moe_experts")(
            sup_e, sup_b0, sup_nb, sup_out, *([xs] * MOE_SUB), w_gate, w_up, w_down)


def _final_kernel(x_ref, ya_ref, yb_ref, w_ref, g2_ref, fg_ref, o_ref):
    wts = w_ref[...]
    moe = ya_ref[...].astype(F32) * wts[:, 0:1] + yb_ref[...].astype(F32) * wts[:, 1:2]
    x = x_ref[...] + g2_ref[...] * moe
    ms = jnp.mean(x * x, axis=-1, keepdims=True)
    o_ref[...] = x * lax.rsqrt(ms + EPS) * fg_ref[...]


def _final_call(x, ya, yb, wts, mods, ig, fg):
    b, l, d = x.shape
    tr = min(256, l)
    nt = l // tr
    row = lambda i, r: (i * nt + r, 0)
    return pl.pallas_call(
        _final_kernel, grid=(b, nt),
        in_specs=[pl.BlockSpec((None, tr, d), lambda i, r: (i, r, 0)),
                  pl.BlockSpec((tr, d), row), pl.BlockSpec((tr, d), row),
                  pl.BlockSpec((tr, MOE_TOP_K), row),
                  pl.BlockSpec((None, None, 1, d), lambda i, r: (i, ig, 0, 0)),
                  pl.BlockSpec((1, d), lambda i, r: (0, 0))],
        out_specs=pl.BlockSpec((None, tr, d), lambda i, r: (i, r, 0)),
        out_shape=jax.ShapeDtypeStruct((b, l, d), F32),
        compiler_params=_cp(("parallel", "parallel"), 40), name="combine_final")(x, ya, yb, wts, mods, fg)


def _route(logits, b_group, b_expert):
    m = logits.shape[0]
    ngr = b_group.shape[0]
    ne = b_expert.shape[0]
    epg = ne // ngr
    g_prob = jax.nn.softmax(logits[:, :ngr] + b_group.astype(F32), axis=-1)
    g_idx = jnp.argmax(g_prob, axis=-1).astype(jnp.int32)[:, None]
    g_p = jnp.max(g_prob, axis=-1, keepdims=True)
    lane = lax.broadcasted_iota(jnp.int32, (m, ne), 1)
    cand = jnp.where(lane // epg == g_idx, logits[:, ngr:ngr + ne] + b_expert.astype(F32), -jnp.inf)
    i1 = jnp.argmax(cand, axis=-1).astype(jnp.int32)[:, None]
    v1 = jnp.max(cand, axis=-1, keepdims=True)
    rest = jnp.where(lane == i1, -jnp.inf, cand)
    i2 = jnp.argmax(rest, axis=-1).astype(jnp.int32)[:, None]
    v2 = jnp.max(rest, axis=-1, keepdims=True)
    ex = jnp.exp(v2 - v1)
    weights = g_p * jnp.concatenate([1.0 / (1.0 + ex), ex / (1.0 + ex)], axis=1)
    n_assign = m * MOE_TOP_K
    flat_e = jnp.concatenate([i1[:, 0], i2[:, 0]], axis=0)
    onehot = (flat_e[:, None] == jnp.arange(ne, dtype=flat_e.dtype)[None, :]).astype(jnp.int32)
    csum = jnp.cumsum(onehot, axis=0)
    counts = csum[-1]
    local = jnp.sum(csum * onehot, axis=1) - 1
    nblk_e = (counts + MOE_BLOCK - 1) // MOE_BLOCK
    padded = nblk_e * MOE_BLOCK
    pad_end = jnp.cumsum(padded)
    pad_start = pad_end - padded
    dest = pad_start[flat_e] + local
    n_blocks = -(-(n_assign + ne * (MOE_BLOCK - 1)) // MOE_BLOCK)
    tok = jnp.arange(n_assign, dtype=jnp.int32) % m
    n_slots = n_blocks * MOE_BLOCK
    slot_tok = (jnp.arange(n_slots, dtype=jnp.int32) % m).at[dest].set(tok)
    rsup = MOE_SUB * MOE_BLOCK
    ns_e = (nblk_e + MOE_SUB - 1) // MOE_SUB
    sup_end = jnp.cumsum(ns_e)
    sup_start = sup_end - ns_e
    n_sup = (n_blocks + (MOE_SUB - 1) * ne) // MOE_SUB
    sidx = jnp.arange(n_sup, dtype=jnp.int32)
    last = sup_end[-1] - 1
    s_eff = jnp.minimum(sidx, last)
    e_s = jnp.minimum(jnp.searchsorted(sup_end, s_eff, side='right'), ne - 1).astype(jnp.int32)
    k_s = s_eff - sup_start[e_s]
    b0_s = pad_start[e_s] // MOE_BLOCK + MOE_SUB * k_s
    nb_s = jnp.clip(nblk_e[e_s] - MOE_SUB * k_s, 0, MOE_SUB)
    used = sidx <= last
    sup_b0 = jnp.where(used, b0_s, b0_s + nb_s - 1).astype(jnp.int32)
    sup_nb = jnp.where(used, nb_s, 0).astype(jnp.int32)
    pos = ((sup_start[flat_e] + local // rsup) * rsup + local % rsup).astype(jnp.int32)
    return weights, slot_tok, e_s, sup_b0, sup_nb, sidx, pos


def kernel(x, c, ctx, c_ctx, w_mod, b_mod, norm1_g, w_in, s5_lam_re, s5_lam_im, s5_log_dt, s5_b_re, s5_b_im, s5_c_re, s5_c_im, s5_d, s5_w_val, s5_w_gate, ssd_conv_w, ssd_conv_b, ssd_a_log, ssd_dt_bias, ssd_d, ssd_norm_g, ssd_w_out, w_o, norm2_g, moe_w_group, moe_b_group, moe_w_expert, moe_b_expert, moe_w_gate, moe_w_up, moe_w_down, final_g):
    depth = w_mod.shape[0]
    assert depth == 1, "single-layer block"
    bsz, n_lat, d = x.shape
    l_ctx = ctx.shape[1]
    w5 = s5_d.shape[1]
    nh = ssd_d.shape[1]
    w = nh * SSD_HEAD_DIM
    conv_dim = ssd_conv_w.shape[2]
    ng = (conv_dim - w) // (2 * SSD_STATE)
    ssd_in = w + conv_dim + 2 * nh
    o1, o2 = w5, w5 + ssd_in
    l = 0

    cc = jnp.concatenate([c, c_ctx[None, :]], axis=0)
    cc = jnp.pad(cc, ((0, (-cc.shape[0]) % 8), (0, 0)))
    mods = _mod_call(cc, w_mod[l], b_mod[l]).reshape(cc.shape[0], 6, 1, d)
    i_sh1, i_sc1, i_g1, i_sh2, i_sc2, i_g2 = range(6)

    w_in_l = w_in[l]
    o_dt = o1 + w + conv_dim

    hn_rm, hn_cm = _norm_lat_call(x, norm1_g[l], mods, i_sc1, i_sh1)
    hc = _norm_ctx_call(ctx, norm1_g[l], mods, bsz, i_sc1, i_sh1)
    hn_rm = hn_rm.reshape(bsz * n_lat, d)
    hn_cm = hn_cm.reshape(bsz * n_lat, d)
    hc = hc.reshape(bsz * l_ctx, d)

    u_lat = _mm_call(hn_rm, w_in_l, BF16, "in_s5", 0, o1).reshape(bsz, n_lat, w5)
    u_ctx = _mm_call(hc, w_in_l, BF16, "in_s5_ctx", 0, o1).reshape(bsz, l_ctx, w5)
    gates = _mm_call(hn_rm, w_in_l, BF16, "in_gates", o2, 2 * d)
    p_lat = _mm_call(hn_cm, w_in_l, BF16, "in_ssd", o1, w + conv_dim).reshape(bsz, n_lat, w + conv_dim)
    p_ctx = _mm_call(hc, w_in_l, BF16, "in_ssd_ctx", o1, w + conv_dim).reshape(bsz, l_ctx, w + conv_dim)
    dt_lat = _mm_call(hn_cm, w_in_l, F32, "in_dt", o_dt, 2 * nh).reshape(bsz, n_lat, LANE)
    dt_ctx = _mm_call(hc, w_in_l, F32, "in_dt_ctx", o_dt, 2 * nh).reshape(bsz, l_ctx, LANE)

    nj = w5 // LANE
    bmat, cmat, lre, lim = _s5_params(s5_lam_re[l], s5_lam_im[l], s5_log_dt[l], s5_b_re[l], s5_b_im[l],
                                      s5_c_re[l], s5_c_im[l], bsz)
    s5_zero = jnp.zeros((nj, 2, 2 * bsz, lre.shape[-1]), F32)
    (s5_ctx,) = _s5_call(u_ctx, bmat, cmat, lre, lim, s5_zero, False)
    ya_f, ya_b, _ = _s5_call(u_lat, bmat, cmat, lre, lim, s5_ctx, True)

    xbc_ctx = _conv_call(p_ctx, ssd_conv_w[l], ssd_conv_b[l], w)
    xbc_lat = _conv_call(p_lat, ssd_conv_w[l], ssd_conv_b[l], w)
    pad_h = LANE - 2 * nh
    bias = jnp.pad(ssd_dt_bias[l].astype(F32).reshape(1, 2 * nh), ((0, 0), (0, pad_h)))
    alog = jnp.pad(ssd_a_log[l].astype(F32).reshape(1, 2 * nh), ((0, 0), (0, pad_h)))
    h_zero = jnp.zeros((bsz, 2, SSD_STATE, w), F32)
    (h_ctx,) = _ssd_call(xbc_ctx, dt_ctx, h_zero, bias, alog, nh, ng, False)
    y_f, y_b, _ = _ssd_call(xbc_lat, dt_lat, h_ctx, bias, alog, nh, ng, True)
    d_vec = jnp.repeat(ssd_d[l].astype(F32), SSD_HEAD_DIM).reshape(1, w)
    y_ssd = _gnorm_call(y_f, y_b, xbc_lat, p_lat, d_vec, ssd_norm_g[l].astype(F32).reshape(1, w), ng)

    m_lat = bsz * n_lat
    part_a = _glu_call(u_lat.reshape(m_lat, w5), ya_f.reshape(m_lat, w5), ya_b.reshape(m_lat, w5),
                       s5_d[l].astype(F32).reshape(1, w5),
                       s5_w_val[l].astype(BF16), s5_w_gate[l].astype(BF16), gates)
    merged = _merge_call(y_ssd, ssd_w_out[l], gates, part_a)
    x1 = _resid_call(merged, w_o[l], x, mods, i_g1)

    ngr = moe_w_group.shape[-1]
    ne = moe_w_expert.shape[-1]
    wr = jnp.concatenate([moe_w_group[l], moe_w_expert[l]], axis=1).astype(F32)
    wr = jnp.pad(wr, ((0, 0), (0, (-(ngr + ne)) % LANE)))
    hx, logits = _norm_router_call(x1, norm2_g[l], mods, i_sc2, i_sh2, wr)
    m = bsz * n_lat
    hx = hx.reshape(m, d)
    weights, slot_tok, sup_e, sup_b0, sup_nb, sup_out, pos = _route(
        logits.reshape(m, -1), moe_b_group[l], moe_b_expert[l])
    xs = hx[slot_tok]
    y_slots = _moe_call(sup_e, sup_b0, sup_nb, sup_out, xs, moe_w_gate[l], moe_w_up[l], moe_w_down[l])
    ya = y_slots[pos[:m]]
    yb = y_slots[pos[m:]]
    return _final_call(x1, ya, yb, weights.astype(F32), mods, i_g2, final_g.reshape(1, d))
```

```python
import functools
import math

import jax
import jax.numpy as jnp
from jax import lax
from jax.experimental import pallas as pl
from jax.experimental.pallas import tpu as pltpu

F32 = jnp.float32
BF16 = jnp.bfloat16

GRID_W = 64
EPS = 1e-6
LANE = 128
SSD_HEAD_DIM = 64
SSD_STATE = 128
SSD_CHUNK = 128
SSD_CONV = 5
MOE_TOP_K = 2
MOE_BLOCK = 256
MOE_SUB = 4
XPOSE_TILE = 16
XPOSE_PAD = 8
S5_CHUNK = 128
S5_SLICE = 256
S5_ROW_PAD = 8


def _cp(sem, mb):
    return pltpu.CompilerParams(dimension_semantics=sem, vmem_limit_bytes=mb * 1024 * 1024)


def _sigmoid(x):
    return 1.0 / (1.0 + jnp.exp(-x))


def _silu(x):
    return x * _sigmoid(x)


def _mod_kernel(c_ref, w_ref, b_ref, o_ref):
    s = _silu(c_ref[...])
    o_ref[...] = jnp.dot(s.astype(BF16), w_ref[...].astype(BF16),
                         preferred_element_type=F32) + b_ref[...]


def _mod_call(cc, w, b):
    r, d = cc.shape
    n = w.shape[1]
    tn = min(512, n)
    return pl.pallas_call(
        _mod_kernel, grid=(n // tn,),
        in_specs=[pl.BlockSpec((r, d), lambda j: (0, 0)),
                  pl.BlockSpec((d, tn), lambda j: (0, j)),
                  pl.BlockSpec((1, tn), lambda j: (0, j))],
        out_specs=pl.BlockSpec((r, tn), lambda j: (0, j)),
        out_shape=jax.ShapeDtypeStruct((r, n), F32),
        compiler_params=_cp(("parallel",), 40), name="mod")(cc, w, b.reshape(1, n))


def _rmsmod(x, g, sc, sh):
    ms = jnp.mean(x * x, axis=-1, keepdims=True)
    return (x * lax.rsqrt(ms + EPS) * g) * (1.0 + sc) + sh


def _norm_lat_kernel(x_ref, g_ref, sc_ref, sh_ref, orm_ref, ocm_ref, t_ref, *, tb, d):
    pitch = tb + XPOSE_PAD
    for r in range(tb):
        y = _rmsmod(x_ref[r], g_ref[...], sc_ref[...], sh_ref[...])
        orm_ref[r] = y.astype(BF16)
        for kk in range(d // LANE):
            t_ref[kk, r * pitch:r * pitch + tb, :] = y[:, kk * LANE:(kk + 1) * LANE]
    for wl in range(tb):
        ocm_ref[wl] = jnp.concatenate(
            [t_ref[kk, pl.ds(wl, tb, stride=pitch), :] for kk in range(d // LANE)], axis=1).astype(BF16)


def _norm_lat_call(x, g, mods, isc, ish):
    b, l, d = x.shape
    rows = l // GRID_W
    tb = XPOSE_TILE
    xv = x.reshape(b, rows, GRID_W, d)
    orm, ocm = pl.pallas_call(
        functools.partial(_norm_lat_kernel, tb=tb, d=d), grid=(b, rows // tb, GRID_W // tb),
        in_specs=[pl.BlockSpec((None, tb, tb, d), lambda i, r, c: (i, r, c, 0)),
                  pl.BlockSpec((1, d), lambda i, r, c: (0, 0)),
                  pl.BlockSpec((None, None, 1, d), lambda i, r, c: (i, isc, 0, 0)),
                  pl.BlockSpec((None, None, 1, d), lambda i, r, c: (i, ish, 0, 0))],
        out_specs=[pl.BlockSpec((None, tb, tb, d), lambda i, r, c: (i, r, c, 0)),
                   pl.BlockSpec((None, tb, tb, d), lambda i, r, c: (i, c, r, 0))],
        out_shape=[jax.ShapeDtypeStruct((b, rows, GRID_W, d), BF16),
                   jax.ShapeDtypeStruct((b, GRID_W, rows, d), BF16)],
        scratch_shapes=[pltpu.VMEM((d // LANE, tb * (tb + XPOSE_PAD), LANE), F32)],
        compiler_params=_cp(("parallel", "parallel", "parallel"), 48),
        name="norm1_lat")(xv, g.reshape(1, d), mods, mods)
    return orm.reshape(b, l, d), ocm.reshape(b, l, d)


def _norm_kernel(x_ref, g_ref, sc_ref, sh_ref, o_ref):
    o_ref[...] = _rmsmod(x_ref[...], g_ref[...], sc_ref[...], sh_ref[...]).astype(o_ref.dtype)


def _norm_ctx_call(x, g, mods, row, isc, ish):
    b, l, d = x.shape
    tr = min(256, l)
    return pl.pallas_call(
        _norm_kernel, grid=(b, l // tr),
        in_specs=[pl.BlockSpec((None, tr, d), lambda i, r: (i, r, 0)),
                  pl.BlockSpec((1, d), lambda i, r: (0, 0)),
                  pl.BlockSpec((None, None, 1, d), lambda i, r: (row, isc, 0, 0)),
                  pl.BlockSpec((None, None, 1, d), lambda i, r: (row, ish, 0, 0))],
        out_specs=pl.BlockSpec((None, tr, d), lambda i, r: (i, r, 0)),
        out_shape=jax.ShapeDtypeStruct((b, l, d), BF16),
        compiler_params=_cp(("parallel", "parallel"), 40), name="norm1_ctx")(x, g.reshape(1, d), mods, mods)


def _first_argmax(v, vmax, lane):
    return jnp.min(jnp.where(v == vmax, lane, float(LANE)), axis=-1, keepdims=True)


def _norm_router_kernel(x_ref, g_ref, sc_ref, sh_ref, wr_ref, br_ref, o_ref, sel_ref, *, ngr, ne):
    y = _rmsmod(x_ref[...], g_ref[...], sc_ref[...], sh_ref[...])
    o_ref[...] = y.astype(BF16)
    lg = jnp.dot(y, wr_ref[...], precision=lax.Precision.HIGHEST, preferred_element_type=F32) + br_ref[...]
    epg = ne // ngr
    lane = lax.broadcasted_iota(jnp.int32, lg.shape, 1).astype(F32)
    ninf = -jnp.inf
    gl = jnp.where(lane < ngr, lg, ninf)
    ge = jnp.exp(gl - jnp.max(gl, axis=-1, keepdims=True))
    g_prob = ge / jnp.sum(ge, axis=-1, keepdims=True)
    g_p = jnp.max(g_prob, axis=-1, keepdims=True)
    lo = ngr + epg * _first_argmax(g_prob, g_p, lane)
    cand = jnp.where(jnp.logical_and(lane >= lo, lane < lo + epg), lg, ninf)
    v1 = jnp.max(cand, axis=-1, keepdims=True)
    l1 = _first_argmax(cand, v1, lane)
    rest = jnp.where(lane == l1, ninf, cand)
    v2 = jnp.max(rest, axis=-1, keepdims=True)
    l2 = _first_argmax(rest, v2, lane)
    ex = jnp.exp(v2 - v1)
    w1 = g_p * (1.0 / (1.0 + ex))
    w2 = g_p * (ex / (1.0 + ex))
    sel_ref[...] = jnp.where(lane == 0, l1 - ngr, jnp.where(lane == 1, l2 - ngr,
                             jnp.where(lane == 2, w1, jnp.where(lane == 3, w2, 0.0))))


def _norm_router_call(x, g, mods, isc, ish, wr, br, ngr, ne):
    b, l, d = x.shape
    tr = min(256, l)
    nr = wr.shape[1]
    return pl.pallas_call(
        functools.partial(_norm_router_kernel, ngr=ngr, ne=ne), grid=(b, l // tr),
        in_specs=[pl.BlockSpec((None, tr, d), lambda i, r: (i, r, 0)),
                  pl.BlockSpec((1, d), lambda i, r: (0, 0)),
                  pl.BlockSpec((None, None, 1, d), lambda i, r: (i, isc, 0, 0)),
                  pl.BlockSpec((None, None, 1, d), lambda i, r: (i, ish, 0, 0)),
                  pl.BlockSpec((d, nr), lambda i, r: (0, 0)),
                  pl.BlockSpec((1, nr), lambda i, r: (0, 0))],
        out_specs=[pl.BlockSpec((None, tr, d), lambda i, r: (i, r, 0)),
                   pl.BlockSpec((None, tr, nr), lambda i, r: (i, r, 0))],
        out_shape=[jax.ShapeDtypeStruct((b, l, d), BF16), jax.ShapeDtypeStruct((b, l, nr), F32)],
        compiler_params=_cp(("parallel", "parallel"), 40),
        name="norm2_router")(x, g.reshape(1, d), mods, mods, wr, br)


def _mm_kernel(a_ref, b_ref, o_ref):
    o_ref[...] = jnp.dot(a_ref[...], b_ref[...].astype(BF16), preferred_element_type=F32).astype(o_ref.dtype)


def _mm_tiles(m, n):
    tm = min(1024, m)
    tn = min(512, n)
    return tm, tn


def _mm_call(a, b, out_dtype, name, col0=0, n=None):
    m, k = a.shape
    n = b.shape[1] - col0 if n is None else n
    tm, tn = _mm_tiles(m, n)
    if col0 % LANE or n % tn or n % LANE:
        b = b[:, col0:col0 + n]
        pad = (-n) % LANE
        b = jnp.pad(b, ((0, 0), (0, pad)))
        n, col0 = n + pad, 0
        tm, tn = _mm_tiles(m, n)
    return pl.pallas_call(
        _mm_kernel, grid=(m // tm, n // tn),
        in_specs=[pl.BlockSpec((tm, k), lambda i, j: (i, 0)),
                  pl.BlockSpec((pl.Element(k), pl.Element(tn)), lambda i, j: (0, (col0 // LANE + j * (tn // LANE)) * LANE))],
        out_specs=pl.BlockSpec((tm, tn), lambda i, j: (i, j)),
        out_shape=jax.ShapeDtypeStruct((m, n), out_dtype),
        compiler_params=_cp(("parallel", "parallel"), 48), name=name)(a, b)


def _gelu_tanh(x):
    return x * (0.5 * (1.0 + jnp.tanh(math.sqrt(2.0 / math.pi) * (x + 0.044715 * (x * x * x)))))


def _glu_kernel(u_ref, yf_ref, yb_ref, d_ref, wv_ref, wg_ref, gate_ref, o_ref, a_ref):
    @pl.when(pl.program_id(1) == 0)
    def _():
        y = d_ref[...] * u_ref[...].astype(F32) + yf_ref[...].astype(F32) + yb_ref[...].astype(F32)
        a_ref[...] = _gelu_tanh(y).astype(BF16)

    a = a_ref[...]
    val = jnp.dot(a, wv_ref[...], preferred_element_type=F32)
    gl = jnp.dot(a, wg_ref[...], preferred_element_type=F32)
    o_ref[...] = (_sigmoid(gate_ref[...].astype(F32)) * (val * _sigmoid(gl))).astype(o_ref.dtype)


def _glu_call(u, yf, yb, dvec, wv, wg, gates):
    m, k = u.shape
    n = wv.shape[1]
    tm, tn = _mm_tiles(m, n)
    row = pl.BlockSpec((tm, k), lambda i, j: (i, 0))
    return pl.pallas_call(
        _glu_kernel, grid=(m // tm, n // tn),
        in_specs=[row, row, row,
                  pl.BlockSpec((1, k), lambda i, j: (0, 0)),
                  pl.BlockSpec((k, tn), lambda i, j: (0, j)),
                  pl.BlockSpec((k, tn), lambda i, j: (0, j)),
                  pl.BlockSpec((tm, tn), lambda i, j: (i, j))],
        out_specs=pl.BlockSpec((tm, tn), lambda i, j: (i, j)),
        out_shape=jax.ShapeDtypeStruct((m, n), BF16),
        scratch_shapes=[pltpu.VMEM((tm, k), BF16)],
        compiler_params=_cp(("parallel", "arbitrary"), 56), name="s5_glu")(u, yf, yb, dvec, wv, wg, gates)


def _merge_kernel(a_ref, w_ref, gate_ref, pa_ref, o_ref):
    br = jnp.dot(a_ref[...], w_ref[...].astype(BF16), preferred_element_type=F32)
    o_ref[...] = (pa_ref[...].astype(F32) + _sigmoid(gate_ref[...].astype(F32)) * br).astype(o_ref.dtype)


def _merge_call(a, w, gates, part_a):
    m, k = a.shape
    n = w.shape[1]
    tm, tn = _mm_tiles(m, n)
    off = n // tn
    return pl.pallas_call(
        _merge_kernel, grid=(m // tm, n // tn),
        in_specs=[pl.BlockSpec((tm, k), lambda i, j: (i, 0)),
                  pl.BlockSpec((k, tn), lambda i, j: (0, j)),
                  pl.BlockSpec((tm, tn), lambda i, j: (i, j + off)),
                  pl.BlockSpec((tm, tn), lambda i, j: (i, j))],
        out_specs=pl.BlockSpec((tm, tn), lambda i, j: (i, j)),
        out_shape=jax.ShapeDtypeStruct((m, n), BF16),
        compiler_params=_cp(("parallel", "parallel"), 48), name="ssd_out_merge")(a, w, gates, part_a)


def _resid_kernel(a_ref, w_ref, x_ref, g_ref, o_ref):
    mix = jnp.dot(a_ref[...], w_ref[...].astype(BF16), preferred_element_type=F32)
    o_ref[...] = x_ref[...] + g_ref[...] * mix


def _resid_call(a, w, x, mods, ig):
    b, l, d = x.shape
    k = a.shape[1]
    tm, tn = _mm_tiles(l, d)
    nt = l // tm
    return pl.pallas_call(
        _resid_kernel, grid=(b * nt, d // tn),
        in_specs=[pl.BlockSpec((tm, k), lambda i, j: (i, 0)),
                  pl.BlockSpec((k, tn), lambda i, j: (0, j)),
                  pl.BlockSpec((None, tm, tn), lambda i, j: (i // nt, i % nt, j)),
                  pl.BlockSpec((None, None, 1, tn), lambda i, j: (i // nt, ig, 0, j))],
        out_specs=pl.BlockSpec((None, tm, tn), lambda i, j: (i // nt, i % nt, j)),
        out_shape=jax.ShapeDtypeStruct((b, l, d), F32),
        compiler_params=_cp(("parallel", "parallel"), 48), name="w_o_resid")(a, w, x, mods)


def _s5_scan_steps(buf_ref, carry, ar, ai, lo, hi, *, q, nk, pitch, slice_fn):
    for i in range((hi - lo) // 8):
        slice_fn(i)
        for s8 in range(8):
            rows = pl.ds(lo + i * 8 + s8, q, stride=pitch)
            out = []
            for k in range(nk):
                hr, hi_ = carry[k]
                nr = ar[k] * hr - ai[k] * hi_ + buf_ref[k, rows, :]
                ni = ar[k] * hi_ + ai[k] * hr + buf_ref[nk + k, rows, :]
                buf_ref[k, rows, :] = nr
                buf_ref[nk + k, rows, :] = ni
                out.append((nr, ni))
            carry = tuple(out)
    return carry


def _s5_kernel(*refs, nb, t, p8, need_y, nc):
    if need_y:
        (uf0_ref, ub0_ref, ufn_ref, ubn_ref, b_ref, c_ref, lre_ref, lim_ref, h0_ref,
         yf_ref, yb_ref, h_ref, buf0, buf1, buf2, u_ref, yacc_ref) = refs
    else:
        (uf0_ref, ub0_ref, ufn_ref, ubn_ref, b_ref, c_ref, lre_ref, lim_ref, h0_ref,
         h_ref, buf0, buf1, buf2, u_ref) = refs
    c = pl.program_id(1)
    q = 2 * nb
    nk = p8 // LANE
    nsl = 2 * p8 // S5_SLICE
    spl = S5_SLICE // LANE
    pitch = t + S5_ROW_PAD
    bufs = (buf0, buf1, buf2)
    rev = (lax.broadcasted_iota(jnp.int32, (t, t), 0) + lax.broadcasted_iota(jnp.int32, (t, t), 1)
           == t - 1).astype(BF16)

    def stage_u(uf, ub):
        u_ref[0] = uf[...].reshape(nb * t, LANE)
        for b in range(nb):
            u_ref[1, b * t:(b + 1) * t, :] = jnp.dot(rev, ub[b], preferred_element_type=F32).astype(BF16)

    def bu_slice(dst, i):
        d = i // nsl
        sl = i - d * nsl
        bu = jnp.dot(u_ref[d], b_ref[d, sl], preferred_element_type=F32)
        for b in range(nb):
            r0 = (d * nb + b) * pitch
            for kk in range(spl):
                dst[sl * spl + kk, pl.ds(r0, t), :] = bu[b * t:(b + 1) * t, kk * LANE:(kk + 1) * LANE]

    def y_slice(src, i):
        d = i // nsl
        sl = i - d * nsl
        parts = []
        for b in range(nb):
            r0 = (d * nb + b) * pitch
            parts.append(jnp.concatenate(
                [src[sl * spl + kk, pl.ds(r0, t), :].astype(BF16) for kk in range(spl)], axis=1))
        h = jnp.concatenate(parts, axis=0)
        yacc_ref[d] += jnp.dot(h, c_ref[d, sl], preferred_element_type=F32)

    def y_store():
        yf_ref[...] = yacc_ref[0].reshape(nb, t, LANE).astype(BF16)
        for b in range(nb):
            yb_ref[b] = jnp.dot(rev, yacc_ref[1, b * t:(b + 1) * t, :].astype(BF16),
                                preferred_element_type=F32).astype(BF16)

    @pl.when(c == 0)
    def _():
        h_ref[...] = h0_ref[...]
        if need_y:
            buf2[...] = jnp.zeros(buf2.shape, F32)
        stage_u(uf0_ref, ub0_ref)
        for i in range(2 * nsl):
            bu_slice(buf0, i)

    ar = [lre_ref[:, k * LANE:(k + 1) * LANE] for k in range(nk)]
    ai = [lim_ref[:, k * LANE:(k + 1) * LANE] for k in range(nk)]
    half = t // 2
    assert half // 8 == 2 * nsl, "one matmul slice per recurrence-loop iteration"

    def run_chunk(cur, nxt, prv):
        scan = functools.partial(_s5_scan_steps, cur, ar=ar, ai=ai, q=q, nk=nk, pitch=pitch)
        stage_u(ufn_ref, ubn_ref)
        carry = tuple((h_ref[0, :, k * LANE:(k + 1) * LANE], h_ref[1, :, k * LANE:(k + 1) * LANE])
                      for k in range(nk))
        carry = scan(carry, lo=0, hi=half, slice_fn=functools.partial(bu_slice, nxt))
        if need_y:
            yacc_ref[...] = jnp.zeros(yacc_ref.shape, F32)
            carry = scan(carry, lo=half, hi=t, slice_fn=functools.partial(y_slice, prv))
            y_store()
        else:
            carry = scan(carry, lo=half, hi=t, slice_fn=lambda i: None)
        for k in range(nk):
            h_ref[0, :, k * LANE:(k + 1) * LANE] = carry[k][0]
            h_ref[1, :, k * LANE:(k + 1) * LANE] = carry[k][1]

    for r in range(3):
        roles = (bufs[r], bufs[(r + 1) % 3], bufs[(r + 2) % 3])
        pl.when(jnp.logical_and(c < nc, lax.rem(c, 3) == r))(functools.partial(run_chunk, *roles))
    if need_y:
        @pl.when(c == nc)
        def _():
            yacc_ref[...] = jnp.zeros(yacc_ref.shape, F32)
            for i in range(2 * nsl):
                y_slice(bufs[(nc + 2) % 3], i)
            y_store()


def _s5_call(u, bmat, cmat, lre, lim, h0, need_y):
    nb, l, w5 = u.shape
    nj = w5 // LANE
    q = 2 * nb
    t = S5_CHUNK
    p8 = lre.shape[-1]
    nc = l // t
    nsl = 2 * p8 // S5_SLICE
    nxt = lambda c: jnp.minimum(c + 1, nc - 1)
    prv = lambda c: jnp.maximum(c - 1, 0)
    hspec = pl.BlockSpec((None, 2, q, p8), lambda j, c: (j, 0, 0, 0))
    ublk = (nb, t, LANE)
    in_specs = [pl.BlockSpec(ublk, lambda j, c: (0, 0, j)), pl.BlockSpec(ublk, lambda j, c: (0, nc - 1, j)),
                pl.BlockSpec(ublk, lambda j, c: (0, nxt(c), j)),
                pl.BlockSpec(ublk, lambda j, c: (0, nc - 1 - nxt(c), j)),
                pl.BlockSpec((None, 2, nsl, LANE, S5_SLICE), lambda j, c: (j, 0, 0, 0, 0)),
                pl.BlockSpec((None, 2, nsl, S5_SLICE, LANE), lambda j, c: (j, 0, 0, 0, 0)),
                pl.BlockSpec((None, q, p8), lambda j, c: (j, 0, 0)),
                pl.BlockSpec((None, q, p8), lambda j, c: (j, 0, 0)),
                hspec]
    out_specs = [hspec]
    out_shape = [jax.ShapeDtypeStruct((nj, 2, q, p8), F32)]
    scratch = [pltpu.VMEM((2 * p8 // LANE, q * (t + S5_ROW_PAD), LANE), F32) for _ in range(3)]
    scratch = scratch + [pltpu.VMEM((2, nb * t, LANE), BF16)]
    if need_y:
        out_specs = [pl.BlockSpec(ublk, lambda j, c: (0, prv(c), j)),
                     pl.BlockSpec(ublk, lambda j, c: (0, nc - 1 - prv(c), j))] + out_specs
        out_shape = [jax.ShapeDtypeStruct((nb, l, w5), BF16)] * 2 + out_shape
        scratch = scratch + [pltpu.VMEM((2, nb * t, LANE), F32)]
    bsl = bmat.reshape(nj, 2, LANE, nsl, S5_SLICE).transpose(0, 1, 3, 2, 4)
    csl = cmat.reshape(nj, 2, nsl, S5_SLICE, LANE)
    kern = functools.partial(_s5_kernel, nb=nb, t=t, p8=p8, need_y=need_y, nc=nc)
    return pl.pallas_call(
        kern, grid=(nj, nc + 1 if need_y else nc), in_specs=in_specs, out_specs=out_specs, out_shape=out_shape,
        scratch_shapes=scratch, compiler_params=_cp(("parallel", "arbitrary"), 48),
        name="s5_scan" if need_y else "s5_scan_ctx")(u, u, u, u, bsl, csl, lre, lim, h0)


def _s5_params(lam_re, lam_im, log_dt, b_re, b_im, c_re, c_im, nb):
    _, g, p = lam_re.shape
    s = b_re.shape[-1]
    gpb = LANE // s
    nj = g // gpb
    lam = lax.complex(lam_re.astype(F32), lam_im.astype(F32))
    lam_bar = jnp.exp(lam * jnp.exp(log_dt.astype(F32))[..., None])
    b_bar = ((lam_bar - 1.0) / lam)[..., None] * lax.complex(b_re.astype(F32), b_im.astype(F32))
    eye = jnp.eye(gpb, dtype=F32)

    def bmat_of(bpart):
        bb = bpart.reshape(2, nj, gpb, p, s)
        m = jnp.einsum('dnkps,kl->dnkslp', bb, eye)
        return m.reshape(2, nj, gpb * s, gpb * p).transpose(1, 0, 2, 3)

    bmat = jnp.concatenate([bmat_of(b_bar.real), bmat_of(b_bar.imag)], axis=-1).astype(BF16)

    def cmat_of(cpart):
        cc = cpart.astype(F32).reshape(2, nj, gpb, s, p)
        m = jnp.einsum('dnksp,kl->dnkpls', cc, eye)
        return m.reshape(2, nj, gpb * p, gpb * s).transpose(1, 0, 2, 3)

    cmat = jnp.concatenate([cmat_of(c_re), -cmat_of(c_im)], axis=-2).astype(BF16)

    def lam_of(part):
        v = part.reshape(2, nj, gpb * p).transpose(1, 0, 2)
        return jnp.repeat(v, nb, axis=1)

    return bmat, cmat, lam_of(lam_bar.real), lam_of(lam_bar.imag)


def _conv_kernel(x_ref, w_ref, b_ref, o_ref, *, l):
    x = x_ref[...].astype(F32)
    rows = lax.broadcasted_iota(jnp.int32, x.shape, 0)
    half = SSD_CONV // 2
    acc = x * w_ref[half:half + 1, :] + b_ref[...]
    for k in range(SSD_CONV):
        if k == half:
            continue
        off = k - half
        xs = pltpu.roll(x, shift=(-off) % l, axis=0)
        valid = jnp.logical_and(rows + off >= 0, rows + off < l)
        acc = acc + jnp.where(valid, xs, 0.0) * w_ref[k:k + 1, :]
    o_ref[...] = _silu(acc).astype(o_ref.dtype)


def _conv_call(proj, w, bias, col0):
    b, l, _ = proj.shape
    c = w.shape[1]
    tc = 256
    off = col0 // tc
    return pl.pallas_call(
        functools.partial(_conv_kernel, l=l), grid=(b, c // tc),
        in_specs=[pl.BlockSpec((None, l, tc), lambda i, j: (i, 0, j + off)),
                  pl.BlockSpec((SSD_CONV, tc), lambda i, j: (0, j)),
                  pl.BlockSpec((1, tc), lambda i, j: (0, j))],
        out_specs=pl.BlockSpec((None, l, tc), lambda i, j: (i, 0, j)),
        out_shape=jax.ShapeDtypeStruct((b, l, c), BF16),
        compiler_params=_cp(("parallel", "parallel"), 40), name="ssd_conv")(proj, w, bias.reshape(1, c))


def _softplus(x):
    return jnp.maximum(x, 0.0) + jnp.log1p(jnp.exp(-jnp.abs(x)))


def _ssd_dir(xbc_ref, dt_ref, y_ref, h_ref, bias, a_neg, d, *, nh, ng, need_y):
    qn = SSD_CHUNK
    hd = SSD_HEAD_DIM
    w = nh * hd
    gw = w // ng
    gn = ng * SSD_STATE
    ii = lax.broadcasted_iota(jnp.int32, (qn, qn), 0)
    jj = lax.broadcasted_iota(jnp.int32, (qn, qn), 1)
    mask = (jj <= ii) if d == 0 else (jj >= ii)
    lmat = mask.astype(F32)
    dtv = _softplus(dt_ref[...] + bias)
    cum = jnp.dot(lmat, dtv * a_neg, precision=lax.Precision.HIGHEST, preferred_element_type=F32)
    cum_t = cum.T
    edge = qn - 1 if d == 0 else 0
    tot = cum[edge:edge + 1, :]
    dt_t = dtv.T
    wt_t = dt_t * jnp.exp(cum_t[:, edge:edge + 1] - cum_t)
    decay = jnp.exp(tot)
    lane = lax.broadcasted_iota(jnp.int32, (qn, LANE), 1)
    left = lane < hd
    zero = jnp.zeros((), BF16)
    for g in range(ng):
        bg = xbc_ref[:, w + g * SSD_STATE:w + (g + 1) * SSD_STATE]
        cg = xbc_ref[:, w + gn + g * SSD_STATE:w + gn + (g + 1) * SSD_STATE]
        bg_t = bg.astype(F32).T
        s_in = h_ref[d, :, g * gw:(g + 1) * gw]
        if need_y:
            cb = lax.dot_general(cg, bg, (((1,), (1,)), ((), ())), preferred_element_type=F32)
            yoff = jnp.dot(cg, s_in.astype(BF16), preferred_element_type=F32)
        for pr in range(gw // LANE):
            c0 = d * nh + (g * gw) // hd + 2 * pr
            col = g * gw + pr * LANE
            xp = xbc_ref[:, col:col + LANE]
            r = jnp.concatenate([jnp.where(left, xp, zero), jnp.where(left, zero, xp)], axis=0)
            tops, bots, cols = [], [], []
            for c in (c0, c0 + 1):
                bots.append((bg_t * wt_t[c:c + 1, :]).astype(BF16))
                if need_y:
                    ccol = jnp.broadcast_to(cum[:, c:c + 1], (qn, qn))
                    seg = jnp.where(mask, jnp.exp(ccol - cum_t[c:c + 1, :]), 0.0)
                    tops.append((cb * seg * dt_t[c:c + 1, :]).astype(BF16))
                    cols.append(ccol)
            dec = jnp.where(left[0:1], decay[:, c0:c0 + 1], decay[:, c0 + 1:c0 + 2])
            s_old = s_in[:, pr * LANE:(pr + 1) * LANE]
            if need_y:
                lhs = jnp.concatenate([jnp.concatenate(tops, axis=1), jnp.concatenate(bots, axis=1)], axis=0)
                out = jnp.dot(lhs, r, preferred_element_type=F32)
                ec = jnp.exp(jnp.where(left, cols[0], cols[1]))
                y_ref[:, col:col + LANE] = (out[:qn] + yoff[:, pr * LANE:(pr + 1) * LANE] * ec).astype(y_ref.dtype)
                s_new = out[qn:]
            else:
                s_new = jnp.dot(jnp.concatenate(bots, axis=1), r, preferred_element_type=F32)
            h_ref[d, :, col:col + LANE] = s_old * dec + s_new


def _ssd_kernel(*refs, nh, ng, need_y):
    if need_y:
        xf_ref, xb_ref, dtf_ref, dtb_ref, h0_ref, bias_ref, alog_ref, yf_ref, yb_ref, h_ref = refs
    else:
        xf_ref, xb_ref, dtf_ref, dtb_ref, h0_ref, bias_ref, alog_ref, h_ref = refs
        yf_ref = yb_ref = None

    @pl.when(pl.program_id(1) == 0)
    def _():
        h_ref[...] = h0_ref[...]

    bias = bias_ref[...]
    a_neg = -jnp.exp(alog_ref[...])
    _ssd_dir(xf_ref, dtf_ref, yf_ref, h_ref, bias, a_neg, 0, nh=nh, ng=ng, need_y=need_y)
    _ssd_dir(xb_ref, dtb_ref, yb_ref, h_ref, bias, a_neg, 1, nh=nh, ng=ng, need_y=need_y)


def _ssd_call(xbc, dt, h0, bias, alog, nh, ng, need_y):
    b, l, cd = xbc.shape
    w = nh * SSD_HEAD_DIM
    nc = l // SSD_CHUNK
    q = SSD_CHUNK
    fwd = lambda i, s: (i, s, 0)
    bwd = lambda i, s: (i, nc - 1 - s, 0)
    hspec = pl.BlockSpec((None, 2, SSD_STATE, w), lambda i, s: (i, 0, 0, 0))
    in_specs = [pl.BlockSpec((None, q, cd), fwd), pl.BlockSpec((None, q, cd), bwd),
                pl.BlockSpec((None, q, LANE), fwd), pl.BlockSpec((None, q, LANE), bwd),
                hspec,
                pl.BlockSpec((1, LANE), lambda i, s: (0, 0)), pl.BlockSpec((1, LANE), lambda i, s: (0, 0))]
    out_specs = [hspec]
    out_shape = [jax.ShapeDtypeStruct((b, 2, SSD_STATE, w), F32)]
    if need_y:
        out_specs = [pl.BlockSpec((None, q, w), fwd), pl.BlockSpec((None, q, w), bwd)] + out_specs
        out_shape = [jax.ShapeDtypeStruct((b, l, w), BF16)] * 2 + out_shape
    return pl.pallas_call(
        functools.partial(_ssd_kernel, nh=nh, ng=ng, need_y=need_y), grid=(b, nc),
        in_specs=in_specs, out_specs=out_specs, out_shape=out_shape,
        compiler_params=_cp(("parallel", "arbitrary"), 48),
        name="ssd_scan" if need_y else "ssd_scan_ctx")(xbc, xbc, dt, dt, h0, bias, alog)


def _gnorm_kernel(yf_ref, yb_ref, xs_ref, z_ref, d_ref, g_ref, o_ref, t_ref, *, rows, tb):
    y = d_ref[...] * xs_ref[...].astype(F32) + yf_ref[...].astype(F32) + yb_ref[...].astype(F32)
    y = y * _silu(z_ref[...].astype(F32))
    ms = jnp.mean(y * y, axis=-1, keepdims=True)
    res = y * lax.rsqrt(ms + EPS) * g_ref[...]
    gw = res.shape[1]
    pitch = rows + XPOSE_PAD
    for wl in range(tb):
        for kk in range(gw // LANE):
            t_ref[kk, wl * pitch:wl * pitch + rows, :] = res[wl * rows:(wl + 1) * rows, kk * LANE:(kk + 1) * LANE]
    for r in range(rows):
        o_ref[r] = jnp.concatenate(
            [t_ref[kk, pl.ds(r, tb, stride=pitch), :] for kk in range(gw // LANE)], axis=1).astype(o_ref.dtype)


def _gnorm_call(yf, yb, xbc, proj, dvec, gvec, ng):
    b, l, w = yf.shape
    rows = l // GRID_W
    tb = XPOSE_TILE
    gw = w // ng
    blk = pl.BlockSpec((None, tb * rows, gw), lambda i, s, g: (i, s, g))
    vec = pl.BlockSpec((1, gw), lambda i, s, g: (0, g))
    out = pl.pallas_call(
        functools.partial(_gnorm_kernel, rows=rows, tb=tb), grid=(b, GRID_W // tb, ng),
        in_specs=[blk, blk, blk, blk, vec, vec],
        out_specs=pl.BlockSpec((None, rows, tb, gw), lambda i, s, g: (i, 0, s, g)),
        out_shape=jax.ShapeDtypeStruct((b, rows, GRID_W, w), BF16),
        scratch_shapes=[pltpu.VMEM((gw // LANE, tb * (rows + XPOSE_PAD), LANE), F32)],
        compiler_params=_cp(("parallel", "parallel", "parallel"), 40),
        name="ssd_gnorm")(yf, yb, xbc, proj, dvec, gvec)
    return out.reshape(b * l, w)


def _moe_kernel(e_ref, b0_ref, nb_ref, so_ref, *refs, nsub, nh):
    xs = refs[:nsub]
    wg_ref, wu_ref, wd_ref, o_ref, act_ref, wdb_ref = refs[nsub:]
    s = pl.program_id(0)
    t = pl.program_id(1)
    nb = nb_ref[s]

    @pl.when(jnp.logical_and(t < nh, nb > 0))
    def _():
        wg = wg_ref[...].astype(BF16)
        wu = wu_ref[...].astype(BF16)
        for k in range(nsub):
            @pl.when(k < nb)
            def _():
                x = xs[k][...]
                gt = jnp.dot(x, wg, preferred_element_type=F32)
                up = jnp.dot(x, wu, preferred_element_type=F32)
                act_ref[k, t] = (_silu(gt) * up).astype(BF16)

    @pl.when(jnp.logical_and(t >= nh, nb > 0))
    def _():
        wdb_ref[...] = wd_ref[...].astype(BF16)
        for k in range(nsub):
            rows = slice(k * MOE_BLOCK, (k + 1) * MOE_BLOCK)

            @pl.when(k < nb)
            def _():
                a = jnp.concatenate([act_ref[k, h] for h in range(nh)], axis=1)
                o_ref[rows, :] = jnp.dot(a, wdb_ref[...], preferred_element_type=F32).astype(o_ref.dtype)

            @pl.when(k >= nb)
            def _():
                o_ref[rows, :] = jnp.zeros((MOE_BLOCK, o_ref.shape[1]), o_ref.dtype)

    @pl.when(jnp.logical_and(t >= nh, nb == 0))
    def _():
        o_ref[...] = jnp.zeros(o_ref.shape, o_ref.dtype)


def _moe_call(sup_e, sup_b0, sup_nb, sup_out, xs, w_gate, w_up, w_down):
    n_slots, d = xs.shape
    hid = w_gate.shape[-1]
    th = min(256, hid)
    tn = min(1024, d)
    nh = hid // th
    nt = d // tn
    ns = sup_e.shape[0]
    rsup = MOE_SUB * MOE_BLOCK

    def x_map(k):
        def index(s, t, e, b0, nb, so):
            sx = jnp.where(t < nh, s, jnp.minimum(s + 1, ns - 1))
            return (b0[sx] + jnp.minimum(k, jnp.maximum(nb[sx] - 1, 0)), 0)
        return index

    def hid_idx(s, t, nb):
        return jnp.where(nb[s] > 0, jnp.minimum(t, nh - 1), nh - 1)

    def col_idx(s, t, nb):
        return jnp.where(nb[s] > 0, jnp.clip(t - nh, 0, nt - 1), nt - 1)

    gs = pltpu.PrefetchScalarGridSpec(
        num_scalar_prefetch=4, grid=(ns, nh + nt),
        in_specs=[pl.BlockSpec((MOE_BLOCK, d), x_map(k)) for k in range(MOE_SUB)] + [
            pl.BlockSpec((None, d, th), lambda s, t, e, b0, nb, so: (e[s], 0, hid_idx(s, t, nb))),
            pl.BlockSpec((None, d, th), lambda s, t, e, b0, nb, so: (e[s], 0, hid_idx(s, t, nb))),
            pl.BlockSpec((None, hid, tn), lambda s, t, e, b0, nb, so: (e[s], 0, col_idx(s, t, nb)))],
        out_specs=pl.BlockSpec((rsup, tn), lambda s, t, e, b0, nb, so: (so[s], jnp.clip(t - nh, 0, nt - 1))),
        scratch_shapes=[pltpu.VMEM((MOE_SUB, nh, MOE_BLOCK, th), BF16), pltpu.VMEM((hid, tn), BF16)])
    return pl.pallas_call(
        functools.partial(_moe_kernel, nsub=MOE_SUB, nh=nh), grid_spec=gs,
        out_shape=jax.ShapeDtypeStruct((ns * rsup, d), BF16),
        compiler_params=_cp(("arbitrary", "arbitrary"), 60), name="moe_experts")(
            sup_e, sup_b0, sup_nb, sup_out, *([xs] * MOE_SUB), w_gate, w_up, w_down)


def _final_kernel(x_ref, ya_ref, yb_ref, w_ref, g2_ref, fg_ref, o_ref):
    wts = w_ref[...]
    moe = ya_ref[...].astype(F32) * wts[:, 0:1] + yb_ref[...].astype(F32) * wts[:, 1:2]
    x = x_ref[...] + g2_ref[...] * moe
    ms = jnp.mean(x * x, axis=-1, keepdims=True)
    o_ref[...] = x * lax.rsqrt(ms + EPS) * fg_ref[...]


def _final_call(x, ya, yb, wts, mods, ig, fg):
    b, l, d = x.shape
    tr = min(256, l)
    nt = l // tr
    row = lambda i, r: (i * nt + r, 0)
    return pl.pallas_call(
        _final_kernel, grid=(b, nt),
        in_specs=[pl.BlockSpec((None, tr, d), lambda i, r: (i, r, 0)),
                  pl.BlockSpec((tr, d), row), pl.BlockSpec((tr, d), row),
                  pl.BlockSpec((tr, MOE_TOP_K), row),
                  pl.BlockSpec((None, None, 1, d), lambda i, r: (i, ig, 0, 0)),
                  pl.BlockSpec((1, d), lambda i, r: (0, 0))],
        out_specs=pl.BlockSpec((None, tr, d), lambda i, r: (i, r, 0)),
        out_shape=jax.ShapeDtypeStruct((b, l, d), F32),
        compiler_params=_cp(("parallel", "parallel"), 40), name="combine_final")(x, ya, yb, wts, mods, fg)


def _route(sel, ne):
    m = sel.shape[0]
    weights = sel[:, MOE_TOP_K:2 * MOE_TOP_K]
    n_assign = m * MOE_TOP_K
    flat_e = jnp.concatenate([sel[:, k] for k in range(MOE_TOP_K)], axis=0).astype(jnp.int32)
    onehot = (flat_e[:, None] == jnp.arange(ne, dtype=flat_e.dtype)[None, :]).astype(jnp.int32)
    csum = jnp.cumsum(onehot, axis=0)
    counts = csum[-1]
    local = jnp.sum(csum * onehot, axis=1) - 1
    nblk_e = (counts + MOE_BLOCK - 1) // MOE_BLOCK
    padded = nblk_e * MOE_BLOCK
    pad_end = jnp.cumsum(padded)
    pad_start = pad_end - padded
    dest = pad_start[flat_e] + local
    n_blocks = -(-(n_assign + ne * (MOE_BLOCK - 1)) // MOE_BLOCK)
    tok = jnp.arange(n_assign, dtype=jnp.int32) % m
    n_slots = n_blocks * MOE_BLOCK
    slot_tok = (jnp.arange(n_slots, dtype=jnp.int32) % m).at[dest].set(tok)
    rsup = MOE_SUB * MOE_BLOCK
    ns_e = (nblk_e + MOE_SUB - 1) // MOE_SUB
    sup_end = jnp.cumsum(ns_e)
    sup_start = sup_end - ns_e
    n_sup = (n_blocks + (MOE_SUB - 1) * ne) // MOE_SUB
    sidx = jnp.arange(n_sup, dtype=jnp.int32)
    last = sup_end[-1] - 1
    s_eff = jnp.minimum(sidx, last)
    e_s = jnp.minimum(jnp.searchsorted(sup_end, s_eff, side='right'), ne - 1).astype(jnp.int32)
    k_s = s_eff - sup_start[e_s]
    b0_s = pad_start[e_s] // MOE_BLOCK + MOE_SUB * k_s
    nb_s = jnp.clip(nblk_e[e_s] - MOE_SUB * k_s, 0, MOE_SUB)
    used = sidx <= last
    sup_b0 = jnp.where(used, b0_s, b0_s + nb_s - 1).astype(jnp.int32)
    sup_nb = jnp.where(used, nb_s, 0).astype(jnp.int32)
    pos = ((sup_start[flat_e] + local // rsup) * rsup + local % rsup).astype(jnp.int32)
    return weights, slot_tok, e_s, sup_b0, sup_nb, sidx, pos


def kernel(x, c, ctx, c_ctx, w_mod, b_mod, norm1_g, w_in, s5_lam_re, s5_lam_im, s5_log_dt, s5_b_re, s5_b_im, s5_c_re, s5_c_im, s5_d, s5_w_val, s5_w_gate, ssd_conv_w, ssd_conv_b, ssd_a_log, ssd_dt_bias, ssd_d, ssd_norm_g, ssd_w_out, w_o, norm2_g, moe_w_group, moe_b_group, moe_w_expert, moe_b_expert, moe_w_gate, moe_w_up, moe_w_down, final_g):
    depth = w_mod.shape[0]
    assert depth == 1, "single-layer block"
    bsz, n_lat, d = x.shape
    l_ctx = ctx.shape[1]
    w5 = s5_d.shape[1]
    nh = ssd_d.shape[1]
    w = nh * SSD_HEAD_DIM
    conv_dim = ssd_conv_w.shape[2]
    ng = (conv_dim - w) // (2 * SSD_STATE)
    ssd_in = w + conv_dim + 2 * nh
    o1, o2 = w5, w5 + ssd_in
    l = 0

    cc = jnp.concatenate([c, c_ctx[None, :]], axis=0)
    cc = jnp.pad(cc, ((0, (-cc.shape[0]) % 8), (0, 0)))
    mods = _mod_call(cc, w_mod[l], b_mod[l]).reshape(cc.shape[0], 6, 1, d)
    i_sh1, i_sc1, i_g1, i_sh2, i_sc2, i_g2 = range(6)

    w_in_l = w_in[l]
    o_dt = o1 + w + conv_dim

    hn_rm, hn_cm = _norm_lat_call(x, norm1_g[l], mods, i_sc1, i_sh1)
    hc = _norm_ctx_call(ctx, norm1_g[l], mods, bsz, i_sc1, i_sh1)
    hn_rm = hn_rm.reshape(bsz * n_lat, d)
    hn_cm = hn_cm.reshape(bsz * n_lat, d)
    hc = hc.reshape(bsz * l_ctx, d)

    u_lat = _mm_call(hn_rm, w_in_l, BF16, "in_s5", 0, o1).reshape(bsz, n_lat, w5)
    u_ctx = _mm_call(hc, w_in_l, BF16, "in_s5_ctx", 0, o1).reshape(bsz, l_ctx, w5)
    gates = _mm_call(hn_rm, w_in_l, BF16, "in_gates", o2, 2 * d)
    p_lat = _mm_call(hn_cm, w_in_l, BF16, "in_ssd", o1, w + conv_dim).reshape(bsz, n_lat, w + conv_dim)
    p_ctx = _mm_call(hc, w_in_l, BF16, "in_ssd_ctx", o1, w + conv_dim).reshape(bsz, l_ctx, w + conv_dim)
    dt_lat = _mm_call(hn_cm, w_in_l, F32, "in_dt", o_dt, 2 * nh).reshape(bsz, n_lat, LANE)
    dt_ctx = _mm_call(hc, w_in_l, F32, "in_dt_ctx", o_dt, 2 * nh).reshape(bsz, l_ctx, LANE)

    nj = w5 // LANE
    bmat, cmat, lre, lim = _s5_params(s5_lam_re[l], s5_lam_im[l], s5_log_dt[l], s5_b_re[l], s5_b_im[l],
                                      s5_c_re[l], s5_c_im[l], bsz)
    s5_zero = jnp.zeros((nj, 2, 2 * bsz, lre.shape[-1]), F32)
    (s5_ctx,) = _s5_call(u_ctx, bmat, cmat, lre, lim, s5_zero, False)
    ya_f, ya_b, _ = _s5_call(u_lat, bmat, cmat, lre, lim, s5_ctx, True)

    xbc_ctx = _conv_call(p_ctx, ssd_conv_w[l], ssd_conv_b[l], w)
    xbc_lat = _conv_call(p_lat, ssd_conv_w[l], ssd_conv_b[l], w)
    pad_h = LANE - 2 * nh
    bias = jnp.pad(ssd_dt_bias[l].astype(F32).reshape(1, 2 * nh), ((0, 0), (0, pad_h)))
    alog = jnp.pad(ssd_a_log[l].astype(F32).reshape(1, 2 * nh), ((0, 0), (0, pad_h)))
    h_zero = jnp.zeros((bsz, 2, SSD_STATE, w), F32)
    (h_ctx,) = _ssd_call(xbc_ctx, dt_ctx, h_zero, bias, alog, nh, ng, False)
    y_f, y_b, _ = _ssd_call(xbc_lat, dt_lat, h_ctx, bias, alog, nh, ng, True)
    d_vec = jnp.repeat(ssd_d[l].astype(F32), SSD_HEAD_DIM).reshape(1, w)
    y_ssd = _gnorm_call(y_f, y_b, xbc_lat, p_lat, d_vec, ssd_norm_g[l].astype(F32).reshape(1, w), ng)

    m_lat = bsz * n_lat
    part_a = _glu_call(u_lat.reshape(m_lat, w5), ya_f.reshape(m_lat, w5), ya_b.reshape(m_lat, w5),
                       s5_d[l].astype(F32).reshape(1, w5),
                       s5_w_val[l].astype(BF16), s5_w_gate[l].astype(BF16), gates)
    merged = _merge_call(y_ssd, ssd_w_out[l], gates, part_a)
    x1 = _resid_call(merged, w_o[l], x, mods, i_g1)

    ngr = moe_w_group.shape[-1]
    ne = moe_w_expert.shape[-1]
    wr = jnp.concatenate([moe_w_group[l], moe_w_expert[l]], axis=1).astype(F32)
    wr = jnp.pad(wr, ((0, 0), (0, (-(ngr + ne)) % LANE)))
    br = jnp.concatenate([moe_b_group[l], moe_b_expert[l]]).astype(F32)
    br = jnp.pad(br, (0, (-(ngr + ne)) % LANE)).reshape(1, -1)
    hx, sel = _norm_router_call(x1, norm2_g[l], mods, i_sc2, i_sh2, wr, br, ngr, ne)
    m = bsz * n_lat
    hx = hx.reshape(m, d)
    weights, slot_tok, sup_e, sup_b0, sup_nb, sup_out, pos = _route(sel.reshape(m, -1), ne)
    xs = hx[slot_tok]
    y_slots = _moe_call(sup_e, sup_b0, sup_nb, sup_out, xs, moe_w_gate[l], moe_w_up[l], moe_w_down[l])
    ya = y_slots[pos[:m]]
    yb = y_slots[pos[m:]]
    return _final_call(x1, ya, yb, weights.astype(F32), mods, i_g2, final_g.reshape(1, d))
```

```python
import functools
import math

import jax
import jax.numpy as jnp
from jax import lax
from jax.experimental import pallas as pl
from jax.experimental.pallas import tpu as pltpu

F32 = jnp.float32
BF16 = jnp.bfloat16

GRID_W = 64
EPS = 1e-6
LANE = 128
SSD_HEAD_DIM = 64
SSD_STATE = 128
SSD_CHUNK = 128
SSD_CONV = 5
MOE_TOP_K = 2
MOE_BLOCK = 256
MOE_SUB = 4
XPOSE_TILE = 16
XPOSE_PAD = 8
S5_CHUNK = 256
S5_ROW_PAD = 8


def _cp(sem, mb):
    return pltpu.CompilerParams(dimension_semantics=sem, vmem_limit_bytes=mb * 1024 * 1024)


def _sigmoid(x):
    return 1.0 / (1.0 + jnp.exp(-x))


def _silu(x):
    return x * _sigmoid(x)


def _mod_kernel(c_ref, w_ref, b_ref, o_ref):
    s = _silu(c_ref[...])
    o_ref[...] = jnp.dot(s.astype(BF16), w_ref[...].astype(BF16),
                         preferred_element_type=F32) + b_ref[...]


def _mod_call(cc, w, b):
    r, d = cc.shape
    n = w.shape[1]
    tn = min(512, n)
    return pl.pallas_call(
        _mod_kernel, grid=(n // tn,),
        in_specs=[pl.BlockSpec((r, d), lambda j: (0, 0)),
                  pl.BlockSpec((d, tn), lambda j: (0, j)),
                  pl.BlockSpec((1, tn), lambda j: (0, j))],
        out_specs=pl.BlockSpec((r, tn), lambda j: (0, j)),
        out_shape=jax.ShapeDtypeStruct((r, n), F32),
        compiler_params=_cp(("parallel",), 40), name="mod")(cc, w, b.reshape(1, n))


def _rmsmod(x, g, sc, sh):
    ms = jnp.mean(x * x, axis=-1, keepdims=True)
    return (x * lax.rsqrt(ms + EPS) * g) * (1.0 + sc) + sh


def _norm_lat_kernel(x_ref, g_ref, sc_ref, sh_ref, orm_ref, ocm_ref, t_ref, *, tb, d):
    pitch = tb + XPOSE_PAD
    for r in range(tb):
        y = _rmsmod(x_ref[r], g_ref[...], sc_ref[...], sh_ref[...])
        orm_ref[r] = y.astype(BF16)
        for kk in range(d // LANE):
            t_ref[kk, r * pitch:r * pitch + tb, :] = y[:, kk * LANE:(kk + 1) * LANE]
    for wl in range(tb):
        ocm_ref[wl] = jnp.concatenate(
            [t_ref[kk, pl.ds(wl, tb, stride=pitch), :] for kk in range(d // LANE)], axis=1).astype(BF16)


def _norm_lat_call(x, g, mods, isc, ish):
    b, l, d = x.shape
    rows = l // GRID_W
    tb = XPOSE_TILE
    xv = x.reshape(b, rows, GRID_W, d)
    orm, ocm = pl.pallas_call(
        functools.partial(_norm_lat_kernel, tb=tb, d=d), grid=(b, rows // tb, GRID_W // tb),
        in_specs=[pl.BlockSpec((None, tb, tb, d), lambda i, r, c: (i, r, c, 0)),
                  pl.BlockSpec((1, d), lambda i, r, c: (0, 0)),
                  pl.BlockSpec((None, None, 1, d), lambda i, r, c: (i, isc, 0, 0)),
                  pl.BlockSpec((None, None, 1, d), lambda i, r, c: (i, ish, 0, 0))],
        out_specs=[pl.BlockSpec((None, tb, tb, d), lambda i, r, c: (i, r, c, 0)),
                   pl.BlockSpec((None, tb, tb, d), lambda i, r, c: (i, c, r, 0))],
        out_shape=[jax.ShapeDtypeStruct((b, rows, GRID_W, d), BF16),
                   jax.ShapeDtypeStruct((b, GRID_W, rows, d), BF16)],
        scratch_shapes=[pltpu.VMEM((d // LANE, tb * (tb + XPOSE_PAD), LANE), F32)],
        compiler_params=_cp(("parallel", "parallel", "parallel"), 48),
        name="norm1_lat")(xv, g.reshape(1, d), mods, mods)
    return orm.reshape(b, l, d), ocm.reshape(b, l, d)


def _norm_kernel(x_ref, g_ref, sc_ref, sh_ref, o_ref):
    o_ref[...] = _rmsmod(x_ref[...], g_ref[...], sc_ref[...], sh_ref[...]).astype(o_ref.dtype)


def _norm_ctx_call(x, g, mods, row, isc, ish):
    b, l, d = x.shape
    tr = min(256, l)
    return pl.pallas_call(
        _norm_kernel, grid=(b, l // tr),
        in_specs=[pl.BlockSpec((None, tr, d), lambda i, r: (i, r, 0)),
                  pl.BlockSpec((1, d), lambda i, r: (0, 0)),
                  pl.BlockSpec((None, None, 1, d), lambda i, r: (row, isc, 0, 0)),
                  pl.BlockSpec((None, None, 1, d), lambda i, r: (row, ish, 0, 0))],
        out_specs=pl.BlockSpec((None, tr, d), lambda i, r: (i, r, 0)),
        out_shape=jax.ShapeDtypeStruct((b, l, d), BF16),
        compiler_params=_cp(("parallel", "parallel"), 40), name="norm1_ctx")(x, g.reshape(1, d), mods, mods)


def _first_argmax(v, vmax, lane):
    return jnp.min(jnp.where(v == vmax, lane, float(LANE)), axis=-1, keepdims=True)


def _norm_router_kernel(x_ref, g_ref, sc_ref, sh_ref, wr_ref, br_ref, o_ref, sel_ref, *, ngr, ne):
    y = _rmsmod(x_ref[...], g_ref[...], sc_ref[...], sh_ref[...])
    o_ref[...] = y.astype(BF16)
    lg = jnp.dot(y, wr_ref[...], precision=lax.Precision.HIGHEST, preferred_element_type=F32) + br_ref[...]
    epg = ne // ngr
    lane = lax.broadcasted_iota(jnp.int32, lg.shape, 1).astype(F32)
    ninf = -jnp.inf
    gl = jnp.where(lane < ngr, lg, ninf)
    ge = jnp.exp(gl - jnp.max(gl, axis=-1, keepdims=True))
    g_prob = ge / jnp.sum(ge, axis=-1, keepdims=True)
    g_p = jnp.max(g_prob, axis=-1, keepdims=True)
    lo = ngr + epg * _first_argmax(g_prob, g_p, lane)
    cand = jnp.where(jnp.logical_and(lane >= lo, lane < lo + epg), lg, ninf)
    v1 = jnp.max(cand, axis=-1, keepdims=True)
    l1 = _first_argmax(cand, v1, lane)
    rest = jnp.where(lane == l1, ninf, cand)
    v2 = jnp.max(rest, axis=-1, keepdims=True)
    l2 = _first_argmax(rest, v2, lane)
    ex = jnp.exp(v2 - v1)
    w1 = g_p * (1.0 / (1.0 + ex))
    w2 = g_p * (ex / (1.0 + ex))
    sel_ref[...] = jnp.where(lane == 0, l1 - ngr, jnp.where(lane == 1, l2 - ngr,
                             jnp.where(lane == 2, w1, jnp.where(lane == 3, w2, 0.0))))


def _norm_router_call(x, g, mods, isc, ish, wr, br, ngr, ne):
    b, l, d = x.shape
    tr = min(256, l)
    nr = wr.shape[1]
    return pl.pallas_call(
        functools.partial(_norm_router_kernel, ngr=ngr, ne=ne), grid=(b, l // tr),
        in_specs=[pl.BlockSpec((None, tr, d), lambda i, r: (i, r, 0)),
                  pl.BlockSpec((1, d), lambda i, r: (0, 0)),
                  pl.BlockSpec((None, None, 1, d), lambda i, r: (i, isc, 0, 0)),
                  pl.BlockSpec((None, None, 1, d), lambda i, r: (i, ish, 0, 0)),
                  pl.BlockSpec((d, nr), lambda i, r: (0, 0)),
                  pl.BlockSpec((1, nr), lambda i, r: (0, 0))],
        out_specs=[pl.BlockSpec((None, tr, d), lambda i, r: (i, r, 0)),
                   pl.BlockSpec((None, tr, nr), lambda i, r: (i, r, 0))],
        out_shape=[jax.ShapeDtypeStruct((b, l, d), BF16), jax.ShapeDtypeStruct((b, l, nr), F32)],
        compiler_params=_cp(("parallel", "parallel"), 40),
        name="norm2_router")(x, g.reshape(1, d), mods, mods, wr, br)


def _mm_kernel(a_ref, b_ref, o_ref):
    o_ref[...] = jnp.dot(a_ref[...], b_ref[...].astype(BF16), preferred_element_type=F32).astype(o_ref.dtype)


def _mm_tiles(m, n):
    tm = min(1024, m)
    tn = min(512, n)
    return tm, tn


def _mm_call(a, b, out_dtype, name, col0=0, n=None):
    m, k = a.shape
    n = b.shape[1] - col0 if n is None else n
    tm, tn = _mm_tiles(m, n)
    if col0 % LANE or n % tn or n % LANE:
        b = b[:, col0:col0 + n]
        pad = (-n) % LANE
        b = jnp.pad(b, ((0, 0), (0, pad)))
        n, col0 = n + pad, 0
        tm, tn = _mm_tiles(m, n)
    return pl.pallas_call(
        _mm_kernel, grid=(m // tm, n // tn),
        in_specs=[pl.BlockSpec((tm, k), lambda i, j: (i, 0)),
                  pl.BlockSpec((pl.Element(k), pl.Element(tn)), lambda i, j: (0, (col0 // LANE + j * (tn // LANE)) * LANE))],
        out_specs=pl.BlockSpec((tm, tn), lambda i, j: (i, j)),
        out_shape=jax.ShapeDtypeStruct((m, n), out_dtype),
        compiler_params=_cp(("parallel", "parallel"), 48), name=name)(a, b)


def _gelu_tanh(x):
    return x * (0.5 * (1.0 + jnp.tanh(math.sqrt(2.0 / math.pi) * (x + 0.044715 * (x * x * x)))))


def _glu_kernel(u_ref, yf_ref, yb_ref, d_ref, wv_ref, wg_ref, gate_ref, o_ref, a_ref):
    @pl.when(pl.program_id(1) == 0)
    def _():
        y = d_ref[...] * u_ref[...].astype(F32) + yf_ref[...].astype(F32) + yb_ref[...].astype(F32)
        a_ref[...] = _gelu_tanh(y).astype(BF16)

    a = a_ref[...]
    val = jnp.dot(a, wv_ref[...], preferred_element_type=F32)
    gl = jnp.dot(a, wg_ref[...], preferred_element_type=F32)
    o_ref[...] = (_sigmoid(gate_ref[...].astype(F32)) * (val * _sigmoid(gl))).astype(o_ref.dtype)


def _glu_call(u, yf, yb, dvec, wv, wg, gates):
    m, k = u.shape
    n = wv.shape[1]
    tm, tn = _mm_tiles(m, n)
    row = pl.BlockSpec((tm, k), lambda i, j: (i, 0))
    return pl.pallas_call(
        _glu_kernel, grid=(m // tm, n // tn),
        in_specs=[row, row, row,
                  pl.BlockSpec((1, k), lambda i, j: (0, 0)),
                  pl.BlockSpec((k, tn), lambda i, j: (0, j)),
                  pl.BlockSpec((k, tn), lambda i, j: (0, j)),
                  pl.BlockSpec((tm, tn), lambda i, j: (i, j))],
        out_specs=pl.BlockSpec((tm, tn), lambda i, j: (i, j)),
        out_shape=jax.ShapeDtypeStruct((m, n), BF16),
        scratch_shapes=[pltpu.VMEM((tm, k), BF16)],
        compiler_params=_cp(("parallel", "arbitrary"), 56), name="s5_glu")(u, yf, yb, dvec, wv, wg, gates)


def _merge_kernel(a_ref, w_ref, gate_ref, pa_ref, o_ref):
    br = jnp.dot(a_ref[...], w_ref[...].astype(BF16), preferred_element_type=F32)
    o_ref[...] = (pa_ref[...].astype(F32) + _sigmoid(gate_ref[...].astype(F32)) * br).astype(o_ref.dtype)


def _merge_call(a, w, gates, part_a):
    m, k = a.shape
    n = w.shape[1]
    tm, tn = _mm_tiles(m, n)
    off = n // tn
    return pl.pallas_call(
        _merge_kernel, grid=(m // tm, n // tn),
        in_specs=[pl.BlockSpec((tm, k), lambda i, j: (i, 0)),
                  pl.BlockSpec((k, tn), lambda i, j: (0, j)),
                  pl.BlockSpec((tm, tn), lambda i, j: (i, j + off)),
                  pl.BlockSpec((tm, tn), lambda i, j: (i, j))],
        out_specs=pl.BlockSpec((tm, tn), lambda i, j: (i, j)),
        out_shape=jax.ShapeDtypeStruct((m, n), BF16),
        compiler_params=_cp(("parallel", "parallel"), 48), name="ssd_out_merge")(a, w, gates, part_a)


def _resid_kernel(a_ref, w_ref, x_ref, g_ref, o_ref):
    mix = jnp.dot(a_ref[...], w_ref[...].astype(BF16), preferred_element_type=F32)
    o_ref[...] = x_ref[...] + g_ref[...] * mix


def _resid_call(a, w, x, mods, ig):
    b, l, d = x.shape
    k = a.shape[1]
    tm, tn = _mm_tiles(l, d)
    nt = l // tm
    return pl.pallas_call(
        _resid_kernel, grid=(b * nt, d // tn),
        in_specs=[pl.BlockSpec((tm, k), lambda i, j: (i, 0)),
                  pl.BlockSpec((k, tn), lambda i, j: (0, j)),
                  pl.BlockSpec((None, tm, tn), lambda i, j: (i // nt, i % nt, j)),
                  pl.BlockSpec((None, None, 1, tn), lambda i, j: (i // nt, ig, 0, j))],
        out_specs=pl.BlockSpec((None, tm, tn), lambda i, j: (i // nt, i % nt, j)),
        out_shape=jax.ShapeDtypeStruct((b, l, d), F32),
        compiler_params=_cp(("parallel", "parallel"), 48), name="w_o_resid")(a, w, x, mods)


def _s5_kernel(*refs, nb, tp, p8, need_y):
    if need_y:
        (fa_ref, fb_ref, ba_ref, bb_ref, b_ref, c_ref, d_ref, lre_ref, lim_ref, h0_ref,
         yf_ref, yb_ref, h_ref, buf_ref, il_ref) = refs
    else:
        fa_ref, fb_ref, ba_ref, bb_ref, b_ref, c_ref, d_ref, lre_ref, lim_ref, h0_ref, h_ref, buf_ref = refs
    q = 2 * nb
    nk = p8 // LANE
    pitch = tp + S5_ROW_PAD

    @pl.when(pl.program_id(1) == 0)
    def _():
        h_ref[...] = h0_ref[...]

    rev = (lax.broadcasted_iota(jnp.int32, (tp, tp), 0) + lax.broadcasted_iota(jnp.int32, (tp, tp), 1)
           == tp - 1).astype(BF16)
    lhs = []
    for d in range(2):
        if d == 0:
            u = jnp.concatenate([fa_ref[...].reshape(nb * tp, LANE), fb_ref[...].reshape(nb * tp, LANE)], axis=1)
        else:
            u = jnp.concatenate(
                [jnp.dot(rev, jnp.concatenate([bb_ref[b], ba_ref[b]], axis=1),
                         preferred_element_type=F32).astype(BF16) for b in range(nb)], axis=0)
        lhs.append(u)
        bu = jnp.dot(u, b_ref[d], preferred_element_type=F32)
        for b in range(nb):
            r0 = (d * nb + b) * pitch
            for k in range(2 * nk):
                buf_ref[k, r0:r0 + tp, :] = bu[b * tp:(b + 1) * tp, k * LANE:(k + 1) * LANE]
    ar = [lre_ref[:, k * LANE:(k + 1) * LANE] for k in range(nk)]
    ai = [lim_ref[:, k * LANE:(k + 1) * LANE] for k in range(nk)]

    def step(s, carry):
        rows = pl.ds(s, q, stride=pitch)
        out = []
        for k in range(nk):
            hr, hi = carry[k]
            nr = ar[k] * hr - ai[k] * hi + buf_ref[k, rows, :]
            ni = ar[k] * hi + ai[k] * hr + buf_ref[nk + k, rows, :]
            buf_ref[k, rows, :] = hr
            buf_ref[nk + k, rows, :] = hi
            out.append((nr, ni))
        return tuple(out)

    init = tuple((h_ref[0, :, k * LANE:(k + 1) * LANE], h_ref[1, :, k * LANE:(k + 1) * LANE]) for k in range(nk))
    fin = lax.fori_loop(0, tp, step, init, unroll=8)
    for k in range(nk):
        h_ref[0, :, k * LANE:(k + 1) * LANE] = fin[k][0]
        h_ref[1, :, k * LANE:(k + 1) * LANE] = fin[k][1]

    if need_y:
        for d in range(2):
            h = jnp.concatenate(
                [jnp.concatenate([buf_ref[k, (d * nb + b) * pitch:(d * nb + b) * pitch + tp, :].astype(BF16)
                                  for k in range(2 * nk)], axis=1) for b in range(nb)], axis=0)
            y = (jnp.dot(h, c_ref[d], preferred_element_type=F32)
                 + jnp.dot(lhs[d], d_ref[d], preferred_element_type=F32))
            for b in range(nb):
                yb = y[b * tp:(b + 1) * tp]
                if d == 1:
                    yb = jnp.dot(rev, yb.astype(BF16), preferred_element_type=F32)
                first, second = (0, 1) if d == 0 else (1, 0)
                il_ref[pl.ds(first, tp, stride=2), :] = yb[:, 0:LANE]
                il_ref[pl.ds(second, tp, stride=2), :] = yb[:, LANE:2 * LANE]
                (yf_ref if d == 0 else yb_ref)[b] = il_ref[...].astype(BF16)


def _s5_call(u, bmat, cmat, dmat, lre, lim, h0, need_y):
    nb, l, w5 = u.shape
    nj = w5 // LANE
    q = 2 * nb
    t = min(S5_CHUNK, l)
    tp = t // 2
    p8 = lre.shape[-1]
    nc = l // t
    u2 = u.reshape(nb, l // 2, 2 * w5)
    hspec = pl.BlockSpec((None, 2, q, p8), lambda j, c: (j, 0, 0, 0))
    pblk = (nb, tp, LANE)
    in_specs = [pl.BlockSpec(pblk, lambda j, c: (0, c, j)), pl.BlockSpec(pblk, lambda j, c: (0, c, nj + j)),
                pl.BlockSpec(pblk, lambda j, c: (0, nc - 1 - c, j)),
                pl.BlockSpec(pblk, lambda j, c: (0, nc - 1 - c, nj + j)),
                pl.BlockSpec((None, 2, 2 * LANE, 2 * p8), lambda j, c: (j, 0, 0, 0)),
                pl.BlockSpec((None, 2, 2 * p8, 2 * LANE), lambda j, c: (j, 0, 0, 0)),
                pl.BlockSpec((None, 2, 2 * LANE, 2 * LANE), lambda j, c: (j, 0, 0, 0)),
                pl.BlockSpec((None, q, p8), lambda j, c: (j, 0, 0)),
                pl.BlockSpec((None, q, p8), lambda j, c: (j, 0, 0)),
                hspec]
    out_specs = [hspec]
    out_shape = [jax.ShapeDtypeStruct((nj, 2, q, p8), F32)]
    scratch = [pltpu.VMEM((2 * p8 // LANE, q * (tp + S5_ROW_PAD), LANE), F32)]
    if need_y:
        out_specs = [pl.BlockSpec((nb, t, LANE), lambda j, c: (0, c, j)),
                     pl.BlockSpec((nb, t, LANE), lambda j, c: (0, nc - 1 - c, j))] + out_specs
        out_shape = [jax.ShapeDtypeStruct((nb, l, w5), BF16)] * 2 + out_shape
        scratch = scratch + [pltpu.VMEM((t, LANE), F32)]
    kern = functools.partial(_s5_kernel, nb=nb, tp=tp, p8=p8, need_y=need_y)
    return pl.pallas_call(
        kern, grid=(nj, nc), in_specs=in_specs, out_specs=out_specs, out_shape=out_shape,
        scratch_shapes=scratch, compiler_params=_cp(("parallel", "arbitrary"), 48),
        name="s5_scan" if need_y else "s5_scan_ctx")(u2, u2, u2, u2, bmat, cmat, dmat, lre, lim, h0)


def _s5_params(lam_re, lam_im, log_dt, b_re, b_im, c_re, c_im, nb):
    _, g, p = lam_re.shape
    s = b_re.shape[-1]
    gpb = LANE // s
    nj = g // gpb
    lam = lax.complex(lam_re.astype(F32), lam_im.astype(F32))
    lam_bar = jnp.exp(lam * jnp.exp(log_dt.astype(F32))[..., None])
    b_bar = ((lam_bar - 1.0) / lam)[..., None] * lax.complex(b_re.astype(F32), b_im.astype(F32))
    eye = jnp.eye(gpb, dtype=F32)

    def bmat_of(bpart):
        bb = bpart.reshape(2, nj, gpb, p, s)
        m = jnp.einsum('dnkps,kl->dnkslp', bb, eye)
        return m.reshape(2, nj, gpb * s, gpb * p).transpose(1, 0, 2, 3)

    def b_cols(bc):
        return jnp.concatenate([bmat_of(bc.real), bmat_of(bc.imag)], axis=-1)

    bmat = jnp.concatenate([b_cols(lam_bar[..., None] * b_bar), b_cols(b_bar)], axis=-2).astype(BF16)

    def cmat_of(cpart):
        cc = cpart.astype(F32).reshape(2, nj, gpb, s, p)
        m = jnp.einsum('dnksp,kl->dnkpls', cc, eye)
        return m.reshape(2, nj, gpb * p, gpb * s).transpose(1, 0, 2, 3)

    def c_rows(cc):
        return jnp.concatenate([cmat_of(cc.real), -cmat_of(cc.imag)], axis=-2)

    cc = lax.complex(c_re.astype(F32), c_im.astype(F32))
    c_l1 = cc * lam_bar[:, :, None, :]
    c_l2 = c_l1 * lam_bar[:, :, None, :]
    cmat = jnp.concatenate([c_rows(c_l1), c_rows(c_l2)], axis=-1).astype(BF16)

    def direct_of(cpart):
        m0 = jnp.einsum('dgsp,dgpt->dgts', cpart, b_bar).real
        mm = jnp.einsum('dnkts,kl->dnktls', m0.reshape(2, nj, gpb, s, s), eye)
        return mm.reshape(2, nj, gpb * s, gpb * s).transpose(1, 0, 2, 3)

    d0 = direct_of(cc)
    dmat = jnp.concatenate([jnp.concatenate([d0, direct_of(c_l1)], axis=-1),
                            jnp.concatenate([jnp.zeros_like(d0), d0], axis=-1)], axis=-2).astype(BF16)

    def lam_of(part):
        v = part.reshape(2, nj, gpb * p).transpose(1, 0, 2)
        return jnp.repeat(v, nb, axis=1)

    lam2 = lam_bar * lam_bar
    return bmat, cmat, dmat, lam_of(lam2.real), lam_of(lam2.imag)


def _conv_kernel(x_ref, w_ref, b_ref, o_ref, *, l):
    x = x_ref[...].astype(F32)
    rows = lax.broadcasted_iota(jnp.int32, x.shape, 0)
    half = SSD_CONV // 2
    acc = x * w_ref[half:half + 1, :] + b_ref[...]
    for k in range(SSD_CONV):
        if k == half:
            continue
        off = k - half
        xs = pltpu.roll(x, shift=(-off) % l, axis=0)
        valid = jnp.logical_and(rows + off >= 0, rows + off < l)
        acc = acc + jnp.where(valid, xs, 0.0) * w_ref[k:k + 1, :]
    o_ref[...] = _silu(acc).astype(o_ref.dtype)


def _conv_call(proj, w, bias, col0):
    b, l, _ = proj.shape
    c = w.shape[1]
    tc = 256
    off = col0 // tc
    return pl.pallas_call(
        functools.partial(_conv_kernel, l=l), grid=(b, c // tc),
        in_specs=[pl.BlockSpec((None, l, tc), lambda i, j: (i, 0, j + off)),
                  pl.BlockSpec((SSD_CONV, tc), lambda i, j: (0, j)),
                  pl.BlockSpec((1, tc), lambda i, j: (0, j))],
        out_specs=pl.BlockSpec((None, l, tc), lambda i, j: (i, 0, j)),
        out_shape=jax.ShapeDtypeStruct((b, l, c), BF16),
        compiler_params=_cp(("parallel", "parallel"), 40), name="ssd_conv")(proj, w, bias.reshape(1, c))


def _softplus(x):
    return jnp.maximum(x, 0.0) + jnp.log1p(jnp.exp(-jnp.abs(x)))


def _ssd_dir(xbc_ref, dt_ref, y_ref, h_ref, bias, a_neg, d, *, nh, ng, need_y):
    qn = SSD_CHUNK
    hd = SSD_HEAD_DIM
    w = nh * hd
    gw = w // ng
    gn = ng * SSD_STATE
    ii = lax.broadcasted_iota(jnp.int32, (qn, qn), 0)
    jj = lax.broadcasted_iota(jnp.int32, (qn, qn), 1)
    mask = (jj <= ii) if d == 0 else (jj >= ii)
    lmat = mask.astype(F32)
    dtv = _softplus(dt_ref[...] + bias)
    cum = jnp.dot(lmat, dtv * a_neg, precision=lax.Precision.HIGHEST, preferred_element_type=F32)
    cum_t = cum.T
    edge = qn - 1 if d == 0 else 0
    tot = cum[edge:edge + 1, :]
    dt_t = dtv.T
    wt_t = dt_t * jnp.exp(cum_t[:, edge:edge + 1] - cum_t)
    decay = jnp.exp(tot)
    lane = lax.broadcasted_iota(jnp.int32, (qn, LANE), 1)
    left = lane < hd
    zero = jnp.zeros((), BF16)
    for g in range(ng):
        bg = xbc_ref[:, w + g * SSD_STATE:w + (g + 1) * SSD_STATE]
        cg = xbc_ref[:, w + gn + g * SSD_STATE:w + gn + (g + 1) * SSD_STATE]
        bg_t = bg.astype(F32).T
        s_in = h_ref[d, :, g * gw:(g + 1) * gw]
        if need_y:
            cb = lax.dot_general(cg, bg, (((1,), (1,)), ((), ())), preferred_element_type=F32)
            yoff = jnp.dot(cg, s_in.astype(BF16), preferred_element_type=F32)
        for pr in range(gw // LANE):
            c0 = d * nh + (g * gw) // hd + 2 * pr
            col = g * gw + pr * LANE
            xp = xbc_ref[:, col:col + LANE]
            r = jnp.concatenate([jnp.where(left, xp, zero), jnp.where(left, zero, xp)], axis=0)
            tops, bots, cols = [], [], []
            for c in (c0, c0 + 1):
                bots.append((bg_t * wt_t[c:c + 1, :]).astype(BF16))
                if need_y:
                    ccol = jnp.broadcast_to(cum[:, c:c + 1], (qn, qn))
                    seg = jnp.where(mask, jnp.exp(ccol - cum_t[c:c + 1, :]), 0.0)
                    tops.append((cb * seg * dt_t[c:c + 1, :]).astype(BF16))
                    cols.append(ccol)
            dec = jnp.where(left[0:1], decay[:, c0:c0 + 1], decay[:, c0 + 1:c0 + 2])
            s_old = s_in[:, pr * LANE:(pr + 1) * LANE]
            if need_y:
                lhs = jnp.concatenate([jnp.concatenate(tops, axis=1), jnp.concatenate(bots, axis=1)], axis=0)
                out = jnp.dot(lhs, r, preferred_element_type=F32)
                ec = jnp.exp(jnp.where(left, cols[0], cols[1]))
                y_ref[:, col:col + LANE] = (out[:qn] + yoff[:, pr * LANE:(pr + 1) * LANE] * ec).astype(y_ref.dtype)
                s_new = out[qn:]
            else:
                s_new = jnp.dot(jnp.concatenate(bots, axis=1), r, preferred_element_type=F32)
            h_ref[d, :, col:col + LANE] = s_old * dec + s_new


def _ssd_kernel(*refs, nh, ng, need_y):
    if need_y:
        xf_ref, xb_ref, dtf_ref, dtb_ref, h0_ref, bias_ref, alog_ref, yf_ref, yb_ref, h_ref = refs
    else:
        xf_ref, xb_ref, dtf_ref, dtb_ref, h0_ref, bias_ref, alog_ref, h_ref = refs
        yf_ref = yb_ref = None

    @pl.when(pl.program_id(1) == 0)
    def _():
        h_ref[...] = h0_ref[...]

    bias = bias_ref[...]
    a_neg = -jnp.exp(alog_ref[...])
    _ssd_dir(xf_ref, dtf_ref, yf_ref, h_ref, bias, a_neg, 0, nh=nh, ng=ng, need_y=need_y)
    _ssd_dir(xb_ref, dtb_ref, yb_ref, h_ref, bias, a_neg, 1, nh=nh, ng=ng, need_y=need_y)


def _ssd_call(xbc, dt, h0, bias, alog, nh, ng, need_y):
    b, l, cd = xbc.shape
    w = nh * SSD_HEAD_DIM
    nc = l // SSD_CHUNK
    q = SSD_CHUNK
    fwd = lambda i, s: (i, s, 0)
    bwd = lambda i, s: (i, nc - 1 - s, 0)
    hspec = pl.BlockSpec((None, 2, SSD_STATE, w), lambda i, s: (i, 0, 0, 0))
    in_specs = [pl.BlockSpec((None, q, cd), fwd), pl.BlockSpec((None, q, cd), bwd),
                pl.BlockSpec((None, q, LANE), fwd), pl.BlockSpec((None, q, LANE), bwd),
                hspec,
                pl.BlockSpec((1, LANE), lambda i, s: (0, 0)), pl.BlockSpec((1, LANE), lambda i, s: (0, 0))]
    out_specs = [hspec]
    out_shape = [jax.ShapeDtypeStruct((b, 2, SSD_STATE, w), F32)]
    if need_y:
        out_specs = [pl.BlockSpec((None, q, w), fwd), pl.BlockSpec((None, q, w), bwd)] + out_specs
        out_shape = [jax.ShapeDtypeStruct((b, l, w), BF16)] * 2 + out_shape
    return pl.pallas_call(
        functools.partial(_ssd_kernel, nh=nh, ng=ng, need_y=need_y), grid=(b, nc),
        in_specs=in_specs, out_specs=out_specs, out_shape=out_shape,
        compiler_params=_cp(("parallel", "arbitrary"), 48),
        name="ssd_scan" if need_y else "ssd_scan_ctx")(xbc, xbc, dt, dt, h0, bias, alog)


def _gnorm_kernel(yf_ref, yb_ref, xs_ref, z_ref, d_ref, g_ref, o_ref, t_ref, *, rows, tb):
    y = d_ref[...] * xs_ref[...].astype(F32) + yf_ref[...].astype(F32) + yb_ref[...].astype(F32)
    y = y * _silu(z_ref[...].astype(F32))
    ms = jnp.mean(y * y, axis=-1, keepdims=True)
    res = y * lax.rsqrt(ms + EPS) * g_ref[...]
    gw = res.shape[1]
    pitch = rows + XPOSE_PAD
    for wl in range(tb):
        for kk in range(gw // LANE):
            t_ref[kk, wl * pitch:wl * pitch + rows, :] = res[wl * rows:(wl + 1) * rows, kk * LANE:(kk + 1) * LANE]
    for r in range(rows):
        o_ref[r] = jnp.concatenate(
            [t_ref[kk, pl.ds(r, tb, stride=pitch), :] for kk in range(gw // LANE)], axis=1).astype(o_ref.dtype)


def _gnorm_call(yf, yb, xbc, proj, dvec, gvec, ng):
    b, l, w = yf.shape
    rows = l // GRID_W
    tb = XPOSE_TILE
    gw = w // ng
    blk = pl.BlockSpec((None, tb * rows, gw), lambda i, s, g: (i, s, g))
    vec = pl.BlockSpec((1, gw), lambda i, s, g: (0, g))
    out = pl.pallas_call(
        functools.partial(_gnorm_kernel, rows=rows, tb=tb), grid=(b, GRID_W // tb, ng),
        in_specs=[blk, blk, blk, blk, vec, vec],
        out_specs=pl.BlockSpec((None, rows, tb, gw), lambda i, s, g: (i, 0, s, g)),
        out_shape=jax.ShapeDtypeStruct((b, rows, GRID_W, w), BF16),
        scratch_shapes=[pltpu.VMEM((gw // LANE, tb * (rows + XPOSE_PAD), LANE), F32)],
        compiler_params=_cp(("parallel", "parallel", "parallel"), 40),
        name="ssd_gnorm")(yf, yb, xbc, proj, dvec, gvec)
    return out.reshape(b * l, w)


def _moe_kernel(e_ref, b0_ref, nb_ref, so_ref, *refs, nsub, nh):
    xs = refs[:nsub]
    wg_ref, wu_ref, wd_ref, o_ref, act_ref, wdb_ref = refs[nsub:]
    s = pl.program_id(0)
    t = pl.program_id(1)
    nb = nb_ref[s]

    @pl.when(jnp.logical_and(t < nh, nb > 0))
    def _():
        wg = wg_ref[...].astype(BF16)
        wu = wu_ref[...].astype(BF16)
        for k in range(nsub):
            @pl.when(k < nb)
            def _():
                x = xs[k][...]
                gt = jnp.dot(x, wg, preferred_element_type=F32)
                up = jnp.dot(x, wu, preferred_element_type=F32)
                act_ref[k, t] = (_silu(gt) * up).astype(BF16)

    @pl.when(jnp.logical_and(t >= nh, nb > 0))
    def _():
        wdb_ref[...] = wd_ref[...].astype(BF16)
        for k in range(nsub):
            rows = slice(k * MOE_BLOCK, (k + 1) * MOE_BLOCK)

            @pl.when(k < nb)
            def _():
                a = jnp.concatenate([act_ref[k, h] for h in range(nh)], axis=1)
                o_ref[rows, :] = jnp.dot(a, wdb_ref[...], preferred_element_type=F32).astype(o_ref.dtype)

            @pl.when(k >= nb)
            def _():
                o_ref[rows, :] = jnp.zeros((MOE_BLOCK, o_ref.shape[1]), o_ref.dtype)

    @pl.when(jnp.logical_and(t >= nh, nb == 0))
    def _():
        o_ref[...] = jnp.zeros(o_ref.shape, o_ref.dtype)


def _moe_call(sup_e, sup_b0, sup_nb, sup_out, xs, w_gate, w_up, w_down):
    n_slots, d = xs.shape
    hid = w_gate.shape[-1]
    th = min(256, hid)
    tn = min(1024, d)
    nh = hid // th
    nt = d // tn
    ns = sup_e.shape[0]
    rsup = MOE_SUB * MOE_BLOCK

    def x_map(k):
        def index(s, t, e, b0, nb, so):
            sx = jnp.where(t < nh, s, jnp.minimum(s + 1, ns - 1))
            return (b0[sx] + jnp.minimum(k, jnp.maximum(nb[sx] - 1, 0)), 0)
        return index

    def hid_idx(s, t, nb):
        return jnp.where(nb[s] > 0, jnp.minimum(t, nh - 1), nh - 1)

    def col_idx(s, t, nb):
        return jnp.where(nb[s] > 0, jnp.clip(t - nh, 0, nt - 1), nt - 1)

    gs = pltpu.PrefetchScalarGridSpec(
        num_scalar_prefetch=4, grid=(ns, nh + nt),
        in_specs=[pl.BlockSpec((MOE_BLOCK, d), x_map(k)) for k in range(MOE_SUB)] + [
            pl.BlockSpec((None, d, th), lambda s, t, e, b0, nb, so: (e[s], 0, hid_idx(s, t, nb))),
            pl.BlockSpec((None, d, th), lambda s, t, e, b0, nb, so: (e[s], 0, hid_idx(s, t, nb))),
            pl.BlockSpec((None, hid, tn), lambda s, t, e, b0, nb, so: (e[s], 0, col_idx(s, t, nb)))],
        out_specs=pl.BlockSpec((rsup, tn), lambda s, t, e, b0, nb, so: (so[s], jnp.clip(t - nh, 0, nt - 1))),
        scratch_shapes=[pltpu.VMEM((MOE_SUB, nh, MOE_BLOCK, th), BF16), pltpu.VMEM((hid, tn), BF16)])
    return pl.pallas_call(
        functools.partial(_moe_kernel, nsub=MOE_SUB, nh=nh), grid_spec=gs,
        out_shape=jax.ShapeDtypeStruct((ns * rsup, d), BF16),
        compiler_params=_cp(("arbitrary", "arbitrary"), 60), name="moe_experts")(
            sup_e, sup_b0, sup_nb, sup_out, *([xs] * MOE_SUB), w_gate, w_up, w_down)


def _final_kernel(x_ref, ya_ref, yb_ref, w_ref, g2_ref, fg_ref, o_ref):
    wts = w_ref[...]
    moe = ya_ref[...].astype(F32) * wts[:, 0:1] + yb_ref[...].astype(F32) * wts[:, 1:2]
    x = x_ref[...] + g2_ref[...] * moe
    ms = jnp.mean(x * x, axis=-1, keepdims=True)
    o_ref[...] = x * lax.rsqrt(ms + EPS) * fg_ref[...]


def _final_call(x, ya, yb, wts, mods, ig, fg):
    b, l, d = x.shape
    tr = min(256, l)
    nt = l // tr
    row = lambda i, r: (i * nt + r, 0)
    return pl.pallas_call(
        _final_kernel, grid=(b, nt),
        in_specs=[pl.BlockSpec((None, tr, d), lambda i, r: (i, r, 0)),
                  pl.BlockSpec((tr, d), row), pl.BlockSpec((tr, d), row),
                  pl.BlockSpec((tr, MOE_TOP_K), row),
                  pl.BlockSpec((None, None, 1, d), lambda i, r: (i, ig, 0, 0)),
                  pl.BlockSpec((1, d), lambda i, r: (0, 0))],
        out_specs=pl.BlockSpec((None, tr, d), lambda i, r: (i, r, 0)),
        out_shape=jax.ShapeDtypeStruct((b, l, d), F32),
        compiler_params=_cp(("parallel", "parallel"), 40), name="combine_final")(x, ya, yb, wts, mods, fg)


def _route(sel, ne):
    m = sel.shape[0]
    weights = sel[:, MOE_TOP_K:2 * MOE_TOP_K]
    n_assign = m * MOE_TOP_K
    flat_e = jnp.concatenate([sel[:, k] for k in range(MOE_TOP_K)], axis=0).astype(jnp.int32)
    onehot = (flat_e[:, None] == jnp.arange(ne, dtype=flat_e.dtype)[None, :]).astype(jnp.int32)
    csum = jnp.cumsum(onehot, axis=0)
    counts = csum[-1]
    local = jnp.sum(csum * onehot, axis=1) - 1
    nblk_e = (counts + MOE_BLOCK - 1) // MOE_BLOCK
    padded = nblk_e * MOE_BLOCK
    pad_end = jnp.cumsum(padded)
    pad_start = pad_end - padded
    dest = pad_start[flat_e] + local
    n_blocks = -(-(n_assign + ne * (MOE_BLOCK - 1)) // MOE_BLOCK)
    tok = jnp.arange(n_assign, dtype=jnp.int32) % m
    n_slots = n_blocks * MOE_BLOCK
    slot_tok = (jnp.arange(n_slots, dtype=jnp.int32) % m).at[dest].set(tok)
    rsup = MOE_SUB * MOE_BLOCK
    ns_e = (nblk_e + MOE_SUB - 1) // MOE_SUB
    sup_end = jnp.cumsum(ns_e)
    sup_start = sup_end - ns_e
    n_sup = (n_blocks + (MOE_SUB - 1) * ne) // MOE_SUB
    sidx = jnp.arange(n_sup, dtype=jnp.int32)
    last = sup_end[-1] - 1
    s_eff = jnp.minimum(sidx, last)
    e_s = jnp.minimum(jnp.searchsorted(sup_end, s_eff, side='right'), ne - 1).astype(jnp.int32)
    k_s = s_eff - sup_start[e_s]
    b0_s = pad_start[e_s] // MOE_BLOCK + MOE_SUB * k_s
    nb_s = jnp.clip(nblk_e[e_s] - MOE_SUB * k_s, 0, MOE_SUB)
    used = sidx <= last
    sup_b0 = jnp.where(used, b0_s, b0_s + nb_s - 1).astype(jnp.int32)
    sup_nb = jnp.where(used, nb_s, 0).astype(jnp.int32)
    pos = ((sup_start[flat_e] + local // rsup) * rsup + local % rsup).astype(jnp.int32)
    return weights, slot_tok, e_s, sup_b0, sup_nb, sidx, pos


def kernel(x, c, ctx, c_ctx, w_mod, b_mod, norm1_g, w_in, s5_lam_re, s5_lam_im, s5_log_dt, s5_b_re, s5_b_im, s5_c_re, s5_c_im, s5_d, s5_w_val, s5_w_gate, ssd_conv_w, ssd_conv_b, ssd_a_log, ssd_dt_bias, ssd_d, ssd_norm_g, ssd_w_out, w_o, norm2_g, moe_w_group, moe_b_group, moe_w_expert, moe_b_expert, moe_w_gate, moe_w_up, moe_w_down, final_g):
    depth = w_mod.shape[0]
    assert depth == 1, "single-layer block"
    bsz, n_lat, d = x.shape
    l_ctx = ctx.shape[1]
    w5 = s5_d.shape[1]
    nh = ssd_d.shape[1]
    w = nh * SSD_HEAD_DIM
    conv_dim = ssd_conv_w.shape[2]
    ng = (conv_dim - w) // (2 * SSD_STATE)
    ssd_in = w + conv_dim + 2 * nh
    o1, o2 = w5, w5 + ssd_in
    l = 0

    cc = jnp.concatenate([c, c_ctx[None, :]], axis=0)
    cc = jnp.pad(cc, ((0, (-cc.shape[0]) % 8), (0, 0)))
    mods = _mod_call(cc, w_mod[l], b_mod[l]).reshape(cc.shape[0], 6, 1, d)
    i_sh1, i_sc1, i_g1, i_sh2, i_sc2, i_g2 = range(6)

    w_in_l = w_in[l]
    o_dt = o1 + w + conv_dim

    hn_rm, hn_cm = _norm_lat_call(x, norm1_g[l], mods, i_sc1, i_sh1)
    hc = _norm_ctx_call(ctx, norm1_g[l], mods, bsz, i_sc1, i_sh1)
    hn_rm = hn_rm.reshape(bsz * n_lat, d)
    hn_cm = hn_cm.reshape(bsz * n_lat, d)
    hc = hc.reshape(bsz * l_ctx, d)

    u_lat = _mm_call(hn_rm, w_in_l, BF16, "in_s5", 0, o1).reshape(bsz, n_lat, w5)
    u_ctx = _mm_call(hc, w_in_l, BF16, "in_s5_ctx", 0, o1).reshape(bsz, l_ctx, w5)
    gates = _mm_call(hn_rm, w_in_l, BF16, "in_gates", o2, 2 * d)
    p_lat = _mm_call(hn_cm, w_in_l, BF16, "in_ssd", o1, w + conv_dim).reshape(bsz, n_lat, w + conv_dim)
    p_ctx = _mm_call(hc, w_in_l, BF16, "in_ssd_ctx", o1, w + conv_dim).reshape(bsz, l_ctx, w + conv_dim)
    dt_lat = _mm_call(hn_cm, w_in_l, F32, "in_dt", o_dt, 2 * nh).reshape(bsz, n_lat, LANE)
    dt_ctx = _mm_call(hc, w_in_l, F32, "in_dt_ctx", o_dt, 2 * nh).reshape(bsz, l_ctx, LANE)

    nj = w5 // LANE
    bmat, cmat, dmat, lre, lim = _s5_params(s5_lam_re[l], s5_lam_im[l], s5_log_dt[l], s5_b_re[l], s5_b_im[l],
                                            s5_c_re[l], s5_c_im[l], bsz)
    s5_zero = jnp.zeros((nj, 2, 2 * bsz, lre.shape[-1]), F32)
    (s5_ctx,) = _s5_call(u_ctx, bmat, cmat, dmat, lre, lim, s5_zero, False)
    ya_f, ya_b, _ = _s5_call(u_lat, bmat, cmat, dmat, lre, lim, s5_ctx, True)

    xbc_ctx = _conv_call(p_ctx, ssd_conv_w[l], ssd_conv_b[l], w)
    xbc_lat = _conv_call(p_lat, ssd_conv_w[l], ssd_conv_b[l], w)
    pad_h = LANE - 2 * nh
    bias = jnp.pad(ssd_dt_bias[l].astype(F32).reshape(1, 2 * nh), ((0, 0), (0, pad_h)))
    alog = jnp.pad(ssd_a_log[l].astype(F32).reshape(1, 2 * nh), ((0, 0), (0, pad_h)))
    h_zero = jnp.zeros((bsz, 2, SSD_STATE, w), F32)
    (h_ctx,) = _ssd_call(xbc_ctx, dt_ctx, h_zero, bias, alog, nh, ng, False)
    y_f, y_b, _ = _ssd_call(xbc_lat, dt_lat, h_ctx, bias, alog, nh, ng, True)
    d_vec = jnp.repeat(ssd_d[l].astype(F32), SSD_HEAD_DIM).reshape(1, w)
    y_ssd = _gnorm_call(y_f, y_b, xbc_lat, p_lat, d_vec, ssd_norm_g[l].astype(F32).reshape(1, w), ng)

    m_lat = bsz * n_lat
    part_a = _glu_call(u_lat.reshape(m_lat, w5), ya_f.reshape(m_lat, w5), ya_b.reshape(m_lat, w5),
                       s5_d[l].astype(F32).reshape(1, w5),
                       s5_w_val[l].astype(BF16), s5_w_gate[l].astype(BF16), gates)
    merged = _merge_call(y_ssd, ssd_w_out[l], gates, part_a)
    x1 = _resid_call(merged, w_o[l], x, mods, i_g1)

    ngr = moe_w_group.shape[-1]
    ne = moe_w_expert.shape[-1]
    wr = jnp.concatenate([moe_w_group[l], moe_w_expert[l]], axis=1).astype(F32)
    wr = jnp.pad(wr, ((0, 0), (0, (-(ngr + ne)) % LANE)))
    br = jnp.concatenate([moe_b_group[l], moe_b_expert[l]]).astype(F32)
    br = jnp.pad(br, (0, (-(ngr + ne)) % LANE)).reshape(1, -1)
    hx, sel = _norm_router_call(x1, norm2_g[l], mods, i_sc2, i_sh2, wr, br, ngr, ne)
    m = bsz * n_lat
    hx = hx.reshape(m, d)
    weights, slot_tok, sup_e, sup_b0, sup_nb, sup_out, pos = _route(sel.reshape(m, -1), ne)
    xs = hx[slot_tok]
    y_slots = _moe_call(sup_e, sup_b0, sup_nb, sup_out, xs, moe_w_gate[l], moe_w_up[l], moe_w_down[l])
    ya = y_slots[pos[:m]]
    yb = y_slots[pos[m:]]
    return _final_call(x1, ya, yb, weights.astype(F32), mods, i_g2, final_g.reshape(1, d))
```

```python
import functools
import math

import jax
import jax.numpy as jnp
from jax import lax
from jax.experimental import pallas as pl
from jax.experimental.pallas import tpu as pltpu

F32 = jnp.float32
BF16 = jnp.bfloat16

GRID_W = 64
EPS = 1e-6
LANE = 128
SSD_HEAD_DIM = 64
SSD_STATE = 128
SSD_CHUNK = 128
SSD_CONV = 5
MOE_TOP_K = 2
MOE_BLOCK = 256
MOE_SUB = 4
XPOSE_TILE = 16
XPOSE_PAD = 8
S5_CHUNK = 256
S5_ROW_PAD = 8


def _cp(sem, mb):
    return pltpu.CompilerParams(dimension_semantics=sem, vmem_limit_bytes=mb * 1024 * 1024)


def _sigmoid(x):
    return 1.0 / (1.0 + jnp.exp(-x))


def _silu(x):
    return x * _sigmoid(x)


def _mod_kernel(c_ref, w_ref, b_ref, o_ref):
    s = _silu(c_ref[...])
    o_ref[...] = jnp.dot(s.astype(BF16), w_ref[...].astype(BF16),
                         preferred_element_type=F32) + b_ref[...]


def _mod_call(cc, w, b):
    r, d = cc.shape
    n = w.shape[1]
    tn = min(512, n)
    return pl.pallas_call(
        _mod_kernel, grid=(n // tn,),
        in_specs=[pl.BlockSpec((r, d), lambda j: (0, 0)),
                  pl.BlockSpec((d, tn), lambda j: (0, j)),
                  pl.BlockSpec((1, tn), lambda j: (0, j))],
        out_specs=pl.BlockSpec((r, tn), lambda j: (0, j)),
        out_shape=jax.ShapeDtypeStruct((r, n), F32),
        compiler_params=_cp(("parallel",), 40), name="mod")(cc, w, b.reshape(1, n))


def _rmsmod(x, g, sc, sh):
    ms = jnp.mean(x * x, axis=-1, keepdims=True)
    return (x * lax.rsqrt(ms + EPS) * g) * (1.0 + sc) + sh


def _norm_lat_kernel(x_ref, g_ref, sc_ref, sh_ref, orm_ref, ocm_ref, t_ref, *, tb, d):
    pitch = tb + XPOSE_PAD
    for r in range(tb):
        y = _rmsmod(x_ref[r], g_ref[...], sc_ref[...], sh_ref[...])
        orm_ref[r] = y.astype(BF16)
        for kk in range(d // LANE):
            t_ref[kk, r * pitch:r * pitch + tb, :] = y[:, kk * LANE:(kk + 1) * LANE]
    for wl in range(tb):
        ocm_ref[wl] = jnp.concatenate(
            [t_ref[kk, pl.ds(wl, tb, stride=pitch), :] for kk in range(d // LANE)], axis=1).astype(BF16)


def _norm_lat_call(x, g, mods, isc, ish):
    b, l, d = x.shape
    rows = l // GRID_W
    tb = XPOSE_TILE
    xv = x.reshape(b, rows, GRID_W, d)
    orm, ocm = pl.pallas_call(
        functools.partial(_norm_lat_kernel, tb=tb, d=d), grid=(b, rows // tb, GRID_W // tb),
        in_specs=[pl.BlockSpec((None, tb, tb, d), lambda i, r, c: (i, r, c, 0)),
                  pl.BlockSpec((1, d), lambda i, r, c: (0, 0)),
                  pl.BlockSpec((None, None, 1, d), lambda i, r, c: (i, isc, 0, 0)),
                  pl.BlockSpec((None, None, 1, d), lambda i, r, c: (i, ish, 0, 0))],
        out_specs=[pl.BlockSpec((None, tb, tb, d), lambda i, r, c: (i, r, c, 0)),
                   pl.BlockSpec((None, tb, tb, d), lambda i, r, c: (i, c, r, 0))],
        out_shape=[jax.ShapeDtypeStruct((b, rows, GRID_W, d), BF16),
                   jax.ShapeDtypeStruct((b, GRID_W, rows, d), BF16)],
        scratch_shapes=[pltpu.VMEM((d // LANE, tb * (tb + XPOSE_PAD), LANE), F32)],
        compiler_params=_cp(("parallel", "parallel", "parallel"), 48),
        name="norm1_lat")(xv, g.reshape(1, d), mods, mods)
    return orm.reshape(b, l, d), ocm.reshape(b, l, d)


def _norm_kernel(x_ref, g_ref, sc_ref, sh_ref, o_ref):
    o_ref[...] = _rmsmod(x_ref[...], g_ref[...], sc_ref[...], sh_ref[...]).astype(o_ref.dtype)


def _norm_ctx_call(x, g, mods, row, isc, ish):
    b, l, d = x.shape
    tr = min(256, l)
    return pl.pallas_call(
        _norm_kernel, grid=(b, l // tr),
        in_specs=[pl.BlockSpec((None, tr, d), lambda i, r: (i, r, 0)),
                  pl.BlockSpec((1, d), lambda i, r: (0, 0)),
                  pl.BlockSpec((None, None, 1, d), lambda i, r: (row, isc, 0, 0)),
                  pl.BlockSpec((None, None, 1, d), lambda i, r: (row, ish, 0, 0))],
        out_specs=pl.BlockSpec((None, tr, d), lambda i, r: (i, r, 0)),
        out_shape=jax.ShapeDtypeStruct((b, l, d), BF16),
        compiler_params=_cp(("parallel", "parallel"), 40), name="norm1_ctx")(x, g.reshape(1, d), mods, mods)


def _first_argmax(v, vmax, lane):
    return jnp.min(jnp.where(v == vmax, lane, float(LANE)), axis=-1, keepdims=True)


def _norm_router_kernel(x_ref, g_ref, sc_ref, sh_ref, wr_ref, br_ref, o_ref, sel_ref, *, ngr, ne):
    y = _rmsmod(x_ref[...], g_ref[...], sc_ref[...], sh_ref[...])
    o_ref[...] = y.astype(BF16)
    lg = jnp.dot(y, wr_ref[...], precision=lax.Precision.HIGHEST, preferred_element_type=F32) + br_ref[...]
    epg = ne // ngr
    lane = lax.broadcasted_iota(jnp.int32, lg.shape, 1).astype(F32)
    ninf = -jnp.inf
    gl = jnp.where(lane < ngr, lg, ninf)
    ge = jnp.exp(gl - jnp.max(gl, axis=-1, keepdims=True))
    g_prob = ge / jnp.sum(ge, axis=-1, keepdims=True)
    g_p = jnp.max(g_prob, axis=-1, keepdims=True)
    lo = ngr + epg * _first_argmax(g_prob, g_p, lane)
    cand = jnp.where(jnp.logical_and(lane >= lo, lane < lo + epg), lg, ninf)
    v1 = jnp.max(cand, axis=-1, keepdims=True)
    l1 = _first_argmax(cand, v1, lane)
    rest = jnp.where(lane == l1, ninf, cand)
    v2 = jnp.max(rest, axis=-1, keepdims=True)
    l2 = _first_argmax(rest, v2, lane)
    ex = jnp.exp(v2 - v1)
    w1 = g_p * (1.0 / (1.0 + ex))
    w2 = g_p * (ex / (1.0 + ex))
    sel_ref[...] = jnp.where(lane == 0, l1 - ngr, jnp.where(lane == 1, l2 - ngr,
                             jnp.where(lane == 2, w1, jnp.where(lane == 3, w2, 0.0))))


def _norm_router_call(x, g, mods, isc, ish, wr, br, ngr, ne):
    b, l, d = x.shape
    tr = min(256, l)
    nr = wr.shape[1]
    return pl.pallas_call(
        functools.partial(_norm_router_kernel, ngr=ngr, ne=ne), grid=(b, l // tr),
        in_specs=[pl.BlockSpec((None, tr, d), lambda i, r: (i, r, 0)),
                  pl.BlockSpec((1, d), lambda i, r: (0, 0)),
                  pl.BlockSpec((None, None, 1, d), lambda i, r: (i, isc, 0, 0)),
                  pl.BlockSpec((None, None, 1, d), lambda i, r: (i, ish, 0, 0)),
                  pl.BlockSpec((d, nr), lambda i, r: (0, 0)),
                  pl.BlockSpec((1, nr), lambda i, r: (0, 0))],
        out_specs=[pl.BlockSpec((None, tr, d), lambda i, r: (i, r, 0)),
                   pl.BlockSpec((None, tr, nr), lambda i, r: (i, r, 0))],
        out_shape=[jax.ShapeDtypeStruct((b, l, d), BF16), jax.ShapeDtypeStruct((b, l, nr), F32)],
        compiler_params=_cp(("parallel", "parallel"), 40),
        name="norm2_router")(x, g.reshape(1, d), mods, mods, wr, br)


def _mm_kernel(a_ref, b_ref, o_ref):
    o_ref[...] = jnp.dot(a_ref[...], b_ref[...].astype(BF16), preferred_element_type=F32).astype(o_ref.dtype)


def _mm_tiles(m, n):
    tm = min(1024, m)
    tn = min(512, n)
    return tm, tn


def _mm_call(a, b, out_dtype, name, col0=0, n=None):
    m, k = a.shape
    n = b.shape[1] - col0 if n is None else n
    tm, tn = _mm_tiles(m, n)
    if col0 % LANE or n % tn or n % LANE:
        b = b[:, col0:col0 + n]
        pad = (-n) % LANE
        b = jnp.pad(b, ((0, 0), (0, pad)))
        n, col0 = n + pad, 0
        tm, tn = _mm_tiles(m, n)
    return pl.pallas_call(
        _mm_kernel, grid=(m // tm, n // tn),
        in_specs=[pl.BlockSpec((tm, k), lambda i, j: (i, 0)),
                  pl.BlockSpec((pl.Element(k), pl.Element(tn)), lambda i, j: (0, (col0 // LANE + j * (tn // LANE)) * LANE))],
        out_specs=pl.BlockSpec((tm, tn), lambda i, j: (i, j)),
        out_shape=jax.ShapeDtypeStruct((m, n), out_dtype),
        compiler_params=_cp(("parallel", "parallel"), 48), name=name)(a, b)


def _gelu_tanh(x):
    return x * (0.5 * (1.0 + jnp.tanh(math.sqrt(2.0 / math.pi) * (x + 0.044715 * (x * x * x)))))


def _glu_kernel(u_ref, yf_ref, yb_ref, d_ref, wv_ref, wg_ref, gate_ref, o_ref, a_ref):
    @pl.when(pl.program_id(1) == 0)
    def _():
        y = d_ref[...] * u_ref[...].astype(F32) + yf_ref[...].astype(F32) + yb_ref[...].astype(F32)
        a_ref[...] = _gelu_tanh(y).astype(BF16)

    a = a_ref[...]
    val = jnp.dot(a, wv_ref[...], preferred_element_type=F32)
    gl = jnp.dot(a, wg_ref[...], preferred_element_type=F32)
    o_ref[...] = (_sigmoid(gate_ref[...].astype(F32)) * (val * _sigmoid(gl))).astype(o_ref.dtype)


def _glu_call(u, yf, yb, dvec, wv, wg, gates):
    m, k = u.shape
    n = wv.shape[1]
    tm, tn = _mm_tiles(m, n)
    row = pl.BlockSpec((tm, k), lambda i, j: (i, 0))
    return pl.pallas_call(
        _glu_kernel, grid=(m // tm, n // tn),
        in_specs=[row, row, row,
                  pl.BlockSpec((1, k), lambda i, j: (0, 0)),
                  pl.BlockSpec((k, tn), lambda i, j: (0, j)),
                  pl.BlockSpec((k, tn), lambda i, j: (0, j)),
                  pl.BlockSpec((tm, tn), lambda i, j: (i, j))],
        out_specs=pl.BlockSpec((tm, tn), lambda i, j: (i, j)),
        out_shape=jax.ShapeDtypeStruct((m, n), BF16),
        scratch_shapes=[pltpu.VMEM((tm, k), BF16)],
        compiler_params=_cp(("parallel", "arbitrary"), 56), name="s5_glu")(u, yf, yb, dvec, wv, wg, gates)


def _merge_kernel(a_ref, w_ref, gate_ref, pa_ref, o_ref):
    br = jnp.dot(a_ref[...], w_ref[...].astype(BF16), preferred_element_type=F32)
    o_ref[...] = (pa_ref[...].astype(F32) + _sigmoid(gate_ref[...].astype(F32)) * br).astype(o_ref.dtype)


def _merge_call(a, w, gates, part_a):
    m, k = a.shape
    n = w.shape[1]
    tm, tn = _mm_tiles(m, n)
    off = n // tn
    return pl.pallas_call(
        _merge_kernel, grid=(m // tm, n // tn),
        in_specs=[pl.BlockSpec((tm, k), lambda i, j: (i, 0)),
                  pl.BlockSpec((k, tn), lambda i, j: (0, j)),
                  pl.BlockSpec((tm, tn), lambda i, j: (i, j + off)),
                  pl.BlockSpec((tm, tn), lambda i, j: (i, j))],
        out_specs=pl.BlockSpec((tm, tn), lambda i, j: (i, j)),
        out_shape=jax.ShapeDtypeStruct((m, n), BF16),
        compiler_params=_cp(("parallel", "parallel"), 48), name="ssd_out_merge")(a, w, gates, part_a)


def _resid_kernel(a_ref, w_ref, x_ref, g_ref, o_ref):
    mix = jnp.dot(a_ref[...], w_ref[...].astype(BF16), preferred_element_type=F32)
    o_ref[...] = x_ref[...] + g_ref[...] * mix


def _resid_call(a, w, x, mods, ig):
    b, l, d = x.shape
    k = a.shape[1]
    tm, tn = _mm_tiles(l, d)
    nt = l // tm
    return pl.pallas_call(
        _resid_kernel, grid=(b * nt, d // tn),
        in_specs=[pl.BlockSpec((tm, k), lambda i, j: (i, 0)),
                  pl.BlockSpec((k, tn), lambda i, j: (0, j)),
                  pl.BlockSpec((None, tm, tn), lambda i, j: (i // nt, i % nt, j)),
                  pl.BlockSpec((None, None, 1, tn), lambda i, j: (i // nt, ig, 0, j))],
        out_specs=pl.BlockSpec((None, tm, tn), lambda i, j: (i // nt, i % nt, j)),
        out_shape=jax.ShapeDtypeStruct((b, l, d), F32),
        compiler_params=_cp(("parallel", "parallel"), 48), name="w_o_resid")(a, w, x, mods)


def _s5_kernel(*refs, nb, tp, p8, need_y):
    if need_y:
        uf_ref, ub_ref, b_ref, c_ref, d_ref, lre_ref, lim_ref, h0_ref, yf_ref, yb_ref, h_ref, buf_ref, il_ref = refs
    else:
        uf_ref, ub_ref, b_ref, c_ref, d_ref, lre_ref, lim_ref, h0_ref, h_ref, buf_ref, il_ref = refs
    q = 2 * nb
    nk = p8 // LANE
    pitch = tp + S5_ROW_PAD

    @pl.when(pl.program_id(1) == 0)
    def _():
        h_ref[...] = h0_ref[...]

    rev = (lax.broadcasted_iota(jnp.int32, (tp, tp), 0) + lax.broadcasted_iota(jnp.int32, (tp, tp), 1)
           == tp - 1).astype(BF16)
    def pair_rows(u_ref, b, newer_first):
        il_ref[...] = u_ref[b].astype(F32)
        even = il_ref[pl.ds(0, tp, stride=2), :].astype(BF16)
        odd = il_ref[pl.ds(1, tp, stride=2), :].astype(BF16)
        return jnp.concatenate([odd, even] if newer_first else [even, odd], axis=1)

    lhs = []
    for d in range(2):
        if d == 0:
            u = jnp.concatenate([pair_rows(uf_ref, b, False) for b in range(nb)], axis=0)
        else:
            u = jnp.concatenate(
                [jnp.dot(rev, pair_rows(ub_ref, b, True), preferred_element_type=F32).astype(BF16)
                 for b in range(nb)], axis=0)
        lhs.append(u)
        bu = jnp.dot(u, b_ref[d], preferred_element_type=F32)
        for b in range(nb):
            r0 = (d * nb + b) * pitch
            for k in range(2 * nk):
                buf_ref[k, r0:r0 + tp, :] = bu[b * tp:(b + 1) * tp, k * LANE:(k + 1) * LANE]
    ar = [lre_ref[:, k * LANE:(k + 1) * LANE] for k in range(nk)]
    ai = [lim_ref[:, k * LANE:(k + 1) * LANE] for k in range(nk)]

    def step(s, carry):
        rows = pl.ds(s, q, stride=pitch)
        out = []
        for k in range(nk):
            hr, hi = carry[k]
            nr = ar[k] * hr - ai[k] * hi + buf_ref[k, rows, :]
            ni = ar[k] * hi + ai[k] * hr + buf_ref[nk + k, rows, :]
            buf_ref[k, rows, :] = hr
            buf_ref[nk + k, rows, :] = hi
            out.append((nr, ni))
        return tuple(out)

    init = tuple((h_ref[0, :, k * LANE:(k + 1) * LANE], h_ref[1, :, k * LANE:(k + 1) * LANE]) for k in range(nk))
    fin = lax.fori_loop(0, tp, step, init, unroll=8)
    for k in range(nk):
        h_ref[0, :, k * LANE:(k + 1) * LANE] = fin[k][0]
        h_ref[1, :, k * LANE:(k + 1) * LANE] = fin[k][1]

    if need_y:
        for d in range(2):
            h = jnp.concatenate(
                [jnp.concatenate([buf_ref[k, (d * nb + b) * pitch:(d * nb + b) * pitch + tp, :].astype(BF16)
                                  for k in range(2 * nk)], axis=1) for b in range(nb)], axis=0)
            y = (jnp.dot(h, c_ref[d], preferred_element_type=F32)
                 + jnp.dot(lhs[d], d_ref[d], preferred_element_type=F32))
            for b in range(nb):
                yb = y[b * tp:(b + 1) * tp]
                if d == 1:
                    yb = jnp.dot(rev, yb.astype(BF16), preferred_element_type=F32)
                first, second = (0, 1) if d == 0 else (1, 0)
                il_ref[pl.ds(first, tp, stride=2), :] = yb[:, 0:LANE]
                il_ref[pl.ds(second, tp, stride=2), :] = yb[:, LANE:2 * LANE]
                (yf_ref if d == 0 else yb_ref)[b] = il_ref[...].astype(BF16)


def _s5_call(u, bmat, cmat, dmat, lre, lim, h0, need_y):
    nb, l, w5 = u.shape
    nj = w5 // LANE
    q = 2 * nb
    t = min(S5_CHUNK, l)
    tp = t // 2
    p8 = lre.shape[-1]
    nc = l // t
    hspec = pl.BlockSpec((None, 2, q, p8), lambda j, c: (j, 0, 0, 0))
    ublk = (nb, t, LANE)
    in_specs = [pl.BlockSpec(ublk, lambda j, c: (0, c, j)), pl.BlockSpec(ublk, lambda j, c: (0, nc - 1 - c, j)),
                pl.BlockSpec((None, 2, 2 * LANE, 2 * p8), lambda j, c: (j, 0, 0, 0)),
                pl.BlockSpec((None, 2, 2 * p8, 2 * LANE), lambda j, c: (j, 0, 0, 0)),
                pl.BlockSpec((None, 2, 2 * LANE, 2 * LANE), lambda j, c: (j, 0, 0, 0)),
                pl.BlockSpec((None, q, p8), lambda j, c: (j, 0, 0)),
                pl.BlockSpec((None, q, p8), lambda j, c: (j, 0, 0)),
                hspec]
    out_specs = [hspec]
    out_shape = [jax.ShapeDtypeStruct((nj, 2, q, p8), F32)]
    scratch = [pltpu.VMEM((2 * p8 // LANE, q * (tp + S5_ROW_PAD), LANE), F32), pltpu.VMEM((t, LANE), F32)]
    if need_y:
        out_specs = [pl.BlockSpec(ublk, lambda j, c: (0, c, j)),
                     pl.BlockSpec(ublk, lambda j, c: (0, nc - 1 - c, j))] + out_specs
        out_shape = [jax.ShapeDtypeStruct((nb, l, w5), BF16)] * 2 + out_shape
    kern = functools.partial(_s5_kernel, nb=nb, tp=tp, p8=p8, need_y=need_y)
    return pl.pallas_call(
        kern, grid=(nj, nc), in_specs=in_specs, out_specs=out_specs, out_shape=out_shape,
        scratch_shapes=scratch, compiler_params=_cp(("parallel", "arbitrary"), 48),
        name="s5_scan" if need_y else "s5_scan_ctx")(u, u, bmat, cmat, dmat, lre, lim, h0)


def _s5_params(lam_re, lam_im, log_dt, b_re, b_im, c_re, c_im, nb):
    _, g, p = lam_re.shape
    s = b_re.shape[-1]
    gpb = LANE // s
    nj = g // gpb
    lam = lax.complex(lam_re.astype(F32), lam_im.astype(F32))
    lam_bar = jnp.exp(lam * jnp.exp(log_dt.astype(F32))[..., None])
    b_bar = ((lam_bar - 1.0) / lam)[..., None] * lax.complex(b_re.astype(F32), b_im.astype(F32))
    eye = jnp.eye(gpb, dtype=F32)

    def quadrants(parts):
        x = jnp.stack(parts, axis=0)
        r, c = x.shape[-2:]
        x = x.reshape(2, 2, 2, nj, gpb, r, c)
        m = jnp.einsum('abdnkrc,kl->ndakrblc', x, eye)
        return m.reshape(nj, 2, 2 * gpb * r, 2 * gpb * c).astype(BF16)

    tb = lambda z: jnp.swapaxes(z, -1, -2)
    lb = lam_bar[..., None] * b_bar
    bmat = quadrants([tb(lb.real), tb(lb.imag), tb(b_bar.real), tb(b_bar.imag)])
    cc = lax.complex(c_re.astype(F32), c_im.astype(F32))
    c_l1 = cc * lam_bar[:, :, None, :]
    c_l2 = c_l1 * lam_bar[:, :, None, :]
    cmat = quadrants([tb(c_l1.real), tb(c_l2.real), -tb(c_l1.imag), -tb(c_l2.imag)])
    m0 = jnp.einsum('dgsp,dgpt->dgts', cc, b_bar).real
    m1 = jnp.einsum('dgsp,dgpt->dgts', c_l1, b_bar).real
    dmat = quadrants([m0, m1, jnp.zeros_like(m0), m0])

    def lam_of(part):
        v = part.reshape(2, nj, gpb * p).transpose(1, 0, 2)
        return jnp.repeat(v, nb, axis=1)

    lam2 = lam_bar * lam_bar
    return bmat, cmat, dmat, lam_of(lam2.real), lam_of(lam2.imag)


def _conv_kernel(x_ref, w_ref, b_ref, o_ref, *, l):
    x = x_ref[...].astype(F32)
    rows = lax.broadcasted_iota(jnp.int32, x.shape, 0)
    half = SSD_CONV // 2
    acc = x * w_ref[half:half + 1, :] + b_ref[...]
    for k in range(SSD_CONV):
        if k == half:
            continue
        off = k - half
        xs = pltpu.roll(x, shift=(-off) % l, axis=0)
        valid = jnp.logical_and(rows + off >= 0, rows + off < l)
        acc = acc + jnp.where(valid, xs, 0.0) * w_ref[k:k + 1, :]
    o_ref[...] = _silu(acc).astype(o_ref.dtype)


def _conv_call(proj, w, bias, col0):
    b, l, _ = proj.shape
    c = w.shape[1]
    tc = 256
    off = col0 // tc
    return pl.pallas_call(
        functools.partial(_conv_kernel, l=l), grid=(b, c // tc),
        in_specs=[pl.BlockSpec((None, l, tc), lambda i, j: (i, 0, j + off)),
                  pl.BlockSpec((SSD_CONV, tc), lambda i, j: (0, j)),
                  pl.BlockSpec((1, tc), lambda i, j: (0, j))],
        out_specs=pl.BlockSpec((None, l, tc), lambda i, j: (i, 0, j)),
        out_shape=jax.ShapeDtypeStruct((b, l, c), BF16),
        compiler_params=_cp(("parallel", "parallel"), 40), name="ssd_conv")(proj, w, bias.reshape(1, c))


def _softplus(x):
    return jnp.maximum(x, 0.0) + jnp.log1p(jnp.exp(-jnp.abs(x)))


def _ssd_dir(xbc_ref, dt_ref, y_ref, h_ref, bias, a_neg, d, *, nh, ng, need_y):
    qn = SSD_CHUNK
    hd = SSD_HEAD_DIM
    w = nh * hd
    gw = w // ng
    gn = ng * SSD_STATE
    ii = lax.broadcasted_iota(jnp.int32, (qn, qn), 0)
    jj = lax.broadcasted_iota(jnp.int32, (qn, qn), 1)
    mask = (jj <= ii) if d == 0 else (jj >= ii)
    lmat = mask.astype(F32)
    dtv = _softplus(dt_ref[...] + bias)
    cum = jnp.dot(lmat, dtv * a_neg, precision=lax.Precision.HIGHEST, preferred_element_type=F32)
    cum_t = cum.T
    edge = qn - 1 if d == 0 else 0
    tot = cum[edge:edge + 1, :]
    dt_t = dtv.T
    wt_t = dt_t * jnp.exp(cum_t[:, edge:edge + 1] - cum_t)
    decay = jnp.exp(tot)
    lane = lax.broadcasted_iota(jnp.int32, (qn, LANE), 1)
    left = lane < hd
    zero = jnp.zeros((), BF16)
    for g in range(ng):
        bg = xbc_ref[:, w + g * SSD_STATE:w + (g + 1) * SSD_STATE]
        cg = xbc_ref[:, w + gn + g * SSD_STATE:w + gn + (g + 1) * SSD_STATE]
        bg_t = bg.astype(F32).T
        s_in = h_ref[d, :, g * gw:(g + 1) * gw]
        if need_y:
            cb = lax.dot_general(cg, bg, (((1,), (1,)), ((), ())), preferred_element_type=F32)
            yoff = jnp.dot(cg, s_in.astype(BF16), preferred_element_type=F32)
        for pr in range(gw // LANE):
            c0 = d * nh + (g * gw) // hd + 2 * pr
            col = g * gw + pr * LANE
            xp = xbc_ref[:, col:col + LANE]
            r = jnp.concatenate([jnp.where(left, xp, zero), jnp.where(left, zero, xp)], axis=0)
            tops, bots, cols = [], [], []
            for c in (c0, c0 + 1):
                bots.append((bg_t * wt_t[c:c + 1, :]).astype(BF16))
                if need_y:
                    ccol = jnp.broadcast_to(cum[:, c:c + 1], (qn, qn))
                    seg = jnp.where(mask, jnp.exp(ccol - cum_t[c:c + 1, :]), 0.0)
                    tops.append((cb * seg * dt_t[c:c + 1, :]).astype(BF16))
                    cols.append(ccol)
            dec = jnp.where(left[0:1], decay[:, c0:c0 + 1], decay[:, c0 + 1:c0 + 2])
            s_old = s_in[:, pr * LANE:(pr + 1) * LANE]
            if need_y:
                lhs = jnp.concatenate([jnp.concatenate(tops, axis=1), jnp.concatenate(bots, axis=1)], axis=0)
                out = jnp.dot(lhs, r, preferred_element_type=F32)
                ec = jnp.exp(jnp.where(left, cols[0], cols[1]))
                y_ref[:, col:col + LANE] = (out[:qn] + yoff[:, pr * LANE:(pr + 1) * LANE] * ec).astype(y_ref.dtype)
                s_new = out[qn:]
            else:
                s_new = jnp.dot(jnp.concatenate(bots, axis=1), r, preferred_element_type=F32)
            h_ref[d, :, col:col + LANE] = s_old * dec + s_new


def _ssd_kernel(*refs, nh, ng, need_y):
    if need_y:
        xf_ref, xb_ref, dtf_ref, dtb_ref, h0_ref, bias_ref, alog_ref, yf_ref, yb_ref, h_ref = refs
    else:
        xf_ref, xb_ref, dtf_ref, dtb_ref, h0_ref, bias_ref, alog_ref, h_ref = refs
        yf_ref = yb_ref = None

    @pl.when(pl.program_id(1) == 0)
    def _():
        h_ref[...] = h0_ref[...]

    bias = bias_ref[...]
    a_neg = -jnp.exp(alog_ref[...])
    _ssd_dir(xf_ref, dtf_ref, yf_ref, h_ref, bias, a_neg, 0, nh=nh, ng=ng, need_y=need_y)
    _ssd_dir(xb_ref, dtb_ref, yb_ref, h_ref, bias, a_neg, 1, nh=nh, ng=ng, need_y=need_y)


def _ssd_call(xbc, dt, h0, bias, alog, nh, ng, need_y):
    b, l, cd = xbc.shape
    w = nh * SSD_HEAD_DIM
    nc = l // SSD_CHUNK
    q = SSD_CHUNK
    fwd = lambda i, s: (i, s, 0)
    bwd = lambda i, s: (i, nc - 1 - s, 0)
    hspec = pl.BlockSpec((None, 2, SSD_STATE, w), lambda i, s: (i, 0, 0, 0))
    in_specs = [pl.BlockSpec((None, q, cd), fwd), pl.BlockSpec((None, q, cd), bwd),
                pl.BlockSpec((None, q, LANE), fwd), pl.BlockSpec((None, q, LANE), bwd),
                hspec,
                pl.BlockSpec((1, LANE), lambda i, s: (0, 0)), pl.BlockSpec((1, LANE), lambda i, s: (0, 0))]
    out_specs = [hspec]
    out_shape = [jax.ShapeDtypeStruct((b, 2, SSD_STATE, w), F32)]
    if need_y:
        out_specs = [pl.BlockSpec((None, q, w), fwd), pl.BlockSpec((None, q, w), bwd)] + out_specs
        out_shape = [jax.ShapeDtypeStruct((b, l, w), BF16)] * 2 + out_shape
    return pl.pallas_call(
        functools.partial(_ssd_kernel, nh=nh, ng=ng, need_y=need_y), grid=(b, nc),
        in_specs=in_specs, out_specs=out_specs, out_shape=out_shape,
        compiler_params=_cp(("parallel", "arbitrary"), 48),
        name="ssd_scan" if need_y else "ssd_scan_ctx")(xbc, xbc, dt, dt, h0, bias, alog)


def _gnorm_kernel(yf_ref, yb_ref, xs_ref, z_ref, d_ref, g_ref, o_ref, t_ref, *, rows, tb):
    y = d_ref[...] * xs_ref[...].astype(F32) + yf_ref[...].astype(F32) + yb_ref[...].astype(F32)
    y = y * _silu(z_ref[...].astype(F32))
    ms = jnp.mean(y * y, axis=-1, keepdims=True)
    res = y * lax.rsqrt(ms + EPS) * g_ref[...]
    gw = res.shape[1]
    pitch = rows + XPOSE_PAD
    for wl in range(tb):
        for kk in range(gw // LANE):
            t_ref[kk, wl * pitch:wl * pitch + rows, :] = res[wl * rows:(wl + 1) * rows, kk * LANE:(kk + 1) * LANE]
    for r in range(rows):
        o_ref[r] = jnp.concatenate(
            [t_ref[kk, pl.ds(r, tb, stride=pitch), :] for kk in range(gw // LANE)], axis=1).astype(o_ref.dtype)


def _gnorm_call(yf, yb, xbc, proj, dvec, gvec, ng):
    b, l, w = yf.shape
    rows = l // GRID_W
    tb = XPOSE_TILE
    gw = w // ng
    blk = pl.BlockSpec((None, tb * rows, gw), lambda i, s, g: (i, s, g))
    vec = pl.BlockSpec((1, gw), lambda i, s, g: (0, g))
    out = pl.pallas_call(
        functools.partial(_gnorm_kernel, rows=rows, tb=tb), grid=(b, GRID_W // tb, ng),
        in_specs=[blk, blk, blk, blk, vec, vec],
        out_specs=pl.BlockSpec((None, rows, tb, gw), lambda i, s, g: (i, 0, s, g)),
        out_shape=jax.ShapeDtypeStruct((b, rows, GRID_W, w), BF16),
        scratch_shapes=[pltpu.VMEM((gw // LANE, tb * (rows + XPOSE_PAD), LANE), F32)],
        compiler_params=_cp(("parallel", "parallel", "parallel"), 40),
        name="ssd_gnorm")(yf, yb, xbc, proj, dvec, gvec)
    return out.reshape(b * l, w)


def _moe_kernel(e_ref, b0_ref, nb_ref, so_ref, *refs, nsub, nh):
    xs = refs[:nsub]
    wg_ref, wu_ref, wd_ref, o_ref, act_ref, wdb_ref = refs[nsub:]
    s = pl.program_id(0)
    t = pl.program_id(1)
    nb = nb_ref[s]

    @pl.when(jnp.logical_and(t < nh, nb > 0))
    def _():
        wg = wg_ref[...].astype(BF16)
        wu = wu_ref[...].astype(BF16)
        for k in range(nsub):
            @pl.when(k < nb)
            def _():
                x = xs[k][...]
                gt = jnp.dot(x, wg, preferred_element_type=F32)
                up = jnp.dot(x, wu, preferred_element_type=F32)
                act_ref[k, t] = (_silu(gt) * up).astype(BF16)

    @pl.when(jnp.logical_and(t >= nh, nb > 0))
    def _():
        wdb_ref[...] = wd_ref[...].astype(BF16)
        for k in range(nsub):
            rows = slice(k * MOE_BLOCK, (k + 1) * MOE_BLOCK)

            @pl.when(k < nb)
            def _():
                a = jnp.concatenate([act_ref[k, h] for h in range(nh)], axis=1)
                o_ref[rows, :] = jnp.dot(a, wdb_ref[...], preferred_element_type=F32).astype(o_ref.dtype)

            @pl.when(k >= nb)
            def _():
                o_ref[rows, :] = jnp.zeros((MOE_BLOCK, o_ref.shape[1]), o_ref.dtype)

    @pl.when(jnp.logical_and(t >= nh, nb == 0))
    def _():
        o_ref[...] = jnp.zeros(o_ref.shape, o_ref.dtype)


def _moe_call(sup_e, sup_b0, sup_nb, sup_out, xs, w_gate, w_up, w_down):
    n_slots, d = xs.shape
    hid = w_gate.shape[-1]
    th = min(256, hid)
    tn = min(1024, d)
    nh = hid // th
    nt = d // tn
    ns = sup_e.shape[0]
    rsup = MOE_SUB * MOE_BLOCK

    def x_map(k):
        def index(s, t, e, b0, nb, so):
            sx = jnp.where(t < nh, s, jnp.minimum(s + 1, ns - 1))
            return (b0[sx * MOE_SUB + k], 0)
        return index

    def hid_idx(s, t, nb):
        return jnp.where(nb[s] > 0, jnp.minimum(t, nh - 1), nh - 1)

    def col_idx(s, t, nb):
        return jnp.where(nb[s] > 0, jnp.clip(t - nh, 0, nt - 1), nt - 1)

    gs = pltpu.PrefetchScalarGridSpec(
        num_scalar_prefetch=4, grid=(ns, nh + nt),
        in_specs=[pl.BlockSpec((MOE_BLOCK, d), x_map(k)) for k in range(MOE_SUB)] + [
            pl.BlockSpec((None, d, th), lambda s, t, e, b0, nb, so: (e[s], 0, hid_idx(s, t, nb))),
            pl.BlockSpec((None, d, th), lambda s, t, e, b0, nb, so: (e[s], 0, hid_idx(s, t, nb))),
            pl.BlockSpec((None, hid, tn), lambda s, t, e, b0, nb, so: (e[s], 0, col_idx(s, t, nb)))],
        out_specs=pl.BlockSpec((rsup, tn), lambda s, t, e, b0, nb, so: (so[s], jnp.clip(t - nh, 0, nt - 1))),
        scratch_shapes=[pltpu.VMEM((MOE_SUB, nh, MOE_BLOCK, th), BF16), pltpu.VMEM((hid, tn), BF16)])
    return pl.pallas_call(
        functools.partial(_moe_kernel, nsub=MOE_SUB, nh=nh), grid_spec=gs,
        out_shape=jax.ShapeDtypeStruct((ns * rsup, d), BF16),
        compiler_params=_cp(("arbitrary", "arbitrary"), 60), name="moe_experts")(
            sup_e, sup_b0, sup_nb, sup_out, *([xs] * MOE_SUB), w_gate, w_up, w_down)


def _final_kernel(x_ref, ya_ref, yb_ref, w_ref, g2_ref, fg_ref, o_ref):
    wts = w_ref[...]
    moe = ya_ref[...].astype(F32) * wts[:, 0:1] + yb_ref[...].astype(F32) * wts[:, 1:2]
    x = x_ref[...] + g2_ref[...] * moe
    ms = jnp.mean(x * x, axis=-1, keepdims=True)
    o_ref[...] = x * lax.rsqrt(ms + EPS) * fg_ref[...]


def _final_call(x, ya, yb, wts, mods, ig, fg):
    b, l, d = x.shape
    tr = min(256, l)
    nt = l // tr
    row = lambda i, r: (i * nt + r, 0)
    return pl.pallas_call(
        _final_kernel, grid=(b, nt),
        in_specs=[pl.BlockSpec((None, tr, d), lambda i, r: (i, r, 0)),
                  pl.BlockSpec((tr, d), row), pl.BlockSpec((tr, d), row),
                  pl.BlockSpec((tr, MOE_TOP_K), row),
                  pl.BlockSpec((None, None, 1, d), lambda i, r: (i, ig, 0, 0)),
                  pl.BlockSpec((1, d), lambda i, r: (0, 0))],
        out_specs=pl.BlockSpec((None, tr, d), lambda i, r: (i, r, 0)),
        out_shape=jax.ShapeDtypeStruct((b, l, d), F32),
        compiler_params=_cp(("parallel", "parallel"), 40), name="combine_final")(x, ya, yb, wts, mods, fg)


def _route(sel, ne):
    m = sel.shape[0]
    weights = sel[:, MOE_TOP_K:2 * MOE_TOP_K]
    n_assign = m * MOE_TOP_K
    flat_e = jnp.concatenate([sel[:, k] for k in range(MOE_TOP_K)], axis=0).astype(jnp.int32)
    onehot = (flat_e[:, None] == jnp.arange(ne, dtype=flat_e.dtype)[None, :]).astype(jnp.int32)
    csum = jnp.cumsum(onehot, axis=0)
    counts = csum[-1]
    local = jnp.sum(csum * onehot, axis=1) - 1
    nblk_e = (counts + MOE_BLOCK - 1) // MOE_BLOCK
    padded = nblk_e * MOE_BLOCK
    pad_end = jnp.cumsum(padded)
    pad_start = pad_end - padded
    dest = pad_start[flat_e] + local
    n_blocks = -(-(n_assign + ne * (MOE_BLOCK - 1)) // MOE_BLOCK)
    tok = jnp.arange(n_assign, dtype=jnp.int32) % m
    n_slots = n_blocks * MOE_BLOCK
    slot_tok = (jnp.arange(n_slots, dtype=jnp.int32) % m).at[dest].set(tok)
    rsup = MOE_SUB * MOE_BLOCK
    ns_e = (nblk_e + MOE_SUB - 1) // MOE_SUB
    sup_end = jnp.cumsum(ns_e)
    sup_start = sup_end - ns_e
    n_sup = (n_blocks + (MOE_SUB - 1) * ne) // MOE_SUB
    sidx = jnp.arange(n_sup, dtype=jnp.int32)
    last = sup_end[-1] - 1
    s_eff = jnp.minimum(sidx, last)
    e_s = jnp.minimum(jnp.searchsorted(sup_end, s_eff, side='right'), ne - 1).astype(jnp.int32)
    k_s = s_eff - sup_start[e_s]
    b0_s = pad_start[e_s] // MOE_BLOCK + MOE_SUB * k_s
    nb_s = jnp.clip(nblk_e[e_s] - MOE_SUB * k_s, 0, MOE_SUB)
    used = sidx <= last
    sup_nb = jnp.where(used, nb_s, 0).astype(jnp.int32)
    kk = jnp.arange(MOE_SUB, dtype=jnp.int32)[None, :]
    sup_b0 = jnp.maximum(lax.cummax(jnp.where(kk < sup_nb[:, None], b0_s[:, None] + kk, -1), axis=0), 0)
    sup_b0 = sup_b0.reshape(-1).astype(jnp.int32)
    pos = ((sup_start[flat_e] + local // rsup) * rsup + local % rsup).astype(jnp.int32)
    return weights, slot_tok, e_s, sup_b0, sup_nb, sidx, pos


def kernel(x, c, ctx, c_ctx, w_mod, b_mod, norm1_g, w_in, s5_lam_re, s5_lam_im, s5_log_dt, s5_b_re, s5_b_im, s5_c_re, s5_c_im, s5_d, s5_w_val, s5_w_gate, ssd_conv_w, ssd_conv_b, ssd_a_log, ssd_dt_bias, ssd_d, ssd_norm_g, ssd_w_out, w_o, norm2_g, moe_w_group, moe_b_group, moe_w_expert, moe_b_expert, moe_w_gate, moe_w_up, moe_w_down, final_g):
    depth = w_mod.shape[0]
    assert depth == 1, "single-layer block"
    bsz, n_lat, d = x.shape
    l_ctx = ctx.shape[1]
    w5 = s5_d.shape[1]
    nh = ssd_d.shape[1]
    w = nh * SSD_HEAD_DIM
    conv_dim = ssd_conv_w.shape[2]
    ng = (conv_dim - w) // (2 * SSD_STATE)
    ssd_in = w + conv_dim + 2 * nh
    o1, o2 = w5, w5 + ssd_in
    l = 0

    cc = jnp.concatenate([c, c_ctx[None, :]], axis=0)
    cc = jnp.pad(cc, ((0, (-cc.shape[0]) % 8), (0, 0)))
    mods = _mod_call(cc, w_mod[l], b_mod[l]).reshape(cc.shape[0], 6, 1, d)
    i_sh1, i_sc1, i_g1, i_sh2, i_sc2, i_g2 = range(6)

    w_in_l = w_in[l]
    o_dt = o1 + w + conv_dim

    hn_rm, hn_cm = _norm_lat_call(x, norm1_g[l], mods, i_sc1, i_sh1)
    hc = _norm_ctx_call(ctx, norm1_g[l], mods, bsz, i_sc1, i_sh1)
    hn_rm = hn_rm.reshape(bsz * n_lat, d)
    hn_cm = hn_cm.reshape(bsz * n_lat, d)
    hc = hc.reshape(bsz * l_ctx, d)

    u_lat = _mm_call(hn_rm, w_in_l, BF16, "in_s5", 0, o1).reshape(bsz, n_lat, w5)
    u_ctx = _mm_call(hc, w_in_l, BF16, "in_s5_ctx", 0, o1).reshape(bsz, l_ctx, w5)
    gates = _mm_call(hn_rm, w_in_l, BF16, "in_gates", o2, 2 * d)
    p_lat = _mm_call(hn_cm, w_in_l, BF16, "in_ssd", o1, w + conv_dim).reshape(bsz, n_lat, w + conv_dim)
    p_ctx = _mm_call(hc, w_in_l, BF16, "in_ssd_ctx", o1, w + conv_dim).reshape(bsz, l_ctx, w + conv_dim)
    dt_lat = _mm_call(hn_cm, w_in_l, F32, "in_dt", o_dt, 2 * nh).reshape(bsz, n_lat, LANE)
    dt_ctx = _mm_call(hc, w_in_l, F32, "in_dt_ctx", o_dt, 2 * nh).reshape(bsz, l_ctx, LANE)

    nj = w5 // LANE
    bmat, cmat, dmat, lre, lim = _s5_params(s5_lam_re[l], s5_lam_im[l], s5_log_dt[l], s5_b_re[l], s5_b_im[l],
                                            s5_c_re[l], s5_c_im[l], bsz)
    s5_zero = jnp.zeros((nj, 2, 2 * bsz, lre.shape[-1]), F32)
    (s5_ctx,) = _s5_call(u_ctx, bmat, cmat, dmat, lre, lim, s5_zero, False)
    ya_f, ya_b, _ = _s5_call(u_lat, bmat, cmat, dmat, lre, lim, s5_ctx, True)

    xbc_ctx = _conv_call(p_ctx, ssd_conv_w[l], ssd_conv_b[l], w)
    xbc_lat = _conv_call(p_lat, ssd_conv_w[l], ssd_conv_b[l], w)
    pad_h = LANE - 2 * nh
    bias = jnp.pad(ssd_dt_bias[l].astype(F32).reshape(1, 2 * nh), ((0, 0), (0, pad_h)))
    alog = jnp.pad(ssd_a_log[l].astype(F32).reshape(1, 2 * nh), ((0, 0), (0, pad_h)))
    h_zero = jnp.zeros((bsz, 2, SSD_STATE, w), F32)
    (h_ctx,) = _ssd_call(xbc_ctx, dt_ctx, h_zero, bias, alog, nh, ng, False)
    y_f, y_b, _ = _ssd_call(xbc_lat, dt_lat, h_ctx, bias, alog, nh, ng, True)
    d_vec = jnp.repeat(ssd_d[l].astype(F32), SSD_HEAD_DIM).reshape(1, w)
    y_ssd = _gnorm_call(y_f, y_b, xbc_lat, p_lat, d_vec, ssd_norm_g[l].astype(F32).reshape(1, w), ng)

    m_lat = bsz * n_lat
    part_a = _glu_call(u_lat.reshape(m_lat, w5), ya_f.reshape(m_lat, w5), ya_b.reshape(m_lat, w5),
                       s5_d[l].astype(F32).reshape(1, w5),
                       s5_w_val[l].astype(BF16), s5_w_gate[l].astype(BF16), gates)
    merged = _merge_call(y_ssd, ssd_w_out[l], gates, part_a)
    x1 = _resid_call(merged, w_o[l], x, mods, i_g1)

    ngr = moe_w_group.shape[-1]
    ne = moe_w_expert.shape[-1]
    wr = jnp.concatenate([moe_w_group[l], moe_w_expert[l]], axis=1).astype(F32)
    wr = jnp.pad(wr, ((0, 0), (0, (-(ngr + ne)) % LANE)))
    br = jnp.concatenate([moe_b_group[l], moe_b_expert[l]]).astype(F32)
    br = jnp.pad(br, (0, (-(ngr + ne)) % LANE)).reshape(1, -1)
    hx, sel = _norm_router_call(x1, norm2_g[l], mods, i_sc2, i_sh2, wr, br, ngr, ne)
    m = bsz * n_lat
    hx = hx.reshape(m, d)
    weights, slot_tok, sup_e, sup_b0, sup_nb, sup_out, pos = _route(sel.reshape(m, -1), ne)
    xs = hx[slot_tok]
    y_slots = _moe_call(sup_e, sup_b0, sup_nb, sup_out, xs, moe_w_gate[l], moe_w_up[l], moe_w_down[l])
    ya = y_slots[pos[:m]]
    yb = y_slots[pos[m:]]
    return _final_call(x1, ya, yb, weights.astype(F32), mods, i_g2, final_g.reshape(1, d))
```

```python
import functools
import math

import jax
import jax.numpy as jnp
from jax import lax
from jax.experimental import pallas as pl
from jax.experimental.pallas import tpu as pltpu

F32 = jnp.float32
BF16 = jnp.bfloat16

GRID_W = 64
EPS = 1e-6
LANE = 128
SSD_HEAD_DIM = 64
SSD_STATE = 128
SSD_CHUNK = 128
SSD_CONV = 5
MOE_TOP_K = 2
MOE_BLOCK = 256
MOE_SUB = 4
XPOSE_TILE = 16
XPOSE_PAD = 8
S5_CHUNK = 256
S5_ROW_PAD = 8


def _cp(sem, mb):
    return pltpu.CompilerParams(dimension_semantics=sem, vmem_limit_bytes=mb * 1024 * 1024)


def _sigmoid(x):
    return 1.0 / (1.0 + jnp.exp(-x))


def _silu(x):
    return x * _sigmoid(x)


def _mod_kernel(c_ref, w_ref, b_ref, o_ref):
    s = _silu(c_ref[...])
    o_ref[...] = jnp.dot(s.astype(BF16), w_ref[...].astype(BF16),
                         preferred_element_type=F32) + b_ref[...]


def _mod_call(cc, w, b):
    r, d = cc.shape
    n = w.shape[1]
    tn = min(512, n)
    return pl.pallas_call(
        _mod_kernel, grid=(n // tn,),
        in_specs=[pl.BlockSpec((r, d), lambda j: (0, 0)),
                  pl.BlockSpec((d, tn), lambda j: (0, j)),
                  pl.BlockSpec((1, tn), lambda j: (0, j))],
        out_specs=pl.BlockSpec((r, tn), lambda j: (0, j)),
        out_shape=jax.ShapeDtypeStruct((r, n), F32),
        compiler_params=_cp(("parallel",), 40), name="mod")(cc, w, b.reshape(1, n))


def _rmsmod(x, g, sc, sh):
    ms = jnp.mean(x * x, axis=-1, keepdims=True)
    return (x * lax.rsqrt(ms + EPS) * g) * (1.0 + sc) + sh


def _norm_lat_kernel(x_ref, g_ref, sc_ref, sh_ref, orm_ref, ocm_ref, t_ref, *, tb, d):
    pitch = tb + XPOSE_PAD
    for r in range(tb):
        y = _rmsmod(x_ref[r], g_ref[...], sc_ref[...], sh_ref[...])
        orm_ref[r] = y.astype(BF16)
        for kk in range(d // LANE):
            t_ref[kk, r * pitch:r * pitch + tb, :] = y[:, kk * LANE:(kk + 1) * LANE]
    for wl in range(tb):
        ocm_ref[wl] = jnp.concatenate(
            [t_ref[kk, pl.ds(wl, tb, stride=pitch), :] for kk in range(d // LANE)], axis=1).astype(BF16)


def _norm_lat_call(x, g, mods, isc, ish):
    b, l, d = x.shape
    rows = l // GRID_W
    tb = XPOSE_TILE
    xv = x.reshape(b, rows, GRID_W, d)
    orm, ocm = pl.pallas_call(
        functools.partial(_norm_lat_kernel, tb=tb, d=d), grid=(b, rows // tb, GRID_W // tb),
        in_specs=[pl.BlockSpec((None, tb, tb, d), lambda i, r, c: (i, r, c, 0)),
                  pl.BlockSpec((1, d), lambda i, r, c: (0, 0)),
                  pl.BlockSpec((None, None, 1, d), lambda i, r, c: (i, isc, 0, 0)),
                  pl.BlockSpec((None, None, 1, d), lambda i, r, c: (i, ish, 0, 0))],
        out_specs=[pl.BlockSpec((None, tb, tb, d), lambda i, r, c: (i, r, c, 0)),
                   pl.BlockSpec((None, tb, tb, d), lambda i, r, c: (i, c, r, 0))],
        out_shape=[jax.ShapeDtypeStruct((b, rows, GRID_W, d), BF16),
                   jax.ShapeDtypeStruct((b, GRID_W, rows, d), BF16)],
        scratch_shapes=[pltpu.VMEM((d // LANE, tb * (tb + XPOSE_PAD), LANE), F32)],
        compiler_params=_cp(("parallel", "parallel", "parallel"), 48),
        name="norm1_lat")(xv, g.reshape(1, d), mods, mods)
    return orm.reshape(b, l, d), ocm.reshape(b, l, d)


def _norm_kernel(x_ref, g_ref, sc_ref, sh_ref, o_ref):
    o_ref[...] = _rmsmod(x_ref[...], g_ref[...], sc_ref[...], sh_ref[...]).astype(o_ref.dtype)


def _norm_ctx_call(x, g, mods, row, isc, ish):
    b, l, d = x.shape
    tr = min(256, l)
    return pl.pallas_call(
        _norm_kernel, grid=(b, l // tr),
        in_specs=[pl.BlockSpec((None, tr, d), lambda i, r: (i, r, 0)),
                  pl.BlockSpec((1, d), lambda i, r: (0, 0)),
                  pl.BlockSpec((None, None, 1, d), lambda i, r: (row, isc, 0, 0)),
                  pl.BlockSpec((None, None, 1, d), lambda i, r: (row, ish, 0, 0))],
        out_specs=pl.BlockSpec((None, tr, d), lambda i, r: (i, r, 0)),
        out_shape=jax.ShapeDtypeStruct((b, l, d), BF16),
        compiler_params=_cp(("parallel", "parallel"), 40), name="norm1_ctx")(x, g.reshape(1, d), mods, mods)


def _first_argmax(v, vmax, lane):
    return jnp.min(jnp.where(v == vmax, lane, float(LANE)), axis=-1, keepdims=True)


def _norm_router_kernel(x_ref, g_ref, sc_ref, sh_ref, wr_ref, br_ref, o_ref, sel_ref, *, ngr, ne):
    y = _rmsmod(x_ref[...], g_ref[...], sc_ref[...], sh_ref[...])
    o_ref[...] = y.astype(BF16)
    y_hi = y.astype(BF16)
    y_lo = (y - y_hi.astype(F32)).astype(BF16)
    w_hi = wr_ref[0]
    lg = (jnp.dot(y_hi, w_hi, preferred_element_type=F32) + jnp.dot(y_hi, wr_ref[1], preferred_element_type=F32)
          + jnp.dot(y_lo, w_hi, preferred_element_type=F32)) + br_ref[...]
    epg = ne // ngr
    lane = lax.broadcasted_iota(jnp.int32, lg.shape, 1).astype(F32)
    ninf = -jnp.inf
    gl = jnp.where(lane < ngr, lg, ninf)
    ge = jnp.exp(gl - jnp.max(gl, axis=-1, keepdims=True))
    g_prob = ge / jnp.sum(ge, axis=-1, keepdims=True)
    g_p = jnp.max(g_prob, axis=-1, keepdims=True)
    lo = ngr + epg * _first_argmax(g_prob, g_p, lane)
    cand = jnp.where(jnp.logical_and(lane >= lo, lane < lo + epg), lg, ninf)
    v1 = jnp.max(cand, axis=-1, keepdims=True)
    l1 = _first_argmax(cand, v1, lane)
    rest = jnp.where(lane == l1, ninf, cand)
    v2 = jnp.max(rest, axis=-1, keepdims=True)
    l2 = _first_argmax(rest, v2, lane)
    ex = jnp.exp(v2 - v1)
    w1 = g_p * (1.0 / (1.0 + ex))
    w2 = g_p * (ex / (1.0 + ex))
    sel_ref[...] = jnp.where(lane == 0, l1 - ngr, jnp.where(lane == 1, l2 - ngr,
                             jnp.where(lane == 2, w1, jnp.where(lane == 3, w2, 0.0))))


def _norm_router_call(x, g, mods, isc, ish, wr, br, ngr, ne):
    b, l, d = x.shape
    tr = min(256, l)
    nr = wr.shape[1]
    w_hi = wr.astype(BF16)
    wr = jnp.stack([w_hi, (wr - w_hi.astype(F32)).astype(BF16)], axis=0)
    return pl.pallas_call(
        functools.partial(_norm_router_kernel, ngr=ngr, ne=ne), grid=(b, l // tr),
        in_specs=[pl.BlockSpec((None, tr, d), lambda i, r: (i, r, 0)),
                  pl.BlockSpec((1, d), lambda i, r: (0, 0)),
                  pl.BlockSpec((None, None, 1, d), lambda i, r: (i, isc, 0, 0)),
                  pl.BlockSpec((None, None, 1, d), lambda i, r: (i, ish, 0, 0)),
                  pl.BlockSpec((2, d, nr), lambda i, r: (0, 0, 0)),
                  pl.BlockSpec((1, nr), lambda i, r: (0, 0))],
        out_specs=[pl.BlockSpec((None, tr, d), lambda i, r: (i, r, 0)),
                   pl.BlockSpec((None, tr, nr), lambda i, r: (i, r, 0))],
        out_shape=[jax.ShapeDtypeStruct((b, l, d), BF16), jax.ShapeDtypeStruct((b, l, nr), F32)],
        compiler_params=_cp(("parallel", "parallel"), 40),
        name="norm2_router")(x, g.reshape(1, d), mods, mods, wr, br)


def _mm_kernel(a_ref, b_ref, o_ref):
    o_ref[...] = jnp.dot(a_ref[...], b_ref[...].astype(BF16), preferred_element_type=F32).astype(o_ref.dtype)


def _mm_tiles(m, n):
    tm = min(1024, m)
    tn = min(512, n)
    return tm, tn


def _mm_call(a, b, out_dtype, name, col0=0, n=None):
    m, k = a.shape
    n = b.shape[1] - col0 if n is None else n
    tm, tn = _mm_tiles(m, n)
    if col0 % LANE or n % tn or n % LANE:
        b = b[:, col0:col0 + n]
        pad = (-n) % LANE
        b = jnp.pad(b, ((0, 0), (0, pad)))
        n, col0 = n + pad, 0
        tm, tn = _mm_tiles(m, n)
    return pl.pallas_call(
        _mm_kernel, grid=(m // tm, n // tn),
        in_specs=[pl.BlockSpec((tm, k), lambda i, j: (i, 0)),
                  pl.BlockSpec((pl.Element(k), pl.Element(tn)), lambda i, j: (0, (col0 // LANE + j * (tn // LANE)) * LANE))],
        out_specs=pl.BlockSpec((tm, tn), lambda i, j: (i, j)),
        out_shape=jax.ShapeDtypeStruct((m, n), out_dtype),
        compiler_params=_cp(("parallel", "parallel"), 48), name=name)(a, b)


def _gelu_tanh(x):
    return x * (0.5 * (1.0 + jnp.tanh(math.sqrt(2.0 / math.pi) * (x + 0.044715 * (x * x * x)))))


def _glu_kernel(u_ref, yf_ref, yb_ref, d_ref, wv_ref, wg_ref, gate_ref, o_ref, a_ref):
    @pl.when(pl.program_id(1) == 0)
    def _():
        y = d_ref[...] * u_ref[...].astype(F32) + yf_ref[...].astype(F32) + yb_ref[...].astype(F32)
        a_ref[...] = _gelu_tanh(y).astype(BF16)

    a = a_ref[...]
    val = jnp.dot(a, wv_ref[...], preferred_element_type=F32)
    gl = jnp.dot(a, wg_ref[...], preferred_element_type=F32)
    o_ref[...] = (_sigmoid(gate_ref[...].astype(F32)) * (val * _sigmoid(gl))).astype(o_ref.dtype)


def _glu_call(u, yf, yb, dvec, wv, wg, gates):
    m, k = u.shape
    n = wv.shape[1]
    tm, tn = _mm_tiles(m, n)
    row = pl.BlockSpec((tm, k), lambda i, j: (i, 0))
    return pl.pallas_call(
        _glu_kernel, grid=(m // tm, n // tn),
        in_specs=[row, row, row,
                  pl.BlockSpec((1, k), lambda i, j: (0, 0)),
                  pl.BlockSpec((k, tn), lambda i, j: (0, j)),
                  pl.BlockSpec((k, tn), lambda i, j: (0, j)),
                  pl.BlockSpec((tm, tn), lambda i, j: (i, j))],
        out_specs=pl.BlockSpec((tm, tn), lambda i, j: (i, j)),
        out_shape=jax.ShapeDtypeStruct((m, n), BF16),
        scratch_shapes=[pltpu.VMEM((tm, k), BF16)],
        compiler_params=_cp(("parallel", "arbitrary"), 56), name="s5_glu")(u, yf, yb, dvec, wv, wg, gates)


def _merge_kernel(a_ref, w_ref, gate_ref, pa_ref, o_ref):
    br = jnp.dot(a_ref[...], w_ref[...].astype(BF16), preferred_element_type=F32)
    o_ref[...] = (pa_ref[...].astype(F32) + _sigmoid(gate_ref[...].astype(F32)) * br).astype(o_ref.dtype)


def _merge_call(a, w, gates, part_a):
    m, k = a.shape
    n = w.shape[1]
    tm, tn = _mm_tiles(m, n)
    off = n // tn
    return pl.pallas_call(
        _merge_kernel, grid=(m // tm, n // tn),
        in_specs=[pl.BlockSpec((tm, k), lambda i, j: (i, 0)),
                  pl.BlockSpec((k, tn), lambda i, j: (0, j)),
                  pl.BlockSpec((tm, tn), lambda i, j: (i, j + off)),
                  pl.BlockSpec((tm, tn), lambda i, j: (i, j))],
        out_specs=pl.BlockSpec((tm, tn), lambda i, j: (i, j)),
        out_shape=jax.ShapeDtypeStruct((m, n), BF16),
        compiler_params=_cp(("parallel", "parallel"), 48), name="ssd_out_merge")(a, w, gates, part_a)


def _resid_kernel(a_ref, w_ref, x_ref, g_ref, o_ref):
    mix = jnp.dot(a_ref[...], w_ref[...].astype(BF16), preferred_element_type=F32)
    o_ref[...] = x_ref[...] + g_ref[...] * mix


def _resid_call(a, w, x, mods, ig):
    b, l, d = x.shape
    k = a.shape[1]
    tm, tn = _mm_tiles(l, d)
    nt = l // tm
    return pl.pallas_call(
        _resid_kernel, grid=(b * nt, d // tn),
        in_specs=[pl.BlockSpec((tm, k), lambda i, j: (i, 0)),
                  pl.BlockSpec((k, tn), lambda i, j: (0, j)),
                  pl.BlockSpec((None, tm, tn), lambda i, j: (i // nt, i % nt, j)),
                  pl.BlockSpec((None, None, 1, tn), lambda i, j: (i // nt, ig, 0, j))],
        out_specs=pl.BlockSpec((None, tm, tn), lambda i, j: (i // nt, i % nt, j)),
        out_shape=jax.ShapeDtypeStruct((b, l, d), F32),
        compiler_params=_cp(("parallel", "parallel"), 48), name="w_o_resid")(a, w, x, mods)


def _s5_kernel(*refs, nb, tp, p8, need_y):
    if need_y:
        uf_ref, ub_ref, b_ref, c_ref, d_ref, lre_ref, lim_ref, h0_ref, yf_ref, yb_ref, h_ref, buf_ref, il_ref = refs
    else:
        uf_ref, ub_ref, b_ref, c_ref, d_ref, lre_ref, lim_ref, h0_ref, h_ref, buf_ref, il_ref = refs
    q = 2 * nb
    nk = p8 // LANE
    pitch = tp + S5_ROW_PAD

    @pl.when(pl.program_id(1) == 0)
    def _():
        h_ref[...] = h0_ref[...]

    rev = (lax.broadcasted_iota(jnp.int32, (tp, tp), 0) + lax.broadcasted_iota(jnp.int32, (tp, tp), 1)
           == tp - 1).astype(BF16)
    def pair_rows(u_ref, b, newer_first):
        il_ref[...] = u_ref[b].astype(F32)
        even = il_ref[pl.ds(0, tp, stride=2), :].astype(BF16)
        odd = il_ref[pl.ds(1, tp, stride=2), :].astype(BF16)
        return jnp.concatenate([odd, even] if newer_first else [even, odd], axis=1)

    lhs = []
    for d in range(2):
        if d == 0:
            u = jnp.concatenate([pair_rows(uf_ref, b, False) for b in range(nb)], axis=0)
        else:
            u = jnp.concatenate(
                [jnp.dot(rev, pair_rows(ub_ref, b, True), preferred_element_type=F32).astype(BF16)
                 for b in range(nb)], axis=0)
        lhs.append(u)
        bu = jnp.dot(u, b_ref[d], preferred_element_type=F32)
        for b in range(nb):
            r0 = (d * nb + b) * pitch
            for k in range(2 * nk):
                buf_ref[k, r0:r0 + tp, :] = bu[b * tp:(b + 1) * tp, k * LANE:(k + 1) * LANE]
    ar = [lre_ref[:, k * LANE:(k + 1) * LANE] for k in range(nk)]
    ai = [lim_ref[:, k * LANE:(k + 1) * LANE] for k in range(nk)]

    def step(s, carry):
        rows = pl.ds(s, q, stride=pitch)
        out = []
        for k in range(nk):
            hr, hi = carry[k]
            nr = ar[k] * hr - ai[k] * hi + buf_ref[k, rows, :]
            ni = ar[k] * hi + ai[k] * hr + buf_ref[nk + k, rows, :]
            buf_ref[k, rows, :] = hr
            buf_ref[nk + k, rows, :] = hi
            out.append((nr, ni))
        return tuple(out)

    init = tuple((h_ref[0, :, k * LANE:(k + 1) * LANE], h_ref[1, :, k * LANE:(k + 1) * LANE]) for k in range(nk))
    fin = lax.fori_loop(0, tp, step, init, unroll=8)
    for k in range(nk):
        h_ref[0, :, k * LANE:(k + 1) * LANE] = fin[k][0]
        h_ref[1, :, k * LANE:(k + 1) * LANE] = fin[k][1]

    if need_y:
        for d in range(2):
            h = jnp.concatenate(
                [jnp.concatenate([buf_ref[k, (d * nb + b) * pitch:(d * nb + b) * pitch + tp, :].astype(BF16)
                                  for k in range(2 * nk)], axis=1) for b in range(nb)], axis=0)
            y = (jnp.dot(h, c_ref[d], preferred_element_type=F32)
                 + jnp.dot(lhs[d], d_ref[d], preferred_element_type=F32))
            for b in range(nb):
                yb = y[b * tp:(b + 1) * tp]
                if d == 1:
                    yb = jnp.dot(rev, yb.astype(BF16), preferred_element_type=F32)
                first, second = (0, 1) if d == 0 else (1, 0)
                il_ref[pl.ds(first, tp, stride=2), :] = yb[:, 0:LANE]
                il_ref[pl.ds(second, tp, stride=2), :] = yb[:, LANE:2 * LANE]
                (yf_ref if d == 0 else yb_ref)[b] = il_ref[...].astype(BF16)


def _s5_call(u, bmat, cmat, dmat, lre, lim, h0, need_y):
    nb, l, w5 = u.shape
    nj = w5 // LANE
    q = 2 * nb
    t = min(S5_CHUNK, l)
    tp = t // 2
    p8 = lre.shape[-1]
    nc = l // t
    hspec = pl.BlockSpec((None, 2, q, p8), lambda j, c: (j, 0, 0, 0))
    ublk = (nb, t, LANE)
    in_specs = [pl.BlockSpec(ublk, lambda j, c: (0, c, j)), pl.BlockSpec(ublk, lambda j, c: (0, nc - 1 - c, j)),
                pl.BlockSpec((None, 2, 2 * LANE, 2 * p8), lambda j, c: (j, 0, 0, 0)),
                pl.BlockSpec((None, 2, 2 * p8, 2 * LANE), lambda j, c: (j, 0, 0, 0)),
                pl.BlockSpec((None, 2, 2 * LANE, 2 * LANE), lambda j, c: (j, 0, 0, 0)),
                pl.BlockSpec((None, q, p8), lambda j, c: (j, 0, 0)),
                pl.BlockSpec((None, q, p8), lambda j, c: (j, 0, 0)),
                hspec]
    out_specs = [hspec]
    out_shape = [jax.ShapeDtypeStruct((nj, 2, q, p8), F32)]
    scratch = [pltpu.VMEM((2 * p8 // LANE, q * (tp + S5_ROW_PAD), LANE), F32), pltpu.VMEM((t, LANE), F32)]
    if need_y:
        out_specs = [pl.BlockSpec(ublk, lambda j, c: (0, c, j)),
                     pl.BlockSpec(ublk, lambda j, c: (0, nc - 1 - c, j))] + out_specs
        out_shape = [jax.ShapeDtypeStruct((nb, l, w5), BF16)] * 2 + out_shape
    kern = functools.partial(_s5_kernel, nb=nb, tp=tp, p8=p8, need_y=need_y)
    return pl.pallas_call(
        kern, grid=(nj, nc), in_specs=in_specs, out_specs=out_specs, out_shape=out_shape,
        scratch_shapes=scratch, compiler_params=_cp(("parallel", "arbitrary"), 48),
        name="s5_scan" if need_y else "s5_scan_ctx")(u, u, bmat, cmat, dmat, lre, lim, h0)


def _s5_params(lam_re, lam_im, log_dt, b_re, b_im, c_re, c_im, nb):
    _, g, p = lam_re.shape
    s = b_re.shape[-1]
    gpb = LANE // s
    nj = g // gpb
    lam = lax.complex(lam_re.astype(F32), lam_im.astype(F32))
    lam_bar = jnp.exp(lam * jnp.exp(log_dt.astype(F32))[..., None])
    b_bar = ((lam_bar - 1.0) / lam)[..., None] * lax.complex(b_re.astype(F32), b_im.astype(F32))
    def quadrants(parts):
        r, c = parts[0].shape[-2:]
        same_group = (jnp.arange(gpb * r)[:, None] // r) == (jnp.arange(gpb * c)[None, :] // c)

        def spread(x):
            return jnp.where(same_group, jnp.tile(x.reshape(2, nj, gpb * r, c), (1, 1, 1, gpb)), 0.0)

        top = jnp.concatenate([spread(parts[0]), spread(parts[1])], axis=-1)
        bot = jnp.concatenate([spread(parts[2]), spread(parts[3])], axis=-1)
        return jnp.concatenate([top, bot], axis=-2).transpose(1, 0, 2, 3).astype(BF16)

    tb = lambda z: jnp.swapaxes(z, -1, -2)
    lb = lam_bar[..., None] * b_bar
    bmat = quadrants([tb(lb.real), tb(lb.imag), tb(b_bar.real), tb(b_bar.imag)])
    cc = lax.complex(c_re.astype(F32), c_im.astype(F32))
    c_l1 = cc * lam_bar[:, :, None, :]
    c_l2 = c_l1 * lam_bar[:, :, None, :]
    cmat = quadrants([tb(c_l1.real), tb(c_l2.real), -tb(c_l1.imag), -tb(c_l2.imag)])
    m0 = jnp.einsum('dgsp,dgpt->dgts', cc, b_bar).real
    m1 = jnp.einsum('dgsp,dgpt->dgts', c_l1, b_bar).real
    dmat = quadrants([m0, m1, jnp.zeros_like(m0), m0])

    def lam_of(part):
        v = part.reshape(2, nj, gpb * p).transpose(1, 0, 2)
        return jnp.repeat(v, nb, axis=1)

    lam2 = lam_bar * lam_bar
    return bmat, cmat, dmat, lam_of(lam2.real), lam_of(lam2.imag)


def _conv_kernel(x_ref, w_ref, b_ref, o_ref, *, l):
    x = x_ref[...].astype(F32)
    rows = lax.broadcasted_iota(jnp.int32, x.shape, 0)
    half = SSD_CONV // 2
    acc = x * w_ref[half:half + 1, :] + b_ref[...]
    for k in range(SSD_CONV):
        if k == half:
            continue
        off = k - half
        xs = pltpu.roll(x, shift=(-off) % l, axis=0)
        valid = jnp.logical_and(rows + off >= 0, rows + off < l)
        acc = acc + jnp.where(valid, xs, 0.0) * w_ref[k:k + 1, :]
    o_ref[...] = _silu(acc).astype(o_ref.dtype)


def _conv_call(proj, w, bias, col0):
    b, l, _ = proj.shape
    c = w.shape[1]
    tc = 256
    off = col0 // tc
    return pl.pallas_call(
        functools.partial(_conv_kernel, l=l), grid=(b, c // tc),
        in_specs=[pl.BlockSpec((None, l, tc), lambda i, j: (i, 0, j + off)),
                  pl.BlockSpec((SSD_CONV, tc), lambda i, j: (0, j)),
                  pl.BlockSpec((1, tc), lambda i, j: (0, j))],
        out_specs=pl.BlockSpec((None, l, tc), lambda i, j: (i, 0, j)),
        out_shape=jax.ShapeDtypeStruct((b, l, c), BF16),
        compiler_params=_cp(("parallel", "parallel"), 40), name="ssd_conv")(proj, w, bias.reshape(1, c))


def _softplus(x):
    return jnp.maximum(x, 0.0) + jnp.log1p(jnp.exp(-jnp.abs(x)))


def _ssd_dir(xbc_ref, dt_ref, y_ref, h_ref, bias, a_neg, d, *, nh, ng, need_y):
    qn = SSD_CHUNK
    hd = SSD_HEAD_DIM
    w = nh * hd
    gw = w // ng
    gn = ng * SSD_STATE
    ii = lax.broadcasted_iota(jnp.int32, (qn, qn), 0)
    jj = lax.broadcasted_iota(jnp.int32, (qn, qn), 1)
    mask = (jj <= ii) if d == 0 else (jj >= ii)
    lmat = mask.astype(F32)
    dtv = _softplus(dt_ref[...] + bias)
    cum = jnp.dot(lmat, dtv * a_neg, precision=lax.Precision.HIGHEST, preferred_element_type=F32)
    cum_t = cum.T
    edge = qn - 1 if d == 0 else 0
    tot = cum[edge:edge + 1, :]
    dt_t = dtv.T
    wt_t = dt_t * jnp.exp(cum_t[:, edge:edge + 1] - cum_t)
    decay = jnp.exp(tot)
    lane = lax.broadcasted_iota(jnp.int32, (qn, LANE), 1)
    left = lane < hd
    zero = jnp.zeros((), BF16)
    for g in range(ng):
        bg = xbc_ref[:, w + g * SSD_STATE:w + (g + 1) * SSD_STATE]
        cg = xbc_ref[:, w + gn + g * SSD_STATE:w + gn + (g + 1) * SSD_STATE]
        bg_t = bg.astype(F32).T
        s_in = h_ref[d, :, g * gw:(g + 1) * gw]
        if need_y:
            cb = lax.dot_general(cg, bg, (((1,), (1,)), ((), ())), preferred_element_type=F32)
            yoff = jnp.dot(cg, s_in.astype(BF16), preferred_element_type=F32)
        for pr in range(gw // LANE):
            c0 = d * nh + (g * gw) // hd + 2 * pr
            col = g * gw + pr * LANE
            xp = xbc_ref[:, col:col + LANE]
            r = jnp.concatenate([jnp.where(left, xp, zero), jnp.where(left, zero, xp)], axis=0)
            tops, bots, cols = [], [], []
            for c in (c0, c0 + 1):
                bots.append((bg_t * wt_t[c:c + 1, :]).astype(BF16))
                if need_y:
                    ccol = jnp.broadcast_to(cum[:, c:c + 1], (qn, qn))
                    seg = jnp.where(mask, jnp.exp(ccol - cum_t[c:c + 1, :]), 0.0)
                    tops.append((cb * seg * dt_t[c:c + 1, :]).astype(BF16))
                    cols.append(ccol)
            dec = jnp.where(left[0:1], decay[:, c0:c0 + 1], decay[:, c0 + 1:c0 + 2])
            s_old = s_in[:, pr * LANE:(pr + 1) * LANE]
            if need_y:
                lhs = jnp.concatenate([jnp.concatenate(tops, axis=1), jnp.concatenate(bots, axis=1)], axis=0)
                out = jnp.dot(lhs, r, preferred_element_type=F32)
                ec = jnp.exp(jnp.where(left, cols[0], cols[1]))
                y_ref[:, col:col + LANE] = (out[:qn] + yoff[:, pr * LANE:(pr + 1) * LANE] * ec).astype(y_ref.dtype)
                s_new = out[qn:]
            else:
                s_new = jnp.dot(jnp.concatenate(bots, axis=1), r, preferred_element_type=F32)
            h_ref[d, :, col:col + LANE] = s_old * dec + s_new


def _ssd_kernel(*refs, nh, ng, need_y):
    if need_y:
        xf_ref, xb_ref, dtf_ref, dtb_ref, h0_ref, bias_ref, alog_ref, yf_ref, yb_ref, h_ref = refs
    else:
        xf_ref, xb_ref, dtf_ref, dtb_ref, h0_ref, bias_ref, alog_ref, h_ref = refs
        yf_ref = yb_ref = None

    @pl.when(pl.program_id(1) == 0)
    def _():
        h_ref[...] = h0_ref[...]

    bias = bias_ref[...]
    a_neg = -jnp.exp(alog_ref[...])
    _ssd_dir(xf_ref, dtf_ref, yf_ref, h_ref, bias, a_neg, 0, nh=nh, ng=ng, need_y=need_y)
    _ssd_dir(xb_ref, dtb_ref, yb_ref, h_ref, bias, a_neg, 1, nh=nh, ng=ng, need_y=need_y)


def _ssd_call(xbc, dt, h0, bias, alog, nh, ng, need_y):
    b, l, cd = xbc.shape
    w = nh * SSD_HEAD_DIM
    nc = l // SSD_CHUNK
    q = SSD_CHUNK
    fwd = lambda i, s: (i, s, 0)
    bwd = lambda i, s: (i, nc - 1 - s, 0)
    hspec = pl.BlockSpec((None, 2, SSD_STATE, w), lambda i, s: (i, 0, 0, 0))
    in_specs = [pl.BlockSpec((None, q, cd), fwd), pl.BlockSpec((None, q, cd), bwd),
                pl.BlockSpec((None, q, LANE), fwd), pl.BlockSpec((None, q, LANE), bwd),
                hspec,
                pl.BlockSpec((1, LANE), lambda i, s: (0, 0)), pl.BlockSpec((1, LANE), lambda i, s: (0, 0))]
    out_specs = [hspec]
    out_shape = [jax.ShapeDtypeStruct((b, 2, SSD_STATE, w), F32)]
    if need_y:
        out_specs = [pl.BlockSpec((None, q, w), fwd), pl.BlockSpec((None, q, w), bwd)] + out_specs
        out_shape = [jax.ShapeDtypeStruct((b, l, w), BF16)] * 2 + out_shape
    return pl.pallas_call(
        functools.partial(_ssd_kernel, nh=nh, ng=ng, need_y=need_y), grid=(b, nc),
        in_specs=in_specs, out_specs=out_specs, out_shape=out_shape,
        compiler_params=_cp(("parallel", "arbitrary"), 48),
        name="ssd_scan" if need_y else "ssd_scan_ctx")(xbc, xbc, dt, dt, h0, bias, alog)


def _gnorm_kernel(yf_ref, yb_ref, xs_ref, z_ref, d_ref, g_ref, o_ref, t_ref, *, rows, tb):
    y = d_ref[...] * xs_ref[...].astype(F32) + yf_ref[...].astype(F32) + yb_ref[...].astype(F32)
    y = y * _silu(z_ref[...].astype(F32))
    ms = jnp.mean(y * y, axis=-1, keepdims=True)
    res = y * lax.rsqrt(ms + EPS) * g_ref[...]
    gw = res.shape[1]
    pitch = rows + XPOSE_PAD
    for wl in range(tb):
        for kk in range(gw // LANE):
            t_ref[kk, wl * pitch:wl * pitch + rows, :] = res[wl * rows:(wl + 1) * rows, kk * LANE:(kk + 1) * LANE]
    for r in range(rows):
        o_ref[r] = jnp.concatenate(
            [t_ref[kk, pl.ds(r, tb, stride=pitch), :] for kk in range(gw // LANE)], axis=1).astype(o_ref.dtype)


def _gnorm_call(yf, yb, xbc, proj, dvec, gvec, ng):
    b, l, w = yf.shape
    rows = l // GRID_W
    tb = XPOSE_TILE
    gw = w // ng
    blk = pl.BlockSpec((None, tb * rows, gw), lambda i, s, g: (i, s, g))
    vec = pl.BlockSpec((1, gw), lambda i, s, g: (0, g))
    out = pl.pallas_call(
        functools.partial(_gnorm_kernel, rows=rows, tb=tb), grid=(b, GRID_W // tb, ng),
        in_specs=[blk, blk, blk, blk, vec, vec],
        out_specs=pl.BlockSpec((None, rows, tb, gw), lambda i, s, g: (i, 0, s, g)),
        out_shape=jax.ShapeDtypeStruct((b, rows, GRID_W, w), BF16),
        scratch_shapes=[pltpu.VMEM((gw // LANE, tb * (rows + XPOSE_PAD), LANE), F32)],
        compiler_params=_cp(("parallel", "parallel", "parallel"), 40),
        name="ssd_gnorm")(yf, yb, xbc, proj, dvec, gvec)
    return out.reshape(b * l, w)


def _moe_kernel(e_ref, b0_ref, nb_ref, so_ref, *refs, nsub, nh):
    xs = refs[:nsub]
    wg_ref, wu_ref, wd_ref, o_ref, act_ref, wdb_ref = refs[nsub:]
    s = pl.program_id(0)
    t = pl.program_id(1)
    nb = nb_ref[s]

    @pl.when(jnp.logical_and(t < nh, nb > 0))
    def _():
        wg = wg_ref[...].astype(BF16)
        wu = wu_ref[...].astype(BF16)
        for k in range(nsub):
            @pl.when(k < nb)
            def _():
                x = xs[k][...]
                gt = jnp.dot(x, wg, preferred_element_type=F32)
                up = jnp.dot(x, wu, preferred_element_type=F32)
                act_ref[k, t] = (_silu(gt) * up).astype(BF16)

    @pl.when(jnp.logical_and(t >= nh, nb > 0))
    def _():
        wdb_ref[...] = wd_ref[...].astype(BF16)
        for k in range(nsub):
            rows = slice(k * MOE_BLOCK, (k + 1) * MOE_BLOCK)

            @pl.when(k < nb)
            def _():
                a = jnp.concatenate([act_ref[k, h] for h in range(nh)], axis=1)
                o_ref[rows, :] = jnp.dot(a, wdb_ref[...], preferred_element_type=F32).astype(o_ref.dtype)

            @pl.when(k >= nb)
            def _():
                o_ref[rows, :] = jnp.zeros((MOE_BLOCK, o_ref.shape[1]), o_ref.dtype)

    @pl.when(jnp.logical_and(t >= nh, nb == 0))
    def _():
        o_ref[...] = jnp.zeros(o_ref.shape, o_ref.dtype)


def _moe_call(sup_e, sup_b0, sup_nb, sup_out, xs, w_gate, w_up, w_down):
    n_slots, d = xs.shape
    hid = w_gate.shape[-1]
    th = min(256, hid)
    tn = min(1024, d)
    nh = hid // th
    nt = d // tn
    ns = sup_e.shape[0]
    rsup = MOE_SUB * MOE_BLOCK

    def x_map(k):
        def index(s, t, e, b0, nb, so):
            sx = jnp.where(t < nh, s, jnp.minimum(s + 1, ns - 1))
            return (b0[sx * MOE_SUB + k], 0)
        return index

    def hid_idx(s, t, nb):
        return jnp.where(nb[s] > 0, jnp.minimum(t, nh - 1), nh - 1)

    def col_idx(s, t, nb):
        return jnp.where(nb[s] > 0, jnp.clip(t - nh, 0, nt - 1), nt - 1)

    gs = pltpu.PrefetchScalarGridSpec(
        num_scalar_prefetch=4, grid=(ns, nh + nt),
        in_specs=[pl.BlockSpec((MOE_BLOCK, d), x_map(k)) for k in range(MOE_SUB)] + [
            pl.BlockSpec((None, d, th), lambda s, t, e, b0, nb, so: (e[s], 0, hid_idx(s, t, nb))),
            pl.BlockSpec((None, d, th), lambda s, t, e, b0, nb, so: (e[s], 0, hid_idx(s, t, nb))),
            pl.BlockSpec((None, hid, tn), lambda s, t, e, b0, nb, so: (e[s], 0, col_idx(s, t, nb)))],
        out_specs=pl.BlockSpec((rsup, tn), lambda s, t, e, b0, nb, so: (so[s], jnp.clip(t - nh, 0, nt - 1))),
        scratch_shapes=[pltpu.VMEM((MOE_SUB, nh, MOE_BLOCK, th), BF16), pltpu.VMEM((hid, tn), BF16)])
    return pl.pallas_call(
        functools.partial(_moe_kernel, nsub=MOE_SUB, nh=nh), grid_spec=gs,
        out_shape=jax.ShapeDtypeStruct((ns * rsup, d), BF16),
        compiler_params=_cp(("arbitrary", "arbitrary"), 60), name="moe_experts")(
            sup_e, sup_b0, sup_nb, sup_out, *([xs] * MOE_SUB), w_gate, w_up, w_down)


def _final_kernel(x_ref, ya_ref, yb_ref, w_ref, g2_ref, fg_ref, o_ref):
    wts = w_ref[...]
    moe = ya_ref[...].astype(F32) * wts[:, 0:1] + yb_ref[...].astype(F32) * wts[:, 1:2]
    x = x_ref[...] + g2_ref[...] * moe
    ms = jnp.mean(x * x, axis=-1, keepdims=True)
    o_ref[...] = x * lax.rsqrt(ms + EPS) * fg_ref[...]


def _final_call(x, ya, yb, wts, mods, ig, fg):
    b, l, d = x.shape
    tr = min(256, l)
    nt = l // tr
    row = lambda i, r: (i * nt + r, 0)
    return pl.pallas_call(
        _final_kernel, grid=(b, nt),
        in_specs=[pl.BlockSpec((None, tr, d), lambda i, r: (i, r, 0)),
                  pl.BlockSpec((tr, d), row), pl.BlockSpec((tr, d), row),
                  pl.BlockSpec((tr, MOE_TOP_K), row),
                  pl.BlockSpec((None, None, 1, d), lambda i, r: (i, ig, 0, 0)),
                  pl.BlockSpec((1, d), lambda i, r: (0, 0))],
        out_specs=pl.BlockSpec((None, tr, d), lambda i, r: (i, r, 0)),
        out_shape=jax.ShapeDtypeStruct((b, l, d), F32),
        compiler_params=_cp(("parallel", "parallel"), 40), name="combine_final")(x, ya, yb, wts, mods, fg)


def _route(sel, ne):
    m = sel.shape[0]
    weights = sel[:, MOE_TOP_K:2 * MOE_TOP_K]
    n_assign = m * MOE_TOP_K
    flat_e = jnp.concatenate([sel[:, k] for k in range(MOE_TOP_K)], axis=0).astype(jnp.int32)
    onehot = (flat_e[:, None] == jnp.arange(ne, dtype=flat_e.dtype)[None, :]).astype(jnp.int32)
    csum = jnp.cumsum(onehot, axis=0)
    counts = csum[-1]
    local = jnp.sum(csum * onehot, axis=1) - 1
    nblk_e = (counts + MOE_BLOCK - 1) // MOE_BLOCK
    padded = nblk_e * MOE_BLOCK
    pad_end = jnp.cumsum(padded)
    pad_start = pad_end - padded
    dest = pad_start[flat_e] + local
    n_blocks = -(-(n_assign + ne * (MOE_BLOCK - 1)) // MOE_BLOCK)
    tok = jnp.arange(n_assign, dtype=jnp.int32) % m
    n_slots = n_blocks * MOE_BLOCK
    slot_tok = (jnp.arange(n_slots, dtype=jnp.int32) % m).at[dest].set(tok)
    rsup = MOE_SUB * MOE_BLOCK
    ns_e = (nblk_e + MOE_SUB - 1) // MOE_SUB
    sup_end = jnp.cumsum(ns_e)
    sup_start = sup_end - ns_e
    n_sup = (n_blocks + (MOE_SUB - 1) * ne) // MOE_SUB
    sidx = jnp.arange(n_sup, dtype=jnp.int32)
    last = sup_end[-1] - 1
    s_eff = jnp.minimum(sidx, last)
    e_s = jnp.minimum(jnp.searchsorted(sup_end, s_eff, side='right'), ne - 1).astype(jnp.int32)
    k_s = s_eff - sup_start[e_s]
    b0_s = pad_start[e_s] // MOE_BLOCK + MOE_SUB * k_s
    nb_s = jnp.clip(nblk_e[e_s] - MOE_SUB * k_s, 0, MOE_SUB)
    used = sidx <= last
    sup_nb = jnp.where(used, nb_s, 0).astype(jnp.int32)
    kk = jnp.arange(MOE_SUB, dtype=jnp.int32)[None, :]
    sup_b0 = jnp.maximum(lax.cummax(jnp.where(kk < sup_nb[:, None], b0_s[:, None] + kk, -1), axis=0), 0)
    sup_b0 = sup_b0.reshape(-1).astype(jnp.int32)
    pos = ((sup_start[flat_e] + local // rsup) * rsup + local % rsup).astype(jnp.int32)
    return weights, slot_tok, e_s, sup_b0, sup_nb, sidx, pos


def kernel(x, c, ctx, c_ctx, w_mod, b_mod, norm1_g, w_in, s5_lam_re, s5_lam_im, s5_log_dt, s5_b_re, s5_b_im, s5_c_re, s5_c_im, s5_d, s5_w_val, s5_w_gate, ssd_conv_w, ssd_conv_b, ssd_a_log, ssd_dt_bias, ssd_d, ssd_norm_g, ssd_w_out, w_o, norm2_g, moe_w_group, moe_b_group, moe_w_expert, moe_b_expert, moe_w_gate, moe_w_up, moe_w_down, final_g):
    depth = w_mod.shape[0]
    assert depth == 1, "single-layer block"
    bsz, n_lat, d = x.shape
    l_ctx = ctx.shape[1]
    w5 = s5_d.shape[1]
    nh = ssd_d.shape[1]
    w = nh * SSD_HEAD_DIM
    conv_dim = ssd_conv_w.shape[2]
    ng = (conv_dim - w) // (2 * SSD_STATE)
    ssd_in = w + conv_dim + 2 * nh
    o1, o2 = w5, w5 + ssd_in
    l = 0

    cc = jnp.concatenate([c, c_ctx[None, :]], axis=0)
    cc = jnp.pad(cc, ((0, (-cc.shape[0]) % 8), (0, 0)))
    mods = _mod_call(cc, w_mod[l], b_mod[l]).reshape(cc.shape[0], 6, 1, d)
    i_sh1, i_sc1, i_g1, i_sh2, i_sc2, i_g2 = range(6)

    w_in_l = w_in[l]
    o_dt = o1 + w + conv_dim

    hn_rm, hn_cm = _norm_lat_call(x, norm1_g[l], mods, i_sc1, i_sh1)
    hc = _norm_ctx_call(ctx, norm1_g[l], mods, bsz, i_sc1, i_sh1)
    hn_rm = hn_rm.reshape(bsz * n_lat, d)
    hn_cm = hn_cm.reshape(bsz * n_lat, d)
    hc = hc.reshape(bsz * l_ctx, d)

    u_lat = _mm_call(hn_rm, w_in_l, BF16, "in_s5", 0, o1).reshape(bsz, n_lat, w5)
    u_ctx = _mm_call(hc, w_in_l, BF16, "in_s5_ctx", 0, o1).reshape(bsz, l_ctx, w5)
    gates = _mm_call(hn_rm, w_in_l, BF16, "in_gates", o2, 2 * d)
    p_lat = _mm_call(hn_cm, w_in_l, BF16, "in_ssd", o1, w + conv_dim).reshape(bsz, n_lat, w + conv_dim)
    p_ctx = _mm_call(hc, w_in_l, BF16, "in_ssd_ctx", o1, w + conv_dim).reshape(bsz, l_ctx, w + conv_dim)
    dt_lat = _mm_call(hn_cm, w_in_l, F32, "in_dt", o_dt, 2 * nh).reshape(bsz, n_lat, LANE)
    dt_ctx = _mm_call(hc, w_in_l, F32, "in_dt_ctx", o_dt, 2 * nh).reshape(bsz, l_ctx, LANE)

    nj = w5 // LANE
    bmat, cmat, dmat, lre, lim = _s5_params(s5_lam_re[l], s5_lam_im[l], s5_log_dt[l], s5_b_re[l], s5_b_im[l],
                                            s5_c_re[l], s5_c_im[l], bsz)
    s5_zero = jnp.zeros((nj, 2, 2 * bsz, lre.shape[-1]), F32)
    (s5_ctx,) = _s5_call(u_ctx, bmat, cmat, dmat, lre, lim, s5_zero, False)
    ya_f, ya_b, _ = _s5_call(u_lat, bmat, cmat, dmat, lre, lim, s5_ctx, True)

    xbc_ctx = _conv_call(p_ctx, ssd_conv_w[l], ssd_conv_b[l], w)
    xbc_lat = _conv_call(p_lat, ssd_conv_w[l], ssd_conv_b[l], w)
    pad_h = LANE - 2 * nh
    bias = jnp.pad(ssd_dt_bias[l].astype(F32).reshape(1, 2 * nh), ((0, 0), (0, pad_h)))
    alog = jnp.pad(ssd_a_log[l].astype(F32).reshape(1, 2 * nh), ((0, 0), (0, pad_h)))
    h_zero = jnp.zeros((bsz, 2, SSD_STATE, w), F32)
    (h_ctx,) = _ssd_call(xbc_ctx, dt_ctx, h_zero, bias, alog, nh, ng, False)
    y_f, y_b, _ = _ssd_call(xbc_lat, dt_lat, h_ctx, bias, alog, nh, ng, True)
    d_vec = jnp.repeat(ssd_d[l].astype(F32), SSD_HEAD_DIM).reshape(1, w)
    y_ssd = _gnorm_call(y_f, y_b, xbc_lat, p_lat, d_vec, ssd_norm_g[l].astype(F32).reshape(1, w), ng)

    m_lat = bsz * n_lat
    part_a = _glu_call(u_lat.reshape(m_lat, w5), ya_f.reshape(m_lat, w5), ya_b.reshape(m_lat, w5),
                       s5_d[l].astype(F32).reshape(1, w5),
                       s5_w_val[l].astype(BF16), s5_w_gate[l].astype(BF16), gates)
    merged = _merge_call(y_ssd, ssd_w_out[l], gates, part_a)
    x1 = _resid_call(merged, w_o[l], x, mods, i_g1)

    ngr = moe_w_group.shape[-1]
    ne = moe_w_expert.shape[-1]
    wr = jnp.concatenate([moe_w_group[l], moe_w_expert[l]], axis=1).astype(F32)
    wr = jnp.pad(wr, ((0, 0), (0, (-(ngr + ne)) % LANE)))
    br = jnp.concatenate([moe_b_group[l], moe_b_expert[l]]).astype(F32)
    br = jnp.pad(br, (0, (-(ngr + ne)) % LANE)).reshape(1, -1)
    hx, sel = _norm_router_call(x1, norm2_g[l], mods, i_sc2, i_sh2, wr, br, ngr, ne)
    m = bsz * n_lat
    hx = hx.reshape(m, d)
    weights, slot_tok, sup_e, sup_b0, sup_nb, sup_out, pos = _route(sel.reshape(m, -1), ne)
    xs = hx[slot_tok]
    y_slots = _moe_call(sup_e, sup_b0, sup_nb, sup_out, xs, moe_w_gate[l], moe_w_up[l], moe_w_down[l])
    ya = y_slots[pos[:m]]
    yb = y_slots[pos[m:]]
    return _final_call(x1, ya, yb, weights.astype(F32), mods, i_g2, final_g.reshape(1, d))
```

```python
import functools
import math

import jax
import jax.numpy as jnp
from jax import lax
from jax.experimental import pallas as pl
from jax.experimental.pallas import tpu as pltpu

F32 = jnp.float32
BF16 = jnp.bfloat16

GRID_W = 64
EPS = 1e-6
LANE = 128
SSD_HEAD_DIM = 64
SSD_STATE = 128
SSD_CHUNK = 128
SSD_CONV = 5
MOE_TOP_K = 2
MOE_BLOCK = 256
MOE_SUB = 4
XPOSE_TILE = 16
XPOSE_PAD = 8
S5_CHUNK = 256
S5_ROW_PAD = 8


def _cp(sem, mb):
    return pltpu.CompilerParams(dimension_semantics=sem, vmem_limit_bytes=mb * 1024 * 1024)


def _sigmoid(x):
    return 1.0 / (1.0 + jnp.exp(-x))


def _silu(x):
    return x * _sigmoid(x)


def _mod_kernel(c_ref, w_ref, b_ref, o_ref):
    s = _silu(c_ref[...])
    o_ref[...] = jnp.dot(s.astype(BF16), w_ref[...].astype(BF16),
                         preferred_element_type=F32) + b_ref[...]


def _mod_call(cc, w, b):
    r, d = cc.shape
    n = w.shape[1]
    tn = min(512, n)
    return pl.pallas_call(
        _mod_kernel, grid=(n // tn,),
        in_specs=[pl.BlockSpec((r, d), lambda j: (0, 0)),
                  pl.BlockSpec((d, tn), lambda j: (0, j)),
                  pl.BlockSpec((1, tn), lambda j: (0, j))],
        out_specs=pl.BlockSpec((r, tn), lambda j: (0, j)),
        out_shape=jax.ShapeDtypeStruct((r, n), F32),
        compiler_params=_cp(("parallel",), 40), name="mod")(cc, w, b.reshape(1, n))


def _rmsmod(x, g, sc, sh):
    ms = jnp.mean(x * x, axis=-1, keepdims=True)
    return (x * lax.rsqrt(ms + EPS) * g) * (1.0 + sc) + sh


def _norm_lat_kernel(x_ref, g_ref, sc_ref, sh_ref, orm_ref, ocm_ref, t_ref, *, tb, d):
    pitch = tb + XPOSE_PAD
    for r in range(tb):
        y = _rmsmod(x_ref[r], g_ref[...], sc_ref[...], sh_ref[...])
        orm_ref[r] = y.astype(BF16)
        for kk in range(d // LANE):
            t_ref[kk, r * pitch:r * pitch + tb, :] = y[:, kk * LANE:(kk + 1) * LANE]
    for wl in range(tb):
        ocm_ref[wl] = jnp.concatenate(
            [t_ref[kk, pl.ds(wl, tb, stride=pitch), :] for kk in range(d // LANE)], axis=1).astype(BF16)


def _norm_lat_call(x, g, mods, isc, ish):
    b, l, d = x.shape
    rows = l // GRID_W
    tb = XPOSE_TILE
    xv = x.reshape(b, rows, GRID_W, d)
    orm, ocm = pl.pallas_call(
        functools.partial(_norm_lat_kernel, tb=tb, d=d), grid=(b, rows // tb, GRID_W // tb),
        in_specs=[pl.BlockSpec((None, tb, tb, d), lambda i, r, c: (i, r, c, 0)),
                  pl.BlockSpec((1, d), lambda i, r, c: (0, 0)),
                  pl.BlockSpec((None, None, 1, d), lambda i, r, c: (i, isc, 0, 0)),
                  pl.BlockSpec((None, None, 1, d), lambda i, r, c: (i, ish, 0, 0))],
        out_specs=[pl.BlockSpec((None, tb, tb, d), lambda i, r, c: (i, r, c, 0)),
                   pl.BlockSpec((None, tb, tb, d), lambda i, r, c: (i, c, r, 0))],
        out_shape=[jax.ShapeDtypeStruct((b, rows, GRID_W, d), BF16),
                   jax.ShapeDtypeStruct((b, GRID_W, rows, d), BF16)],
        scratch_shapes=[pltpu.VMEM((d // LANE, tb * (tb + XPOSE_PAD), LANE), F32)],
        compiler_params=_cp(("parallel", "parallel", "parallel"), 48),
        name="norm1_lat")(xv, g.reshape(1, d), mods, mods)
    return orm.reshape(b, l, d), ocm.reshape(b, l, d)


def _norm_kernel(x_ref, g_ref, sc_ref, sh_ref, o_ref):
    o_ref[...] = _rmsmod(x_ref[...], g_ref[...], sc_ref[...], sh_ref[...]).astype(o_ref.dtype)


def _norm_ctx_call(x, g, mods, row, isc, ish):
    b, l, d = x.shape
    tr = min(256, l)
    return pl.pallas_call(
        _norm_kernel, grid=(b, l // tr),
        in_specs=[pl.BlockSpec((None, tr, d), lambda i, r: (i, r, 0)),
                  pl.BlockSpec((1, d), lambda i, r: (0, 0)),
                  pl.BlockSpec((None, None, 1, d), lambda i, r: (row, isc, 0, 0)),
                  pl.BlockSpec((None, None, 1, d), lambda i, r: (row, ish, 0, 0))],
        out_specs=pl.BlockSpec((None, tr, d), lambda i, r: (i, r, 0)),
        out_shape=jax.ShapeDtypeStruct((b, l, d), BF16),
        compiler_params=_cp(("parallel", "parallel"), 40), name="norm1_ctx")(x, g.reshape(1, d), mods, mods)


def _first_argmax(v, vmax, lane):
    return jnp.min(jnp.where(v == vmax, lane, float(LANE)), axis=-1, keepdims=True)


def _norm_router_kernel(x_ref, g_ref, sc_ref, sh_ref, wr_ref, br_ref, o_ref, sel_ref, *, ngr, ne):
    y = _rmsmod(x_ref[...], g_ref[...], sc_ref[...], sh_ref[...])
    o_ref[...] = y.astype(BF16)
    y_hi = y.astype(BF16)
    y_lo = (y - y_hi.astype(F32)).astype(BF16)
    w_hi = wr_ref[0]
    lg = (jnp.dot(y_hi, w_hi, preferred_element_type=F32) + jnp.dot(y_hi, wr_ref[1], preferred_element_type=F32)
          + jnp.dot(y_lo, w_hi, preferred_element_type=F32)) + br_ref[...]
    epg = ne // ngr
    lane = lax.broadcasted_iota(jnp.int32, lg.shape, 1).astype(F32)
    ninf = -jnp.inf
    gl = jnp.where(lane < ngr, lg, ninf)
    ge = jnp.exp(gl - jnp.max(gl, axis=-1, keepdims=True))
    g_prob = ge / jnp.sum(ge, axis=-1, keepdims=True)
    g_p = jnp.max(g_prob, axis=-1, keepdims=True)
    lo = ngr + epg * _first_argmax(g_prob, g_p, lane)
    cand = jnp.where(jnp.logical_and(lane >= lo, lane < lo + epg), lg, ninf)
    v1 = jnp.max(cand, axis=-1, keepdims=True)
    l1 = _first_argmax(cand, v1, lane)
    rest = jnp.where(lane == l1, ninf, cand)
    v2 = jnp.max(rest, axis=-1, keepdims=True)
    l2 = _first_argmax(rest, v2, lane)
    ex = jnp.exp(v2 - v1)
    w1 = g_p * (1.0 / (1.0 + ex))
    w2 = g_p * (ex / (1.0 + ex))
    sel_ref[...] = jnp.where(lane == 0, l1 - ngr, jnp.where(lane == 1, l2 - ngr,
                             jnp.where(lane == 2, w1, jnp.where(lane == 3, w2, 0.0))))


def _norm_router_call(x, g, mods, isc, ish, wr, br, ngr, ne):
    b, l, d = x.shape
    tr = min(256, l)
    nr = wr.shape[1]
    w_hi = wr.astype(BF16)
    wr = jnp.stack([w_hi, (wr - w_hi.astype(F32)).astype(BF16)], axis=0)
    return pl.pallas_call(
        functools.partial(_norm_router_kernel, ngr=ngr, ne=ne), grid=(b, l // tr),
        in_specs=[pl.BlockSpec((None, tr, d), lambda i, r: (i, r, 0)),
                  pl.BlockSpec((1, d), lambda i, r: (0, 0)),
                  pl.BlockSpec((None, None, 1, d), lambda i, r: (i, isc, 0, 0)),
                  pl.BlockSpec((None, None, 1, d), lambda i, r: (i, ish, 0, 0)),
                  pl.BlockSpec((2, d, nr), lambda i, r: (0, 0, 0)),
                  pl.BlockSpec((1, nr), lambda i, r: (0, 0))],
        out_specs=[pl.BlockSpec((None, tr, d), lambda i, r: (i, r, 0)),
                   pl.BlockSpec((None, tr, nr), lambda i, r: (i, r, 0))],
        out_shape=[jax.ShapeDtypeStruct((b, l, d), BF16), jax.ShapeDtypeStruct((b, l, nr), F32)],
        compiler_params=_cp(("parallel", "parallel"), 40),
        name="norm2_router")(x, g.reshape(1, d), mods, mods, wr, br)


def _mm_kernel(a_ref, b_ref, o_ref):
    o_ref[...] = jnp.dot(a_ref[...], b_ref[...].astype(BF16), preferred_element_type=F32).astype(o_ref.dtype)


def _mm_tiles(m, n):
    tm = min(1024, m)
    tn = min(512, n)
    return tm, tn


def _mm_call(a, b, out_dtype, name, col0=0, n=None):
    m, k = a.shape
    n = b.shape[1] - col0 if n is None else n
    tm, tn = _mm_tiles(m, n)
    if col0 % LANE or n % tn or n % LANE:
        b = b[:, col0:col0 + n]
        pad = (-n) % LANE
        b = jnp.pad(b, ((0, 0), (0, pad)))
        n, col0 = n + pad, 0
        tm, tn = _mm_tiles(m, n)
    return pl.pallas_call(
        _mm_kernel, grid=(m // tm, n // tn),
        in_specs=[pl.BlockSpec((tm, k), lambda i, j: (i, 0)),
                  pl.BlockSpec((pl.Element(k), pl.Element(tn)), lambda i, j: (0, (col0 // LANE + j * (tn // LANE)) * LANE))],
        out_specs=pl.BlockSpec((tm, tn), lambda i, j: (i, j)),
        out_shape=jax.ShapeDtypeStruct((m, n), out_dtype),
        compiler_params=_cp(("parallel", "parallel"), 48), name=name)(a, b)


def _gelu_tanh(x):
    return x * (0.5 * (1.0 + jnp.tanh(math.sqrt(2.0 / math.pi) * (x + 0.044715 * (x * x * x)))))


def _glu_kernel(u_ref, yf_ref, yb_ref, d_ref, wv_ref, wg_ref, gate_ref, o_ref, a_ref):
    @pl.when(pl.program_id(1) == 0)
    def _():
        y = d_ref[...] * u_ref[...].astype(F32) + yf_ref[...].astype(F32) + yb_ref[...].astype(F32)
        a_ref[...] = _gelu_tanh(y).astype(BF16)

    a = a_ref[...]
    val = jnp.dot(a, wv_ref[...], preferred_element_type=F32)
    gl = jnp.dot(a, wg_ref[...], preferred_element_type=F32)
    o_ref[...] = (_sigmoid(gate_ref[...].astype(F32)) * (val * _sigmoid(gl))).astype(o_ref.dtype)


def _glu_call(u, yf, yb, dvec, wv, wg, gates):
    m, k = u.shape
    n = wv.shape[1]
    tm, tn = _mm_tiles(m, n)
    row = pl.BlockSpec((tm, k), lambda i, j: (i, 0))
    return pl.pallas_call(
        _glu_kernel, grid=(m // tm, n // tn),
        in_specs=[row, row, row,
                  pl.BlockSpec((1, k), lambda i, j: (0, 0)),
                  pl.BlockSpec((k, tn), lambda i, j: (0, j)),
                  pl.BlockSpec((k, tn), lambda i, j: (0, j)),
                  pl.BlockSpec((tm, tn), lambda i, j: (i, j))],
        out_specs=pl.BlockSpec((tm, tn), lambda i, j: (i, j)),
        out_shape=jax.ShapeDtypeStruct((m, n), BF16),
        scratch_shapes=[pltpu.VMEM((tm, k), BF16)],
        compiler_params=_cp(("parallel", "arbitrary"), 56), name="s5_glu")(u, yf, yb, dvec, wv, wg, gates)


def _merge_kernel(a_ref, w_ref, gate_ref, pa_ref, o_ref):
    br = jnp.dot(a_ref[...], w_ref[...].astype(BF16), preferred_element_type=F32)
    o_ref[...] = (pa_ref[...].astype(F32) + _sigmoid(gate_ref[...].astype(F32)) * br).astype(o_ref.dtype)


def _merge_call(a, w, gates, part_a):
    m, k = a.shape
    n = w.shape[1]
    tm, tn = _mm_tiles(m, n)
    off = n // tn
    return pl.pallas_call(
        _merge_kernel, grid=(m // tm, n // tn),
        in_specs=[pl.BlockSpec((tm, k), lambda i, j: (i, 0)),
                  pl.BlockSpec((k, tn), lambda i, j: (0, j)),
                  pl.BlockSpec((tm, tn), lambda i, j: (i, j + off)),
                  pl.BlockSpec((tm, tn), lambda i, j: (i, j))],
        out_specs=pl.BlockSpec((tm, tn), lambda i, j: (i, j)),
        out_shape=jax.ShapeDtypeStruct((m, n), BF16),
        compiler_params=_cp(("parallel", "parallel"), 48), name="ssd_out_merge")(a, w, gates, part_a)


def _resid_kernel(a_ref, w_ref, x_ref, g_ref, o_ref):
    mix = jnp.dot(a_ref[...], w_ref[...].astype(BF16), preferred_element_type=F32)
    o_ref[...] = x_ref[...] + g_ref[...] * mix


def _resid_call(a, w, x, mods, ig):
    b, l, d = x.shape
    k = a.shape[1]
    tm, tn = _mm_tiles(l, d)
    nt = l // tm
    return pl.pallas_call(
        _resid_kernel, grid=(b * nt, d // tn),
        in_specs=[pl.BlockSpec((tm, k), lambda i, j: (i, 0)),
                  pl.BlockSpec((k, tn), lambda i, j: (0, j)),
                  pl.BlockSpec((None, tm, tn), lambda i, j: (i // nt, i % nt, j)),
                  pl.BlockSpec((None, None, 1, tn), lambda i, j: (i // nt, ig, 0, j))],
        out_specs=pl.BlockSpec((None, tm, tn), lambda i, j: (i // nt, i % nt, j)),
        out_shape=jax.ShapeDtypeStruct((b, l, d), F32),
        compiler_params=_cp(("parallel", "parallel"), 48), name="w_o_resid")(a, w, x, mods)


def _s5_kernel(*refs, nb, tp, p8, need_y):
    if need_y:
        uf_ref, ub_ref, b_ref, c_ref, d_ref, lre_ref, lim_ref, h0_ref, yf_ref, yb_ref, h_ref, buf_ref, il_ref = refs
    else:
        uf_ref, ub_ref, b_ref, c_ref, d_ref, lre_ref, lim_ref, h0_ref, h_ref, buf_ref, il_ref = refs
    q = 2 * nb
    nk = p8 // LANE
    pitch = tp + S5_ROW_PAD

    @pl.when(pl.program_id(1) == 0)
    def _():
        h_ref[...] = h0_ref[...]

    rev = (lax.broadcasted_iota(jnp.int32, (tp, tp), 0) + lax.broadcasted_iota(jnp.int32, (tp, tp), 1)
           == tp - 1).astype(BF16)
    def pair_rows(u_ref, b, newer_first):
        il_ref[...] = u_ref[b].astype(F32)
        even = il_ref[pl.ds(0, tp, stride=2), :].astype(BF16)
        odd = il_ref[pl.ds(1, tp, stride=2), :].astype(BF16)
        return jnp.concatenate([odd, even] if newer_first else [even, odd], axis=1)

    lhs = []
    for d in range(2):
        if d == 0:
            u = jnp.concatenate([pair_rows(uf_ref, b, False) for b in range(nb)], axis=0)
        else:
            u = jnp.concatenate(
                [jnp.dot(rev, pair_rows(ub_ref, b, True), preferred_element_type=F32).astype(BF16)
                 for b in range(nb)], axis=0)
        lhs.append(u)
        bu = jnp.dot(u, b_ref[d], preferred_element_type=F32)
        for b in range(nb):
            r0 = (d * nb + b) * pitch
            for k in range(2 * nk):
                buf_ref[k, r0:r0 + tp, :] = bu[b * tp:(b + 1) * tp, k * LANE:(k + 1) * LANE]
    ar = [lre_ref[:, k * LANE:(k + 1) * LANE] for k in range(nk)]
    ai = [lim_ref[:, k * LANE:(k + 1) * LANE] for k in range(nk)]

    def step(s, carry):
        rows = pl.ds(s, q, stride=pitch)
        out = []
        for k in range(nk):
            hr, hi = carry[k]
            nr = ar[k] * hr - ai[k] * hi + buf_ref[k, rows, :]
            ni = ar[k] * hi + ai[k] * hr + buf_ref[nk + k, rows, :]
            buf_ref[k, rows, :] = hr
            buf_ref[nk + k, rows, :] = hi
            out.append((nr, ni))
        return tuple(out)

    init = tuple((h_ref[0, :, k * LANE:(k + 1) * LANE], h_ref[1, :, k * LANE:(k + 1) * LANE]) for k in range(nk))
    fin = lax.fori_loop(0, tp, step, init, unroll=8)
    for k in range(nk):
        h_ref[0, :, k * LANE:(k + 1) * LANE] = fin[k][0]
        h_ref[1, :, k * LANE:(k + 1) * LANE] = fin[k][1]

    if need_y:
        for d in range(2):
            h = jnp.concatenate(
                [jnp.concatenate([buf_ref[k, (d * nb + b) * pitch:(d * nb + b) * pitch + tp, :].astype(BF16)
                                  for k in range(2 * nk)], axis=1) for b in range(nb)], axis=0)
            y = (jnp.dot(h, c_ref[d], preferred_element_type=F32)
                 + jnp.dot(lhs[d], d_ref[d], preferred_element_type=F32))
            for b in range(nb):
                yb = y[b * tp:(b + 1) * tp]
                if d == 1:
                    yb = jnp.dot(rev, yb.astype(BF16), preferred_element_type=F32)
                first, second = (0, 1) if d == 0 else (1, 0)
                il_ref[pl.ds(first, tp, stride=2), :] = yb[:, 0:LANE]
                il_ref[pl.ds(second, tp, stride=2), :] = yb[:, LANE:2 * LANE]
                (yf_ref if d == 0 else yb_ref)[b] = il_ref[...].astype(BF16)


def _s5_call(u, bmat, cmat, dmat, lre, lim, h0, need_y):
    nb, l, w5 = u.shape
    nj = w5 // LANE
    q = 2 * nb
    t = min(S5_CHUNK, l)
    tp = t // 2
    p8 = lre.shape[-1]
    nc = l // t
    hspec = pl.BlockSpec((None, 2, q, p8), lambda j, c: (j, 0, 0, 0))
    ublk = (nb, t, LANE)
    in_specs = [pl.BlockSpec(ublk, lambda j, c: (0, c, j)), pl.BlockSpec(ublk, lambda j, c: (0, nc - 1 - c, j)),
                pl.BlockSpec((None, 2, 2 * LANE, 2 * p8), lambda j, c: (j, 0, 0, 0)),
                pl.BlockSpec((None, 2, 2 * p8, 2 * LANE), lambda j, c: (j, 0, 0, 0)),
                pl.BlockSpec((None, 2, 2 * LANE, 2 * LANE), lambda j, c: (j, 0, 0, 0)),
                pl.BlockSpec((None, q, p8), lambda j, c: (j, 0, 0)),
                pl.BlockSpec((None, q, p8), lambda j, c: (j, 0, 0)),
                hspec]
    out_specs = [hspec]
    out_shape = [jax.ShapeDtypeStruct((nj, 2, q, p8), F32)]
    scratch = [pltpu.VMEM((2 * p8 // LANE, q * (tp + S5_ROW_PAD), LANE), F32), pltpu.VMEM((t, LANE), F32)]
    if need_y:
        out_specs = [pl.BlockSpec(ublk, lambda j, c: (0, c, j)),
                     pl.BlockSpec(ublk, lambda j, c: (0, nc - 1 - c, j))] + out_specs
        out_shape = [jax.ShapeDtypeStruct((nb, l, w5), BF16)] * 2 + out_shape
    kern = functools.partial(_s5_kernel, nb=nb, tp=tp, p8=p8, need_y=need_y)
    return pl.pallas_call(
        kern, grid=(nj, nc), in_specs=in_specs, out_specs=out_specs, out_shape=out_shape,
        scratch_shapes=scratch, compiler_params=_cp(("parallel", "arbitrary"), 48),
        name="s5_scan" if need_y else "s5_scan_ctx")(u, u, bmat, cmat, dmat, lre, lim, h0)


def _s5_params(lam_re, lam_im, log_dt, b_re, b_im, c_re, c_im, nb):
    _, g, p = lam_re.shape
    s = b_re.shape[-1]
    gpb = LANE // s
    nj = g // gpb
    lam = lax.complex(lam_re.astype(F32), lam_im.astype(F32))
    lam_bar = jnp.exp(lam * jnp.exp(log_dt.astype(F32))[..., None])
    b_bar = ((lam_bar - 1.0) / lam)[..., None] * lax.complex(b_re.astype(F32), b_im.astype(F32))
    def quadrants(parts):
        r, c = parts[0].shape[-2:]
        same_group = (jnp.arange(gpb * r)[:, None] // r) == (jnp.arange(gpb * c)[None, :] // c)

        def spread(x):
            return jnp.where(same_group, jnp.tile(x.reshape(2, nj, gpb * r, c), (1, 1, 1, gpb)), 0.0)

        top = jnp.concatenate([spread(parts[0]), spread(parts[1])], axis=-1)
        bot = jnp.concatenate([spread(parts[2]), spread(parts[3])], axis=-1)
        return jnp.concatenate([top, bot], axis=-2).transpose(1, 0, 2, 3).astype(BF16)

    tb = lambda z: jnp.swapaxes(z, -1, -2)
    lb = lam_bar[..., None] * b_bar
    bmat = quadrants([tb(lb.real), tb(lb.imag), tb(b_bar.real), tb(b_bar.imag)])
    cc = lax.complex(c_re.astype(F32), c_im.astype(F32))
    c_l1 = cc * lam_bar[:, :, None, :]
    c_l2 = c_l1 * lam_bar[:, :, None, :]
    cmat = quadrants([tb(c_l1.real), tb(c_l2.real), -tb(c_l1.imag), -tb(c_l2.imag)])
    m0 = jnp.einsum('dgsp,dgpt->dgts', cc, b_bar).real
    m1 = jnp.einsum('dgsp,dgpt->dgts', c_l1, b_bar).real
    dmat = quadrants([m0, m1, jnp.zeros_like(m0), m0])

    def lam_of(part):
        v = part.reshape(2, nj, gpb * p).transpose(1, 0, 2)
        return jnp.repeat(v, nb, axis=1)

    lam2 = lam_bar * lam_bar
    return bmat, cmat, dmat, lam_of(lam2.real), lam_of(lam2.imag)


def _conv_kernel(x_ref, w_ref, b_ref, o_ref, *, l):
    x = x_ref[...].astype(F32)
    rows = lax.broadcasted_iota(jnp.int32, x.shape, 0)
    half = SSD_CONV // 2
    acc = x * w_ref[half:half + 1, :] + b_ref[...]
    for k in range(SSD_CONV):
        if k == half:
            continue
        off = k - half
        xs = pltpu.roll(x, shift=(-off) % l, axis=0)
        valid = jnp.logical_and(rows + off >= 0, rows + off < l)
        acc = acc + jnp.where(valid, xs, 0.0) * w_ref[k:k + 1, :]
    o_ref[...] = _silu(acc).astype(o_ref.dtype)


def _conv_call(proj, w, bias, col0):
    b, l, _ = proj.shape
    c = w.shape[1]
    tc = 256
    off = col0 // tc
    return pl.pallas_call(
        functools.partial(_conv_kernel, l=l), grid=(b, c // tc),
        in_specs=[pl.BlockSpec((None, l, tc), lambda i, j: (i, 0, j + off)),
                  pl.BlockSpec((SSD_CONV, tc), lambda i, j: (0, j)),
                  pl.BlockSpec((1, tc), lambda i, j: (0, j))],
        out_specs=pl.BlockSpec((None, l, tc), lambda i, j: (i, 0, j)),
        out_shape=jax.ShapeDtypeStruct((b, l, c), BF16),
        compiler_params=_cp(("parallel", "parallel"), 40), name="ssd_conv")(proj, w, bias.reshape(1, c))


def _softplus(x):
    return jnp.maximum(x, 0.0) + jnp.log1p(jnp.exp(-jnp.abs(x)))


def _ssd_dir(xbc_ref, dt_ref, y_ref, h_ref, bias, a_neg, d, *, nh, ng, need_y):
    qn = SSD_CHUNK
    hd = SSD_HEAD_DIM
    w = nh * hd
    gw = w // ng
    gn = ng * SSD_STATE
    ii = lax.broadcasted_iota(jnp.int32, (qn, qn), 0)
    jj = lax.broadcasted_iota(jnp.int32, (qn, qn), 1)
    mask = (jj <= ii) if d == 0 else (jj >= ii)
    lmat = mask.astype(F32)
    dtv = _softplus(dt_ref[...] + bias)
    cum = jnp.dot(lmat, dtv * a_neg, precision=lax.Precision.HIGHEST, preferred_element_type=F32)
    cum_t = cum.T
    edge = qn - 1 if d == 0 else 0
    tot = cum[edge:edge + 1, :]
    dt_t = dtv.T
    wt_t = dt_t * jnp.exp(cum_t[:, edge:edge + 1] - cum_t)
    decay = jnp.exp(tot)
    lane = lax.broadcasted_iota(jnp.int32, (qn, LANE), 1)
    left = lane < hd
    zero = jnp.zeros((), BF16)
    for g in range(ng):
        bg = xbc_ref[:, w + g * SSD_STATE:w + (g + 1) * SSD_STATE]
        cg = xbc_ref[:, w + gn + g * SSD_STATE:w + gn + (g + 1) * SSD_STATE]
        bg_t = bg.astype(F32).T
        s_in = h_ref[d, :, g * gw:(g + 1) * gw]
        if need_y:
            cb = lax.dot_general(cg, bg, (((1,), (1,)), ((), ())), preferred_element_type=F32)
            yoff = jnp.dot(cg, s_in.astype(BF16), preferred_element_type=F32)
        for pr in range(gw // LANE):
            c0 = d * nh + (g * gw) // hd + 2 * pr
            col = g * gw + pr * LANE
            xp = xbc_ref[:, col:col + LANE]
            r = jnp.concatenate([jnp.where(left, xp, zero), jnp.where(left, zero, xp)], axis=0)
            tops, bots, cols = [], [], []
            for c in (c0, c0 + 1):
                bots.append((bg_t * wt_t[c:c + 1, :]).astype(BF16))
                if need_y:
                    ccol = jnp.broadcast_to(cum[:, c:c + 1], (qn, qn))
                    seg = jnp.where(mask, jnp.exp(ccol - cum_t[c:c + 1, :]), 0.0)
                    tops.append((cb * seg * dt_t[c:c + 1, :]).astype(BF16))
                    cols.append(ccol)
            dec = jnp.where(left[0:1], decay[:, c0:c0 + 1], decay[:, c0 + 1:c0 + 2])
            s_old = s_in[:, pr * LANE:(pr + 1) * LANE]
            if need_y:
                lhs = jnp.concatenate([jnp.concatenate(tops, axis=1), jnp.concatenate(bots, axis=1)], axis=0)
                out = jnp.dot(lhs, r, preferred_element_type=F32)
                ec = jnp.exp(jnp.where(left, cols[0], cols[1]))
                y_ref[:, col:col + LANE] = (out[:qn] + yoff[:, pr * LANE:(pr + 1) * LANE] * ec).astype(y_ref.dtype)
                s_new = out[qn:]
            else:
                s_new = jnp.dot(jnp.concatenate(bots, axis=1), r, preferred_element_type=F32)
            h_ref[d, :, col:col + LANE] = s_old * dec + s_new


def _ssd_kernel(*refs, nh, ng, need_y):
    if need_y:
        xf_ref, xb_ref, dtf_ref, dtb_ref, h0_ref, bias_ref, alog_ref, yf_ref, yb_ref, h_ref = refs
    else:
        xf_ref, xb_ref, dtf_ref, dtb_ref, h0_ref, bias_ref, alog_ref, h_ref = refs
        yf_ref = yb_ref = None

    @pl.when(pl.program_id(1) == 0)
    def _():
        h_ref[...] = h0_ref[...]

    bias = bias_ref[...]
    a_neg = -jnp.exp(alog_ref[...])
    _ssd_dir(xf_ref, dtf_ref, yf_ref, h_ref, bias, a_neg, 0, nh=nh, ng=ng, need_y=need_y)
    _ssd_dir(xb_ref, dtb_ref, yb_ref, h_ref, bias, a_neg, 1, nh=nh, ng=ng, need_y=need_y)


def _ssd_call(xbc, dt, h0, bias, alog, nh, ng, need_y):
    b, l, cd = xbc.shape
    w = nh * SSD_HEAD_DIM
    nc = l // SSD_CHUNK
    q = SSD_CHUNK
    fwd = lambda i, s: (i, s, 0)
    bwd = lambda i, s: (i, nc - 1 - s, 0)
    hspec = pl.BlockSpec((None, 2, SSD_STATE, w), lambda i, s: (i, 0, 0, 0))
    in_specs = [pl.BlockSpec((None, q, cd), fwd), pl.BlockSpec((None, q, cd), bwd),
                pl.BlockSpec((None, q, LANE), fwd), pl.BlockSpec((None, q, LANE), bwd),
                hspec,
                pl.BlockSpec((1, LANE), lambda i, s: (0, 0)), pl.BlockSpec((1, LANE), lambda i, s: (0, 0))]
    out_specs = [hspec]
    out_shape = [jax.ShapeDtypeStruct((b, 2, SSD_STATE, w), F32)]
    if need_y:
        out_specs = [pl.BlockSpec((None, q, w), fwd), pl.BlockSpec((None, q, w), bwd)] + out_specs
        out_shape = [jax.ShapeDtypeStruct((b, l, w), BF16)] * 2 + out_shape
    return pl.pallas_call(
        functools.partial(_ssd_kernel, nh=nh, ng=ng, need_y=need_y), grid=(b, nc),
        in_specs=in_specs, out_specs=out_specs, out_shape=out_shape,
        compiler_params=_cp(("parallel", "arbitrary"), 48),
        name="ssd_scan" if need_y else "ssd_scan_ctx")(xbc, xbc, dt, dt, h0, bias, alog)


def _gnorm_kernel(yf_ref, yb_ref, xs_ref, z_ref, d_ref, g_ref, o_ref, t_ref, *, rows, tb, gw):
    y = d_ref[...] * xs_ref[...].astype(F32) + yf_ref[...].astype(F32) + yb_ref[...].astype(F32)
    y = y * _silu(z_ref[...].astype(F32))
    bw = y.shape[1]
    parts = []
    for g in range(bw // gw):
        yg = y[:, g * gw:(g + 1) * gw]
        parts.append(yg * lax.rsqrt(jnp.mean(yg * yg, axis=-1, keepdims=True) + EPS))
    res = jnp.concatenate(parts, axis=1) * g_ref[...]
    pitch = rows + XPOSE_PAD
    for wl in range(tb):
        for kk in range(bw // LANE):
            t_ref[kk, wl * pitch:wl * pitch + rows, :] = res[wl * rows:(wl + 1) * rows, kk * LANE:(kk + 1) * LANE]
    for r in range(rows):
        o_ref[r] = jnp.concatenate(
            [t_ref[kk, pl.ds(r, tb, stride=pitch), :] for kk in range(bw // LANE)], axis=1).astype(o_ref.dtype)


def _gnorm_call(yf, yb, xbc, proj, dvec, gvec, ng):
    b, l, w = yf.shape
    rows = l // GRID_W
    tb = XPOSE_TILE
    gw = w // ng
    bw = gw * (2 if ng % 2 == 0 else 1)
    blk = pl.BlockSpec((None, tb * rows, bw), lambda i, s, g: (i, s, g))
    vec = pl.BlockSpec((1, bw), lambda i, s, g: (0, g))
    out = pl.pallas_call(
        functools.partial(_gnorm_kernel, rows=rows, tb=tb, gw=gw), grid=(b, GRID_W // tb, w // bw),
        in_specs=[blk, blk, blk, blk, vec, vec],
        out_specs=pl.BlockSpec((None, rows, tb, bw), lambda i, s, g: (i, 0, s, g)),
        out_shape=jax.ShapeDtypeStruct((b, rows, GRID_W, w), BF16),
        scratch_shapes=[pltpu.VMEM((bw // LANE, tb * (rows + XPOSE_PAD), LANE), F32)],
        compiler_params=_cp(("parallel", "parallel", "parallel"), 40),
        name="ssd_gnorm")(yf, yb, xbc, proj, dvec, gvec)
    return out.reshape(b * l, w)


def _moe_kernel(e_ref, b0_ref, nb_ref, *refs, nsub, nh):
    xs = refs[:nsub]
    wg_ref, wu_ref, wd_ref, o_ref, act_ref, wdb_ref = refs[nsub:]
    s = pl.program_id(0)
    t = pl.program_id(1)
    nb = nb_ref[s]

    @pl.when(jnp.logical_and(t < nh, nb > 0))
    def _():
        wg = wg_ref[...].astype(BF16)
        wu = wu_ref[...].astype(BF16)
        for k in range(nsub):
            @pl.when(k < nb)
            def _():
                x = xs[k][...]
                gt = jnp.dot(x, wg, preferred_element_type=F32)
                up = jnp.dot(x, wu, preferred_element_type=F32)
                act_ref[k, t] = (_silu(gt) * up).astype(BF16)

    @pl.when(jnp.logical_and(t >= nh, nb > 0))
    def _():
        wdb_ref[...] = wd_ref[...].astype(BF16)
        for k in range(nsub):
            rows = slice(k * MOE_BLOCK, (k + 1) * MOE_BLOCK)

            @pl.when(k < nb)
            def _():
                a = jnp.concatenate([act_ref[k, h] for h in range(nh)], axis=1)
                o_ref[rows, :] = jnp.dot(a, wdb_ref[...], preferred_element_type=F32).astype(o_ref.dtype)

            @pl.when(k >= nb)
            def _():
                o_ref[rows, :] = jnp.zeros((MOE_BLOCK, o_ref.shape[1]), o_ref.dtype)

    @pl.when(jnp.logical_and(t >= nh, nb == 0))
    def _():
        o_ref[...] = jnp.zeros(o_ref.shape, o_ref.dtype)


def _moe_call(sup_e, sup_b0, sup_nb, xs, w_gate, w_up, w_down):
    n_slots, d = xs.shape
    hid = w_gate.shape[-1]
    th = min(256, hid)
    tn = min(1024, d)
    nh = hid // th
    nt = d // tn
    ns = sup_e.shape[0]
    rsup = MOE_SUB * MOE_BLOCK

    def x_map(k):
        def index(s, t, e, b0, nb):
            sx = jnp.where(t < nh, s, jnp.minimum(s + 1, ns - 1))
            return (b0[sx * MOE_SUB + k], 0)
        return index

    def hid_idx(s, t, nb):
        return jnp.where(nb[s] > 0, jnp.minimum(t, nh - 1), nh - 1)

    def col_idx(s, t, nb):
        return jnp.where(nb[s] > 0, jnp.clip(t - nh, 0, nt - 1), nt - 1)

    gs = pltpu.PrefetchScalarGridSpec(
        num_scalar_prefetch=3, grid=(ns, nh + nt),
        in_specs=[pl.BlockSpec((MOE_BLOCK, d), x_map(k)) for k in range(MOE_SUB)] + [
            pl.BlockSpec((None, d, th), lambda s, t, e, b0, nb: (e[s], 0, hid_idx(s, t, nb))),
            pl.BlockSpec((None, d, th), lambda s, t, e, b0, nb: (e[s], 0, hid_idx(s, t, nb))),
            pl.BlockSpec((None, hid, tn), lambda s, t, e, b0, nb: (e[s], 0, col_idx(s, t, nb)))],
        out_specs=pl.BlockSpec((rsup, tn), lambda s, t, e, b0, nb: (s, jnp.clip(t - nh, 0, nt - 1))),
        scratch_shapes=[pltpu.VMEM((MOE_SUB, nh, MOE_BLOCK, th), BF16), pltpu.VMEM((hid, tn), BF16)])
    return pl.pallas_call(
        functools.partial(_moe_kernel, nsub=MOE_SUB, nh=nh), grid_spec=gs,
        out_shape=jax.ShapeDtypeStruct((ns * rsup, d), BF16),
        compiler_params=_cp(("arbitrary", "arbitrary"), 60), name="moe_experts")(
            sup_e, sup_b0, sup_nb, *([xs] * MOE_SUB), w_gate, w_up, w_down)


def _final_kernel(x_ref, ya_ref, yb_ref, w_ref, g2_ref, fg_ref, o_ref):
    wts = w_ref[...]
    moe = ya_ref[...].astype(F32) * wts[:, 0:1] + yb_ref[...].astype(F32) * wts[:, 1:2]
    x = x_ref[...] + g2_ref[...] * moe
    ms = jnp.mean(x * x, axis=-1, keepdims=True)
    o_ref[...] = x * lax.rsqrt(ms + EPS) * fg_ref[...]


def _final_call(x, ya, yb, wts, mods, ig, fg):
    b, l, d = x.shape
    tr = min(256, l)
    nt = l // tr
    row = lambda i, r: (i * nt + r, 0)
    return pl.pallas_call(
        _final_kernel, grid=(b, nt),
        in_specs=[pl.BlockSpec((None, tr, d), lambda i, r: (i, r, 0)),
                  pl.BlockSpec((tr, d), row), pl.BlockSpec((tr, d), row),
                  pl.BlockSpec((tr, MOE_TOP_K), row),
                  pl.BlockSpec((None, None, 1, d), lambda i, r: (i, ig, 0, 0)),
                  pl.BlockSpec((1, d), lambda i, r: (0, 0))],
        out_specs=pl.BlockSpec((None, tr, d), lambda i, r: (i, r, 0)),
        out_shape=jax.ShapeDtypeStruct((b, l, d), F32),
        compiler_params=_cp(("parallel", "parallel"), 40), name="combine_final")(x, ya, yb, wts, mods, fg)


def _route(sel, ne):
    m = sel.shape[0]
    weights = sel[:, MOE_TOP_K:2 * MOE_TOP_K]
    n_assign = m * MOE_TOP_K
    flat_e = jnp.concatenate([sel[:, k] for k in range(MOE_TOP_K)], axis=0).astype(jnp.int32)
    onehot = (flat_e[:, None] == jnp.arange(ne, dtype=flat_e.dtype)[None, :]).astype(jnp.int32)
    csum = jnp.cumsum(onehot, axis=0)
    counts = csum[-1]
    local = jnp.sum(csum * onehot, axis=1) - 1
    nblk_e = (counts + MOE_BLOCK - 1) // MOE_BLOCK
    padded = nblk_e * MOE_BLOCK
    pad_end = jnp.cumsum(padded)
    pad_start = pad_end - padded
    dest = pad_start[flat_e] + local
    n_blocks = -(-(n_assign + ne * (MOE_BLOCK - 1)) // MOE_BLOCK)
    tok = jnp.arange(n_assign, dtype=jnp.int32) % m
    n_slots = n_blocks * MOE_BLOCK
    slot_tok = (jnp.arange(n_slots, dtype=jnp.int32) % m).at[dest].set(tok)
    rsup = MOE_SUB * MOE_BLOCK
    ns_e = (nblk_e + MOE_SUB - 1) // MOE_SUB
    sup_end = jnp.cumsum(ns_e)
    sup_start = sup_end - ns_e
    n_sup = (n_blocks + (MOE_SUB - 1) * ne) // MOE_SUB
    sidx = jnp.arange(n_sup, dtype=jnp.int32)
    last = sup_end[-1] - 1
    s_eff = jnp.minimum(sidx, last)
    e_s = jnp.minimum(jnp.searchsorted(sup_end, s_eff, side='right'), ne - 1).astype(jnp.int32)
    k_s = s_eff - sup_start[e_s]
    b0_s = pad_start[e_s] // MOE_BLOCK + MOE_SUB * k_s
    nb_s = jnp.clip(nblk_e[e_s] - MOE_SUB * k_s, 0, MOE_SUB)
    used = sidx <= last
    sup_nb = jnp.where(used, nb_s, 0).astype(jnp.int32)
    kk = jnp.arange(MOE_SUB, dtype=jnp.int32)[None, :]
    sup_b0 = jnp.maximum(lax.cummax(jnp.where(kk < sup_nb[:, None], b0_s[:, None] + kk, -1), axis=0), 0)
    sup_b0 = sup_b0.reshape(-1).astype(jnp.int32)
    pos = ((sup_start[flat_e] + local // rsup) * rsup + local % rsup).astype(jnp.int32)
    return weights, slot_tok, e_s, sup_b0, sup_nb, pos


def kernel(x, c, ctx, c_ctx, w_mod, b_mod, norm1_g, w_in, s5_lam_re, s5_lam_im, s5_log_dt, s5_b_re, s5_b_im, s5_c_re, s5_c_im, s5_d, s5_w_val, s5_w_gate, ssd_conv_w, ssd_conv_b, ssd_a_log, ssd_dt_bias, ssd_d, ssd_norm_g, ssd_w_out, w_o, norm2_g, moe_w_group, moe_b_group, moe_w_expert, moe_b_expert, moe_w_gate, moe_w_up, moe_w_down, final_g):
    depth = w_mod.shape[0]
    assert depth == 1, "single-layer block"
    bsz, n_lat, d = x.shape
    l_ctx = ctx.shape[1]
    w5 = s5_d.shape[1]
    nh = ssd_d.shape[1]
    w = nh * SSD_HEAD_DIM
    conv_dim = ssd_conv_w.shape[2]
    ng = (conv_dim - w) // (2 * SSD_STATE)
    ssd_in = w + conv_dim + 2 * nh
    o1, o2 = w5, w5 + ssd_in
    l = 0

    cc = jnp.concatenate([c, c_ctx[None, :]], axis=0)
    cc = jnp.pad(cc, ((0, (-cc.shape[0]) % 8), (0, 0)))
    mods = _mod_call(cc, w_mod[l], b_mod[l]).reshape(cc.shape[0], 6, 1, d)
    i_sh1, i_sc1, i_g1, i_sh2, i_sc2, i_g2 = range(6)

    w_in_l = w_in[l]
    o_dt = o1 + w + conv_dim

    hn_rm, hn_cm = _norm_lat_call(x, norm1_g[l], mods, i_sc1, i_sh1)
    hc = _norm_ctx_call(ctx, norm1_g[l], mods, bsz, i_sc1, i_sh1)
    hn_rm = hn_rm.reshape(bsz * n_lat, d)
    hn_cm = hn_cm.reshape(bsz * n_lat, d)
    hc = hc.reshape(bsz * l_ctx, d)

    u_lat = _mm_call(hn_rm, w_in_l, BF16, "in_s5", 0, o1).reshape(bsz, n_lat, w5)
    u_ctx = _mm_call(hc, w_in_l, BF16, "in_s5_ctx", 0, o1).reshape(bsz, l_ctx, w5)
    gates = _mm_call(hn_rm, w_in_l, BF16, "in_gates", o2, 2 * d)
    p_lat = _mm_call(hn_cm, w_in_l, BF16, "in_ssd", o1, w + conv_dim).reshape(bsz, n_lat, w + conv_dim)
    p_ctx = _mm_call(hc, w_in_l, BF16, "in_ssd_ctx", o1, w + conv_dim).reshape(bsz, l_ctx, w + conv_dim)
    dt_lat = _mm_call(hn_cm, w_in_l, F32, "in_dt", o_dt, 2 * nh).reshape(bsz, n_lat, LANE)
    dt_ctx = _mm_call(hc, w_in_l, F32, "in_dt_ctx", o_dt, 2 * nh).reshape(bsz, l_ctx, LANE)

    nj = w5 // LANE
    bmat, cmat, dmat, lre, lim = _s5_params(s5_lam_re[l], s5_lam_im[l], s5_log_dt[l], s5_b_re[l], s5_b_im[l],
                                            s5_c_re[l], s5_c_im[l], bsz)
    s5_zero = jnp.zeros((nj, 2, 2 * bsz, lre.shape[-1]), F32)
    (s5_ctx,) = _s5_call(u_ctx, bmat, cmat, dmat, lre, lim, s5_zero, False)
    ya_f, ya_b, _ = _s5_call(u_lat, bmat, cmat, dmat, lre, lim, s5_ctx, True)

    xbc_ctx = _conv_call(p_ctx, ssd_conv_w[l], ssd_conv_b[l], w)
    xbc_lat = _conv_call(p_lat, ssd_conv_w[l], ssd_conv_b[l], w)
    pad_h = LANE - 2 * nh
    bias = jnp.pad(ssd_dt_bias[l].astype(F32).reshape(1, 2 * nh), ((0, 0), (0, pad_h)))
    alog = jnp.pad(ssd_a_log[l].astype(F32).reshape(1, 2 * nh), ((0, 0), (0, pad_h)))
    h_zero = jnp.zeros((bsz, 2, SSD_STATE, w), F32)
    (h_ctx,) = _ssd_call(xbc_ctx, dt_ctx, h_zero, bias, alog, nh, ng, False)
    y_f, y_b, _ = _ssd_call(xbc_lat, dt_lat, h_ctx, bias, alog, nh, ng, True)
    d_vec = jnp.repeat(ssd_d[l].astype(F32), SSD_HEAD_DIM).reshape(1, w)
    y_ssd = _gnorm_call(y_f, y_b, xbc_lat, p_lat, d_vec, ssd_norm_g[l].astype(F32).reshape(1, w), ng)

    m_lat = bsz * n_lat
    part_a = _glu_call(u_lat.reshape(m_lat, w5), ya_f.reshape(m_lat, w5), ya_b.reshape(m_lat, w5),
                       s5_d[l].astype(F32).reshape(1, w5),
                       s5_w_val[l].astype(BF16), s5_w_gate[l].astype(BF16), gates)
    merged = _merge_call(y_ssd, ssd_w_out[l], gates, part_a)
    x1 = _resid_call(merged, w_o[l], x, mods, i_g1)

    ngr = moe_w_group.shape[-1]
    ne = moe_w_expert.shape[-1]
    wr = jnp.concatenate([moe_w_group[l], moe_w_expert[l]], axis=1).astype(F32)
    wr = jnp.pad(wr, ((0, 0), (0, (-(ngr + ne)) % LANE)))
    br = jnp.concatenate([moe_b_group[l], moe_b_expert[l]]).astype(F32)
    br = jnp.pad(br, (0, (-(ngr + ne)) % LANE)).reshape(1, -1)
    hx, sel = _norm_router_call(x1, norm2_g[l], mods, i_sc2, i_sh2, wr, br, ngr, ne)
    m = bsz * n_lat
    hx = hx.reshape(m, d)
    weights, slot_tok, sup_e, sup_b0, sup_nb, pos = _route(sel.reshape(m, -1), ne)
    xs = hx[slot_tok]
    y_slots = _moe_call(sup_e, sup_b0, sup_nb, xs, moe_w_gate[l], moe_w_up[l], moe_w_down[l])
    ya = y_slots[pos[:m]]
    yb = y_slots[pos[m:]]
    return _final_call(x1, ya, yb, weights.astype(F32), mods, i_g2, final_g.reshape(1, d))
```

```python
import functools
import math

import jax
import jax.numpy as jnp
from jax import lax
from jax.experimental import pallas as pl
from jax.experimental.pallas import tpu as pltpu

F32 = jnp.float32
BF16 = jnp.bfloat16

GRID_W = 64
EPS = 1e-6
LANE = 128
SSD_HEAD_DIM = 64
SSD_STATE = 128
SSD_CHUNK = 128
SSD_CONV = 5
MOE_TOP_K = 2
MOE_BLOCK = 256
MOE_SUB = 4
XPOSE_TILE = 16
XPOSE_PAD = 8
S5_CHUNK = 256
S5_ROW_PAD = 8


def _cp(sem, mb):
    return pltpu.CompilerParams(dimension_semantics=sem, vmem_limit_bytes=mb * 1024 * 1024)


def _sigmoid(x):
    return 1.0 / (1.0 + jnp.exp(-x))


def _silu(x):
    return x * _sigmoid(x)


def _mod_kernel(c_ref, w_ref, b_ref, o_ref):
    s = _silu(c_ref[...])
    o_ref[...] = jnp.dot(s.astype(BF16), w_ref[...].astype(BF16),
                         preferred_element_type=F32) + b_ref[...]


def _mod_call(cc, w, b):
    r, d = cc.shape
    n = w.shape[1]
    tn = min(512, n)
    return pl.pallas_call(
        _mod_kernel, grid=(n // tn,),
        in_specs=[pl.BlockSpec((r, d), lambda j: (0, 0)),
                  pl.BlockSpec((d, tn), lambda j: (0, j)),
                  pl.BlockSpec((1, tn), lambda j: (0, j))],
        out_specs=pl.BlockSpec((r, tn), lambda j: (0, j)),
        out_shape=jax.ShapeDtypeStruct((r, n), F32),
        compiler_params=_cp(("parallel",), 40), name="mod")(cc, w, b.reshape(1, n))


def _rmsmod(x, g, sc, sh):
    ms = jnp.mean(x * x, axis=-1, keepdims=True)
    return (x * lax.rsqrt(ms + EPS) * g) * (1.0 + sc) + sh


def _norm_lat_kernel(x_ref, g_ref, sc_ref, sh_ref, orm_ref, ocm_ref, t_ref, *, tb, d):
    pitch = tb + XPOSE_PAD
    for r in range(tb):
        y = _rmsmod(x_ref[r], g_ref[...], sc_ref[...], sh_ref[...])
        orm_ref[r] = y.astype(BF16)
        for kk in range(d // LANE):
            t_ref[kk, r * pitch:r * pitch + tb, :] = y[:, kk * LANE:(kk + 1) * LANE]
    for wl in range(tb):
        ocm_ref[wl] = jnp.concatenate(
            [t_ref[kk, pl.ds(wl, tb, stride=pitch), :] for kk in range(d // LANE)], axis=1).astype(BF16)


def _norm_lat_call(x, g, mods, isc, ish):
    b, l, d = x.shape
    rows = l // GRID_W
    tb = XPOSE_TILE
    xv = x.reshape(b, rows, GRID_W, d)
    orm, ocm = pl.pallas_call(
        functools.partial(_norm_lat_kernel, tb=tb, d=d), grid=(b, rows // tb, GRID_W // tb),
        in_specs=[pl.BlockSpec((None, tb, tb, d), lambda i, r, c: (i, r, c, 0)),
                  pl.BlockSpec((1, d), lambda i, r, c: (0, 0)),
                  pl.BlockSpec((None, None, 1, d), lambda i, r, c: (i, isc, 0, 0)),
                  pl.BlockSpec((None, None, 1, d), lambda i, r, c: (i, ish, 0, 0))],
        out_specs=[pl.BlockSpec((None, tb, tb, d), lambda i, r, c: (i, r, c, 0)),
                   pl.BlockSpec((None, tb, tb, d), lambda i, r, c: (i, c, r, 0))],
        out_shape=[jax.ShapeDtypeStruct((b, rows, GRID_W, d), BF16),
                   jax.ShapeDtypeStruct((b, GRID_W, rows, d), BF16)],
        scratch_shapes=[pltpu.VMEM((d // LANE, tb * (tb + XPOSE_PAD), LANE), F32)],
        compiler_params=_cp(("parallel", "parallel", "parallel"), 48),
        name="norm1_lat")(xv, g.reshape(1, d), mods, mods)
    return orm.reshape(b, l, d), ocm.reshape(b, l, d)


def _norm_kernel(x_ref, g_ref, sc_ref, sh_ref, o_ref):
    o_ref[...] = _rmsmod(x_ref[...], g_ref[...], sc_ref[...], sh_ref[...]).astype(o_ref.dtype)


def _norm_ctx_call(x, g, mods, row, isc, ish):
    b, l, d = x.shape
    tr = min(256, l)
    return pl.pallas_call(
        _norm_kernel, grid=(b, l // tr),
        in_specs=[pl.BlockSpec((None, tr, d), lambda i, r: (i, r, 0)),
                  pl.BlockSpec((1, d), lambda i, r: (0, 0)),
                  pl.BlockSpec((None, None, 1, d), lambda i, r: (row, isc, 0, 0)),
                  pl.BlockSpec((None, None, 1, d), lambda i, r: (row, ish, 0, 0))],
        out_specs=pl.BlockSpec((None, tr, d), lambda i, r: (i, r, 0)),
        out_shape=jax.ShapeDtypeStruct((b, l, d), BF16),
        compiler_params=_cp(("parallel", "parallel"), 40), name="norm1_ctx")(x, g.reshape(1, d), mods, mods)


def _first_argmax(v, vmax, lane):
    return jnp.min(jnp.where(v == vmax, lane, float(LANE)), axis=-1, keepdims=True)


def _norm_router_kernel(x_ref, g_ref, sc_ref, sh_ref, wr_ref, br_ref, o_ref, sel_ref, *, ngr, ne):
    y = _rmsmod(x_ref[...], g_ref[...], sc_ref[...], sh_ref[...])
    o_ref[...] = y.astype(BF16)
    y_hi = y.astype(BF16)
    y_lo = (y - y_hi.astype(F32)).astype(BF16)
    w_hi = wr_ref[0]
    lg = (jnp.dot(y_hi, w_hi, preferred_element_type=F32) + jnp.dot(y_hi, wr_ref[1], preferred_element_type=F32)
          + jnp.dot(y_lo, w_hi, preferred_element_type=F32)) + br_ref[...]
    epg = ne // ngr
    lane = lax.broadcasted_iota(jnp.int32, lg.shape, 1).astype(F32)
    ninf = -jnp.inf
    gl = jnp.where(lane < ngr, lg, ninf)
    ge = jnp.exp(gl - jnp.max(gl, axis=-1, keepdims=True))
    g_prob = ge / jnp.sum(ge, axis=-1, keepdims=True)
    g_p = jnp.max(g_prob, axis=-1, keepdims=True)
    lo = ngr + epg * _first_argmax(g_prob, g_p, lane)
    cand = jnp.where(jnp.logical_and(lane >= lo, lane < lo + epg), lg, ninf)
    v1 = jnp.max(cand, axis=-1, keepdims=True)
    l1 = _first_argmax(cand, v1, lane)
    rest = jnp.where(lane == l1, ninf, cand)
    v2 = jnp.max(rest, axis=-1, keepdims=True)
    l2 = _first_argmax(rest, v2, lane)
    ex = jnp.exp(v2 - v1)
    w1 = g_p * (1.0 / (1.0 + ex))
    w2 = g_p * (ex / (1.0 + ex))
    sel_ref[...] = jnp.where(lane == 0, l1 - ngr, jnp.where(lane == 1, l2 - ngr,
                             jnp.where(lane == 2, w1, jnp.where(lane == 3, w2, 0.0))))


def _norm_router_call(x, g, mods, isc, ish, wr, br, ngr, ne):
    b, l, d = x.shape
    tr = min(256, l)
    nr = wr.shape[1]
    w_hi = wr.astype(BF16)
    wr = jnp.stack([w_hi, (wr - w_hi.astype(F32)).astype(BF16)], axis=0)
    return pl.pallas_call(
        functools.partial(_norm_router_kernel, ngr=ngr, ne=ne), grid=(b, l // tr),
        in_specs=[pl.BlockSpec((None, tr, d), lambda i, r: (i, r, 0)),
                  pl.BlockSpec((1, d), lambda i, r: (0, 0)),
                  pl.BlockSpec((None, None, 1, d), lambda i, r: (i, isc, 0, 0)),
                  pl.BlockSpec((None, None, 1, d), lambda i, r: (i, ish, 0, 0)),
                  pl.BlockSpec((2, d, nr), lambda i, r: (0, 0, 0)),
                  pl.BlockSpec((1, nr), lambda i, r: (0, 0))],
        out_specs=[pl.BlockSpec((None, tr, d), lambda i, r: (i, r, 0)),
                   pl.BlockSpec((None, tr, nr), lambda i, r: (i, r, 0))],
        out_shape=[jax.ShapeDtypeStruct((b, l, d), BF16), jax.ShapeDtypeStruct((b, l, nr), F32)],
        compiler_params=_cp(("parallel", "parallel"), 40),
        name="norm2_router")(x, g.reshape(1, d), mods, mods, wr, br)


def _mm_kernel(a_ref, b_ref, o_ref):
    o_ref[...] = jnp.dot(a_ref[...], b_ref[...].astype(BF16), preferred_element_type=F32).astype(o_ref.dtype)


def _mm_tiles(m, n):
    tm = min(1024, m)
    tn = min(512, n)
    return tm, tn


def _mm_call(a, b, out_dtype, name, col0=0, n=None):
    m, k = a.shape
    n = b.shape[1] - col0 if n is None else n
    tm, tn = _mm_tiles(m, n)
    if col0 % LANE or n % tn or n % LANE:
        b = b[:, col0:col0 + n]
        pad = (-n) % LANE
        b = jnp.pad(b, ((0, 0), (0, pad)))
        n, col0 = n + pad, 0
        tm, tn = _mm_tiles(m, n)
    return pl.pallas_call(
        _mm_kernel, grid=(m // tm, n // tn),
        in_specs=[pl.BlockSpec((tm, k), lambda i, j: (i, 0)),
                  pl.BlockSpec((pl.Element(k), pl.Element(tn)), lambda i, j: (0, (col0 // LANE + j * (tn // LANE)) * LANE))],
        out_specs=pl.BlockSpec((tm, tn), lambda i, j: (i, j)),
        out_shape=jax.ShapeDtypeStruct((m, n), out_dtype),
        compiler_params=_cp(("parallel", "parallel"), 48), name=name)(a, b)


def _gelu_tanh(x):
    return x * (0.5 * (1.0 + jnp.tanh(math.sqrt(2.0 / math.pi) * (x + 0.044715 * (x * x * x)))))


def _glu_kernel(u_ref, yf_ref, yb_ref, d_ref, wv_ref, wg_ref, gate_ref, o_ref, a_ref):
    @pl.when(pl.program_id(1) == 0)
    def _():
        y = d_ref[...] * u_ref[...].astype(F32) + yf_ref[...].astype(F32) + yb_ref[...].astype(F32)
        a_ref[...] = _gelu_tanh(y).astype(BF16)

    a = a_ref[...]
    val = jnp.dot(a, wv_ref[...], preferred_element_type=F32)
    gl = jnp.dot(a, wg_ref[...], preferred_element_type=F32)
    o_ref[...] = (_sigmoid(gate_ref[...].astype(F32)) * (val * _sigmoid(gl))).astype(o_ref.dtype)


def _glu_call(u, yf, yb, dvec, wv, wg, gates):
    m, k = u.shape
    n = wv.shape[1]
    tm, tn = _mm_tiles(m, n)
    row = pl.BlockSpec((tm, k), lambda i, j: (i, 0))
    return pl.pallas_call(
        _glu_kernel, grid=(m // tm, n // tn),
        in_specs=[row, row, row,
                  pl.BlockSpec((1, k), lambda i, j: (0, 0)),
                  pl.BlockSpec((k, tn), lambda i, j: (0, j)),
                  pl.BlockSpec((k, tn), lambda i, j: (0, j)),
                  pl.BlockSpec((tm, tn), lambda i, j: (i, j))],
        out_specs=pl.BlockSpec((tm, tn), lambda i, j: (i, j)),
        out_shape=jax.ShapeDtypeStruct((m, n), BF16),
        scratch_shapes=[pltpu.VMEM((tm, k), BF16)],
        compiler_params=_cp(("parallel", "arbitrary"), 56), name="s5_glu")(u, yf, yb, dvec, wv, wg, gates)


def _merge_kernel(a_ref, w_ref, gate_ref, pa_ref, o_ref):
    br = jnp.dot(a_ref[...], w_ref[...].astype(BF16), preferred_element_type=F32)
    o_ref[...] = (pa_ref[...].astype(F32) + _sigmoid(gate_ref[...].astype(F32)) * br).astype(o_ref.dtype)


def _merge_call(a, w, gates, part_a):
    m, k = a.shape
    n = w.shape[1]
    tm, tn = _mm_tiles(m, n)
    off = n // tn
    return pl.pallas_call(
        _merge_kernel, grid=(m // tm, n // tn),
        in_specs=[pl.BlockSpec((tm, k), lambda i, j: (i, 0)),
                  pl.BlockSpec((k, tn), lambda i, j: (0, j)),
                  pl.BlockSpec((tm, tn), lambda i, j: (i, j + off)),
                  pl.BlockSpec((tm, tn), lambda i, j: (i, j))],
        out_specs=pl.BlockSpec((tm, tn), lambda i, j: (i, j)),
        out_shape=jax.ShapeDtypeStruct((m, n), BF16),
        compiler_params=_cp(("parallel", "parallel"), 48), name="ssd_out_merge")(a, w, gates, part_a)


def _resid_kernel(a_ref, w_ref, x_ref, g_ref, o_ref):
    mix = jnp.dot(a_ref[...], w_ref[...].astype(BF16), preferred_element_type=F32)
    o_ref[...] = x_ref[...] + g_ref[...] * mix


def _resid_call(a, w, x, mods, ig):
    b, l, d = x.shape
    k = a.shape[1]
    tm, tn = _mm_tiles(l, d)
    nt = l // tm
    return pl.pallas_call(
        _resid_kernel, grid=(b * nt, d // tn),
        in_specs=[pl.BlockSpec((tm, k), lambda i, j: (i, 0)),
                  pl.BlockSpec((k, tn), lambda i, j: (0, j)),
                  pl.BlockSpec((None, tm, tn), lambda i, j: (i // nt, i % nt, j)),
                  pl.BlockSpec((None, None, 1, tn), lambda i, j: (i // nt, ig, 0, j))],
        out_specs=pl.BlockSpec((None, tm, tn), lambda i, j: (i // nt, i % nt, j)),
        out_shape=jax.ShapeDtypeStruct((b, l, d), F32),
        compiler_params=_cp(("parallel", "parallel"), 48), name="w_o_resid")(a, w, x, mods)


def _s5_kernel(*refs, nb, tp, p8, need_y):
    if need_y:
        uf_ref, ub_ref, b_ref, c_ref, d_ref, lre_ref, lim_ref, h0_ref, yf_ref, yb_ref, h_ref, buf_ref, il_ref = refs
    else:
        uf_ref, ub_ref, b_ref, c_ref, d_ref, lre_ref, lim_ref, h0_ref, h_ref, buf_ref, il_ref = refs
    q = 2 * nb
    nk = p8 // LANE
    pitch = tp + S5_ROW_PAD

    @pl.when(pl.program_id(1) == 0)
    def _():
        h_ref[...] = h0_ref[...]

    rev = (lax.broadcasted_iota(jnp.int32, (tp, tp), 0) + lax.broadcasted_iota(jnp.int32, (tp, tp), 1)
           == tp - 1).astype(BF16)
    def pair_rows(u_ref, b, newer_first):
        il_ref[...] = u_ref[b].astype(F32)
        even = il_ref[pl.ds(0, tp, stride=2), :].astype(BF16)
        odd = il_ref[pl.ds(1, tp, stride=2), :].astype(BF16)
        return jnp.concatenate([odd, even] if newer_first else [even, odd], axis=1)

    lhs = []
    for d in range(2):
        if d == 0:
            u = jnp.concatenate([pair_rows(uf_ref, b, False) for b in range(nb)], axis=0)
        else:
            u = jnp.concatenate(
                [jnp.dot(rev, pair_rows(ub_ref, b, True), preferred_element_type=F32).astype(BF16)
                 for b in range(nb)], axis=0)
        lhs.append(u)
        bu = jnp.dot(u, b_ref[d], preferred_element_type=F32)
        for b in range(nb):
            r0 = (d * nb + b) * pitch
            for k in range(2 * nk):
                buf_ref[k, r0:r0 + tp, :] = bu[b * tp:(b + 1) * tp, k * LANE:(k + 1) * LANE]
    ar = [lre_ref[:, k * LANE:(k + 1) * LANE] for k in range(nk)]
    ai = [lim_ref[:, k * LANE:(k + 1) * LANE] for k in range(nk)]

    def step(s, carry):
        rows = pl.ds(s, q, stride=pitch)
        out = []
        for k in range(nk):
            hr, hi = carry[k]
            nr = ar[k] * hr - ai[k] * hi + buf_ref[k, rows, :]
            ni = ar[k] * hi + ai[k] * hr + buf_ref[nk + k, rows, :]
            buf_ref[k, rows, :] = hr
            buf_ref[nk + k, rows, :] = hi
            out.append((nr, ni))
        return tuple(out)

    init = tuple((h_ref[0, :, k * LANE:(k + 1) * LANE], h_ref[1, :, k * LANE:(k + 1) * LANE]) for k in range(nk))
    fin = lax.fori_loop(0, tp, step, init, unroll=8)
    for k in range(nk):
        h_ref[0, :, k * LANE:(k + 1) * LANE] = fin[k][0]
        h_ref[1, :, k * LANE:(k + 1) * LANE] = fin[k][1]

    if need_y:
        for d in range(2):
            h = jnp.concatenate(
                [jnp.concatenate([buf_ref[k, (d * nb + b) * pitch:(d * nb + b) * pitch + tp, :].astype(BF16)
                                  for k in range(2 * nk)], axis=1) for b in range(nb)], axis=0)
            y = (jnp.dot(h, c_ref[d], preferred_element_type=F32)
                 + jnp.dot(lhs[d], d_ref[d], preferred_element_type=F32))
            for b in range(nb):
                yb = y[b * tp:(b + 1) * tp]
                if d == 1:
                    yb = jnp.dot(rev, yb.astype(BF16), preferred_element_type=F32)
                first, second = (0, 1) if d == 0 else (1, 0)
                il_ref[pl.ds(first, tp, stride=2), :] = yb[:, 0:LANE]
                il_ref[pl.ds(second, tp, stride=2), :] = yb[:, LANE:2 * LANE]
                (yf_ref if d == 0 else yb_ref)[b] = il_ref[...].astype(BF16)


def _s5_call(u, bmat, cmat, dmat, lre, lim, h0, need_y):
    nb, l, w5 = u.shape
    nj = w5 // LANE
    q = 2 * nb
    t = min(S5_CHUNK, l)
    tp = t // 2
    p8 = lre.shape[-1]
    nc = l // t
    hspec = pl.BlockSpec((None, 2, q, p8), lambda j, c: (j, 0, 0, 0))
    ublk = (nb, t, LANE)
    in_specs = [pl.BlockSpec(ublk, lambda j, c: (0, c, j)), pl.BlockSpec(ublk, lambda j, c: (0, nc - 1 - c, j)),
                pl.BlockSpec((None, 2, 2 * LANE, 2 * p8), lambda j, c: (j, 0, 0, 0)),
                pl.BlockSpec((None, 2, 2 * p8, 2 * LANE), lambda j, c: (j, 0, 0, 0)),
                pl.BlockSpec((None, 2, 2 * LANE, 2 * LANE), lambda j, c: (j, 0, 0, 0)),
                pl.BlockSpec((None, q, p8), lambda j, c: (j, 0, 0)),
                pl.BlockSpec((None, q, p8), lambda j, c: (j, 0, 0)),
                hspec]
    out_specs = [hspec]
    out_shape = [jax.ShapeDtypeStruct((nj, 2, q, p8), F32)]
    scratch = [pltpu.VMEM((2 * p8 // LANE, q * (tp + S5_ROW_PAD), LANE), F32), pltpu.VMEM((t, LANE), F32)]
    if need_y:
        out_specs = [pl.BlockSpec(ublk, lambda j, c: (0, c, j)),
                     pl.BlockSpec(ublk, lambda j, c: (0, nc - 1 - c, j))] + out_specs
        out_shape = [jax.ShapeDtypeStruct((nb, l, w5), BF16)] * 2 + out_shape
    kern = functools.partial(_s5_kernel, nb=nb, tp=tp, p8=p8, need_y=need_y)
    return pl.pallas_call(
        kern, grid=(nj, nc), in_specs=in_specs, out_specs=out_specs, out_shape=out_shape,
        scratch_shapes=scratch, compiler_params=_cp(("parallel", "arbitrary"), 48),
        name="s5_scan" if need_y else "s5_scan_ctx")(u, u, bmat, cmat, dmat, lre, lim, h0)


def _s5_params(lam_re, lam_im, log_dt, b_re, b_im, c_re, c_im, nb):
    _, g, p = lam_re.shape
    s = b_re.shape[-1]
    gpb = LANE // s
    nj = g // gpb
    lam = lax.complex(lam_re.astype(F32), lam_im.astype(F32))
    lam_bar = jnp.exp(lam * jnp.exp(log_dt.astype(F32))[..., None])
    b_bar = ((lam_bar - 1.0) / lam)[..., None] * lax.complex(b_re.astype(F32), b_im.astype(F32))
    def quadrants(parts):
        r, c = parts[0].shape[-2:]
        same_group = (jnp.arange(gpb * r)[:, None] // r) == (jnp.arange(gpb * c)[None, :] // c)

        def spread(x):
            return jnp.where(same_group, jnp.tile(x.reshape(2, nj, gpb * r, c), (1, 1, 1, gpb)), 0.0)

        top = jnp.concatenate([spread(parts[0]), spread(parts[1])], axis=-1)
        bot = jnp.concatenate([spread(parts[2]), spread(parts[3])], axis=-1)
        return jnp.concatenate([top, bot], axis=-2).transpose(1, 0, 2, 3).astype(BF16)

    tb = lambda z: jnp.swapaxes(z, -1, -2)
    lb = lam_bar[..., None] * b_bar
    bmat = quadrants([tb(lb.real), tb(lb.imag), tb(b_bar.real), tb(b_bar.imag)])
    cc = lax.complex(c_re.astype(F32), c_im.astype(F32))
    c_l1 = cc * lam_bar[:, :, None, :]
    c_l2 = c_l1 * lam_bar[:, :, None, :]
    cmat = quadrants([tb(c_l1.real), tb(c_l2.real), -tb(c_l1.imag), -tb(c_l2.imag)])
    m0 = jnp.einsum('dgsp,dgpt->dgts', cc, b_bar).real
    m1 = jnp.einsum('dgsp,dgpt->dgts', c_l1, b_bar).real
    dmat = quadrants([m0, m1, jnp.zeros_like(m0), m0])

    def lam_of(part):
        v = part.reshape(2, nj, gpb * p).transpose(1, 0, 2)
        return jnp.repeat(v, nb, axis=1)

    lam2 = lam_bar * lam_bar
    return bmat, cmat, dmat, lam_of(lam2.real), lam_of(lam2.imag)


def _conv_silu(x, w, bias):
    l = x.shape[0]
    rows = lax.broadcasted_iota(jnp.int32, x.shape, 0)
    half = SSD_CONV // 2
    acc = x * w[half:half + 1, :] + bias
    for k in range(SSD_CONV):
        if k == half:
            continue
        off = k - half
        xs = pltpu.roll(x, shift=(-off) % l, axis=0)
        valid = jnp.logical_and(rows + off >= 0, rows + off < l)
        acc = acc + jnp.where(valid, xs, 0.0) * w[k:k + 1, :]
    return _silu(acc)


def _conv_kernel(x_ref, w_ref, b_ref, o_ref):
    o_ref[...] = _conv_silu(x_ref[...].astype(F32), w_ref[...], b_ref[...]).astype(o_ref.dtype)


def _proj_conv_kernel(a_ref, w_ref, cw_ref, cb_ref, o_ref):
    a = a_ref[...]
    wb = w_ref[...].astype(BF16)
    hw = wb.shape[1] // 2
    for c in range(2):
        cols = slice(c * hw, (c + 1) * hw)
        p = jnp.dot(a, wb[:, cols], preferred_element_type=F32)
        o_ref[:, cols] = _conv_silu(p, cw_ref[:, cols], cb_ref[:, cols]).astype(o_ref.dtype)


def _proj_conv_call(hn, w_all, col0, cw, cb):
    b, l, d = hn.shape
    c = cw.shape[1]
    tn = 512
    return pl.pallas_call(
        _proj_conv_kernel, grid=(b, c // tn),
        in_specs=[pl.BlockSpec((None, l, d), lambda i, j: (i, 0, 0), pipeline_mode=pl.Buffered(1)),
                  pl.BlockSpec((pl.Element(d), pl.Element(tn)),
                               lambda i, j: (0, (col0 // LANE + j * (tn // LANE)) * LANE)),
                  pl.BlockSpec((SSD_CONV, tn), lambda i, j: (0, j)),
                  pl.BlockSpec((1, tn), lambda i, j: (0, j))],
        out_specs=pl.BlockSpec((None, l, tn), lambda i, j: (i, 0, j)),
        out_shape=jax.ShapeDtypeStruct((b, l, c), BF16),
        compiler_params=_cp(("parallel", "arbitrary"), 60), name="in_xbc_conv")(hn, w_all, cw, cb.reshape(1, c))


def _conv_call(proj, w, bias, col0):
    b, l, _ = proj.shape
    c = w.shape[1]
    tc = 256
    off = col0 // tc
    return pl.pallas_call(
        _conv_kernel, grid=(b, c // tc),
        in_specs=[pl.BlockSpec((None, l, tc), lambda i, j: (i, 0, j + off)),
                  pl.BlockSpec((SSD_CONV, tc), lambda i, j: (0, j)),
                  pl.BlockSpec((1, tc), lambda i, j: (0, j))],
        out_specs=pl.BlockSpec((None, l, tc), lambda i, j: (i, 0, j)),
        out_shape=jax.ShapeDtypeStruct((b, l, c), BF16),
        compiler_params=_cp(("parallel", "parallel"), 40), name="ssd_conv")(proj, w, bias.reshape(1, c))


def _softplus(x):
    return jnp.maximum(x, 0.0) + jnp.log1p(jnp.exp(-jnp.abs(x)))


def _ssd_dir(xbc_ref, dt_ref, y_ref, h_ref, bias, a_neg, d, *, nh, ng, need_y):
    qn = SSD_CHUNK
    hd = SSD_HEAD_DIM
    w = nh * hd
    gw = w // ng
    gn = ng * SSD_STATE
    ii = lax.broadcasted_iota(jnp.int32, (qn, qn), 0)
    jj = lax.broadcasted_iota(jnp.int32, (qn, qn), 1)
    mask = (jj <= ii) if d == 0 else (jj >= ii)
    lmat = mask.astype(F32)
    dtv = _softplus(dt_ref[...] + bias)
    cum = jnp.dot(lmat, dtv * a_neg, precision=lax.Precision.HIGHEST, preferred_element_type=F32)
    cum_t = cum.T
    edge = qn - 1 if d == 0 else 0
    tot = cum[edge:edge + 1, :]
    dt_t = dtv.T
    wt_t = dt_t * jnp.exp(cum_t[:, edge:edge + 1] - cum_t)
    decay = jnp.exp(tot)
    lane = lax.broadcasted_iota(jnp.int32, (qn, LANE), 1)
    left = lane < hd
    zero = jnp.zeros((), BF16)
    for g in range(ng):
        bg = xbc_ref[:, w + g * SSD_STATE:w + (g + 1) * SSD_STATE]
        cg = xbc_ref[:, w + gn + g * SSD_STATE:w + gn + (g + 1) * SSD_STATE]
        bg_t = bg.astype(F32).T
        s_in = h_ref[d, :, g * gw:(g + 1) * gw]
        if need_y:
            cb = lax.dot_general(cg, bg, (((1,), (1,)), ((), ())), preferred_element_type=F32)
            yoff = jnp.dot(cg, s_in.astype(BF16), preferred_element_type=F32)
        for pr in range(gw // LANE):
            c0 = d * nh + (g * gw) // hd + 2 * pr
            col = g * gw + pr * LANE
            xp = xbc_ref[:, col:col + LANE]
            r = jnp.concatenate([jnp.where(left, xp, zero), jnp.where(left, zero, xp)], axis=0)
            tops, bots, cols = [], [], []
            for c in (c0, c0 + 1):
                bots.append((bg_t * wt_t[c:c + 1, :]).astype(BF16))
                if need_y:
                    ccol = jnp.broadcast_to(cum[:, c:c + 1], (qn, qn))
                    seg = jnp.where(mask, jnp.exp(ccol - cum_t[c:c + 1, :]), 0.0)
                    tops.append((cb * seg * dt_t[c:c + 1, :]).astype(BF16))
                    cols.append(ccol)
            dec = jnp.where(left[0:1], decay[:, c0:c0 + 1], decay[:, c0 + 1:c0 + 2])
            s_old = s_in[:, pr * LANE:(pr + 1) * LANE]
            if need_y:
                lhs = jnp.concatenate([jnp.concatenate(tops, axis=1), jnp.concatenate(bots, axis=1)], axis=0)
                out = jnp.dot(lhs, r, preferred_element_type=F32)
                ec = jnp.exp(jnp.where(left, cols[0], cols[1]))
                y_ref[:, col:col + LANE] = (out[:qn] + yoff[:, pr * LANE:(pr + 1) * LANE] * ec).astype(y_ref.dtype)
                s_new = out[qn:]
            else:
                s_new = jnp.dot(jnp.concatenate(bots, axis=1), r, preferred_element_type=F32)
            h_ref[d, :, col:col + LANE] = s_old * dec + s_new


def _ssd_kernel(*refs, nh, ng, need_y):
    if need_y:
        xf_ref, xb_ref, dtf_ref, dtb_ref, h0_ref, bias_ref, alog_ref, yf_ref, yb_ref, h_ref = refs
    else:
        xf_ref, xb_ref, dtf_ref, dtb_ref, h0_ref, bias_ref, alog_ref, h_ref = refs
        yf_ref = yb_ref = None

    @pl.when(pl.program_id(1) == 0)
    def _():
        h_ref[...] = h0_ref[...]

    bias = bias_ref[...]
    a_neg = -jnp.exp(alog_ref[...])
    _ssd_dir(xf_ref, dtf_ref, yf_ref, h_ref, bias, a_neg, 0, nh=nh, ng=ng, need_y=need_y)
    _ssd_dir(xb_ref, dtb_ref, yb_ref, h_ref, bias, a_neg, 1, nh=nh, ng=ng, need_y=need_y)


def _ssd_call(xbc, dt, h0, bias, alog, nh, ng, need_y):
    b, l, cd = xbc.shape
    w = nh * SSD_HEAD_DIM
    nc = l // SSD_CHUNK
    q = SSD_CHUNK
    fwd = lambda i, s: (i, s, 0)
    bwd = lambda i, s: (i, nc - 1 - s, 0)
    hspec = pl.BlockSpec((None, 2, SSD_STATE, w), lambda i, s: (i, 0, 0, 0))
    in_specs = [pl.BlockSpec((None, q, cd), fwd), pl.BlockSpec((None, q, cd), bwd),
                pl.BlockSpec((None, q, LANE), fwd), pl.BlockSpec((None, q, LANE), bwd),
                hspec,
                pl.BlockSpec((1, LANE), lambda i, s: (0, 0)), pl.BlockSpec((1, LANE), lambda i, s: (0, 0))]
    out_specs = [hspec]
    out_shape = [jax.ShapeDtypeStruct((b, 2, SSD_STATE, w), F32)]
    if need_y:
        out_specs = [pl.BlockSpec((None, q, w), fwd), pl.BlockSpec((None, q, w), bwd)] + out_specs
        out_shape = [jax.ShapeDtypeStruct((b, l, w), BF16)] * 2 + out_shape
    return pl.pallas_call(
        functools.partial(_ssd_kernel, nh=nh, ng=ng, need_y=need_y), grid=(b, nc),
        in_specs=in_specs, out_specs=out_specs, out_shape=out_shape,
        compiler_params=_cp(("parallel", "arbitrary"), 48),
        name="ssd_scan" if need_y else "ssd_scan_ctx")(xbc, xbc, dt, dt, h0, bias, alog)


def _gnorm_kernel(yf_ref, yb_ref, xs_ref, z_ref, d_ref, g_ref, o_ref, t_ref, *, rows, tb, gw):
    y = d_ref[...] * xs_ref[...].astype(F32) + yf_ref[...].astype(F32) + yb_ref[...].astype(F32)
    y = y * _silu(z_ref[...].astype(F32))
    bw = y.shape[1]
    parts = []
    for g in range(bw // gw):
        yg = y[:, g * gw:(g + 1) * gw]
        parts.append(yg * lax.rsqrt(jnp.mean(yg * yg, axis=-1, keepdims=True) + EPS))
    res = jnp.concatenate(parts, axis=1) * g_ref[...]
    pitch = rows + XPOSE_PAD
    for wl in range(tb):
        for kk in range(bw // LANE):
            t_ref[kk, wl * pitch:wl * pitch + rows, :] = res[wl * rows:(wl + 1) * rows, kk * LANE:(kk + 1) * LANE]
    for r in range(rows):
        o_ref[r] = jnp.concatenate(
            [t_ref[kk, pl.ds(r, tb, stride=pitch), :] for kk in range(bw // LANE)], axis=1).astype(o_ref.dtype)


def _gnorm_call(yf, yb, xbc, proj, dvec, gvec, ng):
    b, l, w = yf.shape
    rows = l // GRID_W
    tb = XPOSE_TILE
    gw = w // ng
    bw = gw * (2 if ng % 2 == 0 else 1)
    blk = pl.BlockSpec((None, tb * rows, bw), lambda i, s, g: (i, s, g))
    vec = pl.BlockSpec((1, bw), lambda i, s, g: (0, g))
    out = pl.pallas_call(
        functools.partial(_gnorm_kernel, rows=rows, tb=tb, gw=gw), grid=(b, GRID_W // tb, w // bw),
        in_specs=[blk, blk, blk, blk, vec, vec],
        out_specs=pl.BlockSpec((None, rows, tb, bw), lambda i, s, g: (i, 0, s, g)),
        out_shape=jax.ShapeDtypeStruct((b, rows, GRID_W, w), BF16),
        scratch_shapes=[pltpu.VMEM((bw // LANE, tb * (rows + XPOSE_PAD), LANE), F32)],
        compiler_params=_cp(("parallel", "parallel", "parallel"), 40),
        name="ssd_gnorm")(yf, yb, xbc, proj, dvec, gvec)
    return out.reshape(b * l, w)


def _moe_kernel(e_ref, b0_ref, nb_ref, *refs, nsub, nh):
    xs = refs[:nsub]
    wg_ref, wu_ref, wd_ref, o_ref, act_ref, wdb_ref = refs[nsub:]
    s = pl.program_id(0)
    t = pl.program_id(1)
    nb = nb_ref[s]

    @pl.when(jnp.logical_and(t < nh, nb > 0))
    def _():
        wg = wg_ref[...].astype(BF16)
        wu = wu_ref[...].astype(BF16)
        for k in range(nsub):
            @pl.when(k < nb)
            def _():
                x = xs[k][...]
                gt = jnp.dot(x, wg, preferred_element_type=F32)
                up = jnp.dot(x, wu, preferred_element_type=F32)
                act_ref[k, t] = (_silu(gt) * up).astype(BF16)

    @pl.when(jnp.logical_and(t >= nh, nb > 0))
    def _():
        wdb_ref[...] = wd_ref[...].astype(BF16)
        for k in range(nsub):
            rows = slice(k * MOE_BLOCK, (k + 1) * MOE_BLOCK)

            @pl.when(k < nb)
            def _():
                a = jnp.concatenate([act_ref[k, h] for h in range(nh)], axis=1)
                o_ref[rows, :] = jnp.dot(a, wdb_ref[...], preferred_element_type=F32).astype(o_ref.dtype)

            @pl.when(k >= nb)
            def _():
                o_ref[rows, :] = jnp.zeros((MOE_BLOCK, o_ref.shape[1]), o_ref.dtype)

    @pl.when(jnp.logical_and(t >= nh, nb == 0))
    def _():
        o_ref[...] = jnp.zeros(o_ref.shape, o_ref.dtype)


def _moe_call(sup_e, sup_b0, sup_nb, xs, w_gate, w_up, w_down):
    n_slots, d = xs.shape
    hid = w_gate.shape[-1]
    th = min(256, hid)
    tn = min(1024, d)
    nh = hid // th
    nt = d // tn
    ns = sup_e.shape[0]
    rsup = MOE_SUB * MOE_BLOCK

    def x_map(k):
        def index(s, t, e, b0, nb):
            sx = jnp.where(t < nh, s, jnp.minimum(s + 1, ns - 1))
            return (b0[sx * MOE_SUB + k], 0)
        return index

    def hid_idx(s, t, nb):
        return jnp.where(nb[s] > 0, jnp.minimum(t, nh - 1), nh - 1)

    def col_idx(s, t, nb):
        return jnp.where(nb[s] > 0, jnp.clip(t - nh, 0, nt - 1), nt - 1)

    gs = pltpu.PrefetchScalarGridSpec(
        num_scalar_prefetch=3, grid=(ns, nh + nt),
        in_specs=[pl.BlockSpec((MOE_BLOCK, d), x_map(k)) for k in range(MOE_SUB)] + [
            pl.BlockSpec((None, d, th), lambda s, t, e, b0, nb: (e[s], 0, hid_idx(s, t, nb))),
            pl.BlockSpec((None, d, th), lambda s, t, e, b0, nb: (e[s], 0, hid_idx(s, t, nb))),
            pl.BlockSpec((None, hid, tn), lambda s, t, e, b0, nb: (e[s], 0, col_idx(s, t, nb)))],
        out_specs=pl.BlockSpec((rsup, tn), lambda s, t, e, b0, nb: (s, jnp.clip(t - nh, 0, nt - 1))),
        scratch_shapes=[pltpu.VMEM((MOE_SUB, nh, MOE_BLOCK, th), BF16), pltpu.VMEM((hid, tn), BF16)])
    return pl.pallas_call(
        functools.partial(_moe_kernel, nsub=MOE_SUB, nh=nh), grid_spec=gs,
        out_shape=jax.ShapeDtypeStruct((ns * rsup, d), BF16),
        compiler_params=_cp(("arbitrary", "arbitrary"), 60), name="moe_experts")(
            sup_e, sup_b0, sup_nb, *([xs] * MOE_SUB), w_gate, w_up, w_down)


def _final_kernel(x_ref, ya_ref, yb_ref, w_ref, g2_ref, fg_ref, o_ref):
    wts = w_ref[...]
    moe = ya_ref[...].astype(F32) * wts[:, 0:1] + yb_ref[...].astype(F32) * wts[:, 1:2]
    x = x_ref[...] + g2_ref[...] * moe
    ms = jnp.mean(x * x, axis=-1, keepdims=True)
    o_ref[...] = x * lax.rsqrt(ms + EPS) * fg_ref[...]


def _final_call(x, ya, yb, wts, mods, ig, fg):
    b, l, d = x.shape
    tr = min(256, l)
    nt = l // tr
    row = lambda i, r: (i * nt + r, 0)
    return pl.pallas_call(
        _final_kernel, grid=(b, nt),
        in_specs=[pl.BlockSpec((None, tr, d), lambda i, r: (i, r, 0)),
                  pl.BlockSpec((tr, d), row), pl.BlockSpec((tr, d), row),
                  pl.BlockSpec((tr, MOE_TOP_K), row),
                  pl.BlockSpec((None, None, 1, d), lambda i, r: (i, ig, 0, 0)),
                  pl.BlockSpec((1, d), lambda i, r: (0, 0))],
        out_specs=pl.BlockSpec((None, tr, d), lambda i, r: (i, r, 0)),
        out_shape=jax.ShapeDtypeStruct((b, l, d), F32),
        compiler_params=_cp(("parallel", "parallel"), 40), name="combine_final")(x, ya, yb, wts, mods, fg)


def _route(sel, ne):
    m = sel.shape[0]
    weights = sel[:, MOE_TOP_K:2 * MOE_TOP_K]
    n_assign = m * MOE_TOP_K
    flat_e = jnp.concatenate([sel[:, k] for k in range(MOE_TOP_K)], axis=0).astype(jnp.int32)
    onehot = (flat_e[:, None] == jnp.arange(ne, dtype=flat_e.dtype)[None, :]).astype(jnp.int32)
    csum = jnp.cumsum(onehot, axis=0)
    counts = csum[-1]
    local = jnp.sum(csum * onehot, axis=1) - 1
    nblk_e = (counts + MOE_BLOCK - 1) // MOE_BLOCK
    padded = nblk_e * MOE_BLOCK
    pad_end = jnp.cumsum(padded)
    pad_start = pad_end - padded
    dest = pad_start[flat_e] + local
    n_blocks = -(-(n_assign + ne * (MOE_BLOCK - 1)) // MOE_BLOCK)
    tok = jnp.arange(n_assign, dtype=jnp.int32) % m
    n_slots = n_blocks * MOE_BLOCK
    slot_tok = (jnp.arange(n_slots, dtype=jnp.int32) % m).at[dest].set(tok)
    rsup = MOE_SUB * MOE_BLOCK
    ns_e = (nblk_e + MOE_SUB - 1) // MOE_SUB
    sup_end = jnp.cumsum(ns_e)
    sup_start = sup_end - ns_e
    n_sup = (n_blocks + (MOE_SUB - 1) * ne) // MOE_SUB
    sidx = jnp.arange(n_sup, dtype=jnp.int32)
    last = sup_end[-1] - 1
    s_eff = jnp.minimum(sidx, last)
    e_s = jnp.minimum(jnp.searchsorted(sup_end, s_eff, side='right'), ne - 1).astype(jnp.int32)
    k_s = s_eff - sup_start[e_s]
    b0_s = pad_start[e_s] // MOE_BLOCK + MOE_SUB * k_s
    nb_s = jnp.clip(nblk_e[e_s] - MOE_SUB * k_s, 0, MOE_SUB)
    used = sidx <= last
    sup_nb = jnp.where(used, nb_s, 0).astype(jnp.int32)
    kk = jnp.arange(MOE_SUB, dtype=jnp.int32)[None, :]
    sup_b0 = jnp.maximum(lax.cummax(jnp.where(kk < sup_nb[:, None], b0_s[:, None] + kk, -1), axis=0), 0)
    sup_b0 = sup_b0.reshape(-1).astype(jnp.int32)
    pos = ((sup_start[flat_e] + local // rsup) * rsup + local % rsup).astype(jnp.int32)
    return weights, slot_tok, e_s, sup_b0, sup_nb, pos


def kernel(x, c, ctx, c_ctx, w_mod, b_mod, norm1_g, w_in, s5_lam_re, s5_lam_im, s5_log_dt, s5_b_re, s5_b_im, s5_c_re, s5_c_im, s5_d, s5_w_val, s5_w_gate, ssd_conv_w, ssd_conv_b, ssd_a_log, ssd_dt_bias, ssd_d, ssd_norm_g, ssd_w_out, w_o, norm2_g, moe_w_group, moe_b_group, moe_w_expert, moe_b_expert, moe_w_gate, moe_w_up, moe_w_down, final_g):
    depth = w_mod.shape[0]
    assert depth == 1, "single-layer block"
    bsz, n_lat, d = x.shape
    l_ctx = ctx.shape[1]
    w5 = s5_d.shape[1]
    nh = ssd_d.shape[1]
    w = nh * SSD_HEAD_DIM
    conv_dim = ssd_conv_w.shape[2]
    ng = (conv_dim - w) // (2 * SSD_STATE)
    ssd_in = w + conv_dim + 2 * nh
    o1, o2 = w5, w5 + ssd_in
    l = 0

    cc = jnp.concatenate([c, c_ctx[None, :]], axis=0)
    cc = jnp.pad(cc, ((0, (-cc.shape[0]) % 8), (0, 0)))
    mods = _mod_call(cc, w_mod[l], b_mod[l]).reshape(cc.shape[0], 6, 1, d)
    i_sh1, i_sc1, i_g1, i_sh2, i_sc2, i_g2 = range(6)

    w_in_l = w_in[l]
    o_dt = o1 + w + conv_dim

    hn_rm, hn_cm = _norm_lat_call(x, norm1_g[l], mods, i_sc1, i_sh1)
    hc = _norm_ctx_call(ctx, norm1_g[l], mods, bsz, i_sc1, i_sh1)
    hn_rm = hn_rm.reshape(bsz * n_lat, d)
    hn_cm = hn_cm.reshape(bsz * n_lat, d)
    hc = hc.reshape(bsz * l_ctx, d)

    u_lat = _mm_call(hn_rm, w_in_l, BF16, "in_s5", 0, o1).reshape(bsz, n_lat, w5)
    u_ctx = _mm_call(hc, w_in_l, BF16, "in_s5_ctx", 0, o1).reshape(bsz, l_ctx, w5)
    gates = _mm_call(hn_rm, w_in_l, BF16, "in_gates", o2, 2 * d)
    z_lat = _mm_call(hn_cm, w_in_l, BF16, "in_z", o1, w).reshape(bsz, n_lat, w)
    xbc_lat = _proj_conv_call(hn_cm.reshape(bsz, n_lat, d), w_in_l, o1 + w, ssd_conv_w[l], ssd_conv_b[l])
    p_ctx = _mm_call(hc, w_in_l, BF16, "in_xbc_ctx", o1 + w, conv_dim).reshape(bsz, l_ctx, conv_dim)
    dt_lat = _mm_call(hn_cm, w_in_l, F32, "in_dt", o_dt, 2 * nh).reshape(bsz, n_lat, LANE)
    dt_ctx = _mm_call(hc, w_in_l, F32, "in_dt_ctx", o_dt, 2 * nh).reshape(bsz, l_ctx, LANE)

    nj = w5 // LANE
    bmat, cmat, dmat, lre, lim = _s5_params(s5_lam_re[l], s5_lam_im[l], s5_log_dt[l], s5_b_re[l], s5_b_im[l],
                                            s5_c_re[l], s5_c_im[l], bsz)
    s5_zero = jnp.zeros((nj, 2, 2 * bsz, lre.shape[-1]), F32)
    (s5_ctx,) = _s5_call(u_ctx, bmat, cmat, dmat, lre, lim, s5_zero, False)
    ya_f, ya_b, _ = _s5_call(u_lat, bmat, cmat, dmat, lre, lim, s5_ctx, True)

    xbc_ctx = _conv_call(p_ctx, ssd_conv_w[l], ssd_conv_b[l], 0)
    pad_h = LANE - 2 * nh
    bias = jnp.pad(ssd_dt_bias[l].astype(F32).reshape(1, 2 * nh), ((0, 0), (0, pad_h)))
    alog = jnp.pad(ssd_a_log[l].astype(F32).reshape(1, 2 * nh), ((0, 0), (0, pad_h)))
    h_zero = jnp.zeros((bsz, 2, SSD_STATE, w), F32)
    (h_ctx,) = _ssd_call(xbc_ctx, dt_ctx, h_zero, bias, alog, nh, ng, False)
    y_f, y_b, _ = _ssd_call(xbc_lat, dt_lat, h_ctx, bias, alog, nh, ng, True)
    d_vec = jnp.repeat(ssd_d[l].astype(F32), SSD_HEAD_DIM).reshape(1, w)
    y_ssd = _gnorm_call(y_f, y_b, xbc_lat, z_lat, d_vec, ssd_norm_g[l].astype(F32).reshape(1, w), ng)

    m_lat = bsz * n_lat
    part_a = _glu_call(u_lat.reshape(m_lat, w5), ya_f.reshape(m_lat, w5), ya_b.reshape(m_lat, w5),
                       s5_d[l].astype(F32).reshape(1, w5),
                       s5_w_val[l].astype(BF16), s5_w_gate[l].astype(BF16), gates)
    merged = _merge_call(y_ssd, ssd_w_out[l], gates, part_a)
    x1 = _resid_call(merged, w_o[l], x, mods, i_g1)

    ngr = moe_w_group.shape[-1]
    ne = moe_w_expert.shape[-1]
    wr = jnp.concatenate([moe_w_group[l], moe_w_expert[l]], axis=1).astype(F32)
    wr = jnp.pad(wr, ((0, 0), (0, (-(ngr + ne)) % LANE)))
    br = jnp.concatenate([moe_b_group[l], moe_b_expert[l]]).astype(F32)
    br = jnp.pad(br, (0, (-(ngr + ne)) % LANE)).reshape(1, -1)
    hx, sel = _norm_router_call(x1, norm2_g[l], mods, i_sc2, i_sh2, wr, br, ngr, ne)
    m = bsz * n_lat
    hx = hx.reshape(m, d)
    weights, slot_tok, sup_e, sup_b0, sup_nb, pos = _route(sel.reshape(m, -1), ne)
    xs = hx[slot_tok]
    y_slots = _moe_call(sup_e, sup_b0, sup_nb, xs, moe_w_gate[l], moe_w_up[l], moe_w_down[l])
    ya = y_slots[pos[:m]]
    yb = y_slots[pos[m:]]
    return _final_call(x1, ya, yb, weights.astype(F32), mods, i_g2, final_g.reshape(1, d))
```

```python
import functools
import math

import jax
import jax.numpy as jnp
from jax import lax
from jax.experimental import pallas as pl
from jax.experimental.pallas import tpu as pltpu

F32 = jnp.float32
BF16 = jnp.bfloat16

GRID_W = 64
EPS = 1e-6
LANE = 128
SSD_HEAD_DIM = 64
SSD_STATE = 128
SSD_CHUNK = 128
SSD_CONV = 5
MOE_TOP_K = 2
MOE_BLOCK = 256
MOE_SUB = 4
XPOSE_TILE = 16
XPOSE_PAD = 8
S5_CHUNK = 256
S5_ROW_PAD = 8


def _cp(sem, mb):
    return pltpu.CompilerParams(dimension_semantics=sem, vmem_limit_bytes=mb * 1024 * 1024)


def _sigmoid(x):
    return 1.0 / (1.0 + jnp.exp(-x))


def _silu(x):
    return x * _sigmoid(x)


def _mod_kernel(c_ref, w_ref, b_ref, o_ref):
    s = _silu(c_ref[...])
    o_ref[...] = jnp.dot(s.astype(BF16), w_ref[...].astype(BF16),
                         preferred_element_type=F32) + b_ref[...]


def _mod_call(cc, w, b):
    r, d = cc.shape
    n = w.shape[1]
    tn = min(512, n)
    return pl.pallas_call(
        _mod_kernel, grid=(n // tn,),
        in_specs=[pl.BlockSpec((r, d), lambda j: (0, 0)),
                  pl.BlockSpec((d, tn), lambda j: (0, j)),
                  pl.BlockSpec((1, tn), lambda j: (0, j))],
        out_specs=pl.BlockSpec((r, tn), lambda j: (0, j)),
        out_shape=jax.ShapeDtypeStruct((r, n), F32),
        compiler_params=_cp(("parallel",), 40), name="mod")(cc, w, b.reshape(1, n))


def _rmsmod(x, g, sc, sh):
    ms = jnp.mean(x * x, axis=-1, keepdims=True)
    return (x * lax.rsqrt(ms + EPS) * g) * (1.0 + sc) + sh


def _norm_lat_kernel(x_ref, g_ref, sc_ref, sh_ref, orm_ref, ocm_ref, t_ref, *, tb, d):
    pitch = tb + XPOSE_PAD
    for r in range(tb):
        y = _rmsmod(x_ref[r], g_ref[...], sc_ref[...], sh_ref[...])
        orm_ref[r] = y.astype(BF16)
        for kk in range(d // LANE):
            t_ref[kk, r * pitch:r * pitch + tb, :] = y[:, kk * LANE:(kk + 1) * LANE]
    for wl in range(tb):
        ocm_ref[wl] = jnp.concatenate(
            [t_ref[kk, pl.ds(wl, tb, stride=pitch), :] for kk in range(d // LANE)], axis=1).astype(BF16)


def _norm_lat_call(x, g, mods, isc, ish):
    b, l, d = x.shape
    rows = l // GRID_W
    tb = XPOSE_TILE
    xv = x.reshape(b, rows, GRID_W, d)
    orm, ocm = pl.pallas_call(
        functools.partial(_norm_lat_kernel, tb=tb, d=d), grid=(b, rows // tb, GRID_W // tb),
        in_specs=[pl.BlockSpec((None, tb, tb, d), lambda i, r, c: (i, r, c, 0)),
                  pl.BlockSpec((1, d), lambda i, r, c: (0, 0)),
                  pl.BlockSpec((None, None, 1, d), lambda i, r, c: (i, isc, 0, 0)),
                  pl.BlockSpec((None, None, 1, d), lambda i, r, c: (i, ish, 0, 0))],
        out_specs=[pl.BlockSpec((None, tb, tb, d), lambda i, r, c: (i, r, c, 0)),
                   pl.BlockSpec((None, tb, tb, d), lambda i, r, c: (i, c, r, 0))],
        out_shape=[jax.ShapeDtypeStruct((b, rows, GRID_W, d), BF16),
                   jax.ShapeDtypeStruct((b, GRID_W, rows, d), BF16)],
        scratch_shapes=[pltpu.VMEM((d // LANE, tb * (tb + XPOSE_PAD), LANE), F32)],
        compiler_params=_cp(("parallel", "parallel", "parallel"), 48),
        name="norm1_lat")(xv, g.reshape(1, d), mods, mods)
    return orm.reshape(b, l, d), ocm.reshape(b, l, d)


def _norm_kernel(x_ref, g_ref, sc_ref, sh_ref, o_ref):
    o_ref[...] = _rmsmod(x_ref[...], g_ref[...], sc_ref[...], sh_ref[...]).astype(o_ref.dtype)


def _norm_ctx_call(x, g, mods, row, isc, ish):
    b, l, d = x.shape
    tr = min(256, l)
    return pl.pallas_call(
        _norm_kernel, grid=(b, l // tr),
        in_specs=[pl.BlockSpec((None, tr, d), lambda i, r: (i, r, 0)),
                  pl.BlockSpec((1, d), lambda i, r: (0, 0)),
                  pl.BlockSpec((None, None, 1, d), lambda i, r: (row, isc, 0, 0)),
                  pl.BlockSpec((None, None, 1, d), lambda i, r: (row, ish, 0, 0))],
        out_specs=pl.BlockSpec((None, tr, d), lambda i, r: (i, r, 0)),
        out_shape=jax.ShapeDtypeStruct((b, l, d), BF16),
        compiler_params=_cp(("parallel", "parallel"), 40), name="norm1_ctx")(x, g.reshape(1, d), mods, mods)


def _first_argmax(v, vmax, lane):
    return jnp.min(jnp.where(v == vmax, lane, float(LANE)), axis=-1, keepdims=True)


def _norm_router_kernel(x_ref, g_ref, sc_ref, sh_ref, wr_ref, br_ref, o_ref, sel_ref, *, ngr, ne):
    y = _rmsmod(x_ref[...], g_ref[...], sc_ref[...], sh_ref[...])
    o_ref[...] = y.astype(BF16)
    y_hi = y.astype(BF16)
    y_lo = (y - y_hi.astype(F32)).astype(BF16)
    w_hi = wr_ref[0]
    lg = (jnp.dot(y_hi, w_hi, preferred_element_type=F32) + jnp.dot(y_hi, wr_ref[1], preferred_element_type=F32)
          + jnp.dot(y_lo, w_hi, preferred_element_type=F32)) + br_ref[...]
    epg = ne // ngr
    lane = lax.broadcasted_iota(jnp.int32, lg.shape, 1).astype(F32)
    ninf = -jnp.inf
    gl = jnp.where(lane < ngr, lg, ninf)
    ge = jnp.exp(gl - jnp.max(gl, axis=-1, keepdims=True))
    g_prob = ge / jnp.sum(ge, axis=-1, keepdims=True)
    g_p = jnp.max(g_prob, axis=-1, keepdims=True)
    lo = ngr + epg * _first_argmax(g_prob, g_p, lane)
    cand = jnp.where(jnp.logical_and(lane >= lo, lane < lo + epg), lg, ninf)
    v1 = jnp.max(cand, axis=-1, keepdims=True)
    l1 = _first_argmax(cand, v1, lane)
    rest = jnp.where(lane == l1, ninf, cand)
    v2 = jnp.max(rest, axis=-1, keepdims=True)
    l2 = _first_argmax(rest, v2, lane)
    ex = jnp.exp(v2 - v1)
    w1 = g_p * (1.0 / (1.0 + ex))
    w2 = g_p * (ex / (1.0 + ex))
    sel_ref[...] = jnp.where(lane == 0, l1 - ngr, jnp.where(lane == 1, l2 - ngr,
                             jnp.where(lane == 2, w1, jnp.where(lane == 3, w2, 0.0))))


def _norm_router_call(x, g, mods, isc, ish, wr, br, ngr, ne):
    b, l, d = x.shape
    tr = min(256, l)
    nr = wr.shape[1]
    w_hi = wr.astype(BF16)
    wr = jnp.stack([w_hi, (wr - w_hi.astype(F32)).astype(BF16)], axis=0)
    return pl.pallas_call(
        functools.partial(_norm_router_kernel, ngr=ngr, ne=ne), grid=(b, l // tr),
        in_specs=[pl.BlockSpec((None, tr, d), lambda i, r: (i, r, 0)),
                  pl.BlockSpec((1, d), lambda i, r: (0, 0)),
                  pl.BlockSpec((None, None, 1, d), lambda i, r: (i, isc, 0, 0)),
                  pl.BlockSpec((None, None, 1, d), lambda i, r: (i, ish, 0, 0)),
                  pl.BlockSpec((2, d, nr), lambda i, r: (0, 0, 0)),
                  pl.BlockSpec((1, nr), lambda i, r: (0, 0))],
        out_specs=[pl.BlockSpec((None, tr, d), lambda i, r: (i, r, 0)),
                   pl.BlockSpec((None, tr, nr), lambda i, r: (i, r, 0))],
        out_shape=[jax.ShapeDtypeStruct((b, l, d), BF16), jax.ShapeDtypeStruct((b, l, nr), F32)],
        compiler_params=_cp(("parallel", "parallel"), 40),
        name="norm2_router")(x, g.reshape(1, d), mods, mods, wr, br)


def _mm_kernel(a_ref, b_ref, o_ref):
    o_ref[...] = jnp.dot(a_ref[...], b_ref[...].astype(BF16), preferred_element_type=F32).astype(o_ref.dtype)


def _mm_tiles(m, n):
    tm = min(1024, m)
    tn = min(512, n)
    return tm, tn


def _mm_call(a, b, out_dtype, name, col0=0, n=None):
    m, k = a.shape
    n = b.shape[1] - col0 if n is None else n
    tm, tn = _mm_tiles(m, n)
    if col0 % LANE or n % tn or n % LANE:
        b = b[:, col0:col0 + n]
        pad = (-n) % LANE
        b = jnp.pad(b, ((0, 0), (0, pad)))
        n, col0 = n + pad, 0
        tm, tn = _mm_tiles(m, n)
    return pl.pallas_call(
        _mm_kernel, grid=(m // tm, n // tn),
        in_specs=[pl.BlockSpec((tm, k), lambda i, j: (i, 0)),
                  pl.BlockSpec((pl.Element(k), pl.Element(tn)), lambda i, j: (0, (col0 // LANE + j * (tn // LANE)) * LANE))],
        out_specs=pl.BlockSpec((tm, tn), lambda i, j: (i, j)),
        out_shape=jax.ShapeDtypeStruct((m, n), out_dtype),
        compiler_params=_cp(("parallel", "parallel"), 48), name=name)(a, b)


def _gelu_tanh(x):
    return x * (0.5 * (1.0 + jnp.tanh(math.sqrt(2.0 / math.pi) * (x + 0.044715 * (x * x * x)))))


def _glu_kernel(u_ref, yf_ref, yb_ref, d_ref, wv_ref, wg_ref, gate_ref, o_ref, a_ref):
    @pl.when(pl.program_id(1) == 0)
    def _():
        y = d_ref[...] * u_ref[...].astype(F32) + yf_ref[...].astype(F32) + yb_ref[...].astype(F32)
        a_ref[...] = _gelu_tanh(y).astype(BF16)

    a = a_ref[...]
    val = jnp.dot(a, wv_ref[...], preferred_element_type=F32)
    gl = jnp.dot(a, wg_ref[...], preferred_element_type=F32)
    o_ref[...] = (_sigmoid(gate_ref[...].astype(F32)) * (val * _sigmoid(gl))).astype(o_ref.dtype)


def _glu_call(u, yf, yb, dvec, wv, wg, gates):
    m, k = u.shape
    n = wv.shape[1]
    tm, tn = _mm_tiles(m, n)
    row = pl.BlockSpec((tm, k), lambda i, j: (i, 0))
    return pl.pallas_call(
        _glu_kernel, grid=(m // tm, n // tn),
        in_specs=[row, row, row,
                  pl.BlockSpec((1, k), lambda i, j: (0, 0)),
                  pl.BlockSpec((k, tn), lambda i, j: (0, j)),
                  pl.BlockSpec((k, tn), lambda i, j: (0, j)),
                  pl.BlockSpec((tm, tn), lambda i, j: (i, j))],
        out_specs=pl.BlockSpec((tm, tn), lambda i, j: (i, j)),
        out_shape=jax.ShapeDtypeStruct((m, n), BF16),
        scratch_shapes=[pltpu.VMEM((tm, k), BF16)],
        compiler_params=_cp(("parallel", "arbitrary"), 56), name="s5_glu")(u, yf, yb, dvec, wv, wg, gates)


def _merge_kernel(a_ref, w_ref, gate_ref, pa_ref, o_ref):
    br = jnp.dot(a_ref[...], w_ref[...].astype(BF16), preferred_element_type=F32)
    o_ref[...] = (pa_ref[...].astype(F32) + _sigmoid(gate_ref[...].astype(F32)) * br).astype(o_ref.dtype)


def _merge_call(a, w, gates, part_a):
    m, k = a.shape
    n = w.shape[1]
    tm, tn = _mm_tiles(m, n)
    off = n // tn
    return pl.pallas_call(
        _merge_kernel, grid=(m // tm, n // tn),
        in_specs=[pl.BlockSpec((tm, k), lambda i, j: (i, 0)),
                  pl.BlockSpec((k, tn), lambda i, j: (0, j)),
                  pl.BlockSpec((tm, tn), lambda i, j: (i, j + off)),
                  pl.BlockSpec((tm, tn), lambda i, j: (i, j))],
        out_specs=pl.BlockSpec((tm, tn), lambda i, j: (i, j)),
        out_shape=jax.ShapeDtypeStruct((m, n), BF16),
        compiler_params=_cp(("parallel", "parallel"), 48), name="ssd_out_merge")(a, w, gates, part_a)


def _resid_kernel(a_ref, w_ref, x_ref, g_ref, o_ref):
    mix = jnp.dot(a_ref[...], w_ref[...].astype(BF16), preferred_element_type=F32)
    o_ref[...] = x_ref[...] + g_ref[...] * mix


def _resid_call(a, w, x, mods, ig):
    b, l, d = x.shape
    k = a.shape[1]
    tm, tn = _mm_tiles(l, d)
    nt = l // tm
    return pl.pallas_call(
        _resid_kernel, grid=(b * nt, d // tn),
        in_specs=[pl.BlockSpec((tm, k), lambda i, j: (i, 0)),
                  pl.BlockSpec((k, tn), lambda i, j: (0, j)),
                  pl.BlockSpec((None, tm, tn), lambda i, j: (i // nt, i % nt, j)),
                  pl.BlockSpec((None, None, 1, tn), lambda i, j: (i // nt, ig, 0, j))],
        out_specs=pl.BlockSpec((None, tm, tn), lambda i, j: (i // nt, i % nt, j)),
        out_shape=jax.ShapeDtypeStruct((b, l, d), F32),
        compiler_params=_cp(("parallel", "parallel"), 48), name="w_o_resid")(a, w, x, mods)


def _s5_kernel(*refs, nb, tp, p8, need_y):
    if need_y:
        uf_ref, ub_ref, b_ref, c_ref, d_ref, lre_ref, lim_ref, h0_ref, yf_ref, yb_ref, h_ref, buf_ref, il_ref = refs
    else:
        uf_ref, ub_ref, b_ref, c_ref, d_ref, lre_ref, lim_ref, h0_ref, h_ref, buf_ref, il_ref = refs
    q = 2 * nb
    nk = p8 // LANE
    pitch = tp + S5_ROW_PAD

    @pl.when(pl.program_id(1) == 0)
    def _():
        h_ref[...] = h0_ref[...]

    rev = (lax.broadcasted_iota(jnp.int32, (tp, tp), 0) + lax.broadcasted_iota(jnp.int32, (tp, tp), 1)
           == tp - 1).astype(BF16)
    def pair_rows(u_ref, b, newer_first):
        il_ref[...] = u_ref[b].astype(F32)
        even = il_ref[pl.ds(0, tp, stride=2), :].astype(BF16)
        odd = il_ref[pl.ds(1, tp, stride=2), :].astype(BF16)
        return jnp.concatenate([odd, even] if newer_first else [even, odd], axis=1)

    lhs = []
    for d in range(2):
        if d == 0:
            u = jnp.concatenate([pair_rows(uf_ref, b, False) for b in range(nb)], axis=0)
        else:
            u = jnp.concatenate(
                [jnp.dot(rev, pair_rows(ub_ref, b, True), preferred_element_type=F32).astype(BF16)
                 for b in range(nb)], axis=0)
        lhs.append(u)
        bu = jnp.dot(u, b_ref[d], preferred_element_type=F32)
        for b in range(nb):
            r0 = (d * nb + b) * pitch
            for k in range(2 * nk):
                buf_ref[k, r0:r0 + tp, :] = bu[b * tp:(b + 1) * tp, k * LANE:(k + 1) * LANE]
    ar = [lre_ref[:, k * LANE:(k + 1) * LANE] for k in range(nk)]
    ai = [lim_ref[:, k * LANE:(k + 1) * LANE] for k in range(nk)]

    def step(s, carry):
        rows = pl.ds(s, q, stride=pitch)
        out = []
        for k in range(nk):
            hr, hi = carry[k]
            nr = ar[k] * hr - ai[k] * hi + buf_ref[k, rows, :]
            ni = ar[k] * hi + ai[k] * hr + buf_ref[nk + k, rows, :]
            buf_ref[k, rows, :] = hr
            buf_ref[nk + k, rows, :] = hi
            out.append((nr, ni))
        return tuple(out)

    init = tuple((h_ref[0, :, k * LANE:(k + 1) * LANE], h_ref[1, :, k * LANE:(k + 1) * LANE]) for k in range(nk))
    fin = lax.fori_loop(0, tp, step, init, unroll=8)
    for k in range(nk):
        h_ref[0, :, k * LANE:(k + 1) * LANE] = fin[k][0]
        h_ref[1, :, k * LANE:(k + 1) * LANE] = fin[k][1]

    if need_y:
        for d in range(2):
            h = jnp.concatenate(
                [jnp.concatenate([buf_ref[k, (d * nb + b) * pitch:(d * nb + b) * pitch + tp, :].astype(BF16)
                                  for k in range(2 * nk)], axis=1) for b in range(nb)], axis=0)
            y = (jnp.dot(h, c_ref[d], preferred_element_type=F32)
                 + jnp.dot(lhs[d], d_ref[d], preferred_element_type=F32))
            for b in range(nb):
                yb = y[b * tp:(b + 1) * tp]
                if d == 1:
                    yb = jnp.dot(rev, yb.astype(BF16), preferred_element_type=F32)
                first, second = (0, 1) if d == 0 else (1, 0)
                il_ref[pl.ds(first, tp, stride=2), :] = yb[:, 0:LANE]
                il_ref[pl.ds(second, tp, stride=2), :] = yb[:, LANE:2 * LANE]
                (yf_ref if d == 0 else yb_ref)[b] = il_ref[...].astype(BF16)


def _s5_call(u, bmat, cmat, dmat, lre, lim, h0, need_y):
    nb, l, w5 = u.shape
    nj = w5 // LANE
    q = 2 * nb
    t = min(S5_CHUNK, l)
    tp = t // 2
    p8 = lre.shape[-1]
    nc = l // t
    hspec = pl.BlockSpec((None, 2, q, p8), lambda j, c: (j, 0, 0, 0))
    ublk = (nb, t, LANE)
    in_specs = [pl.BlockSpec(ublk, lambda j, c: (0, c, j)), pl.BlockSpec(ublk, lambda j, c: (0, nc - 1 - c, j)),
                pl.BlockSpec((None, 2, 2 * LANE, 2 * p8), lambda j, c: (j, 0, 0, 0)),
                pl.BlockSpec((None, 2, 2 * p8, 2 * LANE), lambda j, c: (j, 0, 0, 0)),
                pl.BlockSpec((None, 2, 2 * LANE, 2 * LANE), lambda j, c: (j, 0, 0, 0)),
                pl.BlockSpec((None, q, p8), lambda j, c: (j, 0, 0)),
                pl.BlockSpec((None, q, p8), lambda j, c: (j, 0, 0)),
                hspec]
    out_specs = [hspec]
    out_shape = [jax.ShapeDtypeStruct((nj, 2, q, p8), F32)]
    scratch = [pltpu.VMEM((2 * p8 // LANE, q * (tp + S5_ROW_PAD), LANE), F32), pltpu.VMEM((t, LANE), F32)]
    if need_y:
        out_specs = [pl.BlockSpec(ublk, lambda j, c: (0, c, j)),
                     pl.BlockSpec(ublk, lambda j, c: (0, nc - 1 - c, j))] + out_specs
        out_shape = [jax.ShapeDtypeStruct((nb, l, w5), BF16)] * 2 + out_shape
    kern = functools.partial(_s5_kernel, nb=nb, tp=tp, p8=p8, need_y=need_y)
    return pl.pallas_call(
        kern, grid=(nj, nc), in_specs=in_specs, out_specs=out_specs, out_shape=out_shape,
        scratch_shapes=scratch, compiler_params=_cp(("parallel", "arbitrary"), 48),
        name="s5_scan" if need_y else "s5_scan_ctx")(u, u, bmat, cmat, dmat, lre, lim, h0)


def _s5_params(lam_re, lam_im, log_dt, b_re, b_im, c_re, c_im, nb):
    _, g, p = lam_re.shape
    s = b_re.shape[-1]
    gpb = LANE // s
    nj = g // gpb
    lam = lax.complex(lam_re.astype(F32), lam_im.astype(F32))
    lam_bar = jnp.exp(lam * jnp.exp(log_dt.astype(F32))[..., None])
    b_bar = ((lam_bar - 1.0) / lam)[..., None] * lax.complex(b_re.astype(F32), b_im.astype(F32))
    def quadrants(parts):
        r, c = parts[0].shape[-2:]
        same_group = (jnp.arange(gpb * r)[:, None] // r) == (jnp.arange(gpb * c)[None, :] // c)

        def spread(x):
            return jnp.where(same_group, jnp.tile(x.reshape(2, nj, gpb * r, c), (1, 1, 1, gpb)), 0.0)

        top = jnp.concatenate([spread(parts[0]), spread(parts[1])], axis=-1)
        bot = jnp.concatenate([spread(parts[2]), spread(parts[3])], axis=-1)
        return jnp.concatenate([top, bot], axis=-2).transpose(1, 0, 2, 3).astype(BF16)

    tb = lambda z: jnp.swapaxes(z, -1, -2)
    lb = lam_bar[..., None] * b_bar
    bmat = quadrants([tb(lb.real), tb(lb.imag), tb(b_bar.real), tb(b_bar.imag)])
    cc = lax.complex(c_re.astype(F32), c_im.astype(F32))
    c_l1 = cc * lam_bar[:, :, None, :]
    c_l2 = c_l1 * lam_bar[:, :, None, :]
    cmat = quadrants([tb(c_l1.real), tb(c_l2.real), -tb(c_l1.imag), -tb(c_l2.imag)])
    m0 = jnp.einsum('dgsp,dgpt->dgts', cc, b_bar).real
    m1 = jnp.einsum('dgsp,dgpt->dgts', c_l1, b_bar).real
    dmat = quadrants([m0, m1, jnp.zeros_like(m0), m0])

    def lam_of(part):
        v = part.reshape(2, nj, gpb * p).transpose(1, 0, 2)
        return jnp.repeat(v, nb, axis=1)

    lam2 = lam_bar * lam_bar
    return bmat, cmat, dmat, lam_of(lam2.real), lam_of(lam2.imag)


def _conv_silu(x, w, bias):
    l = x.shape[0]
    rows = lax.broadcasted_iota(jnp.int32, x.shape, 0)
    half = SSD_CONV // 2
    acc = x * w[half:half + 1, :] + bias
    for k in range(SSD_CONV):
        if k == half:
            continue
        off = k - half
        xs = pltpu.roll(x, shift=(-off) % l, axis=0)
        valid = jnp.logical_and(rows + off >= 0, rows + off < l)
        acc = acc + jnp.where(valid, xs, 0.0) * w[k:k + 1, :]
    return _silu(acc)


def _conv_kernel(x_ref, w_ref, b_ref, o_ref):
    o_ref[...] = _conv_silu(x_ref[...].astype(F32), w_ref[...], b_ref[...]).astype(o_ref.dtype)


def _proj_conv_kernel(a_ref, w_ref, cw_ref, cb_ref, o_ref):
    a = a_ref[...]
    wb = w_ref[...].astype(BF16)
    hw = wb.shape[1] // 2
    for c in range(2):
        cols = slice(c * hw, (c + 1) * hw)
        p = jnp.dot(a, wb[:, cols], preferred_element_type=F32)
        o_ref[:, cols] = _conv_silu(p, cw_ref[:, cols], cb_ref[:, cols]).astype(o_ref.dtype)


def _proj_conv_call(hn, w_all, col0, cw, cb):
    b, l, d = hn.shape
    c = cw.shape[1]
    tn = 512
    return pl.pallas_call(
        _proj_conv_kernel, grid=(b, c // tn),
        in_specs=[pl.BlockSpec((None, l, d), lambda i, j: (i, 0, 0), pipeline_mode=pl.Buffered(1)),
                  pl.BlockSpec((pl.Element(d), pl.Element(tn)),
                               lambda i, j: (0, (col0 // LANE + j * (tn // LANE)) * LANE)),
                  pl.BlockSpec((SSD_CONV, tn), lambda i, j: (0, j)),
                  pl.BlockSpec((1, tn), lambda i, j: (0, j))],
        out_specs=pl.BlockSpec((None, l, tn), lambda i, j: (i, 0, j)),
        out_shape=jax.ShapeDtypeStruct((b, l, c), BF16),
        compiler_params=_cp(("parallel", "arbitrary"), 60), name="in_xbc_conv")(hn, w_all, cw, cb.reshape(1, c))


def _conv_call(proj, w, bias, col0):
    b, l, _ = proj.shape
    c = w.shape[1]
    tc = 256
    off = col0 // tc
    return pl.pallas_call(
        _conv_kernel, grid=(b, c // tc),
        in_specs=[pl.BlockSpec((None, l, tc), lambda i, j: (i, 0, j + off)),
                  pl.BlockSpec((SSD_CONV, tc), lambda i, j: (0, j)),
                  pl.BlockSpec((1, tc), lambda i, j: (0, j))],
        out_specs=pl.BlockSpec((None, l, tc), lambda i, j: (i, 0, j)),
        out_shape=jax.ShapeDtypeStruct((b, l, c), BF16),
        compiler_params=_cp(("parallel", "parallel"), 40), name="ssd_conv")(proj, w, bias.reshape(1, c))


def _softplus(x):
    return jnp.maximum(x, 0.0) + jnp.log1p(jnp.exp(-jnp.abs(x)))


def _ssd_dir(xbc_ref, dt_ref, y_ref, h_ref, bias, a_neg, d, *, nh, ng, need_y):
    qn = SSD_CHUNK
    hd = SSD_HEAD_DIM
    w = nh * hd
    gw = w // ng
    gn = ng * SSD_STATE
    ii = lax.broadcasted_iota(jnp.int32, (qn, qn), 0)
    jj = lax.broadcasted_iota(jnp.int32, (qn, qn), 1)
    mask = (jj <= ii) if d == 0 else (jj >= ii)
    lmat = mask.astype(F32)
    dtv = _softplus(dt_ref[...] + bias)
    cum = jnp.dot(lmat, dtv * a_neg, precision=lax.Precision.HIGHEST, preferred_element_type=F32)
    cum_t = cum.T
    edge = qn - 1 if d == 0 else 0
    tot = cum[edge:edge + 1, :]
    dt_t = dtv.T
    wt_t = dt_t * jnp.exp(cum_t[:, edge:edge + 1] - cum_t)
    decay = jnp.exp(tot)
    lane = lax.broadcasted_iota(jnp.int32, (qn, LANE), 1)
    left = lane < hd
    zero = jnp.zeros((), BF16)
    for g in range(ng):
        bg = xbc_ref[:, w + g * SSD_STATE:w + (g + 1) * SSD_STATE]
        cg = xbc_ref[:, w + gn + g * SSD_STATE:w + gn + (g + 1) * SSD_STATE]
        bg_t = bg.astype(F32).T
        s_in = h_ref[d, :, g * gw:(g + 1) * gw]
        if need_y:
            cb = lax.dot_general(cg, bg, (((1,), (1,)), ((), ())), preferred_element_type=F32)
            yoff = jnp.dot(cg, s_in.astype(BF16), preferred_element_type=F32)
        for pr in range(gw // LANE):
            c0 = d * nh + (g * gw) // hd + 2 * pr
            col = g * gw + pr * LANE
            xp = xbc_ref[:, col:col + LANE]
            r = jnp.concatenate([jnp.where(left, xp, zero), jnp.where(left, zero, xp)], axis=0)
            tops, bots, cols = [], [], []
            for c in (c0, c0 + 1):
                bots.append((bg_t * wt_t[c:c + 1, :]).astype(BF16))
                if need_y:
                    ccol = jnp.broadcast_to(cum[:, c:c + 1], (qn, qn))
                    seg = jnp.where(mask, jnp.exp(ccol - cum_t[c:c + 1, :]), 0.0)
                    tops.append((cb * seg * dt_t[c:c + 1, :]).astype(BF16))
                    cols.append(ccol)
            dec = jnp.where(left[0:1], decay[:, c0:c0 + 1], decay[:, c0 + 1:c0 + 2])
            s_old = s_in[:, pr * LANE:(pr + 1) * LANE]
            if need_y:
                lhs = jnp.concatenate([jnp.concatenate(tops, axis=1), jnp.concatenate(bots, axis=1)], axis=0)
                out = jnp.dot(lhs, r, preferred_element_type=F32)
                ec = jnp.exp(jnp.where(left, cols[0], cols[1]))
                y_ref[:, col:col + LANE] = (out[:qn] + yoff[:, pr * LANE:(pr + 1) * LANE] * ec).astype(y_ref.dtype)
                s_new = out[qn:]
            else:
                s_new = jnp.dot(jnp.concatenate(bots, axis=1), r, preferred_element_type=F32)
            h_ref[d, :, col:col + LANE] = s_old * dec + s_new


def _ssd_kernel(*refs, nh, ng, need_y):
    if need_y:
        xf_ref, xb_ref, dtf_ref, dtb_ref, h0_ref, bias_ref, alog_ref, yf_ref, yb_ref, h_ref = refs
    else:
        xf_ref, xb_ref, dtf_ref, dtb_ref, h0_ref, bias_ref, alog_ref, h_ref = refs
        yf_ref = yb_ref = None

    @pl.when(pl.program_id(1) == 0)
    def _():
        h_ref[...] = h0_ref[...]

    bias = bias_ref[...]
    a_neg = -jnp.exp(alog_ref[...])
    _ssd_dir(xf_ref, dtf_ref, yf_ref, h_ref, bias, a_neg, 0, nh=nh, ng=ng, need_y=need_y)
    _ssd_dir(xb_ref, dtb_ref, yb_ref, h_ref, bias, a_neg, 1, nh=nh, ng=ng, need_y=need_y)


def _ssd_call(xbc, dt, h0, bias, alog, nh, ng, need_y):
    b, l, cd = xbc.shape
    w = nh * SSD_HEAD_DIM
    nc = l // SSD_CHUNK
    q = SSD_CHUNK
    fwd = lambda i, s: (i, s, 0)
    bwd = lambda i, s: (i, nc - 1 - s, 0)
    hspec = pl.BlockSpec((None, 2, SSD_STATE, w), lambda i, s: (i, 0, 0, 0))
    in_specs = [pl.BlockSpec((None, q, cd), fwd), pl.BlockSpec((None, q, cd), bwd),
                pl.BlockSpec((None, q, LANE), fwd), pl.BlockSpec((None, q, LANE), bwd),
                hspec,
                pl.BlockSpec((1, LANE), lambda i, s: (0, 0)), pl.BlockSpec((1, LANE), lambda i, s: (0, 0))]
    out_specs = [hspec]
    out_shape = [jax.ShapeDtypeStruct((b, 2, SSD_STATE, w), F32)]
    if need_y:
        out_specs = [pl.BlockSpec((None, q, w), fwd), pl.BlockSpec((None, q, w), bwd)] + out_specs
        out_shape = [jax.ShapeDtypeStruct((b, l, w), BF16)] * 2 + out_shape
    return pl.pallas_call(
        functools.partial(_ssd_kernel, nh=nh, ng=ng, need_y=need_y), grid=(b, nc),
        in_specs=in_specs, out_specs=out_specs, out_shape=out_shape,
        compiler_params=_cp(("parallel", "arbitrary"), 48),
        name="ssd_scan" if need_y else "ssd_scan_ctx")(xbc, xbc, dt, dt, h0, bias, alog)


def _gnorm_kernel(yf_ref, yb_ref, xs_ref, z_ref, d_ref, g_ref, o_ref, t_ref, *, rows, tb, gw):
    y = d_ref[...] * xs_ref[...].astype(F32) + yf_ref[...].astype(F32) + yb_ref[...].astype(F32)
    y = y * _silu(z_ref[...].astype(F32))
    bw = y.shape[1]
    parts = []
    for g in range(bw // gw):
        yg = y[:, g * gw:(g + 1) * gw]
        parts.append(yg * lax.rsqrt(jnp.mean(yg * yg, axis=-1, keepdims=True) + EPS))
    res = jnp.concatenate(parts, axis=1) * g_ref[...]
    pitch = rows + XPOSE_PAD
    for wl in range(tb):
        for kk in range(bw // LANE):
            t_ref[kk, wl * pitch:wl * pitch + rows, :] = res[wl * rows:(wl + 1) * rows, kk * LANE:(kk + 1) * LANE]
    for r in range(rows):
        o_ref[r] = jnp.concatenate(
            [t_ref[kk, pl.ds(r, tb, stride=pitch), :] for kk in range(bw // LANE)], axis=1).astype(o_ref.dtype)


def _gnorm_call(yf, yb, xbc, proj, dvec, gvec, ng):
    b, l, w = yf.shape
    rows = l // GRID_W
    tb = XPOSE_TILE
    gw = w // ng
    bw = gw * (2 if ng % 2 == 0 else 1)
    blk = pl.BlockSpec((None, tb * rows, bw), lambda i, s, g: (i, s, g))
    vec = pl.BlockSpec((1, bw), lambda i, s, g: (0, g))
    out = pl.pallas_call(
        functools.partial(_gnorm_kernel, rows=rows, tb=tb, gw=gw), grid=(b, GRID_W // tb, w // bw),
        in_specs=[blk, blk, blk, blk, vec, vec],
        out_specs=pl.BlockSpec((None, rows, tb, bw), lambda i, s, g: (i, 0, s, g)),
        out_shape=jax.ShapeDtypeStruct((b, rows, GRID_W, w), BF16),
        scratch_shapes=[pltpu.VMEM((bw // LANE, tb * (rows + XPOSE_PAD), LANE), F32)],
        compiler_params=_cp(("parallel", "parallel", "parallel"), 40),
        name="ssd_gnorm")(yf, yb, xbc, proj, dvec, gvec)
    return out.reshape(b * l, w)


def _moe_kernel(e_ref, b0_ref, nb_ref, *refs, nsub, nh):
    xs = refs[:nsub]
    wg_ref, wu_ref, wd_ref, o_ref, act_ref, wdb_ref = refs[nsub:]
    s = pl.program_id(0)
    t = pl.program_id(1)
    nb = nb_ref[s]

    @pl.when(jnp.logical_and(t < nh, nb > 0))
    def _():
        wg = wg_ref[...].astype(BF16)
        wu = wu_ref[...].astype(BF16)

        def act_block(k):
            x = xs[k][...]
            gt = jnp.dot(x, wg, preferred_element_type=F32)
            up = jnp.dot(x, wu, preferred_element_type=F32)
            act_ref[k, t] = (_silu(gt) * up).astype(BF16)

        act_block(0)
        for k in range(1, nsub):
            pl.when(k < nb)(functools.partial(act_block, k))

    @pl.when(jnp.logical_and(t >= nh, nb > 0))
    def _():
        wdb = wd_ref[...].astype(BF16)
        wdb_ref[...] = wdb

        def down_block(k, w):
            a = jnp.concatenate([act_ref[k, h] for h in range(nh)], axis=1)
            o_ref[k * MOE_BLOCK:(k + 1) * MOE_BLOCK, :] = jnp.dot(a, w, preferred_element_type=F32).astype(o_ref.dtype)

        down_block(0, wdb)
        for k in range(1, nsub):
            pl.when(k < nb)(lambda k=k: down_block(k, wdb_ref[...]))

            @pl.when(k >= nb)
            def _():
                o_ref[k * MOE_BLOCK:(k + 1) * MOE_BLOCK, :] = jnp.zeros((MOE_BLOCK, o_ref.shape[1]), o_ref.dtype)

    @pl.when(jnp.logical_and(t >= nh, nb == 0))
    def _():
        o_ref[...] = jnp.zeros(o_ref.shape, o_ref.dtype)


def _moe_call(sup_e, sup_b0, sup_nb, xs, w_gate, w_up, w_down):
    n_slots, d = xs.shape
    hid = w_gate.shape[-1]
    th = min(256, hid)
    tn = min(1024, d)
    nh = hid // th
    nt = d // tn
    ns = sup_e.shape[0]
    rsup = MOE_SUB * MOE_BLOCK

    def x_map(k):
        def index(s, t, e, b0, nb):
            sx = jnp.where(t < nh, s, jnp.minimum(s + 1, ns - 1))
            return (b0[sx * MOE_SUB + k], 0)
        return index

    def hid_idx(s, t, nb):
        return jnp.where(nb[s] > 0, jnp.minimum(t, nh - 1), nh - 1)

    def col_idx(s, t, nb):
        return jnp.where(nb[s] > 0, jnp.clip(t - nh, 0, nt - 1), nt - 1)

    gs = pltpu.PrefetchScalarGridSpec(
        num_scalar_prefetch=3, grid=(ns, nh + nt),
        in_specs=[pl.BlockSpec((MOE_BLOCK, d), x_map(k)) for k in range(MOE_SUB)] + [
            pl.BlockSpec((None, d, th), lambda s, t, e, b0, nb: (e[s], 0, hid_idx(s, t, nb))),
            pl.BlockSpec((None, d, th), lambda s, t, e, b0, nb: (e[s], 0, hid_idx(s, t, nb))),
            pl.BlockSpec((None, hid, tn), lambda s, t, e, b0, nb: (e[s], 0, col_idx(s, t, nb)))],
        out_specs=pl.BlockSpec((rsup, tn), lambda s, t, e, b0, nb: (s, jnp.clip(t - nh, 0, nt - 1))),
        scratch_shapes=[pltpu.VMEM((MOE_SUB, nh, MOE_BLOCK, th), BF16), pltpu.VMEM((hid, tn), BF16)])
    return pl.pallas_call(
        functools.partial(_moe_kernel, nsub=MOE_SUB, nh=nh), grid_spec=gs,
        out_shape=jax.ShapeDtypeStruct((ns * rsup, d), BF16),
        compiler_params=_cp(("arbitrary", "arbitrary"), 60), name="moe_experts")(
            sup_e, sup_b0, sup_nb, *([xs] * MOE_SUB), w_gate, w_up, w_down)


def _final_kernel(x_ref, ya_ref, yb_ref, w_ref, g2_ref, fg_ref, o_ref):
    wts = w_ref[...]
    moe = ya_ref[...].astype(F32) * wts[:, 0:1] + yb_ref[...].astype(F32) * wts[:, 1:2]
    x = x_ref[...] + g2_ref[...] * moe
    ms = jnp.mean(x * x, axis=-1, keepdims=True)
    o_ref[...] = x * lax.rsqrt(ms + EPS) * fg_ref[...]


def _final_call(x, ya, yb, wts, mods, ig, fg):
    b, l, d = x.shape
    tr = min(256, l)
    nt = l // tr
    row = lambda i, r: (i * nt + r, 0)
    return pl.pallas_call(
        _final_kernel, grid=(b, nt),
        in_specs=[pl.BlockSpec((None, tr, d), lambda i, r: (i, r, 0)),
                  pl.BlockSpec((tr, d), row), pl.BlockSpec((tr, d), row),
                  pl.BlockSpec((tr, MOE_TOP_K), row),
                  pl.BlockSpec((None, None, 1, d), lambda i, r: (i, ig, 0, 0)),
                  pl.BlockSpec((1, d), lambda i, r: (0, 0))],
        out_specs=pl.BlockSpec((None, tr, d), lambda i, r: (i, r, 0)),
        out_shape=jax.ShapeDtypeStruct((b, l, d), F32),
        compiler_params=_cp(("parallel", "parallel"), 40), name="combine_final")(x, ya, yb, wts, mods, fg)


def _route(sel, ne):
    m = sel.shape[0]
    weights = sel[:, MOE_TOP_K:2 * MOE_TOP_K]
    n_assign = m * MOE_TOP_K
    flat_e = jnp.concatenate([sel[:, k] for k in range(MOE_TOP_K)], axis=0).astype(jnp.int32)
    onehot = (flat_e[:, None] == jnp.arange(ne, dtype=flat_e.dtype)[None, :]).astype(jnp.int32)
    csum = jnp.cumsum(onehot, axis=0)
    counts = csum[-1]
    local = jnp.sum(csum * onehot, axis=1) - 1
    nblk_e = (counts + MOE_BLOCK - 1) // MOE_BLOCK
    padded = nblk_e * MOE_BLOCK
    pad_end = jnp.cumsum(padded)
    pad_start = pad_end - padded
    dest = pad_start[flat_e] + local
    n_blocks = -(-(n_assign + ne * (MOE_BLOCK - 1)) // MOE_BLOCK)
    tok = jnp.arange(n_assign, dtype=jnp.int32) % m
    n_slots = n_blocks * MOE_BLOCK
    slot_tok = (jnp.arange(n_slots, dtype=jnp.int32) % m).at[dest].set(tok)
    rsup = MOE_SUB * MOE_BLOCK
    ns_e = (nblk_e + MOE_SUB - 1) // MOE_SUB
    sup_end = jnp.cumsum(ns_e)
    sup_start = sup_end - ns_e
    n_sup = (n_blocks + (MOE_SUB - 1) * ne) // MOE_SUB
    sidx = jnp.arange(n_sup, dtype=jnp.int32)
    last = sup_end[-1] - 1
    s_eff = jnp.minimum(sidx, last)
    e_s = jnp.minimum(jnp.searchsorted(sup_end, s_eff, side='right'), ne - 1).astype(jnp.int32)
    k_s = s_eff - sup_start[e_s]
    b0_s = pad_start[e_s] // MOE_BLOCK + MOE_SUB * k_s
    nb_s = jnp.clip(nblk_e[e_s] - MOE_SUB * k_s, 0, MOE_SUB)
    used = sidx <= last
    sup_nb = jnp.where(used, nb_s, 0).astype(jnp.int32)
    kk = jnp.arange(MOE_SUB, dtype=jnp.int32)[None, :]
    sup_b0 = jnp.maximum(lax.cummax(jnp.where(kk < sup_nb[:, None], b0_s[:, None] + kk, -1), axis=0), 0)
    sup_b0 = sup_b0.reshape(-1).astype(jnp.int32)
    pos = ((sup_start[flat_e] + local // rsup) * rsup + local % rsup).astype(jnp.int32)
    return weights, slot_tok, e_s, sup_b0, sup_nb, pos


def kernel(x, c, ctx, c_ctx, w_mod, b_mod, norm1_g, w_in, s5_lam_re, s5_lam_im, s5_log_dt, s5_b_re, s5_b_im, s5_c_re, s5_c_im, s5_d, s5_w_val, s5_w_gate, ssd_conv_w, ssd_conv_b, ssd_a_log, ssd_dt_bias, ssd_d, ssd_norm_g, ssd_w_out, w_o, norm2_g, moe_w_group, moe_b_group, moe_w_expert, moe_b_expert, moe_w_gate, moe_w_up, moe_w_down, final_g):
    depth = w_mod.shape[0]
    assert depth == 1, "single-layer block"
    bsz, n_lat, d = x.shape
    l_ctx = ctx.shape[1]
    w5 = s5_d.shape[1]
    nh = ssd_d.shape[1]
    w = nh * SSD_HEAD_DIM
    conv_dim = ssd_conv_w.shape[2]
    ng = (conv_dim - w) // (2 * SSD_STATE)
    ssd_in = w + conv_dim + 2 * nh
    o1, o2 = w5, w5 + ssd_in
    l = 0

    cc = jnp.concatenate([c, c_ctx[None, :]], axis=0)
    cc = jnp.pad(cc, ((0, (-cc.shape[0]) % 8), (0, 0)))
    mods = _mod_call(cc, w_mod[l], b_mod[l]).reshape(cc.shape[0], 6, 1, d)
    i_sh1, i_sc1, i_g1, i_sh2, i_sc2, i_g2 = range(6)

    w_in_l = w_in[l]
    o_dt = o1 + w + conv_dim

    hn_rm, hn_cm = _norm_lat_call(x, norm1_g[l], mods, i_sc1, i_sh1)
    hc = _norm_ctx_call(ctx, norm1_g[l], mods, bsz, i_sc1, i_sh1)
    hn_rm = hn_rm.reshape(bsz * n_lat, d)
    hn_cm = hn_cm.reshape(bsz * n_lat, d)
    hc = hc.reshape(bsz * l_ctx, d)

    u_lat = _mm_call(hn_rm, w_in_l, BF16, "in_s5", 0, o1).reshape(bsz, n_lat, w5)
    u_ctx = _mm_call(hc, w_in_l, BF16, "in_s5_ctx", 0, o1).reshape(bsz, l_ctx, w5)
    gates = _mm_call(hn_rm, w_in_l, BF16, "in_gates", o2, 2 * d)
    z_lat = _mm_call(hn_cm, w_in_l, BF16, "in_z", o1, w).reshape(bsz, n_lat, w)
    xbc_lat = _proj_conv_call(hn_cm.reshape(bsz, n_lat, d), w_in_l, o1 + w, ssd_conv_w[l], ssd_conv_b[l])
    p_ctx = _mm_call(hc, w_in_l, BF16, "in_xbc_ctx", o1 + w, conv_dim).reshape(bsz, l_ctx, conv_dim)
    dt_lat = _mm_call(hn_cm, w_in_l, F32, "in_dt", o_dt, 2 * nh).reshape(bsz, n_lat, LANE)
    dt_ctx = _mm_call(hc, w_in_l, F32, "in_dt_ctx", o_dt, 2 * nh).reshape(bsz, l_ctx, LANE)

    nj = w5 // LANE
    bmat, cmat, dmat, lre, lim = _s5_params(s5_lam_re[l], s5_lam_im[l], s5_log_dt[l], s5_b_re[l], s5_b_im[l],
                                            s5_c_re[l], s5_c_im[l], bsz)
    s5_zero = jnp.zeros((nj, 2, 2 * bsz, lre.shape[-1]), F32)
    (s5_ctx,) = _s5_call(u_ctx, bmat, cmat, dmat, lre, lim, s5_zero, False)
    ya_f, ya_b, _ = _s5_call(u_lat, bmat, cmat, dmat, lre, lim, s5_ctx, True)

    xbc_ctx = _conv_call(p_ctx, ssd_conv_w[l], ssd_conv_b[l], 0)
    pad_h = LANE - 2 * nh
    bias = jnp.pad(ssd_dt_bias[l].astype(F32).reshape(1, 2 * nh), ((0, 0), (0, pad_h)))
    alog = jnp.pad(ssd_a_log[l].astype(F32).reshape(1, 2 * nh), ((0, 0), (0, pad_h)))
    h_zero = jnp.zeros((bsz, 2, SSD_STATE, w), F32)
    (h_ctx,) = _ssd_call(xbc_ctx, dt_ctx, h_zero, bias, alog, nh, ng, False)
    y_f, y_b, _ = _ssd_call(xbc_lat, dt_lat, h_ctx, bias, alog, nh, ng, True)
    d_vec = jnp.repeat(ssd_d[l].astype(F32), SSD_HEAD_DIM).reshape(1, w)
    y_ssd = _gnorm_call(y_f, y_b, xbc_lat, z_lat, d_vec, ssd_norm_g[l].astype(F32).reshape(1, w), ng)

    m_lat = bsz * n_lat
    part_a = _glu_call(u_lat.reshape(m_lat, w5), ya_f.reshape(m_lat, w5), ya_b.reshape(m_lat, w5),
                       s5_d[l].astype(F32).reshape(1, w5),
                       s5_w_val[l].astype(BF16), s5_w_gate[l].astype(BF16), gates)
    merged = _merge_call(y_ssd, ssd_w_out[l], gates, part_a)
    x1 = _resid_call(merged, w_o[l], x, mods, i_g1)

    ngr = moe_w_group.shape[-1]
    ne = moe_w_expert.shape[-1]
    wr = jnp.concatenate([moe_w_group[l], moe_w_expert[l]], axis=1).astype(F32)
    wr = jnp.pad(wr, ((0, 0), (0, (-(ngr + ne)) % LANE)))
    br = jnp.concatenate([moe_b_group[l], moe_b_expert[l]]).astype(F32)
    br = jnp.pad(br, (0, (-(ngr + ne)) % LANE)).reshape(1, -1)
    hx, sel = _norm_router_call(x1, norm2_g[l], mods, i_sc2, i_sh2, wr, br, ngr, ne)
    m = bsz * n_lat
    hx = hx.reshape(m, d)
    weights, slot_tok, sup_e, sup_b0, sup_nb, pos = _route(sel.reshape(m, -1), ne)
    xs = hx[slot_tok]
    y_slots = _moe_call(sup_e, sup_b0, sup_nb, xs, moe_w_gate[l], moe_w_up[l], moe_w_down[l])
    ya = y_slots[pos[:m]]
    yb = y_slots[pos[m:]]
    return _final_call(x1, ya, yb, weights.astype(F32), mods, i_g2, final_g.reshape(1, d))
```

```python
import functools
import math

import jax
import jax.numpy as jnp
from jax import lax
from jax.experimental import pallas as pl
from jax.experimental.pallas import tpu as pltpu

F32 = jnp.float32
BF16 = jnp.bfloat16

GRID_W = 64
EPS = 1e-6
LANE = 128
SSD_HEAD_DIM = 64
SSD_STATE = 128
SSD_CHUNK = 128
SSD_CONV = 5
MOE_TOP_K = 2
MOE_BLOCK = 256
MOE_SUB = 4
XPOSE_TILE = 16
XPOSE_PAD = 8
S5_CHUNK = 256
S5_ROW_PAD = 8


def _cp(sem, mb):
    return pltpu.CompilerParams(dimension_semantics=sem, vmem_limit_bytes=mb * 1024 * 1024)


def _sigmoid(x):
    return 1.0 / (1.0 + jnp.exp(-x))


def _silu(x):
    return x * _sigmoid(x)


def _mod_kernel(c_ref, w_ref, b_ref, o_ref):
    s = _silu(c_ref[...])
    o_ref[...] = jnp.dot(s.astype(BF16), w_ref[...].astype(BF16),
                         preferred_element_type=F32) + b_ref[...]


def _mod_call(cc, w, b):
    r, d = cc.shape
    n = w.shape[1]
    tn = min(512, n)
    return pl.pallas_call(
        _mod_kernel, grid=(n // tn,),
        in_specs=[pl.BlockSpec((r, d), lambda j: (0, 0)),
                  pl.BlockSpec((d, tn), lambda j: (0, j)),
                  pl.BlockSpec((1, tn), lambda j: (0, j))],
        out_specs=pl.BlockSpec((r, tn), lambda j: (0, j)),
        out_shape=jax.ShapeDtypeStruct((r, n), F32),
        compiler_params=_cp(("parallel",), 40), name="mod")(cc, w, b.reshape(1, n))


def _rmsmod(x, g, sc, sh):
    ms = jnp.mean(x * x, axis=-1, keepdims=True)
    return (x * lax.rsqrt(ms + EPS) * g) * (1.0 + sc) + sh


def _norm_lat_kernel(x_ref, g_ref, sc_ref, sh_ref, orm_ref, ocm_ref, t_ref, *, tb, d):
    pitch = tb + XPOSE_PAD
    for r in range(tb):
        y = _rmsmod(x_ref[r], g_ref[...], sc_ref[...], sh_ref[...])
        orm_ref[r] = y.astype(BF16)
        for kk in range(d // LANE):
            t_ref[kk, r * pitch:r * pitch + tb, :] = y[:, kk * LANE:(kk + 1) * LANE]
    for wl in range(tb):
        ocm_ref[wl] = jnp.concatenate(
            [t_ref[kk, pl.ds(wl, tb, stride=pitch), :] for kk in range(d // LANE)], axis=1).astype(BF16)


def _norm_lat_call(x, g, mods, isc, ish):
    b, l, d = x.shape
    rows = l // GRID_W
    tb = XPOSE_TILE
    xv = x.reshape(b, rows, GRID_W, d)
    orm, ocm = pl.pallas_call(
        functools.partial(_norm_lat_kernel, tb=tb, d=d), grid=(b, rows // tb, GRID_W // tb),
        in_specs=[pl.BlockSpec((None, tb, tb, d), lambda i, r, c: (i, r, c, 0)),
                  pl.BlockSpec((1, d), lambda i, r, c: (0, 0)),
                  pl.BlockSpec((None, None, 1, d), lambda i, r, c: (i, isc, 0, 0)),
                  pl.BlockSpec((None, None, 1, d), lambda i, r, c: (i, ish, 0, 0))],
        out_specs=[pl.BlockSpec((None, tb, tb, d), lambda i, r, c: (i, r, c, 0)),
                   pl.BlockSpec((None, tb, tb, d), lambda i, r, c: (i, c, r, 0))],
        out_shape=[jax.ShapeDtypeStruct((b, rows, GRID_W, d), BF16),
                   jax.ShapeDtypeStruct((b, GRID_W, rows, d), BF16)],
        scratch_shapes=[pltpu.VMEM((d // LANE, tb * (tb + XPOSE_PAD), LANE), F32)],
        compiler_params=_cp(("parallel", "parallel", "parallel"), 48),
        name="norm1_lat")(xv, g.reshape(1, d), mods, mods)
    return orm.reshape(b, l, d), ocm.reshape(b, l, d)


def _norm_kernel(x_ref, g_ref, sc_ref, sh_ref, o_ref):
    o_ref[...] = _rmsmod(x_ref[...], g_ref[...], sc_ref[...], sh_ref[...]).astype(o_ref.dtype)


def _norm_ctx_call(x, g, mods, row, isc, ish):
    b, l, d = x.shape
    tr = min(256, l)
    return pl.pallas_call(
        _norm_kernel, grid=(b, l // tr),
        in_specs=[pl.BlockSpec((None, tr, d), lambda i, r: (i, r, 0)),
                  pl.BlockSpec((1, d), lambda i, r: (0, 0)),
                  pl.BlockSpec((None, None, 1, d), lambda i, r: (row, isc, 0, 0)),
                  pl.BlockSpec((None, None, 1, d), lambda i, r: (row, ish, 0, 0))],
        out_specs=pl.BlockSpec((None, tr, d), lambda i, r: (i, r, 0)),
        out_shape=jax.ShapeDtypeStruct((b, l, d), BF16),
        compiler_params=_cp(("parallel", "parallel"), 40), name="norm1_ctx")(x, g.reshape(1, d), mods, mods)


def _first_argmax(v, vmax, lane):
    return jnp.min(jnp.where(v == vmax, lane, float(LANE)), axis=-1, keepdims=True)


def _norm_router_kernel(x_ref, g_ref, sc_ref, sh_ref, wr_ref, br_ref, o_ref, sel_ref, *, ngr, ne):
    y = _rmsmod(x_ref[...], g_ref[...], sc_ref[...], sh_ref[...])
    o_ref[...] = y.astype(BF16)
    y_hi = y.astype(BF16)
    y_lo = (y - y_hi.astype(F32)).astype(BF16)
    w_hi = wr_ref[0]
    lg = (jnp.dot(y_hi, w_hi, preferred_element_type=F32) + jnp.dot(y_hi, wr_ref[1], preferred_element_type=F32)
          + jnp.dot(y_lo, w_hi, preferred_element_type=F32)) + br_ref[...]
    epg = ne // ngr
    lane = lax.broadcasted_iota(jnp.int32, lg.shape, 1).astype(F32)
    ninf = -jnp.inf
    gl = jnp.where(lane < ngr, lg, ninf)
    ge = jnp.exp(gl - jnp.max(gl, axis=-1, keepdims=True))
    g_prob = ge / jnp.sum(ge, axis=-1, keepdims=True)
    g_p = jnp.max(g_prob, axis=-1, keepdims=True)
    lo = ngr + epg * _first_argmax(g_prob, g_p, lane)
    cand = jnp.where(jnp.logical_and(lane >= lo, lane < lo + epg), lg, ninf)
    v1 = jnp.max(cand, axis=-1, keepdims=True)
    l1 = _first_argmax(cand, v1, lane)
    rest = jnp.where(lane == l1, ninf, cand)
    v2 = jnp.max(rest, axis=-1, keepdims=True)
    l2 = _first_argmax(rest, v2, lane)
    ex = jnp.exp(v2 - v1)
    w1 = g_p * (1.0 / (1.0 + ex))
    w2 = g_p * (ex / (1.0 + ex))
    sel_ref[...] = jnp.where(lane == 0, l1 - ngr, jnp.where(lane == 1, l2 - ngr,
                             jnp.where(lane == 2, w1, jnp.where(lane == 3, w2, 0.0))))


def _norm_router_call(x, g, mods, isc, ish, wr, br, ngr, ne):
    b, l, d = x.shape
    tr = min(256, l)
    nr = wr.shape[1]
    w_hi = wr.astype(BF16)
    wr = jnp.stack([w_hi, (wr - w_hi.astype(F32)).astype(BF16)], axis=0)
    return pl.pallas_call(
        functools.partial(_norm_router_kernel, ngr=ngr, ne=ne), grid=(b, l // tr),
        in_specs=[pl.BlockSpec((None, tr, d), lambda i, r: (i, r, 0)),
                  pl.BlockSpec((1, d), lambda i, r: (0, 0)),
                  pl.BlockSpec((None, None, 1, d), lambda i, r: (i, isc, 0, 0)),
                  pl.BlockSpec((None, None, 1, d), lambda i, r: (i, ish, 0, 0)),
                  pl.BlockSpec((2, d, nr), lambda i, r: (0, 0, 0)),
                  pl.BlockSpec((1, nr), lambda i, r: (0, 0))],
        out_specs=[pl.BlockSpec((None, tr, d), lambda i, r: (i, r, 0)),
                   pl.BlockSpec((None, tr, nr), lambda i, r: (i, r, 0))],
        out_shape=[jax.ShapeDtypeStruct((b, l, d), BF16), jax.ShapeDtypeStruct((b, l, nr), F32)],
        compiler_params=_cp(("parallel", "parallel"), 40),
        name="norm2_router")(x, g.reshape(1, d), mods, mods, wr, br)


def _mm_kernel(a_ref, b_ref, o_ref):
    o_ref[...] = jnp.dot(a_ref[...], b_ref[...].astype(BF16), preferred_element_type=F32).astype(o_ref.dtype)


def _mm_tiles(m, n):
    tm = min(1024, m)
    tn = min(512, n)
    return tm, tn


def _mm_call(a, b, out_dtype, name, col0=0, n=None):
    m, k = a.shape
    n = b.shape[1] - col0 if n is None else n
    tm, tn = _mm_tiles(m, n)
    if col0 % LANE or n % tn or n % LANE:
        b = b[:, col0:col0 + n]
        pad = (-n) % LANE
        b = jnp.pad(b, ((0, 0), (0, pad)))
        n, col0 = n + pad, 0
        tm, tn = _mm_tiles(m, n)
    return pl.pallas_call(
        _mm_kernel, grid=(m // tm, n // tn),
        in_specs=[pl.BlockSpec((tm, k), lambda i, j: (i, 0)),
                  pl.BlockSpec((pl.Element(k), pl.Element(tn)), lambda i, j: (0, (col0 // LANE + j * (tn // LANE)) * LANE))],
        out_specs=pl.BlockSpec((tm, tn), lambda i, j: (i, j)),
        out_shape=jax.ShapeDtypeStruct((m, n), out_dtype),
        compiler_params=_cp(("parallel", "parallel"), 48), name=name)(a, b)


def _gelu_tanh(x):
    return x * (0.5 * (1.0 + jnp.tanh(math.sqrt(2.0 / math.pi) * (x + 0.044715 * (x * x * x)))))


def _glu_kernel(u_ref, yf_ref, yb_ref, d_ref, wv_ref, wg_ref, gate_ref, o_ref, a_ref):
    @pl.when(pl.program_id(1) == 0)
    def _():
        y = d_ref[...] * u_ref[...].astype(F32) + yf_ref[...].astype(F32) + yb_ref[...].astype(F32)
        a_ref[...] = _gelu_tanh(y).astype(BF16)

    a = a_ref[...]
    val = jnp.dot(a, wv_ref[...], preferred_element_type=F32)
    gl = jnp.dot(a, wg_ref[...], preferred_element_type=F32)
    o_ref[...] = (_sigmoid(gate_ref[...].astype(F32)) * (val * _sigmoid(gl))).astype(o_ref.dtype)


def _glu_call(u, yf, yb, dvec, wv, wg, gates):
    m, k = u.shape
    n = wv.shape[1]
    tm, tn = _mm_tiles(m, n)
    row = pl.BlockSpec((tm, k), lambda i, j: (i, 0))
    return pl.pallas_call(
        _glu_kernel, grid=(m // tm, n // tn),
        in_specs=[row, row, row,
                  pl.BlockSpec((1, k), lambda i, j: (0, 0)),
                  pl.BlockSpec((k, tn), lambda i, j: (0, j)),
                  pl.BlockSpec((k, tn), lambda i, j: (0, j)),
                  pl.BlockSpec((tm, tn), lambda i, j: (i, j))],
        out_specs=pl.BlockSpec((tm, tn), lambda i, j: (i, j)),
        out_shape=jax.ShapeDtypeStruct((m, n), BF16),
        scratch_shapes=[pltpu.VMEM((tm, k), BF16)],
        compiler_params=_cp(("parallel", "arbitrary"), 56), name="s5_glu")(u, yf, yb, dvec, wv, wg, gates)


def _merge_kernel(a_ref, w_ref, gate_ref, pa_ref, o_ref):
    br = jnp.dot(a_ref[...], w_ref[...].astype(BF16), preferred_element_type=F32)
    o_ref[...] = (pa_ref[...].astype(F32) + _sigmoid(gate_ref[...].astype(F32)) * br).astype(o_ref.dtype)


def _merge_call(a, w, gates, part_a):
    m, k = a.shape
    n = w.shape[1]
    tm, tn = _mm_tiles(m, n)
    off = n // tn
    return pl.pallas_call(
        _merge_kernel, grid=(m // tm, n // tn),
        in_specs=[pl.BlockSpec((tm, k), lambda i, j: (i, 0)),
                  pl.BlockSpec((k, tn), lambda i, j: (0, j)),
                  pl.BlockSpec((tm, tn), lambda i, j: (i, j + off)),
                  pl.BlockSpec((tm, tn), lambda i, j: (i, j))],
        out_specs=pl.BlockSpec((tm, tn), lambda i, j: (i, j)),
        out_shape=jax.ShapeDtypeStruct((m, n), BF16),
        compiler_params=_cp(("parallel", "parallel"), 48), name="ssd_out_merge")(a, w, gates, part_a)


def _resid_kernel(a_ref, w_ref, x_ref, g_ref, o_ref):
    mix = jnp.dot(a_ref[...], w_ref[...].astype(BF16), preferred_element_type=F32)
    o_ref[...] = x_ref[...] + g_ref[...] * mix


def _resid_call(a, w, x, mods, ig):
    b, l, d = x.shape
    k = a.shape[1]
    tm, tn = _mm_tiles(l, d)
    nt = l // tm
    return pl.pallas_call(
        _resid_kernel, grid=(b * nt, d // tn),
        in_specs=[pl.BlockSpec((tm, k), lambda i, j: (i, 0)),
                  pl.BlockSpec((k, tn), lambda i, j: (0, j)),
                  pl.BlockSpec((None, tm, tn), lambda i, j: (i // nt, i % nt, j)),
                  pl.BlockSpec((None, None, 1, tn), lambda i, j: (i // nt, ig, 0, j))],
        out_specs=pl.BlockSpec((None, tm, tn), lambda i, j: (i // nt, i % nt, j)),
        out_shape=jax.ShapeDtypeStruct((b, l, d), F32),
        compiler_params=_cp(("parallel", "parallel"), 48), name="w_o_resid")(a, w, x, mods)


def _s5_kernel(*refs, nb, tp, p8, need_y):
    if need_y:
        uf_ref, ub_ref, b_ref, c_ref, d_ref, lre_ref, lim_ref, h0_ref, yf_ref, yb_ref, h_ref, buf_ref, il_ref = refs
    else:
        uf_ref, ub_ref, b_ref, c_ref, d_ref, lre_ref, lim_ref, h0_ref, h_ref, buf_ref, il_ref = refs
    q = 2 * nb
    nk = p8 // LANE
    pitch = tp + S5_ROW_PAD

    @pl.when(pl.program_id(1) == 0)
    def _():
        h_ref[...] = h0_ref[...]

    rev = (lax.broadcasted_iota(jnp.int32, (tp, tp), 0) + lax.broadcasted_iota(jnp.int32, (tp, tp), 1)
           == tp - 1).astype(BF16)
    def pair_rows(u_ref, b, newer_first):
        il_ref[...] = u_ref[b].astype(F32)
        even = il_ref[pl.ds(0, tp, stride=2), :].astype(BF16)
        odd = il_ref[pl.ds(1, tp, stride=2), :].astype(BF16)
        return jnp.concatenate([odd, even] if newer_first else [even, odd], axis=1)

    lhs = []
    for d in range(2):
        if d == 0:
            u = jnp.concatenate([pair_rows(uf_ref, b, False) for b in range(nb)], axis=0)
        else:
            u = jnp.concatenate(
                [jnp.dot(rev, pair_rows(ub_ref, b, True), preferred_element_type=F32).astype(BF16)
                 for b in range(nb)], axis=0)
        lhs.append(u)
        bu = jnp.dot(u, b_ref[d], preferred_element_type=F32)
        for b in range(nb):
            r0 = (d * nb + b) * pitch
            for k in range(2 * nk):
                buf_ref[k, r0:r0 + tp, :] = bu[b * tp:(b + 1) * tp, k * LANE:(k + 1) * LANE]
    ar = [lre_ref[:, k * LANE:(k + 1) * LANE] for k in range(nk)]
    ai = [lim_ref[:, k * LANE:(k + 1) * LANE] for k in range(nk)]

    def step(s, carry):
        rows = pl.ds(s, q, stride=pitch)
        out = []
        for k in range(nk):
            hr, hi = carry[k]
            nr = ar[k] * hr - ai[k] * hi + buf_ref[k, rows, :]
            ni = ar[k] * hi + ai[k] * hr + buf_ref[nk + k, rows, :]
            buf_ref[k, rows, :] = hr
            buf_ref[nk + k, rows, :] = hi
            out.append((nr, ni))
        return tuple(out)

    init = tuple((h_ref[0, :, k * LANE:(k + 1) * LANE], h_ref[1, :, k * LANE:(k + 1) * LANE]) for k in range(nk))
    fin = lax.fori_loop(0, tp, step, init, unroll=8)
    for k in range(nk):
        h_ref[0, :, k * LANE:(k + 1) * LANE] = fin[k][0]
        h_ref[1, :, k * LANE:(k + 1) * LANE] = fin[k][1]

    if need_y:
        for d in range(2):
            h = jnp.concatenate(
                [jnp.concatenate([buf_ref[k, (d * nb + b) * pitch:(d * nb + b) * pitch + tp, :].astype(BF16)
                                  for k in range(2 * nk)], axis=1) for b in range(nb)], axis=0)
            y = (jnp.dot(h, c_ref[d], preferred_element_type=F32)
                 + jnp.dot(lhs[d], d_ref[d], preferred_element_type=F32))
            for b in range(nb):
                yb = y[b * tp:(b + 1) * tp]
                if d == 1:
                    yb = jnp.dot(rev, yb.astype(BF16), preferred_element_type=F32)
                first, second = (0, 1) if d == 0 else (1, 0)
                il_ref[pl.ds(first, tp, stride=2), :] = yb[:, 0:LANE]
                il_ref[pl.ds(second, tp, stride=2), :] = yb[:, LANE:2 * LANE]
                (yf_ref if d == 0 else yb_ref)[b] = il_ref[...].astype(BF16)


def _s5_call(u, bmat, cmat, dmat, lre, lim, h0, need_y):
    nb, l, w5 = u.shape
    nj = w5 // LANE
    q = 2 * nb
    t = min(S5_CHUNK, l)
    tp = t // 2
    p8 = lre.shape[-1]
    nc = l // t
    hspec = pl.BlockSpec((None, 2, q, p8), lambda j, c: (j, 0, 0, 0))
    ublk = (nb, t, LANE)
    in_specs = [pl.BlockSpec(ublk, lambda j, c: (0, c, j)), pl.BlockSpec(ublk, lambda j, c: (0, nc - 1 - c, j)),
                pl.BlockSpec((None, 2, 2 * LANE, 2 * p8), lambda j, c: (j, 0, 0, 0)),
                pl.BlockSpec((None, 2, 2 * p8, 2 * LANE), lambda j, c: (j, 0, 0, 0)),
                pl.BlockSpec((None, 2, 2 * LANE, 2 * LANE), lambda j, c: (j, 0, 0, 0)),
                pl.BlockSpec((None, q, p8), lambda j, c: (j, 0, 0)),
                pl.BlockSpec((None, q, p8), lambda j, c: (j, 0, 0)),
                hspec]
    out_specs = [hspec]
    out_shape = [jax.ShapeDtypeStruct((nj, 2, q, p8), F32)]
    scratch = [pltpu.VMEM((2 * p8 // LANE, q * (tp + S5_ROW_PAD), LANE), F32), pltpu.VMEM((t, LANE), F32)]
    if need_y:
        out_specs = [pl.BlockSpec(ublk, lambda j, c: (0, c, j)),
                     pl.BlockSpec(ublk, lambda j, c: (0, nc - 1 - c, j))] + out_specs
        out_shape = [jax.ShapeDtypeStruct((nb, l, w5), BF16)] * 2 + out_shape
    kern = functools.partial(_s5_kernel, nb=nb, tp=tp, p8=p8, need_y=need_y)
    return pl.pallas_call(
        kern, grid=(nj, nc), in_specs=in_specs, out_specs=out_specs, out_shape=out_shape,
        scratch_shapes=scratch, compiler_params=_cp(("parallel", "arbitrary"), 48),
        name="s5_scan" if need_y else "s5_scan_ctx")(u, u, bmat, cmat, dmat, lre, lim, h0)


def _s5_params(lam_re, lam_im, log_dt, b_re, b_im, c_re, c_im, nb):
    _, g, p = lam_re.shape
    s = b_re.shape[-1]
    gpb = LANE // s
    nj = g // gpb
    lam = lax.complex(lam_re.astype(F32), lam_im.astype(F32))
    lam_bar = jnp.exp(lam * jnp.exp(log_dt.astype(F32))[..., None])
    b_bar = ((lam_bar - 1.0) / lam)[..., None] * lax.complex(b_re.astype(F32), b_im.astype(F32))
    def quadrants(parts):
        r, c = parts[0].shape[-2:]
        same_group = (jnp.arange(gpb * r)[:, None] // r) == (jnp.arange(gpb * c)[None, :] // c)

        def spread(x):
            return jnp.where(same_group, jnp.tile(x.reshape(2, nj, gpb * r, c), (1, 1, 1, gpb)), 0.0)

        top = jnp.concatenate([spread(parts[0]), spread(parts[1])], axis=-1)
        bot = jnp.concatenate([spread(parts[2]), spread(parts[3])], axis=-1)
        return jnp.concatenate([top, bot], axis=-2).transpose(1, 0, 2, 3).astype(BF16)

    tb = lambda z: jnp.swapaxes(z, -1, -2)
    lb = lam_bar[..., None] * b_bar
    bmat = quadrants([tb(lb.real), tb(lb.imag), tb(b_bar.real), tb(b_bar.imag)])
    cc = lax.complex(c_re.astype(F32), c_im.astype(F32))
    c_l1 = cc * lam_bar[:, :, None, :]
    c_l2 = c_l1 * lam_bar[:, :, None, :]
    cmat = quadrants([tb(c_l1.real), tb(c_l2.real), -tb(c_l1.imag), -tb(c_l2.imag)])
    m0 = jnp.einsum('dgsp,dgpt->dgts', cc, b_bar).real
    m1 = jnp.einsum('dgsp,dgpt->dgts', c_l1, b_bar).real
    dmat = quadrants([m0, m1, jnp.zeros_like(m0), m0])

    def lam_of(part):
        v = part.reshape(2, nj, gpb * p).transpose(1, 0, 2)
        return jnp.repeat(v, nb, axis=1)

    lam2 = lam_bar * lam_bar
    return bmat, cmat, dmat, lam_of(lam2.real), lam_of(lam2.imag)


def _conv_silu(x, w, bias):
    l = x.shape[0]
    rows = lax.broadcasted_iota(jnp.int32, x.shape, 0)
    half = SSD_CONV // 2
    acc = x * w[half:half + 1, :] + bias
    for k in range(SSD_CONV):
        if k == half:
            continue
        off = k - half
        xs = pltpu.roll(x, shift=(-off) % l, axis=0)
        valid = jnp.logical_and(rows + off >= 0, rows + off < l)
        acc = acc + jnp.where(valid, xs, 0.0) * w[k:k + 1, :]
    return _silu(acc)


def _conv_kernel(x_ref, w_ref, b_ref, o_ref):
    o_ref[...] = _conv_silu(x_ref[...].astype(F32), w_ref[...], b_ref[...]).astype(o_ref.dtype)


def _proj_conv_kernel(a_ref, w_ref, cw_ref, cb_ref, o_ref):
    a = a_ref[...]
    wb = w_ref[...].astype(BF16)
    hw = wb.shape[1] // 2
    for c in range(2):
        cols = slice(c * hw, (c + 1) * hw)
        p = jnp.dot(a, wb[:, cols], preferred_element_type=F32)
        o_ref[:, cols] = _conv_silu(p, cw_ref[:, cols], cb_ref[:, cols]).astype(o_ref.dtype)


def _proj_conv_call(hn, w_all, col0, cw, cb):
    b, l, d = hn.shape
    c = cw.shape[1]
    tn = 512
    return pl.pallas_call(
        _proj_conv_kernel, grid=(b, c // tn),
        in_specs=[pl.BlockSpec((None, l, d), lambda i, j: (i, 0, 0), pipeline_mode=pl.Buffered(1)),
                  pl.BlockSpec((pl.Element(d), pl.Element(tn)),
                               lambda i, j: (0, (col0 // LANE + j * (tn // LANE)) * LANE)),
                  pl.BlockSpec((SSD_CONV, tn), lambda i, j: (0, j)),
                  pl.BlockSpec((1, tn), lambda i, j: (0, j))],
        out_specs=pl.BlockSpec((None, l, tn), lambda i, j: (i, 0, j)),
        out_shape=jax.ShapeDtypeStruct((b, l, c), BF16),
        compiler_params=_cp(("parallel", "arbitrary"), 60), name="in_xbc_conv")(hn, w_all, cw, cb.reshape(1, c))


def _conv_call(proj, w, bias, col0):
    b, l, _ = proj.shape
    c = w.shape[1]
    tc = 256
    off = col0 // tc
    return pl.pallas_call(
        _conv_kernel, grid=(b, c // tc),
        in_specs=[pl.BlockSpec((None, l, tc), lambda i, j: (i, 0, j + off)),
                  pl.BlockSpec((SSD_CONV, tc), lambda i, j: (0, j)),
                  pl.BlockSpec((1, tc), lambda i, j: (0, j))],
        out_specs=pl.BlockSpec((None, l, tc), lambda i, j: (i, 0, j)),
        out_shape=jax.ShapeDtypeStruct((b, l, c), BF16),
        compiler_params=_cp(("parallel", "parallel"), 40), name="ssd_conv")(proj, w, bias.reshape(1, c))


def _softplus(x):
    return jnp.maximum(x, 0.0) + jnp.log1p(jnp.exp(-jnp.abs(x)))


def _ssd_dir(xbc_ref, dt_ref, y_ref, h_ref, bias, a_neg, d, *, nh, ng, need_y):
    qn = SSD_CHUNK
    hd = SSD_HEAD_DIM
    w = nh * hd
    gw = w // ng
    gn = ng * SSD_STATE
    ii = lax.broadcasted_iota(jnp.int32, (qn, qn), 0)
    jj = lax.broadcasted_iota(jnp.int32, (qn, qn), 1)
    mask = (jj <= ii) if d == 0 else (jj >= ii)
    lmat = mask.astype(F32)
    dtv = _softplus(dt_ref[...] + bias)
    cum = jnp.dot(lmat, dtv * a_neg, precision=lax.Precision.HIGHEST, preferred_element_type=F32)
    cum_t = cum.T
    edge = qn - 1 if d == 0 else 0
    tot = cum[edge:edge + 1, :]
    dt_t = dtv.T
    wt_t = dt_t * jnp.exp(cum_t[:, edge:edge + 1] - cum_t)
    decay = jnp.exp(tot)
    lane = lax.broadcasted_iota(jnp.int32, (qn, LANE), 1)
    left = lane < hd
    zero = jnp.zeros((), BF16)
    for g in range(ng):
        bg = xbc_ref[:, w + g * SSD_STATE:w + (g + 1) * SSD_STATE]
        cg = xbc_ref[:, w + gn + g * SSD_STATE:w + gn + (g + 1) * SSD_STATE]
        bg_t = bg.astype(F32).T
        s_in = h_ref[d, :, g * gw:(g + 1) * gw]
        if need_y:
            cb = lax.dot_general(cg, bg, (((1,), (1,)), ((), ())), preferred_element_type=F32)
            yoff = jnp.dot(cg, s_in.astype(BF16), preferred_element_type=F32)
        for pr in range(gw // LANE):
            c0 = d * nh + (g * gw) // hd + 2 * pr
            col = g * gw + pr * LANE
            xp = xbc_ref[:, col:col + LANE]
            r = jnp.concatenate([jnp.where(left, xp, zero), jnp.where(left, zero, xp)], axis=0)
            tops, bots, cols = [], [], []
            for c in (c0, c0 + 1):
                bots.append((bg_t * wt_t[c:c + 1, :]).astype(BF16))
                if need_y:
                    ccol = jnp.broadcast_to(cum[:, c:c + 1], (qn, qn))
                    seg = jnp.where(mask, jnp.exp(ccol - cum_t[c:c + 1, :]), 0.0)
                    tops.append((cb * seg * dt_t[c:c + 1, :]).astype(BF16))
                    cols.append(ccol)
            dec = jnp.where(left[0:1], decay[:, c0:c0 + 1], decay[:, c0 + 1:c0 + 2])
            s_old = s_in[:, pr * LANE:(pr + 1) * LANE]
            if need_y:
                lhs = jnp.concatenate([jnp.concatenate(tops, axis=1), jnp.concatenate(bots, axis=1)], axis=0)
                out = jnp.dot(lhs, r, preferred_element_type=F32)
                ec = jnp.exp(jnp.where(left, cols[0], cols[1]))
                y_ref[:, col:col + LANE] = (out[:qn] + yoff[:, pr * LANE:(pr + 1) * LANE] * ec).astype(y_ref.dtype)
                s_new = out[qn:]
            else:
                s_new = jnp.dot(jnp.concatenate(bots, axis=1), r, preferred_element_type=F32)
            h_ref[d, :, col:col + LANE] = s_old * dec + s_new


def _ssd_kernel(*refs, nh, ng, need_y):
    if need_y:
        xf_ref, xb_ref, dtf_ref, dtb_ref, h0_ref, bias_ref, alog_ref, yf_ref, yb_ref, h_ref = refs
    else:
        xf_ref, xb_ref, dtf_ref, dtb_ref, h0_ref, bias_ref, alog_ref, h_ref = refs
        yf_ref = yb_ref = None

    @pl.when(pl.program_id(1) == 0)
    def _():
        h_ref[...] = h0_ref[...]

    bias = bias_ref[...]
    a_neg = -jnp.exp(alog_ref[...])
    _ssd_dir(xf_ref, dtf_ref, yf_ref, h_ref, bias, a_neg, 0, nh=nh, ng=ng, need_y=need_y)
    _ssd_dir(xb_ref, dtb_ref, yb_ref, h_ref, bias, a_neg, 1, nh=nh, ng=ng, need_y=need_y)


def _ssd_call(xbc, dt, h0, bias, alog, nh, ng, need_y):
    b, l, cd = xbc.shape
    w = nh * SSD_HEAD_DIM
    nc = l // SSD_CHUNK
    q = SSD_CHUNK
    fwd = lambda i, s: (i, s, 0)
    bwd = lambda i, s: (i, nc - 1 - s, 0)
    hspec = pl.BlockSpec((None, 2, SSD_STATE, w), lambda i, s: (i, 0, 0, 0))
    in_specs = [pl.BlockSpec((None, q, cd), fwd), pl.BlockSpec((None, q, cd), bwd),
                pl.BlockSpec((None, q, LANE), fwd), pl.BlockSpec((None, q, LANE), bwd),
                hspec,
                pl.BlockSpec((1, LANE), lambda i, s: (0, 0)), pl.BlockSpec((1, LANE), lambda i, s: (0, 0))]
    out_specs = [hspec]
    out_shape = [jax.ShapeDtypeStruct((b, 2, SSD_STATE, w), F32)]
    if need_y:
        out_specs = [pl.BlockSpec((None, q, w), fwd), pl.BlockSpec((None, q, w), bwd)] + out_specs
        out_shape = [jax.ShapeDtypeStruct((b, l, w), BF16)] * 2 + out_shape
    return pl.pallas_call(
        functools.partial(_ssd_kernel, nh=nh, ng=ng, need_y=need_y), grid=(b, nc),
        in_specs=in_specs, out_specs=out_specs, out_shape=out_shape,
        compiler_params=_cp(("parallel", "arbitrary"), 48),
        name="ssd_scan" if need_y else "ssd_scan_ctx")(xbc, xbc, dt, dt, h0, bias, alog)


def _gnorm_kernel(yf_ref, yb_ref, xs_ref, z_ref, d_ref, g_ref, o_ref, t_ref, *, rows, tb, gw):
    y = d_ref[...] * xs_ref[...].astype(F32) + yf_ref[...].astype(F32) + yb_ref[...].astype(F32)
    y = y * _silu(z_ref[...].astype(F32))
    bw = y.shape[1]
    parts = []
    for g in range(bw // gw):
        yg = y[:, g * gw:(g + 1) * gw]
        parts.append(yg * lax.rsqrt(jnp.mean(yg * yg, axis=-1, keepdims=True) + EPS))
    res = jnp.concatenate(parts, axis=1) * g_ref[...]
    pitch = rows + XPOSE_PAD
    for wl in range(tb):
        for kk in range(bw // LANE):
            t_ref[kk, wl * pitch:wl * pitch + rows, :] = res[wl * rows:(wl + 1) * rows, kk * LANE:(kk + 1) * LANE]
    for r in range(rows):
        o_ref[r] = jnp.concatenate(
            [t_ref[kk, pl.ds(r, tb, stride=pitch), :] for kk in range(bw // LANE)], axis=1).astype(o_ref.dtype)


def _gnorm_call(yf, yb, xbc, proj, dvec, gvec, ng):
    b, l, w = yf.shape
    rows = l // GRID_W
    tb = XPOSE_TILE
    gw = w // ng
    bw = gw * (2 if ng % 2 == 0 else 1)
    blk = pl.BlockSpec((None, tb * rows, bw), lambda i, s, g: (i, s, g))
    vec = pl.BlockSpec((1, bw), lambda i, s, g: (0, g))
    out = pl.pallas_call(
        functools.partial(_gnorm_kernel, rows=rows, tb=tb, gw=gw), grid=(b, GRID_W // tb, w // bw),
        in_specs=[blk, blk, blk, blk, vec, vec],
        out_specs=pl.BlockSpec((None, rows, tb, bw), lambda i, s, g: (i, 0, s, g)),
        out_shape=jax.ShapeDtypeStruct((b, rows, GRID_W, w), BF16),
        scratch_shapes=[pltpu.VMEM((bw // LANE, tb * (rows + XPOSE_PAD), LANE), F32)],
        compiler_params=_cp(("parallel", "parallel", "parallel"), 40),
        name="ssd_gnorm")(yf, yb, xbc, proj, dvec, gvec)
    return out.reshape(b * l, w)


def _moe_kernel(e_ref, b0_ref, nb_ref, *refs, nsub, nh):
    xs = refs[:nsub]
    wg_ref, wu_ref, wd_ref, o_ref, act_ref, wdb_ref = refs[nsub:]
    s = pl.program_id(0)
    t = pl.program_id(1)
    nb = nb_ref[s]

    @pl.when(jnp.logical_and(t < nh, nb > 0))
    def _():
        wg = wg_ref[...].astype(BF16)
        wu = wu_ref[...].astype(BF16)

        def act_block(k):
            x = xs[k][...]
            gt = jnp.dot(x, wg, preferred_element_type=F32)
            up = jnp.dot(x, wu, preferred_element_type=F32)
            act_ref[k, t] = (_silu(gt) * up).astype(BF16)

        act_block(0)
        for k in range(1, nsub):
            pl.when(k < nb)(functools.partial(act_block, k))

    @pl.when(jnp.logical_and(t >= nh, nb > 0))
    def _():
        wdb = wd_ref[...].astype(BF16)
        wdb_ref[...] = wdb

        def down_block(k, w):
            a = jnp.concatenate([act_ref[k, h] for h in range(nh)], axis=1)
            o_ref[k * MOE_BLOCK:(k + 1) * MOE_BLOCK, :] = jnp.dot(a, w, preferred_element_type=F32).astype(o_ref.dtype)

        down_block(0, wdb)
        for k in range(1, nsub):
            pl.when(k < nb)(lambda k=k: down_block(k, wdb_ref[...]))

            @pl.when(k >= nb)
            def _():
                o_ref[k * MOE_BLOCK:(k + 1) * MOE_BLOCK, :] = jnp.zeros((MOE_BLOCK, o_ref.shape[1]), o_ref.dtype)

    @pl.when(jnp.logical_and(t >= nh, nb == 0))
    def _():
        o_ref[...] = jnp.zeros(o_ref.shape, o_ref.dtype)


def _moe_call(sup_e, sup_b0, sup_nb, xs, w_gate, w_up, w_down):
    n_slots, d = xs.shape
    hid = w_gate.shape[-1]
    th = min(256, hid)
    tn = min(1024, d)
    nh = hid // th
    nt = d // tn
    ns = sup_e.shape[0]
    rsup = MOE_SUB * MOE_BLOCK

    def x_map(k):
        def index(s, t, e, b0, nb):
            sx = jnp.where(t < nh, s, jnp.minimum(s + 1, ns - 1))
            return (b0[sx * MOE_SUB + k], 0)
        return index

    def hid_idx(s, t, nb):
        return jnp.where(nb[s] > 0, jnp.minimum(t, nh - 1), nh - 1)

    def col_idx(s, t, nb):
        return jnp.where(nb[s] > 0, jnp.clip(t - nh, 0, nt - 1), nt - 1)

    gs = pltpu.PrefetchScalarGridSpec(
        num_scalar_prefetch=3, grid=(ns, nh + nt),
        in_specs=[pl.BlockSpec((MOE_BLOCK, d), x_map(k)) for k in range(MOE_SUB)] + [
            pl.BlockSpec((None, d, th), lambda s, t, e, b0, nb: (e[s], 0, hid_idx(s, t, nb))),
            pl.BlockSpec((None, d, th), lambda s, t, e, b0, nb: (e[s], 0, hid_idx(s, t, nb))),
            pl.BlockSpec((None, hid, tn), lambda s, t, e, b0, nb: (e[s], 0, col_idx(s, t, nb)))],
        out_specs=pl.BlockSpec((rsup, tn), lambda s, t, e, b0, nb: (s, jnp.clip(t - nh, 0, nt - 1))),
        scratch_shapes=[pltpu.VMEM((MOE_SUB, nh, MOE_BLOCK, th), BF16), pltpu.VMEM((hid, tn), BF16)])
    return pl.pallas_call(
        functools.partial(_moe_kernel, nsub=MOE_SUB, nh=nh), grid_spec=gs,
        out_shape=jax.ShapeDtypeStruct((ns * rsup, d), BF16),
        compiler_params=_cp(("arbitrary", "arbitrary"), 60), name="moe_experts")(
            sup_e, sup_b0, sup_nb, *([xs] * MOE_SUB), w_gate, w_up, w_down)


def _final_kernel(x_ref, ya_ref, yb_ref, w_ref, g2_ref, fg_ref, o_ref):
    wts = w_ref[...]
    moe = ya_ref[...].astype(F32) * wts[:, 0:1] + yb_ref[...].astype(F32) * wts[:, 1:2]
    x = x_ref[...] + g2_ref[...] * moe
    ms = jnp.mean(x * x, axis=-1, keepdims=True)
    o_ref[...] = x * lax.rsqrt(ms + EPS) * fg_ref[...]


def _final_call(x, y2, wts, mods, ig, fg):
    b, l, d = x.shape
    tr = min(256, l)
    nt = l // tr
    row = lambda i, r: (i * nt + r, 0)
    row2 = lambda i, r: (b * nt + i * nt + r, 0)
    ya = yb = y2
    return pl.pallas_call(
        _final_kernel, grid=(b, nt),
        in_specs=[pl.BlockSpec((None, tr, d), lambda i, r: (i, r, 0)),
                  pl.BlockSpec((tr, d), row), pl.BlockSpec((tr, d), row2),
                  pl.BlockSpec((tr, MOE_TOP_K), row),
                  pl.BlockSpec((None, None, 1, d), lambda i, r: (i, ig, 0, 0)),
                  pl.BlockSpec((1, d), lambda i, r: (0, 0))],
        out_specs=pl.BlockSpec((None, tr, d), lambda i, r: (i, r, 0)),
        out_shape=jax.ShapeDtypeStruct((b, l, d), F32),
        compiler_params=_cp(("parallel", "parallel"), 40), name="combine_final")(x, ya, yb, wts, mods, fg)


def _route(sel, ne):
    m = sel.shape[0]
    weights = sel[:, MOE_TOP_K:2 * MOE_TOP_K]
    n_assign = m * MOE_TOP_K
    flat_e = jnp.concatenate([sel[:, k] for k in range(MOE_TOP_K)], axis=0).astype(jnp.int32)
    onehot = (flat_e[:, None] == jnp.arange(ne, dtype=flat_e.dtype)[None, :]).astype(jnp.int32)
    csum = jnp.cumsum(onehot, axis=0)
    counts = csum[-1]
    local = jnp.sum(csum * onehot, axis=1) - 1
    nblk_e = (counts + MOE_BLOCK - 1) // MOE_BLOCK
    padded = nblk_e * MOE_BLOCK
    pad_end = jnp.cumsum(padded)
    pad_start = pad_end - padded
    dest = pad_start[flat_e] + local
    n_blocks = -(-(n_assign + ne * (MOE_BLOCK - 1)) // MOE_BLOCK)
    tok = jnp.arange(n_assign, dtype=jnp.int32) % m
    n_slots = n_blocks * MOE_BLOCK
    slot_tok = (jnp.arange(n_slots, dtype=jnp.int32) % m).at[dest].set(tok)
    rsup = MOE_SUB * MOE_BLOCK
    ns_e = (nblk_e + MOE_SUB - 1) // MOE_SUB
    sup_end = jnp.cumsum(ns_e)
    sup_start = sup_end - ns_e
    n_sup = (n_blocks + (MOE_SUB - 1) * ne) // MOE_SUB
    sidx = jnp.arange(n_sup, dtype=jnp.int32)
    last = sup_end[-1] - 1
    s_eff = jnp.minimum(sidx, last)
    e_s = jnp.minimum(jnp.searchsorted(sup_end, s_eff, side='right'), ne - 1).astype(jnp.int32)
    k_s = s_eff - sup_start[e_s]
    b0_s = pad_start[e_s] // MOE_BLOCK + MOE_SUB * k_s
    nb_s = jnp.clip(nblk_e[e_s] - MOE_SUB * k_s, 0, MOE_SUB)
    used = sidx <= last
    sup_nb = jnp.where(used, nb_s, 0).astype(jnp.int32)
    kk = jnp.arange(MOE_SUB, dtype=jnp.int32)[None, :]
    sup_b0 = jnp.maximum(lax.cummax(jnp.where(kk < sup_nb[:, None], b0_s[:, None] + kk, -1), axis=0), 0)
    sup_b0 = sup_b0.reshape(-1).astype(jnp.int32)
    pos = ((sup_start[flat_e] + local // rsup) * rsup + local % rsup).astype(jnp.int32)
    return weights, slot_tok, e_s, sup_b0, sup_nb, pos


def kernel(x, c, ctx, c_ctx, w_mod, b_mod, norm1_g, w_in, s5_lam_re, s5_lam_im, s5_log_dt, s5_b_re, s5_b_im, s5_c_re, s5_c_im, s5_d, s5_w_val, s5_w_gate, ssd_conv_w, ssd_conv_b, ssd_a_log, ssd_dt_bias, ssd_d, ssd_norm_g, ssd_w_out, w_o, norm2_g, moe_w_group, moe_b_group, moe_w_expert, moe_b_expert, moe_w_gate, moe_w_up, moe_w_down, final_g):
    depth = w_mod.shape[0]
    assert depth == 1, "single-layer block"
    bsz, n_lat, d = x.shape
    l_ctx = ctx.shape[1]
    w5 = s5_d.shape[1]
    nh = ssd_d.shape[1]
    w = nh * SSD_HEAD_DIM
    conv_dim = ssd_conv_w.shape[2]
    ng = (conv_dim - w) // (2 * SSD_STATE)
    ssd_in = w + conv_dim + 2 * nh
    o1, o2 = w5, w5 + ssd_in
    l = 0

    cc = jnp.concatenate([c, c_ctx[None, :]], axis=0)
    cc = jnp.pad(cc, ((0, (-cc.shape[0]) % 8), (0, 0)))
    mods = _mod_call(cc, w_mod[l], b_mod[l]).reshape(cc.shape[0], 6, 1, d)
    i_sh1, i_sc1, i_g1, i_sh2, i_sc2, i_g2 = range(6)

    w_in_l = w_in[l]
    o_dt = o1 + w + conv_dim

    hn_rm, hn_cm = _norm_lat_call(x, norm1_g[l], mods, i_sc1, i_sh1)
    hc = _norm_ctx_call(ctx, norm1_g[l], mods, bsz, i_sc1, i_sh1)
    hn_rm = hn_rm.reshape(bsz * n_lat, d)
    hn_cm = hn_cm.reshape(bsz * n_lat, d)
    hc = hc.reshape(bsz * l_ctx, d)

    u_lat = _mm_call(hn_rm, w_in_l, BF16, "in_s5", 0, o1).reshape(bsz, n_lat, w5)
    u_ctx = _mm_call(hc, w_in_l, BF16, "in_s5_ctx", 0, o1).reshape(bsz, l_ctx, w5)
    gates = _mm_call(hn_rm, w_in_l, BF16, "in_gates", o2, 2 * d)
    z_lat = _mm_call(hn_cm, w_in_l, BF16, "in_z", o1, w).reshape(bsz, n_lat, w)
    xbc_lat = _proj_conv_call(hn_cm.reshape(bsz, n_lat, d), w_in_l, o1 + w, ssd_conv_w[l], ssd_conv_b[l])
    p_ctx = _mm_call(hc, w_in_l, BF16, "in_xbc_ctx", o1 + w, conv_dim).reshape(bsz, l_ctx, conv_dim)
    dt_lat = _mm_call(hn_cm, w_in_l, F32, "in_dt", o_dt, 2 * nh).reshape(bsz, n_lat, LANE)
    dt_ctx = _mm_call(hc, w_in_l, F32, "in_dt_ctx", o_dt, 2 * nh).reshape(bsz, l_ctx, LANE)

    nj = w5 // LANE
    bmat, cmat, dmat, lre, lim = _s5_params(s5_lam_re[l], s5_lam_im[l], s5_log_dt[l], s5_b_re[l], s5_b_im[l],
                                            s5_c_re[l], s5_c_im[l], bsz)
    s5_zero = jnp.zeros((nj, 2, 2 * bsz, lre.shape[-1]), F32)
    (s5_ctx,) = _s5_call(u_ctx, bmat, cmat, dmat, lre, lim, s5_zero, False)
    ya_f, ya_b, _ = _s5_call(u_lat, bmat, cmat, dmat, lre, lim, s5_ctx, True)

    xbc_ctx = _conv_call(p_ctx, ssd_conv_w[l], ssd_conv_b[l], 0)
    pad_h = LANE - 2 * nh
    bias = jnp.pad(ssd_dt_bias[l].astype(F32).reshape(1, 2 * nh), ((0, 0), (0, pad_h)))
    alog = jnp.pad(ssd_a_log[l].astype(F32).reshape(1, 2 * nh), ((0, 0), (0, pad_h)))
    h_zero = jnp.zeros((bsz, 2, SSD_STATE, w), F32)
    (h_ctx,) = _ssd_call(xbc_ctx, dt_ctx, h_zero, bias, alog, nh, ng, False)
    y_f, y_b, _ = _ssd_call(xbc_lat, dt_lat, h_ctx, bias, alog, nh, ng, True)
    d_vec = jnp.repeat(ssd_d[l].astype(F32), SSD_HEAD_DIM).reshape(1, w)
    y_ssd = _gnorm_call(y_f, y_b, xbc_lat, z_lat, d_vec, ssd_norm_g[l].astype(F32).reshape(1, w), ng)

    m_lat = bsz * n_lat
    part_a = _glu_call(u_lat.reshape(m_lat, w5), ya_f.reshape(m_lat, w5), ya_b.reshape(m_lat, w5),
                       s5_d[l].astype(F32).reshape(1, w5),
                       s5_w_val[l].astype(BF16), s5_w_gate[l].astype(BF16), gates)
    merged = _merge_call(y_ssd, ssd_w_out[l], gates, part_a)
    x1 = _resid_call(merged, w_o[l], x, mods, i_g1)

    ngr = moe_w_group.shape[-1]
    ne = moe_w_expert.shape[-1]
    wr = jnp.concatenate([moe_w_group[l], moe_w_expert[l]], axis=1).astype(F32)
    wr = jnp.pad(wr, ((0, 0), (0, (-(ngr + ne)) % LANE)))
    br = jnp.concatenate([moe_b_group[l], moe_b_expert[l]]).astype(F32)
    br = jnp.pad(br, (0, (-(ngr + ne)) % LANE)).reshape(1, -1)
    hx, sel = _norm_router_call(x1, norm2_g[l], mods, i_sc2, i_sh2, wr, br, ngr, ne)
    m = bsz * n_lat
    hx = hx.reshape(m, d)
    weights, slot_tok, sup_e, sup_b0, sup_nb, pos = _route(sel.reshape(m, -1), ne)
    xs = hx[slot_tok]
    y_slots = _moe_call(sup_e, sup_b0, sup_nb, xs, moe_w_gate[l], moe_w_up[l], moe_w_down[l])
    y2 = y_slots[pos]
    return _final_call(x1, y2, weights.astype(F32), mods, i_g2, final_g.reshape(1, d))
```

```python
import functools
import math

import jax
import jax.numpy as jnp
from jax import lax
from jax.experimental import pallas as pl
from jax.experimental.pallas import tpu as pltpu

F32 = jnp.float32
BF16 = jnp.bfloat16

GRID_W = 64
EPS = 1e-6
LANE = 128
SSD_HEAD_DIM = 64
SSD_STATE = 128
SSD_CHUNK = 128
SSD_CONV = 5
MOE_TOP_K = 2
MOE_BLOCK = 256
MOE_SUB = 4
XPOSE_TILE = 16
XPOSE_PAD = 8
S5_CHUNK = 512
S5_ROW_PAD = 8


def _cp(sem, mb):
    return pltpu.CompilerParams(dimension_semantics=sem, vmem_limit_bytes=mb * 1024 * 1024)


def _sigmoid(x):
    return 1.0 / (1.0 + jnp.exp(-x))


def _silu(x):
    return x * _sigmoid(x)


def _mod_kernel(c_ref, w_ref, b_ref, o_ref):
    s = _silu(c_ref[...])
    o_ref[...] = jnp.dot(s.astype(BF16), w_ref[...].astype(BF16),
                         preferred_element_type=F32) + b_ref[...]


def _mod_call(cc, w, b):
    r, d = cc.shape
    n = w.shape[1]
    tn = min(512, n)
    return pl.pallas_call(
        _mod_kernel, grid=(n // tn,),
        in_specs=[pl.BlockSpec((r, d), lambda j: (0, 0)),
                  pl.BlockSpec((d, tn), lambda j: (0, j)),
                  pl.BlockSpec((1, tn), lambda j: (0, j))],
        out_specs=pl.BlockSpec((r, tn), lambda j: (0, j)),
        out_shape=jax.ShapeDtypeStruct((r, n), F32),
        compiler_params=_cp(("parallel",), 40), name="mod")(cc, w, b.reshape(1, n))


def _rmsmod(x, g, sc, sh):
    ms = jnp.mean(x * x, axis=-1, keepdims=True)
    return (x * lax.rsqrt(ms + EPS) * g) * (1.0 + sc) + sh


def _norm_lat_kernel(x_ref, g_ref, sc_ref, sh_ref, orm_ref, ocm_ref, t_ref, *, tb, d):
    pitch = tb + XPOSE_PAD
    for r in range(tb):
        y = _rmsmod(x_ref[r], g_ref[...], sc_ref[...], sh_ref[...])
        orm_ref[r] = y.astype(BF16)
        for kk in range(d // LANE):
            t_ref[kk, r * pitch:r * pitch + tb, :] = y[:, kk * LANE:(kk + 1) * LANE]
    for wl in range(tb):
        ocm_ref[wl] = jnp.concatenate(
            [t_ref[kk, pl.ds(wl, tb, stride=pitch), :] for kk in range(d // LANE)], axis=1).astype(BF16)


def _norm_lat_call(x, g, mods, isc, ish):
    b, l, d = x.shape
    rows = l // GRID_W
    tb = XPOSE_TILE
    xv = x.reshape(b, rows, GRID_W, d)
    orm, ocm = pl.pallas_call(
        functools.partial(_norm_lat_kernel, tb=tb, d=d), grid=(b, rows // tb, GRID_W // tb),
        in_specs=[pl.BlockSpec((None, tb, tb, d), lambda i, r, c: (i, r, c, 0)),
                  pl.BlockSpec((1, d), lambda i, r, c: (0, 0)),
                  pl.BlockSpec((None, None, 1, d), lambda i, r, c: (i, isc, 0, 0)),
                  pl.BlockSpec((None, None, 1, d), lambda i, r, c: (i, ish, 0, 0))],
        out_specs=[pl.BlockSpec((None, tb, tb, d), lambda i, r, c: (i, r, c, 0)),
                   pl.BlockSpec((None, tb, tb, d), lambda i, r, c: (i, c, r, 0))],
        out_shape=[jax.ShapeDtypeStruct((b, rows, GRID_W, d), BF16),
                   jax.ShapeDtypeStruct((b, GRID_W, rows, d), BF16)],
        scratch_shapes=[pltpu.VMEM((d // LANE, tb * (tb + XPOSE_PAD), LANE), F32)],
        compiler_params=_cp(("parallel", "parallel", "parallel"), 48),
        name="norm1_lat")(xv, g.reshape(1, d), mods, mods)
    return orm.reshape(b, l, d), ocm.reshape(b, l, d)


def _norm_kernel(x_ref, g_ref, sc_ref, sh_ref, o_ref):
    o_ref[...] = _rmsmod(x_ref[...], g_ref[...], sc_ref[...], sh_ref[...]).astype(o_ref.dtype)


def _norm_ctx_call(x, g, mods, row, isc, ish):
    b, l, d = x.shape
    tr = min(256, l)
    return pl.pallas_call(
        _norm_kernel, grid=(b, l // tr),
        in_specs=[pl.BlockSpec((None, tr, d), lambda i, r: (i, r, 0)),
                  pl.BlockSpec((1, d), lambda i, r: (0, 0)),
                  pl.BlockSpec((None, None, 1, d), lambda i, r: (row, isc, 0, 0)),
                  pl.BlockSpec((None, None, 1, d), lambda i, r: (row, ish, 0, 0))],
        out_specs=pl.BlockSpec((None, tr, d), lambda i, r: (i, r, 0)),
        out_shape=jax.ShapeDtypeStruct((b, l, d), BF16),
        compiler_params=_cp(("parallel", "parallel"), 40), name="norm1_ctx")(x, g.reshape(1, d), mods, mods)


def _first_argmax(v, vmax, lane):
    return jnp.min(jnp.where(v == vmax, lane, float(LANE)), axis=-1, keepdims=True)


def _norm_router_kernel(x_ref, g_ref, sc_ref, sh_ref, wr_ref, br_ref, o_ref, sel_ref, *, ngr, ne):
    y = _rmsmod(x_ref[...], g_ref[...], sc_ref[...], sh_ref[...])
    o_ref[...] = y.astype(BF16)
    y_hi = y.astype(BF16)
    y_lo = (y - y_hi.astype(F32)).astype(BF16)
    w_hi = wr_ref[0]
    lg = (jnp.dot(y_hi, w_hi, preferred_element_type=F32) + jnp.dot(y_hi, wr_ref[1], preferred_element_type=F32)
          + jnp.dot(y_lo, w_hi, preferred_element_type=F32)) + br_ref[...]
    epg = ne // ngr
    lane = lax.broadcasted_iota(jnp.int32, lg.shape, 1).astype(F32)
    ninf = -jnp.inf
    gl = jnp.where(lane < ngr, lg, ninf)
    ge = jnp.exp(gl - jnp.max(gl, axis=-1, keepdims=True))
    g_prob = ge / jnp.sum(ge, axis=-1, keepdims=True)
    g_p = jnp.max(g_prob, axis=-1, keepdims=True)
    lo = ngr + epg * _first_argmax(g_prob, g_p, lane)
    cand = jnp.where(jnp.logical_and(lane >= lo, lane < lo + epg), lg, ninf)
    v1 = jnp.max(cand, axis=-1, keepdims=True)
    l1 = _first_argmax(cand, v1, lane)
    rest = jnp.where(lane == l1, ninf, cand)
    v2 = jnp.max(rest, axis=-1, keepdims=True)
    l2 = _first_argmax(rest, v2, lane)
    ex = jnp.exp(v2 - v1)
    w1 = g_p * (1.0 / (1.0 + ex))
    w2 = g_p * (ex / (1.0 + ex))
    sel_ref[...] = jnp.where(lane == 0, l1 - ngr, jnp.where(lane == 1, l2 - ngr,
                             jnp.where(lane == 2, w1, jnp.where(lane == 3, w2, 0.0))))


def _norm_router_call(x, g, mods, isc, ish, wr, br, ngr, ne):
    b, l, d = x.shape
    tr = min(256, l)
    nr = wr.shape[1]
    w_hi = wr.astype(BF16)
    wr = jnp.stack([w_hi, (wr - w_hi.astype(F32)).astype(BF16)], axis=0)
    return pl.pallas_call(
        functools.partial(_norm_router_kernel, ngr=ngr, ne=ne), grid=(b, l // tr),
        in_specs=[pl.BlockSpec((None, tr, d), lambda i, r: (i, r, 0)),
                  pl.BlockSpec((1, d), lambda i, r: (0, 0)),
                  pl.BlockSpec((None, None, 1, d), lambda i, r: (i, isc, 0, 0)),
                  pl.BlockSpec((None, None, 1, d), lambda i, r: (i, ish, 0, 0)),
                  pl.BlockSpec((2, d, nr), lambda i, r: (0, 0, 0)),
                  pl.BlockSpec((1, nr), lambda i, r: (0, 0))],
        out_specs=[pl.BlockSpec((None, tr, d), lambda i, r: (i, r, 0)),
                   pl.BlockSpec((None, tr, nr), lambda i, r: (i, r, 0))],
        out_shape=[jax.ShapeDtypeStruct((b, l, d), BF16), jax.ShapeDtypeStruct((b, l, nr), F32)],
        compiler_params=_cp(("parallel", "parallel"), 40),
        name="norm2_router")(x, g.reshape(1, d), mods, mods, wr, br)


def _mm_kernel(a_ref, b_ref, o_ref):
    o_ref[...] = jnp.dot(a_ref[...], b_ref[...].astype(BF16), preferred_element_type=F32).astype(o_ref.dtype)


def _mm_tiles(m, n):
    tm = min(1024, m)
    tn = min(512, n)
    return tm, tn


def _mm_call(a, b, out_dtype, name, col0=0, n=None):
    m, k = a.shape
    n = b.shape[1] - col0 if n is None else n
    tm, tn = _mm_tiles(m, n)
    if col0 % LANE or n % tn or n % LANE:
        b = b[:, col0:col0 + n]
        pad = (-n) % LANE
        b = jnp.pad(b, ((0, 0), (0, pad)))
        n, col0 = n + pad, 0
        tm, tn = _mm_tiles(m, n)
    return pl.pallas_call(
        _mm_kernel, grid=(m // tm, n // tn),
        in_specs=[pl.BlockSpec((tm, k), lambda i, j: (i, 0)),
                  pl.BlockSpec((pl.Element(k), pl.Element(tn)), lambda i, j: (0, (col0 // LANE + j * (tn // LANE)) * LANE))],
        out_specs=pl.BlockSpec((tm, tn), lambda i, j: (i, j)),
        out_shape=jax.ShapeDtypeStruct((m, n), out_dtype),
        compiler_params=_cp(("parallel", "parallel"), 48), name=name)(a, b)


def _gelu_tanh(x):
    return x * (0.5 * (1.0 + jnp.tanh(math.sqrt(2.0 / math.pi) * (x + 0.044715 * (x * x * x)))))


def _glu_kernel(u_ref, yf_ref, yb_ref, d_ref, wv_ref, wg_ref, gate_ref, o_ref, a_ref):
    @pl.when(pl.program_id(1) == 0)
    def _():
        y = d_ref[...] * u_ref[...].astype(F32) + yf_ref[...].astype(F32) + yb_ref[...].astype(F32)
        a_ref[...] = _gelu_tanh(y).astype(BF16)

    a = a_ref[...]
    val = jnp.dot(a, wv_ref[...], preferred_element_type=F32)
    gl = jnp.dot(a, wg_ref[...], preferred_element_type=F32)
    o_ref[...] = (_sigmoid(gate_ref[...].astype(F32)) * (val * _sigmoid(gl))).astype(o_ref.dtype)


def _glu_call(u, yf, yb, dvec, wv, wg, gates):
    m, k = u.shape
    n = wv.shape[1]
    tm, tn = _mm_tiles(m, n)
    row = pl.BlockSpec((tm, k), lambda i, j: (i, 0))
    return pl.pallas_call(
        _glu_kernel, grid=(m // tm, n // tn),
        in_specs=[row, row, row,
                  pl.BlockSpec((1, k), lambda i, j: (0, 0)),
                  pl.BlockSpec((k, tn), lambda i, j: (0, j)),
                  pl.BlockSpec((k, tn), lambda i, j: (0, j)),
                  pl.BlockSpec((tm, tn), lambda i, j: (i, j))],
        out_specs=pl.BlockSpec((tm, tn), lambda i, j: (i, j)),
        out_shape=jax.ShapeDtypeStruct((m, n), BF16),
        scratch_shapes=[pltpu.VMEM((tm, k), BF16)],
        compiler_params=_cp(("parallel", "arbitrary"), 56), name="s5_glu")(u, yf, yb, dvec, wv, wg, gates)


def _merge_kernel(a_ref, w_ref, gate_ref, pa_ref, o_ref):
    br = jnp.dot(a_ref[...], w_ref[...].astype(BF16), preferred_element_type=F32)
    o_ref[...] = (pa_ref[...].astype(F32) + _sigmoid(gate_ref[...].astype(F32)) * br).astype(o_ref.dtype)


def _merge_call(a, w, gates, part_a):
    m, k = a.shape
    n = w.shape[1]
    tm, tn = _mm_tiles(m, n)
    off = n // tn
    return pl.pallas_call(
        _merge_kernel, grid=(m // tm, n // tn),
        in_specs=[pl.BlockSpec((tm, k), lambda i, j: (i, 0)),
                  pl.BlockSpec((k, tn), lambda i, j: (0, j)),
                  pl.BlockSpec((tm, tn), lambda i, j: (i, j + off)),
                  pl.BlockSpec((tm, tn), lambda i, j: (i, j))],
        out_specs=pl.BlockSpec((tm, tn), lambda i, j: (i, j)),
        out_shape=jax.ShapeDtypeStruct((m, n), BF16),
        compiler_params=_cp(("parallel", "parallel"), 48), name="ssd_out_merge")(a, w, gates, part_a)


def _resid_kernel(a_ref, w_ref, x_ref, g_ref, o_ref):
    mix = jnp.dot(a_ref[...], w_ref[...].astype(BF16), preferred_element_type=F32)
    o_ref[...] = x_ref[...] + g_ref[...] * mix


def _resid_call(a, w, x, mods, ig):
    b, l, d = x.shape
    k = a.shape[1]
    tm, tn = _mm_tiles(l, d)
    nt = l // tm
    return pl.pallas_call(
        _resid_kernel, grid=(b * nt, d // tn),
        in_specs=[pl.BlockSpec((tm, k), lambda i, j: (i, 0)),
                  pl.BlockSpec((k, tn), lambda i, j: (0, j)),
                  pl.BlockSpec((None, tm, tn), lambda i, j: (i // nt, i % nt, j)),
                  pl.BlockSpec((None, None, 1, tn), lambda i, j: (i // nt, ig, 0, j))],
        out_specs=pl.BlockSpec((None, tm, tn), lambda i, j: (i // nt, i % nt, j)),
        out_shape=jax.ShapeDtypeStruct((b, l, d), F32),
        compiler_params=_cp(("parallel", "parallel"), 48), name="w_o_resid")(a, w, x, mods)


def _s5_kernel(*refs, nb, tp, p8, need_y):
    if need_y:
        uf_ref, ub_ref, b_ref, c_ref, d_ref, lre_ref, lim_ref, h0_ref, yf_ref, yb_ref, h_ref, buf_ref, il_ref = refs
    else:
        uf_ref, ub_ref, b_ref, c_ref, d_ref, lre_ref, lim_ref, h0_ref, h_ref, buf_ref, il_ref = refs
    q = 2 * nb
    nk = p8 // LANE
    pitch = tp + S5_ROW_PAD

    @pl.when(pl.program_id(1) == 0)
    def _():
        h_ref[...] = h0_ref[...]

    rev = (lax.broadcasted_iota(jnp.int32, (tp, tp), 0) + lax.broadcasted_iota(jnp.int32, (tp, tp), 1)
           == tp - 1).astype(BF16)
    def pair_rows(u_ref, b, newer_first):
        il_ref[...] = u_ref[b].astype(F32)
        even = il_ref[pl.ds(0, tp, stride=2), :].astype(BF16)
        odd = il_ref[pl.ds(1, tp, stride=2), :].astype(BF16)
        return jnp.concatenate([odd, even] if newer_first else [even, odd], axis=1)

    lhs = []
    for d in range(2):
        if d == 0:
            u = jnp.concatenate([pair_rows(uf_ref, b, False) for b in range(nb)], axis=0)
        else:
            u = jnp.concatenate(
                [jnp.dot(rev, pair_rows(ub_ref, b, True), preferred_element_type=F32).astype(BF16)
                 for b in range(nb)], axis=0)
        lhs.append(u)
        bu = jnp.dot(u, b_ref[d], preferred_element_type=F32)
        for b in range(nb):
            r0 = (d * nb + b) * pitch
            for k in range(2 * nk):
                buf_ref[k, r0:r0 + tp, :] = bu[b * tp:(b + 1) * tp, k * LANE:(k + 1) * LANE]
    ar = [lre_ref[:, k * LANE:(k + 1) * LANE] for k in range(nk)]
    ai = [lim_ref[:, k * LANE:(k + 1) * LANE] for k in range(nk)]

    def step(s, carry):
        rows = pl.ds(s, q, stride=pitch)
        out = []
        for k in range(nk):
            hr, hi = carry[k]
            nr = ar[k] * hr - ai[k] * hi + buf_ref[k, rows, :]
            ni = ar[k] * hi + ai[k] * hr + buf_ref[nk + k, rows, :]
            buf_ref[k, rows, :] = hr
            buf_ref[nk + k, rows, :] = hi
            out.append((nr, ni))
        return tuple(out)

    init = tuple((h_ref[0, :, k * LANE:(k + 1) * LANE], h_ref[1, :, k * LANE:(k + 1) * LANE]) for k in range(nk))
    fin = lax.fori_loop(0, tp, step, init, unroll=8)
    for k in range(nk):
        h_ref[0, :, k * LANE:(k + 1) * LANE] = fin[k][0]
        h_ref[1, :, k * LANE:(k + 1) * LANE] = fin[k][1]

    if need_y:
        for d in range(2):
            h = jnp.concatenate(
                [jnp.concatenate([buf_ref[k, (d * nb + b) * pitch:(d * nb + b) * pitch + tp, :].astype(BF16)
                                  for k in range(2 * nk)], axis=1) for b in range(nb)], axis=0)
            y = (jnp.dot(h, c_ref[d], preferred_element_type=F32)
                 + jnp.dot(lhs[d], d_ref[d], preferred_element_type=F32))
            for b in range(nb):
                yb = y[b * tp:(b + 1) * tp]
                if d == 1:
                    yb = jnp.dot(rev, yb.astype(BF16), preferred_element_type=F32)
                first, second = (0, 1) if d == 0 else (1, 0)
                il_ref[pl.ds(first, tp, stride=2), :] = yb[:, 0:LANE]
                il_ref[pl.ds(second, tp, stride=2), :] = yb[:, LANE:2 * LANE]
                (yf_ref if d == 0 else yb_ref)[b] = il_ref[...].astype(BF16)


def _s5_call(u, bmat, cmat, dmat, lre, lim, h0, need_y):
    nb, l, w5 = u.shape
    nj = w5 // LANE
    q = 2 * nb
    t = min(S5_CHUNK, l)
    tp = t // 2
    p8 = lre.shape[-1]
    nc = l // t
    hspec = pl.BlockSpec((None, 2, q, p8), lambda j, c: (j, 0, 0, 0))
    ublk = (nb, t, LANE)
    in_specs = [pl.BlockSpec(ublk, lambda j, c: (0, c, j)), pl.BlockSpec(ublk, lambda j, c: (0, nc - 1 - c, j)),
                pl.BlockSpec((None, 2, 2 * LANE, 2 * p8), lambda j, c: (j, 0, 0, 0)),
                pl.BlockSpec((None, 2, 2 * p8, 2 * LANE), lambda j, c: (j, 0, 0, 0)),
                pl.BlockSpec((None, 2, 2 * LANE, 2 * LANE), lambda j, c: (j, 0, 0, 0)),
                pl.BlockSpec((None, q, p8), lambda j, c: (j, 0, 0)),
                pl.BlockSpec((None, q, p8), lambda j, c: (j, 0, 0)),
                hspec]
    out_specs = [hspec]
    out_shape = [jax.ShapeDtypeStruct((nj, 2, q, p8), F32)]
    scratch = [pltpu.VMEM((2 * p8 // LANE, q * (tp + S5_ROW_PAD), LANE), F32), pltpu.VMEM((t, LANE), F32)]
    if need_y:
        out_specs = [pl.BlockSpec(ublk, lambda j, c: (0, c, j)),
                     pl.BlockSpec(ublk, lambda j, c: (0, nc - 1 - c, j))] + out_specs
        out_shape = [jax.ShapeDtypeStruct((nb, l, w5), BF16)] * 2 + out_shape
    kern = functools.partial(_s5_kernel, nb=nb, tp=tp, p8=p8, need_y=need_y)
    return pl.pallas_call(
        kern, grid=(nj, nc), in_specs=in_specs, out_specs=out_specs, out_shape=out_shape,
        scratch_shapes=scratch, compiler_params=_cp(("parallel", "arbitrary"), 48),
        name="s5_scan" if need_y else "s5_scan_ctx")(u, u, bmat, cmat, dmat, lre, lim, h0)


def _s5_params(lam_re, lam_im, log_dt, b_re, b_im, c_re, c_im, nb):
    _, g, p = lam_re.shape
    s = b_re.shape[-1]
    gpb = LANE // s
    nj = g // gpb
    lam = lax.complex(lam_re.astype(F32), lam_im.astype(F32))
    lam_bar = jnp.exp(lam * jnp.exp(log_dt.astype(F32))[..., None])
    b_bar = ((lam_bar - 1.0) / lam)[..., None] * lax.complex(b_re.astype(F32), b_im.astype(F32))
    def quadrants(parts):
        r, c = parts[0].shape[-2:]
        same_group = (jnp.arange(gpb * r)[:, None] // r) == (jnp.arange(gpb * c)[None, :] // c)

        def spread(x):
            return jnp.where(same_group, jnp.tile(x.reshape(2, nj, gpb * r, c), (1, 1, 1, gpb)), 0.0)

        top = jnp.concatenate([spread(parts[0]), spread(parts[1])], axis=-1)
        bot = jnp.concatenate([spread(parts[2]), spread(parts[3])], axis=-1)
        return jnp.concatenate([top, bot], axis=-2).transpose(1, 0, 2, 3).astype(BF16)

    tb = lambda z: jnp.swapaxes(z, -1, -2)
    lb = lam_bar[..., None] * b_bar
    bmat = quadrants([tb(lb.real), tb(lb.imag), tb(b_bar.real), tb(b_bar.imag)])
    cc = lax.complex(c_re.astype(F32), c_im.astype(F32))
    c_l1 = cc * lam_bar[:, :, None, :]
    c_l2 = c_l1 * lam_bar[:, :, None, :]
    cmat = quadrants([tb(c_l1.real), tb(c_l2.real), -tb(c_l1.imag), -tb(c_l2.imag)])
    m0 = jnp.einsum('dgsp,dgpt->dgts', cc, b_bar).real
    m1 = jnp.einsum('dgsp,dgpt->dgts', c_l1, b_bar).real
    dmat = quadrants([m0, m1, jnp.zeros_like(m0), m0])

    def lam_of(part):
        v = part.reshape(2, nj, gpb * p).transpose(1, 0, 2)
        return jnp.repeat(v, nb, axis=1)

    lam2 = lam_bar * lam_bar
    return bmat, cmat, dmat, lam_of(lam2.real), lam_of(lam2.imag)


def _conv_silu(x, w, bias):
    l = x.shape[0]
    rows = lax.broadcasted_iota(jnp.int32, x.shape, 0)
    half = SSD_CONV // 2
    acc = x * w[half:half + 1, :] + bias
    for k in range(SSD_CONV):
        if k == half:
            continue
        off = k - half
        xs = pltpu.roll(x, shift=(-off) % l, axis=0)
        valid = jnp.logical_and(rows + off >= 0, rows + off < l)
        acc = acc + jnp.where(valid, xs, 0.0) * w[k:k + 1, :]
    return _silu(acc)


def _conv_kernel(x_ref, w_ref, b_ref, o_ref):
    o_ref[...] = _conv_silu(x_ref[...].astype(F32), w_ref[...], b_ref[...]).astype(o_ref.dtype)


def _proj_conv_kernel(a_ref, w_ref, cw_ref, cb_ref, o_ref):
    a = a_ref[...]
    wb = w_ref[...].astype(BF16)
    hw = wb.shape[1] // 2
    for c in range(2):
        cols = slice(c * hw, (c + 1) * hw)
        p = jnp.dot(a, wb[:, cols], preferred_element_type=F32)
        o_ref[:, cols] = _conv_silu(p, cw_ref[:, cols], cb_ref[:, cols]).astype(o_ref.dtype)


def _proj_conv_call(hn, w_all, col0, cw, cb):
    b, l, d = hn.shape
    c = cw.shape[1]
    tn = 512
    return pl.pallas_call(
        _proj_conv_kernel, grid=(b, c // tn),
        in_specs=[pl.BlockSpec((None, l, d), lambda i, j: (i, 0, 0), pipeline_mode=pl.Buffered(1)),
                  pl.BlockSpec((pl.Element(d), pl.Element(tn)),
                               lambda i, j: (0, (col0 // LANE + j * (tn // LANE)) * LANE)),
                  pl.BlockSpec((SSD_CONV, tn), lambda i, j: (0, j)),
                  pl.BlockSpec((1, tn), lambda i, j: (0, j))],
        out_specs=pl.BlockSpec((None, l, tn), lambda i, j: (i, 0, j)),
        out_shape=jax.ShapeDtypeStruct((b, l, c), BF16),
        compiler_params=_cp(("parallel", "arbitrary"), 60), name="in_xbc_conv")(hn, w_all, cw, cb.reshape(1, c))


def _conv_call(proj, w, bias, col0):
    b, l, _ = proj.shape
    c = w.shape[1]
    tc = 256
    off = col0 // tc
    return pl.pallas_call(
        _conv_kernel, grid=(b, c // tc),
        in_specs=[pl.BlockSpec((None, l, tc), lambda i, j: (i, 0, j + off)),
                  pl.BlockSpec((SSD_CONV, tc), lambda i, j: (0, j)),
                  pl.BlockSpec((1, tc), lambda i, j: (0, j))],
        out_specs=pl.BlockSpec((None, l, tc), lambda i, j: (i, 0, j)),
        out_shape=jax.ShapeDtypeStruct((b, l, c), BF16),
        compiler_params=_cp(("parallel", "parallel"), 40), name="ssd_conv")(proj, w, bias.reshape(1, c))


def _softplus(x):
    return jnp.maximum(x, 0.0) + jnp.log1p(jnp.exp(-jnp.abs(x)))


def _ssd_dir(xbc_ref, dt_ref, y_ref, h_ref, bias, a_neg, d, *, nh, ng, need_y):
    qn = SSD_CHUNK
    hd = SSD_HEAD_DIM
    w = nh * hd
    gw = w // ng
    gn = ng * SSD_STATE
    ii = lax.broadcasted_iota(jnp.int32, (qn, qn), 0)
    jj = lax.broadcasted_iota(jnp.int32, (qn, qn), 1)
    mask = (jj <= ii) if d == 0 else (jj >= ii)
    lmat = mask.astype(F32)
    dtv = _softplus(dt_ref[...] + bias)
    cum = jnp.dot(lmat, dtv * a_neg, precision=lax.Precision.HIGHEST, preferred_element_type=F32)
    cum_t = cum.T
    edge = qn - 1 if d == 0 else 0
    tot = cum[edge:edge + 1, :]
    dt_t = dtv.T
    wt_t = dt_t * jnp.exp(cum_t[:, edge:edge + 1] - cum_t)
    decay = jnp.exp(tot)
    lane = lax.broadcasted_iota(jnp.int32, (qn, LANE), 1)
    left = lane < hd
    zero = jnp.zeros((), BF16)
    for g in range(ng):
        bg = xbc_ref[:, w + g * SSD_STATE:w + (g + 1) * SSD_STATE]
        cg = xbc_ref[:, w + gn + g * SSD_STATE:w + gn + (g + 1) * SSD_STATE]
        bg_t = bg.astype(F32).T
        s_in = h_ref[d, :, g * gw:(g + 1) * gw]
        if need_y:
            cb = lax.dot_general(cg, bg, (((1,), (1,)), ((), ())), preferred_element_type=F32)
            yoff = jnp.dot(cg, s_in.astype(BF16), preferred_element_type=F32)
        for pr in range(gw // LANE):
            c0 = d * nh + (g * gw) // hd + 2 * pr
            col = g * gw + pr * LANE
            xp = xbc_ref[:, col:col + LANE]
            r = jnp.concatenate([jnp.where(left, xp, zero), jnp.where(left, zero, xp)], axis=0)
            tops, bots, cols = [], [], []
            for c in (c0, c0 + 1):
                bots.append((bg_t * wt_t[c:c + 1, :]).astype(BF16))
                if need_y:
                    ccol = jnp.broadcast_to(cum[:, c:c + 1], (qn, qn))
                    seg = jnp.where(mask, jnp.exp(ccol - cum_t[c:c + 1, :]), 0.0)
                    tops.append((cb * seg * dt_t[c:c + 1, :]).astype(BF16))
                    cols.append(ccol)
            dec = jnp.where(left[0:1], decay[:, c0:c0 + 1], decay[:, c0 + 1:c0 + 2])
            s_old = s_in[:, pr * LANE:(pr + 1) * LANE]
            if need_y:
                lhs = jnp.concatenate([jnp.concatenate(tops, axis=1), jnp.concatenate(bots, axis=1)], axis=0)
                out = jnp.dot(lhs, r, preferred_element_type=F32)
                ec = jnp.exp(jnp.where(left, cols[0], cols[1]))
                y_ref[:, col:col + LANE] = (out[:qn] + yoff[:, pr * LANE:(pr + 1) * LANE] * ec).astype(y_ref.dtype)
                s_new = out[qn:]
            else:
                s_new = jnp.dot(jnp.concatenate(bots, axis=1), r, preferred_element_type=F32)
            h_ref[d, :, col:col + LANE] = s_old * dec + s_new


def _ssd_kernel(*refs, nh, ng, need_y):
    if need_y:
        xf_ref, xb_ref, dtf_ref, dtb_ref, h0_ref, bias_ref, alog_ref, yf_ref, yb_ref, h_ref = refs
    else:
        xf_ref, xb_ref, dtf_ref, dtb_ref, h0_ref, bias_ref, alog_ref, h_ref = refs
        yf_ref = yb_ref = None

    @pl.when(pl.program_id(1) == 0)
    def _():
        h_ref[...] = h0_ref[...]

    bias = bias_ref[...]
    a_neg = -jnp.exp(alog_ref[...])
    _ssd_dir(xf_ref, dtf_ref, yf_ref, h_ref, bias, a_neg, 0, nh=nh, ng=ng, need_y=need_y)
    _ssd_dir(xb_ref, dtb_ref, yb_ref, h_ref, bias, a_neg, 1, nh=nh, ng=ng, need_y=need_y)


def _ssd_call(xbc, dt, h0, bias, alog, nh, ng, need_y):
    b, l, cd = xbc.shape
    w = nh * SSD_HEAD_DIM
    nc = l // SSD_CHUNK
    q = SSD_CHUNK
    fwd = lambda i, s: (i, s, 0)
    bwd = lambda i, s: (i, nc - 1 - s, 0)
    hspec = pl.BlockSpec((None, 2, SSD_STATE, w), lambda i, s: (i, 0, 0, 0))
    in_specs = [pl.BlockSpec((None, q, cd), fwd), pl.BlockSpec((None, q, cd), bwd),
                pl.BlockSpec((None, q, LANE), fwd), pl.BlockSpec((None, q, LANE), bwd),
                hspec,
                pl.BlockSpec((1, LANE), lambda i, s: (0, 0)), pl.BlockSpec((1, LANE), lambda i, s: (0, 0))]
    out_specs = [hspec]
    out_shape = [jax.ShapeDtypeStruct((b, 2, SSD_STATE, w), F32)]
    if need_y:
        out_specs = [pl.BlockSpec((None, q, w), fwd), pl.BlockSpec((None, q, w), bwd)] + out_specs
        out_shape = [jax.ShapeDtypeStruct((b, l, w), BF16)] * 2 + out_shape
    return pl.pallas_call(
        functools.partial(_ssd_kernel, nh=nh, ng=ng, need_y=need_y), grid=(b, nc),
        in_specs=in_specs, out_specs=out_specs, out_shape=out_shape,
        compiler_params=_cp(("parallel", "arbitrary"), 48),
        name="ssd_scan" if need_y else "ssd_scan_ctx")(xbc, xbc, dt, dt, h0, bias, alog)


def _gnorm_kernel(yf_ref, yb_ref, xs_ref, z_ref, d_ref, g_ref, o_ref, t_ref, *, rows, tb, gw):
    y = d_ref[...] * xs_ref[...].astype(F32) + yf_ref[...].astype(F32) + yb_ref[...].astype(F32)
    y = y * _silu(z_ref[...].astype(F32))
    bw = y.shape[1]
    parts = []
    for g in range(bw // gw):
        yg = y[:, g * gw:(g + 1) * gw]
        parts.append(yg * lax.rsqrt(jnp.mean(yg * yg, axis=-1, keepdims=True) + EPS))
    res = jnp.concatenate(parts, axis=1) * g_ref[...]
    pitch = rows + XPOSE_PAD
    for wl in range(tb):
        for kk in range(bw // LANE):
            t_ref[kk, wl * pitch:wl * pitch + rows, :] = res[wl * rows:(wl + 1) * rows, kk * LANE:(kk + 1) * LANE]
    for r in range(rows):
        o_ref[r] = jnp.concatenate(
            [t_ref[kk, pl.ds(r, tb, stride=pitch), :] for kk in range(bw // LANE)], axis=1).astype(o_ref.dtype)


def _gnorm_call(yf, yb, xbc, proj, dvec, gvec, ng):
    b, l, w = yf.shape
    rows = l // GRID_W
    tb = XPOSE_TILE
    gw = w // ng
    bw = gw * (2 if ng % 2 == 0 else 1)
    blk = pl.BlockSpec((None, tb * rows, bw), lambda i, s, g: (i, s, g))
    vec = pl.BlockSpec((1, bw), lambda i, s, g: (0, g))
    out = pl.pallas_call(
        functools.partial(_gnorm_kernel, rows=rows, tb=tb, gw=gw), grid=(b, GRID_W // tb, w // bw),
        in_specs=[blk, blk, blk, blk, vec, vec],
        out_specs=pl.BlockSpec((None, rows, tb, bw), lambda i, s, g: (i, 0, s, g)),
        out_shape=jax.ShapeDtypeStruct((b, rows, GRID_W, w), BF16),
        scratch_shapes=[pltpu.VMEM((bw // LANE, tb * (rows + XPOSE_PAD), LANE), F32)],
        compiler_params=_cp(("parallel", "parallel", "parallel"), 40),
        name="ssd_gnorm")(yf, yb, xbc, proj, dvec, gvec)
    return out.reshape(b * l, w)


def _moe_kernel(e_ref, b0_ref, nb_ref, *refs, nsub, nh):
    xs = refs[:nsub]
    wg_ref, wu_ref, wd_ref, o_ref, act_ref, wdb_ref = refs[nsub:]
    s = pl.program_id(0)
    t = pl.program_id(1)
    nb = nb_ref[s]

    @pl.when(jnp.logical_and(t < nh, nb > 0))
    def _():
        wg = wg_ref[...].astype(BF16)
        wu = wu_ref[...].astype(BF16)

        def act_block(k):
            x = xs[k][...]
            gt = jnp.dot(x, wg, preferred_element_type=F32)
            up = jnp.dot(x, wu, preferred_element_type=F32)
            act_ref[k, t] = (_silu(gt) * up).astype(BF16)

        act_block(0)
        for k in range(1, nsub):
            pl.when(k < nb)(functools.partial(act_block, k))

    @pl.when(jnp.logical_and(t >= nh, nb > 0))
    def _():
        wdb = wd_ref[...].astype(BF16)
        wdb_ref[...] = wdb

        def down_block(k, w):
            a = jnp.concatenate([act_ref[k, h] for h in range(nh)], axis=1)
            o_ref[k * MOE_BLOCK:(k + 1) * MOE_BLOCK, :] = jnp.dot(a, w, preferred_element_type=F32).astype(o_ref.dtype)

        down_block(0, wdb)
        for k in range(1, nsub):
            pl.when(k < nb)(lambda k=k: down_block(k, wdb_ref[...]))

            @pl.when(k >= nb)
            def _():
                o_ref[k * MOE_BLOCK:(k + 1) * MOE_BLOCK, :] = jnp.zeros((MOE_BLOCK, o_ref.shape[1]), o_ref.dtype)

    @pl.when(jnp.logical_and(t >= nh, nb == 0))
    def _():
        o_ref[...] = jnp.zeros(o_ref.shape, o_ref.dtype)


def _moe_call(sup_e, sup_b0, sup_nb, xs, w_gate, w_up, w_down):
    n_slots, d = xs.shape
    hid = w_gate.shape[-1]
    th = min(256, hid)
    tn = min(1024, d)
    nh = hid // th
    nt = d // tn
    ns = sup_e.shape[0]
    rsup = MOE_SUB * MOE_BLOCK

    def x_map(k):
        def index(s, t, e, b0, nb):
            sx = jnp.where(t < nh, s, jnp.minimum(s + 1, ns - 1))
            return (b0[sx * MOE_SUB + k], 0)
        return index

    def hid_idx(s, t, nb):
        return jnp.where(nb[s] > 0, jnp.minimum(t, nh - 1), nh - 1)

    def col_idx(s, t, nb):
        return jnp.where(nb[s] > 0, jnp.clip(t - nh, 0, nt - 1), nt - 1)

    gs = pltpu.PrefetchScalarGridSpec(
        num_scalar_prefetch=3, grid=(ns, nh + nt),
        in_specs=[pl.BlockSpec((MOE_BLOCK, d), x_map(k)) for k in range(MOE_SUB)] + [
            pl.BlockSpec((None, d, th), lambda s, t, e, b0, nb: (e[s], 0, hid_idx(s, t, nb))),
            pl.BlockSpec((None, d, th), lambda s, t, e, b0, nb: (e[s], 0, hid_idx(s, t, nb))),
            pl.BlockSpec((None, hid, tn), lambda s, t, e, b0, nb: (e[s], 0, col_idx(s, t, nb)))],
        out_specs=pl.BlockSpec((rsup, tn), lambda s, t, e, b0, nb: (s, jnp.clip(t - nh, 0, nt - 1))),
        scratch_shapes=[pltpu.VMEM((MOE_SUB, nh, MOE_BLOCK, th), BF16), pltpu.VMEM((hid, tn), BF16)])
    return pl.pallas_call(
        functools.partial(_moe_kernel, nsub=MOE_SUB, nh=nh), grid_spec=gs,
        out_shape=jax.ShapeDtypeStruct((ns * rsup, d), BF16),
        compiler_params=_cp(("arbitrary", "arbitrary"), 60), name="moe_experts")(
            sup_e, sup_b0, sup_nb, *([xs] * MOE_SUB), w_gate, w_up, w_down)


def _final_kernel(x_ref, ya_ref, yb_ref, w_ref, g2_ref, fg_ref, o_ref):
    wts = w_ref[...]
    moe = ya_ref[...].astype(F32) * wts[:, 0:1] + yb_ref[...].astype(F32) * wts[:, 1:2]
    x = x_ref[...] + g2_ref[...] * moe
    ms = jnp.mean(x * x, axis=-1, keepdims=True)
    o_ref[...] = x * lax.rsqrt(ms + EPS) * fg_ref[...]


def _final_call(x, y2, wts, mods, ig, fg):
    b, l, d = x.shape
    tr = min(256, l)
    nt = l // tr
    row = lambda i, r: (i * nt + r, 0)
    row2 = lambda i, r: (b * nt + i * nt + r, 0)
    ya = yb = y2
    return pl.pallas_call(
        _final_kernel, grid=(b, nt),
        in_specs=[pl.BlockSpec((None, tr, d), lambda i, r: (i, r, 0)),
                  pl.BlockSpec((tr, d), row), pl.BlockSpec((tr, d), row2),
                  pl.BlockSpec((tr, MOE_TOP_K), row),
                  pl.BlockSpec((None, None, 1, d), lambda i, r: (i, ig, 0, 0)),
                  pl.BlockSpec((1, d), lambda i, r: (0, 0))],
        out_specs=pl.BlockSpec((None, tr, d), lambda i, r: (i, r, 0)),
        out_shape=jax.ShapeDtypeStruct((b, l, d), F32),
        compiler_params=_cp(("parallel", "parallel"), 40), name="combine_final")(x, ya, yb, wts, mods, fg)


def _route(sel, ne):
    m = sel.shape[0]
    weights = sel[:, MOE_TOP_K:2 * MOE_TOP_K]
    n_assign = m * MOE_TOP_K
    flat_e = jnp.concatenate([sel[:, k] for k in range(MOE_TOP_K)], axis=0).astype(jnp.int32)
    onehot = (flat_e[:, None] == jnp.arange(ne, dtype=flat_e.dtype)[None, :]).astype(jnp.int32)
    csum = jnp.cumsum(onehot, axis=0)
    counts = csum[-1]
    local = jnp.sum(csum * onehot, axis=1) - 1
    nblk_e = (counts + MOE_BLOCK - 1) // MOE_BLOCK
    padded = nblk_e * MOE_BLOCK
    pad_end = jnp.cumsum(padded)
    pad_start = pad_end - padded
    dest = pad_start[flat_e] + local
    n_blocks = -(-(n_assign + ne * (MOE_BLOCK - 1)) // MOE_BLOCK)
    tok = jnp.arange(n_assign, dtype=jnp.int32) % m
    n_slots = n_blocks * MOE_BLOCK
    slot_tok = (jnp.arange(n_slots, dtype=jnp.int32) % m).at[dest].set(tok)
    rsup = MOE_SUB * MOE_BLOCK
    ns_e = (nblk_e + MOE_SUB - 1) // MOE_SUB
    sup_end = jnp.cumsum(ns_e)
    sup_start = sup_end - ns_e
    n_sup = (n_blocks + (MOE_SUB - 1) * ne) // MOE_SUB
    sidx = jnp.arange(n_sup, dtype=jnp.int32)
    last = sup_end[-1] - 1
    s_eff = jnp.minimum(sidx, last)
    e_s = jnp.minimum(jnp.searchsorted(sup_end, s_eff, side='right'), ne - 1).astype(jnp.int32)
    k_s = s_eff - sup_start[e_s]
    b0_s = pad_start[e_s] // MOE_BLOCK + MOE_SUB * k_s
    nb_s = jnp.clip(nblk_e[e_s] - MOE_SUB * k_s, 0, MOE_SUB)
    used = sidx <= last
    sup_nb = jnp.where(used, nb_s, 0).astype(jnp.int32)
    kk = jnp.arange(MOE_SUB, dtype=jnp.int32)[None, :]
    sup_b0 = jnp.maximum(lax.cummax(jnp.where(kk < sup_nb[:, None], b0_s[:, None] + kk, -1), axis=0), 0)
    sup_b0 = sup_b0.reshape(-1).astype(jnp.int32)
    pos = ((sup_start[flat_e] + local // rsup) * rsup + local % rsup).astype(jnp.int32)
    return weights, slot_tok, e_s, sup_b0, sup_nb, pos


def kernel(x, c, ctx, c_ctx, w_mod, b_mod, norm1_g, w_in, s5_lam_re, s5_lam_im, s5_log_dt, s5_b_re, s5_b_im, s5_c_re, s5_c_im, s5_d, s5_w_val, s5_w_gate, ssd_conv_w, ssd_conv_b, ssd_a_log, ssd_dt_bias, ssd_d, ssd_norm_g, ssd_w_out, w_o, norm2_g, moe_w_group, moe_b_group, moe_w_expert, moe_b_expert, moe_w_gate, moe_w_up, moe_w_down, final_g):
    depth = w_mod.shape[0]
    assert depth == 1, "single-layer block"
    bsz, n_lat, d = x.shape
    l_ctx = ctx.shape[1]
    w5 = s5_d.shape[1]
    nh = ssd_d.shape[1]
    w = nh * SSD_HEAD_DIM
    conv_dim = ssd_conv_w.shape[2]
    ng = (conv_dim - w) // (2 * SSD_STATE)
    ssd_in = w + conv_dim + 2 * nh
    o1, o2 = w5, w5 + ssd_in
    l = 0

    cc = jnp.concatenate([c, c_ctx[None, :]], axis=0)
    cc = jnp.pad(cc, ((0, (-cc.shape[0]) % 8), (0, 0)))
    mods = _mod_call(cc, w_mod[l], b_mod[l]).reshape(cc.shape[0], 6, 1, d)
    i_sh1, i_sc1, i_g1, i_sh2, i_sc2, i_g2 = range(6)

    w_in_l = w_in[l]
    o_dt = o1 + w + conv_dim

    hn_rm, hn_cm = _norm_lat_call(x, norm1_g[l], mods, i_sc1, i_sh1)
    hc = _norm_ctx_call(ctx, norm1_g[l], mods, bsz, i_sc1, i_sh1)
    hn_rm = hn_rm.reshape(bsz * n_lat, d)
    hn_cm = hn_cm.reshape(bsz * n_lat, d)
    hc = hc.reshape(bsz * l_ctx, d)

    u_lat = _mm_call(hn_rm, w_in_l, BF16, "in_s5", 0, o1).reshape(bsz, n_lat, w5)
    u_ctx = _mm_call(hc, w_in_l, BF16, "in_s5_ctx", 0, o1).reshape(bsz, l_ctx, w5)
    gates = _mm_call(hn_rm, w_in_l, BF16, "in_gates", o2, 2 * d)
    z_lat = _mm_call(hn_cm, w_in_l, BF16, "in_z", o1, w).reshape(bsz, n_lat, w)
    xbc_lat = _proj_conv_call(hn_cm.reshape(bsz, n_lat, d), w_in_l, o1 + w, ssd_conv_w[l], ssd_conv_b[l])
    p_ctx = _mm_call(hc, w_in_l, BF16, "in_xbc_ctx", o1 + w, conv_dim).reshape(bsz, l_ctx, conv_dim)
    dt_lat = _mm_call(hn_cm, w_in_l, F32, "in_dt", o_dt, 2 * nh).reshape(bsz, n_lat, LANE)
    dt_ctx = _mm_call(hc, w_in_l, F32, "in_dt_ctx", o_dt, 2 * nh).reshape(bsz, l_ctx, LANE)

    nj = w5 // LANE
    bmat, cmat, dmat, lre, lim = _s5_params(s5_lam_re[l], s5_lam_im[l], s5_log_dt[l], s5_b_re[l], s5_b_im[l],
                                            s5_c_re[l], s5_c_im[l], bsz)
    s5_zero = jnp.zeros((nj, 2, 2 * bsz, lre.shape[-1]), F32)
    (s5_ctx,) = _s5_call(u_ctx, bmat, cmat, dmat, lre, lim, s5_zero, False)
    ya_f, ya_b, _ = _s5_call(u_lat, bmat, cmat, dmat, lre, lim, s5_ctx, True)

    xbc_ctx = _conv_call(p_ctx, ssd_conv_w[l], ssd_conv_b[l], 0)
    pad_h = LANE - 2 * nh
    bias = jnp.pad(ssd_dt_bias[l].astype(F32).reshape(1, 2 * nh), ((0, 0), (0, pad_h)))
    alog = jnp.pad(ssd_a_log[l].astype(F32).reshape(1, 2 * nh), ((0, 0), (0, pad_h)))
    h_zero = jnp.zeros((bsz, 2, SSD_STATE, w), F32)
    (h_ctx,) = _ssd_call(xbc_ctx, dt_ctx, h_zero, bias, alog, nh, ng, False)
    y_f, y_b, _ = _ssd_call(xbc_lat, dt_lat, h_ctx, bias, alog, nh, ng, True)
    d_vec = jnp.repeat(ssd_d[l].astype(F32), SSD_HEAD_DIM).reshape(1, w)
    y_ssd = _gnorm_call(y_f, y_b, xbc_lat, z_lat, d_vec, ssd_norm_g[l].astype(F32).reshape(1, w), ng)

    m_lat = bsz * n_lat
    part_a = _glu_call(u_lat.reshape(m_lat, w5), ya_f.reshape(m_lat, w5), ya_b.reshape(m_lat, w5),
                       s5_d[l].astype(F32).reshape(1, w5),
                       s5_w_val[l].astype(BF16), s5_w_gate[l].astype(BF16), gates)
    merged = _merge_call(y_ssd, ssd_w_out[l], gates, part_a)
    x1 = _resid_call(merged, w_o[l], x, mods, i_g1)

    ngr = moe_w_group.shape[-1]
    ne = moe_w_expert.shape[-1]
    wr = jnp.concatenate([moe_w_group[l], moe_w_expert[l]], axis=1).astype(F32)
    wr = jnp.pad(wr, ((0, 0), (0, (-(ngr + ne)) % LANE)))
    br = jnp.concatenate([moe_b_group[l], moe_b_expert[l]]).astype(F32)
    br = jnp.pad(br, (0, (-(ngr + ne)) % LANE)).reshape(1, -1)
    hx, sel = _norm_router_call(x1, norm2_g[l], mods, i_sc2, i_sh2, wr, br, ngr, ne)
    m = bsz * n_lat
    hx = hx.reshape(m, d)
    weights, slot_tok, sup_e, sup_b0, sup_nb, pos = _route(sel.reshape(m, -1), ne)
    xs = hx[slot_tok]
    y_slots = _moe_call(sup_e, sup_b0, sup_nb, xs, moe_w_gate[l], moe_w_up[l], moe_w_down[l])
    y2 = y_slots[pos]
    return _final_call(x1, y2, weights.astype(F32), mods, i_g2, final_g.reshape(1, d))
```

```python
import functools
import math

import jax
import jax.numpy as jnp
from jax import lax
from jax.experimental import pallas as pl
from jax.experimental.pallas import tpu as pltpu

F32 = jnp.float32
BF16 = jnp.bfloat16

GRID_W = 64
EPS = 1e-6
LANE = 128
SSD_HEAD_DIM = 64
SSD_STATE = 128
SSD_CHUNK = 128
SSD_CONV = 5
MOE_TOP_K = 2
MOE_BLOCK = 256
MOE_SUB = 4
XPOSE_TILE = 16
XPOSE_PAD = 8
S5_CHUNK = 512
S5_ROW_PAD = 8


def _cp(sem, mb):
    return pltpu.CompilerParams(dimension_semantics=sem, vmem_limit_bytes=mb * 1024 * 1024)


def _sigmoid(x):
    return 1.0 / (1.0 + jnp.exp(-x))


def _silu(x):
    return x * _sigmoid(x)


def _mod_kernel(c_ref, w_ref, b_ref, o_ref):
    s = _silu(c_ref[...])
    o_ref[...] = jnp.dot(s.astype(BF16), w_ref[...].astype(BF16),
                         preferred_element_type=F32) + b_ref[...]


def _mod_call(cc, w, b):
    r, d = cc.shape
    n = w.shape[1]
    tn = min(512, n)
    return pl.pallas_call(
        _mod_kernel, grid=(n // tn,),
        in_specs=[pl.BlockSpec((r, d), lambda j: (0, 0)),
                  pl.BlockSpec((d, tn), lambda j: (0, j)),
                  pl.BlockSpec((1, tn), lambda j: (0, j))],
        out_specs=pl.BlockSpec((r, tn), lambda j: (0, j)),
        out_shape=jax.ShapeDtypeStruct((r, n), F32),
        compiler_params=_cp(("parallel",), 40), name="mod")(cc, w, b.reshape(1, n))


def _rmsmod(x, g, sc, sh):
    ms = jnp.mean(x * x, axis=-1, keepdims=True)
    return (x * lax.rsqrt(ms + EPS) * g) * (1.0 + sc) + sh


def _norm_lat_kernel(x_ref, g_ref, sc_ref, sh_ref, orm_ref, ocm_ref, t_ref, *, tb, d):
    pitch = tb + XPOSE_PAD
    for r in range(tb):
        y = _rmsmod(x_ref[r], g_ref[...], sc_ref[...], sh_ref[...])
        orm_ref[r] = y.astype(BF16)
        for kk in range(d // LANE):
            t_ref[kk, r * pitch:r * pitch + tb, :] = y[:, kk * LANE:(kk + 1) * LANE]
    for wl in range(tb):
        ocm_ref[wl] = jnp.concatenate(
            [t_ref[kk, pl.ds(wl, tb, stride=pitch), :] for kk in range(d // LANE)], axis=1).astype(BF16)


def _norm_lat_call(x, g, mods, isc, ish):
    b, l, d = x.shape
    rows = l // GRID_W
    tb = XPOSE_TILE
    xv = x.reshape(b, rows, GRID_W, d)
    orm, ocm = pl.pallas_call(
        functools.partial(_norm_lat_kernel, tb=tb, d=d), grid=(b, rows // tb, GRID_W // tb),
        in_specs=[pl.BlockSpec((None, tb, tb, d), lambda i, r, c: (i, r, c, 0)),
                  pl.BlockSpec((1, d), lambda i, r, c: (0, 0)),
                  pl.BlockSpec((None, None, 1, d), lambda i, r, c: (i, isc, 0, 0)),
                  pl.BlockSpec((None, None, 1, d), lambda i, r, c: (i, ish, 0, 0))],
        out_specs=[pl.BlockSpec((None, tb, tb, d), lambda i, r, c: (i, r, c, 0)),
                   pl.BlockSpec((None, tb, tb, d), lambda i, r, c: (i, c, r, 0))],
        out_shape=[jax.ShapeDtypeStruct((b, rows, GRID_W, d), BF16),
                   jax.ShapeDtypeStruct((b, GRID_W, rows, d), BF16)],
        scratch_shapes=[pltpu.VMEM((d // LANE, tb * (tb + XPOSE_PAD), LANE), F32)],
        compiler_params=_cp(("parallel", "parallel", "parallel"), 48),
        name="norm1_lat")(xv, g.reshape(1, d), mods, mods)
    return orm.reshape(b, l, d), ocm.reshape(b, l, d)


def _norm_kernel(x_ref, g_ref, sc_ref, sh_ref, o_ref):
    o_ref[...] = _rmsmod(x_ref[...], g_ref[...], sc_ref[...], sh_ref[...]).astype(o_ref.dtype)


def _norm_ctx_call(x, g, mods, row, isc, ish):
    b, l, d = x.shape
    tr = min(256, l)
    return pl.pallas_call(
        _norm_kernel, grid=(b, l // tr),
        in_specs=[pl.BlockSpec((None, tr, d), lambda i, r: (i, r, 0)),
                  pl.BlockSpec((1, d), lambda i, r: (0, 0)),
                  pl.BlockSpec((None, None, 1, d), lambda i, r: (row, isc, 0, 0)),
                  pl.BlockSpec((None, None, 1, d), lambda i, r: (row, ish, 0, 0))],
        out_specs=pl.BlockSpec((None, tr, d), lambda i, r: (i, r, 0)),
        out_shape=jax.ShapeDtypeStruct((b, l, d), BF16),
        compiler_params=_cp(("parallel", "parallel"), 40), name="norm1_ctx")(x, g.reshape(1, d), mods, mods)


def _first_argmax(v, vmax, lane):
    return jnp.min(jnp.where(v == vmax, lane, float(LANE)), axis=-1, keepdims=True)


def _norm_router_kernel(x_ref, g_ref, sc_ref, sh_ref, wr_ref, br_ref, o_ref, sel_ref, *, ngr, ne):
    y = _rmsmod(x_ref[...], g_ref[...], sc_ref[...], sh_ref[...])
    o_ref[...] = y.astype(BF16)
    y_hi = y.astype(BF16)
    y_lo = (y - y_hi.astype(F32)).astype(BF16)
    w_hi = wr_ref[0]
    lg = (jnp.dot(y_hi, w_hi, preferred_element_type=F32) + jnp.dot(y_hi, wr_ref[1], preferred_element_type=F32)
          + jnp.dot(y_lo, w_hi, preferred_element_type=F32)) + br_ref[...]
    epg = ne // ngr
    lane = lax.broadcasted_iota(jnp.int32, lg.shape, 1).astype(F32)
    ninf = -jnp.inf
    gl = jnp.where(lane < ngr, lg, ninf)
    ge = jnp.exp(gl - jnp.max(gl, axis=-1, keepdims=True))
    g_prob = ge / jnp.sum(ge, axis=-1, keepdims=True)
    g_p = jnp.max(g_prob, axis=-1, keepdims=True)
    lo = ngr + epg * _first_argmax(g_prob, g_p, lane)
    cand = jnp.where(jnp.logical_and(lane >= lo, lane < lo + epg), lg, ninf)
    v1 = jnp.max(cand, axis=-1, keepdims=True)
    l1 = _first_argmax(cand, v1, lane)
    rest = jnp.where(lane == l1, ninf, cand)
    v2 = jnp.max(rest, axis=-1, keepdims=True)
    l2 = _first_argmax(rest, v2, lane)
    ex = jnp.exp(v2 - v1)
    w1 = g_p * (1.0 / (1.0 + ex))
    w2 = g_p * (ex / (1.0 + ex))
    sel_ref[...] = jnp.where(lane == 0, l1 - ngr, jnp.where(lane == 1, l2 - ngr,
                             jnp.where(lane == 2, w1, jnp.where(lane == 3, w2, 0.0))))


def _norm_router_call(x, g, mods, isc, ish, wr, br, ngr, ne):
    b, l, d = x.shape
    tr = min(256, l)
    nr = wr.shape[1]
    w_hi = wr.astype(BF16)
    wr = jnp.stack([w_hi, (wr - w_hi.astype(F32)).astype(BF16)], axis=0)
    return pl.pallas_call(
        functools.partial(_norm_router_kernel, ngr=ngr, ne=ne), grid=(b, l // tr),
        in_specs=[pl.BlockSpec((None, tr, d), lambda i, r: (i, r, 0)),
                  pl.BlockSpec((1, d), lambda i, r: (0, 0)),
                  pl.BlockSpec((None, None, 1, d), lambda i, r: (i, isc, 0, 0)),
                  pl.BlockSpec((None, None, 1, d), lambda i, r: (i, ish, 0, 0)),
                  pl.BlockSpec((2, d, nr), lambda i, r: (0, 0, 0)),
                  pl.BlockSpec((1, nr), lambda i, r: (0, 0))],
        out_specs=[pl.BlockSpec((None, tr, d), lambda i, r: (i, r, 0)),
                   pl.BlockSpec((None, tr, nr), lambda i, r: (i, r, 0))],
        out_shape=[jax.ShapeDtypeStruct((b, l, d), BF16), jax.ShapeDtypeStruct((b, l, nr), F32)],
        compiler_params=_cp(("parallel", "parallel"), 40),
        name="norm2_router")(x, g.reshape(1, d), mods, mods, wr, br)


def _mm_kernel(a_ref, b_ref, o_ref):
    o_ref[...] = jnp.dot(a_ref[...], b_ref[...].astype(BF16), preferred_element_type=F32).astype(o_ref.dtype)


def _mm_tiles(m, n, tm_max=1024):
    tm = min(tm_max, m)
    tn = min(512, n)
    return tm, tn


def _mm_call(a, b, out_dtype, name, col0=0, n=None, tm_max=1024):
    m, k = a.shape
    n = b.shape[1] - col0 if n is None else n
    tm, tn = _mm_tiles(m, n, tm_max)
    if col0 % LANE or n % tn or n % LANE:
        b = b[:, col0:col0 + n]
        pad = (-n) % LANE
        b = jnp.pad(b, ((0, 0), (0, pad)))
        n, col0 = n + pad, 0
        tm, tn = _mm_tiles(m, n, tm_max)
    return pl.pallas_call(
        _mm_kernel, grid=(m // tm, n // tn),
        in_specs=[pl.BlockSpec((tm, k), lambda i, j: (i, 0)),
                  pl.BlockSpec((pl.Element(k), pl.Element(tn)), lambda i, j: (0, (col0 // LANE + j * (tn // LANE)) * LANE))],
        out_specs=pl.BlockSpec((tm, tn), lambda i, j: (i, j)),
        out_shape=jax.ShapeDtypeStruct((m, n), out_dtype),
        compiler_params=_cp(("parallel", "parallel"), 48 if tm <= 1024 else 60), name=name)(a, b)


def _gelu_tanh(x):
    return x * (0.5 * (1.0 + jnp.tanh(math.sqrt(2.0 / math.pi) * (x + 0.044715 * (x * x * x)))))


def _glu_kernel(u_ref, yf_ref, yb_ref, d_ref, wv_ref, wg_ref, gate_ref, o_ref, a_ref):
    @pl.when(pl.program_id(1) == 0)
    def _():
        y = d_ref[...] * u_ref[...].astype(F32) + yf_ref[...].astype(F32) + yb_ref[...].astype(F32)
        a_ref[...] = _gelu_tanh(y).astype(BF16)

    a = a_ref[...]
    val = jnp.dot(a, wv_ref[...], preferred_element_type=F32)
    gl = jnp.dot(a, wg_ref[...], preferred_element_type=F32)
    o_ref[...] = (_sigmoid(gate_ref[...].astype(F32)) * (val * _sigmoid(gl))).astype(o_ref.dtype)


def _glu_call(u, yf, yb, dvec, wv, wg, gates):
    m, k = u.shape
    n = wv.shape[1]
    tm, tn = _mm_tiles(m, n)
    row = pl.BlockSpec((tm, k), lambda i, j: (i, 0))
    return pl.pallas_call(
        _glu_kernel, grid=(m // tm, n // tn),
        in_specs=[row, row, row,
                  pl.BlockSpec((1, k), lambda i, j: (0, 0)),
                  pl.BlockSpec((k, tn), lambda i, j: (0, j)),
                  pl.BlockSpec((k, tn), lambda i, j: (0, j)),
                  pl.BlockSpec((tm, tn), lambda i, j: (i, j))],
        out_specs=pl.BlockSpec((tm, tn), lambda i, j: (i, j)),
        out_shape=jax.ShapeDtypeStruct((m, n), BF16),
        scratch_shapes=[pltpu.VMEM((tm, k), BF16)],
        compiler_params=_cp(("parallel", "arbitrary"), 56), name="s5_glu")(u, yf, yb, dvec, wv, wg, gates)


def _merge_kernel(a_ref, w_ref, gate_ref, pa_ref, o_ref):
    br = jnp.dot(a_ref[...], w_ref[...].astype(BF16), preferred_element_type=F32)
    o_ref[...] = (pa_ref[...].astype(F32) + _sigmoid(gate_ref[...].astype(F32)) * br).astype(o_ref.dtype)


def _merge_call(a, w, gates, part_a):
    m, k = a.shape
    n = w.shape[1]
    tm, tn = _mm_tiles(m, n, 2048)
    off = n // tn
    return pl.pallas_call(
        _merge_kernel, grid=(m // tm, n // tn),
        in_specs=[pl.BlockSpec((tm, k), lambda i, j: (i, 0), pipeline_mode=pl.Buffered(1)),
                  pl.BlockSpec((k, tn), lambda i, j: (0, j)),
                  pl.BlockSpec((tm, tn), lambda i, j: (i, j + off)),
                  pl.BlockSpec((tm, tn), lambda i, j: (i, j))],
        out_specs=pl.BlockSpec((tm, tn), lambda i, j: (i, j)),
        out_shape=jax.ShapeDtypeStruct((m, n), BF16),
        compiler_params=_cp(("parallel", "arbitrary"), 60), name="ssd_out_merge")(a, w, gates, part_a)


def _resid_kernel(a_ref, w_ref, x_ref, g_ref, o_ref):
    mix = jnp.dot(a_ref[...], w_ref[...].astype(BF16), preferred_element_type=F32)
    o_ref[...] = x_ref[...] + g_ref[...] * mix


def _resid_call(a, w, x, mods, ig):
    b, l, d = x.shape
    k = a.shape[1]
    tm, tn = _mm_tiles(l, d, 2048)
    nt = l // tm
    return pl.pallas_call(
        _resid_kernel, grid=(b * nt, d // tn),
        in_specs=[pl.BlockSpec((tm, k), lambda i, j: (i, 0), pipeline_mode=pl.Buffered(1)),
                  pl.BlockSpec((k, tn), lambda i, j: (0, j)),
                  pl.BlockSpec((None, tm, tn), lambda i, j: (i // nt, i % nt, j)),
                  pl.BlockSpec((None, None, 1, tn), lambda i, j: (i // nt, ig, 0, j))],
        out_specs=pl.BlockSpec((None, tm, tn), lambda i, j: (i // nt, i % nt, j)),
        out_shape=jax.ShapeDtypeStruct((b, l, d), F32),
        compiler_params=_cp(("parallel", "arbitrary"), 60), name="w_o_resid")(a, w, x, mods)


def _s5_kernel(*refs, nb, tp, p8, need_y):
    if need_y:
        uf_ref, ub_ref, b_ref, c_ref, d_ref, lre_ref, lim_ref, h0_ref, yf_ref, yb_ref, h_ref, buf_ref, il_ref = refs
    else:
        uf_ref, ub_ref, b_ref, c_ref, d_ref, lre_ref, lim_ref, h0_ref, h_ref, buf_ref, il_ref = refs
    q = 2 * nb
    nk = p8 // LANE
    pitch = tp + S5_ROW_PAD

    @pl.when(pl.program_id(1) == 0)
    def _():
        h_ref[...] = h0_ref[...]

    rev = (lax.broadcasted_iota(jnp.int32, (tp, tp), 0) + lax.broadcasted_iota(jnp.int32, (tp, tp), 1)
           == tp - 1).astype(BF16)
    def pair_rows(u_ref, b, newer_first):
        il_ref[...] = u_ref[b].astype(F32)
        even = il_ref[pl.ds(0, tp, stride=2), :].astype(BF16)
        odd = il_ref[pl.ds(1, tp, stride=2), :].astype(BF16)
        return jnp.concatenate([odd, even] if newer_first else [even, odd], axis=1)

    lhs = []
    for d in range(2):
        if d == 0:
            u = jnp.concatenate([pair_rows(uf_ref, b, False) for b in range(nb)], axis=0)
        else:
            u = jnp.concatenate(
                [jnp.dot(rev, pair_rows(ub_ref, b, True), preferred_element_type=F32).astype(BF16)
                 for b in range(nb)], axis=0)
        lhs.append(u)
        bu = jnp.dot(u, b_ref[d], preferred_element_type=F32)
        for b in range(nb):
            r0 = (d * nb + b) * pitch
            for k in range(2 * nk):
                buf_ref[k, r0:r0 + tp, :] = bu[b * tp:(b + 1) * tp, k * LANE:(k + 1) * LANE]
    ar = [lre_ref[:, k * LANE:(k + 1) * LANE] for k in range(nk)]
    ai = [lim_ref[:, k * LANE:(k + 1) * LANE] for k in range(nk)]

    def step(s, carry):
        rows = pl.ds(s, q, stride=pitch)
        out = []
        for k in range(nk):
            hr, hi = carry[k]
            nr = ar[k] * hr - ai[k] * hi + buf_ref[k, rows, :]
            ni = ar[k] * hi + ai[k] * hr + buf_ref[nk + k, rows, :]
            buf_ref[k, rows, :] = hr
            buf_ref[nk + k, rows, :] = hi
            out.append((nr, ni))
        return tuple(out)

    init = tuple((h_ref[0, :, k * LANE:(k + 1) * LANE], h_ref[1, :, k * LANE:(k + 1) * LANE]) for k in range(nk))
    fin = lax.fori_loop(0, tp, step, init, unroll=8)
    for k in range(nk):
        h_ref[0, :, k * LANE:(k + 1) * LANE] = fin[k][0]
        h_ref[1, :, k * LANE:(k + 1) * LANE] = fin[k][1]

    if need_y:
        for d in range(2):
            h = jnp.concatenate(
                [jnp.concatenate([buf_ref[k, (d * nb + b) * pitch:(d * nb + b) * pitch + tp, :].astype(BF16)
                                  for k in range(2 * nk)], axis=1) for b in range(nb)], axis=0)
            y = (jnp.dot(h, c_ref[d], preferred_element_type=F32)
                 + jnp.dot(lhs[d], d_ref[d], preferred_element_type=F32))
            for b in range(nb):
                yb = y[b * tp:(b + 1) * tp]
                if d == 1:
                    yb = jnp.dot(rev, yb.astype(BF16), preferred_element_type=F32)
                first, second = (0, 1) if d == 0 else (1, 0)
                il_ref[pl.ds(first, tp, stride=2), :] = yb[:, 0:LANE]
                il_ref[pl.ds(second, tp, stride=2), :] = yb[:, LANE:2 * LANE]
                (yf_ref if d == 0 else yb_ref)[b] = il_ref[...].astype(BF16)


def _s5_call(u, bmat, cmat, dmat, lre, lim, h0, need_y):
    nb, l, w5 = u.shape
    nj = w5 // LANE
    q = 2 * nb
    t = min(S5_CHUNK, l)
    tp = t // 2
    p8 = lre.shape[-1]
    nc = l // t
    hspec = pl.BlockSpec((None, 2, q, p8), lambda j, c: (j, 0, 0, 0))
    ublk = (nb, t, LANE)
    in_specs = [pl.BlockSpec(ublk, lambda j, c: (0, c, j)), pl.BlockSpec(ublk, lambda j, c: (0, nc - 1 - c, j)),
                pl.BlockSpec((None, 2, 2 * LANE, 2 * p8), lambda j, c: (j, 0, 0, 0)),
                pl.BlockSpec((None, 2, 2 * p8, 2 * LANE), lambda j, c: (j, 0, 0, 0)),
                pl.BlockSpec((None, 2, 2 * LANE, 2 * LANE), lambda j, c: (j, 0, 0, 0)),
                pl.BlockSpec((None, q, p8), lambda j, c: (j, 0, 0)),
                pl.BlockSpec((None, q, p8), lambda j, c: (j, 0, 0)),
                hspec]
    out_specs = [hspec]
    out_shape = [jax.ShapeDtypeStruct((nj, 2, q, p8), F32)]
    scratch = [pltpu.VMEM((2 * p8 // LANE, q * (tp + S5_ROW_PAD), LANE), F32), pltpu.VMEM((t, LANE), F32)]
    if need_y:
        out_specs = [pl.BlockSpec(ublk, lambda j, c: (0, c, j)),
                     pl.BlockSpec(ublk, lambda j, c: (0, nc - 1 - c, j))] + out_specs
        out_shape = [jax.ShapeDtypeStruct((nb, l, w5), BF16)] * 2 + out_shape
    kern = functools.partial(_s5_kernel, nb=nb, tp=tp, p8=p8, need_y=need_y)
    return pl.pallas_call(
        kern, grid=(nj, nc), in_specs=in_specs, out_specs=out_specs, out_shape=out_shape,
        scratch_shapes=scratch, compiler_params=_cp(("parallel", "arbitrary"), 48),
        name="s5_scan" if need_y else "s5_scan_ctx")(u, u, bmat, cmat, dmat, lre, lim, h0)


def _s5_params(lam_re, lam_im, log_dt, b_re, b_im, c_re, c_im, nb):
    _, g, p = lam_re.shape
    s = b_re.shape[-1]
    gpb = LANE // s
    nj = g // gpb
    lam = lax.complex(lam_re.astype(F32), lam_im.astype(F32))
    lam_bar = jnp.exp(lam * jnp.exp(log_dt.astype(F32))[..., None])
    b_bar = ((lam_bar - 1.0) / lam)[..., None] * lax.complex(b_re.astype(F32), b_im.astype(F32))
    def quadrants(parts):
        r, c = parts[0].shape[-2:]
        same_group = (jnp.arange(gpb * r)[:, None] // r) == (jnp.arange(gpb * c)[None, :] // c)

        def spread(x):
            return jnp.where(same_group, jnp.tile(x.reshape(2, nj, gpb * r, c), (1, 1, 1, gpb)), 0.0)

        top = jnp.concatenate([spread(parts[0]), spread(parts[1])], axis=-1)
        bot = jnp.concatenate([spread(parts[2]), spread(parts[3])], axis=-1)
        return jnp.concatenate([top, bot], axis=-2).transpose(1, 0, 2, 3).astype(BF16)

    tb = lambda z: jnp.swapaxes(z, -1, -2)
    lb = lam_bar[..., None] * b_bar
    bmat = quadrants([tb(lb.real), tb(lb.imag), tb(b_bar.real), tb(b_bar.imag)])
    cc = lax.complex(c_re.astype(F32), c_im.astype(F32))
    c_l1 = cc * lam_bar[:, :, None, :]
    c_l2 = c_l1 * lam_bar[:, :, None, :]
    cmat = quadrants([tb(c_l1.real), tb(c_l2.real), -tb(c_l1.imag), -tb(c_l2.imag)])
    m0 = jnp.einsum('dgsp,dgpt->dgts', cc, b_bar).real
    m1 = jnp.einsum('dgsp,dgpt->dgts', c_l1, b_bar).real
    dmat = quadrants([m0, m1, jnp.zeros_like(m0), m0])

    def lam_of(part):
        v = part.reshape(2, nj, gpb * p).transpose(1, 0, 2)
        return jnp.repeat(v, nb, axis=1)

    lam2 = lam_bar * lam_bar
    return bmat, cmat, dmat, lam_of(lam2.real), lam_of(lam2.imag)


def _conv_silu(x, w, bias):
    l = x.shape[0]
    rows = lax.broadcasted_iota(jnp.int32, x.shape, 0)
    half = SSD_CONV // 2
    acc = x * w[half:half + 1, :] + bias
    for k in range(SSD_CONV):
        if k == half:
            continue
        off = k - half
        xs = pltpu.roll(x, shift=(-off) % l, axis=0)
        valid = jnp.logical_and(rows + off >= 0, rows + off < l)
        acc = acc + jnp.where(valid, xs, 0.0) * w[k:k + 1, :]
    return _silu(acc)


def _conv_kernel(x_ref, w_ref, b_ref, o_ref):
    o_ref[...] = _conv_silu(x_ref[...].astype(F32), w_ref[...], b_ref[...]).astype(o_ref.dtype)


def _proj_conv_kernel(a_ref, w_ref, cw_ref, cb_ref, o_ref):
    a = a_ref[...]
    wb = w_ref[...].astype(BF16)
    hw = wb.shape[1] // 2
    for c in range(2):
        cols = slice(c * hw, (c + 1) * hw)
        p = jnp.dot(a, wb[:, cols], preferred_element_type=F32)
        o_ref[:, cols] = _conv_silu(p, cw_ref[:, cols], cb_ref[:, cols]).astype(o_ref.dtype)


def _proj_conv_call(hn, w_all, col0, cw, cb):
    b, l, d = hn.shape
    c = cw.shape[1]
    tn = 512
    return pl.pallas_call(
        _proj_conv_kernel, grid=(b, c // tn),
        in_specs=[pl.BlockSpec((None, l, d), lambda i, j: (i, 0, 0), pipeline_mode=pl.Buffered(1)),
                  pl.BlockSpec((pl.Element(d), pl.Element(tn)),
                               lambda i, j: (0, (col0 // LANE + j * (tn // LANE)) * LANE)),
                  pl.BlockSpec((SSD_CONV, tn), lambda i, j: (0, j)),
                  pl.BlockSpec((1, tn), lambda i, j: (0, j))],
        out_specs=pl.BlockSpec((None, l, tn), lambda i, j: (i, 0, j)),
        out_shape=jax.ShapeDtypeStruct((b, l, c), BF16),
        compiler_params=_cp(("parallel", "arbitrary"), 60), name="in_xbc_conv")(hn, w_all, cw, cb.reshape(1, c))


def _conv_call(proj, w, bias, col0):
    b, l, _ = proj.shape
    c = w.shape[1]
    tc = 256
    off = col0 // tc
    return pl.pallas_call(
        _conv_kernel, grid=(b, c // tc),
        in_specs=[pl.BlockSpec((None, l, tc), lambda i, j: (i, 0, j + off)),
                  pl.BlockSpec((SSD_CONV, tc), lambda i, j: (0, j)),
                  pl.BlockSpec((1, tc), lambda i, j: (0, j))],
        out_specs=pl.BlockSpec((None, l, tc), lambda i, j: (i, 0, j)),
        out_shape=jax.ShapeDtypeStruct((b, l, c), BF16),
        compiler_params=_cp(("parallel", "parallel"), 40), name="ssd_conv")(proj, w, bias.reshape(1, c))


def _softplus(x):
    return jnp.maximum(x, 0.0) + jnp.log1p(jnp.exp(-jnp.abs(x)))


def _ssd_dir(xbc_ref, dt_ref, y_ref, h_ref, bias, a_neg, d, *, nh, ng, need_y):
    qn = SSD_CHUNK
    hd = SSD_HEAD_DIM
    w = nh * hd
    gw = w // ng
    gn = ng * SSD_STATE
    ii = lax.broadcasted_iota(jnp.int32, (qn, qn), 0)
    jj = lax.broadcasted_iota(jnp.int32, (qn, qn), 1)
    mask = (jj <= ii) if d == 0 else (jj >= ii)
    lmat = mask.astype(F32)
    dtv = _softplus(dt_ref[...] + bias)
    cum = jnp.dot(lmat, dtv * a_neg, precision=lax.Precision.HIGHEST, preferred_element_type=F32)
    cum_t = cum.T
    edge = qn - 1 if d == 0 else 0
    tot = cum[edge:edge + 1, :]
    dt_t = dtv.T
    wt_t = dt_t * jnp.exp(cum_t[:, edge:edge + 1] - cum_t)
    decay = jnp.exp(tot)
    lane = lax.broadcasted_iota(jnp.int32, (qn, LANE), 1)
    left = lane < hd
    zero = jnp.zeros((), BF16)
    for g in range(ng):
        bg = xbc_ref[:, w + g * SSD_STATE:w + (g + 1) * SSD_STATE]
        cg = xbc_ref[:, w + gn + g * SSD_STATE:w + gn + (g + 1) * SSD_STATE]
        bg_t = bg.astype(F32).T
        s_in = h_ref[d, :, g * gw:(g + 1) * gw]
        if need_y:
            cb = lax.dot_general(cg, bg, (((1,), (1,)), ((), ())), preferred_element_type=F32)
            yoff = jnp.dot(cg, s_in.astype(BF16), preferred_element_type=F32)
        for pr in range(gw // LANE):
            c0 = d * nh + (g * gw) // hd + 2 * pr
            col = g * gw + pr * LANE
            xp = xbc_ref[:, col:col + LANE]
            r = jnp.concatenate([jnp.where(left, xp, zero), jnp.where(left, zero, xp)], axis=0)
            tops, bots, cols = [], [], []
            for c in (c0, c0 + 1):
                bots.append((bg_t * wt_t[c:c + 1, :]).astype(BF16))
                if need_y:
                    ccol = jnp.broadcast_to(cum[:, c:c + 1], (qn, qn))
                    seg = jnp.where(mask, jnp.exp(ccol - cum_t[c:c + 1, :]), 0.0)
                    tops.append((cb * seg * dt_t[c:c + 1, :]).astype(BF16))
                    cols.append(ccol)
            dec = jnp.where(left[0:1], decay[:, c0:c0 + 1], decay[:, c0 + 1:c0 + 2])
            s_old = s_in[:, pr * LANE:(pr + 1) * LANE]
            if need_y:
                lhs = jnp.concatenate([jnp.concatenate(tops, axis=1), jnp.concatenate(bots, axis=1)], axis=0)
                out = jnp.dot(lhs, r, preferred_element_type=F32)
                ec = jnp.exp(jnp.where(left, cols[0], cols[1]))
                y_ref[:, col:col + LANE] = (out[:qn] + yoff[:, pr * LANE:(pr + 1) * LANE] * ec).astype(y_ref.dtype)
                s_new = out[qn:]
            else:
                s_new = jnp.dot(jnp.concatenate(bots, axis=1), r, preferred_element_type=F32)
            h_ref[d, :, col:col + LANE] = s_old * dec + s_new


def _ssd_kernel(*refs, nh, ng, need_y):
    if need_y:
        xf_ref, xb_ref, dtf_ref, dtb_ref, h0_ref, bias_ref, alog_ref, yf_ref, yb_ref, h_ref = refs
    else:
        xf_ref, xb_ref, dtf_ref, dtb_ref, h0_ref, bias_ref, alog_ref, h_ref = refs
        yf_ref = yb_ref = None

    @pl.when(pl.program_id(1) == 0)
    def _():
        h_ref[...] = h0_ref[...]

    bias = bias_ref[...]
    a_neg = -jnp.exp(alog_ref[...])
    _ssd_dir(xf_ref, dtf_ref, yf_ref, h_ref, bias, a_neg, 0, nh=nh, ng=ng, need_y=need_y)
    _ssd_dir(xb_ref, dtb_ref, yb_ref, h_ref, bias, a_neg, 1, nh=nh, ng=ng, need_y=need_y)


def _ssd_call(xbc, dt, h0, bias, alog, nh, ng, need_y):
    b, l, cd = xbc.shape
    w = nh * SSD_HEAD_DIM
    nc = l // SSD_CHUNK
    q = SSD_CHUNK
    fwd = lambda i, s: (i, s, 0)
    bwd = lambda i, s: (i, nc - 1 - s, 0)
    hspec = pl.BlockSpec((None, 2, SSD_STATE, w), lambda i, s: (i, 0, 0, 0))
    in_specs = [pl.BlockSpec((None, q, cd), fwd), pl.BlockSpec((None, q, cd), bwd),
                pl.BlockSpec((None, q, LANE), fwd), pl.BlockSpec((None, q, LANE), bwd),
                hspec,
                pl.BlockSpec((1, LANE), lambda i, s: (0, 0)), pl.BlockSpec((1, LANE), lambda i, s: (0, 0))]
    out_specs = [hspec]
    out_shape = [jax.ShapeDtypeStruct((b, 2, SSD_STATE, w), F32)]
    if need_y:
        out_specs = [pl.BlockSpec((None, q, w), fwd), pl.BlockSpec((None, q, w), bwd)] + out_specs
        out_shape = [jax.ShapeDtypeStruct((b, l, w), BF16)] * 2 + out_shape
    return pl.pallas_call(
        functools.partial(_ssd_kernel, nh=nh, ng=ng, need_y=need_y), grid=(b, nc),
        in_specs=in_specs, out_specs=out_specs, out_shape=out_shape,
        compiler_params=_cp(("parallel", "arbitrary"), 48),
        name="ssd_scan" if need_y else "ssd_scan_ctx")(xbc, xbc, dt, dt, h0, bias, alog)


def _gnorm_kernel(yf_ref, yb_ref, xs_ref, z_ref, d_ref, g_ref, o_ref, t_ref, *, rows, tb, gw):
    y = d_ref[...] * xs_ref[...].astype(F32) + yf_ref[...].astype(F32) + yb_ref[...].astype(F32)
    y = y * _silu(z_ref[...].astype(F32))
    bw = y.shape[1]
    parts = []
    for g in range(bw // gw):
        yg = y[:, g * gw:(g + 1) * gw]
        parts.append(yg * lax.rsqrt(jnp.mean(yg * yg, axis=-1, keepdims=True) + EPS))
    res = jnp.concatenate(parts, axis=1) * g_ref[...]
    pitch = rows + XPOSE_PAD
    for wl in range(tb):
        for kk in range(bw // LANE):
            t_ref[kk, wl * pitch:wl * pitch + rows, :] = res[wl * rows:(wl + 1) * rows, kk * LANE:(kk + 1) * LANE]
    for r in range(rows):
        o_ref[r] = jnp.concatenate(
            [t_ref[kk, pl.ds(r, tb, stride=pitch), :] for kk in range(bw // LANE)], axis=1).astype(o_ref.dtype)


def _gnorm_call(yf, yb, xbc, proj, dvec, gvec, ng):
    b, l, w = yf.shape
    rows = l // GRID_W
    tb = XPOSE_TILE
    gw = w // ng
    bw = gw * (2 if ng % 2 == 0 else 1)
    blk = pl.BlockSpec((None, tb * rows, bw), lambda i, s, g: (i, s, g))
    vec = pl.BlockSpec((1, bw), lambda i, s, g: (0, g))
    out = pl.pallas_call(
        functools.partial(_gnorm_kernel, rows=rows, tb=tb, gw=gw), grid=(b, GRID_W // tb, w // bw),
        in_specs=[blk, blk, blk, blk, vec, vec],
        out_specs=pl.BlockSpec((None, rows, tb, bw), lambda i, s, g: (i, 0, s, g)),
        out_shape=jax.ShapeDtypeStruct((b, rows, GRID_W, w), BF16),
        scratch_shapes=[pltpu.VMEM((bw // LANE, tb * (rows + XPOSE_PAD), LANE), F32)],
        compiler_params=_cp(("parallel", "parallel", "parallel"), 40),
        name="ssd_gnorm")(yf, yb, xbc, proj, dvec, gvec)
    return out.reshape(b * l, w)


def _moe_kernel(e_ref, b0_ref, nb_ref, *refs, nsub, nh):
    xs = refs[:nsub]
    wg_ref, wu_ref, wd_ref, o_ref, act_ref, wdb_ref = refs[nsub:]
    s = pl.program_id(0)
    t = pl.program_id(1)
    nb = nb_ref[s]

    @pl.when(jnp.logical_and(t < nh, nb > 0))
    def _():
        wg = wg_ref[...].astype(BF16)
        wu = wu_ref[...].astype(BF16)

        def act_block(k):
            x = xs[k][...]
            gt = jnp.dot(x, wg, preferred_element_type=F32)
            up = jnp.dot(x, wu, preferred_element_type=F32)
            act_ref[k, t] = (_silu(gt) * up).astype(BF16)

        act_block(0)
        for k in range(1, nsub):
            pl.when(k < nb)(functools.partial(act_block, k))

    @pl.when(jnp.logical_and(t >= nh, nb > 0))
    def _():
        wdb = wd_ref[...].astype(BF16)
        wdb_ref[...] = wdb

        def down_block(k, w):
            a = jnp.concatenate([act_ref[k, h] for h in range(nh)], axis=1)
            o_ref[k * MOE_BLOCK:(k + 1) * MOE_BLOCK, :] = jnp.dot(a, w, preferred_element_type=F32).astype(o_ref.dtype)

        down_block(0, wdb)
        for k in range(1, nsub):
            pl.when(k < nb)(lambda k=k: down_block(k, wdb_ref[...]))

            @pl.when(k >= nb)
            def _():
                o_ref[k * MOE_BLOCK:(k + 1) * MOE_BLOCK, :] = jnp.zeros((MOE_BLOCK, o_ref.shape[1]), o_ref.dtype)

    @pl.when(jnp.logical_and(t >= nh, nb == 0))
    def _():
        o_ref[...] = jnp.zeros(o_ref.shape, o_ref.dtype)


def _moe_call(sup_e, sup_b0, sup_nb, xs, w_gate, w_up, w_down):
    n_slots, d = xs.shape
    hid = w_gate.shape[-1]
    th = min(256, hid)
    tn = min(1024, d)
    nh = hid // th
    nt = d // tn
    ns = sup_e.shape[0]
    rsup = MOE_SUB * MOE_BLOCK

    def x_map(k):
        def index(s, t, e, b0, nb):
            sx = jnp.where(t < nh, s, jnp.minimum(s + 1, ns - 1))
            return (b0[sx * MOE_SUB + k], 0)
        return index

    def hid_idx(s, t, nb):
        return jnp.where(nb[s] > 0, jnp.minimum(t, nh - 1), nh - 1)

    def col_idx(s, t, nb):
        return jnp.where(nb[s] > 0, jnp.clip(t - nh, 0, nt - 1), nt - 1)

    gs = pltpu.PrefetchScalarGridSpec(
        num_scalar_prefetch=3, grid=(ns, nh + nt),
        in_specs=[pl.BlockSpec((MOE_BLOCK, d), x_map(k)) for k in range(MOE_SUB)] + [
            pl.BlockSpec((None, d, th), lambda s, t, e, b0, nb: (e[s], 0, hid_idx(s, t, nb))),
            pl.BlockSpec((None, d, th), lambda s, t, e, b0, nb: (e[s], 0, hid_idx(s, t, nb))),
            pl.BlockSpec((None, hid, tn), lambda s, t, e, b0, nb: (e[s], 0, col_idx(s, t, nb)))],
        out_specs=pl.BlockSpec((rsup, tn), lambda s, t, e, b0, nb: (s, jnp.clip(t - nh, 0, nt - 1))),
        scratch_shapes=[pltpu.VMEM((MOE_SUB, nh, MOE_BLOCK, th), BF16), pltpu.VMEM((hid, tn), BF16)])
    return pl.pallas_call(
        functools.partial(_moe_kernel, nsub=MOE_SUB, nh=nh), grid_spec=gs,
        out_shape=jax.ShapeDtypeStruct((ns * rsup, d), BF16),
        compiler_params=_cp(("arbitrary", "arbitrary"), 60), name="moe_experts")(
            sup_e, sup_b0, sup_nb, *([xs] * MOE_SUB), w_gate, w_up, w_down)


def _final_kernel(x_ref, ya_ref, yb_ref, w_ref, g2_ref, fg_ref, o_ref):
    wts = w_ref[...]
    moe = ya_ref[...].astype(F32) * wts[:, 0:1] + yb_ref[...].astype(F32) * wts[:, 1:2]
    x = x_ref[...] + g2_ref[...] * moe
    ms = jnp.mean(x * x, axis=-1, keepdims=True)
    o_ref[...] = x * lax.rsqrt(ms + EPS) * fg_ref[...]


def _final_call(x, y2, wts, mods, ig, fg):
    b, l, d = x.shape
    tr = min(256, l)
    nt = l // tr
    row = lambda i, r: (i * nt + r, 0)
    row2 = lambda i, r: (b * nt + i * nt + r, 0)
    ya = yb = y2
    return pl.pallas_call(
        _final_kernel, grid=(b, nt),
        in_specs=[pl.BlockSpec((None, tr, d), lambda i, r: (i, r, 0)),
                  pl.BlockSpec((tr, d), row), pl.BlockSpec((tr, d), row2),
                  pl.BlockSpec((tr, MOE_TOP_K), row),
                  pl.BlockSpec((None, None, 1, d), lambda i, r: (i, ig, 0, 0)),
                  pl.BlockSpec((1, d), lambda i, r: (0, 0))],
        out_specs=pl.BlockSpec((None, tr, d), lambda i, r: (i, r, 0)),
        out_shape=jax.ShapeDtypeStruct((b, l, d), F32),
        compiler_params=_cp(("parallel", "parallel"), 40), name="combine_final")(x, ya, yb, wts, mods, fg)


def _route(sel, ne):
    m = sel.shape[0]
    weights = sel[:, MOE_TOP_K:2 * MOE_TOP_K]
    n_assign = m * MOE_TOP_K
    flat_e = jnp.concatenate([sel[:, k] for k in range(MOE_TOP_K)], axis=0).astype(jnp.int32)
    onehot = (flat_e[:, None] == jnp.arange(ne, dtype=flat_e.dtype)[None, :]).astype(jnp.int32)
    csum = jnp.cumsum(onehot, axis=0)
    counts = csum[-1]
    local = jnp.sum(csum * onehot, axis=1) - 1
    nblk_e = (counts + MOE_BLOCK - 1) // MOE_BLOCK
    padded = nblk_e * MOE_BLOCK
    pad_end = jnp.cumsum(padded)
    pad_start = pad_end - padded
    dest = pad_start[flat_e] + local
    n_blocks = -(-(n_assign + ne * (MOE_BLOCK - 1)) // MOE_BLOCK)
    tok = jnp.arange(n_assign, dtype=jnp.int32) % m
    n_slots = n_blocks * MOE_BLOCK
    slot_tok = (jnp.arange(n_slots, dtype=jnp.int32) % m).at[dest].set(tok)
    rsup = MOE_SUB * MOE_BLOCK
    ns_e = (nblk_e + MOE_SUB - 1) // MOE_SUB
    sup_end = jnp.cumsum(ns_e)
    sup_start = sup_end - ns_e
    n_sup = (n_blocks + (MOE_SUB - 1) * ne) // MOE_SUB
    sidx = jnp.arange(n_sup, dtype=jnp.int32)
    last = sup_end[-1] - 1
    s_eff = jnp.minimum(sidx, last)
    e_s = jnp.minimum(jnp.searchsorted(sup_end, s_eff, side='right'), ne - 1).astype(jnp.int32)
    k_s = s_eff - sup_start[e_s]
    b0_s = pad_start[e_s] // MOE_BLOCK + MOE_SUB * k_s
    nb_s = jnp.clip(nblk_e[e_s] - MOE_SUB * k_s, 0, MOE_SUB)
    used = sidx <= last
    sup_nb = jnp.where(used, nb_s, 0).astype(jnp.int32)
    kk = jnp.arange(MOE_SUB, dtype=jnp.int32)[None, :]
    sup_b0 = jnp.maximum(lax.cummax(jnp.where(kk < sup_nb[:, None], b0_s[:, None] + kk, -1), axis=0), 0)
    sup_b0 = sup_b0.reshape(-1).astype(jnp.int32)
    pos = ((sup_start[flat_e] + local // rsup) * rsup + local % rsup).astype(jnp.int32)
    return weights, slot_tok, e_s, sup_b0, sup_nb, pos


def kernel(x, c, ctx, c_ctx, w_mod, b_mod, norm1_g, w_in, s5_lam_re, s5_lam_im, s5_log_dt, s5_b_re, s5_b_im, s5_c_re, s5_c_im, s5_d, s5_w_val, s5_w_gate, ssd_conv_w, ssd_conv_b, ssd_a_log, ssd_dt_bias, ssd_d, ssd_norm_g, ssd_w_out, w_o, norm2_g, moe_w_group, moe_b_group, moe_w_expert, moe_b_expert, moe_w_gate, moe_w_up, moe_w_down, final_g):
    depth = w_mod.shape[0]
    assert depth == 1, "single-layer block"
    bsz, n_lat, d = x.shape
    l_ctx = ctx.shape[1]
    w5 = s5_d.shape[1]
    nh = ssd_d.shape[1]
    w = nh * SSD_HEAD_DIM
    conv_dim = ssd_conv_w.shape[2]
    ng = (conv_dim - w) // (2 * SSD_STATE)
    ssd_in = w + conv_dim + 2 * nh
    o1, o2 = w5, w5 + ssd_in
    l = 0

    cc = jnp.concatenate([c, c_ctx[None, :]], axis=0)
    cc = jnp.pad(cc, ((0, (-cc.shape[0]) % 8), (0, 0)))
    mods = _mod_call(cc, w_mod[l], b_mod[l]).reshape(cc.shape[0], 6, 1, d)
    i_sh1, i_sc1, i_g1, i_sh2, i_sc2, i_g2 = range(6)

    w_in_l = w_in[l]
    o_dt = o1 + w + conv_dim

    hn_rm, hn_cm = _norm_lat_call(x, norm1_g[l], mods, i_sc1, i_sh1)
    hc = _norm_ctx_call(ctx, norm1_g[l], mods, bsz, i_sc1, i_sh1)
    hn_rm = hn_rm.reshape(bsz * n_lat, d)
    hn_cm = hn_cm.reshape(bsz * n_lat, d)
    hc = hc.reshape(bsz * l_ctx, d)

    u_lat = _mm_call(hn_rm, w_in_l, BF16, "in_s5", 0, o1, tm_max=2048).reshape(bsz, n_lat, w5)
    u_ctx = _mm_call(hc, w_in_l, BF16, "in_s5_ctx", 0, o1).reshape(bsz, l_ctx, w5)
    gates = _mm_call(hn_rm, w_in_l, BF16, "in_gates", o2, 2 * d)
    z_lat = _mm_call(hn_cm, w_in_l, BF16, "in_z", o1, w).reshape(bsz, n_lat, w)
    xbc_lat = _proj_conv_call(hn_cm.reshape(bsz, n_lat, d), w_in_l, o1 + w, ssd_conv_w[l], ssd_conv_b[l])
    p_ctx = _mm_call(hc, w_in_l, BF16, "in_xbc_ctx", o1 + w, conv_dim).reshape(bsz, l_ctx, conv_dim)
    dt_lat = _mm_call(hn_cm, w_in_l, F32, "in_dt", o_dt, 2 * nh).reshape(bsz, n_lat, LANE)
    dt_ctx = _mm_call(hc, w_in_l, F32, "in_dt_ctx", o_dt, 2 * nh).reshape(bsz, l_ctx, LANE)

    nj = w5 // LANE
    bmat, cmat, dmat, lre, lim = _s5_params(s5_lam_re[l], s5_lam_im[l], s5_log_dt[l], s5_b_re[l], s5_b_im[l],
                                            s5_c_re[l], s5_c_im[l], bsz)
    s5_zero = jnp.zeros((nj, 2, 2 * bsz, lre.shape[-1]), F32)
    (s5_ctx,) = _s5_call(u_ctx, bmat, cmat, dmat, lre, lim, s5_zero, False)
    ya_f, ya_b, _ = _s5_call(u_lat, bmat, cmat, dmat, lre, lim, s5_ctx, True)

    xbc_ctx = _conv_call(p_ctx, ssd_conv_w[l], ssd_conv_b[l], 0)
    pad_h = LANE - 2 * nh
    bias = jnp.pad(ssd_dt_bias[l].astype(F32).reshape(1, 2 * nh), ((0, 0), (0, pad_h)))
    alog = jnp.pad(ssd_a_log[l].astype(F32).reshape(1, 2 * nh), ((0, 0), (0, pad_h)))
    h_zero = jnp.zeros((bsz, 2, SSD_STATE, w), F32)
    (h_ctx,) = _ssd_call(xbc_ctx, dt_ctx, h_zero, bias, alog, nh, ng, False)
    y_f, y_b, _ = _ssd_call(xbc_lat, dt_lat, h_ctx, bias, alog, nh, ng, True)
    d_vec = jnp.repeat(ssd_d[l].astype(F32), SSD_HEAD_DIM).reshape(1, w)
    y_ssd = _gnorm_call(y_f, y_b, xbc_lat, z_lat, d_vec, ssd_norm_g[l].astype(F32).reshape(1, w), ng)

    m_lat = bsz * n_lat
    part_a = _glu_call(u_lat.reshape(m_lat, w5), ya_f.reshape(m_lat, w5), ya_b.reshape(m_lat, w5),
                       s5_d[l].astype(F32).reshape(1, w5),
                       s5_w_val[l].astype(BF16), s5_w_gate[l].astype(BF16), gates)
    merged = _merge_call(y_ssd, ssd_w_out[l], gates, part_a)
    x1 = _resid_call(merged, w_o[l], x, mods, i_g1)

    ngr = moe_w_group.shape[-1]
    ne = moe_w_expert.shape[-1]
    wr = jnp.concatenate([moe_w_group[l], moe_w_expert[l]], axis=1).astype(F32)
    wr = jnp.pad(wr, ((0, 0), (0, (-(ngr + ne)) % LANE)))
    br = jnp.concatenate([moe_b_group[l], moe_b_expert[l]]).astype(F32)
    br = jnp.pad(br, (0, (-(ngr + ne)) % LANE)).reshape(1, -1)
    hx, sel = _norm_router_call(x1, norm2_g[l], mods, i_sc2, i_sh2, wr, br, ngr, ne)
    m = bsz * n_lat
    hx = hx.reshape(m, d)
    weights, slot_tok, sup_e, sup_b0, sup_nb, pos = _route(sel.reshape(m, -1), ne)
    xs = hx[slot_tok]
    y_slots = _moe_call(sup_e, sup_b0, sup_nb, xs, moe_w_gate[l], moe_w_up[l], moe_w_down[l])
    y2 = y_slots[pos]
    return _final_call(x1, y2, weights.astype(F32), mods, i_g2, final_g.reshape(1, d))
```

```python
import functools
import math

import jax
import jax.numpy as jnp
from jax import lax
from jax.experimental import pallas as pl
from jax.experimental.pallas import tpu as pltpu

F32 = jnp.float32
BF16 = jnp.bfloat16

GRID_W = 64
EPS = 1e-6
LANE = 128
SSD_HEAD_DIM = 64
SSD_STATE = 128
SSD_CHUNK = 128
SSD_CONV = 5
MOE_TOP_K = 2
MOE_BLOCK = 256
MOE_SUB = 4
XPOSE_TILE = 16
XPOSE_PAD = 8
S5_CHUNK = 512
S5_ROW_PAD = 8


def _cp(sem, mb):
    return pltpu.CompilerParams(dimension_semantics=sem, vmem_limit_bytes=mb * 1024 * 1024)


def _sigmoid(x):
    return 1.0 / (1.0 + jnp.exp(-x))


def _silu(x):
    return x * _sigmoid(x)


def _mod_kernel(c_ref, w_ref, b_ref, o_ref):
    s = _silu(c_ref[...])
    o_ref[...] = jnp.dot(s.astype(BF16), w_ref[...].astype(BF16),
                         preferred_element_type=F32) + b_ref[...]


def _mod_call(cc, w, b):
    r, d = cc.shape
    n = w.shape[1]
    tn = min(512, n)
    return pl.pallas_call(
        _mod_kernel, grid=(n // tn,),
        in_specs=[pl.BlockSpec((r, d), lambda j: (0, 0)),
                  pl.BlockSpec((d, tn), lambda j: (0, j)),
                  pl.BlockSpec((1, tn), lambda j: (0, j))],
        out_specs=pl.BlockSpec((r, tn), lambda j: (0, j)),
        out_shape=jax.ShapeDtypeStruct((r, n), F32),
        compiler_params=_cp(("parallel",), 40), name="mod")(cc, w, b.reshape(1, n))


def _rmsmod(x, g, sc, sh):
    ms = jnp.mean(x * x, axis=-1, keepdims=True)
    return (x * lax.rsqrt(ms + EPS) * g) * (1.0 + sc) + sh


def _norm_lat_kernel(x_ref, g_ref, sc_ref, sh_ref, orm_ref, ocm_ref, t_ref, *, tb, d):
    pitch = tb + XPOSE_PAD
    for r in range(tb):
        y = _rmsmod(x_ref[r], g_ref[...], sc_ref[...], sh_ref[...])
        orm_ref[r] = y.astype(BF16)
        for kk in range(d // LANE):
            t_ref[kk, r * pitch:r * pitch + tb, :] = y[:, kk * LANE:(kk + 1) * LANE]
    for wl in range(tb):
        ocm_ref[wl] = jnp.concatenate(
            [t_ref[kk, pl.ds(wl, tb, stride=pitch), :] for kk in range(d // LANE)], axis=1).astype(BF16)


def _norm_lat_call(x, g, mods, isc, ish):
    b, l, d = x.shape
    rows = l // GRID_W
    tb = XPOSE_TILE
    xv = x.reshape(b, rows, GRID_W, d)
    orm, ocm = pl.pallas_call(
        functools.partial(_norm_lat_kernel, tb=tb, d=d), grid=(b, rows // tb, GRID_W // tb),
        in_specs=[pl.BlockSpec((None, tb, tb, d), lambda i, r, c: (i, r, c, 0)),
                  pl.BlockSpec((1, d), lambda i, r, c: (0, 0)),
                  pl.BlockSpec((None, None, 1, d), lambda i, r, c: (i, isc, 0, 0)),
                  pl.BlockSpec((None, None, 1, d), lambda i, r, c: (i, ish, 0, 0))],
        out_specs=[pl.BlockSpec((None, tb, tb, d), lambda i, r, c: (i, r, c, 0)),
                   pl.BlockSpec((None, tb, tb, d), lambda i, r, c: (i, c, r, 0))],
        out_shape=[jax.ShapeDtypeStruct((b, rows, GRID_W, d), BF16),
                   jax.ShapeDtypeStruct((b, GRID_W, rows, d), BF16)],
        scratch_shapes=[pltpu.VMEM((d // LANE, tb * (tb + XPOSE_PAD), LANE), F32)],
        compiler_params=_cp(("parallel", "parallel", "parallel"), 48),
        name="norm1_lat")(xv, g.reshape(1, d), mods, mods)
    return orm.reshape(b, l, d), ocm.reshape(b, l, d)


def _norm_kernel(x_ref, g_ref, sc_ref, sh_ref, o_ref):
    o_ref[...] = _rmsmod(x_ref[...], g_ref[...], sc_ref[...], sh_ref[...]).astype(o_ref.dtype)


def _norm_ctx_call(x, g, mods, row, isc, ish):
    b, l, d = x.shape
    tr = min(256, l)
    return pl.pallas_call(
        _norm_kernel, grid=(b, l // tr),
        in_specs=[pl.BlockSpec((None, tr, d), lambda i, r: (i, r, 0)),
                  pl.BlockSpec((1, d), lambda i, r: (0, 0)),
                  pl.BlockSpec((None, None, 1, d), lambda i, r: (row, isc, 0, 0)),
                  pl.BlockSpec((None, None, 1, d), lambda i, r: (row, ish, 0, 0))],
        out_specs=pl.BlockSpec((None, tr, d), lambda i, r: (i, r, 0)),
        out_shape=jax.ShapeDtypeStruct((b, l, d), BF16),
        compiler_params=_cp(("parallel", "parallel"), 40), name="norm1_ctx")(x, g.reshape(1, d), mods, mods)


def _first_argmax(v, vmax, lane):
    return jnp.min(jnp.where(v == vmax, lane, float(LANE)), axis=-1, keepdims=True)


def _norm_router_kernel(x_ref, g_ref, sc_ref, sh_ref, wr_ref, br_ref, o_ref, sel_ref, *, ngr, ne):
    y = _rmsmod(x_ref[...], g_ref[...], sc_ref[...], sh_ref[...])
    o_ref[...] = y.astype(BF16)
    y_hi = y.astype(BF16)
    y_lo = (y - y_hi.astype(F32)).astype(BF16)
    w_hi = wr_ref[0]
    lg = (jnp.dot(y_hi, w_hi, preferred_element_type=F32) + jnp.dot(y_hi, wr_ref[1], preferred_element_type=F32)
          + jnp.dot(y_lo, w_hi, preferred_element_type=F32)) + br_ref[...]
    epg = ne // ngr
    lane = lax.broadcasted_iota(jnp.int32, lg.shape, 1).astype(F32)
    ninf = -jnp.inf
    gl = jnp.where(lane < ngr, lg, ninf)
    ge = jnp.exp(gl - jnp.max(gl, axis=-1, keepdims=True))
    g_prob = ge / jnp.sum(ge, axis=-1, keepdims=True)
    g_p = jnp.max(g_prob, axis=-1, keepdims=True)
    lo = ngr + epg * _first_argmax(g_prob, g_p, lane)
    cand = jnp.where(jnp.logical_and(lane >= lo, lane < lo + epg), lg, ninf)
    v1 = jnp.max(cand, axis=-1, keepdims=True)
    l1 = _first_argmax(cand, v1, lane)
    rest = jnp.where(lane == l1, ninf, cand)
    v2 = jnp.max(rest, axis=-1, keepdims=True)
    l2 = _first_argmax(rest, v2, lane)
    ex = jnp.exp(v2 - v1)
    w1 = g_p * (1.0 / (1.0 + ex))
    w2 = g_p * (ex / (1.0 + ex))
    sel_ref[...] = jnp.where(lane == 0, l1 - ngr, jnp.where(lane == 1, l2 - ngr,
                             jnp.where(lane == 2, w1, jnp.where(lane == 3, w2, 0.0))))


def _norm_router_call(x, g, mods, isc, ish, wr, br, ngr, ne):
    b, l, d = x.shape
    tr = min(256, l)
    nr = wr.shape[1]
    w_hi = wr.astype(BF16)
    wr = jnp.stack([w_hi, (wr - w_hi.astype(F32)).astype(BF16)], axis=0)
    return pl.pallas_call(
        functools.partial(_norm_router_kernel, ngr=ngr, ne=ne), grid=(b, l // tr),
        in_specs=[pl.BlockSpec((None, tr, d), lambda i, r: (i, r, 0)),
                  pl.BlockSpec((1, d), lambda i, r: (0, 0)),
                  pl.BlockSpec((None, None, 1, d), lambda i, r: (i, isc, 0, 0)),
                  pl.BlockSpec((None, None, 1, d), lambda i, r: (i, ish, 0, 0)),
                  pl.BlockSpec((2, d, nr), lambda i, r: (0, 0, 0)),
                  pl.BlockSpec((1, nr), lambda i, r: (0, 0))],
        out_specs=[pl.BlockSpec((None, tr, d), lambda i, r: (i, r, 0)),
                   pl.BlockSpec((None, tr, nr), lambda i, r: (i, r, 0))],
        out_shape=[jax.ShapeDtypeStruct((b, l, d), BF16), jax.ShapeDtypeStruct((b, l, nr), F32)],
        compiler_params=_cp(("parallel", "parallel"), 40),
        name="norm2_router")(x, g.reshape(1, d), mods, mods, wr, br)


def _mm_kernel(a_ref, b_ref, o_ref):
    o_ref[...] = jnp.dot(a_ref[...], b_ref[...].astype(BF16), preferred_element_type=F32).astype(o_ref.dtype)


def _mm_tiles(m, n, tm_max=1024):
    tm = min(tm_max, m)
    tn = min(512, n)
    return tm, tn


def _mm_call(a, b, out_dtype, name, col0=0, n=None, tm_max=1024):
    m, k = a.shape
    n = b.shape[1] - col0 if n is None else n
    tm, tn = _mm_tiles(m, n, tm_max)
    if col0 % LANE or n % tn or n % LANE:
        b = b[:, col0:col0 + n]
        pad = (-n) % LANE
        b = jnp.pad(b, ((0, 0), (0, pad)))
        n, col0 = n + pad, 0
        tm, tn = _mm_tiles(m, n, tm_max)
    return pl.pallas_call(
        _mm_kernel, grid=(m // tm, n // tn),
        in_specs=[pl.BlockSpec((tm, k), lambda i, j: (i, 0)),
                  pl.BlockSpec((pl.Element(k), pl.Element(tn)), lambda i, j: (0, (col0 // LANE + j * (tn // LANE)) * LANE))],
        out_specs=pl.BlockSpec((tm, tn), lambda i, j: (i, j)),
        out_shape=jax.ShapeDtypeStruct((m, n), out_dtype),
        compiler_params=_cp(("parallel", "parallel"), 48 if tm <= 1024 else 60), name=name)(a, b)


def _gelu_tanh(x):
    return x * (0.5 * (1.0 + jnp.tanh(math.sqrt(2.0 / math.pi) * (x + 0.044715 * (x * x * x)))))


def _glu_kernel(u_ref, yf_ref, yb_ref, d_ref, wv_ref, wg_ref, gate_ref, o_ref, a_ref):
    @pl.when(pl.program_id(1) == 0)
    def _():
        y = d_ref[...] * u_ref[...].astype(F32) + yf_ref[...].astype(F32) + yb_ref[...].astype(F32)
        a_ref[...] = _gelu_tanh(y).astype(BF16)

    a = a_ref[...]
    val = jnp.dot(a, wv_ref[...], preferred_element_type=F32)
    gl = jnp.dot(a, wg_ref[...], preferred_element_type=F32)
    o_ref[...] = (_sigmoid(gate_ref[...].astype(F32)) * (val * _sigmoid(gl))).astype(o_ref.dtype)


def _glu_call(u, yf, yb, dvec, wv, wg, gates):
    m, k = u.shape
    n = wv.shape[1]
    tm, tn = _mm_tiles(m, n)
    row = pl.BlockSpec((tm, k), lambda i, j: (i, 0))
    return pl.pallas_call(
        _glu_kernel, grid=(m // tm, n // tn),
        in_specs=[row, row, row,
                  pl.BlockSpec((1, k), lambda i, j: (0, 0)),
                  pl.BlockSpec((k, tn), lambda i, j: (0, j)),
                  pl.BlockSpec((k, tn), lambda i, j: (0, j)),
                  pl.BlockSpec((tm, tn), lambda i, j: (i, j))],
        out_specs=pl.BlockSpec((tm, tn), lambda i, j: (i, j)),
        out_shape=jax.ShapeDtypeStruct((m, n), BF16),
        scratch_shapes=[pltpu.VMEM((tm, k), BF16)],
        compiler_params=_cp(("parallel", "arbitrary"), 56), name="s5_glu")(u, yf, yb, dvec, wv, wg, gates)


def _merge_kernel(a_ref, w_ref, gate_ref, pa_ref, o_ref):
    br = jnp.dot(a_ref[...], w_ref[...].astype(BF16), preferred_element_type=F32)
    o_ref[...] = (pa_ref[...].astype(F32) + _sigmoid(gate_ref[...].astype(F32)) * br).astype(o_ref.dtype)


def _merge_call(a, w, gates, part_a):
    m, k = a.shape
    n = w.shape[1]
    tm, tn = _mm_tiles(m, n)
    off = n // tn
    return pl.pallas_call(
        _merge_kernel, grid=(m // tm, n // tn),
        in_specs=[pl.BlockSpec((tm, k), lambda i, j: (i, 0)),
                  pl.BlockSpec((k, tn), lambda i, j: (0, j)),
                  pl.BlockSpec((tm, tn), lambda i, j: (i, j + off)),
                  pl.BlockSpec((tm, tn), lambda i, j: (i, j))],
        out_specs=pl.BlockSpec((tm, tn), lambda i, j: (i, j)),
        out_shape=jax.ShapeDtypeStruct((m, n), BF16),
        compiler_params=_cp(("parallel", "parallel"), 48), name="ssd_out_merge")(a, w, gates, part_a)


def _resid_kernel(a_ref, w_ref, x_ref, g_ref, o_ref):
    mix = jnp.dot(a_ref[...], w_ref[...].astype(BF16), preferred_element_type=F32)
    o_ref[...] = x_ref[...] + g_ref[...] * mix


def _resid_call(a, w, x, mods, ig):
    b, l, d = x.shape
    k = a.shape[1]
    tm, tn = _mm_tiles(l, d)
    nt = l // tm
    return pl.pallas_call(
        _resid_kernel, grid=(b * nt, d // tn),
        in_specs=[pl.BlockSpec((tm, k), lambda i, j: (i, 0)),
                  pl.BlockSpec((k, tn), lambda i, j: (0, j)),
                  pl.BlockSpec((None, tm, tn), lambda i, j: (i // nt, i % nt, j)),
                  pl.BlockSpec((None, None, 1, tn), lambda i, j: (i // nt, ig, 0, j))],
        out_specs=pl.BlockSpec((None, tm, tn), lambda i, j: (i // nt, i % nt, j)),
        out_shape=jax.ShapeDtypeStruct((b, l, d), F32),
        compiler_params=_cp(("parallel", "parallel"), 48), name="w_o_resid")(a, w, x, mods)


def _s5_kernel(*refs, nb, tp, p8, need_y):
    if need_y:
        uf_ref, ub_ref, b_ref, c_ref, d_ref, lre_ref, lim_ref, h0_ref, yf_ref, yb_ref, h_ref, buf_ref, il_ref = refs
    else:
        uf_ref, ub_ref, b_ref, c_ref, d_ref, lre_ref, lim_ref, h0_ref, h_ref, buf_ref, il_ref = refs
    q = 2 * nb
    nk = p8 // LANE
    pitch = tp + S5_ROW_PAD

    @pl.when(pl.program_id(1) == 0)
    def _():
        h_ref[...] = h0_ref[...]

    rev = (lax.broadcasted_iota(jnp.int32, (tp, tp), 0) + lax.broadcasted_iota(jnp.int32, (tp, tp), 1)
           == tp - 1).astype(BF16)
    def pair_rows(u_ref, b, newer_first):
        il_ref[...] = u_ref[b].astype(F32)
        even = il_ref[pl.ds(0, tp, stride=2), :].astype(BF16)
        odd = il_ref[pl.ds(1, tp, stride=2), :].astype(BF16)
        return jnp.concatenate([odd, even] if newer_first else [even, odd], axis=1)

    lhs = []
    for d in range(2):
        if d == 0:
            u = jnp.concatenate([pair_rows(uf_ref, b, False) for b in range(nb)], axis=0)
        else:
            u = jnp.concatenate(
                [jnp.dot(rev, pair_rows(ub_ref, b, True), preferred_element_type=F32).astype(BF16)
                 for b in range(nb)], axis=0)
        lhs.append(u)
        bu = jnp.dot(u, b_ref[d], preferred_element_type=F32)
        for b in range(nb):
            r0 = (d * nb + b) * pitch
            for k in range(2 * nk):
                buf_ref[k, r0:r0 + tp, :] = bu[b * tp:(b + 1) * tp, k * LANE:(k + 1) * LANE]
    ar = [lre_ref[:, k * LANE:(k + 1) * LANE] for k in range(nk)]
    ai = [lim_ref[:, k * LANE:(k + 1) * LANE] for k in range(nk)]

    def step(s, carry):
        rows = pl.ds(s, q, stride=pitch)
        out = []
        for k in range(nk):
            hr, hi = carry[k]
            nr = ar[k] * hr - ai[k] * hi + buf_ref[k, rows, :]
            ni = ar[k] * hi + ai[k] * hr + buf_ref[nk + k, rows, :]
            buf_ref[k, rows, :] = hr
            buf_ref[nk + k, rows, :] = hi
            out.append((nr, ni))
        return tuple(out)

    init = tuple((h_ref[0, :, k * LANE:(k + 1) * LANE], h_ref[1, :, k * LANE:(k + 1) * LANE]) for k in range(nk))
    fin = lax.fori_loop(0, tp, step, init, unroll=8)
    for k in range(nk):
        h_ref[0, :, k * LANE:(k + 1) * LANE] = fin[k][0]
        h_ref[1, :, k * LANE:(k + 1) * LANE] = fin[k][1]

    if need_y:
        for d in range(2):
            h = jnp.concatenate(
                [jnp.concatenate([buf_ref[k, (d * nb + b) * pitch:(d * nb + b) * pitch + tp, :].astype(BF16)
                                  for k in range(2 * nk)], axis=1) for b in range(nb)], axis=0)
            y = (jnp.dot(h, c_ref[d], preferred_element_type=F32)
                 + jnp.dot(lhs[d], d_ref[d], preferred_element_type=F32))
            for b in range(nb):
                yb = y[b * tp:(b + 1) * tp]
                if d == 1:
                    yb = jnp.dot(rev, yb.astype(BF16), preferred_element_type=F32)
                first, second = (0, 1) if d == 0 else (1, 0)
                il_ref[pl.ds(first, tp, stride=2), :] = yb[:, 0:LANE]
                il_ref[pl.ds(second, tp, stride=2), :] = yb[:, LANE:2 * LANE]
                (yf_ref if d == 0 else yb_ref)[b] = il_ref[...].astype(BF16)


def _s5_call(u, bmat, cmat, dmat, lre, lim, h0, need_y):
    nb, l, w5 = u.shape
    nj = w5 // LANE
    q = 2 * nb
    t = min(S5_CHUNK, l)
    tp = t // 2
    p8 = lre.shape[-1]
    nc = l // t
    hspec = pl.BlockSpec((None, 2, q, p8), lambda j, c: (j, 0, 0, 0))
    ublk = (nb, t, LANE)
    in_specs = [pl.BlockSpec(ublk, lambda j, c: (0, c, j)), pl.BlockSpec(ublk, lambda j, c: (0, nc - 1 - c, j)),
                pl.BlockSpec((None, 2, 2 * LANE, 2 * p8), lambda j, c: (j, 0, 0, 0)),
                pl.BlockSpec((None, 2, 2 * p8, 2 * LANE), lambda j, c: (j, 0, 0, 0)),
                pl.BlockSpec((None, 2, 2 * LANE, 2 * LANE), lambda j, c: (j, 0, 0, 0)),
                pl.BlockSpec((None, q, p8), lambda j, c: (j, 0, 0)),
                pl.BlockSpec((None, q, p8), lambda j, c: (j, 0, 0)),
                hspec]
    out_specs = [hspec]
    out_shape = [jax.ShapeDtypeStruct((nj, 2, q, p8), F32)]
    scratch = [pltpu.VMEM((2 * p8 // LANE, q * (tp + S5_ROW_PAD), LANE), F32), pltpu.VMEM((t, LANE), F32)]
    if need_y:
        out_specs = [pl.BlockSpec(ublk, lambda j, c: (0, c, j)),
                     pl.BlockSpec(ublk, lambda j, c: (0, nc - 1 - c, j))] + out_specs
        out_shape = [jax.ShapeDtypeStruct((nb, l, w5), BF16)] * 2 + out_shape
    kern = functools.partial(_s5_kernel, nb=nb, tp=tp, p8=p8, need_y=need_y)
    return pl.pallas_call(
        kern, grid=(nj, nc), in_specs=in_specs, out_specs=out_specs, out_shape=out_shape,
        scratch_shapes=scratch, compiler_params=_cp(("parallel", "arbitrary"), 48),
        name="s5_scan" if need_y else "s5_scan_ctx")(u, u, bmat, cmat, dmat, lre, lim, h0)


def _s5_params(lam_re, lam_im, log_dt, b_re, b_im, c_re, c_im, nb):
    _, g, p = lam_re.shape
    s = b_re.shape[-1]
    gpb = LANE // s
    nj = g // gpb
    lam = lax.complex(lam_re.astype(F32), lam_im.astype(F32))
    lam_bar = jnp.exp(lam * jnp.exp(log_dt.astype(F32))[..., None])
    b_bar = ((lam_bar - 1.0) / lam)[..., None] * lax.complex(b_re.astype(F32), b_im.astype(F32))
    def quadrants(parts):
        r, c = parts[0].shape[-2:]
        same_group = (jnp.arange(gpb * r)[:, None] // r) == (jnp.arange(gpb * c)[None, :] // c)

        def spread(x):
            return jnp.where(same_group, jnp.tile(x.reshape(2, nj, gpb * r, c), (1, 1, 1, gpb)), 0.0)

        top = jnp.concatenate([spread(parts[0]), spread(parts[1])], axis=-1)
        bot = jnp.concatenate([spread(parts[2]), spread(parts[3])], axis=-1)
        return jnp.concatenate([top, bot], axis=-2).transpose(1, 0, 2, 3).astype(BF16)

    tb = lambda z: jnp.swapaxes(z, -1, -2)
    lb = lam_bar[..., None] * b_bar
    bmat = quadrants([tb(lb.real), tb(lb.imag), tb(b_bar.real), tb(b_bar.imag)])
    cc = lax.complex(c_re.astype(F32), c_im.astype(F32))
    c_l1 = cc * lam_bar[:, :, None, :]
    c_l2 = c_l1 * lam_bar[:, :, None, :]
    cmat = quadrants([tb(c_l1.real), tb(c_l2.real), -tb(c_l1.imag), -tb(c_l2.imag)])
    m0 = jnp.einsum('dgsp,dgpt->dgts', cc, b_bar).real
    m1 = jnp.einsum('dgsp,dgpt->dgts', c_l1, b_bar).real
    dmat = quadrants([m0, m1, jnp.zeros_like(m0), m0])

    def lam_of(part):
        v = part.reshape(2, nj, gpb * p).transpose(1, 0, 2)
        return jnp.repeat(v, nb, axis=1)

    lam2 = lam_bar * lam_bar
    return bmat, cmat, dmat, lam_of(lam2.real), lam_of(lam2.imag)


def _conv_silu(x, w, bias):
    l = x.shape[0]
    rows = lax.broadcasted_iota(jnp.int32, x.shape, 0)
    half = SSD_CONV // 2
    acc = x * w[half:half + 1, :] + bias
    for k in range(SSD_CONV):
        if k == half:
            continue
        off = k - half
        xs = pltpu.roll(x, shift=(-off) % l, axis=0)
        valid = jnp.logical_and(rows + off >= 0, rows + off < l)
        acc = acc + jnp.where(valid, xs, 0.0) * w[k:k + 1, :]
    return _silu(acc)


def _conv_kernel(x_ref, w_ref, b_ref, o_ref):
    o_ref[...] = _conv_silu(x_ref[...].astype(F32), w_ref[...], b_ref[...]).astype(o_ref.dtype)


def _proj_conv_kernel(a_ref, w_ref, cw_ref, cb_ref, o_ref):
    a = a_ref[...]
    wb = w_ref[...].astype(BF16)
    hw = wb.shape[1] // 2
    for c in range(2):
        cols = slice(c * hw, (c + 1) * hw)
        p = jnp.dot(a, wb[:, cols], preferred_element_type=F32)
        o_ref[:, cols] = _conv_silu(p, cw_ref[:, cols], cb_ref[:, cols]).astype(o_ref.dtype)


def _proj_conv_call(hn, w_all, col0, cw, cb):
    b, l, d = hn.shape
    c = cw.shape[1]
    tn = 512
    return pl.pallas_call(
        _proj_conv_kernel, grid=(b, c // tn),
        in_specs=[pl.BlockSpec((None, l, d), lambda i, j: (i, 0, 0), pipeline_mode=pl.Buffered(1)),
                  pl.BlockSpec((pl.Element(d), pl.Element(tn)),
                               lambda i, j: (0, (col0 // LANE + j * (tn // LANE)) * LANE)),
                  pl.BlockSpec((SSD_CONV, tn), lambda i, j: (0, j)),
                  pl.BlockSpec((1, tn), lambda i, j: (0, j))],
        out_specs=pl.BlockSpec((None, l, tn), lambda i, j: (i, 0, j)),
        out_shape=jax.ShapeDtypeStruct((b, l, c), BF16),
        compiler_params=_cp(("parallel", "arbitrary"), 60), name="in_xbc_conv")(hn, w_all, cw, cb.reshape(1, c))


def _conv_call(proj, w, bias, col0):
    b, l, _ = proj.shape
    c = w.shape[1]
    tc = 256
    off = col0 // tc
    return pl.pallas_call(
        _conv_kernel, grid=(b, c // tc),
        in_specs=[pl.BlockSpec((None, l, tc), lambda i, j: (i, 0, j + off)),
                  pl.BlockSpec((SSD_CONV, tc), lambda i, j: (0, j)),
                  pl.BlockSpec((1, tc), lambda i, j: (0, j))],
        out_specs=pl.BlockSpec((None, l, tc), lambda i, j: (i, 0, j)),
        out_shape=jax.ShapeDtypeStruct((b, l, c), BF16),
        compiler_params=_cp(("parallel", "parallel"), 40), name="ssd_conv")(proj, w, bias.reshape(1, c))


def _softplus(x):
    return jnp.maximum(x, 0.0) + jnp.log1p(jnp.exp(-jnp.abs(x)))


def _ssd_dir(xbc_ref, dt_ref, y_ref, h_ref, bias, a_neg, d, *, nh, ng, need_y):
    qn = SSD_CHUNK
    hd = SSD_HEAD_DIM
    w = nh * hd
    gw = w // ng
    gn = ng * SSD_STATE
    ii = lax.broadcasted_iota(jnp.int32, (qn, qn), 0)
    jj = lax.broadcasted_iota(jnp.int32, (qn, qn), 1)
    mask = (jj <= ii) if d == 0 else (jj >= ii)
    lmat = mask.astype(F32)
    dtv = _softplus(dt_ref[...] + bias)
    cum = jnp.dot(lmat, dtv * a_neg, precision=lax.Precision.HIGHEST, preferred_element_type=F32)
    cum_t = cum.T
    edge = qn - 1 if d == 0 else 0
    tot = cum[edge:edge + 1, :]
    dt_t = dtv.T
    wt_t = dt_t * jnp.exp(cum_t[:, edge:edge + 1] - cum_t)
    decay = jnp.exp(tot)
    lane = lax.broadcasted_iota(jnp.int32, (qn, LANE), 1)
    left = lane < hd
    zero = jnp.zeros((), BF16)
    for g in range(ng):
        bg = xbc_ref[:, w + g * SSD_STATE:w + (g + 1) * SSD_STATE]
        cg = xbc_ref[:, w + gn + g * SSD_STATE:w + gn + (g + 1) * SSD_STATE]
        bg_t = bg.astype(F32).T
        s_in = h_ref[d, :, g * gw:(g + 1) * gw]
        if need_y:
            cb = lax.dot_general(cg, bg, (((1,), (1,)), ((), ())), preferred_element_type=F32)
            yoff = jnp.dot(cg, s_in.astype(BF16), preferred_element_type=F32)
        for pr in range(gw // LANE):
            c0 = d * nh + (g * gw) // hd + 2 * pr
            col = g * gw + pr * LANE
            xp = xbc_ref[:, col:col + LANE]
            r = jnp.concatenate([jnp.where(left, xp, zero), jnp.where(left, zero, xp)], axis=0)
            tops, bots, cols = [], [], []
            for c in (c0, c0 + 1):
                bots.append((bg_t * wt_t[c:c + 1, :]).astype(BF16))
                if need_y:
                    ccol = jnp.broadcast_to(cum[:, c:c + 1], (qn, qn))
                    seg = jnp.where(mask, jnp.exp(ccol - cum_t[c:c + 1, :]), 0.0)
                    tops.append((cb * seg * dt_t[c:c + 1, :]).astype(BF16))
                    cols.append(ccol)
            dec = jnp.where(left[0:1], decay[:, c0:c0 + 1], decay[:, c0 + 1:c0 + 2])
            s_old = s_in[:, pr * LANE:(pr + 1) * LANE]
            if need_y:
                lhs = jnp.concatenate([jnp.concatenate(tops, axis=1), jnp.concatenate(bots, axis=1)], axis=0)
                out = jnp.dot(lhs, r, preferred_element_type=F32)
                ec = jnp.exp(jnp.where(left, cols[0], cols[1]))
                y_ref[:, col:col + LANE] = (out[:qn] + yoff[:, pr * LANE:(pr + 1) * LANE] * ec).astype(y_ref.dtype)
                s_new = out[qn:]
            else:
                s_new = jnp.dot(jnp.concatenate(bots, axis=1), r, preferred_element_type=F32)
            h_ref[d, :, col:col + LANE] = s_old * dec + s_new


def _ssd_kernel(*refs, nh, ng, need_y):
    if need_y:
        xf_ref, xb_ref, dtf_ref, dtb_ref, h0_ref, bias_ref, alog_ref, yf_ref, yb_ref, h_ref = refs
    else:
        xf_ref, xb_ref, dtf_ref, dtb_ref, h0_ref, bias_ref, alog_ref, h_ref = refs
        yf_ref = yb_ref = None

    @pl.when(pl.program_id(1) == 0)
    def _():
        h_ref[...] = h0_ref[...]

    bias = bias_ref[...]
    a_neg = -jnp.exp(alog_ref[...])
    _ssd_dir(xf_ref, dtf_ref, yf_ref, h_ref, bias, a_neg, 0, nh=nh, ng=ng, need_y=need_y)
    _ssd_dir(xb_ref, dtb_ref, yb_ref, h_ref, bias, a_neg, 1, nh=nh, ng=ng, need_y=need_y)


def _ssd_call(xbc, dt, h0, bias, alog, nh, ng, need_y):
    b, l, cd = xbc.shape
    w = nh * SSD_HEAD_DIM
    nc = l // SSD_CHUNK
    q = SSD_CHUNK
    fwd = lambda i, s: (i, s, 0)
    bwd = lambda i, s: (i, nc - 1 - s, 0)
    hspec = pl.BlockSpec((None, 2, SSD_STATE, w), lambda i, s: (i, 0, 0, 0))
    in_specs = [pl.BlockSpec((None, q, cd), fwd), pl.BlockSpec((None, q, cd), bwd),
                pl.BlockSpec((None, q, LANE), fwd), pl.BlockSpec((None, q, LANE), bwd),
                hspec,
                pl.BlockSpec((1, LANE), lambda i, s: (0, 0)), pl.BlockSpec((1, LANE), lambda i, s: (0, 0))]
    out_specs = [hspec]
    out_shape = [jax.ShapeDtypeStruct((b, 2, SSD_STATE, w), F32)]
    if need_y:
        out_specs = [pl.BlockSpec((None, q, w), fwd), pl.BlockSpec((None, q, w), bwd)] + out_specs
        out_shape = [jax.ShapeDtypeStruct((b, l, w), BF16)] * 2 + out_shape
    return pl.pallas_call(
        functools.partial(_ssd_kernel, nh=nh, ng=ng, need_y=need_y), grid=(b, nc),
        in_specs=in_specs, out_specs=out_specs, out_shape=out_shape,
        compiler_params=_cp(("parallel", "arbitrary"), 48),
        name="ssd_scan" if need_y else "ssd_scan_ctx")(xbc, xbc, dt, dt, h0, bias, alog)


def _gnorm_kernel(yf_ref, yb_ref, xs_ref, z_ref, d_ref, g_ref, o_ref, t_ref, *, rows, tb, gw):
    y = d_ref[...] * xs_ref[...].astype(F32) + yf_ref[...].astype(F32) + yb_ref[...].astype(F32)
    y = y * _silu(z_ref[...].astype(F32))
    bw = y.shape[1]
    parts = []
    for g in range(bw // gw):
        yg = y[:, g * gw:(g + 1) * gw]
        parts.append(yg * lax.rsqrt(jnp.mean(yg * yg, axis=-1, keepdims=True) + EPS))
    res = jnp.concatenate(parts, axis=1) * g_ref[...]
    pitch = rows + XPOSE_PAD
    for wl in range(tb):
        for kk in range(bw // LANE):
            t_ref[kk, wl * pitch:wl * pitch + rows, :] = res[wl * rows:(wl + 1) * rows, kk * LANE:(kk + 1) * LANE]
    for r in range(rows):
        o_ref[r] = jnp.concatenate(
            [t_ref[kk, pl.ds(r, tb, stride=pitch), :] for kk in range(bw // LANE)], axis=1).astype(o_ref.dtype)


def _gnorm_call(yf, yb, xbc, proj, dvec, gvec, ng):
    b, l, w = yf.shape
    rows = l // GRID_W
    tb = XPOSE_TILE
    gw = w // ng
    bw = gw * (2 if ng % 2 == 0 else 1)
    blk = pl.BlockSpec((None, tb * rows, bw), lambda i, s, g: (i, s, g))
    vec = pl.BlockSpec((1, bw), lambda i, s, g: (0, g))
    out = pl.pallas_call(
        functools.partial(_gnorm_kernel, rows=rows, tb=tb, gw=gw), grid=(b, GRID_W // tb, w // bw),
        in_specs=[blk, blk, blk, blk, vec, vec],
        out_specs=pl.BlockSpec((None, rows, tb, bw), lambda i, s, g: (i, 0, s, g)),
        out_shape=jax.ShapeDtypeStruct((b, rows, GRID_W, w), BF16),
        scratch_shapes=[pltpu.VMEM((bw // LANE, tb * (rows + XPOSE_PAD), LANE), F32)],
        compiler_params=_cp(("parallel", "parallel", "parallel"), 40),
        name="ssd_gnorm")(yf, yb, xbc, proj, dvec, gvec)
    return out.reshape(b * l, w)


def _moe_kernel(e_ref, b0_ref, nb_ref, *refs, nsub, nh):
    xs = refs[:nsub]
    wg_ref, wu_ref, wd_ref, o_ref, act_ref, wdb_ref = refs[nsub:]
    s = pl.program_id(0)
    t = pl.program_id(1)
    nb = nb_ref[s]

    @pl.when(jnp.logical_and(t < nh, nb > 0))
    def _():
        wg = wg_ref[...].astype(BF16)
        wu = wu_ref[...].astype(BF16)

        def act_block(k):
            x = xs[k][...]
            gt = jnp.dot(x, wg, preferred_element_type=F32)
            up = jnp.dot(x, wu, preferred_element_type=F32)
            act_ref[k, t] = (_silu(gt) * up).astype(BF16)

        act_block(0)
        for k in range(1, nsub):
            pl.when(k < nb)(functools.partial(act_block, k))

    @pl.when(jnp.logical_and(t >= nh, nb > 0))
    def _():
        wdb = wd_ref[...].astype(BF16)
        wdb_ref[...] = wdb

        def down_block(k, w):
            a = jnp.concatenate([act_ref[k, h] for h in range(nh)], axis=1)
            o_ref[k * MOE_BLOCK:(k + 1) * MOE_BLOCK, :] = jnp.dot(a, w, preferred_element_type=F32).astype(o_ref.dtype)

        down_block(0, wdb)
        for k in range(1, nsub):
            pl.when(k < nb)(lambda k=k: down_block(k, wdb_ref[...]))

            @pl.when(k >= nb)
            def _():
                o_ref[k * MOE_BLOCK:(k + 1) * MOE_BLOCK, :] = jnp.zeros((MOE_BLOCK, o_ref.shape[1]), o_ref.dtype)

    @pl.when(jnp.logical_and(t >= nh, nb == 0))
    def _():
        o_ref[...] = jnp.zeros(o_ref.shape, o_ref.dtype)


def _moe_call(sup_e, sup_b0, sup_nb, xs, w_gate, w_up, w_down):
    n_slots, d = xs.shape
    hid = w_gate.shape[-1]
    th = min(256, hid)
    tn = min(1024, d)
    nh = hid // th
    nt = d // tn
    ns = sup_e.shape[0]
    rsup = MOE_SUB * MOE_BLOCK

    def x_map(k):
        def index(s, t, e, b0, nb):
            sx = jnp.where(t < nh, s, jnp.minimum(s + 1, ns - 1))
            return (b0[sx * MOE_SUB + k], 0)
        return index

    def hid_idx(s, t, nb):
        return jnp.where(nb[s] > 0, jnp.minimum(t, nh - 1), nh - 1)

    def col_idx(s, t, nb):
        return jnp.where(nb[s] > 0, jnp.clip(t - nh, 0, nt - 1), nt - 1)

    gs = pltpu.PrefetchScalarGridSpec(
        num_scalar_prefetch=3, grid=(ns, nh + nt),
        in_specs=[pl.BlockSpec((MOE_BLOCK, d), x_map(k)) for k in range(MOE_SUB)] + [
            pl.BlockSpec((None, d, th), lambda s, t, e, b0, nb: (e[s], 0, hid_idx(s, t, nb))),
            pl.BlockSpec((None, d, th), lambda s, t, e, b0, nb: (e[s], 0, hid_idx(s, t, nb))),
            pl.BlockSpec((None, hid, tn), lambda s, t, e, b0, nb: (e[s], 0, col_idx(s, t, nb)))],
        out_specs=pl.BlockSpec((rsup, tn), lambda s, t, e, b0, nb: (s, jnp.clip(t - nh, 0, nt - 1))),
        scratch_shapes=[pltpu.VMEM((MOE_SUB, nh, MOE_BLOCK, th), BF16), pltpu.VMEM((hid, tn), BF16)])
    return pl.pallas_call(
        functools.partial(_moe_kernel, nsub=MOE_SUB, nh=nh), grid_spec=gs,
        out_shape=jax.ShapeDtypeStruct((ns * rsup, d), BF16),
        compiler_params=_cp(("arbitrary", "arbitrary"), 60), name="moe_experts")(
            sup_e, sup_b0, sup_nb, *([xs] * MOE_SUB), w_gate, w_up, w_down)


def _final_kernel(x_ref, ya_ref, yb_ref, w_ref, g2_ref, fg_ref, o_ref):
    wts = w_ref[...]
    moe = ya_ref[...].astype(F32) * wts[:, 0:1] + yb_ref[...].astype(F32) * wts[:, 1:2]
    x = x_ref[...] + g2_ref[...] * moe
    ms = jnp.mean(x * x, axis=-1, keepdims=True)
    o_ref[...] = x * lax.rsqrt(ms + EPS) * fg_ref[...]


def _final_call(x, y2, wts, mods, ig, fg):
    b, l, d = x.shape
    tr = min(256, l)
    nt = l // tr
    row = lambda i, r: (i * nt + r, 0)
    row2 = lambda i, r: (b * nt + i * nt + r, 0)
    ya = yb = y2
    return pl.pallas_call(
        _final_kernel, grid=(b, nt),
        in_specs=[pl.BlockSpec((None, tr, d), lambda i, r: (i, r, 0)),
                  pl.BlockSpec((tr, d), row), pl.BlockSpec((tr, d), row2),
                  pl.BlockSpec((tr, MOE_TOP_K), row),
                  pl.BlockSpec((None, None, 1, d), lambda i, r: (i, ig, 0, 0)),
                  pl.BlockSpec((1, d), lambda i, r: (0, 0))],
        out_specs=pl.BlockSpec((None, tr, d), lambda i, r: (i, r, 0)),
        out_shape=jax.ShapeDtypeStruct((b, l, d), F32),
        compiler_params=_cp(("parallel", "parallel"), 40), name="combine_final")(x, ya, yb, wts, mods, fg)


def _route(sel, ne):
    m = sel.shape[0]
    weights = sel[:, MOE_TOP_K:2 * MOE_TOP_K]
    n_assign = m * MOE_TOP_K
    flat_e = jnp.concatenate([sel[:, k] for k in range(MOE_TOP_K)], axis=0).astype(jnp.int32)
    onehot = (flat_e[:, None] == jnp.arange(ne, dtype=flat_e.dtype)[None, :]).astype(jnp.int32)
    csum = jnp.cumsum(onehot, axis=0)
    counts = csum[-1]
    local = jnp.sum(csum * onehot, axis=1) - 1
    nblk_e = (counts + MOE_BLOCK - 1) // MOE_BLOCK
    padded = nblk_e * MOE_BLOCK
    pad_end = jnp.cumsum(padded)
    pad_start = pad_end - padded
    dest = pad_start[flat_e] + local
    n_blocks = -(-(n_assign + ne * (MOE_BLOCK - 1)) // MOE_BLOCK)
    tok = jnp.arange(n_assign, dtype=jnp.int32) % m
    n_slots = n_blocks * MOE_BLOCK
    slot_tok = (jnp.arange(n_slots, dtype=jnp.int32) % m).at[dest].set(tok)
    rsup = MOE_SUB * MOE_BLOCK
    ns_e = (nblk_e + MOE_SUB - 1) // MOE_SUB
    sup_end = jnp.cumsum(ns_e)
    sup_start = sup_end - ns_e
    n_sup = (n_blocks + (MOE_SUB - 1) * ne) // MOE_SUB
    sidx = jnp.arange(n_sup, dtype=jnp.int32)
    last = sup_end[-1] - 1
    s_eff = jnp.minimum(sidx, last)
    e_s = jnp.minimum(jnp.searchsorted(sup_end, s_eff, side='right'), ne - 1).astype(jnp.int32)
    k_s = s_eff - sup_start[e_s]
    b0_s = pad_start[e_s] // MOE_BLOCK + MOE_SUB * k_s
    nb_s = jnp.clip(nblk_e[e_s] - MOE_SUB * k_s, 0, MOE_SUB)
    used = sidx <= last
    sup_nb = jnp.where(used, nb_s, 0).astype(jnp.int32)
    kk = jnp.arange(MOE_SUB, dtype=jnp.int32)[None, :]
    sup_b0 = jnp.maximum(lax.cummax(jnp.where(kk < sup_nb[:, None], b0_s[:, None] + kk, -1), axis=0), 0)
    sup_b0 = sup_b0.reshape(-1).astype(jnp.int32)
    pos = ((sup_start[flat_e] + local // rsup) * rsup + local % rsup).astype(jnp.int32)
    return weights, slot_tok, e_s, sup_b0, sup_nb, pos


def kernel(x, c, ctx, c_ctx, w_mod, b_mod, norm1_g, w_in, s5_lam_re, s5_lam_im, s5_log_dt, s5_b_re, s5_b_im, s5_c_re, s5_c_im, s5_d, s5_w_val, s5_w_gate, ssd_conv_w, ssd_conv_b, ssd_a_log, ssd_dt_bias, ssd_d, ssd_norm_g, ssd_w_out, w_o, norm2_g, moe_w_group, moe_b_group, moe_w_expert, moe_b_expert, moe_w_gate, moe_w_up, moe_w_down, final_g):
    depth = w_mod.shape[0]
    assert depth == 1, "single-layer block"
    bsz, n_lat, d = x.shape
    l_ctx = ctx.shape[1]
    w5 = s5_d.shape[1]
    nh = ssd_d.shape[1]
    w = nh * SSD_HEAD_DIM
    conv_dim = ssd_conv_w.shape[2]
    ng = (conv_dim - w) // (2 * SSD_STATE)
    ssd_in = w + conv_dim + 2 * nh
    o1, o2 = w5, w5 + ssd_in
    l = 0

    cc = jnp.concatenate([c, c_ctx[None, :]], axis=0)
    cc = jnp.pad(cc, ((0, (-cc.shape[0]) % 8), (0, 0)))
    mods = _mod_call(cc, w_mod[l], b_mod[l]).reshape(cc.shape[0], 6, 1, d)
    i_sh1, i_sc1, i_g1, i_sh2, i_sc2, i_g2 = range(6)

    w_in_l = w_in[l]
    o_dt = o1 + w + conv_dim

    hn_rm, hn_cm = _norm_lat_call(x, norm1_g[l], mods, i_sc1, i_sh1)
    hc = _norm_ctx_call(ctx, norm1_g[l], mods, bsz, i_sc1, i_sh1)
    hn_rm = hn_rm.reshape(bsz * n_lat, d)
    hn_cm = hn_cm.reshape(bsz * n_lat, d)
    hc = hc.reshape(bsz * l_ctx, d)

    u_lat = _mm_call(hn_rm, w_in_l, BF16, "in_s5", 0, o1, tm_max=2048).reshape(bsz, n_lat, w5)
    u_ctx = _mm_call(hc, w_in_l, BF16, "in_s5_ctx", 0, o1).reshape(bsz, l_ctx, w5)
    gates = _mm_call(hn_rm, w_in_l, BF16, "in_gates", o2, 2 * d, tm_max=2048)
    z_lat = _mm_call(hn_cm, w_in_l, BF16, "in_z", o1, w, tm_max=2048).reshape(bsz, n_lat, w)
    xbc_lat = _proj_conv_call(hn_cm.reshape(bsz, n_lat, d), w_in_l, o1 + w, ssd_conv_w[l], ssd_conv_b[l])
    p_ctx = _mm_call(hc, w_in_l, BF16, "in_xbc_ctx", o1 + w, conv_dim).reshape(bsz, l_ctx, conv_dim)
    dt_lat = _mm_call(hn_cm, w_in_l, F32, "in_dt", o_dt, 2 * nh).reshape(bsz, n_lat, LANE)
    dt_ctx = _mm_call(hc, w_in_l, F32, "in_dt_ctx", o_dt, 2 * nh).reshape(bsz, l_ctx, LANE)

    nj = w5 // LANE
    bmat, cmat, dmat, lre, lim = _s5_params(s5_lam_re[l], s5_lam_im[l], s5_log_dt[l], s5_b_re[l], s5_b_im[l],
                                            s5_c_re[l], s5_c_im[l], bsz)
    s5_zero = jnp.zeros((nj, 2, 2 * bsz, lre.shape[-1]), F32)
    (s5_ctx,) = _s5_call(u_ctx, bmat, cmat, dmat, lre, lim, s5_zero, False)
    ya_f, ya_b, _ = _s5_call(u_lat, bmat, cmat, dmat, lre, lim, s5_ctx, True)

    xbc_ctx = _conv_call(p_ctx, ssd_conv_w[l], ssd_conv_b[l], 0)
    pad_h = LANE - 2 * nh
    bias = jnp.pad(ssd_dt_bias[l].astype(F32).reshape(1, 2 * nh), ((0, 0), (0, pad_h)))
    alog = jnp.pad(ssd_a_log[l].astype(F32).reshape(1, 2 * nh), ((0, 0), (0, pad_h)))
    h_zero = jnp.zeros((bsz, 2, SSD_STATE, w), F32)
    (h_ctx,) = _ssd_call(xbc_ctx, dt_ctx, h_zero, bias, alog, nh, ng, False)
    y_f, y_b, _ = _ssd_call(xbc_lat, dt_lat, h_ctx, bias, alog, nh, ng, True)
    d_vec = jnp.repeat(ssd_d[l].astype(F32), SSD_HEAD_DIM).reshape(1, w)
    y_ssd = _gnorm_call(y_f, y_b, xbc_lat, z_lat, d_vec, ssd_norm_g[l].astype(F32).reshape(1, w), ng)

    m_lat = bsz * n_lat
    part_a = _glu_call(u_lat.reshape(m_lat, w5), ya_f.reshape(m_lat, w5), ya_b.reshape(m_lat, w5),
                       s5_d[l].astype(F32).reshape(1, w5),
                       s5_w_val[l].astype(BF16), s5_w_gate[l].astype(BF16), gates)
    merged = _merge_call(y_ssd, ssd_w_out[l], gates, part_a)
    x1 = _resid_call(merged, w_o[l], x, mods, i_g1)

    ngr = moe_w_group.shape[-1]
    ne = moe_w_expert.shape[-1]
    wr = jnp.concatenate([moe_w_group[l], moe_w_expert[l]], axis=1).astype(F32)
    wr = jnp.pad(wr, ((0, 0), (0, (-(ngr + ne)) % LANE)))
    br = jnp.concatenate([moe_b_group[l], moe_b_expert[l]]).astype(F32)
    br = jnp.pad(br, (0, (-(ngr + ne)) % LANE)).reshape(1, -1)
    hx, sel = _norm_router_call(x1, norm2_g[l], mods, i_sc2, i_sh2, wr, br, ngr, ne)
    m = bsz * n_lat
    hx = hx.reshape(m, d)
    weights, slot_tok, sup_e, sup_b0, sup_nb, pos = _route(sel.reshape(m, -1), ne)
    xs = hx[slot_tok]
    y_slots = _moe_call(sup_e, sup_b0, sup_nb, xs, moe_w_gate[l], moe_w_up[l], moe_w_down[l])
    y2 = y_slots[pos]
    return _final_call(x1, y2, weights.astype(F32), mods, i_g2, final_g.reshape(1, d))
```

```python
import functools
import math

import jax
import jax.numpy as jnp
from jax import lax
from jax.experimental import pallas as pl
from jax.experimental.pallas import tpu as pltpu

F32 = jnp.float32
BF16 = jnp.bfloat16

GRID_W = 64
EPS = 1e-6
LANE = 128
SSD_HEAD_DIM = 64
SSD_STATE = 128
SSD_CHUNK = 128
SSD_CONV = 5
MOE_TOP_K = 2
MOE_BLOCK = 256
MOE_SUB = 4
XPOSE_TILE = 16
XPOSE_PAD = 8
S5_CHUNK = 512
S5_ROW_PAD = 8


def _cp(sem, mb):
    return pltpu.CompilerParams(dimension_semantics=sem, vmem_limit_bytes=mb * 1024 * 1024)


def _sigmoid(x):
    return 1.0 / (1.0 + jnp.exp(-x))


def _silu(x):
    return x * _sigmoid(x)


def _mod_kernel(c_ref, w_ref, b_ref, o_ref):
    s = _silu(c_ref[...])
    o_ref[...] = jnp.dot(s.astype(BF16), w_ref[...].astype(BF16),
                         preferred_element_type=F32) + b_ref[...]


def _mod_call(cc, w, b):
    r, d = cc.shape
    n = w.shape[1]
    tn = min(1024, n)
    return pl.pallas_call(
        _mod_kernel, grid=(n // tn,),
        in_specs=[pl.BlockSpec((r, d), lambda j: (0, 0)),
                  pl.BlockSpec((d, tn), lambda j: (0, j)),
                  pl.BlockSpec((1, tn), lambda j: (0, j))],
        out_specs=pl.BlockSpec((r, tn), lambda j: (0, j)),
        out_shape=jax.ShapeDtypeStruct((r, n), F32),
        compiler_params=_cp(("parallel",), 56), name="mod")(cc, w, b.reshape(1, n))


def _rmsmod(x, g, sc, sh):
    ms = jnp.mean(x * x, axis=-1, keepdims=True)
    return (x * lax.rsqrt(ms + EPS) * g) * (1.0 + sc) + sh


def _norm_lat_kernel(x_ref, g_ref, sc_ref, sh_ref, orm_ref, ocm_ref, t_ref, *, tb, d):
    pitch = tb + XPOSE_PAD
    for r in range(tb):
        y = _rmsmod(x_ref[r], g_ref[...], sc_ref[...], sh_ref[...])
        orm_ref[r] = y.astype(BF16)
        for kk in range(d // LANE):
            t_ref[kk, r * pitch:r * pitch + tb, :] = y[:, kk * LANE:(kk + 1) * LANE]
    for wl in range(tb):
        ocm_ref[wl] = jnp.concatenate(
            [t_ref[kk, pl.ds(wl, tb, stride=pitch), :] for kk in range(d // LANE)], axis=1).astype(BF16)


def _norm_lat_call(x, g, mods, isc, ish):
    b, l, d = x.shape
    rows = l // GRID_W
    tb = XPOSE_TILE
    xv = x.reshape(b, rows, GRID_W, d)
    orm, ocm = pl.pallas_call(
        functools.partial(_norm_lat_kernel, tb=tb, d=d), grid=(b, rows // tb, GRID_W // tb),
        in_specs=[pl.BlockSpec((None, tb, tb, d), lambda i, r, c: (i, r, c, 0)),
                  pl.BlockSpec((1, d), lambda i, r, c: (0, 0)),
                  pl.BlockSpec((None, None, 1, d), lambda i, r, c: (i, isc, 0, 0)),
                  pl.BlockSpec((None, None, 1, d), lambda i, r, c: (i, ish, 0, 0))],
        out_specs=[pl.BlockSpec((None, tb, tb, d), lambda i, r, c: (i, r, c, 0)),
                   pl.BlockSpec((None, tb, tb, d), lambda i, r, c: (i, c, r, 0))],
        out_shape=[jax.ShapeDtypeStruct((b, rows, GRID_W, d), BF16),
                   jax.ShapeDtypeStruct((b, GRID_W, rows, d), BF16)],
        scratch_shapes=[pltpu.VMEM((d // LANE, tb * (tb + XPOSE_PAD), LANE), F32)],
        compiler_params=_cp(("parallel", "parallel", "parallel"), 48),
        name="norm1_lat")(xv, g.reshape(1, d), mods, mods)
    return orm.reshape(b, l, d), ocm.reshape(b, l, d)


def _norm_kernel(x_ref, g_ref, sc_ref, sh_ref, o_ref):
    o_ref[...] = _rmsmod(x_ref[...], g_ref[...], sc_ref[...], sh_ref[...]).astype(o_ref.dtype)


def _norm_ctx_call(x, g, mods, row, isc, ish):
    b, l, d = x.shape
    tr = min(256, l)
    return pl.pallas_call(
        _norm_kernel, grid=(b, l // tr),
        in_specs=[pl.BlockSpec((None, tr, d), lambda i, r: (i, r, 0)),
                  pl.BlockSpec((1, d), lambda i, r: (0, 0)),
                  pl.BlockSpec((None, None, 1, d), lambda i, r: (row, isc, 0, 0)),
                  pl.BlockSpec((None, None, 1, d), lambda i, r: (row, ish, 0, 0))],
        out_specs=pl.BlockSpec((None, tr, d), lambda i, r: (i, r, 0)),
        out_shape=jax.ShapeDtypeStruct((b, l, d), BF16),
        compiler_params=_cp(("parallel", "parallel"), 40), name="norm1_ctx")(x, g.reshape(1, d), mods, mods)


def _first_argmax(v, vmax, lane):
    return jnp.min(jnp.where(v == vmax, lane, float(LANE)), axis=-1, keepdims=True)


def _norm_router_kernel(x_ref, g_ref, sc_ref, sh_ref, wr_ref, br_ref, o_ref, sel_ref, *, ngr, ne):
    y = _rmsmod(x_ref[...], g_ref[...], sc_ref[...], sh_ref[...])
    o_ref[...] = y.astype(BF16)
    y_hi = y.astype(BF16)
    y_lo = (y - y_hi.astype(F32)).astype(BF16)
    w_hi = wr_ref[0]
    lg = (jnp.dot(y_hi, w_hi, preferred_element_type=F32) + jnp.dot(y_hi, wr_ref[1], preferred_element_type=F32)
          + jnp.dot(y_lo, w_hi, preferred_element_type=F32)) + br_ref[...]
    epg = ne // ngr
    lane = lax.broadcasted_iota(jnp.int32, lg.shape, 1).astype(F32)
    ninf = -jnp.inf
    gl = jnp.where(lane < ngr, lg, ninf)
    ge = jnp.exp(gl - jnp.max(gl, axis=-1, keepdims=True))
    g_prob = ge / jnp.sum(ge, axis=-1, keepdims=True)
    g_p = jnp.max(g_prob, axis=-1, keepdims=True)
    lo = ngr + epg * _first_argmax(g_prob, g_p, lane)
    cand = jnp.where(jnp.logical_and(lane >= lo, lane < lo + epg), lg, ninf)
    v1 = jnp.max(cand, axis=-1, keepdims=True)
    l1 = _first_argmax(cand, v1, lane)
    rest = jnp.where(lane == l1, ninf, cand)
    v2 = jnp.max(rest, axis=-1, keepdims=True)
    l2 = _first_argmax(rest, v2, lane)
    ex = jnp.exp(v2 - v1)
    w1 = g_p * (1.0 / (1.0 + ex))
    w2 = g_p * (ex / (1.0 + ex))
    sel_ref[...] = jnp.where(lane == 0, l1 - ngr, jnp.where(lane == 1, l2 - ngr,
                             jnp.where(lane == 2, w1, jnp.where(lane == 3, w2, 0.0))))


def _norm_router_call(x, g, mods, isc, ish, wr, br, ngr, ne):
    b, l, d = x.shape
    tr = min(256, l)
    nr = wr.shape[1]
    w_hi = wr.astype(BF16)
    wr = jnp.stack([w_hi, (wr - w_hi.astype(F32)).astype(BF16)], axis=0)
    return pl.pallas_call(
        functools.partial(_norm_router_kernel, ngr=ngr, ne=ne), grid=(b, l // tr),
        in_specs=[pl.BlockSpec((None, tr, d), lambda i, r: (i, r, 0)),
                  pl.BlockSpec((1, d), lambda i, r: (0, 0)),
                  pl.BlockSpec((None, None, 1, d), lambda i, r: (i, isc, 0, 0)),
                  pl.BlockSpec((None, None, 1, d), lambda i, r: (i, ish, 0, 0)),
                  pl.BlockSpec((2, d, nr), lambda i, r: (0, 0, 0)),
                  pl.BlockSpec((1, nr), lambda i, r: (0, 0))],
        out_specs=[pl.BlockSpec((None, tr, d), lambda i, r: (i, r, 0)),
                   pl.BlockSpec((None, tr, nr), lambda i, r: (i, r, 0))],
        out_shape=[jax.ShapeDtypeStruct((b, l, d), BF16), jax.ShapeDtypeStruct((b, l, nr), F32)],
        compiler_params=_cp(("parallel", "parallel"), 40),
        name="norm2_router")(x, g.reshape(1, d), mods, mods, wr, br)


def _mm_kernel(a_ref, b_ref, o_ref):
    o_ref[...] = jnp.dot(a_ref[...], b_ref[...].astype(BF16), preferred_element_type=F32).astype(o_ref.dtype)


def _mm_tiles(m, n, tm_max=1024):
    tm = min(tm_max, m)
    tn = min(512, n)
    return tm, tn


def _mm_call(a, b, out_dtype, name, col0=0, n=None, tm_max=1024):
    m, k = a.shape
    n = b.shape[1] - col0 if n is None else n
    tm, tn = _mm_tiles(m, n, tm_max)
    if col0 % LANE or n % tn or n % LANE:
        b = b[:, col0:col0 + n]
        pad = (-n) % LANE
        b = jnp.pad(b, ((0, 0), (0, pad)))
        n, col0 = n + pad, 0
        tm, tn = _mm_tiles(m, n, tm_max)
    return pl.pallas_call(
        _mm_kernel, grid=(m // tm, n // tn),
        in_specs=[pl.BlockSpec((tm, k), lambda i, j: (i, 0)),
                  pl.BlockSpec((pl.Element(k), pl.Element(tn)), lambda i, j: (0, (col0 // LANE + j * (tn // LANE)) * LANE))],
        out_specs=pl.BlockSpec((tm, tn), lambda i, j: (i, j)),
        out_shape=jax.ShapeDtypeStruct((m, n), out_dtype),
        compiler_params=_cp(("parallel", "parallel"), 48 if tm <= 1024 else 60), name=name)(a, b)


def _gelu_tanh(x):
    return x * (0.5 * (1.0 + jnp.tanh(math.sqrt(2.0 / math.pi) * (x + 0.044715 * (x * x * x)))))


def _glu_kernel(u_ref, yf_ref, yb_ref, d_ref, wv_ref, wg_ref, gate_ref, o_ref, a_ref):
    @pl.when(pl.program_id(1) == 0)
    def _():
        y = d_ref[...] * u_ref[...].astype(F32) + yf_ref[...].astype(F32) + yb_ref[...].astype(F32)
        a_ref[...] = _gelu_tanh(y).astype(BF16)

    a = a_ref[...]
    val = jnp.dot(a, wv_ref[...], preferred_element_type=F32)
    gl = jnp.dot(a, wg_ref[...], preferred_element_type=F32)
    o_ref[...] = (_sigmoid(gate_ref[...].astype(F32)) * (val * _sigmoid(gl))).astype(o_ref.dtype)


def _glu_call(u, yf, yb, dvec, wv, wg, gates):
    m, k = u.shape
    n = wv.shape[1]
    tm, tn = _mm_tiles(m, n)
    row = pl.BlockSpec((tm, k), lambda i, j: (i, 0))
    return pl.pallas_call(
        _glu_kernel, grid=(m // tm, n // tn),
        in_specs=[row, row, row,
                  pl.BlockSpec((1, k), lambda i, j: (0, 0)),
                  pl.BlockSpec((k, tn), lambda i, j: (0, j)),
                  pl.BlockSpec((k, tn), lambda i, j: (0, j)),
                  pl.BlockSpec((tm, tn), lambda i, j: (i, j))],
        out_specs=pl.BlockSpec((tm, tn), lambda i, j: (i, j)),
        out_shape=jax.ShapeDtypeStruct((m, n), BF16),
        scratch_shapes=[pltpu.VMEM((tm, k), BF16)],
        compiler_params=_cp(("parallel", "arbitrary"), 56), name="s5_glu")(u, yf, yb, dvec, wv, wg, gates)


def _merge_kernel(a_ref, w_ref, gate_ref, pa_ref, o_ref):
    br = jnp.dot(a_ref[...], w_ref[...].astype(BF16), preferred_element_type=F32)
    o_ref[...] = (pa_ref[...].astype(F32) + _sigmoid(gate_ref[...].astype(F32)) * br).astype(o_ref.dtype)


def _merge_call(a, w, gates, part_a):
    m, k = a.shape
    n = w.shape[1]
    tm, tn = _mm_tiles(m, n)
    off = n // tn
    return pl.pallas_call(
        _merge_kernel, grid=(m // tm, n // tn),
        in_specs=[pl.BlockSpec((tm, k), lambda i, j: (i, 0)),
                  pl.BlockSpec((k, tn), lambda i, j: (0, j)),
                  pl.BlockSpec((tm, tn), lambda i, j: (i, j + off)),
                  pl.BlockSpec((tm, tn), lambda i, j: (i, j))],
        out_specs=pl.BlockSpec((tm, tn), lambda i, j: (i, j)),
        out_shape=jax.ShapeDtypeStruct((m, n), BF16),
        compiler_params=_cp(("parallel", "parallel"), 48), name="ssd_out_merge")(a, w, gates, part_a)


def _resid_kernel(a_ref, w_ref, x_ref, g_ref, o_ref):
    mix = jnp.dot(a_ref[...], w_ref[...].astype(BF16), preferred_element_type=F32)
    o_ref[...] = x_ref[...] + g_ref[...] * mix


def _resid_call(a, w, x, mods, ig):
    b, l, d = x.shape
    k = a.shape[1]
    tm, tn = _mm_tiles(l, d)
    nt = l // tm
    return pl.pallas_call(
        _resid_kernel, grid=(b * nt, d // tn),
        in_specs=[pl.BlockSpec((tm, k), lambda i, j: (i, 0)),
                  pl.BlockSpec((k, tn), lambda i, j: (0, j)),
                  pl.BlockSpec((None, tm, tn), lambda i, j: (i // nt, i % nt, j)),
                  pl.BlockSpec((None, None, 1, tn), lambda i, j: (i // nt, ig, 0, j))],
        out_specs=pl.BlockSpec((None, tm, tn), lambda i, j: (i // nt, i % nt, j)),
        out_shape=jax.ShapeDtypeStruct((b, l, d), F32),
        compiler_params=_cp(("parallel", "parallel"), 48), name="w_o_resid")(a, w, x, mods)


def _s5_kernel(*refs, nb, tp, p8, need_y):
    if need_y:
        uf_ref, ub_ref, b_ref, c_ref, d_ref, lre_ref, lim_ref, h0_ref, yf_ref, yb_ref, h_ref, buf_ref, il_ref = refs
    else:
        uf_ref, ub_ref, b_ref, c_ref, d_ref, lre_ref, lim_ref, h0_ref, h_ref, buf_ref, il_ref = refs
    q = 2 * nb
    nk = p8 // LANE
    pitch = tp + S5_ROW_PAD

    @pl.when(pl.program_id(1) == 0)
    def _():
        h_ref[...] = h0_ref[...]

    rev = (lax.broadcasted_iota(jnp.int32, (tp, tp), 0) + lax.broadcasted_iota(jnp.int32, (tp, tp), 1)
           == tp - 1).astype(BF16)
    def pair_rows(u_ref, b, newer_first):
        il_ref[...] = u_ref[b].astype(F32)
        even = il_ref[pl.ds(0, tp, stride=2), :].astype(BF16)
        odd = il_ref[pl.ds(1, tp, stride=2), :].astype(BF16)
        return jnp.concatenate([odd, even] if newer_first else [even, odd], axis=1)

    lhs = []
    for d in range(2):
        if d == 0:
            u = jnp.concatenate([pair_rows(uf_ref, b, False) for b in range(nb)], axis=0)
        else:
            u = jnp.concatenate(
                [jnp.dot(rev, pair_rows(ub_ref, b, True), preferred_element_type=F32).astype(BF16)
                 for b in range(nb)], axis=0)
        lhs.append(u)
        bu = jnp.dot(u, b_ref[d], preferred_element_type=F32)
        for b in range(nb):
            r0 = (d * nb + b) * pitch
            for k in range(2 * nk):
                buf_ref[k, r0:r0 + tp, :] = bu[b * tp:(b + 1) * tp, k * LANE:(k + 1) * LANE]
    ar = [lre_ref[:, k * LANE:(k + 1) * LANE] for k in range(nk)]
    ai = [lim_ref[:, k * LANE:(k + 1) * LANE] for k in range(nk)]

    def step(s, carry):
        rows = pl.ds(s, q, stride=pitch)
        out = []
        for k in range(nk):
            hr, hi = carry[k]
            nr = ar[k] * hr - ai[k] * hi + buf_ref[k, rows, :]
            ni = ar[k] * hi + ai[k] * hr + buf_ref[nk + k, rows, :]
            buf_ref[k, rows, :] = hr
            buf_ref[nk + k, rows, :] = hi
            out.append((nr, ni))
        return tuple(out)

    init = tuple((h_ref[0, :, k * LANE:(k + 1) * LANE], h_ref[1, :, k * LANE:(k + 1) * LANE]) for k in range(nk))
    fin = lax.fori_loop(0, tp, step, init, unroll=8)
    for k in range(nk):
        h_ref[0, :, k * LANE:(k + 1) * LANE] = fin[k][0]
        h_ref[1, :, k * LANE:(k + 1) * LANE] = fin[k][1]

    if need_y:
        for d in range(2):
            h = jnp.concatenate(
                [jnp.concatenate([buf_ref[k, (d * nb + b) * pitch:(d * nb + b) * pitch + tp, :].astype(BF16)
                                  for k in range(2 * nk)], axis=1) for b in range(nb)], axis=0)
            y = (jnp.dot(h, c_ref[d], preferred_element_type=F32)
                 + jnp.dot(lhs[d], d_ref[d], preferred_element_type=F32))
            for b in range(nb):
                yb = y[b * tp:(b + 1) * tp]
                if d == 1:
                    yb = jnp.dot(rev, yb.astype(BF16), preferred_element_type=F32)
                first, second = (0, 1) if d == 0 else (1, 0)
                il_ref[pl.ds(first, tp, stride=2), :] = yb[:, 0:LANE]
                il_ref[pl.ds(second, tp, stride=2), :] = yb[:, LANE:2 * LANE]
                (yf_ref if d == 0 else yb_ref)[b] = il_ref[...].astype(BF16)


def _s5_call(u, bmat, cmat, dmat, lre, lim, h0, need_y):
    nb, l, w5 = u.shape
    nj = w5 // LANE
    q = 2 * nb
    t = min(S5_CHUNK, l)
    tp = t // 2
    p8 = lre.shape[-1]
    nc = l // t
    hspec = pl.BlockSpec((None, 2, q, p8), lambda j, c: (j, 0, 0, 0))
    ublk = (nb, t, LANE)
    in_specs = [pl.BlockSpec(ublk, lambda j, c: (0, c, j)), pl.BlockSpec(ublk, lambda j, c: (0, nc - 1 - c, j)),
                pl.BlockSpec((None, 2, 2 * LANE, 2 * p8), lambda j, c: (j, 0, 0, 0)),
                pl.BlockSpec((None, 2, 2 * p8, 2 * LANE), lambda j, c: (j, 0, 0, 0)),
                pl.BlockSpec((None, 2, 2 * LANE, 2 * LANE), lambda j, c: (j, 0, 0, 0)),
                pl.BlockSpec((None, q, p8), lambda j, c: (j, 0, 0)),
                pl.BlockSpec((None, q, p8), lambda j, c: (j, 0, 0)),
                hspec]
    out_specs = [hspec]
    out_shape = [jax.ShapeDtypeStruct((nj, 2, q, p8), F32)]
    scratch = [pltpu.VMEM((2 * p8 // LANE, q * (tp + S5_ROW_PAD), LANE), F32), pltpu.VMEM((t, LANE), F32)]
    if need_y:
        out_specs = [pl.BlockSpec(ublk, lambda j, c: (0, c, j)),
                     pl.BlockSpec(ublk, lambda j, c: (0, nc - 1 - c, j))] + out_specs
        out_shape = [jax.ShapeDtypeStruct((nb, l, w5), BF16)] * 2 + out_shape
    kern = functools.partial(_s5_kernel, nb=nb, tp=tp, p8=p8, need_y=need_y)
    return pl.pallas_call(
        kern, grid=(nj, nc), in_specs=in_specs, out_specs=out_specs, out_shape=out_shape,
        scratch_shapes=scratch, compiler_params=_cp(("parallel", "arbitrary"), 48),
        name="s5_scan" if need_y else "s5_scan_ctx")(u, u, bmat, cmat, dmat, lre, lim, h0)


def _s5_params(lam_re, lam_im, log_dt, b_re, b_im, c_re, c_im, nb):
    _, g, p = lam_re.shape
    s = b_re.shape[-1]
    gpb = LANE // s
    nj = g // gpb
    lam = lax.complex(lam_re.astype(F32), lam_im.astype(F32))
    lam_bar = jnp.exp(lam * jnp.exp(log_dt.astype(F32))[..., None])
    b_bar = ((lam_bar - 1.0) / lam)[..., None] * lax.complex(b_re.astype(F32), b_im.astype(F32))
    def quadrants(parts):
        r, c = parts[0].shape[-2:]
        same_group = (jnp.arange(gpb * r)[:, None] // r) == (jnp.arange(gpb * c)[None, :] // c)

        def spread(x):
            return jnp.where(same_group, jnp.tile(x.reshape(2, nj, gpb * r, c), (1, 1, 1, gpb)), 0.0)

        top = jnp.concatenate([spread(parts[0]), spread(parts[1])], axis=-1)
        bot = jnp.concatenate([spread(parts[2]), spread(parts[3])], axis=-1)
        return jnp.concatenate([top, bot], axis=-2).transpose(1, 0, 2, 3).astype(BF16)

    tb = lambda z: jnp.swapaxes(z, -1, -2)
    lb = lam_bar[..., None] * b_bar
    bmat = quadrants([tb(lb.real), tb(lb.imag), tb(b_bar.real), tb(b_bar.imag)])
    cc = lax.complex(c_re.astype(F32), c_im.astype(F32))
    c_l1 = cc * lam_bar[:, :, None, :]
    c_l2 = c_l1 * lam_bar[:, :, None, :]
    cmat = quadrants([tb(c_l1.real), tb(c_l2.real), -tb(c_l1.imag), -tb(c_l2.imag)])
    m0 = jnp.einsum('dgsp,dgpt->dgts', cc, b_bar).real
    m1 = jnp.einsum('dgsp,dgpt->dgts', c_l1, b_bar).real
    dmat = quadrants([m0, m1, jnp.zeros_like(m0), m0])

    def lam_of(part):
        v = part.reshape(2, nj, gpb * p).transpose(1, 0, 2)
        return jnp.repeat(v, nb, axis=1)

    lam2 = lam_bar * lam_bar
    return bmat, cmat, dmat, lam_of(lam2.real), lam_of(lam2.imag)


def _conv_silu(x, w, bias):
    l = x.shape[0]
    rows = lax.broadcasted_iota(jnp.int32, x.shape, 0)
    half = SSD_CONV // 2
    acc = x * w[half:half + 1, :] + bias
    for k in range(SSD_CONV):
        if k == half:
            continue
        off = k - half
        xs = pltpu.roll(x, shift=(-off) % l, axis=0)
        valid = jnp.logical_and(rows + off >= 0, rows + off < l)
        acc = acc + jnp.where(valid, xs, 0.0) * w[k:k + 1, :]
    return _silu(acc)


def _conv_kernel(x_ref, w_ref, b_ref, o_ref):
    o_ref[...] = _conv_silu(x_ref[...].astype(F32), w_ref[...], b_ref[...]).astype(o_ref.dtype)


def _proj_conv_kernel(a_ref, w_ref, cw_ref, cb_ref, o_ref):
    a = a_ref[...]
    wb = w_ref[...].astype(BF16)
    hw = wb.shape[1] // 2
    for c in range(2):
        cols = slice(c * hw, (c + 1) * hw)
        p = jnp.dot(a, wb[:, cols], preferred_element_type=F32)
        o_ref[:, cols] = _conv_silu(p, cw_ref[:, cols], cb_ref[:, cols]).astype(o_ref.dtype)


def _proj_conv_call(hn, w_all, col0, cw, cb):
    b, l, d = hn.shape
    c = cw.shape[1]
    tn = 512
    return pl.pallas_call(
        _proj_conv_kernel, grid=(b, c // tn),
        in_specs=[pl.BlockSpec((None, l, d), lambda i, j: (i, 0, 0), pipeline_mode=pl.Buffered(1)),
                  pl.BlockSpec((pl.Element(d), pl.Element(tn)),
                               lambda i, j: (0, (col0 // LANE + j * (tn // LANE)) * LANE)),
                  pl.BlockSpec((SSD_CONV, tn), lambda i, j: (0, j)),
                  pl.BlockSpec((1, tn), lambda i, j: (0, j))],
        out_specs=pl.BlockSpec((None, l, tn), lambda i, j: (i, 0, j)),
        out_shape=jax.ShapeDtypeStruct((b, l, c), BF16),
        compiler_params=_cp(("parallel", "arbitrary"), 60), name="in_xbc_conv")(hn, w_all, cw, cb.reshape(1, c))


def _conv_call(proj, w, bias, col0):
    b, l, _ = proj.shape
    c = w.shape[1]
    tc = 256
    off = col0 // tc
    return pl.pallas_call(
        _conv_kernel, grid=(b, c // tc),
        in_specs=[pl.BlockSpec((None, l, tc), lambda i, j: (i, 0, j + off)),
                  pl.BlockSpec((SSD_CONV, tc), lambda i, j: (0, j)),
                  pl.BlockSpec((1, tc), lambda i, j: (0, j))],
        out_specs=pl.BlockSpec((None, l, tc), lambda i, j: (i, 0, j)),
        out_shape=jax.ShapeDtypeStruct((b, l, c), BF16),
        compiler_params=_cp(("parallel", "parallel"), 40), name="ssd_conv")(proj, w, bias.reshape(1, c))


def _softplus(x):
    return jnp.maximum(x, 0.0) + jnp.log1p(jnp.exp(-jnp.abs(x)))


def _ssd_dir(xbc_ref, dt_ref, y_ref, h_ref, bias, a_neg, d, *, nh, ng, need_y):
    qn = SSD_CHUNK
    hd = SSD_HEAD_DIM
    w = nh * hd
    gw = w // ng
    gn = ng * SSD_STATE
    ii = lax.broadcasted_iota(jnp.int32, (qn, qn), 0)
    jj = lax.broadcasted_iota(jnp.int32, (qn, qn), 1)
    mask = (jj <= ii) if d == 0 else (jj >= ii)
    lmat = mask.astype(F32)
    dtv = _softplus(dt_ref[...] + bias)
    cum = jnp.dot(lmat, dtv * a_neg, precision=lax.Precision.HIGHEST, preferred_element_type=F32)
    cum_t = cum.T
    edge = qn - 1 if d == 0 else 0
    tot = cum[edge:edge + 1, :]
    dt_t = dtv.T
    wt_t = dt_t * jnp.exp(cum_t[:, edge:edge + 1] - cum_t)
    decay = jnp.exp(tot)
    lane = lax.broadcasted_iota(jnp.int32, (qn, LANE), 1)
    left = lane < hd
    zero = jnp.zeros((), BF16)
    for g in range(ng):
        bg = xbc_ref[:, w + g * SSD_STATE:w + (g + 1) * SSD_STATE]
        cg = xbc_ref[:, w + gn + g * SSD_STATE:w + gn + (g + 1) * SSD_STATE]
        bg_t = bg.astype(F32).T
        s_in = h_ref[d, :, g * gw:(g + 1) * gw]
        if need_y:
            cb = lax.dot_general(cg, bg, (((1,), (1,)), ((), ())), preferred_element_type=F32)
            yoff = jnp.dot(cg, s_in.astype(BF16), preferred_element_type=F32)
        for pr in range(gw // LANE):
            c0 = d * nh + (g * gw) // hd + 2 * pr
            col = g * gw + pr * LANE
            xp = xbc_ref[:, col:col + LANE]
            r = jnp.concatenate([jnp.where(left, xp, zero), jnp.where(left, zero, xp)], axis=0)
            tops, bots, cols = [], [], []
            for c in (c0, c0 + 1):
                bots.append((bg_t * wt_t[c:c + 1, :]).astype(BF16))
                if need_y:
                    ccol = jnp.broadcast_to(cum[:, c:c + 1], (qn, qn))
                    seg = jnp.where(mask, jnp.exp(ccol - cum_t[c:c + 1, :]), 0.0)
                    tops.append((cb * seg * dt_t[c:c + 1, :]).astype(BF16))
                    cols.append(ccol)
            dec = jnp.where(left[0:1], decay[:, c0:c0 + 1], decay[:, c0 + 1:c0 + 2])
            s_old = s_in[:, pr * LANE:(pr + 1) * LANE]
            if need_y:
                lhs = jnp.concatenate([jnp.concatenate(tops, axis=1), jnp.concatenate(bots, axis=1)], axis=0)
                out = jnp.dot(lhs, r, preferred_element_type=F32)
                ec = jnp.exp(jnp.where(left, cols[0], cols[1]))
                y_ref[:, col:col + LANE] = (out[:qn] + yoff[:, pr * LANE:(pr + 1) * LANE] * ec).astype(y_ref.dtype)
                s_new = out[qn:]
            else:
                s_new = jnp.dot(jnp.concatenate(bots, axis=1), r, preferred_element_type=F32)
            h_ref[d, :, col:col + LANE] = s_old * dec + s_new


def _ssd_kernel(*refs, nh, ng, need_y):
    if need_y:
        xf_ref, xb_ref, dtf_ref, dtb_ref, h0_ref, bias_ref, alog_ref, yf_ref, yb_ref, h_ref = refs
    else:
        xf_ref, xb_ref, dtf_ref, dtb_ref, h0_ref, bias_ref, alog_ref, h_ref = refs
        yf_ref = yb_ref = None

    @pl.when(pl.program_id(1) == 0)
    def _():
        h_ref[...] = h0_ref[...]

    bias = bias_ref[...]
    a_neg = -jnp.exp(alog_ref[...])
    _ssd_dir(xf_ref, dtf_ref, yf_ref, h_ref, bias, a_neg, 0, nh=nh, ng=ng, need_y=need_y)
    _ssd_dir(xb_ref, dtb_ref, yb_ref, h_ref, bias, a_neg, 1, nh=nh, ng=ng, need_y=need_y)


def _ssd_call(xbc, dt, h0, bias, alog, nh, ng, need_y):
    b, l, cd = xbc.shape
    w = nh * SSD_HEAD_DIM
    nc = l // SSD_CHUNK
    q = SSD_CHUNK
    fwd = lambda i, s: (i, s, 0)
    bwd = lambda i, s: (i, nc - 1 - s, 0)
    hspec = pl.BlockSpec((None, 2, SSD_STATE, w), lambda i, s: (i, 0, 0, 0))
    in_specs = [pl.BlockSpec((None, q, cd), fwd), pl.BlockSpec((None, q, cd), bwd),
                pl.BlockSpec((None, q, LANE), fwd), pl.BlockSpec((None, q, LANE), bwd),
                hspec,
                pl.BlockSpec((1, LANE), lambda i, s: (0, 0)), pl.BlockSpec((1, LANE), lambda i, s: (0, 0))]
    out_specs = [hspec]
    out_shape = [jax.ShapeDtypeStruct((b, 2, SSD_STATE, w), F32)]
    if need_y:
        out_specs = [pl.BlockSpec((None, q, w), fwd), pl.BlockSpec((None, q, w), bwd)] + out_specs
        out_shape = [jax.ShapeDtypeStruct((b, l, w), BF16)] * 2 + out_shape
    return pl.pallas_call(
        functools.partial(_ssd_kernel, nh=nh, ng=ng, need_y=need_y), grid=(b, nc),
        in_specs=in_specs, out_specs=out_specs, out_shape=out_shape,
        compiler_params=_cp(("parallel", "arbitrary"), 48),
        name="ssd_scan" if need_y else "ssd_scan_ctx")(xbc, xbc, dt, dt, h0, bias, alog)


def _gnorm_kernel(yf_ref, yb_ref, xs_ref, z_ref, d_ref, g_ref, o_ref, t_ref, *, rows, tb, gw):
    y = d_ref[...] * xs_ref[...].astype(F32) + yf_ref[...].astype(F32) + yb_ref[...].astype(F32)
    y = y * _silu(z_ref[...].astype(F32))
    bw = y.shape[1]
    parts = []
    for g in range(bw // gw):
        yg = y[:, g * gw:(g + 1) * gw]
        parts.append(yg * lax.rsqrt(jnp.mean(yg * yg, axis=-1, keepdims=True) + EPS))
    res = jnp.concatenate(parts, axis=1) * g_ref[...]
    pitch = rows + XPOSE_PAD
    for wl in range(tb):
        for kk in range(bw // LANE):
            t_ref[kk, wl * pitch:wl * pitch + rows, :] = res[wl * rows:(wl + 1) * rows, kk * LANE:(kk + 1) * LANE]
    for r in range(rows):
        o_ref[r] = jnp.concatenate(
            [t_ref[kk, pl.ds(r, tb, stride=pitch), :] for kk in range(bw // LANE)], axis=1).astype(o_ref.dtype)


def _gnorm_call(yf, yb, xbc, proj, dvec, gvec, ng):
    b, l, w = yf.shape
    rows = l // GRID_W
    tb = 2 * XPOSE_TILE
    gw = w // ng
    bw = gw * (2 if ng % 2 == 0 else 1)
    blk = pl.BlockSpec((None, tb * rows, bw), lambda i, s, g: (i, s, g))
    vec = pl.BlockSpec((1, bw), lambda i, s, g: (0, g))
    out = pl.pallas_call(
        functools.partial(_gnorm_kernel, rows=rows, tb=tb, gw=gw), grid=(b, GRID_W // tb, w // bw),
        in_specs=[blk, blk, blk, blk, vec, vec],
        out_specs=pl.BlockSpec((None, rows, tb, bw), lambda i, s, g: (i, 0, s, g)),
        out_shape=jax.ShapeDtypeStruct((b, rows, GRID_W, w), BF16),
        scratch_shapes=[pltpu.VMEM((bw // LANE, tb * (rows + XPOSE_PAD), LANE), F32)],
        compiler_params=_cp(("parallel", "parallel", "parallel"), 40),
        name="ssd_gnorm")(yf, yb, xbc, proj, dvec, gvec)
    return out.reshape(b * l, w)


def _moe_kernel(e_ref, b0_ref, nb_ref, *refs, nsub, nh):
    xs = refs[:nsub]
    wg_ref, wu_ref, wd_ref, o_ref, act_ref, wdb_ref = refs[nsub:]
    s = pl.program_id(0)
    t = pl.program_id(1)
    nb = nb_ref[s]

    @pl.when(jnp.logical_and(t < nh, nb > 0))
    def _():
        wg = wg_ref[...].astype(BF16)
        wu = wu_ref[...].astype(BF16)

        def act_block(k):
            x = xs[k][...]
            gt = jnp.dot(x, wg, preferred_element_type=F32)
            up = jnp.dot(x, wu, preferred_element_type=F32)
            act_ref[k, t] = (_silu(gt) * up).astype(BF16)

        act_block(0)
        for k in range(1, nsub):
            pl.when(k < nb)(functools.partial(act_block, k))

    @pl.when(jnp.logical_and(t >= nh, nb > 0))
    def _():
        wdb = wd_ref[...].astype(BF16)
        wdb_ref[...] = wdb

        def down_block(k, w):
            a = jnp.concatenate([act_ref[k, h] for h in range(nh)], axis=1)
            o_ref[k * MOE_BLOCK:(k + 1) * MOE_BLOCK, :] = jnp.dot(a, w, preferred_element_type=F32).astype(o_ref.dtype)

        down_block(0, wdb)
        for k in range(1, nsub):
            pl.when(k < nb)(lambda k=k: down_block(k, wdb_ref[...]))

            @pl.when(k >= nb)
            def _():
                o_ref[k * MOE_BLOCK:(k + 1) * MOE_BLOCK, :] = jnp.zeros((MOE_BLOCK, o_ref.shape[1]), o_ref.dtype)

    @pl.when(jnp.logical_and(t >= nh, nb == 0))
    def _():
        o_ref[...] = jnp.zeros(o_ref.shape, o_ref.dtype)


def _moe_call(sup_e, sup_b0, sup_nb, xs, w_gate, w_up, w_down):
    n_slots, d = xs.shape
    hid = w_gate.shape[-1]
    th = min(256, hid)
    tn = min(1024, d)
    nh = hid // th
    nt = d // tn
    ns = sup_e.shape[0]
    rsup = MOE_SUB * MOE_BLOCK

    def x_map(k):
        def index(s, t, e, b0, nb):
            sx = jnp.where(t < nh, s, jnp.minimum(s + 1, ns - 1))
            return (b0[sx * MOE_SUB + k], 0)
        return index

    def hid_idx(s, t, nb):
        return jnp.where(nb[s] > 0, jnp.minimum(t, nh - 1), nh - 1)

    def col_idx(s, t, nb):
        return jnp.where(nb[s] > 0, jnp.clip(t - nh, 0, nt - 1), nt - 1)

    gs = pltpu.PrefetchScalarGridSpec(
        num_scalar_prefetch=3, grid=(ns, nh + nt),
        in_specs=[pl.BlockSpec((MOE_BLOCK, d), x_map(k)) for k in range(MOE_SUB)] + [
            pl.BlockSpec((None, d, th), lambda s, t, e, b0, nb: (e[s], 0, hid_idx(s, t, nb))),
            pl.BlockSpec((None, d, th), lambda s, t, e, b0, nb: (e[s], 0, hid_idx(s, t, nb))),
            pl.BlockSpec((None, hid, tn), lambda s, t, e, b0, nb: (e[s], 0, col_idx(s, t, nb)))],
        out_specs=pl.BlockSpec((rsup, tn), lambda s, t, e, b0, nb: (s, jnp.clip(t - nh, 0, nt - 1))),
        scratch_shapes=[pltpu.VMEM((MOE_SUB, nh, MOE_BLOCK, th), BF16), pltpu.VMEM((hid, tn), BF16)])
    return pl.pallas_call(
        functools.partial(_moe_kernel, nsub=MOE_SUB, nh=nh), grid_spec=gs,
        out_shape=jax.ShapeDtypeStruct((ns * rsup, d), BF16),
        compiler_params=_cp(("arbitrary", "arbitrary"), 60), name="moe_experts")(
            sup_e, sup_b0, sup_nb, *([xs] * MOE_SUB), w_gate, w_up, w_down)


def _final_kernel(x_ref, ya_ref, yb_ref, w_ref, g2_ref, fg_ref, o_ref):
    wts = w_ref[...]
    moe = ya_ref[...].astype(F32) * wts[:, 0:1] + yb_ref[...].astype(F32) * wts[:, 1:2]
    x = x_ref[...] + g2_ref[...] * moe
    ms = jnp.mean(x * x, axis=-1, keepdims=True)
    o_ref[...] = x * lax.rsqrt(ms + EPS) * fg_ref[...]


def _final_call(x, y2, wts, mods, ig, fg):
    b, l, d = x.shape
    tr = min(256, l)
    nt = l // tr
    row = lambda i, r: (i * nt + r, 0)
    row2 = lambda i, r: (b * nt + i * nt + r, 0)
    ya = yb = y2
    return pl.pallas_call(
        _final_kernel, grid=(b, nt),
        in_specs=[pl.BlockSpec((None, tr, d), lambda i, r: (i, r, 0)),
                  pl.BlockSpec((tr, d), row), pl.BlockSpec((tr, d), row2),
                  pl.BlockSpec((tr, MOE_TOP_K), row),
                  pl.BlockSpec((None, None, 1, d), lambda i, r: (i, ig, 0, 0)),
                  pl.BlockSpec((1, d), lambda i, r: (0, 0))],
        out_specs=pl.BlockSpec((None, tr, d), lambda i, r: (i, r, 0)),
        out_shape=jax.ShapeDtypeStruct((b, l, d), F32),
        compiler_params=_cp(("parallel", "parallel"), 40), name="combine_final")(x, ya, yb, wts, mods, fg)


def _route(sel, ne):
    m = sel.shape[0]
    weights = sel[:, MOE_TOP_K:2 * MOE_TOP_K]
    n_assign = m * MOE_TOP_K
    flat_e = jnp.concatenate([sel[:, k] for k in range(MOE_TOP_K)], axis=0).astype(jnp.int32)
    onehot = (flat_e[:, None] == jnp.arange(ne, dtype=flat_e.dtype)[None, :]).astype(jnp.int32)
    csum = jnp.cumsum(onehot, axis=0)
    counts = csum[-1]
    local = jnp.sum(csum * onehot, axis=1) - 1
    nblk_e = (counts + MOE_BLOCK - 1) // MOE_BLOCK
    padded = nblk_e * MOE_BLOCK
    pad_end = jnp.cumsum(padded)
    pad_start = pad_end - padded
    dest = pad_start[flat_e] + local
    n_blocks = -(-(n_assign + ne * (MOE_BLOCK - 1)) // MOE_BLOCK)
    tok = jnp.arange(n_assign, dtype=jnp.int32) % m
    n_slots = n_blocks * MOE_BLOCK
    slot_tok = (jnp.arange(n_slots, dtype=jnp.int32) % m).at[dest].set(tok)
    rsup = MOE_SUB * MOE_BLOCK
    ns_e = (nblk_e + MOE_SUB - 1) // MOE_SUB
    sup_end = jnp.cumsum(ns_e)
    sup_start = sup_end - ns_e
    n_sup = (n_blocks + (MOE_SUB - 1) * ne) // MOE_SUB
    sidx = jnp.arange(n_sup, dtype=jnp.int32)
    last = sup_end[-1] - 1
    s_eff = jnp.minimum(sidx, last)
    e_s = jnp.minimum(jnp.searchsorted(sup_end, s_eff, side='right'), ne - 1).astype(jnp.int32)
    k_s = s_eff - sup_start[e_s]
    b0_s = pad_start[e_s] // MOE_BLOCK + MOE_SUB * k_s
    nb_s = jnp.clip(nblk_e[e_s] - MOE_SUB * k_s, 0, MOE_SUB)
    used = sidx <= last
    sup_nb = jnp.where(used, nb_s, 0).astype(jnp.int32)
    kk = jnp.arange(MOE_SUB, dtype=jnp.int32)[None, :]
    sup_b0 = jnp.maximum(lax.cummax(jnp.where(kk < sup_nb[:, None], b0_s[:, None] + kk, -1), axis=0), 0)
    sup_b0 = sup_b0.reshape(-1).astype(jnp.int32)
    pos = ((sup_start[flat_e] + local // rsup) * rsup + local % rsup).astype(jnp.int32)
    return weights, slot_tok, e_s, sup_b0, sup_nb, pos


def kernel(x, c, ctx, c_ctx, w_mod, b_mod, norm1_g, w_in, s5_lam_re, s5_lam_im, s5_log_dt, s5_b_re, s5_b_im, s5_c_re, s5_c_im, s5_d, s5_w_val, s5_w_gate, ssd_conv_w, ssd_conv_b, ssd_a_log, ssd_dt_bias, ssd_d, ssd_norm_g, ssd_w_out, w_o, norm2_g, moe_w_group, moe_b_group, moe_w_expert, moe_b_expert, moe_w_gate, moe_w_up, moe_w_down, final_g):
    depth = w_mod.shape[0]
    assert depth == 1, "single-layer block"
    bsz, n_lat, d = x.shape
    l_ctx = ctx.shape[1]
    w5 = s5_d.shape[1]
    nh = ssd_d.shape[1]
    w = nh * SSD_HEAD_DIM
    conv_dim = ssd_conv_w.shape[2]
    ng = (conv_dim - w) // (2 * SSD_STATE)
    ssd_in = w + conv_dim + 2 * nh
    o1, o2 = w5, w5 + ssd_in
    l = 0

    cc = jnp.concatenate([c, c_ctx[None, :]], axis=0)
    cc = jnp.pad(cc, ((0, (-cc.shape[0]) % 8), (0, 0)))
    mods = _mod_call(cc, w_mod[l], b_mod[l]).reshape(cc.shape[0], 6, 1, d)
    i_sh1, i_sc1, i_g1, i_sh2, i_sc2, i_g2 = range(6)

    w_in_l = w_in[l]
    o_dt = o1 + w + conv_dim

    hn_rm, hn_cm = _norm_lat_call(x, norm1_g[l], mods, i_sc1, i_sh1)
    hc = _norm_ctx_call(ctx, norm1_g[l], mods, bsz, i_sc1, i_sh1)
    hn_rm = hn_rm.reshape(bsz * n_lat, d)
    hn_cm = hn_cm.reshape(bsz * n_lat, d)
    hc = hc.reshape(bsz * l_ctx, d)

    u_lat = _mm_call(hn_rm, w_in_l, BF16, "in_s5", 0, o1, tm_max=2048).reshape(bsz, n_lat, w5)
    u_ctx = _mm_call(hc, w_in_l, BF16, "in_s5_ctx", 0, o1).reshape(bsz, l_ctx, w5)
    gates = _mm_call(hn_rm, w_in_l, BF16, "in_gates", o2, 2 * d, tm_max=2048)
    z_lat = _mm_call(hn_cm, w_in_l, BF16, "in_z", o1, w, tm_max=2048).reshape(bsz, n_lat, w)
    xbc_lat = _proj_conv_call(hn_cm.reshape(bsz, n_lat, d), w_in_l, o1 + w, ssd_conv_w[l], ssd_conv_b[l])
    p_ctx = _mm_call(hc, w_in_l, BF16, "in_xbc_ctx", o1 + w, conv_dim).reshape(bsz, l_ctx, conv_dim)
    dt_lat = _mm_call(hn_cm, w_in_l, F32, "in_dt", o_dt, 2 * nh).reshape(bsz, n_lat, LANE)
    dt_ctx = _mm_call(hc, w_in_l, F32, "in_dt_ctx", o_dt, 2 * nh).reshape(bsz, l_ctx, LANE)

    nj = w5 // LANE
    bmat, cmat, dmat, lre, lim = _s5_params(s5_lam_re[l], s5_lam_im[l], s5_log_dt[l], s5_b_re[l], s5_b_im[l],
                                            s5_c_re[l], s5_c_im[l], bsz)
    s5_zero = jnp.zeros((nj, 2, 2 * bsz, lre.shape[-1]), F32)
    (s5_ctx,) = _s5_call(u_ctx, bmat, cmat, dmat, lre, lim, s5_zero, False)
    ya_f, ya_b, _ = _s5_call(u_lat, bmat, cmat, dmat, lre, lim, s5_ctx, True)

    xbc_ctx = _conv_call(p_ctx, ssd_conv_w[l], ssd_conv_b[l], 0)
    pad_h = LANE - 2 * nh
    bias = jnp.pad(ssd_dt_bias[l].astype(F32).reshape(1, 2 * nh), ((0, 0), (0, pad_h)))
    alog = jnp.pad(ssd_a_log[l].astype(F32).reshape(1, 2 * nh), ((0, 0), (0, pad_h)))
    h_zero = jnp.zeros((bsz, 2, SSD_STATE, w), F32)
    (h_ctx,) = _ssd_call(xbc_ctx, dt_ctx, h_zero, bias, alog, nh, ng, False)
    y_f, y_b, _ = _ssd_call(xbc_lat, dt_lat, h_ctx, bias, alog, nh, ng, True)
    d_vec = jnp.repeat(ssd_d[l].astype(F32), SSD_HEAD_DIM).reshape(1, w)
    y_ssd = _gnorm_call(y_f, y_b, xbc_lat, z_lat, d_vec, ssd_norm_g[l].astype(F32).reshape(1, w), ng)

    m_lat = bsz * n_lat
    part_a = _glu_call(u_lat.reshape(m_lat, w5), ya_f.reshape(m_lat, w5), ya_b.reshape(m_lat, w5),
                       s5_d[l].astype(F32).reshape(1, w5),
                       s5_w_val[l].astype(BF16), s5_w_gate[l].astype(BF16), gates)
    merged = _merge_call(y_ssd, ssd_w_out[l], gates, part_a)
    x1 = _resid_call(merged, w_o[l], x, mods, i_g1)

    ngr = moe_w_group.shape[-1]
    ne = moe_w_expert.shape[-1]
    wr = jnp.concatenate([moe_w_group[l], moe_w_expert[l]], axis=1).astype(F32)
    wr = jnp.pad(wr, ((0, 0), (0, (-(ngr + ne)) % LANE)))
    br = jnp.concatenate([moe_b_group[l], moe_b_expert[l]]).astype(F32)
    br = jnp.pad(br, (0, (-(ngr + ne)) % LANE)).reshape(1, -1)
    hx, sel = _norm_router_call(x1, norm2_g[l], mods, i_sc2, i_sh2, wr, br, ngr, ne)
    m = bsz * n_lat
    hx = hx.reshape(m, d)
    weights, slot_tok, sup_e, sup_b0, sup_nb, pos = _route(sel.reshape(m, -1), ne)
    xs = hx[slot_tok]
    y_slots = _moe_call(sup_e, sup_b0, sup_nb, xs, moe_w_gate[l], moe_w_up[l], moe_w_down[l])
    y2 = y_slots[pos]
    return _final_call(x1, y2, weights.astype(F32), mods, i_g2, final_g.reshape(1, d))
```

```python
import functools
import math

import jax
import jax.numpy as jnp
from jax import lax
from jax.experimental import pallas as pl
from jax.experimental.pallas import tpu as pltpu

F32 = jnp.float32
BF16 = jnp.bfloat16

GRID_W = 64
EPS = 1e-6
LANE = 128
SSD_HEAD_DIM = 64
SSD_STATE = 128
SSD_CHUNK = 128
SSD_CONV = 5
MOE_TOP_K = 2
MOE_BLOCK = 256
MOE_SUB = 4
XPOSE_TILE = 16
XPOSE_PAD = 8
S5_CHUNK = 512
S5_ROW_PAD = 8


def _cp(sem, mb):
    return pltpu.CompilerParams(dimension_semantics=sem, vmem_limit_bytes=mb * 1024 * 1024)


def _sigmoid(x):
    return 1.0 / (1.0 + jnp.exp(-x))


def _silu(x):
    return x * _sigmoid(x)


def _mod_kernel(c_ref, w_ref, b_ref, o_ref):
    s = _silu(c_ref[...])
    o_ref[...] = jnp.dot(s.astype(BF16), w_ref[...].astype(BF16),
                         preferred_element_type=F32) + b_ref[...]


def _mod_call(cc, w, b):
    r, d = cc.shape
    n = w.shape[1]
    tn = min(512, n)
    return pl.pallas_call(
        _mod_kernel, grid=(n // tn,),
        in_specs=[pl.BlockSpec((r, d), lambda j: (0, 0)),
                  pl.BlockSpec((d, tn), lambda j: (0, j)),
                  pl.BlockSpec((1, tn), lambda j: (0, j))],
        out_specs=pl.BlockSpec((r, tn), lambda j: (0, j)),
        out_shape=jax.ShapeDtypeStruct((r, n), F32),
        compiler_params=_cp(("parallel",), 40), name="mod")(cc, w, b.reshape(1, n))


def _rmsmod(x, g, sc, sh):
    ms = jnp.mean(x * x, axis=-1, keepdims=True)
    return (x * lax.rsqrt(ms + EPS) * g) * (1.0 + sc) + sh


def _norm_lat_kernel(x_ref, g_ref, sc_ref, sh_ref, orm_ref, ocm_ref, t_ref, *, tb, d):
    pitch = tb + XPOSE_PAD
    for r in range(tb):
        y = _rmsmod(x_ref[r], g_ref[...], sc_ref[...], sh_ref[...])
        orm_ref[r] = y.astype(BF16)
        for kk in range(d // LANE):
            t_ref[kk, r * pitch:r * pitch + tb, :] = y[:, kk * LANE:(kk + 1) * LANE]
    for wl in range(tb):
        ocm_ref[wl] = jnp.concatenate(
            [t_ref[kk, pl.ds(wl, tb, stride=pitch), :] for kk in range(d // LANE)], axis=1).astype(BF16)


def _norm_lat_call(x, g, mods, isc, ish):
    b, l, d = x.shape
    rows = l // GRID_W
    tb = XPOSE_TILE
    xv = x.reshape(b, rows, GRID_W, d)
    orm, ocm = pl.pallas_call(
        functools.partial(_norm_lat_kernel, tb=tb, d=d), grid=(b, rows // tb, GRID_W // tb),
        in_specs=[pl.BlockSpec((None, tb, tb, d), lambda i, r, c: (i, r, c, 0)),
                  pl.BlockSpec((1, d), lambda i, r, c: (0, 0)),
                  pl.BlockSpec((None, None, 1, d), lambda i, r, c: (i, isc, 0, 0)),
                  pl.BlockSpec((None, None, 1, d), lambda i, r, c: (i, ish, 0, 0))],
        out_specs=[pl.BlockSpec((None, tb, tb, d), lambda i, r, c: (i, r, c, 0)),
                   pl.BlockSpec((None, tb, tb, d), lambda i, r, c: (i, c, r, 0))],
        out_shape=[jax.ShapeDtypeStruct((b, rows, GRID_W, d), BF16),
                   jax.ShapeDtypeStruct((b, GRID_W, rows, d), BF16)],
        scratch_shapes=[pltpu.VMEM((d // LANE, tb * (tb + XPOSE_PAD), LANE), F32)],
        compiler_params=_cp(("parallel", "parallel", "parallel"), 48),
        name="norm1_lat")(xv, g.reshape(1, d), mods, mods)
    return orm.reshape(b, l, d), ocm.reshape(b, l, d)


def _norm_kernel(x_ref, g_ref, sc_ref, sh_ref, o_ref):
    o_ref[...] = _rmsmod(x_ref[...], g_ref[...], sc_ref[...], sh_ref[...]).astype(o_ref.dtype)


def _norm_ctx_call(x, g, mods, row, isc, ish):
    b, l, d = x.shape
    tr = min(256, l)
    return pl.pallas_call(
        _norm_kernel, grid=(b, l // tr),
        in_specs=[pl.BlockSpec((None, tr, d), lambda i, r: (i, r, 0)),
                  pl.BlockSpec((1, d), lambda i, r: (0, 0)),
                  pl.BlockSpec((None, None, 1, d), lambda i, r: (row, isc, 0, 0)),
                  pl.BlockSpec((None, None, 1, d), lambda i, r: (row, ish, 0, 0))],
        out_specs=pl.BlockSpec((None, tr, d), lambda i, r: (i, r, 0)),
        out_shape=jax.ShapeDtypeStruct((b, l, d), BF16),
        compiler_params=_cp(("parallel", "parallel"), 40), name="norm1_ctx")(x, g.reshape(1, d), mods, mods)


def _first_argmax(v, vmax, lane):
    return jnp.min(jnp.where(v == vmax, lane, float(LANE)), axis=-1, keepdims=True)


def _norm_router_kernel(x_ref, g_ref, sc_ref, sh_ref, wr_ref, br_ref, o_ref, sel_ref, *, ngr, ne):
    y = _rmsmod(x_ref[...], g_ref[...], sc_ref[...], sh_ref[...])
    o_ref[...] = y.astype(BF16)
    y_hi = y.astype(BF16)
    y_lo = (y - y_hi.astype(F32)).astype(BF16)
    w_hi = wr_ref[0]
    lg = (jnp.dot(y_hi, w_hi, preferred_element_type=F32) + jnp.dot(y_hi, wr_ref[1], preferred_element_type=F32)
          + jnp.dot(y_lo, w_hi, preferred_element_type=F32)) + br_ref[...]
    epg = ne // ngr
    lane = lax.broadcasted_iota(jnp.int32, lg.shape, 1).astype(F32)
    ninf = -jnp.inf
    gl = jnp.where(lane < ngr, lg, ninf)
    ge = jnp.exp(gl - jnp.max(gl, axis=-1, keepdims=True))
    g_prob = ge / jnp.sum(ge, axis=-1, keepdims=True)
    g_p = jnp.max(g_prob, axis=-1, keepdims=True)
    lo = ngr + epg * _first_argmax(g_prob, g_p, lane)
    cand = jnp.where(jnp.logical_and(lane >= lo, lane < lo + epg), lg, ninf)
    v1 = jnp.max(cand, axis=-1, keepdims=True)
    l1 = _first_argmax(cand, v1, lane)
    rest = jnp.where(lane == l1, ninf, cand)
    v2 = jnp.max(rest, axis=-1, keepdims=True)
    l2 = _first_argmax(rest, v2, lane)
    ex = jnp.exp(v2 - v1)
    w1 = g_p * (1.0 / (1.0 + ex))
    w2 = g_p * (ex / (1.0 + ex))
    sel_ref[...] = jnp.where(lane == 0, l1 - ngr, jnp.where(lane == 1, l2 - ngr,
                             jnp.where(lane == 2, w1, jnp.where(lane == 3, w2, 0.0))))


def _norm_router_call(x, g, mods, isc, ish, wr, br, ngr, ne):
    b, l, d = x.shape
    tr = min(256, l)
    nr = wr.shape[1]
    w_hi = wr.astype(BF16)
    wr = jnp.stack([w_hi, (wr - w_hi.astype(F32)).astype(BF16)], axis=0)
    return pl.pallas_call(
        functools.partial(_norm_router_kernel, ngr=ngr, ne=ne), grid=(b, l // tr),
        in_specs=[pl.BlockSpec((None, tr, d), lambda i, r: (i, r, 0)),
                  pl.BlockSpec((1, d), lambda i, r: (0, 0)),
                  pl.BlockSpec((None, None, 1, d), lambda i, r: (i, isc, 0, 0)),
                  pl.BlockSpec((None, None, 1, d), lambda i, r: (i, ish, 0, 0)),
                  pl.BlockSpec((2, d, nr), lambda i, r: (0, 0, 0)),
                  pl.BlockSpec((1, nr), lambda i, r: (0, 0))],
        out_specs=[pl.BlockSpec((None, tr, d), lambda i, r: (i, r, 0)),
                   pl.BlockSpec((None, tr, nr), lambda i, r: (i, r, 0))],
        out_shape=[jax.ShapeDtypeStruct((b, l, d), BF16), jax.ShapeDtypeStruct((b, l, nr), F32)],
        compiler_params=_cp(("parallel", "parallel"), 40),
        name="norm2_router")(x, g.reshape(1, d), mods, mods, wr, br)


def _mm_kernel(a_ref, b_ref, o_ref):
    o_ref[...] = jnp.dot(a_ref[...], b_ref[...].astype(BF16), preferred_element_type=F32).astype(o_ref.dtype)


def _mm_tiles(m, n, tm_max=1024):
    tm = min(tm_max, m)
    tn = min(512, n)
    return tm, tn


def _mm_call(a, b, out_dtype, name, col0=0, n=None, tm_max=1024):
    m, k = a.shape
    n = b.shape[1] - col0 if n is None else n
    tm, tn = _mm_tiles(m, n, tm_max)
    if col0 % LANE or n % tn or n % LANE:
        b = b[:, col0:col0 + n]
        pad = (-n) % LANE
        b = jnp.pad(b, ((0, 0), (0, pad)))
        n, col0 = n + pad, 0
        tm, tn = _mm_tiles(m, n, tm_max)
    return pl.pallas_call(
        _mm_kernel, grid=(m // tm, n // tn),
        in_specs=[pl.BlockSpec((tm, k), lambda i, j: (i, 0)),
                  pl.BlockSpec((pl.Element(k), pl.Element(tn)), lambda i, j: (0, (col0 // LANE + j * (tn // LANE)) * LANE))],
        out_specs=pl.BlockSpec((tm, tn), lambda i, j: (i, j)),
        out_shape=jax.ShapeDtypeStruct((m, n), out_dtype),
        compiler_params=_cp(("parallel", "parallel"), 48 if tm <= 1024 else 60), name=name)(a, b)


def _gelu_tanh(x):
    return x * (0.5 * (1.0 + jnp.tanh(math.sqrt(2.0 / math.pi) * (x + 0.044715 * (x * x * x)))))


def _glu_kernel(u_ref, yf_ref, yb_ref, d_ref, wv_ref, wg_ref, gate_ref, o_ref, a_ref):
    @pl.when(pl.program_id(1) == 0)
    def _():
        y = d_ref[...] * u_ref[...].astype(F32) + yf_ref[...].astype(F32) + yb_ref[...].astype(F32)
        a_ref[...] = _gelu_tanh(y).astype(BF16)

    a = a_ref[...]
    val = jnp.dot(a, wv_ref[...], preferred_element_type=F32)
    gl = jnp.dot(a, wg_ref[...], preferred_element_type=F32)
    o_ref[...] = (_sigmoid(gate_ref[...].astype(F32)) * (val * _sigmoid(gl))).astype(o_ref.dtype)


def _glu_call(u, yf, yb, dvec, wv, wg, gates):
    m, k = u.shape
    n = wv.shape[1]
    tm, tn = _mm_tiles(m, n)
    row = pl.BlockSpec((tm, k), lambda i, j: (i, 0))
    return pl.pallas_call(
        _glu_kernel, grid=(m // tm, n // tn),
        in_specs=[row, row, row,
                  pl.BlockSpec((1, k), lambda i, j: (0, 0)),
                  pl.BlockSpec((k, tn), lambda i, j: (0, j)),
                  pl.BlockSpec((k, tn), lambda i, j: (0, j)),
                  pl.BlockSpec((tm, tn), lambda i, j: (i, j))],
        out_specs=pl.BlockSpec((tm, tn), lambda i, j: (i, j)),
        out_shape=jax.ShapeDtypeStruct((m, n), BF16),
        scratch_shapes=[pltpu.VMEM((tm, k), BF16)],
        compiler_params=_cp(("parallel", "arbitrary"), 56), name="s5_glu")(u, yf, yb, dvec, wv, wg, gates)


def _merge_kernel(a_ref, w_ref, gate_ref, pa_ref, o_ref):
    br = jnp.dot(a_ref[...], w_ref[...].astype(BF16), preferred_element_type=F32)
    o_ref[...] = (pa_ref[...].astype(F32) + _sigmoid(gate_ref[...].astype(F32)) * br).astype(o_ref.dtype)


def _merge_call(a, w, gates, part_a):
    m, k = a.shape
    n = w.shape[1]
    tm, tn = _mm_tiles(m, n)
    off = n // tn
    return pl.pallas_call(
        _merge_kernel, grid=(m // tm, n // tn),
        in_specs=[pl.BlockSpec((tm, k), lambda i, j: (i, 0)),
                  pl.BlockSpec((k, tn), lambda i, j: (0, j)),
                  pl.BlockSpec((tm, tn), lambda i, j: (i, j + off)),
                  pl.BlockSpec((tm, tn), lambda i, j: (i, j))],
        out_specs=pl.BlockSpec((tm, tn), lambda i, j: (i, j)),
        out_shape=jax.ShapeDtypeStruct((m, n), BF16),
        compiler_params=_cp(("parallel", "parallel"), 48), name="ssd_out_merge")(a, w, gates, part_a)


def _resid_kernel(a_ref, w_ref, x_ref, g_ref, o_ref):
    mix = jnp.dot(a_ref[...], w_ref[...].astype(BF16), preferred_element_type=F32)
    o_ref[...] = x_ref[...] + g_ref[...] * mix


def _resid_call(a, w, x, mods, ig):
    b, l, d = x.shape
    k = a.shape[1]
    tm, tn = _mm_tiles(l, d)
    nt = l // tm
    return pl.pallas_call(
        _resid_kernel, grid=(b * nt, d // tn),
        in_specs=[pl.BlockSpec((tm, k), lambda i, j: (i, 0)),
                  pl.BlockSpec((k, tn), lambda i, j: (0, j)),
                  pl.BlockSpec((None, tm, tn), lambda i, j: (i // nt, i % nt, j)),
                  pl.BlockSpec((None, None, 1, tn), lambda i, j: (i // nt, ig, 0, j))],
        out_specs=pl.BlockSpec((None, tm, tn), lambda i, j: (i // nt, i % nt, j)),
        out_shape=jax.ShapeDtypeStruct((b, l, d), F32),
        compiler_params=_cp(("parallel", "parallel"), 48), name="w_o_resid")(a, w, x, mods)


def _s5_kernel(*refs, nb, tp, p8, need_y):
    if need_y:
        uf_ref, ub_ref, b_ref, c_ref, d_ref, lre_ref, lim_ref, h0_ref, yf_ref, yb_ref, h_ref, buf_ref, il_ref = refs
    else:
        uf_ref, ub_ref, b_ref, c_ref, d_ref, lre_ref, lim_ref, h0_ref, h_ref, buf_ref, il_ref = refs
    q = 2 * nb
    nk = p8 // LANE
    pitch = tp + S5_ROW_PAD

    @pl.when(pl.program_id(1) == 0)
    def _():
        h_ref[...] = h0_ref[...]

    rev = (lax.broadcasted_iota(jnp.int32, (tp, tp), 0) + lax.broadcasted_iota(jnp.int32, (tp, tp), 1)
           == tp - 1).astype(BF16)
    def pair_rows(u_ref, b, newer_first):
        il_ref[...] = u_ref[b].astype(F32)
        even = il_ref[pl.ds(0, tp, stride=2), :].astype(BF16)
        odd = il_ref[pl.ds(1, tp, stride=2), :].astype(BF16)
        return jnp.concatenate([odd, even] if newer_first else [even, odd], axis=1)

    lhs = []
    for d in range(2):
        if d == 0:
            u = jnp.concatenate([pair_rows(uf_ref, b, False) for b in range(nb)], axis=0)
        else:
            u = jnp.concatenate(
                [jnp.dot(rev, pair_rows(ub_ref, b, True), preferred_element_type=F32).astype(BF16)
                 for b in range(nb)], axis=0)
        lhs.append(u)
        bu = jnp.dot(u, b_ref[d], preferred_element_type=F32)
        for b in range(nb):
            r0 = (d * nb + b) * pitch
            for k in range(2 * nk):
                buf_ref[k, r0:r0 + tp, :] = bu[b * tp:(b + 1) * tp, k * LANE:(k + 1) * LANE]
    ar = [lre_ref[:, k * LANE:(k + 1) * LANE] for k in range(nk)]
    ai = [lim_ref[:, k * LANE:(k + 1) * LANE] for k in range(nk)]

    def step(s, carry):
        rows = pl.ds(s, q, stride=pitch)
        out = []
        for k in range(nk):
            hr, hi = carry[k]
            nr = ar[k] * hr - ai[k] * hi + buf_ref[k, rows, :]
            ni = ar[k] * hi + ai[k] * hr + buf_ref[nk + k, rows, :]
            buf_ref[k, rows, :] = hr
            buf_ref[nk + k, rows, :] = hi
            out.append((nr, ni))
        return tuple(out)

    init = tuple((h_ref[0, :, k * LANE:(k + 1) * LANE], h_ref[1, :, k * LANE:(k + 1) * LANE]) for k in range(nk))
    fin = lax.fori_loop(0, tp, step, init, unroll=8)
    for k in range(nk):
        h_ref[0, :, k * LANE:(k + 1) * LANE] = fin[k][0]
        h_ref[1, :, k * LANE:(k + 1) * LANE] = fin[k][1]

    if need_y:
        for d in range(2):
            h = jnp.concatenate(
                [jnp.concatenate([buf_ref[k, (d * nb + b) * pitch:(d * nb + b) * pitch + tp, :].astype(BF16)
                                  for k in range(2 * nk)], axis=1) for b in range(nb)], axis=0)
            y = (jnp.dot(h, c_ref[d], preferred_element_type=F32)
                 + jnp.dot(lhs[d], d_ref[d], preferred_element_type=F32))
            for b in range(nb):
                yb = y[b * tp:(b + 1) * tp]
                if d == 1:
                    yb = jnp.dot(rev, yb.astype(BF16), preferred_element_type=F32)
                first, second = (0, 1) if d == 0 else (1, 0)
                il_ref[pl.ds(first, tp, stride=2), :] = yb[:, 0:LANE]
                il_ref[pl.ds(second, tp, stride=2), :] = yb[:, LANE:2 * LANE]
                (yf_ref if d == 0 else yb_ref)[b] = il_ref[...].astype(BF16)


def _s5_call(u, bmat, cmat, dmat, lre, lim, h0, need_y):
    nb, l, w5 = u.shape
    nj = w5 // LANE
    q = 2 * nb
    t = min(S5_CHUNK, l)
    tp = t // 2
    p8 = lre.shape[-1]
    nc = l // t
    hspec = pl.BlockSpec((None, 2, q, p8), lambda j, c: (j, 0, 0, 0))
    ublk = (nb, t, LANE)
    in_specs = [pl.BlockSpec(ublk, lambda j, c: (0, c, j)), pl.BlockSpec(ublk, lambda j, c: (0, nc - 1 - c, j)),
                pl.BlockSpec((None, 2, 2 * LANE, 2 * p8), lambda j, c: (j, 0, 0, 0)),
                pl.BlockSpec((None, 2, 2 * p8, 2 * LANE), lambda j, c: (j, 0, 0, 0)),
                pl.BlockSpec((None, 2, 2 * LANE, 2 * LANE), lambda j, c: (j, 0, 0, 0)),
                pl.BlockSpec((None, q, p8), lambda j, c: (j, 0, 0)),
                pl.BlockSpec((None, q, p8), lambda j, c: (j, 0, 0)),
                hspec]
    out_specs = [hspec]
    out_shape = [jax.ShapeDtypeStruct((nj, 2, q, p8), F32)]
    scratch = [pltpu.VMEM((2 * p8 // LANE, q * (tp + S5_ROW_PAD), LANE), F32), pltpu.VMEM((t, LANE), F32)]
    if need_y:
        out_specs = [pl.BlockSpec(ublk, lambda j, c: (0, c, j)),
                     pl.BlockSpec(ublk, lambda j, c: (0, nc - 1 - c, j))] + out_specs
        out_shape = [jax.ShapeDtypeStruct((nb, l, w5), BF16)] * 2 + out_shape
    kern = functools.partial(_s5_kernel, nb=nb, tp=tp, p8=p8, need_y=need_y)
    return pl.pallas_call(
        kern, grid=(nj, nc), in_specs=in_specs, out_specs=out_specs, out_shape=out_shape,
        scratch_shapes=scratch, compiler_params=_cp(("parallel", "arbitrary"), 48),
        name="s5_scan" if need_y else "s5_scan_ctx")(u, u, bmat, cmat, dmat, lre, lim, h0)


def _s5_params(lam_re, lam_im, log_dt, b_re, b_im, c_re, c_im, nb):
    _, g, p = lam_re.shape
    s = b_re.shape[-1]
    gpb = LANE // s
    nj = g // gpb
    lam = lax.complex(lam_re.astype(F32), lam_im.astype(F32))
    lam_bar = jnp.exp(lam * jnp.exp(log_dt.astype(F32))[..., None])
    b_bar = ((lam_bar - 1.0) / lam)[..., None] * lax.complex(b_re.astype(F32), b_im.astype(F32))
    def quadrants(parts):
        r, c = parts[0].shape[-2:]
        same_group = (jnp.arange(gpb * r)[:, None] // r) == (jnp.arange(gpb * c)[None, :] // c)

        def spread(x):
            return jnp.where(same_group, jnp.tile(x.reshape(2, nj, gpb * r, c), (1, 1, 1, gpb)), 0.0)

        top = jnp.concatenate([spread(parts[0]), spread(parts[1])], axis=-1)
        bot = jnp.concatenate([spread(parts[2]), spread(parts[3])], axis=-1)
        return jnp.concatenate([top, bot], axis=-2).transpose(1, 0, 2, 3).astype(BF16)

    tb = lambda z: jnp.swapaxes(z, -1, -2)
    lb = lam_bar[..., None] * b_bar
    bmat = quadrants([tb(lb.real), tb(lb.imag), tb(b_bar.real), tb(b_bar.imag)])
    cc = lax.complex(c_re.astype(F32), c_im.astype(F32))
    c_l1 = cc * lam_bar[:, :, None, :]
    c_l2 = c_l1 * lam_bar[:, :, None, :]
    cmat = quadrants([tb(c_l1.real), tb(c_l2.real), -tb(c_l1.imag), -tb(c_l2.imag)])
    m0 = jnp.einsum('dgsp,dgpt->dgts', cc, b_bar).real
    m1 = jnp.einsum('dgsp,dgpt->dgts', c_l1, b_bar).real
    dmat = quadrants([m0, m1, jnp.zeros_like(m0), m0])

    def lam_of(part):
        v = part.reshape(2, nj, gpb * p).transpose(1, 0, 2)
        return jnp.repeat(v, nb, axis=1)

    lam2 = lam_bar * lam_bar
    return bmat, cmat, dmat, lam_of(lam2.real), lam_of(lam2.imag)


def _conv_silu(x, w, bias):
    l = x.shape[0]
    rows = lax.broadcasted_iota(jnp.int32, x.shape, 0)
    half = SSD_CONV // 2
    acc = x * w[half:half + 1, :] + bias
    for k in range(SSD_CONV):
        if k == half:
            continue
        off = k - half
        xs = pltpu.roll(x, shift=(-off) % l, axis=0)
        valid = jnp.logical_and(rows + off >= 0, rows + off < l)
        acc = acc + jnp.where(valid, xs, 0.0) * w[k:k + 1, :]
    return _silu(acc)


def _conv_kernel(x_ref, w_ref, b_ref, o_ref):
    o_ref[...] = _conv_silu(x_ref[...].astype(F32), w_ref[...], b_ref[...]).astype(o_ref.dtype)


def _proj_conv_kernel(a_ref, w_ref, cw_ref, cb_ref, o_ref):
    a = a_ref[...]
    wb = w_ref[...].astype(BF16)
    hw = wb.shape[1] // 2
    for c in range(2):
        cols = slice(c * hw, (c + 1) * hw)
        p = jnp.dot(a, wb[:, cols], preferred_element_type=F32)
        o_ref[:, cols] = _conv_silu(p, cw_ref[:, cols], cb_ref[:, cols]).astype(o_ref.dtype)


def _proj_conv_call(hn, w_all, col0, cw, cb):
    b, l, d = hn.shape
    c = cw.shape[1]
    tn = 512
    return pl.pallas_call(
        _proj_conv_kernel, grid=(b, c // tn),
        in_specs=[pl.BlockSpec((None, l, d), lambda i, j: (i, 0, 0), pipeline_mode=pl.Buffered(1)),
                  pl.BlockSpec((pl.Element(d), pl.Element(tn)),
                               lambda i, j: (0, (col0 // LANE + j * (tn // LANE)) * LANE)),
                  pl.BlockSpec((SSD_CONV, tn), lambda i, j: (0, j)),
                  pl.BlockSpec((1, tn), lambda i, j: (0, j))],
        out_specs=pl.BlockSpec((None, l, tn), lambda i, j: (i, 0, j)),
        out_shape=jax.ShapeDtypeStruct((b, l, c), BF16),
        compiler_params=_cp(("parallel", "arbitrary"), 60), name="in_xbc_conv")(hn, w_all, cw, cb.reshape(1, c))


def _conv_call(proj, w, bias, col0):
    b, l, _ = proj.shape
    c = w.shape[1]
    tc = 256
    off = col0 // tc
    return pl.pallas_call(
        _conv_kernel, grid=(b, c // tc),
        in_specs=[pl.BlockSpec((None, l, tc), lambda i, j: (i, 0, j + off)),
                  pl.BlockSpec((SSD_CONV, tc), lambda i, j: (0, j)),
                  pl.BlockSpec((1, tc), lambda i, j: (0, j))],
        out_specs=pl.BlockSpec((None, l, tc), lambda i, j: (i, 0, j)),
        out_shape=jax.ShapeDtypeStruct((b, l, c), BF16),
        compiler_params=_cp(("parallel", "parallel"), 40), name="ssd_conv")(proj, w, bias.reshape(1, c))


def _softplus(x):
    return jnp.maximum(x, 0.0) + jnp.log1p(jnp.exp(-jnp.abs(x)))


def _ssd_dir(xbc_ref, dt_ref, y_ref, h_ref, bias, a_neg, d, *, nh, ng, need_y):
    qn = SSD_CHUNK
    hd = SSD_HEAD_DIM
    w = nh * hd
    gw = w // ng
    gn = ng * SSD_STATE
    ii = lax.broadcasted_iota(jnp.int32, (qn, qn), 0)
    jj = lax.broadcasted_iota(jnp.int32, (qn, qn), 1)
    mask = (jj <= ii) if d == 0 else (jj >= ii)
    lmat = mask.astype(F32)
    dtv = _softplus(dt_ref[...] + bias)
    cum = jnp.dot(lmat, dtv * a_neg, precision=lax.Precision.HIGHEST, preferred_element_type=F32)
    cum_t = cum.T
    edge = qn - 1 if d == 0 else 0
    tot = cum[edge:edge + 1, :]
    dt_t = dtv.T
    wt_t = dt_t * jnp.exp(cum_t[:, edge:edge + 1] - cum_t)
    decay = jnp.exp(tot)
    lane = lax.broadcasted_iota(jnp.int32, (qn, LANE), 1)
    left = lane < hd
    zero = jnp.zeros((), BF16)
    for g in range(ng):
        bg = xbc_ref[:, w + g * SSD_STATE:w + (g + 1) * SSD_STATE]
        cg = xbc_ref[:, w + gn + g * SSD_STATE:w + gn + (g + 1) * SSD_STATE]
        bg_t = bg.astype(F32).T
        s_in = h_ref[d, :, g * gw:(g + 1) * gw]
        if need_y:
            cb = lax.dot_general(cg, bg, (((1,), (1,)), ((), ())), preferred_element_type=F32)
            yoff = jnp.dot(cg, s_in.astype(BF16), preferred_element_type=F32)
        for pr in range(gw // LANE):
            c0 = d * nh + (g * gw) // hd + 2 * pr
            col = g * gw + pr * LANE
            xp = xbc_ref[:, col:col + LANE]
            r = jnp.concatenate([jnp.where(left, xp, zero), jnp.where(left, zero, xp)], axis=0)
            tops, bots, cols = [], [], []
            for c in (c0, c0 + 1):
                bots.append((bg_t * wt_t[c:c + 1, :]).astype(BF16))
                if need_y:
                    ccol = jnp.broadcast_to(cum[:, c:c + 1], (qn, qn))
                    seg = jnp.where(mask, jnp.exp(ccol - cum_t[c:c + 1, :]), 0.0)
                    tops.append((cb * seg * dt_t[c:c + 1, :]).astype(BF16))
                    cols.append(ccol)
            dec = jnp.where(left[0:1], decay[:, c0:c0 + 1], decay[:, c0 + 1:c0 + 2])
            s_old = s_in[:, pr * LANE:(pr + 1) * LANE]
            if need_y:
                lhs = jnp.concatenate([jnp.concatenate(tops, axis=1), jnp.concatenate(bots, axis=1)], axis=0)
                out = jnp.dot(lhs, r, preferred_element_type=F32)
                ec = jnp.exp(jnp.where(left, cols[0], cols[1]))
                y_ref[:, col:col + LANE] = (out[:qn] + yoff[:, pr * LANE:(pr + 1) * LANE] * ec).astype(y_ref.dtype)
                s_new = out[qn:]
            else:
                s_new = jnp.dot(jnp.concatenate(bots, axis=1), r, preferred_element_type=F32)
            h_ref[d, :, col:col + LANE] = s_old * dec + s_new


def _ssd_kernel(*refs, nh, ng, need_y):
    if need_y:
        xf_ref, xb_ref, dtf_ref, dtb_ref, h0_ref, bias_ref, alog_ref, yf_ref, yb_ref, h_ref = refs
    else:
        xf_ref, xb_ref, dtf_ref, dtb_ref, h0_ref, bias_ref, alog_ref, h_ref = refs
        yf_ref = yb_ref = None

    @pl.when(pl.program_id(1) == 0)
    def _():
        h_ref[...] = h0_ref[...]

    bias = bias_ref[...]
    a_neg = -jnp.exp(alog_ref[...])
    _ssd_dir(xf_ref, dtf_ref, yf_ref, h_ref, bias, a_neg, 0, nh=nh, ng=ng, need_y=need_y)
    _ssd_dir(xb_ref, dtb_ref, yb_ref, h_ref, bias, a_neg, 1, nh=nh, ng=ng, need_y=need_y)


def _ssd_call(xbc, dt, h0, bias, alog, nh, ng, need_y):
    b, l, cd = xbc.shape
    w = nh * SSD_HEAD_DIM
    nc = l // SSD_CHUNK
    q = SSD_CHUNK
    fwd = lambda i, s: (i, s, 0)
    bwd = lambda i, s: (i, nc - 1 - s, 0)
    hspec = pl.BlockSpec((None, 2, SSD_STATE, w), lambda i, s: (i, 0, 0, 0))
    in_specs = [pl.BlockSpec((None, q, cd), fwd), pl.BlockSpec((None, q, cd), bwd),
                pl.BlockSpec((None, q, LANE), fwd), pl.BlockSpec((None, q, LANE), bwd),
                hspec,
                pl.BlockSpec((1, LANE), lambda i, s: (0, 0)), pl.BlockSpec((1, LANE), lambda i, s: (0, 0))]
    out_specs = [hspec]
    out_shape = [jax.ShapeDtypeStruct((b, 2, SSD_STATE, w), F32)]
    if need_y:
        out_specs = [pl.BlockSpec((None, q, w), fwd), pl.BlockSpec((None, q, w), bwd)] + out_specs
        out_shape = [jax.ShapeDtypeStruct((b, l, w), BF16)] * 2 + out_shape
    return pl.pallas_call(
        functools.partial(_ssd_kernel, nh=nh, ng=ng, need_y=need_y), grid=(b, nc),
        in_specs=in_specs, out_specs=out_specs, out_shape=out_shape,
        compiler_params=_cp(("parallel", "arbitrary"), 48),
        name="ssd_scan" if need_y else "ssd_scan_ctx")(xbc, xbc, dt, dt, h0, bias, alog)


def _gnorm_kernel(yf_ref, yb_ref, xs_ref, z_ref, d_ref, g_ref, o_ref, t_ref, *, rows, tb, gw):
    y = d_ref[...] * xs_ref[...].astype(F32) + yf_ref[...].astype(F32) + yb_ref[...].astype(F32)
    y = y * _silu(z_ref[...].astype(F32))
    bw = y.shape[1]
    parts = []
    for g in range(bw // gw):
        yg = y[:, g * gw:(g + 1) * gw]
        parts.append(yg * lax.rsqrt(jnp.mean(yg * yg, axis=-1, keepdims=True) + EPS))
    res = jnp.concatenate(parts, axis=1) * g_ref[...]
    pitch = rows + XPOSE_PAD
    for wl in range(tb):
        for kk in range(bw // LANE):
            t_ref[kk, wl * pitch:wl * pitch + rows, :] = res[wl * rows:(wl + 1) * rows, kk * LANE:(kk + 1) * LANE]
    for r in range(rows):
        o_ref[r] = jnp.concatenate(
            [t_ref[kk, pl.ds(r, tb, stride=pitch), :] for kk in range(bw // LANE)], axis=1).astype(o_ref.dtype)


def _gnorm_call(yf, yb, xbc, proj, dvec, gvec, ng):
    b, l, w = yf.shape
    rows = l // GRID_W
    tb = 2 * XPOSE_TILE
    gw = w // ng
    bw = gw * (2 if ng % 2 == 0 else 1)
    blk = pl.BlockSpec((None, tb * rows, bw), lambda i, s, g: (i, s, g))
    vec = pl.BlockSpec((1, bw), lambda i, s, g: (0, g))
    out = pl.pallas_call(
        functools.partial(_gnorm_kernel, rows=rows, tb=tb, gw=gw), grid=(b, GRID_W // tb, w // bw),
        in_specs=[blk, blk, blk, blk, vec, vec],
        out_specs=pl.BlockSpec((None, rows, tb, bw), lambda i, s, g: (i, 0, s, g)),
        out_shape=jax.ShapeDtypeStruct((b, rows, GRID_W, w), BF16),
        scratch_shapes=[pltpu.VMEM((bw // LANE, tb * (rows + XPOSE_PAD), LANE), F32)],
        compiler_params=_cp(("parallel", "parallel", "parallel"), 40),
        name="ssd_gnorm")(yf, yb, xbc, proj, dvec, gvec)
    return out.reshape(b * l, w)


def _moe_kernel(e_ref, b0_ref, nb_ref, *refs, nsub, nh):
    xs = refs[:nsub]
    wg_ref, wu_ref, wd_ref, o_ref, act_ref, wdb_ref = refs[nsub:]
    s = pl.program_id(0)
    t = pl.program_id(1)
    nb = nb_ref[s]

    @pl.when(jnp.logical_and(t < nh, nb > 0))
    def _():
        wg = wg_ref[...].astype(BF16)
        wu = wu_ref[...].astype(BF16)

        def act_block(k):
            x = xs[k][...]
            gt = jnp.dot(x, wg, preferred_element_type=F32)
            up = jnp.dot(x, wu, preferred_element_type=F32)
            act_ref[k, t] = (_silu(gt) * up).astype(BF16)

        act_block(0)
        for k in range(1, nsub):
            pl.when(k < nb)(functools.partial(act_block, k))

    @pl.when(jnp.logical_and(t >= nh, nb > 0))
    def _():
        wdb = wd_ref[...].astype(BF16)
        wdb_ref[...] = wdb

        def down_block(k, w):
            a = jnp.concatenate([act_ref[k, h] for h in range(nh)], axis=1)
            o_ref[k * MOE_BLOCK:(k + 1) * MOE_BLOCK, :] = jnp.dot(a, w, preferred_element_type=F32).astype(o_ref.dtype)

        down_block(0, wdb)
        for k in range(1, nsub):
            pl.when(k < nb)(lambda k=k: down_block(k, wdb_ref[...]))

            @pl.when(k >= nb)
            def _():
                o_ref[k * MOE_BLOCK:(k + 1) * MOE_BLOCK, :] = jnp.zeros((MOE_BLOCK, o_ref.shape[1]), o_ref.dtype)

    @pl.when(jnp.logical_and(t >= nh, nb == 0))
    def _():
        o_ref[...] = jnp.zeros(o_ref.shape, o_ref.dtype)


def _moe_call(sup_e, sup_b0, sup_nb, xs, w_gate, w_up, w_down):
    n_slots, d = xs.shape
    hid = w_gate.shape[-1]
    th = min(256, hid)
    tn = min(1024, d)
    nh = hid // th
    nt = d // tn
    ns = sup_e.shape[0]
    rsup = MOE_SUB * MOE_BLOCK

    def x_map(k):
        def index(s, t, e, b0, nb):
            sx = jnp.where(t < nh, s, jnp.minimum(s + 1, ns - 1))
            return (b0[sx * MOE_SUB + k], 0)
        return index

    def hid_idx(s, t, nb):
        return jnp.where(nb[s] > 0, jnp.minimum(t, nh - 1), nh - 1)

    def col_idx(s, t, nb):
        return jnp.where(nb[s] > 0, jnp.clip(t - nh, 0, nt - 1), nt - 1)

    gs = pltpu.PrefetchScalarGridSpec(
        num_scalar_prefetch=3, grid=(ns, nh + nt),
        in_specs=[pl.BlockSpec((MOE_BLOCK, d), x_map(k)) for k in range(MOE_SUB)] + [
            pl.BlockSpec((None, d, th), lambda s, t, e, b0, nb: (e[s], 0, hid_idx(s, t, nb))),
            pl.BlockSpec((None, d, th), lambda s, t, e, b0, nb: (e[s], 0, hid_idx(s, t, nb))),
            pl.BlockSpec((None, hid, tn), lambda s, t, e, b0, nb: (e[s], 0, col_idx(s, t, nb)))],
        out_specs=pl.BlockSpec((rsup, tn), lambda s, t, e, b0, nb: (s, jnp.clip(t - nh, 0, nt - 1))),
        scratch_shapes=[pltpu.VMEM((MOE_SUB, nh, MOE_BLOCK, th), BF16), pltpu.VMEM((hid, tn), BF16)])
    return pl.pallas_call(
        functools.partial(_moe_kernel, nsub=MOE_SUB, nh=nh), grid_spec=gs,
        out_shape=jax.ShapeDtypeStruct((ns * rsup, d), BF16),
        compiler_params=_cp(("arbitrary", "arbitrary"), 60), name="moe_experts")(
            sup_e, sup_b0, sup_nb, *([xs] * MOE_SUB), w_gate, w_up, w_down)


def _final_kernel(x_ref, ya_ref, yb_ref, w_ref, g2_ref, fg_ref, o_ref):
    wts = w_ref[...]
    moe = ya_ref[...].astype(F32) * wts[:, 0:1] + yb_ref[...].astype(F32) * wts[:, 1:2]
    x = x_ref[...] + g2_ref[...] * moe
    ms = jnp.mean(x * x, axis=-1, keepdims=True)
    o_ref[...] = x * lax.rsqrt(ms + EPS) * fg_ref[...]


def _final_call(x, y2, wts, mods, ig, fg):
    b, l, d = x.shape
    tr = min(256, l)
    nt = l // tr
    row = lambda i, r: (i * nt + r, 0)
    row2 = lambda i, r: (b * nt + i * nt + r, 0)
    ya = yb = y2
    return pl.pallas_call(
        _final_kernel, grid=(b, nt),
        in_specs=[pl.BlockSpec((None, tr, d), lambda i, r: (i, r, 0)),
                  pl.BlockSpec((tr, d), row), pl.BlockSpec((tr, d), row2),
                  pl.BlockSpec((tr, MOE_TOP_K), row),
                  pl.BlockSpec((None, None, 1, d), lambda i, r: (i, ig, 0, 0)),
                  pl.BlockSpec((1, d), lambda i, r: (0, 0))],
        out_specs=pl.BlockSpec((None, tr, d), lambda i, r: (i, r, 0)),
        out_shape=jax.ShapeDtypeStruct((b, l, d), F32),
        compiler_params=_cp(("parallel", "parallel"), 40), name="combine_final")(x, ya, yb, wts, mods, fg)


def _route(sel, ne):
    m = sel.shape[0]
    weights = sel[:, MOE_TOP_K:2 * MOE_TOP_K]
    n_assign = m * MOE_TOP_K
    flat_e = jnp.concatenate([sel[:, k] for k in range(MOE_TOP_K)], axis=0).astype(jnp.int32)
    onehot = (flat_e[:, None] == jnp.arange(ne, dtype=flat_e.dtype)[None, :]).astype(jnp.int32)
    csum = jnp.cumsum(onehot, axis=0)
    counts = csum[-1]
    local = jnp.sum(csum * onehot, axis=1) - 1
    nblk_e = (counts + MOE_BLOCK - 1) // MOE_BLOCK
    padded = nblk_e * MOE_BLOCK
    pad_end = jnp.cumsum(padded)
    pad_start = pad_end - padded
    dest = pad_start[flat_e] + local
    n_blocks = -(-(n_assign + ne * (MOE_BLOCK - 1)) // MOE_BLOCK)
    tok = jnp.arange(n_assign, dtype=jnp.int32) % m
    n_slots = n_blocks * MOE_BLOCK
    slot_tok = (jnp.arange(n_slots, dtype=jnp.int32) % m).at[dest].set(tok)
    rsup = MOE_SUB * MOE_BLOCK
    ns_e = (nblk_e + MOE_SUB - 1) // MOE_SUB
    sup_end = jnp.cumsum(ns_e)
    sup_start = sup_end - ns_e
    n_sup = (n_blocks + (MOE_SUB - 1) * ne) // MOE_SUB
    sidx = jnp.arange(n_sup, dtype=jnp.int32)
    last = sup_end[-1] - 1
    s_eff = jnp.minimum(sidx, last)
    e_s = jnp.minimum(jnp.searchsorted(sup_end, s_eff, side='right'), ne - 1).astype(jnp.int32)
    k_s = s_eff - sup_start[e_s]
    b0_s = pad_start[e_s] // MOE_BLOCK + MOE_SUB * k_s
    nb_s = jnp.clip(nblk_e[e_s] - MOE_SUB * k_s, 0, MOE_SUB)
    used = sidx <= last
    sup_nb = jnp.where(used, nb_s, 0).astype(jnp.int32)
    kk = jnp.arange(MOE_SUB, dtype=jnp.int32)[None, :]
    sup_b0 = jnp.maximum(lax.cummax(jnp.where(kk < sup_nb[:, None], b0_s[:, None] + kk, -1), axis=0), 0)
    sup_b0 = sup_b0.reshape(-1).astype(jnp.int32)
    pos = ((sup_start[flat_e] + local // rsup) * rsup + local % rsup).astype(jnp.int32)
    return weights, slot_tok, e_s, sup_b0, sup_nb, pos


def kernel(x, c, ctx, c_ctx, w_mod, b_mod, norm1_g, w_in, s5_lam_re, s5_lam_im, s5_log_dt, s5_b_re, s5_b_im, s5_c_re, s5_c_im, s5_d, s5_w_val, s5_w_gate, ssd_conv_w, ssd_conv_b, ssd_a_log, ssd_dt_bias, ssd_d, ssd_norm_g, ssd_w_out, w_o, norm2_g, moe_w_group, moe_b_group, moe_w_expert, moe_b_expert, moe_w_gate, moe_w_up, moe_w_down, final_g):
    depth = w_mod.shape[0]
    assert depth == 1, "single-layer block"
    bsz, n_lat, d = x.shape
    l_ctx = ctx.shape[1]
    w5 = s5_d.shape[1]
    nh = ssd_d.shape[1]
    w = nh * SSD_HEAD_DIM
    conv_dim = ssd_conv_w.shape[2]
    ng = (conv_dim - w) // (2 * SSD_STATE)
    ssd_in = w + conv_dim + 2 * nh
    o1, o2 = w5, w5 + ssd_in
    l = 0

    cc = jnp.concatenate([c, c_ctx[None, :]], axis=0)
    cc = jnp.pad(cc, ((0, (-cc.shape[0]) % 8), (0, 0)))
    mods = _mod_call(cc, w_mod[l], b_mod[l]).reshape(cc.shape[0], 6, 1, d)
    i_sh1, i_sc1, i_g1, i_sh2, i_sc2, i_g2 = range(6)

    w_in_l = w_in[l]
    o_dt = o1 + w + conv_dim

    hn_rm, hn_cm = _norm_lat_call(x, norm1_g[l], mods, i_sc1, i_sh1)
    hc = _norm_ctx_call(ctx, norm1_g[l], mods, bsz, i_sc1, i_sh1)
    hn_rm = hn_rm.reshape(bsz * n_lat, d)
    hn_cm = hn_cm.reshape(bsz * n_lat, d)
    hc = hc.reshape(bsz * l_ctx, d)

    u_lat = _mm_call(hn_rm, w_in_l, BF16, "in_s5", 0, o1, tm_max=2048).reshape(bsz, n_lat, w5)
    u_ctx = _mm_call(hc, w_in_l, BF16, "in_s5_ctx", 0, o1).reshape(bsz, l_ctx, w5)
    gates = _mm_call(hn_rm, w_in_l, BF16, "in_gates", o2, 2 * d, tm_max=2048)
    z_lat = _mm_call(hn_cm, w_in_l, BF16, "in_z", o1, w, tm_max=2048).reshape(bsz, n_lat, w)
    xbc_lat = _proj_conv_call(hn_cm.reshape(bsz, n_lat, d), w_in_l, o1 + w, ssd_conv_w[l], ssd_conv_b[l])
    p_ctx = _mm_call(hc, w_in_l, BF16, "in_xbc_ctx", o1 + w, conv_dim).reshape(bsz, l_ctx, conv_dim)
    dt_lat = _mm_call(hn_cm, w_in_l, F32, "in_dt", o_dt, 2 * nh).reshape(bsz, n_lat, LANE)
    dt_ctx = _mm_call(hc, w_in_l, F32, "in_dt_ctx", o_dt, 2 * nh).reshape(bsz, l_ctx, LANE)

    nj = w5 // LANE
    bmat, cmat, dmat, lre, lim = _s5_params(s5_lam_re[l], s5_lam_im[l], s5_log_dt[l], s5_b_re[l], s5_b_im[l],
                                            s5_c_re[l], s5_c_im[l], bsz)
    s5_zero = jnp.zeros((nj, 2, 2 * bsz, lre.shape[-1]), F32)
    (s5_ctx,) = _s5_call(u_ctx, bmat, cmat, dmat, lre, lim, s5_zero, False)
    ya_f, ya_b, _ = _s5_call(u_lat, bmat, cmat, dmat, lre, lim, s5_ctx, True)

    xbc_ctx = _conv_call(p_ctx, ssd_conv_w[l], ssd_conv_b[l], 0)
    pad_h = LANE - 2 * nh
    bias = jnp.pad(ssd_dt_bias[l].astype(F32).reshape(1, 2 * nh), ((0, 0), (0, pad_h)))
    alog = jnp.pad(ssd_a_log[l].astype(F32).reshape(1, 2 * nh), ((0, 0), (0, pad_h)))
    h_zero = jnp.zeros((bsz, 2, SSD_STATE, w), F32)
    (h_ctx,) = _ssd_call(xbc_ctx, dt_ctx, h_zero, bias, alog, nh, ng, False)
    y_f, y_b, _ = _ssd_call(xbc_lat, dt_lat, h_ctx, bias, alog, nh, ng, True)
    d_vec = jnp.repeat(ssd_d[l].astype(F32), SSD_HEAD_DIM).reshape(1, w)
    y_ssd = _gnorm_call(y_f, y_b, xbc_lat, z_lat, d_vec, ssd_norm_g[l].astype(F32).reshape(1, w), ng)

    m_lat = bsz * n_lat
    part_a = _glu_call(u_lat.reshape(m_lat, w5), ya_f.reshape(m_lat, w5), ya_b.reshape(m_lat, w5),
                       s5_d[l].astype(F32).reshape(1, w5),
                       s5_w_val[l].astype(BF16), s5_w_gate[l].astype(BF16), gates)
    merged = _merge_call(y_ssd, ssd_w_out[l], gates, part_a)
    x1 = _resid_call(merged, w_o[l], x, mods, i_g1)

    ngr = moe_w_group.shape[-1]
    ne = moe_w_expert.shape[-1]
    wr = jnp.concatenate([moe_w_group[l], moe_w_expert[l]], axis=1).astype(F32)
    wr = jnp.pad(wr, ((0, 0), (0, (-(ngr + ne)) % LANE)))
    br = jnp.concatenate([moe_b_group[l], moe_b_expert[l]]).astype(F32)
    br = jnp.pad(br, (0, (-(ngr + ne)) % LANE)).reshape(1, -1)
    hx, sel = _norm_router_call(x1, norm2_g[l], mods, i_sc2, i_sh2, wr, br, ngr, ne)
    m = bsz * n_lat
    hx = hx.reshape(m, d)
    weights, slot_tok, sup_e, sup_b0, sup_nb, pos = _route(sel.reshape(m, -1), ne)
    xs = hx[slot_tok]
    y_slots = _moe_call(sup_e, sup_b0, sup_nb, xs, moe_w_gate[l], moe_w_up[l], moe_w_down[l])
    y2 = y_slots[pos]
    return _final_call(x1, y2, weights.astype(F32), mods, i_g2, final_g.reshape(1, d))
```

```python
import functools
import math

import jax
import jax.numpy as jnp
from jax import lax
from jax.experimental import pallas as pl
from jax.experimental.pallas import tpu as pltpu

F32 = jnp.float32
BF16 = jnp.bfloat16

GRID_W = 64
EPS = 1e-6
LANE = 128
SSD_HEAD_DIM = 64
SSD_STATE = 128
SSD_CHUNK = 128
SSD_CONV = 5
MOE_TOP_K = 2
MOE_BLOCK = 256
MOE_SUB = 4
XPOSE_TILE = 16
XPOSE_PAD = 8
S5_CHUNK = 512
S5_ROW_PAD = 8


def _cp(sem, mb):
    return pltpu.CompilerParams(dimension_semantics=sem, vmem_limit_bytes=mb * 1024 * 1024)


def _sigmoid(x):
    return 1.0 / (1.0 + jnp.exp(-x))


def _silu(x):
    return x * _sigmoid(x)


def _mod_kernel(c_ref, w_ref, b_ref, o_ref):
    s = _silu(c_ref[...])
    o_ref[...] = jnp.dot(s.astype(BF16), w_ref[...].astype(BF16),
                         preferred_element_type=F32) + b_ref[...]


def _mod_call(cc, w, b):
    r, d = cc.shape
    n = w.shape[1]
    tn = min(512, n)
    return pl.pallas_call(
        _mod_kernel, grid=(n // tn,),
        in_specs=[pl.BlockSpec((r, d), lambda j: (0, 0)),
                  pl.BlockSpec((d, tn), lambda j: (0, j)),
                  pl.BlockSpec((1, tn), lambda j: (0, j))],
        out_specs=pl.BlockSpec((r, tn), lambda j: (0, j)),
        out_shape=jax.ShapeDtypeStruct((r, n), F32),
        compiler_params=_cp(("parallel",), 40), name="mod")(cc, w, b.reshape(1, n))


def _rmsmod(x, g, sc, sh):
    ms = jnp.mean(x * x, axis=-1, keepdims=True)
    return (x * lax.rsqrt(ms + EPS) * g) * (1.0 + sc) + sh


def _norm_lat_kernel(x_ref, g_ref, sc_ref, sh_ref, orm_ref, ocm_ref, t_ref, *, tb, d):
    pitch = tb + XPOSE_PAD
    for r in range(tb):
        y = _rmsmod(x_ref[r], g_ref[...], sc_ref[...], sh_ref[...])
        orm_ref[r] = y.astype(BF16)
        for kk in range(d // LANE):
            t_ref[kk, r * pitch:r * pitch + tb, :] = y[:, kk * LANE:(kk + 1) * LANE]
    for wl in range(tb):
        ocm_ref[wl] = jnp.concatenate(
            [t_ref[kk, pl.ds(wl, tb, stride=pitch), :] for kk in range(d // LANE)], axis=1).astype(BF16)


def _norm_lat_call(x, g, mods, isc, ish):
    b, l, d = x.shape
    rows = l // GRID_W
    tb = XPOSE_TILE
    xv = x.reshape(b, rows, GRID_W, d)
    orm, ocm = pl.pallas_call(
        functools.partial(_norm_lat_kernel, tb=tb, d=d), grid=(b, rows // tb, GRID_W // tb),
        in_specs=[pl.BlockSpec((None, tb, tb, d), lambda i, r, c: (i, r, c, 0)),
                  pl.BlockSpec((1, d), lambda i, r, c: (0, 0)),
                  pl.BlockSpec((None, None, 1, d), lambda i, r, c: (i, isc, 0, 0)),
                  pl.BlockSpec((None, None, 1, d), lambda i, r, c: (i, ish, 0, 0))],
        out_specs=[pl.BlockSpec((None, tb, tb, d), lambda i, r, c: (i, r, c, 0)),
                   pl.BlockSpec((None, tb, tb, d), lambda i, r, c: (i, c, r, 0))],
        out_shape=[jax.ShapeDtypeStruct((b, rows, GRID_W, d), BF16),
                   jax.ShapeDtypeStruct((b, GRID_W, rows, d), BF16)],
        scratch_shapes=[pltpu.VMEM((d // LANE, tb * (tb + XPOSE_PAD), LANE), F32)],
        compiler_params=_cp(("parallel", "parallel", "parallel"), 48),
        name="norm1_lat")(xv, g.reshape(1, d), mods, mods)
    return orm.reshape(b, l, d), ocm.reshape(b, l, d)


def _norm_kernel(x_ref, g_ref, sc_ref, sh_ref, o_ref):
    o_ref[...] = _rmsmod(x_ref[...], g_ref[...], sc_ref[...], sh_ref[...]).astype(o_ref.dtype)


def _norm_ctx_call(x, g, mods, row, isc, ish):
    b, l, d = x.shape
    tr = min(256, l)
    return pl.pallas_call(
        _norm_kernel, grid=(b, l // tr),
        in_specs=[pl.BlockSpec((None, tr, d), lambda i, r: (i, r, 0)),
                  pl.BlockSpec((1, d), lambda i, r: (0, 0)),
                  pl.BlockSpec((None, None, 1, d), lambda i, r: (row, isc, 0, 0)),
                  pl.BlockSpec((None, None, 1, d), lambda i, r: (row, ish, 0, 0))],
        out_specs=pl.BlockSpec((None, tr, d), lambda i, r: (i, r, 0)),
        out_shape=jax.ShapeDtypeStruct((b, l, d), BF16),
        compiler_params=_cp(("parallel", "parallel"), 40), name="norm1_ctx")(x, g.reshape(1, d), mods, mods)


def _first_argmax(v, vmax, lane):
    return jnp.min(jnp.where(v == vmax, lane, float(LANE)), axis=-1, keepdims=True)


def _norm_router_kernel(x_ref, g_ref, sc_ref, sh_ref, wr_ref, br_ref, o_ref, sel_ref, *, ngr, ne):
    y = _rmsmod(x_ref[...], g_ref[...], sc_ref[...], sh_ref[...])
    o_ref[...] = y.astype(BF16)
    y_hi = y.astype(BF16)
    y_lo = (y - y_hi.astype(F32)).astype(BF16)
    w_hi = wr_ref[0]
    lg = (jnp.dot(y_hi, w_hi, preferred_element_type=F32) + jnp.dot(y_hi, wr_ref[1], preferred_element_type=F32)
          + jnp.dot(y_lo, w_hi, preferred_element_type=F32)) + br_ref[...]
    epg = ne // ngr
    lane = lax.broadcasted_iota(jnp.int32, lg.shape, 1).astype(F32)
    ninf = -jnp.inf
    gl = jnp.where(lane < ngr, lg, ninf)
    ge = jnp.exp(gl - jnp.max(gl, axis=-1, keepdims=True))
    g_prob = ge / jnp.sum(ge, axis=-1, keepdims=True)
    g_p = jnp.max(g_prob, axis=-1, keepdims=True)
    lo = ngr + epg * _first_argmax(g_prob, g_p, lane)
    cand = jnp.where(jnp.logical_and(lane >= lo, lane < lo + epg), lg, ninf)
    v1 = jnp.max(cand, axis=-1, keepdims=True)
    l1 = _first_argmax(cand, v1, lane)
    rest = jnp.where(lane == l1, ninf, cand)
    v2 = jnp.max(rest, axis=-1, keepdims=True)
    l2 = _first_argmax(rest, v2, lane)
    ex = jnp.exp(v2 - v1)
    w1 = g_p * (1.0 / (1.0 + ex))
    w2 = g_p * (ex / (1.0 + ex))
    sel_ref[...] = jnp.where(lane == 0, l1 - ngr, jnp.where(lane == 1, l2 - ngr,
                             jnp.where(lane == 2, w1, jnp.where(lane == 3, w2, 0.0))))


def _norm_router_call(x, g, mods, isc, ish, wr, br, ngr, ne):
    b, l, d = x.shape
    tr = min(256, l)
    nr = wr.shape[1]
    w_hi = wr.astype(BF16)
    wr = jnp.stack([w_hi, (wr - w_hi.astype(F32)).astype(BF16)], axis=0)
    return pl.pallas_call(
        functools.partial(_norm_router_kernel, ngr=ngr, ne=ne), grid=(b, l // tr),
        in_specs=[pl.BlockSpec((None, tr, d), lambda i, r: (i, r, 0)),
                  pl.BlockSpec((1, d), lambda i, r: (0, 0)),
                  pl.BlockSpec((None, None, 1, d), lambda i, r: (i, isc, 0, 0)),
                  pl.BlockSpec((None, None, 1, d), lambda i, r: (i, ish, 0, 0)),
                  pl.BlockSpec((2, d, nr), lambda i, r: (0, 0, 0)),
                  pl.BlockSpec((1, nr), lambda i, r: (0, 0))],
        out_specs=[pl.BlockSpec((None, tr, d), lambda i, r: (i, r, 0)),
                   pl.BlockSpec((None, tr, nr), lambda i, r: (i, r, 0))],
        out_shape=[jax.ShapeDtypeStruct((b, l, d), BF16), jax.ShapeDtypeStruct((b, l, nr), F32)],
        compiler_params=_cp(("parallel", "parallel"), 40),
        name="norm2_router")(x, g.reshape(1, d), mods, mods, wr, br)


def _mm_kernel(a_ref, b_ref, o_ref):
    o_ref[...] = jnp.dot(a_ref[...], b_ref[...].astype(BF16), preferred_element_type=F32).astype(o_ref.dtype)


def _mm_tiles(m, n, tm_max=1024):
    tm = min(tm_max, m)
    tn = min(512, n)
    return tm, tn


def _mm_call(a, b, out_dtype, name, col0=0, n=None, tm_max=1024):
    m, k = a.shape
    n = b.shape[1] - col0 if n is None else n
    tm, tn = _mm_tiles(m, n, tm_max)
    if col0 % LANE or n % tn or n % LANE:
        b = b[:, col0:col0 + n]
        pad = (-n) % LANE
        b = jnp.pad(b, ((0, 0), (0, pad)))
        n, col0 = n + pad, 0
        tm, tn = _mm_tiles(m, n, tm_max)
    return pl.pallas_call(
        _mm_kernel, grid=(m // tm, n // tn),
        in_specs=[pl.BlockSpec((tm, k), lambda i, j: (i, 0)),
                  pl.BlockSpec((pl.Element(k), pl.Element(tn)), lambda i, j: (0, (col0 // LANE + j * (tn // LANE)) * LANE))],
        out_specs=pl.BlockSpec((tm, tn), lambda i, j: (i, j)),
        out_shape=jax.ShapeDtypeStruct((m, n), out_dtype),
        compiler_params=_cp(("parallel", "parallel"), 48 if tm <= 1024 else 60), name=name)(a, b)


def _mm2_call(a, b, win0, win1, name):
    m, k = a.shape
    (c0, n0), (c1, n1) = win0, win1
    tm, tn = _mm_tiles(m, n0 + n1, 2048)
    j0 = n0 // tn

    def wcol(i, j):
        blk = jnp.where(j < j0, c0 // LANE + j * (tn // LANE), c1 // LANE + (j - j0) * (tn // LANE))
        return (0, blk * LANE)

    return pl.pallas_call(
        _mm_kernel, grid=(m // tm, (n0 + n1) // tn),
        in_specs=[pl.BlockSpec((tm, k), lambda i, j: (i, 0)),
                  pl.BlockSpec((pl.Element(k), pl.Element(tn)), wcol)],
        out_specs=pl.BlockSpec((tm, tn), lambda i, j: (i, j)),
        out_shape=jax.ShapeDtypeStruct((m, n0 + n1), BF16),
        compiler_params=_cp(("parallel", "parallel"), 60), name=name)(a, b)


def _gelu_tanh(x):
    return x * (0.5 * (1.0 + jnp.tanh(math.sqrt(2.0 / math.pi) * (x + 0.044715 * (x * x * x)))))


def _glu_kernel(u_ref, yf_ref, yb_ref, d_ref, wv_ref, wg_ref, gate_ref, o_ref, a_ref):
    @pl.when(pl.program_id(1) == 0)
    def _():
        y = d_ref[...] * u_ref[...].astype(F32) + yf_ref[...].astype(F32) + yb_ref[...].astype(F32)
        a_ref[...] = _gelu_tanh(y).astype(BF16)

    a = a_ref[...]
    val = jnp.dot(a, wv_ref[...], preferred_element_type=F32)
    gl = jnp.dot(a, wg_ref[...], preferred_element_type=F32)
    o_ref[...] = (_sigmoid(gate_ref[...].astype(F32)) * (val * _sigmoid(gl))).astype(o_ref.dtype)


def _glu_call(u, yf, yb, dvec, wv, wg, gates, goff=0):
    m, k = yf.shape
    n = wv.shape[1]
    tm, tn = _mm_tiles(m, n)
    row = pl.BlockSpec((tm, k), lambda i, j: (i, 0))
    return pl.pallas_call(
        _glu_kernel, grid=(m // tm, n // tn),
        in_specs=[row, row, row,
                  pl.BlockSpec((1, k), lambda i, j: (0, 0)),
                  pl.BlockSpec((k, tn), lambda i, j: (0, j)),
                  pl.BlockSpec((k, tn), lambda i, j: (0, j)),
                  pl.BlockSpec((tm, tn), lambda i, j: (i, j + goff))],
        out_specs=pl.BlockSpec((tm, tn), lambda i, j: (i, j)),
        out_shape=jax.ShapeDtypeStruct((m, n), BF16),
        scratch_shapes=[pltpu.VMEM((tm, k), BF16)],
        compiler_params=_cp(("parallel", "arbitrary"), 56), name="s5_glu")(u, yf, yb, dvec, wv, wg, gates)


def _merge_kernel(a_ref, w_ref, gate_ref, pa_ref, o_ref):
    br = jnp.dot(a_ref[...], w_ref[...].astype(BF16), preferred_element_type=F32)
    o_ref[...] = (pa_ref[...].astype(F32) + _sigmoid(gate_ref[...].astype(F32)) * br).astype(o_ref.dtype)


def _merge_call(a, w, gates, part_a, goff=0):
    m, k = a.shape
    n = w.shape[1]
    tm, tn = _mm_tiles(m, n)
    off = n // tn
    return pl.pallas_call(
        _merge_kernel, grid=(m // tm, n // tn),
        in_specs=[pl.BlockSpec((tm, k), lambda i, j: (i, 0)),
                  pl.BlockSpec((k, tn), lambda i, j: (0, j)),
                  pl.BlockSpec((tm, tn), lambda i, j: (i, j + off + goff)),
                  pl.BlockSpec((tm, tn), lambda i, j: (i, j))],
        out_specs=pl.BlockSpec((tm, tn), lambda i, j: (i, j)),
        out_shape=jax.ShapeDtypeStruct((m, n), BF16),
        compiler_params=_cp(("parallel", "parallel"), 48), name="ssd_out_merge")(a, w, gates, part_a)


def _resid_kernel(a_ref, w_ref, x_ref, g_ref, o_ref):
    mix = jnp.dot(a_ref[...], w_ref[...].astype(BF16), preferred_element_type=F32)
    o_ref[...] = x_ref[...] + g_ref[...] * mix


def _resid_call(a, w, x, mods, ig):
    b, l, d = x.shape
    k = a.shape[1]
    tm, tn = _mm_tiles(l, d)
    nt = l // tm
    return pl.pallas_call(
        _resid_kernel, grid=(b * nt, d // tn),
        in_specs=[pl.BlockSpec((tm, k), lambda i, j: (i, 0)),
                  pl.BlockSpec((k, tn), lambda i, j: (0, j)),
                  pl.BlockSpec((None, tm, tn), lambda i, j: (i // nt, i % nt, j)),
                  pl.BlockSpec((None, None, 1, tn), lambda i, j: (i // nt, ig, 0, j))],
        out_specs=pl.BlockSpec((None, tm, tn), lambda i, j: (i // nt, i % nt, j)),
        out_shape=jax.ShapeDtypeStruct((b, l, d), F32),
        compiler_params=_cp(("parallel", "parallel"), 48), name="w_o_resid")(a, w, x, mods)


def _s5_kernel(*refs, nb, tp, p8, need_y):
    if need_y:
        uf_ref, ub_ref, b_ref, c_ref, d_ref, lre_ref, lim_ref, h0_ref, yf_ref, yb_ref, h_ref, buf_ref, il_ref = refs
    else:
        uf_ref, ub_ref, b_ref, c_ref, d_ref, lre_ref, lim_ref, h0_ref, h_ref, buf_ref, il_ref = refs
    q = 2 * nb
    nk = p8 // LANE
    pitch = tp + S5_ROW_PAD

    @pl.when(pl.program_id(1) == 0)
    def _():
        h_ref[...] = h0_ref[...]

    rev = (lax.broadcasted_iota(jnp.int32, (tp, tp), 0) + lax.broadcasted_iota(jnp.int32, (tp, tp), 1)
           == tp - 1).astype(BF16)
    def pair_rows(u_ref, b, newer_first):
        il_ref[...] = u_ref[b].astype(F32)
        even = il_ref[pl.ds(0, tp, stride=2), :].astype(BF16)
        odd = il_ref[pl.ds(1, tp, stride=2), :].astype(BF16)
        return jnp.concatenate([odd, even] if newer_first else [even, odd], axis=1)

    lhs = []
    for d in range(2):
        if d == 0:
            u = jnp.concatenate([pair_rows(uf_ref, b, False) for b in range(nb)], axis=0)
        else:
            u = jnp.concatenate(
                [jnp.dot(rev, pair_rows(ub_ref, b, True), preferred_element_type=F32).astype(BF16)
                 for b in range(nb)], axis=0)
        lhs.append(u)
        bu = jnp.dot(u, b_ref[d], preferred_element_type=F32)
        for b in range(nb):
            r0 = (d * nb + b) * pitch
            for k in range(2 * nk):
                buf_ref[k, r0:r0 + tp, :] = bu[b * tp:(b + 1) * tp, k * LANE:(k + 1) * LANE]
    ar = [lre_ref[:, k * LANE:(k + 1) * LANE] for k in range(nk)]
    ai = [lim_ref[:, k * LANE:(k + 1) * LANE] for k in range(nk)]

    def step(s, carry):
        rows = pl.ds(s, q, stride=pitch)
        out = []
        for k in range(nk):
            hr, hi = carry[k]
            nr = ar[k] * hr - ai[k] * hi + buf_ref[k, rows, :]
            ni = ar[k] * hi + ai[k] * hr + buf_ref[nk + k, rows, :]
            buf_ref[k, rows, :] = hr
            buf_ref[nk + k, rows, :] = hi
            out.append((nr, ni))
        return tuple(out)

    init = tuple((h_ref[0, :, k * LANE:(k + 1) * LANE], h_ref[1, :, k * LANE:(k + 1) * LANE]) for k in range(nk))
    fin = lax.fori_loop(0, tp, step, init, unroll=8)
    for k in range(nk):
        h_ref[0, :, k * LANE:(k + 1) * LANE] = fin[k][0]
        h_ref[1, :, k * LANE:(k + 1) * LANE] = fin[k][1]

    if need_y:
        for d in range(2):
            h = jnp.concatenate(
                [jnp.concatenate([buf_ref[k, (d * nb + b) * pitch:(d * nb + b) * pitch + tp, :].astype(BF16)
                                  for k in range(2 * nk)], axis=1) for b in range(nb)], axis=0)
            y = (jnp.dot(h, c_ref[d], preferred_element_type=F32)
                 + jnp.dot(lhs[d], d_ref[d], preferred_element_type=F32))
            for b in range(nb):
                yb = y[b * tp:(b + 1) * tp]
                if d == 1:
                    yb = jnp.dot(rev, yb.astype(BF16), preferred_element_type=F32)
                first, second = (0, 1) if d == 0 else (1, 0)
                il_ref[pl.ds(first, tp, stride=2), :] = yb[:, 0:LANE]
                il_ref[pl.ds(second, tp, stride=2), :] = yb[:, LANE:2 * LANE]
                (yf_ref if d == 0 else yb_ref)[b] = il_ref[...].astype(BF16)


def _s5_call(u, bmat, cmat, dmat, lre, lim, h0, need_y, w5=None):
    nb, l, _ = u.shape
    w5 = u.shape[2] if w5 is None else w5
    nj = w5 // LANE
    q = 2 * nb
    t = min(S5_CHUNK, l)
    tp = t // 2
    p8 = lre.shape[-1]
    nc = l // t
    hspec = pl.BlockSpec((None, 2, q, p8), lambda j, c: (j, 0, 0, 0))
    ublk = (nb, t, LANE)
    in_specs = [pl.BlockSpec(ublk, lambda j, c: (0, c, j)), pl.BlockSpec(ublk, lambda j, c: (0, nc - 1 - c, j)),
                pl.BlockSpec((None, 2, 2 * LANE, 2 * p8), lambda j, c: (j, 0, 0, 0)),
                pl.BlockSpec((None, 2, 2 * p8, 2 * LANE), lambda j, c: (j, 0, 0, 0)),
                pl.BlockSpec((None, 2, 2 * LANE, 2 * LANE), lambda j, c: (j, 0, 0, 0)),
                pl.BlockSpec((None, q, p8), lambda j, c: (j, 0, 0)),
                pl.BlockSpec((None, q, p8), lambda j, c: (j, 0, 0)),
                hspec]
    out_specs = [hspec]
    out_shape = [jax.ShapeDtypeStruct((nj, 2, q, p8), F32)]
    scratch = [pltpu.VMEM((2 * p8 // LANE, q * (tp + S5_ROW_PAD), LANE), F32), pltpu.VMEM((t, LANE), F32)]
    if need_y:
        out_specs = [pl.BlockSpec(ublk, lambda j, c: (0, c, j)),
                     pl.BlockSpec(ublk, lambda j, c: (0, nc - 1 - c, j))] + out_specs
        out_shape = [jax.ShapeDtypeStruct((nb, l, w5), BF16)] * 2 + out_shape
    kern = functools.partial(_s5_kernel, nb=nb, tp=tp, p8=p8, need_y=need_y)
    return pl.pallas_call(
        kern, grid=(nj, nc), in_specs=in_specs, out_specs=out_specs, out_shape=out_shape,
        scratch_shapes=scratch, compiler_params=_cp(("parallel", "arbitrary"), 48),
        name="s5_scan" if need_y else "s5_scan_ctx")(u, u, bmat, cmat, dmat, lre, lim, h0)


def _s5_params(lam_re, lam_im, log_dt, b_re, b_im, c_re, c_im, nb):
    _, g, p = lam_re.shape
    s = b_re.shape[-1]
    gpb = LANE // s
    nj = g // gpb
    lam = lax.complex(lam_re.astype(F32), lam_im.astype(F32))
    lam_bar = jnp.exp(lam * jnp.exp(log_dt.astype(F32))[..., None])
    b_bar = ((lam_bar - 1.0) / lam)[..., None] * lax.complex(b_re.astype(F32), b_im.astype(F32))
    def quadrants(parts):
        r, c = parts[0].shape[-2:]
        same_group = (jnp.arange(gpb * r)[:, None] // r) == (jnp.arange(gpb * c)[None, :] // c)

        def spread(x):
            return jnp.where(same_group, jnp.tile(x.reshape(2, nj, gpb * r, c), (1, 1, 1, gpb)), 0.0)

        top = jnp.concatenate([spread(parts[0]), spread(parts[1])], axis=-1)
        bot = jnp.concatenate([spread(parts[2]), spread(parts[3])], axis=-1)
        return jnp.concatenate([top, bot], axis=-2).transpose(1, 0, 2, 3).astype(BF16)

    tb = lambda z: jnp.swapaxes(z, -1, -2)
    lb = lam_bar[..., None] * b_bar
    bmat = quadrants([tb(lb.real), tb(lb.imag), tb(b_bar.real), tb(b_bar.imag)])
    cc = lax.complex(c_re.astype(F32), c_im.astype(F32))
    c_l1 = cc * lam_bar[:, :, None, :]
    c_l2 = c_l1 * lam_bar[:, :, None, :]
    cmat = quadrants([tb(c_l1.real), tb(c_l2.real), -tb(c_l1.imag), -tb(c_l2.imag)])
    m0 = jnp.einsum('dgsp,dgpt->dgts', cc, b_bar).real
    m1 = jnp.einsum('dgsp,dgpt->dgts', c_l1, b_bar).real
    dmat = quadrants([m0, m1, jnp.zeros_like(m0), m0])

    def lam_of(part):
        v = part.reshape(2, nj, gpb * p).transpose(1, 0, 2)
        return jnp.repeat(v, nb, axis=1)

    lam2 = lam_bar * lam_bar
    return bmat, cmat, dmat, lam_of(lam2.real), lam_of(lam2.imag)


def _conv_silu(x, w, bias):
    l = x.shape[0]
    rows = lax.broadcasted_iota(jnp.int32, x.shape, 0)
    half = SSD_CONV // 2
    acc = x * w[half:half + 1, :] + bias
    for k in range(SSD_CONV):
        if k == half:
            continue
        off = k - half
        xs = pltpu.roll(x, shift=(-off) % l, axis=0)
        valid = jnp.logical_and(rows + off >= 0, rows + off < l)
        acc = acc + jnp.where(valid, xs, 0.0) * w[k:k + 1, :]
    return _silu(acc)


def _conv_kernel(x_ref, w_ref, b_ref, o_ref):
    o_ref[...] = _conv_silu(x_ref[...].astype(F32), w_ref[...], b_ref[...]).astype(o_ref.dtype)


def _proj_conv_kernel(a_ref, w_ref, cw_ref, cb_ref, o_ref):
    a = a_ref[...]
    wb = w_ref[...].astype(BF16)
    hw = wb.shape[1] // 2
    for c in range(2):
        cols = slice(c * hw, (c + 1) * hw)
        p = jnp.dot(a, wb[:, cols], preferred_element_type=F32)
        o_ref[:, cols] = _conv_silu(p, cw_ref[:, cols], cb_ref[:, cols]).astype(o_ref.dtype)


def _proj_conv_call(hn, w_all, col0, cw, cb):
    b, l, d = hn.shape
    c = cw.shape[1]
    tn = 512
    return pl.pallas_call(
        _proj_conv_kernel, grid=(b, c // tn),
        in_specs=[pl.BlockSpec((None, l, d), lambda i, j: (i, 0, 0), pipeline_mode=pl.Buffered(1)),
                  pl.BlockSpec((pl.Element(d), pl.Element(tn)),
                               lambda i, j: (0, (col0 // LANE + j * (tn // LANE)) * LANE)),
                  pl.BlockSpec((SSD_CONV, tn), lambda i, j: (0, j)),
                  pl.BlockSpec((1, tn), lambda i, j: (0, j))],
        out_specs=pl.BlockSpec((None, l, tn), lambda i, j: (i, 0, j)),
        out_shape=jax.ShapeDtypeStruct((b, l, c), BF16),
        compiler_params=_cp(("parallel", "arbitrary"), 60), name="in_xbc_conv")(hn, w_all, cw, cb.reshape(1, c))


def _conv_call(proj, w, bias, col0):
    b, l, _ = proj.shape
    c = w.shape[1]
    tc = 256
    off = col0 // tc
    return pl.pallas_call(
        _conv_kernel, grid=(b, c // tc),
        in_specs=[pl.BlockSpec((None, l, tc), lambda i, j: (i, 0, j + off)),
                  pl.BlockSpec((SSD_CONV, tc), lambda i, j: (0, j)),
                  pl.BlockSpec((1, tc), lambda i, j: (0, j))],
        out_specs=pl.BlockSpec((None, l, tc), lambda i, j: (i, 0, j)),
        out_shape=jax.ShapeDtypeStruct((b, l, c), BF16),
        compiler_params=_cp(("parallel", "parallel"), 40), name="ssd_conv")(proj, w, bias.reshape(1, c))


def _softplus(x):
    return jnp.maximum(x, 0.0) + jnp.log1p(jnp.exp(-jnp.abs(x)))


def _ssd_dir(xbc_ref, dt_ref, y_ref, h_ref, bias, a_neg, d, *, nh, ng, need_y):
    qn = SSD_CHUNK
    hd = SSD_HEAD_DIM
    w = nh * hd
    gw = w // ng
    gn = ng * SSD_STATE
    ii = lax.broadcasted_iota(jnp.int32, (qn, qn), 0)
    jj = lax.broadcasted_iota(jnp.int32, (qn, qn), 1)
    mask = (jj <= ii) if d == 0 else (jj >= ii)
    lmat = mask.astype(F32)
    dtv = _softplus(dt_ref[...] + bias)
    cum = jnp.dot(lmat, dtv * a_neg, precision=lax.Precision.HIGHEST, preferred_element_type=F32)
    cum_t = cum.T
    edge = qn - 1 if d == 0 else 0
    tot = cum[edge:edge + 1, :]
    dt_t = dtv.T
    wt_t = dt_t * jnp.exp(cum_t[:, edge:edge + 1] - cum_t)
    decay = jnp.exp(tot)
    lane = lax.broadcasted_iota(jnp.int32, (qn, LANE), 1)
    left = lane < hd
    zero = jnp.zeros((), BF16)
    for g in range(ng):
        bg = xbc_ref[:, w + g * SSD_STATE:w + (g + 1) * SSD_STATE]
        cg = xbc_ref[:, w + gn + g * SSD_STATE:w + gn + (g + 1) * SSD_STATE]
        bg_t = bg.astype(F32).T
        s_in = h_ref[d, :, g * gw:(g + 1) * gw]
        if need_y:
            cb = lax.dot_general(cg, bg, (((1,), (1,)), ((), ())), preferred_element_type=F32)
            yoff = jnp.dot(cg, s_in.astype(BF16), preferred_element_type=F32)
        for pr in range(gw // LANE):
            c0 = d * nh + (g * gw) // hd + 2 * pr
            col = g * gw + pr * LANE
            xp = xbc_ref[:, col:col + LANE]
            r = jnp.concatenate([jnp.where(left, xp, zero), jnp.where(left, zero, xp)], axis=0)
            tops, bots, cols = [], [], []
            for c in (c0, c0 + 1):
                bots.append((bg_t * wt_t[c:c + 1, :]).astype(BF16))
                if need_y:
                    ccol = jnp.broadcast_to(cum[:, c:c + 1], (qn, qn))
                    seg = jnp.where(mask, jnp.exp(ccol - cum_t[c:c + 1, :]), 0.0)
                    tops.append((cb * seg * dt_t[c:c + 1, :]).astype(BF16))
                    cols.append(ccol)
            dec = jnp.where(left[0:1], decay[:, c0:c0 + 1], decay[:, c0 + 1:c0 + 2])
            s_old = s_in[:, pr * LANE:(pr + 1) * LANE]
            if need_y:
                lhs = jnp.concatenate([jnp.concatenate(tops, axis=1), jnp.concatenate(bots, axis=1)], axis=0)
                out = jnp.dot(lhs, r, preferred_element_type=F32)
                ec = jnp.exp(jnp.where(left, cols[0], cols[1]))
                y_ref[:, col:col + LANE] = (out[:qn] + yoff[:, pr * LANE:(pr + 1) * LANE] * ec).astype(y_ref.dtype)
                s_new = out[qn:]
            else:
                s_new = jnp.dot(jnp.concatenate(bots, axis=1), r, preferred_element_type=F32)
            h_ref[d, :, col:col + LANE] = s_old * dec + s_new


def _ssd_kernel(*refs, nh, ng, need_y):
    if need_y:
        xf_ref, xb_ref, dtf_ref, dtb_ref, h0_ref, bias_ref, alog_ref, yf_ref, yb_ref, h_ref = refs
    else:
        xf_ref, xb_ref, dtf_ref, dtb_ref, h0_ref, bias_ref, alog_ref, h_ref = refs
        yf_ref = yb_ref = None

    @pl.when(pl.program_id(1) == 0)
    def _():
        h_ref[...] = h0_ref[...]

    bias = bias_ref[...]
    a_neg = -jnp.exp(alog_ref[...])
    _ssd_dir(xf_ref, dtf_ref, yf_ref, h_ref, bias, a_neg, 0, nh=nh, ng=ng, need_y=need_y)
    _ssd_dir(xb_ref, dtb_ref, yb_ref, h_ref, bias, a_neg, 1, nh=nh, ng=ng, need_y=need_y)


def _ssd_call(xbc, dt, h0, bias, alog, nh, ng, need_y):
    b, l, cd = xbc.shape
    w = nh * SSD_HEAD_DIM
    nc = l // SSD_CHUNK
    q = SSD_CHUNK
    fwd = lambda i, s: (i, s, 0)
    bwd = lambda i, s: (i, nc - 1 - s, 0)
    hspec = pl.BlockSpec((None, 2, SSD_STATE, w), lambda i, s: (i, 0, 0, 0))
    in_specs = [pl.BlockSpec((None, q, cd), fwd), pl.BlockSpec((None, q, cd), bwd),
                pl.BlockSpec((None, q, LANE), fwd), pl.BlockSpec((None, q, LANE), bwd),
                hspec,
                pl.BlockSpec((1, LANE), lambda i, s: (0, 0)), pl.BlockSpec((1, LANE), lambda i, s: (0, 0))]
    out_specs = [hspec]
    out_shape = [jax.ShapeDtypeStruct((b, 2, SSD_STATE, w), F32)]
    if need_y:
        out_specs = [pl.BlockSpec((None, q, w), fwd), pl.BlockSpec((None, q, w), bwd)] + out_specs
        out_shape = [jax.ShapeDtypeStruct((b, l, w), BF16)] * 2 + out_shape
    return pl.pallas_call(
        functools.partial(_ssd_kernel, nh=nh, ng=ng, need_y=need_y), grid=(b, nc),
        in_specs=in_specs, out_specs=out_specs, out_shape=out_shape,
        compiler_params=_cp(("parallel", "arbitrary"), 48),
        name="ssd_scan" if need_y else "ssd_scan_ctx")(xbc, xbc, dt, dt, h0, bias, alog)


def _gnorm_kernel(yf_ref, yb_ref, xs_ref, z_ref, d_ref, g_ref, o_ref, t_ref, *, rows, tb, gw):
    y = d_ref[...] * xs_ref[...].astype(F32) + yf_ref[...].astype(F32) + yb_ref[...].astype(F32)
    y = y * _silu(z_ref[...].astype(F32))
    bw = y.shape[1]
    parts = []
    for g in range(bw // gw):
        yg = y[:, g * gw:(g + 1) * gw]
        parts.append(yg * lax.rsqrt(jnp.mean(yg * yg, axis=-1, keepdims=True) + EPS))
    res = jnp.concatenate(parts, axis=1) * g_ref[...]
    pitch = rows + XPOSE_PAD
    for wl in range(tb):
        for kk in range(bw // LANE):
            t_ref[kk, wl * pitch:wl * pitch + rows, :] = res[wl * rows:(wl + 1) * rows, kk * LANE:(kk + 1) * LANE]
    for r in range(rows):
        o_ref[r] = jnp.concatenate(
            [t_ref[kk, pl.ds(r, tb, stride=pitch), :] for kk in range(bw // LANE)], axis=1).astype(o_ref.dtype)


def _gnorm_call(yf, yb, xbc, proj, dvec, gvec, ng):
    b, l, w = yf.shape
    rows = l // GRID_W
    tb = 2 * XPOSE_TILE
    gw = w // ng
    bw = gw * (2 if ng % 2 == 0 else 1)
    blk = pl.BlockSpec((None, tb * rows, bw), lambda i, s, g: (i, s, g))
    vec = pl.BlockSpec((1, bw), lambda i, s, g: (0, g))
    out = pl.pallas_call(
        functools.partial(_gnorm_kernel, rows=rows, tb=tb, gw=gw), grid=(b, GRID_W // tb, w // bw),
        in_specs=[blk, blk, blk, blk, vec, vec],
        out_specs=pl.BlockSpec((None, rows, tb, bw), lambda i, s, g: (i, 0, s, g)),
        out_shape=jax.ShapeDtypeStruct((b, rows, GRID_W, w), BF16),
        scratch_shapes=[pltpu.VMEM((bw // LANE, tb * (rows + XPOSE_PAD), LANE), F32)],
        compiler_params=_cp(("parallel", "parallel", "parallel"), 40),
        name="ssd_gnorm")(yf, yb, xbc, proj, dvec, gvec)
    return out.reshape(b * l, w)


def _moe_kernel(e_ref, b0_ref, nb_ref, *refs, nsub, nh):
    xs = refs[:nsub]
    wg_ref, wu_ref, wd_ref, o_ref, act_ref, wdb_ref = refs[nsub:]
    s = pl.program_id(0)
    t = pl.program_id(1)
    nb = nb_ref[s]

    @pl.when(jnp.logical_and(t < nh, nb > 0))
    def _():
        wg = wg_ref[...].astype(BF16)
        wu = wu_ref[...].astype(BF16)

        def act_block(k):
            x = xs[k][...]
            gt = jnp.dot(x, wg, preferred_element_type=F32)
            up = jnp.dot(x, wu, preferred_element_type=F32)
            act_ref[k, t] = (_silu(gt) * up).astype(BF16)

        act_block(0)
        for k in range(1, nsub):
            pl.when(k < nb)(functools.partial(act_block, k))

    @pl.when(jnp.logical_and(t >= nh, nb > 0))
    def _():
        wdb = wd_ref[...].astype(BF16)
        wdb_ref[...] = wdb

        def down_block(k, w):
            a = jnp.concatenate([act_ref[k, h] for h in range(nh)], axis=1)
            o_ref[k * MOE_BLOCK:(k + 1) * MOE_BLOCK, :] = jnp.dot(a, w, preferred_element_type=F32).astype(o_ref.dtype)

        down_block(0, wdb)
        for k in range(1, nsub):
            pl.when(k < nb)(lambda k=k: down_block(k, wdb_ref[...]))

            @pl.when(k >= nb)
            def _():
                o_ref[k * MOE_BLOCK:(k + 1) * MOE_BLOCK, :] = jnp.zeros((MOE_BLOCK, o_ref.shape[1]), o_ref.dtype)

    @pl.when(jnp.logical_and(t >= nh, nb == 0))
    def _():
        o_ref[...] = jnp.zeros(o_ref.shape, o_ref.dtype)


def _moe_call(sup_e, sup_b0, sup_nb, xs, w_gate, w_up, w_down):
    n_slots, d = xs.shape
    hid = w_gate.shape[-1]
    th = min(256, hid)
    tn = min(1024, d)
    nh = hid // th
    nt = d // tn
    ns = sup_e.shape[0]
    rsup = MOE_SUB * MOE_BLOCK

    def x_map(k):
        def index(s, t, e, b0, nb):
            sx = jnp.where(t < nh, s, jnp.minimum(s + 1, ns - 1))
            return (b0[sx * MOE_SUB + k], 0)
        return index

    def hid_idx(s, t, nb):
        return jnp.where(nb[s] > 0, jnp.minimum(t, nh - 1), nh - 1)

    def col_idx(s, t, nb):
        return jnp.where(nb[s] > 0, jnp.clip(t - nh, 0, nt - 1), nt - 1)

    gs = pltpu.PrefetchScalarGridSpec(
        num_scalar_prefetch=3, grid=(ns, nh + nt),
        in_specs=[pl.BlockSpec((MOE_BLOCK, d), x_map(k)) for k in range(MOE_SUB)] + [
            pl.BlockSpec((None, d, th), lambda s, t, e, b0, nb: (e[s], 0, hid_idx(s, t, nb))),
            pl.BlockSpec((None, d, th), lambda s, t, e, b0, nb: (e[s], 0, hid_idx(s, t, nb))),
            pl.BlockSpec((None, hid, tn), lambda s, t, e, b0, nb: (e[s], 0, col_idx(s, t, nb)))],
        out_specs=pl.BlockSpec((rsup, tn), lambda s, t, e, b0, nb: (s, jnp.clip(t - nh, 0, nt - 1))),
        scratch_shapes=[pltpu.VMEM((MOE_SUB, nh, MOE_BLOCK, th), BF16), pltpu.VMEM((hid, tn), BF16)])
    return pl.pallas_call(
        functools.partial(_moe_kernel, nsub=MOE_SUB, nh=nh), grid_spec=gs,
        out_shape=jax.ShapeDtypeStruct((ns * rsup, d), BF16),
        compiler_params=_cp(("arbitrary", "arbitrary"), 60), name="moe_experts")(
            sup_e, sup_b0, sup_nb, *([xs] * MOE_SUB), w_gate, w_up, w_down)


def _final_kernel(x_ref, ya_ref, yb_ref, w_ref, g2_ref, fg_ref, o_ref):
    wts = w_ref[...]
    moe = ya_ref[...].astype(F32) * wts[:, 0:1] + yb_ref[...].astype(F32) * wts[:, 1:2]
    x = x_ref[...] + g2_ref[...] * moe
    ms = jnp.mean(x * x, axis=-1, keepdims=True)
    o_ref[...] = x * lax.rsqrt(ms + EPS) * fg_ref[...]


def _final_call(x, y2, wts, mods, ig, fg):
    b, l, d = x.shape
    tr = min(256, l)
    nt = l // tr
    row = lambda i, r: (i * nt + r, 0)
    row2 = lambda i, r: (b * nt + i * nt + r, 0)
    ya = yb = y2
    return pl.pallas_call(
        _final_kernel, grid=(b, nt),
        in_specs=[pl.BlockSpec((None, tr, d), lambda i, r: (i, r, 0)),
                  pl.BlockSpec((tr, d), row), pl.BlockSpec((tr, d), row2),
                  pl.BlockSpec((tr, MOE_TOP_K), row),
                  pl.BlockSpec((None, None, 1, d), lambda i, r: (i, ig, 0, 0)),
                  pl.BlockSpec((1, d), lambda i, r: (0, 0))],
        out_specs=pl.BlockSpec((None, tr, d), lambda i, r: (i, r, 0)),
        out_shape=jax.ShapeDtypeStruct((b, l, d), F32),
        compiler_params=_cp(("parallel", "parallel"), 40), name="combine_final")(x, ya, yb, wts, mods, fg)


def _route(sel, ne):
    m = sel.shape[0]
    weights = sel[:, MOE_TOP_K:2 * MOE_TOP_K]
    n_assign = m * MOE_TOP_K
    flat_e = jnp.concatenate([sel[:, k] for k in range(MOE_TOP_K)], axis=0).astype(jnp.int32)
    onehot = (flat_e[:, None] == jnp.arange(ne, dtype=flat_e.dtype)[None, :]).astype(jnp.int32)
    csum = jnp.cumsum(onehot, axis=0)
    counts = csum[-1]
    local = jnp.sum(csum * onehot, axis=1) - 1
    nblk_e = (counts + MOE_BLOCK - 1) // MOE_BLOCK
    padded = nblk_e * MOE_BLOCK
    pad_end = jnp.cumsum(padded)
    pad_start = pad_end - padded
    dest = pad_start[flat_e] + local
    n_blocks = -(-(n_assign + ne * (MOE_BLOCK - 1)) // MOE_BLOCK)
    tok = jnp.arange(n_assign, dtype=jnp.int32) % m
    n_slots = n_blocks * MOE_BLOCK
    slot_tok = (jnp.arange(n_slots, dtype=jnp.int32) % m).at[dest].set(tok)
    rsup = MOE_SUB * MOE_BLOCK
    ns_e = (nblk_e + MOE_SUB - 1) // MOE_SUB
    sup_end = jnp.cumsum(ns_e)
    sup_start = sup_end - ns_e
    n_sup = (n_blocks + (MOE_SUB - 1) * ne) // MOE_SUB
    sidx = jnp.arange(n_sup, dtype=jnp.int32)
    last = sup_end[-1] - 1
    s_eff = jnp.minimum(sidx, last)
    e_s = jnp.minimum(jnp.searchsorted(sup_end, s_eff, side='right'), ne - 1).astype(jnp.int32)
    k_s = s_eff - sup_start[e_s]
    b0_s = pad_start[e_s] // MOE_BLOCK + MOE_SUB * k_s
    nb_s = jnp.clip(nblk_e[e_s] - MOE_SUB * k_s, 0, MOE_SUB)
    used = sidx <= last
    sup_nb = jnp.where(used, nb_s, 0).astype(jnp.int32)
    kk = jnp.arange(MOE_SUB, dtype=jnp.int32)[None, :]
    sup_b0 = jnp.maximum(lax.cummax(jnp.where(kk < sup_nb[:, None], b0_s[:, None] + kk, -1), axis=0), 0)
    sup_b0 = sup_b0.reshape(-1).astype(jnp.int32)
    pos = ((sup_start[flat_e] + local // rsup) * rsup + local % rsup).astype(jnp.int32)
    return weights, slot_tok, e_s, sup_b0, sup_nb, pos


def kernel(x, c, ctx, c_ctx, w_mod, b_mod, norm1_g, w_in, s5_lam_re, s5_lam_im, s5_log_dt, s5_b_re, s5_b_im, s5_c_re, s5_c_im, s5_d, s5_w_val, s5_w_gate, ssd_conv_w, ssd_conv_b, ssd_a_log, ssd_dt_bias, ssd_d, ssd_norm_g, ssd_w_out, w_o, norm2_g, moe_w_group, moe_b_group, moe_w_expert, moe_b_expert, moe_w_gate, moe_w_up, moe_w_down, final_g):
    depth = w_mod.shape[0]
    assert depth == 1, "single-layer block"
    bsz, n_lat, d = x.shape
    l_ctx = ctx.shape[1]
    w5 = s5_d.shape[1]
    nh = ssd_d.shape[1]
    w = nh * SSD_HEAD_DIM
    conv_dim = ssd_conv_w.shape[2]
    ng = (conv_dim - w) // (2 * SSD_STATE)
    ssd_in = w + conv_dim + 2 * nh
    o1, o2 = w5, w5 + ssd_in
    l = 0

    cc = jnp.concatenate([c, c_ctx[None, :]], axis=0)
    cc = jnp.pad(cc, ((0, (-cc.shape[0]) % 8), (0, 0)))
    mods = _mod_call(cc, w_mod[l], b_mod[l]).reshape(cc.shape[0], 6, 1, d)
    i_sh1, i_sc1, i_g1, i_sh2, i_sc2, i_g2 = range(6)

    w_in_l = w_in[l]
    o_dt = o1 + w + conv_dim

    hn_rm, hn_cm = _norm_lat_call(x, norm1_g[l], mods, i_sc1, i_sh1)
    hc = _norm_ctx_call(ctx, norm1_g[l], mods, bsz, i_sc1, i_sh1)
    hn_rm = hn_rm.reshape(bsz * n_lat, d)
    hn_cm = hn_cm.reshape(bsz * n_lat, d)
    hc = hc.reshape(bsz * l_ctx, d)

    tn_g = _mm_tiles(bsz * n_lat, 2 * d)[1]
    if w5 % tn_g == 0:
        gates = _mm2_call(hn_rm, w_in_l, (0, o1), (o2, 2 * d), "in_s5_gates")
        u_lat, goff = gates.reshape(bsz, n_lat, w5 + 2 * d), w5 // tn_g
    else:
        u_lat = _mm_call(hn_rm, w_in_l, BF16, "in_s5", 0, o1, tm_max=2048).reshape(bsz, n_lat, w5)
        gates, goff = _mm_call(hn_rm, w_in_l, BF16, "in_gates", o2, 2 * d, tm_max=2048), 0
    u_ctx = _mm_call(hc, w_in_l, BF16, "in_s5_ctx", 0, o1).reshape(bsz, l_ctx, w5)
    z_lat = _mm_call(hn_cm, w_in_l, BF16, "in_z", o1, w, tm_max=2048).reshape(bsz, n_lat, w)
    xbc_lat = _proj_conv_call(hn_cm.reshape(bsz, n_lat, d), w_in_l, o1 + w, ssd_conv_w[l], ssd_conv_b[l])
    p_ctx = _mm_call(hc, w_in_l, BF16, "in_xbc_ctx", o1 + w, conv_dim).reshape(bsz, l_ctx, conv_dim)
    dt_lat = _mm_call(hn_cm, w_in_l, F32, "in_dt", o_dt, 2 * nh).reshape(bsz, n_lat, LANE)
    dt_ctx = _mm_call(hc, w_in_l, F32, "in_dt_ctx", o_dt, 2 * nh).reshape(bsz, l_ctx, LANE)

    nj = w5 // LANE
    bmat, cmat, dmat, lre, lim = _s5_params(s5_lam_re[l], s5_lam_im[l], s5_log_dt[l], s5_b_re[l], s5_b_im[l],
                                            s5_c_re[l], s5_c_im[l], bsz)
    s5_zero = jnp.zeros((nj, 2, 2 * bsz, lre.shape[-1]), F32)
    (s5_ctx,) = _s5_call(u_ctx, bmat, cmat, dmat, lre, lim, s5_zero, False)
    ya_f, ya_b, _ = _s5_call(u_lat, bmat, cmat, dmat, lre, lim, s5_ctx, True, w5=w5)

    xbc_ctx = _conv_call(p_ctx, ssd_conv_w[l], ssd_conv_b[l], 0)
    pad_h = LANE - 2 * nh
    bias = jnp.pad(ssd_dt_bias[l].astype(F32).reshape(1, 2 * nh), ((0, 0), (0, pad_h)))
    alog = jnp.pad(ssd_a_log[l].astype(F32).reshape(1, 2 * nh), ((0, 0), (0, pad_h)))
    h_zero = jnp.zeros((bsz, 2, SSD_STATE, w), F32)
    (h_ctx,) = _ssd_call(xbc_ctx, dt_ctx, h_zero, bias, alog, nh, ng, False)
    y_f, y_b, _ = _ssd_call(xbc_lat, dt_lat, h_ctx, bias, alog, nh, ng, True)
    d_vec = jnp.repeat(ssd_d[l].astype(F32), SSD_HEAD_DIM).reshape(1, w)
    y_ssd = _gnorm_call(y_f, y_b, xbc_lat, z_lat, d_vec, ssd_norm_g[l].astype(F32).reshape(1, w), ng)

    m_lat = bsz * n_lat
    part_a = _glu_call(u_lat.reshape(m_lat, -1), ya_f.reshape(m_lat, w5), ya_b.reshape(m_lat, w5),
                       s5_d[l].astype(F32).reshape(1, w5),
                       s5_w_val[l].astype(BF16), s5_w_gate[l].astype(BF16), gates, goff)
    merged = _merge_call(y_ssd, ssd_w_out[l], gates, part_a, goff)
    x1 = _resid_call(merged, w_o[l], x, mods, i_g1)

    ngr = moe_w_group.shape[-1]
    ne = moe_w_expert.shape[-1]
    wr = jnp.concatenate([moe_w_group[l], moe_w_expert[l]], axis=1).astype(F32)
    wr = jnp.pad(wr, ((0, 0), (0, (-(ngr + ne)) % LANE)))
    br = jnp.concatenate([moe_b_group[l], moe_b_expert[l]]).astype(F32)
    br = jnp.pad(br, (0, (-(ngr + ne)) % LANE)).reshape(1, -1)
    hx, sel = _norm_router_call(x1, norm2_g[l], mods, i_sc2, i_sh2, wr, br, ngr, ne)
    m = bsz * n_lat
    hx = hx.reshape(m, d)
    weights, slot_tok, sup_e, sup_b0, sup_nb, pos = _route(sel.reshape(m, -1), ne)
    xs = hx[slot_tok]
    y_slots = _moe_call(sup_e, sup_b0, sup_nb, xs, moe_w_gate[l], moe_w_up[l], moe_w_down[l])
    y2 = y_slots[pos]
    return _final_call(x1, y2, weights.astype(F32), mods, i_g2, final_g.reshape(1, d))
```
